```python
import jax, jax.numpy as jnp
from jax import lax
import numpy as np

D_MODEL = 1024
BATCH = 16
SEQ = 256
DEPTH = 2
DEC_BATCH = 4
DEC_SEQ = 4096
PAST_LEN = 512

GRID_W = 64
N_EVEN = (DEPTH + 1) // 2
N_ODD = DEPTH // 2
EPS = 1e-6
A_H = 4
A_DK = 128
A_DV = 256
A_CHUNK = 128
B_H = 8
B_HKV = 2
B_HD = 64
Q_BLOCK = 128
ROPE_THETA = 10000.0
C_H = 4
C_DK = 128
C_DV = 256
C_RANK = 16
C_TAU = 16.0
C_CHUNK = 64
FF_DENSE = 2816
N_EXPERTS = 8
TOP_K = 2
FF_EXPERT = 2816

EVEN_SPLITS = (A_H * A_DK, A_H * A_DK, A_H * A_DV, A_H * A_DV, B_H * B_HD, B_HKV * B_HD, B_HKV * B_HD)
EVEN_IN = A_H * (2 * A_DK + 2 * A_DV) + (B_H + 2 * B_HKV) * B_HD
EVEN_OUT = A_H * A_DV + B_H * B_HD
ODD_SPLITS = (C_H * C_DK, C_H * C_DK, C_H * C_DV, C_H * C_DV, C_RANK, C_RANK)
ODD_IN = C_H * (2 * C_DK + 2 * C_DV) + 2 * C_RANK
ODD_OUT = C_H * C_DV

kernel_name = "hybrid_prefix_diffusion_retention_gqa_gla_moe"


def _split(z, sizes):
    idx = np.cumsum(np.array(sizes))[:-1].tolist()
    return jnp.split(z, idx, axis=-1)


def _flip(t):
    return jnp.flip(t, axis=1)


def rmsnorm(x, g):
    xf = x.astype(jnp.float32)
    y = xf * lax.rsqrt(jnp.mean(xf * xf, axis=-1, keepdims=True) + EPS)
    return (y * g.astype(jnp.float32)).astype(x.dtype)


def headnorm(o, g):
    of = o.astype(jnp.float32)
    d = of - jnp.mean(of, axis=-1, keepdims=True)
    return d * lax.rsqrt(jnp.mean(d * d, axis=-1, keepdims=True) + EPS) * g.astype(jnp.float32)


def axial_rope(T, hd):
    rows = T // GRID_W
    row = jnp.repeat(jnp.arange(rows, dtype=jnp.float32), GRID_W)
    col = jnp.tile(jnp.arange(GRID_W, dtype=jnp.float32), rows)
    n = hd // 4
    inv = ROPE_THETA ** (-jnp.arange(n, dtype=jnp.float32) / n)
    ang = jnp.concatenate([row[:, None] * inv, col[:, None] * inv], axis=-1)
    return jnp.cos(ang), jnp.sin(ang)


def apply_rope(x, cos, sin):
    xr = x.astype(jnp.float32).reshape(x.shape[:-1] + (x.shape[-1] // 2, 2))
    x1, x2 = xr[..., 0], xr[..., 1]
    c = cos[None, :, None, :]
    s = sin[None, :, None, :]
    out = jnp.stack([x1 * c - x2 * s, x1 * s + x2 * c], axis=-1)
    return out.reshape(x.shape).astype(x.dtype)


def block_attention(q, k, v):
    B, T, H, hd = q.shape
    hkv = k.shape[2]
    g = H // hkv
    nb = T // Q_BLOCK
    qb = q.reshape(B, nb, Q_BLOCK, hkv, g, hd).transpose(1, 0, 2, 3, 4, 5)
    scale = hd ** -0.5

    def one(qi):
        s = jnp.einsum('bqkgd,bskd->bkgqs', qi, k).astype(jnp.float32) * scale
        p = jax.nn.softmax(s, axis=-1).astype(v.dtype)
        return jnp.einsum('bkgqs,bskd->bqkgd', p, v)

    o = lax.map(one, qb)
    return o.transpose(1, 0, 2, 3, 4, 5).reshape(B, T, H * hd)


def retention_scan(q, k, v, log_gamma, s0):
    B, T, H, dk = q.shape
    dv = v.shape[-1]
    C = A_CHUNK
    n = T // C

    def chunks(t):
        return t.astype(jnp.float32).reshape(B, n, C, H, -1).transpose(1, 0, 3, 2, 4)

    qc = chunks(q) * (dk ** -0.5)
    kc = chunks(k)
    vc = chunks(v)
    lg = log_gamma.astype(jnp.float32)
    idx = jnp.arange(C, dtype=jnp.float32)
    diff = idx[:, None] - idx[None, :]
    dmask = jnp.where(diff >= 0, jnp.exp(lg[:, None, None] * jnp.maximum(diff, 0.0)), 0.0)
    q_dec = jnp.exp(lg[:, None] * (idx + 1.0))[..., None]
    k_dec = jnp.exp(lg[:, None] * (C - 1.0 - idx))[..., None]
    c_dec = jnp.exp(lg * C)[:, None, None]

    def step(S, inp):
        qi, ki, vi = inp
        att = jnp.einsum('bhid,bhjd->bhij', qi, ki) * dmask
        o = jnp.einsum('bhij,bhjv->bhiv', att, vi) + jnp.einsum('bhid,bhdv->bhiv', qi * q_dec, S)
        S = c_dec * S + jnp.einsum('bhjd,bhjv->bhdv', ki * k_dec, vi)
        return S, o

    S, o = lax.scan(step, s0.astype(jnp.float32), (qc, kc, vc))
    return o.transpose(1, 0, 3, 2, 4).reshape(B, T, H, dv), S


def gla_scan(q, k, v, log_a, s0):
    B, T, H, dk = q.shape
    dv = v.shape[-1]
    C = C_CHUNK
    n = T // C

    def chunks(t):
        return t.astype(jnp.float32).reshape(B, n, C, H, -1).transpose(1, 0, 3, 2, 4)

    qc = chunks(q) * (dk ** -0.5)
    kc = chunks(k)
    vc = chunks(v)
    ac = chunks(log_a)
    causal = jnp.tril(jnp.ones((C, C), dtype=bool))[:, :, None]

    def step(S, inp):
        qi, ki, vi, ai = inp
        b = jnp.cumsum(ai, axis=2)
        diff = b[:, :, :, None, :] - b[:, :, None, :, :]
        dec = jnp.where(causal, jnp.exp(jnp.where(causal, diff, 0.0)), 0.0)
        att = jnp.einsum('bhid,bhjd,bhijd->bhij', qi, ki, dec)
        o = jnp.einsum('bhij,bhjv->bhiv', att, vi) + jnp.einsum('bhid,bhdv->bhiv', qi * jnp.exp(b), S)
        b_last = b[:, :, -1:, :]
        S = jnp.exp(b_last[:, :, 0, :])[..., None] * S + jnp.einsum('bhjd,bhjv->bhdv', ki * jnp.exp(b_last - b), vi)
        return S, o

    S, o = lax.scan(step, s0.astype(jnp.float32), (qc, kc, vc, ac))
    return o.transpose(1, 0, 3, 2, 4).reshape(B, T, H, dv), S


def even_mixer(h, w_in, w_out, log_gamma, a_g, q_g, k_g, rope, ctx):
    B, T, _ = h.shape
    aq, ak, av, ag, bq, bk, bv = _split(h @ w_in, EVEN_SPLITS)
    aq = aq.reshape(B, T, A_H, A_DK)
    ak = ak.reshape(B, T, A_H, A_DK)
    av = av.reshape(B, T, A_H, A_DV)
    if ctx is None:
        s0f = jnp.zeros((B, A_H, A_DK, A_DV), jnp.float32)
        s0b = s0f
    else:
        s0f, s0b = ctx[2], ctx[3]
    of, sf = retention_scan(aq, ak, av, log_gamma[0], s0f)
    ob, sb = retention_scan(_flip(aq), _flip(ak), _flip(av), log_gamma[1], s0b)
    o_a = headnorm(of + _flip(ob), a_g).reshape(B, T, A_H * A_DV).astype(h.dtype) * jax.nn.silu(ag)

    bq = rmsnorm(bq.reshape(B, T, B_H, B_HD), q_g)
    bk = rmsnorm(bk.reshape(B, T, B_HKV, B_HD), k_g)
    bv = bv.reshape(B, T, B_HKV, B_HD)
    if ctx is None:
        keys, vals = bk, bv
    else:
        cos, sin = rope
        bq = apply_rope(bq, cos, sin)
        bk_lat = apply_rope(bk, cos, sin)
        keys = jnp.concatenate([ctx[0].astype(bk.dtype), bk_lat], axis=1)
        vals = jnp.concatenate([ctx[1].astype(bv.dtype), bv], axis=1)
    o_b = block_attention(bq, keys, vals)
    y = jnp.concatenate([o_a, o_b.astype(h.dtype)], axis=-1) @ w_out
    return y, (bk, bv, sf, sb)


def odd_mixer(h, w_in, w_a2, b_a, c_g, w_out, ctx):
    B, T, _ = h.shape
    q, k, v, g, lf, lb = _split(h @ w_in, ODD_SPLITS)
    q = q.reshape(B, T, C_H, C_DK)
    k = k.reshape(B, T, C_H, C_DK)
    v = v.reshape(B, T, C_H, C_DV)
    log_af = (jax.nn.log_sigmoid((lf @ w_a2[0] + b_a[0]).astype(jnp.float32)) / C_TAU).reshape(B, T, C_H, C_DK)
    log_ab = (jax.nn.log_sigmoid((lb @ w_a2[1] + b_a[1]).astype(jnp.float32)) / C_TAU).reshape(B, T, C_H, C_DK)
    if ctx is None:
        s0f = jnp.zeros((B, C_H, C_DK, C_DV), jnp.float32)
        s0b = s0f
    else:
        s0f, s0b = ctx
    of, sf = gla_scan(q, k, v, log_af, s0f)
    ob, sb = gla_scan(_flip(q), _flip(k), _flip(v), _flip(log_ab), s0b)
    o = rmsnorm(of + _flip(ob), c_g).reshape(B, T, C_H * C_DV).astype(h.dtype)
    y = (o * jax.nn.silu(g)) @ w_out
    return y, (sf, sb)


def swiglu(h, w1, w3, w2):
    return (jax.nn.silu(h @ w1) * (h @ w3)) @ w2


def moe_swiglu(h, w_router, w1, w3, w2):
    logits = (h @ w_router).astype(jnp.float32)
    top_v, top_i = lax.top_k(logits, TOP_K)
    top_w = jax.nn.softmax(top_v, axis=-1)
    gates = jnp.sum(jax.nn.one_hot(top_i, N_EXPERTS, dtype=jnp.float32) * top_w[..., None], axis=-2)
    out = jnp.zeros_like(h)
    for e in range(N_EXPERTS):
        out = out + gates[..., e:e + 1].astype(h.dtype) * swiglu(h, w1[e], w3[e], w2[e])
    return out


def setup_inputs(seed: int = 0) -> dict:
    key = jax.random.key(seed)
    ks = iter(jax.random.split(key, 48))
    f32 = jnp.float32
    D = D_MODEL

    def nrm(shape, scale):
        return jax.random.normal(next(ks), shape, f32) * scale

    inp = {}
    inp['x_prompt'] = nrm((BATCH, SEQ, D), 1.0)
    inp['x_sample'] = nrm((DEC_BATCH, DEC_SEQ, D), 1.0)
    inp['c'] = nrm((DEC_BATCH, D), 1.0)
    inp['cache_b_k'] = nrm((DEC_BATCH, N_EVEN, PAST_LEN, B_HKV, B_HD), 1.0)
    inp['cache_b_v'] = nrm((DEC_BATCH, N_EVEN, PAST_LEN, B_HKV, B_HD), 1.0)
    inp['state_a_fwd'] = nrm((DEC_BATCH, N_EVEN, A_H, A_DK, A_DV), 0.1)
    inp['state_a_bwd'] = nrm((DEC_BATCH, N_EVEN, A_H, A_DK, A_DV), 0.1)
    inp['state_c_fwd'] = nrm((DEC_BATCH, N_ODD, C_H, C_DK, C_DV), 0.1)
    inp['state_c_bwd'] = nrm((DEC_BATCH, N_ODD, C_H, C_DK, C_DV), 0.1)
    inp['c_ctx'] = nrm((D,), 1.0)
    inp['w_mod'] = nrm((DEPTH, D, 6 * D), 0.5 * D ** -0.5)
    inp['b_mod'] = nrm((DEPTH, 6 * D), 0.02)
    inp['norm1_g'] = 1.0 + nrm((DEPTH, D), 0.05)
    inp['norm2_g'] = 1.0 + nrm((DEPTH, D), 0.05)
    inp['final_g'] = 1.0 + nrm((D,), 0.05)
    inp['even_w_in'] = nrm((N_EVEN, D, EVEN_IN), D ** -0.5)
    inp['even_w_out'] = nrm((N_EVEN, EVEN_OUT, D), EVEN_OUT ** -0.5)
    base = jnp.log1p(-jnp.power(2.0, -5.0 - jnp.arange(A_H, dtype=f32)))
    inp['a_log_gamma'] = base * jnp.exp(nrm((N_EVEN, 2, A_H), 0.1))
    inp['a_norm_g'] = 1.0 + nrm((N_EVEN, A_H, A_DV), 0.05)
    inp['b_q_g'] = 1.0 + nrm((N_EVEN, B_HD), 0.05)
    inp['b_k_g'] = 1.0 + nrm((N_EVEN, B_HD), 0.05)
    inp['odd_w_in'] = nrm((N_ODD, D, ODD_IN), D ** -0.5)
    inp['c_w_a2'] = nrm((N_ODD, 2, C_RANK, C_H * C_DK), C_RANK ** -0.5)
    inp['c_b_a'] = nrm((N_ODD, 2, C_H * C_DK), 0.1)
    inp['c_norm_g'] = 1.0 + nrm((N_ODD, C_H, C_DV), 0.05)
    inp['odd_w_out'] = nrm((N_ODD, ODD_OUT, D), ODD_OUT ** -0.5)
    inp['ff_w1'] = nrm((N_EVEN, D, FF_DENSE), D ** -0.5)
    inp['ff_w3'] = nrm((N_EVEN, D, FF_DENSE), D ** -0.5)
    inp['ff_w2'] = nrm((N_EVEN, FF_DENSE, D), FF_DENSE ** -0.5)
    inp['router_w'] = nrm((N_ODD, D, N_EXPERTS), D ** -0.5)
    inp['moe_w1'] = nrm((N_ODD, N_EXPERTS, D, FF_EXPERT), D ** -0.5)
    inp['moe_w3'] = nrm((N_ODD, N_EXPERTS, D, FF_EXPERT), D ** -0.5)
    inp['moe_w2'] = nrm((N_ODD, N_EXPERTS, FF_EXPERT, D), FF_EXPERT ** -0.5)
    return inp


def reference(x_prompt, x_sample, c, cache_b_k, cache_b_v, state_a_fwd, state_a_bwd, state_c_fwd, state_c_bwd,
              c_ctx, w_mod, b_mod, norm1_g, norm2_g, final_g, even_w_in, even_w_out, a_log_gamma, a_norm_g,
              b_q_g, b_k_g, odd_w_in, c_w_a2, c_b_a, c_norm_g, odd_w_out, ff_w1, ff_w3, ff_w2,
              router_w, moe_w1, moe_w3, moe_w2):
    cond_ctx = jax.nn.silu(c_ctx)[None, :]
    cond_lat = jax.nn.silu(c)
    rope = axial_rope(x_sample.shape[1], B_HD)

    def run(x, cond, use_cache):
        kept_k, kept_v, kept_af, kept_ab, kept_cf, kept_cb = [], [], [], [], [], []
        for l in range(DEPTH):
            mod = (cond @ w_mod[l] + b_mod[l])[:, None, :]
            sh1, sc1, g1, sh2, sc2, g2 = jnp.split(mod, 6, axis=-1)
            h = rmsnorm(x, norm1_g[l]) * (1.0 + sc1) + sh1
            if l % 2 == 0:
                i = l // 2
                ctx = (cache_b_k[:, i], cache_b_v[:, i], state_a_fwd[:, i], state_a_bwd[:, i]) if use_cache else None
                y, (bk, bv, sf, sb) = even_mixer(h, even_w_in[i], even_w_out[i], a_log_gamma[i], a_norm_g[i],
                                                 b_q_g[i], b_k_g[i], rope, ctx)
                kept_k.append(bk)
                kept_v.append(bv)
                kept_af.append(sf)
                kept_ab.append(sb)
            else:
                i = l // 2
                ctx = (state_c_fwd[:, i], state_c_bwd[:, i]) if use_cache else None
                y, (sf, sb) = odd_mixer(h, odd_w_in[i], c_w_a2[i], c_b_a[i], c_norm_g[i], odd_w_out[i], ctx)
                kept_cf.append(sf)
                kept_cb.append(sb)
            x = x + g1 * y
            h = rmsnorm(x, norm2_g[l]) * (1.0 + sc2) + sh2
            if l % 2 == 0:
                f = swiglu(h, ff_w1[l // 2], ff_w3[l // 2], ff_w2[l // 2])
            else:
                f = moe_swiglu(h, router_w[l // 2], moe_w1[l // 2], moe_w3[l // 2], moe_w2[l // 2])
            x = x + g2 * f
        return rmsnorm(x, final_g), (kept_k, kept_v, kept_af, kept_ab, kept_cf, kept_cb)

    y_prompt, kept = run(x_prompt, cond_ctx, False)
    new_cache_b_k = jnp.stack(kept[0], axis=1)
    new_cache_b_v = jnp.stack(kept[1], axis=1)
    new_state_a_fwd = jnp.stack(kept[2], axis=1)
    new_state_a_bwd = jnp.stack(kept[3], axis=1)
    new_state_c_fwd = jnp.stack(kept[4], axis=1)
    new_state_c_bwd = jnp.stack(kept[5], axis=1)
    y_sample, _ = run(x_sample, cond_lat, True)
    return (y_prompt, y_sample, new_cache_b_k, new_cache_b_v, new_state_a_fwd, new_state_a_bwd,
            new_state_c_fwd, new_state_c_bwd)
```

```python
import functools

import numpy as np
import jax
import jax.numpy as jnp
from jax import lax
from jax.experimental import pallas as pl
from jax.experimental.pallas import tpu as pltpu

F32 = jnp.float32
BF16 = jnp.bfloat16
EPS = 1e-6
HIGHEST = lax.Precision.HIGHEST

VMEM_LIMIT_BYTES = 56 * 1024 * 1024

A_H, A_DK, A_DV = 4, 128, 256
B_H, B_HKV, B_HD = 8, 2, 64
C_H, C_DK, C_DV, C_RANK = 4, 128, 256, 16
C_TAU = 16.0
GRID_W = 64
ROPE_THETA = 10000.0
N_EXPERTS = 8
LANES = 128
SUB = 8
RET_CHUNK = 128
GLA_CHUNK = 128
Q_TILE = 128


def _params(*sem):
    return pltpu.CompilerParams(dimension_semantics=sem, vmem_limit_bytes=VMEM_LIMIT_BYTES)


def _dot(a, b):
    return jnp.dot(a, b, preferred_element_type=F32)


def _dot_nt(a, b):
    return lax.dot_general(a, b, (((1,), (1,)), ((), ())), preferred_element_type=F32)


def _silu(x):
    return x * jax.nn.sigmoid(x)


def _norm_mod(x, g, sh, sc):
    r = lax.rsqrt(jnp.mean(x * x, axis=-1, keepdims=True) + EPS)
    return (x * r * g) * (1.0 + sc) + sh


def _mod_kernel(c_ref, w_ref, b_ref, o_ref):
    c = c_ref[...]
    o_ref[...] = jnp.dot(_silu(c), w_ref[...], precision=HIGHEST, preferred_element_type=F32) + b_ref[...]


def _modulation(conds, w_mod, b_mod):
    L, D, D6 = w_mod.shape
    R = conds.shape[0]
    TN = 1024
    return pl.pallas_call(
        _mod_kernel,
        grid=(L, D6 // TN),
        in_specs=[pl.BlockSpec((R, D), lambda l, j: (0, 0)),
                  pl.BlockSpec((None, D, TN), lambda l, j: (l, 0, j)),
                  pl.BlockSpec((None, 1, TN), lambda l, j: (l, 0, j))],
        out_specs=pl.BlockSpec((None, R, TN), lambda l, j: (l, 0, j)),
        out_shape=jax.ShapeDtypeStruct((L, R, D6), F32),
        compiler_params=_params("arbitrary", "arbitrary"),
        name="modulation",
    )(conds, w_mod, b_mod.reshape(L, 1, D6))


def _mod_spec(part, D, TM, rows_per_group, axis):
    def idx(*g):
        return ((g[axis] * TM) // rows_per_group, part, 0, 0)
    return pl.BlockSpec((None, None, 1, D), idx)


def _norm_mm_kernel(x_ref, g_ref, sh_ref, sc_ref, w_ref, o_ref, wbf_ref):
    @pl.when(pl.program_id(1) == 0)
    def _():
        wbf_ref[...] = w_ref[...].astype(BF16)

    h = _norm_mod(x_ref[...], g_ref[...], sh_ref[...], sc_ref[...])
    o_ref[...] = _dot(h.astype(BF16), wbf_ref[...]).astype(o_ref.dtype)


def _norm_mm(x, g, mod, parts, w, TN, TM, rows_per_group):
    N, D = x.shape
    NO = w.shape[1]
    return pl.pallas_call(
        _norm_mm_kernel,
        grid=(NO // TN, N // TM),
        in_specs=[pl.BlockSpec((TM, D), lambda j, i: (i, 0)),
                  pl.BlockSpec((1, D), lambda j, i: (0, 0)),
                  _mod_spec(parts[0], D, TM, rows_per_group, 1),
                  _mod_spec(parts[1], D, TM, rows_per_group, 1),
                  pl.BlockSpec((D, TN), lambda j, i: (0, j))],
        out_specs=pl.BlockSpec((TM, TN), lambda j, i: (i, j)),
        out_shape=jax.ShapeDtypeStruct((N, NO), F32),
        scratch_shapes=[pltpu.VMEM((D, TN), BF16)],
        compiler_params=_params("arbitrary", "arbitrary"),
        name="norm_mm",
    )(x, g.reshape(1, D), mod, mod, w)


def _ret_kernel(lg_ref, q_ref, k_ref, v_ref, ag_ref, s0f_ref, s0b_ref, ng_ref,
                o_ref, sf_ref, sb_ref, s_scr, of_scr, *, n, C):
    d = pl.program_id(1)
    c = pl.program_id(2)

    @pl.when(jnp.logical_and(c == 0, d == 0))
    def _():
        s_scr[...] = s0f_ref[...]

    @pl.when(jnp.logical_and(c == 0, d == 1))
    def _():
        s_scr[...] = s0b_ref[...]

    df = d.astype(F32)
    sgn = 1.0 - 2.0 * df
    ii = lax.broadcasted_iota(jnp.int32, (C, C), 0).astype(F32)
    jj = lax.broadcasted_iota(jnp.int32, (C, C), 1).astype(F32)
    dd = (ii - jj) * sgn
    feeds = dd >= 0.0
    ddc = jnp.maximum(dd, 0.0)
    ri = lax.broadcasted_iota(jnp.int32, (C, 1), 0).astype(F32)
    pos_q = (ri + 1.0) + df * (C - 2.0 * ri - 1.0)
    pos_k = (C - 1.0 - ri) + df * (2.0 * ri - C + 1.0)
    chunk_len = jnp.full((1, A_DV), float(C), F32)

    outs = []
    for h in range(A_H):
        lg = lg_ref[d, h]
        dmask = jnp.where(feeds, jnp.exp(lg * ddc), 0.0)
        qh = q_ref[:, h * A_DK:(h + 1) * A_DK] * (A_DK ** -0.5)
        kh = k_ref[:, h * A_DK:(h + 1) * A_DK]
        vh = v_ref[:, h * A_DV:(h + 1) * A_DV].astype(BF16)
        s = s_scr[h]
        att = _dot_nt(qh.astype(BF16), kh.astype(BF16)) * dmask
        o = _dot(att.astype(BF16), vh) + _dot((qh * jnp.exp(lg * pos_q)).astype(BF16), s.astype(BF16))
        kd = kh * jnp.exp(lg * pos_k)
        s_scr[h] = jnp.exp(lg * chunk_len) * s + _dot(kd.T.astype(BF16), vh)
        outs.append(o)
    o_all = jnp.concatenate(outs, axis=-1)

    @pl.when(d == 0)
    def _():
        of_scr[c] = o_all

    @pl.when(d == 1)
    def _():
        tot = o_all + of_scr[n - 1 - c]
        res = []
        for h in range(A_H):
            sl = slice(h * A_DV, (h + 1) * A_DV)
            t = tot[:, sl]
            dev = t - jnp.mean(t, axis=-1, keepdims=True)
            y = dev * lax.rsqrt(jnp.mean(dev * dev, axis=-1, keepdims=True) + EPS) * ng_ref[:, sl]
            res.append(y * _silu(ag_ref[:, sl]))
        o_ref[...] = jnp.concatenate(res, axis=-1).astype(o_ref.dtype)

    @pl.when(jnp.logical_and(c == n - 1, d == 0))
    def _():
        sf_ref[...] = s_scr[...]

    @pl.when(jnp.logical_and(c == n - 1, d == 1))
    def _():
        sb_ref[...] = s_scr[...]


def _retention(z, B, T, log_gamma, s0f, s0b, norm_g):
    C = RET_CHUNK
    n = T // C
    HV = A_H * A_DV

    def row(b, d, c):
        return b * n + c + d * (n - 1 - 2 * c)

    st_spec = pl.BlockSpec((None, A_H, A_DK, A_DV), lambda b, d, c: (b, 0, 0, 0))
    st_shape = jax.ShapeDtypeStruct((B, A_H, A_DK, A_DV), F32)
    return pl.pallas_call(
        functools.partial(_ret_kernel, n=n, C=C),
        grid=(B, 2, n),
        in_specs=[pl.BlockSpec(memory_space=pltpu.SMEM),
                  pl.BlockSpec((C, 512), lambda b, d, c: (row(b, d, c), 0)),
                  pl.BlockSpec((C, 512), lambda b, d, c: (row(b, d, c), 1)),
                  pl.BlockSpec((C, HV), lambda b, d, c: (row(b, d, c), 1)),
                  pl.BlockSpec((C, HV), lambda b, d, c: (row(b, d, c), 2)),
                  st_spec, st_spec,
                  pl.BlockSpec((1, HV), lambda b, d, c: (0, 0))],
        out_specs=[pl.BlockSpec((C, HV), lambda b, d, c: (b * n + (n - 1) - d * c, 0)),
                   st_spec, st_spec],
        out_shape=[jax.ShapeDtypeStruct((B * T, HV), BF16), st_shape, st_shape],
        scratch_shapes=[pltpu.VMEM((A_H, A_DK, A_DV), F32), pltpu.VMEM((n, C, HV), F32)],
        compiler_params=_params("arbitrary", "arbitrary", "arbitrary"),
        name="retention",
    )(log_gamma, z, z, z, z, s0f, s0b, norm_g.reshape(1, HV))


def _group_sum_matrix(width, group):
    i = np.arange(width)
    return jnp.asarray((i[:, None] // group == i[None, :] // group).astype(np.float32), dtype=BF16)


def _q_pad_matrix():
    m = np.zeros((B_H * B_HD, B_H * LANES), np.float32)
    g = B_H // B_HKV
    for h in range(B_H):
        for t in range(B_HD):
            m[h * B_HD + t, h * LANES + (h // g) * B_HD + t] = 1.0
    return jnp.asarray(m, dtype=BF16)


def _rope_tables(T):
    rows = T // GRID_W
    row = np.repeat(np.arange(rows, dtype=np.float64), GRID_W)
    col = np.tile(np.arange(GRID_W, dtype=np.float64), rows)
    nq = B_HD // 4
    inv = ROPE_THETA ** (-np.arange(nq, dtype=np.float64) / nq)
    ang = np.concatenate([row[:, None] * inv, col[:, None] * inv], axis=-1)
    cos = np.repeat(np.cos(ang), 2, axis=-1)
    sin = np.repeat(np.sin(ang), 2, axis=-1)
    sign = np.tile(np.array([-1.0, 1.0]), B_HD // 2)
    reps = LANES // B_HD
    return (jnp.asarray(np.tile(cos, (1, reps)), dtype=F32),
            jnp.asarray(np.tile(sin * sign, (1, reps)), dtype=F32))


def _group_rmsnorm(x, gsum, g):
    x2 = x * x
    hi = x2.astype(BF16)
    lo = (x2 - hi.astype(F32)).astype(BF16)
    ss = _dot(hi, gsum) + _dot(lo, gsum)
    return x * lax.rsqrt(ss * (1.0 / B_HD) + EPS) * g


def _rotate_pairs(x, cos, sin_signed):
    n = x.shape[1]
    lane = lax.broadcasted_iota(jnp.int32, x.shape, 1)
    partner = jnp.where(lane % 2 == 0, pltpu.roll(x, n - 1, 1), pltpu.roll(x, 1, 1))
    reps = n // LANES
    if reps > 1:
        cos = jnp.concatenate([cos] * reps, axis=1)
        sin_signed = jnp.concatenate([sin_signed] * reps, axis=1)
    return x * cos + partner * sin_signed


def _bprep_kernel(z_ref, qg_ref, kg_ref, cos_ref, sin_ref, gq_ref, gk_ref, pad_ref,
                  qpad_ref, kn_ref, kr_ref, vb_ref, *, rope):
    nq = B_H * B_HD
    nk = B_HKV * B_HD
    qn = _group_rmsnorm(z_ref[:, 0:nq], gq_ref[...], qg_ref[...])
    kn = _group_rmsnorm(z_ref[:, nq:nq + nk], gk_ref[...], kg_ref[...])
    kn_ref[...] = kn
    if rope:
        qn = _rotate_pairs(qn, cos_ref[...], sin_ref[...])
        kn = _rotate_pairs(kn, cos_ref[...], sin_ref[...])
    kr_ref[...] = kn.astype(BF16)
    vb_ref[...] = z_ref[:, nq + nk:nq + 2 * nk].astype(BF16)
    qs = (qn * (B_HD ** -0.5)).astype(BF16)
    qpad_ref[...] = _dot(qs, pad_ref[...]).astype(BF16)


def _bprep(z, T, q_g, k_g, rope):
    N = z.shape[0]
    TM = min(512, T)
    nq = B_H * B_HD
    nk = B_HKV * B_HD
    width = nq + 2 * nk
    col = (A_H * (2 * A_DK + 2 * A_DV)) // width
    cos, sin = _rope_tables(T if rope else TM)
    nt = T // TM if rope else 1
    const = lambda i: (0, 0)
    return pl.pallas_call(
        functools.partial(_bprep_kernel, rope=rope),
        grid=(N // TM,),
        in_specs=[pl.BlockSpec((TM, width), lambda i: (i, col)),
                  pl.BlockSpec((1, nq), const),
                  pl.BlockSpec((1, nk), const),
                  pl.BlockSpec((TM, LANES), lambda i: (i % nt, 0)),
                  pl.BlockSpec((TM, LANES), lambda i: (i % nt, 0)),
                  pl.BlockSpec((nq, nq), const),
                  pl.BlockSpec((nk, nk), const),
                  pl.BlockSpec((nq, B_H * LANES), const)],
        out_specs=[pl.BlockSpec((TM, B_H * LANES), lambda i: (i, 0)),
                   pl.BlockSpec((TM, nk), lambda i: (i, 0)),
                   pl.BlockSpec((TM, nk), lambda i: (i, 0)),
                   pl.BlockSpec((TM, nk), lambda i: (i, 0))],
        out_shape=[jax.ShapeDtypeStruct((N, B_H * LANES), BF16),
                   jax.ShapeDtypeStruct((N, nk), F32),
                   jax.ShapeDtypeStruct((N, nk), BF16),
                   jax.ShapeDtypeStruct((N, nk), BF16)],
        compiler_params=_params("arbitrary"),
        name="attn_prep",
    )(z, jnp.tile(q_g, B_H).reshape(1, nq), jnp.tile(k_g, B_HKV).reshape(1, nk), cos, sin,
      _group_sum_matrix(nq, B_HD), _group_sum_matrix(nk, B_HD), _q_pad_matrix())


def _attn_kernel(*refs, has_cache):
    if has_cache:
        q_ref, k_ref, v_ref, ck_ref, cv_ref, o_ref = refs
        ck = ck_ref[...].astype(BF16)
        cv = cv_ref[...].astype(BF16)
    else:
        q_ref, k_ref, v_ref, o_ref = refs
    k = k_ref[...]
    v = v_ref[...]
    g = B_H // B_HKV
    lane = lax.broadcasted_iota(jnp.int32, (q_ref.shape[0], LANES), 1)
    outs = []
    for j in range(B_H // 2):
        pair = []
        for half in range(2):
            h = 2 * j + half
            q = q_ref[:, h * LANES:(h + 1) * LANES]
            s = _dot_nt(q, k)
            m = jnp.max(s, axis=-1, keepdims=True)
            if has_cache:
                sc = _dot_nt(q, ck)
                m = jnp.maximum(m, jnp.max(sc, axis=-1, keepdims=True))
            p = jnp.exp(s - m)
            l = jnp.sum(p, axis=-1, keepdims=True)
            r = _dot(p.astype(BF16), v)
            if has_cache:
                pc = jnp.exp(sc - m)
                l = l + jnp.sum(pc, axis=-1, keepdims=True)
                r = r + _dot(pc.astype(BF16), cv)
            r = r / l
            if h // g != half:
                r = pltpu.roll(r, B_HD, 1)
            pair.append(r)
        outs.append(jnp.where(lane < B_HD, pair[0], pair[1]))
    o_ref[...] = jnp.concatenate(outs, axis=-1).astype(o_ref.dtype)


def _attention(qpad, kr, vb, B, T, cache_k, cache_v):
    has_cache = cache_k is not None
    TQ = Q_TILE
    nq = T // TQ
    nk = B_HKV * B_HD
    in_specs = [pl.BlockSpec((TQ, B_H * LANES), lambda b, i: (b * nq + i, 0)),
                pl.BlockSpec((T, nk), lambda b, i: (b, 0)),
                pl.BlockSpec((T, nk), lambda b, i: (b, 0))]
    args = [qpad, kr, vb]
    if has_cache:
        P = cache_k.shape[1]
        in_specs += [pl.BlockSpec((None, P, nk), lambda b, i: (b, 0, 0))] * 2
        args += [cache_k, cache_v]
    return pl.pallas_call(
        functools.partial(_attn_kernel, has_cache=has_cache),
        grid=(B, nq),
        in_specs=in_specs,
        out_specs=pl.BlockSpec((TQ, B_H * B_HD), lambda b, i: (b * nq + i, 0)),
        out_shape=jax.ShapeDtypeStruct((B * T, B_H * B_HD), BF16),
        compiler_params=_params("arbitrary", "arbitrary"),
        name="attention",
    )(*args)


def _proj_res_kernel(*refs, n_in):
    x_ref, gate_ref = refs[0], refs[1]
    o_refs = refs[2:2 + n_in]
    w_refs = refs[2 + n_in:2 + 2 * n_in]
    out_ref = refs[2 + 2 * n_in]
    wbf_refs = refs[3 + 2 * n_in:]

    @pl.when(pl.program_id(0) == 0)
    def _():
        for w_ref, wbf_ref in zip(w_refs, wbf_refs):
            wbf_ref[...] = w_ref[...].astype(BF16)

    acc = _dot(o_refs[0][...], wbf_refs[0][...])
    for o_ref, wbf_ref in zip(o_refs[1:], wbf_refs[1:]):
        acc = acc + _dot(o_ref[...], wbf_ref[...])
    out_ref[...] = x_ref[...] + gate_ref[...] * acc


def _proj_res(x, mod, part, acts, w, rows_per_group):
    N, D = x.shape
    TM = min(512, rows_per_group)
    n_in = len(acts)
    widths = [a.shape[1] for a in acts]
    offs = np.cumsum([0] + widths[:-1]).tolist()
    in_specs = [pl.BlockSpec((TM, D), lambda i: (i, 0)),
                _mod_spec(part, D, TM, rows_per_group, 0)]
    in_specs += [pl.BlockSpec((TM, wd), lambda i: (i, 0)) for wd in widths]
    in_specs += [pl.BlockSpec((wd, D), functools.partial(lambda i, blk: (blk, 0), blk=off // wd))
                 for wd, off in zip(widths, offs)]
    return pl.pallas_call(
        functools.partial(_proj_res_kernel, n_in=n_in),
        grid=(N // TM,),
        in_specs=in_specs,
        out_specs=pl.BlockSpec((TM, D), lambda i: (i, 0)),
        out_shape=jax.ShapeDtypeStruct((N, D), F32),
        scratch_shapes=[pltpu.VMEM((wd, D), BF16) for wd in widths],
        compiler_params=_params("arbitrary"),
        name="proj_residual",
    )(x, mod, *acts, *([w] * n_in))


def _ffn_kernel(x_ref, g_ref, sh_ref, sc_ref, gate_ref, w1_ref, w3_ref, w2_ref, out_ref, h_scr, acc_scr, *, nf):
    f = pl.program_id(1)

    @pl.when(f == 0)
    def _():
        h_scr[...] = _norm_mod(x_ref[...], g_ref[...], sh_ref[...], sc_ref[...]).astype(BF16)
        acc_scr[...] = jnp.zeros_like(acc_scr)

    h = h_scr[...]
    a = _dot(h, w1_ref[...].astype(BF16))
    b = _dot(h, w3_ref[...].astype(BF16))
    acc_scr[...] += _dot((_silu(a) * b).astype(BF16), w2_ref[...].astype(BF16))

    @pl.when(f == nf - 1)
    def _():
        out_ref[...] = x_ref[...] + gate_ref[...] * acc_scr[...]


def _ffn(x, g, mod, w1, w3, w2, rows_per_group):
    N, D = x.shape
    FF = w1.shape[1]
    TM, TF = min(1024, rows_per_group), 256
    nf = FF // TF
    return pl.pallas_call(
        functools.partial(_ffn_kernel, nf=nf),
        grid=(N // TM, nf),
        in_specs=[pl.BlockSpec((TM, D), lambda i, f: (i, 0)),
                  pl.BlockSpec((1, D), lambda i, f: (0, 0)),
                  _mod_spec(3, D, TM, rows_per_group, 0),
                  _mod_spec(4, D, TM, rows_per_group, 0),
                  _mod_spec(5, D, TM, rows_per_group, 0),
                  pl.BlockSpec((D, TF), lambda i, f: (0, f)),
                  pl.BlockSpec((D, TF), lambda i, f: (0, f)),
                  pl.BlockSpec((TF, D), lambda i, f: (f, 0))],
        out_specs=pl.BlockSpec((TM, D), lambda i, f: (i, 0)),
        out_shape=jax.ShapeDtypeStruct((N, D), F32),
        scratch_shapes=[pltpu.VMEM((TM, D), BF16), pltpu.VMEM((TM, D), F32)],
        compiler_params=_params("arbitrary", "arbitrary"),
        name="ffn",
    )(x, g.reshape(1, D), mod, mod, mod, w1, w3, w2)


def _moe_kernel(x_ref, g_ref, sh_ref, sc_ref, gate_ref, rw_ref, w1_ref, w3_ref, w2_ref, fg_ref,
                out_ref, h_scr, acc_scr, gates_scr, *, nf):
    e = pl.program_id(1)
    f = pl.program_id(2)
    lane = lax.broadcasted_iota(jnp.int32, gates_scr.shape, 1).astype(F32)

    @pl.when(jnp.logical_and(e == 0, f == 0))
    def _():
        h = _norm_mod(x_ref[...], g_ref[...], sh_ref[...], sc_ref[...])
        h_scr[...] = h.astype(BF16)
        acc_scr[...] = jnp.zeros_like(acc_scr)
        logits = jnp.dot(h, rw_ref[...], precision=HIGHEST, preferred_element_type=F32)
        logits = jnp.where(lane < N_EXPERTS, logits, -jnp.inf)
        m1 = jnp.max(logits, axis=-1, keepdims=True)
        i1 = jnp.min(jnp.where(logits == m1, lane, float(LANES)), axis=-1, keepdims=True)
        rest = jnp.where(lane == i1, -jnp.inf, logits)
        m2 = jnp.max(rest, axis=-1, keepdims=True)
        i2 = jnp.min(jnp.where(rest == m2, lane, float(LANES)), axis=-1, keepdims=True)
        e2 = jnp.exp(m2 - m1)
        gates_scr[...] = jnp.where(lane == i1, 1.0 / (1.0 + e2), jnp.where(lane == i2, e2 / (1.0 + e2), 0.0))

    ge = jnp.sum(jnp.where(lane == e.astype(F32), gates_scr[...], 0.0), axis=-1, keepdims=True)
    h = h_scr[...]
    a = _dot(h, w1_ref[...].astype(BF16))
    b = _dot(h, w3_ref[...].astype(BF16))
    acc_scr[...] += _dot((_silu(a) * b * ge).astype(BF16), w2_ref[...].astype(BF16))

    @pl.when(jnp.logical_and(e == N_EXPERTS - 1, f == nf - 1))
    def _():
        y = x_ref[...] + gate_ref[...] * acc_scr[...]
        out_ref[...] = y * lax.rsqrt(jnp.mean(y * y, axis=-1, keepdims=True) + EPS) * fg_ref[...]


def _moe(x, g, mod, router_w, w1, w3, w2, final_g, rows_per_group):
    N, D = x.shape
    E, _, FF = w1.shape
    TM, TF = min(1024, rows_per_group), 256
    nf = FF // TF
    rw = jnp.pad(router_w, ((0, 0), (0, LANES - E)))
    return pl.pallas_call(
        functools.partial(_moe_kernel, nf=nf),
        grid=(N // TM, E, nf),
        in_specs=[pl.BlockSpec((TM, D), lambda i, e, f: (i, 0)),
                  pl.BlockSpec((1, D), lambda i, e, f: (0, 0)),
                  _mod_spec(3, D, TM, rows_per_group, 0),
                  _mod_spec(4, D, TM, rows_per_group, 0),
                  _mod_spec(5, D, TM, rows_per_group, 0),
                  pl.BlockSpec((D, LANES), lambda i, e, f: (0, 0)),
                  pl.BlockSpec((None, D, TF), lambda i, e, f: (e, 0, f)),
                  pl.BlockSpec((None, D, TF), lambda i, e, f: (e, 0, f)),
                  pl.BlockSpec((None, TF, D), lambda i, e, f: (e, f, 0)),
                  pl.BlockSpec((1, D), lambda i, e, f: (0, 0))],
        out_specs=pl.BlockSpec((TM, D), lambda i, e, f: (i, 0)),
        out_shape=jax.ShapeDtypeStruct((N, D), F32),
        scratch_shapes=[pltpu.VMEM((TM, D), BF16), pltpu.VMEM((TM, D), F32), pltpu.VMEM((TM, LANES), F32)],
        compiler_params=_params("arbitrary", "arbitrary", "arbitrary"),
        name="moe",
    )(x, g.reshape(1, D), mod, mod, mod, rw, w1, w3, w2, final_g.reshape(1, D))


def _gla_levels(C):
    lv, c = [], C // 2
    while c >= SUB:
        lv.append(c)
        c //= 2
    return lv


def _gla_tables(C):
    levels = _gla_levels(C)
    nr = 2 + 2 * len(levels)
    mat = np.zeros((2, nr * C, C), np.float32)
    code = np.zeros((2, C, C), np.int32)
    for d in range(2):
        p = np.arange(C) if d == 0 else C - 1 - np.arange(C)
        pi, pj = p[:, None], p[None, :]
        mat[d, 0:C] = pj <= pi
        mat[d, C:2 * C] = pj > pi
        code[d] = np.where((pj <= pi) & (pi // SUB == pj // SUB), 1, 0)
        for lv, c in enumerate(levels):
            blk = pi // c
            later = blk % 2 == 1
            mat[d, (2 + 2 * lv) * C:(3 + 2 * lv) * C] = later & (pj > blk * c - 1) & (pj <= pi)
            mat[d, (3 + 2 * lv) * C:(4 + 2 * lv) * C] = (~later) & (pj > pi) & (pj <= (blk + 1) * c - 1)
            pair = (pi // (2 * c) == pj // (2 * c)) & (pi // c != pj // c) & (pj <= pi)
            code[d] = np.where(pair, 2 + lv, code[d])
    ones = np.zeros((SUB * LANES, C), np.float32)
    for jj in range(SUB):
        ones[jj * LANES:(jj + 1) * LANES, jj::SUB] = 1.0
    return jnp.asarray(mat, dtype=BF16), jnp.asarray(code), jnp.asarray(ones, dtype=BF16)


def _bcast_sublane(x, jj):
    r, w = x.shape
    x3 = x.reshape(r // SUB, SUB, w)
    return jnp.broadcast_to(x3[:, jj:jj + 1, :], x3.shape).reshape(r, w)


def _t128(x):
    r, w = x.shape
    if w > LANES:
        return jnp.concatenate([x[:, i:i + LANES].T for i in range(0, w, LANES)], axis=0)
    return jnp.concatenate([x[i:i + LANES, :].T for i in range(0, r, LANES)], axis=1)


def _gla_kernel(q_ref, k_ref, v_ref, g_ref, lr_ref, wg_ref, ba_ref, mat_ref, code_ref, ones_ref,
                s0f_ref, s0b_ref, ng_ref, o_ref, sf_ref, sb_ref, st_scr, of_scr, *, n, C):
    d = pl.program_id(1)
    c = pl.program_id(2)
    levels = _gla_levels(C)

    @pl.when(jnp.logical_and(c == 0, d == 0))
    def _():
        for h in range(C_H):
            st_scr[h] = _t128(s0f_ref[h])

    @pl.when(jnp.logical_and(c == 0, d == 1))
    def _():
        for h in range(C_H):
            st_scr[h] = _t128(s0b_ref[h])

    xg = jnp.dot(lr_ref[...], wg_ref[...], precision=HIGHEST, preferred_element_type=F32) + ba_ref[...]
    la = (jnp.minimum(xg, 0.0) - jnp.log1p(jnp.exp(-jnp.abs(xg)))) * (1.0 / C_TAU)
    hi = la.astype(BF16)
    r1 = la - hi.astype(F32)
    mid = r1.astype(BF16)
    lo = (r1 - mid.astype(F32)).astype(BF16)
    mat = mat_ref[...]
    cum = _dot(mat, hi) + _dot(mat, mid) + _dot(mat, lo)
    code = code_ref[...]
    ones = ones_ref[...]

    outs = []
    for h in range(C_H):
        ks = slice(h * C_DK, (h + 1) * C_DK)
        qh = q_ref[:, ks] * (C_DK ** -0.5)
        kh = k_ref[:, ks]
        vh = v_ref[:, h * C_DV:(h + 1) * C_DV]
        vb = vh.astype(BF16)
        b = cum[0:C, ks]
        b_rest = cum[C:2 * C, ks]
        ps = []
        for jj in range(SUB):
            dec = jnp.exp(jnp.minimum(b - _bcast_sublane(b, jj), 0.0))
            ps.append((qh * _bcast_sublane(kh, jj) * dec).astype(BF16))
        att = jnp.where(code == 1, _dot(jnp.concatenate(ps, axis=1), ones), 0.0)
        for lv in range(len(levels)):
            eq = cum[(2 + 2 * lv) * C:(3 + 2 * lv) * C, ks]
            ek = cum[(3 + 2 * lv) * C:(4 + 2 * lv) * C, ks]
            a_lv = _dot_nt((qh * jnp.exp(eq)).astype(BF16), (kh * jnp.exp(ek)).astype(BF16))
            att = jnp.where(code == 2 + lv, a_lv, att)
        st = st_scr[h]
        o = _dot(att.astype(BF16), vb) + _dot_nt((qh * jnp.exp(b)).astype(BF16), st.astype(BF16))
        b_end = b[0:1, :] + b_rest[0:1, :]
        ke = (kh * jnp.exp(b_rest)).astype(BF16)
        st_scr[h] = jnp.exp(b_end) * st + _dot(_t128(vh).astype(BF16), ke)
        outs.append(o)
    o_all = jnp.concatenate(outs, axis=-1)

    @pl.when(d == 0)
    def _():
        of_scr[c] = o_all

    @pl.when(d == 1)
    def _():
        tot = o_all + of_scr[n - 1 - c]
        res = []
        for h in range(C_H):
            sl = slice(h * C_DV, (h + 1) * C_DV)
            t = tot[:, sl]
            y = t * lax.rsqrt(jnp.mean(t * t, axis=-1, keepdims=True) + EPS) * ng_ref[:, sl]
            res.append(y * _silu(g_ref[:, sl]))
        o_ref[...] = jnp.concatenate(res, axis=-1).astype(o_ref.dtype)

    @pl.when(jnp.logical_and(c == n - 1, d == 0))
    def _():
        for h in range(C_H):
            sf_ref[h] = _t128(st_scr[h])

    @pl.when(jnp.logical_and(c == n - 1, d == 1))
    def _():
        for h in range(C_H):
            sb_ref[h] = _t128(st_scr[h])


def _gla(z, B, T, w_a2, b_a, s0f, s0b, norm_g):
    C = GLA_CHUNK
    n = T // C
    HK = C_H * C_DK
    HV = C_H * C_DV
    mat, code, ones = _gla_tables(C)
    nr = mat.shape[1] // C
    wg = jnp.zeros((2, LANES, HK), F32)
    for dr in range(2):
        wg = wg.at[dr, dr * C_RANK:(dr + 1) * C_RANK, :].set(w_a2[dr])

    def row(b, d, c):
        return b * n + c + d * (n - 1 - 2 * c)

    st_spec = pl.BlockSpec((None, C_H, C_DK, C_DV), lambda b, d, c: (b, 0, 0, 0))
    st_shape = jax.ShapeDtypeStruct((B, C_H, C_DK, C_DV), F32)
    return pl.pallas_call(
        functools.partial(_gla_kernel, n=n, C=C),
        grid=(B, 2, n),
        in_specs=[pl.BlockSpec((C, HK), lambda b, d, c: (row(b, d, c), 0)),
                  pl.BlockSpec((C, HK), lambda b, d, c: (row(b, d, c), 1)),
                  pl.BlockSpec((C, HV), lambda b, d, c: (row(b, d, c), 1)),
                  pl.BlockSpec((C, HV), lambda b, d, c: (row(b, d, c), 2)),
                  pl.BlockSpec((C, LANES), lambda b, d, c: (row(b, d, c), (2 * HK + 2 * HV) // LANES)),
                  pl.BlockSpec((None, LANES, HK), lambda b, d, c: (d, 0, 0)),
                  pl.BlockSpec((None, 1, HK), lambda b, d, c: (d, 0, 0)),
                  pl.BlockSpec((None, nr * C, C), lambda b, d, c: (d, 0, 0)),
                  pl.BlockSpec((None, C, C), lambda b, d, c: (d, 0, 0)),
                  pl.BlockSpec((SUB * LANES, C), lambda b, d, c: (0, 0)),
                  st_spec, st_spec,
                  pl.BlockSpec((1, HV), lambda b, d, c: (0, 0))],
        out_specs=[pl.BlockSpec((C, HV), lambda b, d, c: (b * n + (n - 1) - d * c, 0)),
                   st_spec, st_spec],
        out_shape=[jax.ShapeDtypeStruct((B * T, HV), BF16), st_shape, st_shape],
        scratch_shapes=[pltpu.VMEM((C_H, C_DV, C_DK), F32), pltpu.VMEM((n, C, HV), F32)],
        compiler_params=_params("arbitrary", "arbitrary", "arbitrary"),
        name="gla",
    )(z, z, z, z, z, wg, b_a.reshape(2, 1, HK), mat, code, ones, s0f, s0b, norm_g.reshape(1, HV))


def _run_stream(x, B, T, mods, ctx, p):
    N, D = x.shape
    rpg = N // mods[0].shape[0]
    TM = min(1024, rpg)

    z = _norm_mm(x, p['norm1_g'][0], mods[0], (0, 1), p['even_w_in'][0], 1280, TM, rpg)
    if ctx is None:
        s0 = jnp.zeros((B, A_H, A_DK, A_DV), F32)
        a_f0, a_b0, cache_k, cache_v = s0, s0, None, None
    else:
        cache_k, cache_v, a_f0, a_b0 = ctx[0], ctx[1], ctx[2], ctx[3]
    o_a, a_sf, a_sb = _retention(z, B, T, p['a_log_gamma'][0], a_f0, a_b0, p['a_norm_g'][0])
    qpad, k_norm, k_rot, v_bf = _bprep(z, T, p['b_q_g'][0], p['b_k_g'][0], rope=ctx is not None)
    o_b = _attention(qpad, k_rot, v_bf, B, T, cache_k, cache_v)
    x = _proj_res(x, mods[0], 2, [o_a, o_b], p['even_w_out'][0], rpg)
    x = _ffn(x, p['norm2_g'][0], mods[0], p['ff_w1'][0], p['ff_w3'][0], p['ff_w2'][0], rpg)

    w_in = p['odd_w_in'][0]
    pad = (-w_in.shape[1]) % 640
    z1 = _norm_mm(x, p['norm1_g'][1], mods[1], (0, 1), jnp.pad(w_in, ((0, 0), (0, pad))), 640, TM, rpg)
    if ctx is None:
        s0 = jnp.zeros((B, C_H, C_DK, C_DV), F32)
        c_f0, c_b0 = s0, s0
    else:
        c_f0, c_b0 = ctx[4], ctx[5]
    o_c, c_sf, c_sb = _gla(z1, B, T, p['c_w_a2'][0], p['c_b_a'][0], c_f0, c_b0, p['c_norm_g'][0])
    x = _proj_res(x, mods[1], 2, [o_c], p['odd_w_out'][0], rpg)
    y = _moe(x, p['norm2_g'][1], mods[1], p['router_w'][0], p['moe_w1'][0], p['moe_w3'][0], p['moe_w2'][0],
             p['final_g'], rpg)
    v_raw = z[:, EVEN_V_OFF:EVEN_V_OFF + B_HKV * B_HD]
    return y, (k_norm, v_raw, a_sf, a_sb, c_sf, c_sb)


EVEN_V_OFF = A_H * (2 * A_DK + 2 * A_DV) + (B_H + B_HKV) * B_HD


def kernel(x_prompt, x_sample, c, cache_b_k, cache_b_v, state_a_fwd, state_a_bwd, state_c_fwd, state_c_bwd,
           c_ctx, w_mod, b_mod, norm1_g, norm2_g, final_g, even_w_in, even_w_out, a_log_gamma, a_norm_g,
           b_q_g, b_k_g, odd_w_in, c_w_a2, c_b_a, c_norm_g, odd_w_out, ff_w1, ff_w3, ff_w2,
           router_w, moe_w1, moe_w3, moe_w2):
    Bp, Tp, D = x_prompt.shape
    Bs, Ts, _ = x_sample.shape
    L = w_mod.shape[0]
    assert L == 2 and even_w_in.shape[0] == 1 and odd_w_in.shape[0] == 1
    p = dict(norm1_g=norm1_g, norm2_g=norm2_g, final_g=final_g, even_w_in=even_w_in, even_w_out=even_w_out,
             a_log_gamma=a_log_gamma, a_norm_g=a_norm_g, b_q_g=b_q_g, b_k_g=b_k_g, odd_w_in=odd_w_in,
             c_w_a2=c_w_a2, c_b_a=c_b_a, c_norm_g=c_norm_g, odd_w_out=odd_w_out, ff_w1=ff_w1, ff_w3=ff_w3,
             ff_w2=ff_w2, router_w=router_w, moe_w1=moe_w1, moe_w3=moe_w3, moe_w2=moe_w2)

    rows = 8
    conds = jnp.concatenate([c_ctx[None, :], c, jnp.zeros((rows - 1 - Bs, D), F32)], axis=0)
    mod = _modulation(conds, w_mod, b_mod).reshape(L, rows, 6, 1, D)
    mods_p = [mod[l, 0:1] for l in range(L)]
    mods_s = [mod[l, 1:1 + Bs] for l in range(L)]

    y_p, kept = _run_stream(x_prompt.reshape(Bp * Tp, D), Bp, Tp, mods_p, None, p)
    nk = B_HKV * B_HD
    ctx = (cache_b_k[:, 0].reshape(Bs, -1, nk), cache_b_v[:, 0].reshape(Bs, -1, nk),
           state_a_fwd[:, 0], state_a_bwd[:, 0], state_c_fwd[:, 0], state_c_bwd[:, 0])
    y_s, _ = _run_stream(x_sample.reshape(Bs * Ts, D), Bs, Ts, mods_s, ctx, p)

    k_norm, v_raw, a_sf, a_sb, c_sf, c_sb = kept
    return (y_p.reshape(Bp, Tp, D), y_s.reshape(Bs, Ts, D),
            k_norm.reshape(Bp, 1, Tp, B_HKV, B_HD), v_raw.reshape(Bp, 1, Tp, B_HKV, B_HD),
            a_sf[:, None], a_sb[:, None], c_sf[:, None], c_sb[:, None])
```

```python
import functools

import numpy as np
import jax
import jax.numpy as jnp
from jax import lax
from jax.experimental import pallas as pl
from jax.experimental.pallas import tpu as pltpu

F32 = jnp.float32
BF16 = jnp.bfloat16
EPS = 1e-6
HIGHEST = lax.Precision.HIGHEST

VMEM_LIMIT_BYTES = 56 * 1024 * 1024

A_H, A_DK, A_DV = 4, 128, 256
B_H, B_HKV, B_HD = 8, 2, 64
C_H, C_DK, C_DV, C_RANK = 4, 128, 256, 16
C_TAU = 16.0
GRID_W = 64
ROPE_THETA = 10000.0
N_EXPERTS = 8
LANES = 128
SUB = 8
RET_CHUNK = 128
GLA_CHUNK = 128
Q_TILE = 128


def _params(*sem):
    return pltpu.CompilerParams(dimension_semantics=sem, vmem_limit_bytes=VMEM_LIMIT_BYTES)


def _dot(a, b):
    return jnp.dot(a, b, preferred_element_type=F32)


def _dot_nt(a, b):
    return lax.dot_general(a, b, (((1,), (1,)), ((), ())), preferred_element_type=F32)


def _silu(x):
    return x * jax.nn.sigmoid(x)


def _norm_mod(x, g, sh, sc):
    r = lax.rsqrt(jnp.mean(x * x, axis=-1, keepdims=True) + EPS)
    return (x * r * g) * (1.0 + sc) + sh


def _mod_kernel(c_ref, w_ref, b_ref, o_ref):
    c = c_ref[...]
    o_ref[...] = jnp.dot(_silu(c), w_ref[...], precision=HIGHEST, preferred_element_type=F32) + b_ref[...]


def _modulation(conds, w_mod, b_mod):
    L, D, D6 = w_mod.shape
    R = conds.shape[0]
    TN = 1024
    return pl.pallas_call(
        _mod_kernel,
        grid=(L, D6 // TN),
        in_specs=[pl.BlockSpec((R, D), lambda l, j: (0, 0)),
                  pl.BlockSpec((None, D, TN), lambda l, j: (l, 0, j)),
                  pl.BlockSpec((None, 1, TN), lambda l, j: (l, 0, j))],
        out_specs=pl.BlockSpec((None, R, TN), lambda l, j: (l, 0, j)),
        out_shape=jax.ShapeDtypeStruct((L, R, D6), F32),
        compiler_params=_params("arbitrary", "arbitrary"),
        name="modulation",
    )(conds, w_mod, b_mod.reshape(L, 1, D6))


def _mod_spec(part, D, TM, rows_per_group, axis):
    def idx(*g):
        return ((g[axis] * TM) // rows_per_group, part, 0, 0)
    return pl.BlockSpec((None, None, 1, D), idx)


def _norm_mm_kernel(x_ref, g_ref, sh_ref, sc_ref, w_ref, o_ref, wbf_ref):
    @pl.when(pl.program_id(1) == 0)
    def _():
        wbf_ref[...] = w_ref[...].astype(BF16)

    h = _norm_mod(x_ref[...], g_ref[...], sh_ref[...], sc_ref[...])
    o_ref[...] = _dot(h.astype(BF16), wbf_ref[...]).astype(o_ref.dtype)


def _norm_mm(x, g, mod, parts, w, TN, TM, rows_per_group):
    N, D = x.shape
    NO = w.shape[1]
    return pl.pallas_call(
        _norm_mm_kernel,
        grid=(NO // TN, N // TM),
        in_specs=[pl.BlockSpec((TM, D), lambda j, i: (i, 0)),
                  pl.BlockSpec((1, D), lambda j, i: (0, 0)),
                  _mod_spec(parts[0], D, TM, rows_per_group, 1),
                  _mod_spec(parts[1], D, TM, rows_per_group, 1),
                  pl.BlockSpec((D, TN), lambda j, i: (0, j))],
        out_specs=pl.BlockSpec((TM, TN), lambda j, i: (i, j)),
        out_shape=jax.ShapeDtypeStruct((N, NO), F32),
        scratch_shapes=[pltpu.VMEM((D, TN), BF16)],
        compiler_params=_params("arbitrary", "arbitrary"),
        name="norm_mm",
    )(x, g.reshape(1, D), mod, mod, w)


def _ret_kernel(lg_ref, q_ref, k_ref, v_ref, ag_ref, s0f_ref, s0b_ref, ng_ref,
                o_ref, sf_ref, sb_ref, s_scr, of_scr, *, n, C):
    d = pl.program_id(1)
    c = pl.program_id(2)

    @pl.when(jnp.logical_and(c == 0, d == 0))
    def _():
        s_scr[...] = s0f_ref[...]

    @pl.when(jnp.logical_and(c == 0, d == 1))
    def _():
        s_scr[...] = s0b_ref[...]

    df = d.astype(F32)
    sgn = 1.0 - 2.0 * df
    ii = lax.broadcasted_iota(jnp.int32, (C, C), 0).astype(F32)
    jj = lax.broadcasted_iota(jnp.int32, (C, C), 1).astype(F32)
    dd = (ii - jj) * sgn
    feeds = dd >= 0.0
    ddc = jnp.maximum(dd, 0.0)
    ri = lax.broadcasted_iota(jnp.int32, (C, 1), 0).astype(F32)
    pos_q = (ri + 1.0) + df * (C - 2.0 * ri - 1.0)
    pos_k = (C - 1.0 - ri) + df * (2.0 * ri - C + 1.0)
    chunk_len = jnp.full((1, A_DV), float(C), F32)

    outs = []
    for h in range(A_H):
        lg = lg_ref[d, h]
        dmask = jnp.where(feeds, jnp.exp(lg * ddc), 0.0)
        qh = q_ref[:, h * A_DK:(h + 1) * A_DK] * (A_DK ** -0.5)
        kh = k_ref[:, h * A_DK:(h + 1) * A_DK]
        vh = v_ref[:, h * A_DV:(h + 1) * A_DV].astype(BF16)
        s = s_scr[h]
        att = _dot_nt(qh.astype(BF16), kh.astype(BF16)) * dmask
        o = _dot(att.astype(BF16), vh) + _dot((qh * jnp.exp(lg * pos_q)).astype(BF16), s.astype(BF16))
        kd = kh * jnp.exp(lg * pos_k)
        s_scr[h] = jnp.exp(lg * chunk_len) * s + _dot(kd.T.astype(BF16), vh)
        outs.append(o)
    o_all = jnp.concatenate(outs, axis=-1)

    @pl.when(d == 0)
    def _():
        of_scr[c] = o_all

    @pl.when(d == 1)
    def _():
        tot = o_all + of_scr[n - 1 - c]
        res = []
        for h in range(A_H):
            sl = slice(h * A_DV, (h + 1) * A_DV)
            t = tot[:, sl]
            dev = t - jnp.mean(t, axis=-1, keepdims=True)
            y = dev * lax.rsqrt(jnp.mean(dev * dev, axis=-1, keepdims=True) + EPS) * ng_ref[:, sl]
            res.append(y * _silu(ag_ref[:, sl]))
        o_ref[...] = jnp.concatenate(res, axis=-1).astype(o_ref.dtype)

    @pl.when(jnp.logical_and(c == n - 1, d == 0))
    def _():
        sf_ref[...] = s_scr[...]

    @pl.when(jnp.logical_and(c == n - 1, d == 1))
    def _():
        sb_ref[...] = s_scr[...]


def _retention(z, B, T, log_gamma, s0f, s0b, norm_g):
    C = RET_CHUNK
    n = T // C
    HV = A_H * A_DV

    def row(b, d, c):
        return b * n + c + d * (n - 1 - 2 * c)

    st_spec = pl.BlockSpec((None, A_H, A_DK, A_DV), lambda b, d, c: (b, 0, 0, 0))
    st_shape = jax.ShapeDtypeStruct((B, A_H, A_DK, A_DV), F32)
    return pl.pallas_call(
        functools.partial(_ret_kernel, n=n, C=C),
        grid=(B, 2, n),
        in_specs=[pl.BlockSpec(memory_space=pltpu.SMEM),
                  pl.BlockSpec((C, 512), lambda b, d, c: (row(b, d, c), 0)),
                  pl.BlockSpec((C, 512), lambda b, d, c: (row(b, d, c), 1)),
                  pl.BlockSpec((C, HV), lambda b, d, c: (row(b, d, c), 1)),
                  pl.BlockSpec((C, HV), lambda b, d, c: (row(b, d, c), 2)),
                  st_spec, st_spec,
                  pl.BlockSpec((1, HV), lambda b, d, c: (0, 0))],
        out_specs=[pl.BlockSpec((C, HV), lambda b, d, c: (b * n + (n - 1) - d * c, 0)),
                   st_spec, st_spec],
        out_shape=[jax.ShapeDtypeStruct((B * T, HV), BF16), st_shape, st_shape],
        scratch_shapes=[pltpu.VMEM((A_H, A_DK, A_DV), F32), pltpu.VMEM((n, C, HV), F32)],
        compiler_params=_params("arbitrary", "arbitrary", "arbitrary"),
        name="retention",
    )(log_gamma, z, z, z, z, s0f, s0b, norm_g.reshape(1, HV))


def _group_sum_matrix(width, group):
    i = np.arange(width)
    return jnp.asarray((i[:, None] // group == i[None, :] // group).astype(np.float32), dtype=BF16)


def _q_pad_matrix():
    m = np.zeros((B_H * B_HD, B_H * LANES), np.float32)
    g = B_H // B_HKV
    for h in range(B_H):
        for t in range(B_HD):
            m[h * B_HD + t, h * LANES + (h // g) * B_HD + t] = 1.0
    return jnp.asarray(m, dtype=BF16)


def _rope_tables(T):
    rows = T // GRID_W
    row = np.repeat(np.arange(rows, dtype=np.float64), GRID_W)
    col = np.tile(np.arange(GRID_W, dtype=np.float64), rows)
    nq = B_HD // 4
    inv = ROPE_THETA ** (-np.arange(nq, dtype=np.float64) / nq)
    ang = np.concatenate([row[:, None] * inv, col[:, None] * inv], axis=-1)
    cos = np.repeat(np.cos(ang), 2, axis=-1)
    sin = np.repeat(np.sin(ang), 2, axis=-1)
    sign = np.tile(np.array([-1.0, 1.0]), B_HD // 2)
    reps = LANES // B_HD
    return (jnp.asarray(np.tile(cos, (1, reps)), dtype=F32),
            jnp.asarray(np.tile(sin * sign, (1, reps)), dtype=F32))


def _group_rmsnorm(x, gsum, g):
    x2 = x * x
    hi = x2.astype(BF16)
    lo = (x2 - hi.astype(F32)).astype(BF16)
    ss = _dot(hi, gsum) + _dot(lo, gsum)
    return x * lax.rsqrt(ss * (1.0 / B_HD) + EPS) * g


def _rotate_pairs(x, cos, sin_signed):
    n = x.shape[1]
    lane = lax.broadcasted_iota(jnp.int32, x.shape, 1)
    partner = jnp.where(lane % 2 == 0, pltpu.roll(x, n - 1, 1), pltpu.roll(x, 1, 1))
    reps = n // LANES
    if reps > 1:
        cos = jnp.concatenate([cos] * reps, axis=1)
        sin_signed = jnp.concatenate([sin_signed] * reps, axis=1)
    return x * cos + partner * sin_signed


def _bprep_kernel(z_ref, qg_ref, kg_ref, cos_ref, sin_ref, gq_ref, gk_ref, pad_ref,
                  qpad_ref, kn_ref, kr_ref, vb_ref, *, rope):
    nq = B_H * B_HD
    nk = B_HKV * B_HD
    qn = _group_rmsnorm(z_ref[:, 0:nq], gq_ref[...], qg_ref[...])
    kn = _group_rmsnorm(z_ref[:, nq:nq + nk], gk_ref[...], kg_ref[...])
    kn_ref[...] = kn
    if rope:
        qn = _rotate_pairs(qn, cos_ref[...], sin_ref[...])
        kn = _rotate_pairs(kn, cos_ref[...], sin_ref[...])
    kr_ref[...] = kn.astype(BF16)
    vb_ref[...] = z_ref[:, nq + nk:nq + 2 * nk].astype(BF16)
    qs = (qn * (B_HD ** -0.5)).astype(BF16)
    qpad_ref[...] = _dot(qs, pad_ref[...]).astype(BF16)


def _bprep(z, T, q_g, k_g, rope):
    N = z.shape[0]
    TM = min(512, T)
    nq = B_H * B_HD
    nk = B_HKV * B_HD
    width = nq + 2 * nk
    col = (A_H * (2 * A_DK + 2 * A_DV)) // width
    cos, sin = _rope_tables(T if rope else TM)
    nt = T // TM if rope else 1
    const = lambda i: (0, 0)
    return pl.pallas_call(
        functools.partial(_bprep_kernel, rope=rope),
        grid=(N // TM,),
        in_specs=[pl.BlockSpec((TM, width), lambda i: (i, col)),
                  pl.BlockSpec((1, nq), const),
                  pl.BlockSpec((1, nk), const),
                  pl.BlockSpec((TM, LANES), lambda i: (i % nt, 0)),
                  pl.BlockSpec((TM, LANES), lambda i: (i % nt, 0)),
                  pl.BlockSpec((nq, nq), const),
                  pl.BlockSpec((nk, nk), const),
                  pl.BlockSpec((nq, B_H * LANES), const)],
        out_specs=[pl.BlockSpec((TM, B_H * LANES), lambda i: (i, 0)),
                   pl.BlockSpec((TM, nk), lambda i: (i, 0)),
                   pl.BlockSpec((TM, nk), lambda i: (i, 0)),
                   pl.BlockSpec((TM, nk), lambda i: (i, 0))],
        out_shape=[jax.ShapeDtypeStruct((N, B_H * LANES), BF16),
                   jax.ShapeDtypeStruct((N, nk), F32),
                   jax.ShapeDtypeStruct((N, nk), BF16),
                   jax.ShapeDtypeStruct((N, nk), BF16)],
        compiler_params=_params("arbitrary"),
        name="attn_prep",
    )(z, jnp.tile(q_g, B_H).reshape(1, nq), jnp.tile(k_g, B_HKV).reshape(1, nk), cos, sin,
      _group_sum_matrix(nq, B_HD), _group_sum_matrix(nk, B_HD), _q_pad_matrix())


def _attn_kernel(*refs, has_cache):
    if has_cache:
        q_ref, k_ref, v_ref, ck_ref, cv_ref, o_ref = refs
        ck = ck_ref[...].astype(BF16)
        cv = cv_ref[...].astype(BF16)
    else:
        q_ref, k_ref, v_ref, o_ref = refs
    k = k_ref[...]
    v = v_ref[...]
    g = B_H // B_HKV
    lane = lax.broadcasted_iota(jnp.int32, (q_ref.shape[0], LANES), 1)
    outs = []
    for j in range(B_H // 2):
        pair = []
        for half in range(2):
            h = 2 * j + half
            q = q_ref[:, h * LANES:(h + 1) * LANES]
            s = _dot_nt(q, k)
            m = jnp.max(s, axis=-1, keepdims=True)
            if has_cache:
                sc = _dot_nt(q, ck)
                m = jnp.maximum(m, jnp.max(sc, axis=-1, keepdims=True))
            p = jnp.exp(s - m)
            l = jnp.sum(p, axis=-1, keepdims=True)
            r = _dot(p.astype(BF16), v)
            if has_cache:
                pc = jnp.exp(sc - m)
                l = l + jnp.sum(pc, axis=-1, keepdims=True)
                r = r + _dot(pc.astype(BF16), cv)
            r = r / l
            if h // g != half:
                r = pltpu.roll(r, B_HD, 1)
            pair.append(r)
        outs.append(jnp.where(lane < B_HD, pair[0], pair[1]))
    o_ref[...] = jnp.concatenate(outs, axis=-1).astype(o_ref.dtype)


def _attention(qpad, kr, vb, B, T, cache_k, cache_v):
    has_cache = cache_k is not None
    TQ = Q_TILE
    nq = T // TQ
    nk = B_HKV * B_HD
    in_specs = [pl.BlockSpec((TQ, B_H * LANES), lambda b, i: (b * nq + i, 0)),
                pl.BlockSpec((T, nk), lambda b, i: (b, 0)),
                pl.BlockSpec((T, nk), lambda b, i: (b, 0))]
    args = [qpad, kr, vb]
    if has_cache:
        P = cache_k.shape[1]
        in_specs += [pl.BlockSpec((None, P, nk), lambda b, i: (b, 0, 0))] * 2
        args += [cache_k, cache_v]
    return pl.pallas_call(
        functools.partial(_attn_kernel, has_cache=has_cache),
        grid=(B, nq),
        in_specs=in_specs,
        out_specs=pl.BlockSpec((TQ, B_H * B_HD), lambda b, i: (b * nq + i, 0)),
        out_shape=jax.ShapeDtypeStruct((B * T, B_H * B_HD), BF16),
        compiler_params=_params("arbitrary", "arbitrary"),
        name="attention",
    )(*args)


def _proj_res_kernel(*refs, n_in):
    x_ref, gate_ref = refs[0], refs[1]
    o_refs = refs[2:2 + n_in]
    w_refs = refs[2 + n_in:2 + 2 * n_in]
    out_ref = refs[2 + 2 * n_in]
    wbf_refs = refs[3 + 2 * n_in:]

    @pl.when(pl.program_id(0) == 0)
    def _():
        for w_ref, wbf_ref in zip(w_refs, wbf_refs):
            wbf_ref[...] = w_ref[...].astype(BF16)

    acc = _dot(o_refs[0][...], wbf_refs[0][...])
    for o_ref, wbf_ref in zip(o_refs[1:], wbf_refs[1:]):
        acc = acc + _dot(o_ref[...], wbf_ref[...])
    out_ref[...] = x_ref[...] + gate_ref[...] * acc


def _proj_res(x, mod, part, acts, w, rows_per_group):
    N, D = x.shape
    TM = min(512, rows_per_group)
    n_in = len(acts)
    widths = [a.shape[1] for a in acts]
    offs = np.cumsum([0] + widths[:-1]).tolist()
    in_specs = [pl.BlockSpec((TM, D), lambda i: (i, 0)),
                _mod_spec(part, D, TM, rows_per_group, 0)]
    in_specs += [pl.BlockSpec((TM, wd), lambda i: (i, 0)) for wd in widths]
    in_specs += [pl.BlockSpec((wd, D), functools.partial(lambda i, blk: (blk, 0), blk=off // wd))
                 for wd, off in zip(widths, offs)]
    return pl.pallas_call(
        functools.partial(_proj_res_kernel, n_in=n_in),
        grid=(N // TM,),
        in_specs=in_specs,
        out_specs=pl.BlockSpec((TM, D), lambda i: (i, 0)),
        out_shape=jax.ShapeDtypeStruct((N, D), F32),
        scratch_shapes=[pltpu.VMEM((wd, D), BF16) for wd in widths],
        compiler_params=_params("arbitrary"),
        name="proj_residual",
    )(x, mod, *acts, *([w] * n_in))


def _ffn_kernel(x_ref, g_ref, sh_ref, sc_ref, gate_ref, w1_ref, w3_ref, w2_ref, out_ref, h_scr, acc_scr, *, nf):
    f = pl.program_id(1)

    @pl.when(f == 0)
    def _():
        h_scr[...] = _norm_mod(x_ref[...], g_ref[...], sh_ref[...], sc_ref[...]).astype(BF16)
        acc_scr[...] = jnp.zeros_like(acc_scr)

    h = h_scr[...]
    a = _dot(h, w1_ref[...].astype(BF16))
    b = _dot(h, w3_ref[...].astype(BF16))
    acc_scr[...] += _dot((_silu(a) * b).astype(BF16), w2_ref[...].astype(BF16))

    @pl.when(f == nf - 1)
    def _():
        out_ref[...] = x_ref[...] + gate_ref[...] * acc_scr[...]


def _ffn(x, g, mod, w1, w3, w2, rows_per_group):
    N, D = x.shape
    FF = w1.shape[1]
    TM, TF = min(1024, rows_per_group), 256
    nf = FF // TF
    return pl.pallas_call(
        functools.partial(_ffn_kernel, nf=nf),
        grid=(N // TM, nf),
        in_specs=[pl.BlockSpec((TM, D), lambda i, f: (i, 0)),
                  pl.BlockSpec((1, D), lambda i, f: (0, 0)),
                  _mod_spec(3, D, TM, rows_per_group, 0),
                  _mod_spec(4, D, TM, rows_per_group, 0),
                  _mod_spec(5, D, TM, rows_per_group, 0),
                  pl.BlockSpec((D, TF), lambda i, f: (0, f)),
                  pl.BlockSpec((D, TF), lambda i, f: (0, f)),
                  pl.BlockSpec((TF, D), lambda i, f: (f, 0))],
        out_specs=pl.BlockSpec((TM, D), lambda i, f: (i, 0)),
        out_shape=jax.ShapeDtypeStruct((N, D), F32),
        scratch_shapes=[pltpu.VMEM((TM, D), BF16), pltpu.VMEM((TM, D), F32)],
        compiler_params=_params("arbitrary", "arbitrary"),
        name="ffn",
    )(x, g.reshape(1, D), mod, mod, mod, w1, w3, w2)


MOE_SB = 512
MOE_TRG = 256
MOE_TR = 1024


def _two_stream_specs(shape, ntp, ax=0):
    def idx_p(*g):
        return (jnp.minimum(g[ax], ntp - 1), 0)

    def idx_s(*g):
        return (jnp.maximum(g[ax] - ntp, 0), 0)
    return pl.BlockSpec(shape, idx_p), pl.BlockSpec(shape, idx_s)


def _pool_mod_spec(part, D, TM, ntp, rows_per_group):
    def idx(i, *_):
        return (jnp.where(i < ntp, 0, 1 + ((i - ntp) * TM) // rows_per_group), part, 0, 0)
    return pl.BlockSpec((None, None, 1, D), idx)


def _route_kernel(xp_ref, xs_ref, g_ref, sh_ref, sc_ref, rw_ref, tri_ref, h_ref, info_ref, cum_ref, carry_scr,
                  *, ntp):
    i = pl.program_id(0)

    @pl.when(i == 0)
    def _():
        carry_scr[...] = jnp.zeros_like(carry_scr)

    x = jnp.where(i < ntp, xp_ref[...], xs_ref[...])
    h = _norm_mod(x, g_ref[...], sh_ref[...], sc_ref[...])
    h_ref[...] = h.astype(BF16)
    lane = lax.broadcasted_iota(jnp.int32, info_ref.shape, 1).astype(F32)
    logits = jnp.dot(h, rw_ref[...], precision=HIGHEST, preferred_element_type=F32)
    logits = jnp.where(lane < N_EXPERTS, logits, -jnp.inf)
    m1 = jnp.max(logits, axis=-1, keepdims=True)
    i1 = jnp.min(jnp.where(logits == m1, lane, float(LANES)), axis=-1, keepdims=True)
    rest = jnp.where(lane == i1, -jnp.inf, logits)
    m2 = jnp.max(rest, axis=-1, keepdims=True)
    i2 = jnp.min(jnp.where(rest == m2, lane, float(LANES)), axis=-1, keepdims=True)
    e2 = jnp.exp(m2 - m1)
    w1 = 1.0 / (1.0 + e2)
    w2 = e2 / (1.0 + e2)
    ind = jnp.where(jnp.logical_or(lane == i1, lane == i2), 1.0, 0.0)
    before = _dot(tri_ref[...], ind.astype(BF16)) + carry_scr[...]
    r1 = jnp.sum(jnp.where(lane == i1, before, 0.0), axis=-1, keepdims=True)
    r2 = jnp.sum(jnp.where(lane == i2, before, 0.0), axis=-1, keepdims=True)
    total = carry_scr[...] + jnp.sum(ind, axis=0, keepdims=True)
    carry_scr[...] = total
    cum_ref[...] = total
    info = jnp.where(lane == 0.0, i1, jnp.where(lane == 1.0, i2, jnp.where(lane == 2.0, w1, jnp.where(
        lane == 3.0, w2, jnp.where(lane == 4.0, r1, jnp.where(lane == 5.0, r2, 0.0))))))
    info_ref[...] = info


def _moe_route(xp, xs, g, mod, router_w, rows_per_group):
    Np, D = xp.shape
    N = Np + xs.shape[0]
    TM = MOE_SB
    ntp = Np // TM
    nt = N // TM
    rw = jnp.pad(router_w, ((0, 0), (0, LANES - router_w.shape[1])))
    tri = jnp.asarray(np.tril(np.ones((TM, TM), np.float32), -1), dtype=BF16)
    xp_spec, xs_spec = _two_stream_specs((TM, D), ntp)
    return pl.pallas_call(
        functools.partial(_route_kernel, ntp=ntp),
        grid=(nt,),
        in_specs=[xp_spec, xs_spec,
                  pl.BlockSpec((1, D), lambda i: (0, 0)),
                  _pool_mod_spec(3, D, TM, ntp, rows_per_group),
                  _pool_mod_spec(4, D, TM, ntp, rows_per_group),
                  pl.BlockSpec((D, LANES), lambda i: (0, 0)),
                  pl.BlockSpec((TM, TM), lambda i: (0, 0))],
        out_specs=[pl.BlockSpec((TM, D), lambda i: (i, 0)),
                   pl.BlockSpec((TM, LANES), lambda i: (i, 0)),
                   pl.BlockSpec((None, 1, LANES), lambda i: (i, 0, 0))],
        out_shape=[jax.ShapeDtypeStruct((N, D), BF16),
                   jax.ShapeDtypeStruct((N, LANES), F32),
                   jax.ShapeDtypeStruct((nt, 1, LANES), F32)],
        scratch_shapes=[pltpu.VMEM((1, LANES), F32)],
        compiler_params=_params("arbitrary"),
        name="moe_route",
    )(xp, xs, g.reshape(1, D), mod, mod, rw, tri)


def _moe_plan(info, cum, N):
    E, SB, TRG, TR = N_EXPERTS, MOE_SB, MOE_TRG, MOE_TR
    NB = N // SB
    rmax = 2 * N + E * TR
    RG, RT = rmax // TRG, rmax // TR
    PMAX = RG + E * NB
    i32 = jnp.int32
    cum_e = cum[:, 0, :E].astype(i32).T
    cnt = cum_e[:, -1]
    tiles = (cnt + TR - 1) // TR
    start = TR * (jnp.cumsum(tiles) - tiles)

    e1, e2 = info[:, 0].astype(i32), info[:, 1].astype(i32)
    pos1 = start[e1] + info[:, 4].astype(i32)
    pos2 = start[e2] + info[:, 5].astype(i32)
    posf = jnp.stack([pos1, pos2], axis=1).astype(F32)
    pos_cols = jnp.concatenate([posf, info[:, 2:4], jnp.zeros((N, 4), F32)], axis=1)
    pos_rows = jnp.concatenate([posf.T, jnp.zeros((6, N), F32)], axis=0)

    def region(row0):
        e = jnp.clip(jnp.sum(row0[:, None] >= start[None, :], axis=1) - 1, 0, E - 1)
        return e, row0 - start[e]

    eq, lo = region(jnp.arange(RG, dtype=i32) * TRG)
    hi = jnp.minimum(lo + TRG, cnt[eq])
    first = jnp.sum(cum_e[eq] <= lo[:, None], axis=1)
    last = jnp.sum(cum_e[eq] < hi[:, None], axis=1)
    nblk = jnp.where(hi > lo, last - first + 1, 0)
    pend = jnp.cumsum(nblk)
    npairs = pend[-1]
    p = jnp.arange(PMAX, dtype=i32)
    valid = p < npairs
    pc = jnp.minimum(p, npairs - 1)
    q_of = jnp.minimum(jnp.sum(pend[None, :] <= pc[:, None], axis=1), RG - 1).astype(i32)
    pstart = pend - nblk
    s_of = (first[q_of] + pc - pstart[q_of]).astype(i32)
    g_first = jnp.logical_and(valid, pc == pstart[q_of]).astype(i32)
    gather_plan = (q_of, s_of, valid.astype(i32), g_first)

    order = jnp.argsort(jnp.where(valid, s_of * RG + q_of, jnp.iinfo(jnp.int32).max))
    s2, q2 = s_of[order], q_of[order]
    s2 = jnp.where(valid, s2, s2[npairs - 1])
    q2 = jnp.where(valid, q2, q2[npairs - 1])
    prev = jnp.concatenate([jnp.full((1,), -1, i32), s2[:-1]])
    nxt = jnp.concatenate([s2[1:], jnp.full((1,), -1, i32)])
    c_first = jnp.logical_and(valid, s2 != prev).astype(i32)
    c_last = jnp.logical_and(valid, jnp.logical_or(s2 != nxt, p == npairs - 1)).astype(i32)
    combine_plan = (s2.astype(i32), q2.astype(i32), valid.astype(i32), c_first, c_last)

    te, tlo = region(jnp.arange(RT, dtype=i32) * TR)
    tvalid = jnp.clip(cnt[te] - tlo, 0, TR)
    last_t = jnp.sum(tiles) - 1
    t_idx = jnp.where(tvalid > 0, jnp.arange(RT, dtype=i32), last_t).astype(i32)
    ffn_plan = (t_idx, te[t_idx].astype(i32), tvalid.astype(i32))
    return pos_cols, pos_rows, gather_plan, combine_plan, ffn_plan, rmax


def _moe_gather_kernel(q_ref, s_ref, valid_ref, first_ref, pos_ref, h_ref, out_ref):
    p = pl.program_id(0)

    @pl.when(valid_ref[p] == 1)
    def _():
        rows, toks = out_ref.shape[0], h_ref.shape[0]
        row = (lax.broadcasted_iota(jnp.int32, (rows, 1), 0) + q_ref[p] * rows).astype(F32)
        hit = jnp.logical_or(pos_ref[0:1, :] == row, pos_ref[1:2, :] == row)
        sel = jnp.where(hit, 1.0, 0.0).astype(BF16)
        part = _dot(sel, h_ref[...]).astype(BF16)

        @pl.when(first_ref[p] == 1)
        def _():
            out_ref[...] = part

        @pl.when(first_ref[p] == 0)
        def _():
            out_ref[...] = out_ref[...] + part


def _moe_gather(h, pos_rows, plan, rmax):
    N, D = h.shape
    pmax = plan[0].shape[0]
    return pl.pallas_call(
        _moe_gather_kernel,
        grid_spec=pltpu.PrefetchScalarGridSpec(
            num_scalar_prefetch=4, grid=(pmax,),
            in_specs=[pl.BlockSpec((SUB, MOE_SB), lambda p, q, s, v, f: (0, s[p])),
                      pl.BlockSpec((MOE_SB, D), lambda p, q, s, v, f: (s[p], 0))],
            out_specs=pl.BlockSpec((MOE_TRG, D), lambda p, q, s, v, f: (q[p], 0))),
        out_shape=jax.ShapeDtypeStruct((rmax, D), BF16),
        compiler_params=_params("arbitrary"),
        name="moe_gather",
    )(*plan, pos_rows, h)


def _moe_ffn_kernel(t_ref, e_ref, nv_ref, x_ref, w1_ref, w3_ref, w2_ref, out_ref, acc_scr, w1b, w3b, w2b, *, nf):
    t = pl.program_id(0)
    f = pl.program_id(1)
    nv = nv_ref[t]

    @pl.when(nv > 0)
    def _():
        w1b[...] = w1_ref[...].astype(BF16)
        w3b[...] = w3_ref[...].astype(BF16)
        w2b[...] = w2_ref[...].astype(BF16)

    for sub in range(MOE_TR // MOE_TRG):
        rows = slice(sub * MOE_TRG, (sub + 1) * MOE_TRG)

        @pl.when(sub * MOE_TRG < nv)
        def _():
            x = x_ref[rows, :]
            a = _dot(x, w1b[...])
            b = _dot(x, w3b[...])
            part = _dot((_silu(a) * b).astype(BF16), w2b[...])

            @pl.when(f == 0)
            def _():
                acc_scr[rows, :] = part

            @pl.when(f > 0)
            def _():
                acc_scr[rows, :] += part

            @pl.when(f == nf - 1)
            def _():
                out_ref[rows, :] = acc_scr[rows, :].astype(out_ref.dtype)


def _moe_ffn(xs, plan, w1, w3, w2):
    rmax, D = xs.shape
    FF = w1.shape[2]
    TF = 256
    nf = FF // TF
    RT = rmax // MOE_TR

    def fidx(t, f, nv):
        return jnp.where(nv[t] > 0, f, nf - 1)

    return pl.pallas_call(
        functools.partial(_moe_ffn_kernel, nf=nf),
        grid_spec=pltpu.PrefetchScalarGridSpec(
            num_scalar_prefetch=3, grid=(RT, nf),
            in_specs=[pl.BlockSpec((MOE_TR, D), lambda t, f, ti, e, nv: (ti[t], 0)),
                      pl.BlockSpec((None, D, TF), lambda t, f, ti, e, nv: (e[t], 0, fidx(t, f, nv))),
                      pl.BlockSpec((None, D, TF), lambda t, f, ti, e, nv: (e[t], 0, fidx(t, f, nv))),
                      pl.BlockSpec((None, TF, D), lambda t, f, ti, e, nv: (e[t], fidx(t, f, nv), 0))],
            out_specs=pl.BlockSpec((MOE_TR, D), lambda t, f, ti, e, nv: (ti[t], 0)),
            scratch_shapes=[pltpu.VMEM((MOE_TR, D), F32), pltpu.VMEM((D, TF), BF16), pltpu.VMEM((D, TF), BF16),
                            pltpu.VMEM((TF, D), BF16)]),
        out_shape=jax.ShapeDtypeStruct((rmax, D), BF16),
        compiler_params=_params("arbitrary", "arbitrary"),
        name="moe_ffn",
    )(*plan, xs, w1, w3, w2)


def _moe_combine_kernel(s_ref, q_ref, valid_ref, first_ref, last_ref, pos_ref, ys_ref, xp_ref, xs_ref, gate_ref,
                        fg_ref, op_ref, os_ref, acc_scr, *, ntp):
    p = pl.program_id(0)

    @pl.when(valid_ref[p] == 1)
    def _():
        rows = ys_ref.shape[0]
        col = (lax.broadcasted_iota(jnp.int32, (1, rows), 1) + q_ref[p] * rows).astype(F32)
        sel = (jnp.where(pos_ref[:, 0:1] == col, pos_ref[:, 2:3], 0.0)
               + jnp.where(pos_ref[:, 1:2] == col, pos_ref[:, 3:4], 0.0)).astype(BF16)
        part = _dot(sel, ys_ref[...])

        @pl.when(first_ref[p] == 1)
        def _():
            acc_scr[...] = part

        @pl.when(first_ref[p] == 0)
        def _():
            acc_scr[...] += part

        @pl.when(last_ref[p] == 1)
        def _():
            s = s_ref[p]
            x = jnp.where(s < ntp, xp_ref[...], xs_ref[...])
            y = x + gate_ref[...] * acc_scr[...]
            out = y * lax.rsqrt(jnp.mean(y * y, axis=-1, keepdims=True) + EPS) * fg_ref[...]

            @pl.when(s < ntp)
            def _():
                op_ref[...] = out

            @pl.when(s >= ntp)
            def _():
                os_ref[...] = out


def _moe_combine(ys, pos_cols, plan, xp, xs, mod, final_g, rows_per_group):
    Np, D = xp.shape
    Ns = xs.shape[0]
    SB = MOE_SB
    ntp = Np // SB
    pmax = plan[0].shape[0]

    def blk(p, s, *_):
        return s[p]

    def tok_p(p, s, *_):
        return (jnp.minimum(s[p], ntp - 1), 0)

    def tok_s(p, s, *_):
        return (jnp.maximum(s[p] - ntp, 0), 0)

    def gate_idx(p, s, *_):
        return (jnp.where(s[p] < ntp, 0, 1 + ((s[p] - ntp) * SB) // rows_per_group), 5, 0, 0)

    return pl.pallas_call(
        functools.partial(_moe_combine_kernel, ntp=ntp),
        grid_spec=pltpu.PrefetchScalarGridSpec(
            num_scalar_prefetch=5, grid=(pmax,),
            in_specs=[pl.BlockSpec((SB, SUB), lambda p, s, q, *_: (s[p], 0)),
                      pl.BlockSpec((MOE_TRG, D), lambda p, s, q, *_: (q[p], 0)),
                      pl.BlockSpec((SB, D), tok_p),
                      pl.BlockSpec((SB, D), tok_s),
                      pl.BlockSpec((None, None, 1, D), gate_idx),
                      pl.BlockSpec((1, D), lambda p, *_: (0, 0))],
            out_specs=[pl.BlockSpec((SB, D), tok_p), pl.BlockSpec((SB, D), tok_s)],
            scratch_shapes=[pltpu.VMEM((SB, D), F32)]),
        out_shape=[jax.ShapeDtypeStruct((Np, D), F32), jax.ShapeDtypeStruct((Ns, D), F32)],
        compiler_params=_params("arbitrary"),
        name="moe_combine",
    )(*plan, pos_cols, ys, xp, xs, mod, final_g.reshape(1, D))


def _moe(xp, xs, g, mod, router_w, w1, w3, w2, final_g, rows_per_group):
    N = xp.shape[0] + xs.shape[0]
    h, info, cum = _moe_route(xp, xs, g, mod, router_w, rows_per_group)
    pos_cols, pos_rows, gather_plan, combine_plan, ffn_plan, rmax = _moe_plan(info, cum, N)
    x_sorted = _moe_gather(h, pos_rows, gather_plan, rmax)
    y_sorted = _moe_ffn(x_sorted, ffn_plan, w1, w3, w2)
    return _moe_combine(y_sorted, pos_cols, combine_plan, xp, xs, mod, final_g, rows_per_group)


def _gla_levels(C):
    lv, c = [], C // 2
    while c >= SUB:
        lv.append(c)
        c //= 2
    return lv


def _gla_tables(C):
    levels = _gla_levels(C)
    nr = 2 + 2 * len(levels)
    mat = np.zeros((2, nr * C, C), np.float32)
    code = np.zeros((2, C, C), np.int32)
    for d in range(2):
        p = np.arange(C) if d == 0 else C - 1 - np.arange(C)
        pi, pj = p[:, None], p[None, :]
        mat[d, 0:C] = pj <= pi
        mat[d, C:2 * C] = pj > pi
        code[d] = np.where((pj <= pi) & (pi // SUB == pj // SUB), 1, 0)
        for lv, c in enumerate(levels):
            blk = pi // c
            later = blk % 2 == 1
            mat[d, (2 + 2 * lv) * C:(3 + 2 * lv) * C] = later & (pj > blk * c - 1) & (pj <= pi)
            mat[d, (3 + 2 * lv) * C:(4 + 2 * lv) * C] = (~later) & (pj > pi) & (pj <= (blk + 1) * c - 1)
            pair = (pi // (2 * c) == pj // (2 * c)) & (pi // c != pj // c) & (pj <= pi)
            code[d] = np.where(pair, 2 + lv, code[d])
    ones = np.zeros((SUB * LANES, C), np.float32)
    for jj in range(SUB):
        ones[jj * LANES:(jj + 1) * LANES, jj::SUB] = 1.0
    return jnp.asarray(mat, dtype=BF16), jnp.asarray(code), jnp.asarray(ones, dtype=BF16)


def _bcast_sublane(x, jj):
    r, w = x.shape
    x3 = x.reshape(r // SUB, SUB, w)
    return jnp.broadcast_to(x3[:, jj:jj + 1, :], x3.shape).reshape(r, w)


def _t128(x):
    r, w = x.shape
    if w > LANES:
        return jnp.concatenate([x[:, i:i + LANES].T for i in range(0, w, LANES)], axis=0)
    return jnp.concatenate([x[i:i + LANES, :].T for i in range(0, r, LANES)], axis=1)


def _gla_kernel(q_ref, k_ref, v_ref, g_ref, lr_ref, wg_ref, ba_ref, mat_ref, code_ref, ones_ref,
                s0f_ref, s0b_ref, ng_ref, o_ref, sf_ref, sb_ref, st_scr, of_scr, *, n, C):
    d = pl.program_id(1)
    c = pl.program_id(2)
    levels = _gla_levels(C)

    @pl.when(jnp.logical_and(c == 0, d == 0))
    def _():
        for h in range(C_H):
            st_scr[h] = _t128(s0f_ref[h])

    @pl.when(jnp.logical_and(c == 0, d == 1))
    def _():
        for h in range(C_H):
            st_scr[h] = _t128(s0b_ref[h])

    xg = jnp.dot(lr_ref[...], wg_ref[...], precision=HIGHEST, preferred_element_type=F32) + ba_ref[...]
    la = (jnp.minimum(xg, 0.0) - jnp.log1p(jnp.exp(-jnp.abs(xg)))) * (1.0 / C_TAU)
    hi = la.astype(BF16)
    r1 = la - hi.astype(F32)
    mid = r1.astype(BF16)
    lo = (r1 - mid.astype(F32)).astype(BF16)
    mat = mat_ref[...]
    cum = _dot(mat, hi) + _dot(mat, mid) + _dot(mat, lo)
    code = code_ref[...]
    ones = ones_ref[...]

    outs = []
    for h in range(C_H):
        ks = slice(h * C_DK, (h + 1) * C_DK)
        qh = q_ref[:, ks] * (C_DK ** -0.5)
        kh = k_ref[:, ks]
        vh = v_ref[:, h * C_DV:(h + 1) * C_DV]
        vb = vh.astype(BF16)
        b = cum[0:C, ks]
        b_rest = cum[C:2 * C, ks]
        ps = []
        for jj in range(SUB):
            dec = jnp.exp(jnp.minimum(b - _bcast_sublane(b, jj), 0.0))
            ps.append((qh * _bcast_sublane(kh, jj) * dec).astype(BF16))
        att = jnp.where(code == 1, _dot(jnp.concatenate(ps, axis=1), ones), 0.0)
        for lv in range(len(levels)):
            eq = cum[(2 + 2 * lv) * C:(3 + 2 * lv) * C, ks]
            ek = cum[(3 + 2 * lv) * C:(4 + 2 * lv) * C, ks]
            a_lv = _dot_nt((qh * jnp.exp(eq)).astype(BF16), (kh * jnp.exp(ek)).astype(BF16))
            att = jnp.where(code == 2 + lv, a_lv, att)
        st = st_scr[h]
        o = _dot(att.astype(BF16), vb) + _dot_nt((qh * jnp.exp(b)).astype(BF16), st.astype(BF16))
        b_end = b[0:1, :] + b_rest[0:1, :]
        ke = (kh * jnp.exp(b_rest)).astype(BF16)
        st_scr[h] = jnp.exp(b_end) * st + _dot(_t128(vh).astype(BF16), ke)
        outs.append(o)
    o_all = jnp.concatenate(outs, axis=-1)

    @pl.when(d == 0)
    def _():
        of_scr[c] = o_all

    @pl.when(d == 1)
    def _():
        tot = o_all + of_scr[n - 1 - c]
        res = []
        for h in range(C_H):
            sl = slice(h * C_DV, (h + 1) * C_DV)
            t = tot[:, sl]
            y = t * lax.rsqrt(jnp.mean(t * t, axis=-1, keepdims=True) + EPS) * ng_ref[:, sl]
            res.append(y * _silu(g_ref[:, sl]))
        o_ref[...] = jnp.concatenate(res, axis=-1).astype(o_ref.dtype)

    @pl.when(jnp.logical_and(c == n - 1, d == 0))
    def _():
        for h in range(C_H):
            sf_ref[h] = _t128(st_scr[h])

    @pl.when(jnp.logical_and(c == n - 1, d == 1))
    def _():
        for h in range(C_H):
            sb_ref[h] = _t128(st_scr[h])


def _gla(z, B, T, w_a2, b_a, s0f, s0b, norm_g):
    C = GLA_CHUNK
    n = T // C
    HK = C_H * C_DK
    HV = C_H * C_DV
    mat, code, ones = _gla_tables(C)
    nr = mat.shape[1] // C
    wg = jnp.zeros((2, LANES, HK), F32)
    for dr in range(2):
        wg = wg.at[dr, dr * C_RANK:(dr + 1) * C_RANK, :].set(w_a2[dr])

    def row(b, d, c):
        return b * n + c + d * (n - 1 - 2 * c)

    st_spec = pl.BlockSpec((None, C_H, C_DK, C_DV), lambda b, d, c: (b, 0, 0, 0))
    st_shape = jax.ShapeDtypeStruct((B, C_H, C_DK, C_DV), F32)
    return pl.pallas_call(
        functools.partial(_gla_kernel, n=n, C=C),
        grid=(B, 2, n),
        in_specs=[pl.BlockSpec((C, HK), lambda b, d, c: (row(b, d, c), 0)),
                  pl.BlockSpec((C, HK), lambda b, d, c: (row(b, d, c), 1)),
                  pl.BlockSpec((C, HV), lambda b, d, c: (row(b, d, c), 1)),
                  pl.BlockSpec((C, HV), lambda b, d, c: (row(b, d, c), 2)),
                  pl.BlockSpec((C, LANES), lambda b, d, c: (row(b, d, c), (2 * HK + 2 * HV) // LANES)),
                  pl.BlockSpec((None, LANES, HK), lambda b, d, c: (d, 0, 0)),
                  pl.BlockSpec((None, 1, HK), lambda b, d, c: (d, 0, 0)),
                  pl.BlockSpec((None, nr * C, C), lambda b, d, c: (d, 0, 0)),
                  pl.BlockSpec((None, C, C), lambda b, d, c: (d, 0, 0)),
                  pl.BlockSpec((SUB * LANES, C), lambda b, d, c: (0, 0)),
                  st_spec, st_spec,
                  pl.BlockSpec((1, HV), lambda b, d, c: (0, 0))],
        out_specs=[pl.BlockSpec((C, HV), lambda b, d, c: (b * n + (n - 1) - d * c, 0)),
                   st_spec, st_spec],
        out_shape=[jax.ShapeDtypeStruct((B * T, HV), BF16), st_shape, st_shape],
        scratch_shapes=[pltpu.VMEM((C_H, C_DV, C_DK), F32), pltpu.VMEM((n, C, HV), F32)],
        compiler_params=_params("arbitrary", "arbitrary", "arbitrary"),
        name="gla",
    )(z, z, z, z, z, wg, b_a.reshape(2, 1, HK), mat, code, ones, s0f, s0b, norm_g.reshape(1, HV))


def _run_stream(x, B, T, mods, ctx, p):
    N, D = x.shape
    rpg = N // mods[0].shape[0]
    TM = min(1024, rpg)

    z = _norm_mm(x, p['norm1_g'][0], mods[0], (0, 1), p['even_w_in'][0], 1280, TM, rpg)
    if ctx is None:
        s0 = jnp.zeros((B, A_H, A_DK, A_DV), F32)
        a_f0, a_b0, cache_k, cache_v = s0, s0, None, None
    else:
        cache_k, cache_v, a_f0, a_b0 = ctx[0], ctx[1], ctx[2], ctx[3]
    o_a, a_sf, a_sb = _retention(z, B, T, p['a_log_gamma'][0], a_f0, a_b0, p['a_norm_g'][0])
    qpad, k_norm, k_rot, v_bf = _bprep(z, T, p['b_q_g'][0], p['b_k_g'][0], rope=ctx is not None)
    o_b = _attention(qpad, k_rot, v_bf, B, T, cache_k, cache_v)
    x = _proj_res(x, mods[0], 2, [o_a, o_b], p['even_w_out'][0], rpg)
    x = _ffn(x, p['norm2_g'][0], mods[0], p['ff_w1'][0], p['ff_w3'][0], p['ff_w2'][0], rpg)

    w_in = p['odd_w_in'][0]
    pad = (-w_in.shape[1]) % 640
    z1 = _norm_mm(x, p['norm1_g'][1], mods[1], (0, 1), jnp.pad(w_in, ((0, 0), (0, pad))), 640, TM, rpg)
    if ctx is None:
        s0 = jnp.zeros((B, C_H, C_DK, C_DV), F32)
        c_f0, c_b0 = s0, s0
    else:
        c_f0, c_b0 = ctx[4], ctx[5]
    o_c, c_sf, c_sb = _gla(z1, B, T, p['c_w_a2'][0], p['c_b_a'][0], c_f0, c_b0, p['c_norm_g'][0])
    x = _proj_res(x, mods[1], 2, [o_c], p['odd_w_out'][0], rpg)
    v_raw = z[:, EVEN_V_OFF:EVEN_V_OFF + B_HKV * B_HD]
    return x, (k_norm, v_raw, a_sf, a_sb, c_sf, c_sb)


EVEN_V_OFF = A_H * (2 * A_DK + 2 * A_DV) + (B_H + B_HKV) * B_HD


def kernel(x_prompt, x_sample, c, cache_b_k, cache_b_v, state_a_fwd, state_a_bwd, state_c_fwd, state_c_bwd,
           c_ctx, w_mod, b_mod, norm1_g, norm2_g, final_g, even_w_in, even_w_out, a_log_gamma, a_norm_g,
           b_q_g, b_k_g, odd_w_in, c_w_a2, c_b_a, c_norm_g, odd_w_out, ff_w1, ff_w3, ff_w2,
           router_w, moe_w1, moe_w3, moe_w2):
    Bp, Tp, D = x_prompt.shape
    Bs, Ts, _ = x_sample.shape
    L = w_mod.shape[0]
    assert L == 2 and even_w_in.shape[0] == 1 and odd_w_in.shape[0] == 1
    p = dict(norm1_g=norm1_g, norm2_g=norm2_g, final_g=final_g, even_w_in=even_w_in, even_w_out=even_w_out,
             a_log_gamma=a_log_gamma, a_norm_g=a_norm_g, b_q_g=b_q_g, b_k_g=b_k_g, odd_w_in=odd_w_in,
             c_w_a2=c_w_a2, c_b_a=c_b_a, c_norm_g=c_norm_g, odd_w_out=odd_w_out, ff_w1=ff_w1, ff_w3=ff_w3,
             ff_w2=ff_w2, router_w=router_w, moe_w1=moe_w1, moe_w3=moe_w3, moe_w2=moe_w2)

    rows = 8
    conds = jnp.concatenate([c_ctx[None, :], c, jnp.zeros((rows - 1 - Bs, D), F32)], axis=0)
    mod = _modulation(conds, w_mod, b_mod).reshape(L, rows, 6, 1, D)
    mods_p = [mod[l, 0:1] for l in range(L)]
    mods_s = [mod[l, 1:1 + Bs] for l in range(L)]

    x_p, kept = _run_stream(x_prompt.reshape(Bp * Tp, D), Bp, Tp, mods_p, None, p)
    nk = B_HKV * B_HD
    ctx = (cache_b_k[:, 0].reshape(Bs, -1, nk), cache_b_v[:, 0].reshape(Bs, -1, nk),
           state_a_fwd[:, 0], state_a_bwd[:, 0], state_c_fwd[:, 0], state_c_bwd[:, 0])
    x_s, _ = _run_stream(x_sample.reshape(Bs * Ts, D), Bs, Ts, mods_s, ctx, p)
    y_p, y_s = _moe(x_p, x_s, norm2_g[1], mod[1, 0:1 + Bs], router_w[0], moe_w1[0], moe_w3[0], moe_w2[0],
                    final_g, Ts)

    k_norm, v_raw, a_sf, a_sb, c_sf, c_sb = kept
    return (y_p.reshape(Bp, Tp, D), y_s.reshape(Bs, Ts, D),
            k_norm.reshape(Bp, 1, Tp, B_HKV, B_HD), v_raw.reshape(Bp, 1, Tp, B_HKV, B_HD),
            a_sf[:, None], a_sb[:, None], c_sf[:, None], c_sb[:, None])
```

```python
import functools

import numpy as np
import jax
import jax.numpy as jnp
from jax import lax
from jax.experimental import pallas as pl
from jax.experimental.pallas import tpu as pltpu

F32 = jnp.float32
BF16 = jnp.bfloat16
EPS = 1e-6
HIGHEST = lax.Precision.HIGHEST
LOG2E = 1.4426950408889634

VMEM_LIMIT_BYTES = 56 * 1024 * 1024

A_H, A_DK, A_DV = 4, 128, 256
B_H, B_HKV, B_HD = 8, 2, 64
C_H, C_DK, C_DV, C_RANK = 4, 128, 256, 16
C_TAU = 16.0
GRID_W = 64
ROPE_THETA = 10000.0
N_EXPERTS = 8
LANES = 128
SUB = 8
RET_CHUNK = 128
GLA_CHUNK = 128
Q_TILE = 128


def _params(*sem):
    return pltpu.CompilerParams(dimension_semantics=sem, vmem_limit_bytes=VMEM_LIMIT_BYTES)


def _dot(a, b):
    return jnp.dot(a, b, preferred_element_type=F32)


def _dot_nt(a, b):
    return lax.dot_general(a, b, (((1,), (1,)), ((), ())), preferred_element_type=F32)


def _silu(x):
    return x * jax.nn.sigmoid(x)


def _norm_mod(x, g, sh, sc):
    r = lax.rsqrt(jnp.mean(x * x, axis=-1, keepdims=True) + EPS)
    return (x * r * g) * (1.0 + sc) + sh


def _mod_kernel(c_ref, w_ref, b_ref, o_ref):
    c = c_ref[...]
    o_ref[...] = jnp.dot(_silu(c), w_ref[...], precision=HIGHEST, preferred_element_type=F32) + b_ref[...]


def _modulation(conds, w_mod, b_mod):
    L, D, D6 = w_mod.shape
    R = conds.shape[0]
    TN = 1024
    return pl.pallas_call(
        _mod_kernel,
        grid=(L, D6 // TN),
        in_specs=[pl.BlockSpec((R, D), lambda l, j: (0, 0)),
                  pl.BlockSpec((None, D, TN), lambda l, j: (l, 0, j)),
                  pl.BlockSpec((None, 1, TN), lambda l, j: (l, 0, j))],
        out_specs=pl.BlockSpec((None, R, TN), lambda l, j: (l, 0, j)),
        out_shape=jax.ShapeDtypeStruct((L, R, D6), F32),
        compiler_params=_params("arbitrary", "arbitrary"),
        name="modulation",
    )(conds, w_mod, b_mod.reshape(L, 1, D6))


def _mod_spec(part, D, TM, rows_per_group, axis):
    def idx(*g):
        return ((g[axis] * TM) // rows_per_group, part, 0, 0)
    return pl.BlockSpec((None, None, 1, D), idx)


def _norm_mm_kernel(x_ref, g_ref, sh_ref, sc_ref, w_ref, o_ref, wbf_ref):
    @pl.when(pl.program_id(1) == 0)
    def _():
        wbf_ref[...] = w_ref[...].astype(BF16)

    h = _norm_mod(x_ref[...], g_ref[...], sh_ref[...], sc_ref[...])
    o_ref[...] = _dot(h.astype(BF16), wbf_ref[...]).astype(o_ref.dtype)


def _norm_mm(x, g, mod, parts, w, TN, TM, rows_per_group):
    N, D = x.shape
    NO = w.shape[1]
    return pl.pallas_call(
        _norm_mm_kernel,
        grid=(NO // TN, N // TM),
        in_specs=[pl.BlockSpec((TM, D), lambda j, i: (i, 0)),
                  pl.BlockSpec((1, D), lambda j, i: (0, 0)),
                  _mod_spec(parts[0], D, TM, rows_per_group, 1),
                  _mod_spec(parts[1], D, TM, rows_per_group, 1),
                  pl.BlockSpec((D, TN), lambda j, i: (0, j))],
        out_specs=pl.BlockSpec((TM, TN), lambda j, i: (i, j)),
        out_shape=jax.ShapeDtypeStruct((N, NO), F32),
        scratch_shapes=[pltpu.VMEM((D, TN), BF16)],
        compiler_params=_params("arbitrary", "arbitrary"),
        name="norm_mm",
    )(x, g.reshape(1, D), mod, mod, w)


def _ret_kernel(lg_ref, q_ref, k_ref, v_ref, ag_ref, s0f_ref, s0b_ref, ng_ref,
                o_ref, sf_ref, sb_ref, s_scr, of_scr, *, n, C):
    d = pl.program_id(1)
    c = pl.program_id(2)

    @pl.when(jnp.logical_and(c == 0, d == 0))
    def _():
        s_scr[...] = s0f_ref[...]

    @pl.when(jnp.logical_and(c == 0, d == 1))
    def _():
        s_scr[...] = s0b_ref[...]

    df = d.astype(F32)
    sgn = 1.0 - 2.0 * df
    ii = lax.broadcasted_iota(jnp.int32, (C, C), 0).astype(F32)
    jj = lax.broadcasted_iota(jnp.int32, (C, C), 1).astype(F32)
    dd = (ii - jj) * sgn
    feeds = dd >= 0.0
    ddc = jnp.maximum(dd, 0.0)
    ri = lax.broadcasted_iota(jnp.int32, (C, 1), 0).astype(F32)
    pos_q = (ri + 1.0) + df * (C - 2.0 * ri - 1.0)
    pos_k = (C - 1.0 - ri) + df * (2.0 * ri - C + 1.0)
    chunk_len = jnp.full((1, A_DV), float(C), F32)

    outs = []
    for h in range(A_H):
        lg = lg_ref[d, h]
        dmask = jnp.where(feeds, jnp.exp(lg * ddc), 0.0)
        qh = q_ref[:, h * A_DK:(h + 1) * A_DK] * (A_DK ** -0.5)
        kh = k_ref[:, h * A_DK:(h + 1) * A_DK]
        vh = v_ref[:, h * A_DV:(h + 1) * A_DV].astype(BF16)
        s = s_scr[h]
        att = _dot_nt(qh.astype(BF16), kh.astype(BF16)) * dmask
        o = _dot(att.astype(BF16), vh) + _dot((qh * jnp.exp(lg * pos_q)).astype(BF16), s.astype(BF16))
        kd = kh * jnp.exp(lg * pos_k)
        s_scr[h] = jnp.exp(lg * chunk_len) * s + _dot(kd.T.astype(BF16), vh)
        outs.append(o)
    o_all = jnp.concatenate(outs, axis=-1)

    @pl.when(d == 0)
    def _():
        of_scr[c] = o_all

    @pl.when(d == 1)
    def _():
        tot = o_all + of_scr[n - 1 - c]
        res = []
        for h in range(A_H):
            sl = slice(h * A_DV, (h + 1) * A_DV)
            t = tot[:, sl]
            dev = t - jnp.mean(t, axis=-1, keepdims=True)
            y = dev * lax.rsqrt(jnp.mean(dev * dev, axis=-1, keepdims=True) + EPS) * ng_ref[:, sl]
            res.append(y * _silu(ag_ref[:, sl]))
        o_ref[...] = jnp.concatenate(res, axis=-1).astype(o_ref.dtype)

    @pl.when(jnp.logical_and(c == n - 1, d == 0))
    def _():
        sf_ref[...] = s_scr[...]

    @pl.when(jnp.logical_and(c == n - 1, d == 1))
    def _():
        sb_ref[...] = s_scr[...]


def _retention(z, B, T, log_gamma, s0f, s0b, norm_g):
    C = RET_CHUNK
    n = T // C
    HV = A_H * A_DV

    def row(b, d, c):
        return b * n + c + d * (n - 1 - 2 * c)

    st_spec = pl.BlockSpec((None, A_H, A_DK, A_DV), lambda b, d, c: (b, 0, 0, 0))
    st_shape = jax.ShapeDtypeStruct((B, A_H, A_DK, A_DV), F32)
    return pl.pallas_call(
        functools.partial(_ret_kernel, n=n, C=C),
        grid=(B, 2, n),
        in_specs=[pl.BlockSpec(memory_space=pltpu.SMEM),
                  pl.BlockSpec((C, 512), lambda b, d, c: (row(b, d, c), 0)),
                  pl.BlockSpec((C, 512), lambda b, d, c: (row(b, d, c), 1)),
                  pl.BlockSpec((C, HV), lambda b, d, c: (row(b, d, c), 1)),
                  pl.BlockSpec((C, HV), lambda b, d, c: (row(b, d, c), 2)),
                  st_spec, st_spec,
                  pl.BlockSpec((1, HV), lambda b, d, c: (0, 0))],
        out_specs=[pl.BlockSpec((C, HV), lambda b, d, c: (b * n + (n - 1) - d * c, 0)),
                   st_spec, st_spec],
        out_shape=[jax.ShapeDtypeStruct((B * T, HV), BF16), st_shape, st_shape],
        scratch_shapes=[pltpu.VMEM((A_H, A_DK, A_DV), F32), pltpu.VMEM((n, C, HV), F32)],
        compiler_params=_params("arbitrary", "arbitrary", "arbitrary"),
        name="retention",
    )(log_gamma, z, z, z, z, s0f, s0b, norm_g.reshape(1, HV))


def _group_sum_matrix(width, group):
    i = np.arange(width)
    return jnp.asarray((i[:, None] // group == i[None, :] // group).astype(np.float32), dtype=BF16)


def _q_pad_matrix():
    m = np.zeros((B_H * B_HD, B_H * LANES), np.float32)
    g = B_H // B_HKV
    for h in range(B_H):
        for t in range(B_HD):
            m[h * B_HD + t, h * LANES + (h // g) * B_HD + t] = 1.0
    return jnp.asarray(m, dtype=BF16)


def _rope_tables(T):
    rows = T // GRID_W
    row = np.repeat(np.arange(rows, dtype=np.float64), GRID_W)
    col = np.tile(np.arange(GRID_W, dtype=np.float64), rows)
    nq = B_HD // 4
    inv = ROPE_THETA ** (-np.arange(nq, dtype=np.float64) / nq)
    ang = np.concatenate([row[:, None] * inv, col[:, None] * inv], axis=-1)
    cos = np.repeat(np.cos(ang), 2, axis=-1)
    sin = np.repeat(np.sin(ang), 2, axis=-1)
    sign = np.tile(np.array([-1.0, 1.0]), B_HD // 2)
    reps = LANES // B_HD
    return (jnp.asarray(np.tile(cos, (1, reps)), dtype=F32),
            jnp.asarray(np.tile(sin * sign, (1, reps)), dtype=F32))


def _group_rmsnorm(x, gsum, g):
    x2 = x * x
    hi = x2.astype(BF16)
    lo = (x2 - hi.astype(F32)).astype(BF16)
    ss = _dot(hi, gsum) + _dot(lo, gsum)
    return x * lax.rsqrt(ss * (1.0 / B_HD) + EPS) * g


def _rotate_pairs(x, cos, sin_signed):
    n = x.shape[1]
    lane = lax.broadcasted_iota(jnp.int32, x.shape, 1)
    partner = jnp.where(lane % 2 == 0, pltpu.roll(x, n - 1, 1), pltpu.roll(x, 1, 1))
    reps = n // LANES
    if reps > 1:
        cos = jnp.concatenate([cos] * reps, axis=1)
        sin_signed = jnp.concatenate([sin_signed] * reps, axis=1)
    return x * cos + partner * sin_signed


def _bprep_kernel(z_ref, qg_ref, kg_ref, cos_ref, sin_ref, gq_ref, gk_ref, pad_ref,
                  qpad_ref, kn_ref, kr_ref, vb_ref, *, rope):
    nq = B_H * B_HD
    nk = B_HKV * B_HD
    qn = _group_rmsnorm(z_ref[:, 0:nq], gq_ref[...], qg_ref[...])
    kn = _group_rmsnorm(z_ref[:, nq:nq + nk], gk_ref[...], kg_ref[...])
    kn_ref[...] = kn
    if rope:
        qn = _rotate_pairs(qn, cos_ref[...], sin_ref[...])
        kn = _rotate_pairs(kn, cos_ref[...], sin_ref[...])
    kr_ref[...] = kn.astype(BF16)
    vb_ref[...] = z_ref[:, nq + nk:nq + 2 * nk].astype(BF16)
    qs = (qn * (B_HD ** -0.5 * LOG2E)).astype(BF16)
    qpad_ref[...] = _dot(qs, pad_ref[...]).astype(BF16)


def _bprep(z, T, q_g, k_g, rope):
    N = z.shape[0]
    TM = min(512, T)
    nq = B_H * B_HD
    nk = B_HKV * B_HD
    width = nq + 2 * nk
    col = (A_H * (2 * A_DK + 2 * A_DV)) // width
    cos, sin = _rope_tables(T if rope else TM)
    nt = T // TM if rope else 1
    const = lambda i: (0, 0)
    return pl.pallas_call(
        functools.partial(_bprep_kernel, rope=rope),
        grid=(N // TM,),
        in_specs=[pl.BlockSpec((TM, width), lambda i: (i, col)),
                  pl.BlockSpec((1, nq), const),
                  pl.BlockSpec((1, nk), const),
                  pl.BlockSpec((TM, LANES), lambda i: (i % nt, 0)),
                  pl.BlockSpec((TM, LANES), lambda i: (i % nt, 0)),
                  pl.BlockSpec((nq, nq), const),
                  pl.BlockSpec((nk, nk), const),
                  pl.BlockSpec((nq, B_H * LANES), const)],
        out_specs=[pl.BlockSpec((TM, B_H * LANES), lambda i: (i, 0)),
                   pl.BlockSpec((TM, nk), lambda i: (i, 0)),
                   pl.BlockSpec((TM, nk), lambda i: (i, 0)),
                   pl.BlockSpec((TM, nk), lambda i: (i, 0))],
        out_shape=[jax.ShapeDtypeStruct((N, B_H * LANES), BF16),
                   jax.ShapeDtypeStruct((N, nk), F32),
                   jax.ShapeDtypeStruct((N, nk), BF16),
                   jax.ShapeDtypeStruct((N, nk), BF16)],
        compiler_params=_params("arbitrary"),
        name="attn_prep",
    )(z, jnp.tile(q_g, B_H).reshape(1, nq), jnp.tile(k_g, B_HKV).reshape(1, nk), cos, sin,
      _group_sum_matrix(nq, B_HD), _group_sum_matrix(nk, B_HD), _q_pad_matrix())


def _lane_fold(x, op):
    acc = x[:, 0:LANES]
    for j in range(1, x.shape[1] // LANES):
        acc = op(acc, x[:, j * LANES:(j + 1) * LANES])
    return acc


def _attn_kernel(*refs, has_cache, kc):
    if has_cache:
        q_ref, k_ref, v_ref, ck_ref, cv_ref, o_ref, s_scr, m_scr, l_scr, acc_scr = refs
        ncache = ck_ref.shape[0] // kc
    else:
        q_ref, k_ref, v_ref, o_ref, s_scr, m_scr, l_scr, acc_scr = refs
        ncache = 0
    tq = q_ref.shape[0]
    nlat = k_ref.shape[0] // kc
    q = jnp.concatenate([q_ref[:, h * LANES:(h + 1) * LANES] for h in range(B_H)], axis=0)

    def lat_rows(c):
        return pl.ds(pl.multiple_of(c * kc, kc), kc)

    def scores(c, kblk):
        s = _dot_nt(q, kblk)
        s_scr[c] = s
        m_scr[...] = jnp.maximum(m_scr[...], _lane_fold(s, jnp.maximum))

    m_scr[...] = jnp.full(m_scr.shape, -jnp.inf, F32)
    for c in range(ncache):
        scores(c, ck_ref[c * kc:(c + 1) * kc, :].astype(BF16))

    def pass1(c, carry):
        scores(ncache + c, k_ref[lat_rows(c), :])
        return carry
    lax.fori_loop(0, nlat, pass1, 0)
    m = jnp.max(m_scr[...], axis=-1, keepdims=True)

    def weights(c, vblk):
        p = jnp.exp2(s_scr[c] - m)
        l_scr[...] += _lane_fold(p, jnp.add)
        acc_scr[...] += _dot(p.astype(BF16), vblk)

    l_scr[...] = jnp.zeros_like(l_scr)
    acc_scr[...] = jnp.zeros_like(acc_scr)
    for c in range(ncache):
        weights(c, cv_ref[c * kc:(c + 1) * kc, :].astype(BF16))

    def pass2(c, carry):
        weights(ncache + c, v_ref[lat_rows(c), :])
        return carry
    lax.fori_loop(0, nlat, pass2, 0)

    r_all = acc_scr[...] / jnp.sum(l_scr[...], axis=-1, keepdims=True)
    g = B_H // B_HKV
    lane = lax.broadcasted_iota(jnp.int32, (tq, LANES), 1)
    outs = []
    for j in range(B_H // 2):
        pair = []
        for half in range(2):
            h = 2 * j + half
            r = r_all[h * tq:(h + 1) * tq, :]
            if h // g != half:
                r = pltpu.roll(r, B_HD, 1)
            pair.append(r)
        outs.append(jnp.where(lane < B_HD, pair[0], pair[1]))
    o_ref[...] = jnp.concatenate(outs, axis=-1).astype(o_ref.dtype)


def _attention(qpad, kr, vb, B, T, cache_k, cache_v):
    has_cache = cache_k is not None
    TQ = Q_TILE
    nq = T // TQ
    nk = B_HKV * B_HD
    in_specs = [pl.BlockSpec((TQ, B_H * LANES), lambda b, i: (b * nq + i, 0)),
                pl.BlockSpec((T, nk), lambda b, i: (b, 0)),
                pl.BlockSpec((T, nk), lambda b, i: (b, 0))]
    args = [qpad, kr, vb]
    kc = min(512, T)
    nchunks = T // kc
    if has_cache:
        P = cache_k.shape[1]
        assert P % kc == 0
        nchunks += P // kc
        in_specs += [pl.BlockSpec((None, P, nk), lambda b, i: (b, 0, 0))] * 2
        args += [cache_k, cache_v]
    R = B_H * TQ
    return pl.pallas_call(
        functools.partial(_attn_kernel, has_cache=has_cache, kc=kc),
        grid=(B, nq),
        in_specs=in_specs,
        out_specs=pl.BlockSpec((TQ, B_H * B_HD), lambda b, i: (b * nq + i, 0)),
        out_shape=jax.ShapeDtypeStruct((B * T, B_H * B_HD), BF16),
        scratch_shapes=[pltpu.VMEM((nchunks, R, kc), F32), pltpu.VMEM((R, LANES), F32),
                        pltpu.VMEM((R, LANES), F32), pltpu.VMEM((R, LANES), F32)],
        compiler_params=_params("arbitrary", "arbitrary"),
        name="attention",
    )(*args)


def _proj_res_kernel(*refs, n_in):
    x_ref, gate_ref = refs[0], refs[1]
    o_refs = refs[2:2 + n_in]
    w_refs = refs[2 + n_in:2 + 2 * n_in]
    out_ref = refs[2 + 2 * n_in]
    wbf_refs = refs[3 + 2 * n_in:]

    @pl.when(pl.program_id(0) == 0)
    def _():
        for w_ref, wbf_ref in zip(w_refs, wbf_refs):
            wbf_ref[...] = w_ref[...].astype(BF16)

    acc = _dot(o_refs[0][...], wbf_refs[0][...])
    for o_ref, wbf_ref in zip(o_refs[1:], wbf_refs[1:]):
        acc = acc + _dot(o_ref[...], wbf_ref[...])
    out_ref[...] = x_ref[...] + gate_ref[...] * acc


def _proj_res(x, mod, part, acts, w, rows_per_group):
    N, D = x.shape
    TM = min(512, rows_per_group)
    n_in = len(acts)
    widths = [a.shape[1] for a in acts]
    offs = np.cumsum([0] + widths[:-1]).tolist()
    in_specs = [pl.BlockSpec((TM, D), lambda i: (i, 0)),
                _mod_spec(part, D, TM, rows_per_group, 0)]
    in_specs += [pl.BlockSpec((TM, wd), lambda i: (i, 0)) for wd in widths]
    in_specs += [pl.BlockSpec((wd, D), functools.partial(lambda i, blk: (blk, 0), blk=off // wd))
                 for wd, off in zip(widths, offs)]
    return pl.pallas_call(
        functools.partial(_proj_res_kernel, n_in=n_in),
        grid=(N // TM,),
        in_specs=in_specs,
        out_specs=pl.BlockSpec((TM, D), lambda i: (i, 0)),
        out_shape=jax.ShapeDtypeStruct((N, D), F32),
        scratch_shapes=[pltpu.VMEM((wd, D), BF16) for wd in widths],
        compiler_params=_params("arbitrary"),
        name="proj_residual",
    )(x, mod, *acts, *([w] * n_in))


def _ffn_kernel(x_ref, g_ref, sh_ref, sc_ref, gate_ref, w1_ref, w3_ref, w2_ref, out_ref, h_scr, acc_scr, *, nf):
    f = pl.program_id(1)

    @pl.when(f == 0)
    def _():
        h_scr[...] = _norm_mod(x_ref[...], g_ref[...], sh_ref[...], sc_ref[...]).astype(BF16)
        acc_scr[...] = jnp.zeros_like(acc_scr)

    h = h_scr[...]
    a = _dot(h, w1_ref[...].astype(BF16))
    b = _dot(h, w3_ref[...].astype(BF16))
    acc_scr[...] += _dot((_silu(a) * b).astype(BF16), w2_ref[...].astype(BF16))

    @pl.when(f == nf - 1)
    def _():
        out_ref[...] = x_ref[...] + gate_ref[...] * acc_scr[...]


def _ffn(x, g, mod, w1, w3, w2, rows_per_group):
    N, D = x.shape
    FF = w1.shape[1]
    TM, TF = min(1024, rows_per_group), 256
    nf = FF // TF
    return pl.pallas_call(
        functools.partial(_ffn_kernel, nf=nf),
        grid=(N // TM, nf),
        in_specs=[pl.BlockSpec((TM, D), lambda i, f: (i, 0)),
                  pl.BlockSpec((1, D), lambda i, f: (0, 0)),
                  _mod_spec(3, D, TM, rows_per_group, 0),
                  _mod_spec(4, D, TM, rows_per_group, 0),
                  _mod_spec(5, D, TM, rows_per_group, 0),
                  pl.BlockSpec((D, TF), lambda i, f: (0, f)),
                  pl.BlockSpec((D, TF), lambda i, f: (0, f)),
                  pl.BlockSpec((TF, D), lambda i, f: (f, 0))],
        out_specs=pl.BlockSpec((TM, D), lambda i, f: (i, 0)),
        out_shape=jax.ShapeDtypeStruct((N, D), F32),
        scratch_shapes=[pltpu.VMEM((TM, D), BF16), pltpu.VMEM((TM, D), F32)],
        compiler_params=_params("arbitrary", "arbitrary"),
        name="ffn",
    )(x, g.reshape(1, D), mod, mod, mod, w1, w3, w2)


MOE_SB = 512
MOE_TRG = 256
MOE_TR = 1024


def _two_stream_specs(shape, ntp, ax=0):
    def idx_p(*g):
        return (jnp.minimum(g[ax], ntp - 1), 0)

    def idx_s(*g):
        return (jnp.maximum(g[ax] - ntp, 0), 0)
    return pl.BlockSpec(shape, idx_p), pl.BlockSpec(shape, idx_s)


def _pool_mod_spec(part, D, TM, ntp, rows_per_group):
    def idx(i, *_):
        return (jnp.where(i < ntp, 0, 1 + ((i - ntp) * TM) // rows_per_group), part, 0, 0)
    return pl.BlockSpec((None, None, 1, D), idx)


def _route_kernel(xp_ref, xs_ref, g_ref, sh_ref, sc_ref, rw_ref, tri_ref, h_ref, info_ref, cum_ref, carry_scr,
                  *, ntp):
    i = pl.program_id(0)

    @pl.when(i == 0)
    def _():
        carry_scr[...] = jnp.zeros_like(carry_scr)

    x = jnp.where(i < ntp, xp_ref[...], xs_ref[...])
    h = _norm_mod(x, g_ref[...], sh_ref[...], sc_ref[...])
    h_ref[...] = h.astype(BF16)
    lane = lax.broadcasted_iota(jnp.int32, info_ref.shape, 1).astype(F32)
    logits = jnp.dot(h, rw_ref[...], precision=HIGHEST, preferred_element_type=F32)
    logits = jnp.where(lane < N_EXPERTS, logits, -jnp.inf)
    m1 = jnp.max(logits, axis=-1, keepdims=True)
    i1 = jnp.min(jnp.where(logits == m1, lane, float(LANES)), axis=-1, keepdims=True)
    rest = jnp.where(lane == i1, -jnp.inf, logits)
    m2 = jnp.max(rest, axis=-1, keepdims=True)
    i2 = jnp.min(jnp.where(rest == m2, lane, float(LANES)), axis=-1, keepdims=True)
    e2 = jnp.exp(m2 - m1)
    w1 = 1.0 / (1.0 + e2)
    w2 = e2 / (1.0 + e2)
    ind = jnp.where(jnp.logical_or(lane == i1, lane == i2), 1.0, 0.0)
    before = _dot(tri_ref[...], ind.astype(BF16)) + carry_scr[...]
    r1 = jnp.sum(jnp.where(lane == i1, before, 0.0), axis=-1, keepdims=True)
    r2 = jnp.sum(jnp.where(lane == i2, before, 0.0), axis=-1, keepdims=True)
    total = carry_scr[...] + jnp.sum(ind, axis=0, keepdims=True)
    carry_scr[...] = total
    cum_ref[...] = total
    info = jnp.where(lane == 0.0, i1, jnp.where(lane == 1.0, i2, jnp.where(lane == 2.0, w1, jnp.where(
        lane == 3.0, w2, jnp.where(lane == 4.0, r1, jnp.where(lane == 5.0, r2, 0.0))))))
    info_ref[...] = info


def _moe_route(xp, xs, g, mod, router_w, rows_per_group):
    Np, D = xp.shape
    N = Np + xs.shape[0]
    TM = MOE_SB
    ntp = Np // TM
    nt = N // TM
    rw = jnp.pad(router_w, ((0, 0), (0, LANES - router_w.shape[1])))
    tri = jnp.asarray(np.tril(np.ones((TM, TM), np.float32), -1), dtype=BF16)
    xp_spec, xs_spec = _two_stream_specs((TM, D), ntp)
    return pl.pallas_call(
        functools.partial(_route_kernel, ntp=ntp),
        grid=(nt,),
        in_specs=[xp_spec, xs_spec,
                  pl.BlockSpec((1, D), lambda i: (0, 0)),
                  _pool_mod_spec(3, D, TM, ntp, rows_per_group),
                  _pool_mod_spec(4, D, TM, ntp, rows_per_group),
                  pl.BlockSpec((D, LANES), lambda i: (0, 0)),
                  pl.BlockSpec((TM, TM), lambda i: (0, 0))],
        out_specs=[pl.BlockSpec((TM, D), lambda i: (i, 0)),
                   pl.BlockSpec((TM, LANES), lambda i: (i, 0)),
                   pl.BlockSpec((None, 1, LANES), lambda i: (i, 0, 0))],
        out_shape=[jax.ShapeDtypeStruct((N, D), BF16),
                   jax.ShapeDtypeStruct((N, LANES), F32),
                   jax.ShapeDtypeStruct((nt, 1, LANES), F32)],
        scratch_shapes=[pltpu.VMEM((1, LANES), F32)],
        compiler_params=_params("arbitrary"),
        name="moe_route",
    )(xp, xs, g.reshape(1, D), mod, mod, rw, tri)


def _moe_plan(info, cum, N):
    E, SB, TRG, TR = N_EXPERTS, MOE_SB, MOE_TRG, MOE_TR
    NB = N // SB
    rmax = 2 * N + E * TR
    RG, RT = rmax // TRG, rmax // TR
    PMAX = RG + E * NB
    i32 = jnp.int32
    cum_e = cum[:, 0, :E].astype(i32).T
    cnt = cum_e[:, -1]
    tiles = (cnt + TR - 1) // TR
    start = TR * (jnp.cumsum(tiles) - tiles)

    e1, e2 = info[:, 0].astype(i32), info[:, 1].astype(i32)
    pos1 = start[e1] + info[:, 4].astype(i32)
    pos2 = start[e2] + info[:, 5].astype(i32)
    posf = jnp.stack([pos1, pos2], axis=1).astype(F32)
    pos_cols = jnp.concatenate([posf, info[:, 2:4], jnp.zeros((N, 4), F32)], axis=1)
    pos_rows = jnp.concatenate([posf.T, jnp.zeros((6, N), F32)], axis=0)

    def region(row0):
        e = jnp.clip(jnp.sum(row0[:, None] >= start[None, :], axis=1) - 1, 0, E - 1)
        return e, row0 - start[e]

    eq, lo = region(jnp.arange(RG, dtype=i32) * TRG)
    hi = jnp.minimum(lo + TRG, cnt[eq])
    first = jnp.sum(cum_e[eq] <= lo[:, None], axis=1)
    last = jnp.sum(cum_e[eq] < hi[:, None], axis=1)
    nblk = jnp.where(hi > lo, last - first + 1, 0)
    pend = jnp.cumsum(nblk)
    npairs = pend[-1]
    p = jnp.arange(PMAX, dtype=i32)
    valid = p < npairs
    pc = jnp.minimum(p, npairs - 1)
    q_of = jnp.minimum(jnp.sum(pend[None, :] <= pc[:, None], axis=1), RG - 1).astype(i32)
    pstart = pend - nblk
    s_of = (first[q_of] + pc - pstart[q_of]).astype(i32)
    g_first = jnp.logical_and(valid, pc == pstart[q_of]).astype(i32)
    gather_plan = (q_of, s_of, valid.astype(i32), g_first)

    order = jnp.argsort(jnp.where(valid, s_of * RG + q_of, jnp.iinfo(jnp.int32).max))
    s2, q2 = s_of[order], q_of[order]
    s2 = jnp.where(valid, s2, s2[npairs - 1])
    q2 = jnp.where(valid, q2, q2[npairs - 1])
    prev = jnp.concatenate([jnp.full((1,), -1, i32), s2[:-1]])
    nxt = jnp.concatenate([s2[1:], jnp.full((1,), -1, i32)])
    c_first = jnp.logical_and(valid, s2 != prev).astype(i32)
    c_last = jnp.logical_and(valid, jnp.logical_or(s2 != nxt, p == npairs - 1)).astype(i32)
    combine_plan = (s2.astype(i32), q2.astype(i32), valid.astype(i32), c_first, c_last)

    te, tlo = region(jnp.arange(RT, dtype=i32) * TR)
    tvalid = jnp.clip(cnt[te] - tlo, 0, TR)
    last_t = jnp.sum(tiles) - 1
    t_idx = jnp.where(tvalid > 0, jnp.arange(RT, dtype=i32), last_t).astype(i32)
    ffn_plan = (t_idx, te[t_idx].astype(i32), tvalid.astype(i32))
    return pos_cols, pos_rows, gather_plan, combine_plan, ffn_plan, rmax


def _moe_gather_kernel(q_ref, s_ref, valid_ref, first_ref, pos_ref, h_ref, out_ref):
    p = pl.program_id(0)

    @pl.when(valid_ref[p] == 1)
    def _():
        rows, toks = out_ref.shape[0], h_ref.shape[0]
        row = (lax.broadcasted_iota(jnp.int32, (rows, 1), 0) + q_ref[p] * rows).astype(F32)
        hit = jnp.logical_or(pos_ref[0:1, :] == row, pos_ref[1:2, :] == row)
        sel = jnp.where(hit, 1.0, 0.0).astype(BF16)
        part = _dot(sel, h_ref[...]).astype(BF16)

        @pl.when(first_ref[p] == 1)
        def _():
            out_ref[...] = part

        @pl.when(first_ref[p] == 0)
        def _():
            out_ref[...] = out_ref[...] + part


def _moe_gather(h, pos_rows, plan, rmax):
    N, D = h.shape
    pmax = plan[0].shape[0]
    return pl.pallas_call(
        _moe_gather_kernel,
        grid_spec=pltpu.PrefetchScalarGridSpec(
            num_scalar_prefetch=4, grid=(pmax,),
            in_specs=[pl.BlockSpec((SUB, MOE_SB), lambda p, q, s, v, f: (0, s[p])),
                      pl.BlockSpec((MOE_SB, D), lambda p, q, s, v, f: (s[p], 0))],
            out_specs=pl.BlockSpec((MOE_TRG, D), lambda p, q, s, v, f: (q[p], 0))),
        out_shape=jax.ShapeDtypeStruct((rmax, D), BF16),
        compiler_params=_params("arbitrary"),
        name="moe_gather",
    )(*plan, pos_rows, h)


def _moe_ffn_kernel(t_ref, e_ref, nv_ref, x_ref, w1_ref, w3_ref, w2_ref, out_ref, acc_scr, w1b, w3b, w2b, *, nf):
    t = pl.program_id(0)
    f = pl.program_id(1)
    nv = nv_ref[t]

    @pl.when(nv > 0)
    def _():
        w1b[...] = w1_ref[...].astype(BF16)
        w3b[...] = w3_ref[...].astype(BF16)
        w2b[...] = w2_ref[...].astype(BF16)

    def block(rows):
        x = x_ref[rows, :]
        a = _dot(x, w1b[...])
        b = _dot(x, w3b[...])
        part = _dot((_silu(a) * b).astype(BF16), w2b[...])

        @pl.when(f == 0)
        def _():
            acc_scr[rows, :] = part

        @pl.when(f > 0)
        def _():
            acc_scr[rows, :] += part

        @pl.when(f == nf - 1)
        def _():
            out_ref[rows, :] = acc_scr[rows, :].astype(out_ref.dtype)

    nsub = MOE_TR // MOE_TRG
    full = nv > (nsub - 1) * MOE_TRG

    @pl.when(full)
    def _():
        block(slice(0, MOE_TR))

    for sub in range(nsub - 1):
        @pl.when(jnp.logical_and(jnp.logical_not(full), sub * MOE_TRG < nv))
        def _():
            block(slice(sub * MOE_TRG, (sub + 1) * MOE_TRG))


def _moe_ffn(xs, plan, w1, w3, w2):
    rmax, D = xs.shape
    FF = w1.shape[2]
    TF = 256
    nf = FF // TF
    RT = rmax // MOE_TR

    def fidx(t, f, nv):
        return jnp.where(nv[t] > 0, f, nf - 1)

    return pl.pallas_call(
        functools.partial(_moe_ffn_kernel, nf=nf),
        grid_spec=pltpu.PrefetchScalarGridSpec(
            num_scalar_prefetch=3, grid=(RT, nf),
            in_specs=[pl.BlockSpec((MOE_TR, D), lambda t, f, ti, e, nv: (ti[t], 0)),
                      pl.BlockSpec((None, D, TF), lambda t, f, ti, e, nv: (e[t], 0, fidx(t, f, nv))),
                      pl.BlockSpec((None, D, TF), lambda t, f, ti, e, nv: (e[t], 0, fidx(t, f, nv))),
                      pl.BlockSpec((None, TF, D), lambda t, f, ti, e, nv: (e[t], fidx(t, f, nv), 0))],
            out_specs=pl.BlockSpec((MOE_TR, D), lambda t, f, ti, e, nv: (ti[t], 0)),
            scratch_shapes=[pltpu.VMEM((MOE_TR, D), F32), pltpu.VMEM((D, TF), BF16), pltpu.VMEM((D, TF), BF16),
                            pltpu.VMEM((TF, D), BF16)]),
        out_shape=jax.ShapeDtypeStruct((rmax, D), BF16),
        compiler_params=_params("arbitrary", "arbitrary"),
        name="moe_ffn",
    )(*plan, xs, w1, w3, w2)


def _moe_combine_kernel(s_ref, q_ref, valid_ref, first_ref, last_ref, pos_ref, ys_ref, xp_ref, xs_ref, gate_ref,
                        fg_ref, op_ref, os_ref, acc_scr, *, ntp):
    p = pl.program_id(0)

    @pl.when(valid_ref[p] == 1)
    def _():
        rows = ys_ref.shape[0]
        col = (lax.broadcasted_iota(jnp.int32, (1, rows), 1) + q_ref[p] * rows).astype(F32)
        sel = (jnp.where(pos_ref[:, 0:1] == col, pos_ref[:, 2:3], 0.0)
               + jnp.where(pos_ref[:, 1:2] == col, pos_ref[:, 3:4], 0.0)).astype(BF16)
        part = _dot(sel, ys_ref[...])

        @pl.when(first_ref[p] == 1)
        def _():
            acc_scr[...] = part

        @pl.when(first_ref[p] == 0)
        def _():
            acc_scr[...] += part

        @pl.when(last_ref[p] == 1)
        def _():
            s = s_ref[p]
            x = jnp.where(s < ntp, xp_ref[...], xs_ref[...])
            y = x + gate_ref[...] * acc_scr[...]
            out = y * lax.rsqrt(jnp.mean(y * y, axis=-1, keepdims=True) + EPS) * fg_ref[...]

            @pl.when(s < ntp)
            def _():
                op_ref[...] = out

            @pl.when(s >= ntp)
            def _():
                os_ref[...] = out


def _moe_combine(ys, pos_cols, plan, xp, xs, mod, final_g, rows_per_group):
    Np, D = xp.shape
    Ns = xs.shape[0]
    SB = MOE_SB
    ntp = Np // SB
    pmax = plan[0].shape[0]

    def blk(p, s, *_):
        return s[p]

    def tok_p(p, s, *_):
        return (jnp.minimum(s[p], ntp - 1), 0)

    def tok_s(p, s, *_):
        return (jnp.maximum(s[p] - ntp, 0), 0)

    def gate_idx(p, s, *_):
        return (jnp.where(s[p] < ntp, 0, 1 + ((s[p] - ntp) * SB) // rows_per_group), 5, 0, 0)

    return pl.pallas_call(
        functools.partial(_moe_combine_kernel, ntp=ntp),
        grid_spec=pltpu.PrefetchScalarGridSpec(
            num_scalar_prefetch=5, grid=(pmax,),
            in_specs=[pl.BlockSpec((SB, SUB), lambda p, s, q, *_: (s[p], 0)),
                      pl.BlockSpec((MOE_TRG, D), lambda p, s, q, *_: (q[p], 0)),
                      pl.BlockSpec((SB, D), tok_p),
                      pl.BlockSpec((SB, D), tok_s),
                      pl.BlockSpec((None, None, 1, D), gate_idx),
                      pl.BlockSpec((1, D), lambda p, *_: (0, 0))],
            out_specs=[pl.BlockSpec((SB, D), tok_p), pl.BlockSpec((SB, D), tok_s)],
            scratch_shapes=[pltpu.VMEM((SB, D), F32)]),
        out_shape=[jax.ShapeDtypeStruct((Np, D), F32), jax.ShapeDtypeStruct((Ns, D), F32)],
        compiler_params=_params("arbitrary"),
        name="moe_combine",
    )(*plan, pos_cols, ys, xp, xs, mod, final_g.reshape(1, D))


def _moe(xp, xs, g, mod, router_w, w1, w3, w2, final_g, rows_per_group):
    N = xp.shape[0] + xs.shape[0]
    h, info, cum = _moe_route(xp, xs, g, mod, router_w, rows_per_group)
    pos_cols, pos_rows, gather_plan, combine_plan, ffn_plan, rmax = _moe_plan(info, cum, N)
    x_sorted = _moe_gather(h, pos_rows, gather_plan, rmax)
    y_sorted = _moe_ffn(x_sorted, ffn_plan, w1, w3, w2)
    return _moe_combine(y_sorted, pos_cols, combine_plan, xp, xs, mod, final_g, rows_per_group)


def _gla_levels(C):
    lv, c = [], C // 2
    while c >= SUB:
        lv.append(c)
        c //= 2
    return lv


def _gla_tables(C):
    levels = _gla_levels(C)
    nr = 2 + 2 * len(levels)
    mat = np.zeros((2, nr * C, C), np.float32)
    code = np.zeros((2, C, C), np.int32)
    for d in range(2):
        p = np.arange(C) if d == 0 else C - 1 - np.arange(C)
        pi, pj = p[:, None], p[None, :]
        mat[d, 0:C] = pj <= pi
        mat[d, C:2 * C] = pj > pi
        code[d] = np.where((pj <= pi) & (pi // SUB == pj // SUB), 1, 0)
        for lv, c in enumerate(levels):
            blk = pi // c
            later = blk % 2 == 1
            mat[d, (2 + 2 * lv) * C:(3 + 2 * lv) * C] = later & (pj > blk * c - 1) & (pj <= pi)
            mat[d, (3 + 2 * lv) * C:(4 + 2 * lv) * C] = (~later) & (pj > pi) & (pj <= (blk + 1) * c - 1)
            pair = (pi // (2 * c) == pj // (2 * c)) & (pi // c != pj // c) & (pj <= pi)
            code[d] = np.where(pair, 2 + lv, code[d])
    ones = np.zeros((SUB * LANES, C), np.float32)
    for jj in range(SUB):
        ones[jj * LANES:(jj + 1) * LANES, jj::SUB] = 1.0
    return jnp.asarray(mat, dtype=BF16), jnp.asarray(code), jnp.asarray(ones, dtype=BF16)


def _bcast_sublane(x, jj):
    r, w = x.shape
    x3 = x.reshape(r // SUB, SUB, w)
    return jnp.broadcast_to(x3[:, jj:jj + 1, :], x3.shape).reshape(r, w)


def _t128(x):
    r, w = x.shape
    if w > LANES:
        return jnp.concatenate([x[:, i:i + LANES].T for i in range(0, w, LANES)], axis=0)
    return jnp.concatenate([x[i:i + LANES, :].T for i in range(0, r, LANES)], axis=1)


def _gla_kernel(q_ref, k_ref, v_ref, g_ref, lr_ref, wg_ref, ba_ref, mat_ref, code_ref, ones_ref,
                s0f_ref, s0b_ref, ng_ref, o_ref, sf_ref, sb_ref, st_scr, of_scr, *, n, C):
    d = pl.program_id(1)
    c = pl.program_id(2)
    levels = _gla_levels(C)

    @pl.when(jnp.logical_and(c == 0, d == 0))
    def _():
        for h in range(C_H):
            st_scr[h] = _t128(s0f_ref[h])

    @pl.when(jnp.logical_and(c == 0, d == 1))
    def _():
        for h in range(C_H):
            st_scr[h] = _t128(s0b_ref[h])

    xg = jnp.dot(lr_ref[...], wg_ref[...], precision=HIGHEST, preferred_element_type=F32) + ba_ref[...]
    la = (jnp.minimum(xg, 0.0) - jnp.log1p(jnp.exp(-jnp.abs(xg)))) * (1.0 / C_TAU)
    hi = la.astype(BF16)
    r1 = la - hi.astype(F32)
    mid = r1.astype(BF16)
    lo = (r1 - mid.astype(F32)).astype(BF16)
    mat = mat_ref[...]
    cum = _dot(mat, hi) + _dot(mat, mid) + _dot(mat, lo)
    code = code_ref[...]
    ones = ones_ref[...]

    outs = []
    for h in range(C_H):
        ks = slice(h * C_DK, (h + 1) * C_DK)
        qh = q_ref[:, ks] * (C_DK ** -0.5)
        kh = k_ref[:, ks]
        vh = v_ref[:, h * C_DV:(h + 1) * C_DV]
        vb = vh.astype(BF16)
        b = cum[0:C, ks]
        b_rest = cum[C:2 * C, ks]
        ps = []
        for jj in range(SUB):
            dec = jnp.exp(jnp.minimum(b - _bcast_sublane(b, jj), 0.0))
            ps.append((qh * _bcast_sublane(kh, jj) * dec).astype(BF16))
        att = jnp.where(code == 1, _dot(jnp.concatenate(ps, axis=1), ones), 0.0)
        for lv in range(len(levels)):
            eq = cum[(2 + 2 * lv) * C:(3 + 2 * lv) * C, ks]
            ek = cum[(3 + 2 * lv) * C:(4 + 2 * lv) * C, ks]
            a_lv = _dot_nt((qh * jnp.exp(eq)).astype(BF16), (kh * jnp.exp(ek)).astype(BF16))
            att = jnp.where(code == 2 + lv, a_lv, att)
        st = st_scr[h]
        o = _dot(att.astype(BF16), vb) + _dot_nt((qh * jnp.exp(b)).astype(BF16), st.astype(BF16))
        b_end = b[0:1, :] + b_rest[0:1, :]
        ke = (kh * jnp.exp(b_rest)).astype(BF16)
        st_scr[h] = jnp.exp(b_end) * st + _dot(_t128(vh).astype(BF16), ke)
        outs.append(o)
    o_all = jnp.concatenate(outs, axis=-1)

    @pl.when(d == 0)
    def _():
        of_scr[c] = o_all

    @pl.when(d == 1)
    def _():
        tot = o_all + of_scr[n - 1 - c]
        res = []
        for h in range(C_H):
            sl = slice(h * C_DV, (h + 1) * C_DV)
            t = tot[:, sl]
            y = t * lax.rsqrt(jnp.mean(t * t, axis=-1, keepdims=True) + EPS) * ng_ref[:, sl]
            res.append(y * _silu(g_ref[:, sl]))
        o_ref[...] = jnp.concatenate(res, axis=-1).astype(o_ref.dtype)

    @pl.when(jnp.logical_and(c == n - 1, d == 0))
    def _():
        for h in range(C_H):
            sf_ref[h] = _t128(st_scr[h])

    @pl.when(jnp.logical_and(c == n - 1, d == 1))
    def _():
        for h in range(C_H):
            sb_ref[h] = _t128(st_scr[h])


def _gla(z, B, T, w_a2, b_a, s0f, s0b, norm_g):
    C = GLA_CHUNK
    n = T // C
    HK = C_H * C_DK
    HV = C_H * C_DV
    mat, code, ones = _gla_tables(C)
    nr = mat.shape[1] // C
    wg = jnp.zeros((2, LANES, HK), F32)
    for dr in range(2):
        wg = wg.at[dr, dr * C_RANK:(dr + 1) * C_RANK, :].set(w_a2[dr])

    def row(b, d, c):
        return b * n + c + d * (n - 1 - 2 * c)

    st_spec = pl.BlockSpec((None, C_H, C_DK, C_DV), lambda b, d, c: (b, 0, 0, 0))
    st_shape = jax.ShapeDtypeStruct((B, C_H, C_DK, C_DV), F32)
    return pl.pallas_call(
        functools.partial(_gla_kernel, n=n, C=C),
        grid=(B, 2, n),
        in_specs=[pl.BlockSpec((C, HK), lambda b, d, c: (row(b, d, c), 0)),
                  pl.BlockSpec((C, HK), lambda b, d, c: (row(b, d, c), 1)),
                  pl.BlockSpec((C, HV), lambda b, d, c: (row(b, d, c), 1)),
                  pl.BlockSpec((C, HV), lambda b, d, c: (row(b, d, c), 2)),
                  pl.BlockSpec((C, LANES), lambda b, d, c: (row(b, d, c), (2 * HK + 2 * HV) // LANES)),
                  pl.BlockSpec((None, LANES, HK), lambda b, d, c: (d, 0, 0)),
                  pl.BlockSpec((None, 1, HK), lambda b, d, c: (d, 0, 0)),
                  pl.BlockSpec((None, nr * C, C), lambda b, d, c: (d, 0, 0)),
                  pl.BlockSpec((None, C, C), lambda b, d, c: (d, 0, 0)),
                  pl.BlockSpec((SUB * LANES, C), lambda b, d, c: (0, 0)),
                  st_spec, st_spec,
                  pl.BlockSpec((1, HV), lambda b, d, c: (0, 0))],
        out_specs=[pl.BlockSpec((C, HV), lambda b, d, c: (b * n + (n - 1) - d * c, 0)),
                   st_spec, st_spec],
        out_shape=[jax.ShapeDtypeStruct((B * T, HV), BF16), st_shape, st_shape],
        scratch_shapes=[pltpu.VMEM((C_H, C_DV, C_DK), F32), pltpu.VMEM((n, C, HV), F32)],
        compiler_params=_params("arbitrary", "arbitrary", "arbitrary"),
        name="gla",
    )(z, z, z, z, z, wg, b_a.reshape(2, 1, HK), mat, code, ones, s0f, s0b, norm_g.reshape(1, HV))


def _run_stream(x, B, T, mods, ctx, p):
    N, D = x.shape
    rpg = N // mods[0].shape[0]
    TM = min(1024, rpg)

    z = _norm_mm(x, p['norm1_g'][0], mods[0], (0, 1), p['even_w_in'][0], 1280, TM, rpg)
    if ctx is None:
        s0 = jnp.zeros((B, A_H, A_DK, A_DV), F32)
        a_f0, a_b0, cache_k, cache_v = s0, s0, None, None
    else:
        cache_k, cache_v, a_f0, a_b0 = ctx[0], ctx[1], ctx[2], ctx[3]
    o_a, a_sf, a_sb = _retention(z, B, T, p['a_log_gamma'][0], a_f0, a_b0, p['a_norm_g'][0])
    qpad, k_norm, k_rot, v_bf = _bprep(z, T, p['b_q_g'][0], p['b_k_g'][0], rope=ctx is not None)
    o_b = _attention(qpad, k_rot, v_bf, B, T, cache_k, cache_v)
    x = _proj_res(x, mods[0], 2, [o_a, o_b], p['even_w_out'][0], rpg)
    x = _ffn(x, p['norm2_g'][0], mods[0], p['ff_w1'][0], p['ff_w3'][0], p['ff_w2'][0], rpg)

    w_in = p['odd_w_in'][0]
    pad = (-w_in.shape[1]) % 640
    z1 = _norm_mm(x, p['norm1_g'][1], mods[1], (0, 1), jnp.pad(w_in, ((0, 0), (0, pad))), 640, TM, rpg)
    if ctx is None:
        s0 = jnp.zeros((B, C_H, C_DK, C_DV), F32)
        c_f0, c_b0 = s0, s0
    else:
        c_f0, c_b0 = ctx[4], ctx[5]
    o_c, c_sf, c_sb = _gla(z1, B, T, p['c_w_a2'][0], p['c_b_a'][0], c_f0, c_b0, p['c_norm_g'][0])
    x = _proj_res(x, mods[1], 2, [o_c], p['odd_w_out'][0], rpg)
    v_raw = z[:, EVEN_V_OFF:EVEN_V_OFF + B_HKV * B_HD]
    return x, (k_norm, v_raw, a_sf, a_sb, c_sf, c_sb)


EVEN_V_OFF = A_H * (2 * A_DK + 2 * A_DV) + (B_H + B_HKV) * B_HD


def kernel(x_prompt, x_sample, c, cache_b_k, cache_b_v, state_a_fwd, state_a_bwd, state_c_fwd, state_c_bwd,
           c_ctx, w_mod, b_mod, norm1_g, norm2_g, final_g, even_w_in, even_w_out, a_log_gamma, a_norm_g,
           b_q_g, b_k_g, odd_w_in, c_w_a2, c_b_a, c_norm_g, odd_w_out, ff_w1, ff_w3, ff_w2,
           router_w, moe_w1, moe_w3, moe_w2):
    Bp, Tp, D = x_prompt.shape
    Bs, Ts, _ = x_sample.shape
    L = w_mod.shape[0]
    assert L == 2 and even_w_in.shape[0] == 1 and odd_w_in.shape[0] == 1
    p = dict(norm1_g=norm1_g, norm2_g=norm2_g, final_g=final_g, even_w_in=even_w_in, even_w_out=even_w_out,
             a_log_gamma=a_log_gamma, a_norm_g=a_norm_g, b_q_g=b_q_g, b_k_g=b_k_g, odd_w_in=odd_w_in,
             c_w_a2=c_w_a2, c_b_a=c_b_a, c_norm_g=c_norm_g, odd_w_out=odd_w_out, ff_w1=ff_w1, ff_w3=ff_w3,
             ff_w2=ff_w2, router_w=router_w, moe_w1=moe_w1, moe_w3=moe_w3, moe_w2=moe_w2)

    rows = 8
    conds = jnp.concatenate([c_ctx[None, :], c, jnp.zeros((rows - 1 - Bs, D), F32)], axis=0)
    mod = _modulation(conds, w_mod, b_mod).reshape(L, rows, 6, 1, D)
    mods_p = [mod[l, 0:1] for l in range(L)]
    mods_s = [mod[l, 1:1 + Bs] for l in range(L)]

    x_p, kept = _run_stream(x_prompt.reshape(Bp * Tp, D), Bp, Tp, mods_p, None, p)
    nk = B_HKV * B_HD
    ctx = (cache_b_k[:, 0].reshape(Bs, -1, nk), cache_b_v[:, 0].reshape(Bs, -1, nk),
           state_a_fwd[:, 0], state_a_bwd[:, 0], state_c_fwd[:, 0], state_c_bwd[:, 0])
    x_s, _ = _run_stream(x_sample.reshape(Bs * Ts, D), Bs, Ts, mods_s, ctx, p)
    y_p, y_s = _moe(x_p, x_s, norm2_g[1], mod[1, 0:1 + Bs], router_w[0], moe_w1[0], moe_w3[0], moe_w2[0],
                    final_g, Ts)

    k_norm, v_raw, a_sf, a_sb, c_sf, c_sb = kept
    return (y_p.reshape(Bp, Tp, D), y_s.reshape(Bs, Ts, D),
            k_norm.reshape(Bp, 1, Tp, B_HKV, B_HD), v_raw.reshape(Bp, 1, Tp, B_HKV, B_HD),
            a_sf[:, None], a_sb[:, None], c_sf[:, None], c_sb[:, None])
```

```python
import functools

import numpy as np
import jax
import jax.numpy as jnp
from jax import lax
from jax.experimental import pallas as pl
from jax.experimental.pallas import tpu as pltpu

F32 = jnp.float32
BF16 = jnp.bfloat16
EPS = 1e-6
HIGHEST = lax.Precision.HIGHEST
LOG2E = 1.4426950408889634

VMEM_LIMIT_BYTES = 56 * 1024 * 1024

A_H, A_DK, A_DV = 4, 128, 256
B_H, B_HKV, B_HD = 8, 2, 64
C_H, C_DK, C_DV, C_RANK = 4, 128, 256, 16
C_TAU = 16.0
GRID_W = 64
ROPE_THETA = 10000.0
N_EXPERTS = 8
LANES = 128
SUB = 8
RET_CHUNK = 128
GLA_CHUNK = 128
Q_TILE = 128


def _params(*sem):
    return pltpu.CompilerParams(dimension_semantics=sem, vmem_limit_bytes=VMEM_LIMIT_BYTES)


def _dot(a, b):
    return jnp.dot(a, b, preferred_element_type=F32)


def _dot_nt(a, b):
    return lax.dot_general(a, b, (((1,), (1,)), ((), ())), preferred_element_type=F32)


def _silu(x):
    return x * jax.nn.sigmoid(x)


def _norm_mod(x, g, sh, sc):
    r = lax.rsqrt(jnp.mean(x * x, axis=-1, keepdims=True) + EPS)
    return (x * r * g) * (1.0 + sc) + sh


def _mod_kernel(c_ref, w_ref, b_ref, o_ref):
    c = c_ref[...]
    o_ref[...] = jnp.dot(_silu(c), w_ref[...], precision=HIGHEST, preferred_element_type=F32) + b_ref[...]


def _modulation(conds, w_mod, b_mod):
    L, D, D6 = w_mod.shape
    R = conds.shape[0]
    TN = 1024
    return pl.pallas_call(
        _mod_kernel,
        grid=(L, D6 // TN),
        in_specs=[pl.BlockSpec((R, D), lambda l, j: (0, 0)),
                  pl.BlockSpec((None, D, TN), lambda l, j: (l, 0, j)),
                  pl.BlockSpec((None, 1, TN), lambda l, j: (l, 0, j))],
        out_specs=pl.BlockSpec((None, R, TN), lambda l, j: (l, 0, j)),
        out_shape=jax.ShapeDtypeStruct((L, R, D6), F32),
        compiler_params=_params("arbitrary", "arbitrary"),
        name="modulation",
    )(conds, w_mod, b_mod.reshape(L, 1, D6))


def _mod_spec(part, D, TM, rows_per_group, axis):
    def idx(*g):
        return ((g[axis] * TM) // rows_per_group, part, 0, 0)
    return pl.BlockSpec((None, None, 1, D), idx)


def _norm_mm_kernel(x_ref, g_ref, sh_ref, sc_ref, w_ref, o_ref, wbf_ref):
    @pl.when(pl.program_id(1) == 0)
    def _():
        wbf_ref[...] = w_ref[...].astype(BF16)

    h = _norm_mod(x_ref[...], g_ref[...], sh_ref[...], sc_ref[...])
    o_ref[...] = _dot(h.astype(BF16), wbf_ref[...]).astype(o_ref.dtype)


def _norm_mm(x, g, mod, parts, w, TN, TM, rows_per_group):
    N, D = x.shape
    NO = w.shape[1]
    return pl.pallas_call(
        _norm_mm_kernel,
        grid=(NO // TN, N // TM),
        in_specs=[pl.BlockSpec((TM, D), lambda j, i: (i, 0)),
                  pl.BlockSpec((1, D), lambda j, i: (0, 0)),
                  _mod_spec(parts[0], D, TM, rows_per_group, 1),
                  _mod_spec(parts[1], D, TM, rows_per_group, 1),
                  pl.BlockSpec((D, TN), lambda j, i: (0, j))],
        out_specs=pl.BlockSpec((TM, TN), lambda j, i: (i, j)),
        out_shape=jax.ShapeDtypeStruct((N, NO), F32),
        scratch_shapes=[pltpu.VMEM((D, TN), BF16)],
        compiler_params=_params("arbitrary", "arbitrary"),
        name="norm_mm",
    )(x, g.reshape(1, D), mod, mod, w)


def _ret_kernel(lg_ref, q_ref, k_ref, v_ref, ag_ref, s0f_ref, s0b_ref, ng_ref,
                o_ref, sf_ref, sb_ref, s_scr, of_scr, *, n, C):
    d = pl.program_id(1)
    c = pl.program_id(2)

    @pl.when(jnp.logical_and(c == 0, d == 0))
    def _():
        s_scr[...] = s0f_ref[...]

    @pl.when(jnp.logical_and(c == 0, d == 1))
    def _():
        s_scr[...] = s0b_ref[...]

    df = d.astype(F32)
    sgn = 1.0 - 2.0 * df
    ii = lax.broadcasted_iota(jnp.int32, (C, C), 0).astype(F32)
    jj = lax.broadcasted_iota(jnp.int32, (C, C), 1).astype(F32)
    dd = (ii - jj) * sgn
    feeds = dd >= 0.0
    ddc = jnp.maximum(dd, 0.0)
    ri = lax.broadcasted_iota(jnp.int32, (C, 1), 0).astype(F32)
    pos_q = (ri + 1.0) + df * (C - 2.0 * ri - 1.0)
    pos_k = (C - 1.0 - ri) + df * (2.0 * ri - C + 1.0)
    chunk_len = jnp.full((1, A_DV), float(C), F32)

    outs = []
    for h in range(A_H):
        lg = lg_ref[d, h]
        dmask = jnp.where(feeds, jnp.exp(lg * ddc), 0.0)
        qh = q_ref[:, h * A_DK:(h + 1) * A_DK] * (A_DK ** -0.5)
        kh = k_ref[:, h * A_DK:(h + 1) * A_DK]
        vh = v_ref[:, h * A_DV:(h + 1) * A_DV].astype(BF16)
        s = s_scr[h]
        att = _dot_nt(qh.astype(BF16), kh.astype(BF16)) * dmask
        o = _dot(att.astype(BF16), vh) + _dot((qh * jnp.exp(lg * pos_q)).astype(BF16), s.astype(BF16))
        kd = kh * jnp.exp(lg * pos_k)
        s_scr[h] = jnp.exp(lg * chunk_len) * s + _dot(kd.T.astype(BF16), vh)
        outs.append(o)
    o_all = jnp.concatenate(outs, axis=-1)

    @pl.when(d == 0)
    def _():
        of_scr[c] = o_all

    @pl.when(d == 1)
    def _():
        tot = o_all + of_scr[n - 1 - c]
        res = []
        for h in range(A_H):
            sl = slice(h * A_DV, (h + 1) * A_DV)
            t = tot[:, sl]
            dev = t - jnp.mean(t, axis=-1, keepdims=True)
            y = dev * lax.rsqrt(jnp.mean(dev * dev, axis=-1, keepdims=True) + EPS) * ng_ref[:, sl]
            res.append(y * _silu(ag_ref[:, sl]))
        o_ref[...] = jnp.concatenate(res, axis=-1).astype(o_ref.dtype)

    @pl.when(jnp.logical_and(c == n - 1, d == 0))
    def _():
        sf_ref[...] = s_scr[...]

    @pl.when(jnp.logical_and(c == n - 1, d == 1))
    def _():
        sb_ref[...] = s_scr[...]


def _retention(z, B, T, log_gamma, s0f, s0b, norm_g):
    C = RET_CHUNK
    n = T // C
    HV = A_H * A_DV

    def row(b, d, c):
        return b * n + c + d * (n - 1 - 2 * c)

    st_spec = pl.BlockSpec((None, A_H, A_DK, A_DV), lambda b, d, c: (b, 0, 0, 0))
    st_shape = jax.ShapeDtypeStruct((B, A_H, A_DK, A_DV), F32)
    return pl.pallas_call(
        functools.partial(_ret_kernel, n=n, C=C),
        grid=(B, 2, n),
        in_specs=[pl.BlockSpec(memory_space=pltpu.SMEM),
                  pl.BlockSpec((C, 512), lambda b, d, c: (row(b, d, c), 0)),
                  pl.BlockSpec((C, 512), lambda b, d, c: (row(b, d, c), 1)),
                  pl.BlockSpec((C, HV), lambda b, d, c: (row(b, d, c), 1)),
                  pl.BlockSpec((C, HV), lambda b, d, c: (row(b, d, c), 2)),
                  st_spec, st_spec,
                  pl.BlockSpec((1, HV), lambda b, d, c: (0, 0))],
        out_specs=[pl.BlockSpec((C, HV), lambda b, d, c: (b * n + (n - 1) - d * c, 0)),
                   st_spec, st_spec],
        out_shape=[jax.ShapeDtypeStruct((B * T, HV), BF16), st_shape, st_shape],
        scratch_shapes=[pltpu.VMEM((A_H, A_DK, A_DV), F32), pltpu.VMEM((n, C, HV), F32)],
        compiler_params=_params("arbitrary", "arbitrary", "arbitrary"),
        name="retention",
    )(log_gamma, z, z, z, z, s0f, s0b, norm_g.reshape(1, HV))


def _group_sum_matrix(width, group):
    i = np.arange(width)
    return jnp.asarray((i[:, None] // group == i[None, :] // group).astype(np.float32), dtype=BF16)


def _q_pad_matrix():
    m = np.zeros((B_H * B_HD, B_H * LANES), np.float32)
    g = B_H // B_HKV
    for h in range(B_H):
        for t in range(B_HD):
            m[h * B_HD + t, h * LANES + (h // g) * B_HD + t] = 1.0
    return jnp.asarray(m, dtype=BF16)


def _rope_tables(T):
    rows = T // GRID_W
    row = np.repeat(np.arange(rows, dtype=np.float64), GRID_W)
    col = np.tile(np.arange(GRID_W, dtype=np.float64), rows)
    nq = B_HD // 4
    inv = ROPE_THETA ** (-np.arange(nq, dtype=np.float64) / nq)
    ang = np.concatenate([row[:, None] * inv, col[:, None] * inv], axis=-1)
    cos = np.repeat(np.cos(ang), 2, axis=-1)
    sin = np.repeat(np.sin(ang), 2, axis=-1)
    sign = np.tile(np.array([-1.0, 1.0]), B_HD // 2)
    reps = LANES // B_HD
    return (jnp.asarray(np.tile(cos, (1, reps)), dtype=F32),
            jnp.asarray(np.tile(sin * sign, (1, reps)), dtype=F32))


def _group_rmsnorm(x, gsum, g):
    x2 = x * x
    hi = x2.astype(BF16)
    lo = (x2 - hi.astype(F32)).astype(BF16)
    ss = _dot(hi, gsum) + _dot(lo, gsum)
    return x * lax.rsqrt(ss * (1.0 / B_HD) + EPS) * g


def _rotate_pairs(x, cos, sin_signed):
    n = x.shape[1]
    lane = lax.broadcasted_iota(jnp.int32, x.shape, 1)
    partner = jnp.where(lane % 2 == 0, pltpu.roll(x, n - 1, 1), pltpu.roll(x, 1, 1))
    reps = n // LANES
    if reps > 1:
        cos = jnp.concatenate([cos] * reps, axis=1)
        sin_signed = jnp.concatenate([sin_signed] * reps, axis=1)
    return x * cos + partner * sin_signed


def _bprep_kernel(z_ref, qg_ref, kg_ref, cos_ref, sin_ref, gq_ref, gk_ref, pad_ref,
                  qpad_ref, kn_ref, kr_ref, vb_ref, *, rope):
    nq = B_H * B_HD
    nk = B_HKV * B_HD
    qn = _group_rmsnorm(z_ref[:, 0:nq], gq_ref[...], qg_ref[...])
    kn = _group_rmsnorm(z_ref[:, nq:nq + nk], gk_ref[...], kg_ref[...])
    kn_ref[...] = kn
    if rope:
        qn = _rotate_pairs(qn, cos_ref[...], sin_ref[...])
        kn = _rotate_pairs(kn, cos_ref[...], sin_ref[...])
    kr_ref[...] = kn.astype(BF16)
    vb_ref[...] = z_ref[:, nq + nk:nq + 2 * nk].astype(BF16)
    qs = (qn * (B_HD ** -0.5 * LOG2E)).astype(BF16)
    qpad_ref[...] = _dot(qs, pad_ref[...]).astype(BF16)


def _bprep(z, T, q_g, k_g, rope):
    N = z.shape[0]
    TM = min(512, T)
    nq = B_H * B_HD
    nk = B_HKV * B_HD
    width = nq + 2 * nk
    col = (A_H * (2 * A_DK + 2 * A_DV)) // width
    cos, sin = _rope_tables(T if rope else TM)
    nt = T // TM if rope else 1
    const = lambda i: (0, 0)
    return pl.pallas_call(
        functools.partial(_bprep_kernel, rope=rope),
        grid=(N // TM,),
        in_specs=[pl.BlockSpec((TM, width), lambda i: (i, col)),
                  pl.BlockSpec((1, nq), const),
                  pl.BlockSpec((1, nk), const),
                  pl.BlockSpec((TM, LANES), lambda i: (i % nt, 0)),
                  pl.BlockSpec((TM, LANES), lambda i: (i % nt, 0)),
                  pl.BlockSpec((nq, nq), const),
                  pl.BlockSpec((nk, nk), const),
                  pl.BlockSpec((nq, B_H * LANES), const)],
        out_specs=[pl.BlockSpec((TM, B_H * LANES), lambda i: (i, 0)),
                   pl.BlockSpec((TM, nk), lambda i: (i, 0)),
                   pl.BlockSpec((TM, nk), lambda i: (i, 0)),
                   pl.BlockSpec((TM, nk), lambda i: (i, 0))],
        out_shape=[jax.ShapeDtypeStruct((N, B_H * LANES), BF16),
                   jax.ShapeDtypeStruct((N, nk), F32),
                   jax.ShapeDtypeStruct((N, nk), BF16),
                   jax.ShapeDtypeStruct((N, nk), BF16)],
        compiler_params=_params("arbitrary"),
        name="attn_prep",
    )(z, jnp.tile(q_g, B_H).reshape(1, nq), jnp.tile(k_g, B_HKV).reshape(1, nk), cos, sin,
      _group_sum_matrix(nq, B_HD), _group_sum_matrix(nk, B_HD), _q_pad_matrix())


def _lane_fold(x, op):
    acc = x[:, 0:LANES]
    for j in range(1, x.shape[1] // LANES):
        acc = op(acc, x[:, j * LANES:(j + 1) * LANES])
    return acc


def _attn_kernel(*refs, has_cache, kc):
    if has_cache:
        q_ref, k_ref, v_ref, ck_ref, cv_ref, o_ref, s_scr, m_scr, l_scr, acc_scr = refs
        ncache = ck_ref.shape[0] // kc
    else:
        q_ref, k_ref, v_ref, o_ref, s_scr, m_scr, l_scr, acc_scr = refs
        ncache = 0
    tq = q_ref.shape[0]
    nlat = k_ref.shape[0] // kc
    q = jnp.concatenate([q_ref[:, h * LANES:(h + 1) * LANES] for h in range(B_H)], axis=0)

    def lat_rows(c):
        return pl.ds(pl.multiple_of(c * kc, kc), kc)

    def scores(c, kblk):
        s = _dot_nt(q, kblk)
        s_scr[c] = s
        m_scr[...] = jnp.maximum(m_scr[...], _lane_fold(s, jnp.maximum))

    m_scr[...] = jnp.full(m_scr.shape, -jnp.inf, F32)
    for c in range(ncache):
        scores(c, ck_ref[c * kc:(c + 1) * kc, :].astype(BF16))

    def pass1(c, carry):
        scores(ncache + c, k_ref[lat_rows(c), :])
        return carry
    lax.fori_loop(0, nlat, pass1, 0)
    m = jnp.max(m_scr[...], axis=-1, keepdims=True)

    def weights(c, vblk):
        p = jnp.exp2(s_scr[c] - m)
        l_scr[...] += _lane_fold(p, jnp.add)
        acc_scr[...] += _dot(p.astype(BF16), vblk)

    l_scr[...] = jnp.zeros_like(l_scr)
    acc_scr[...] = jnp.zeros_like(acc_scr)
    for c in range(ncache):
        weights(c, cv_ref[c * kc:(c + 1) * kc, :].astype(BF16))

    def pass2(c, carry):
        weights(ncache + c, v_ref[lat_rows(c), :])
        return carry
    lax.fori_loop(0, nlat, pass2, 0)

    r_all = acc_scr[...] / jnp.sum(l_scr[...], axis=-1, keepdims=True)
    g = B_H // B_HKV
    lane = lax.broadcasted_iota(jnp.int32, (tq, LANES), 1)
    outs = []
    for j in range(B_H // 2):
        pair = []
        for half in range(2):
            h = 2 * j + half
            r = r_all[h * tq:(h + 1) * tq, :]
            if h // g != half:
                r = pltpu.roll(r, B_HD, 1)
            pair.append(r)
        outs.append(jnp.where(lane < B_HD, pair[0], pair[1]))
    o_ref[...] = jnp.concatenate(outs, axis=-1).astype(o_ref.dtype)


def _attention(qpad, kr, vb, B, T, cache_k, cache_v):
    has_cache = cache_k is not None
    TQ = Q_TILE
    nq = T // TQ
    nk = B_HKV * B_HD
    in_specs = [pl.BlockSpec((TQ, B_H * LANES), lambda b, i: (b * nq + i, 0)),
                pl.BlockSpec((T, nk), lambda b, i: (b, 0)),
                pl.BlockSpec((T, nk), lambda b, i: (b, 0))]
    args = [qpad, kr, vb]
    kc = min(512, T)
    nchunks = T // kc
    if has_cache:
        P = cache_k.shape[1]
        assert P % kc == 0
        nchunks += P // kc
        in_specs += [pl.BlockSpec((None, P, nk), lambda b, i: (b, 0, 0))] * 2
        args += [cache_k, cache_v]
    R = B_H * TQ
    return pl.pallas_call(
        functools.partial(_attn_kernel, has_cache=has_cache, kc=kc),
        grid=(B, nq),
        in_specs=in_specs,
        out_specs=pl.BlockSpec((TQ, B_H * B_HD), lambda b, i: (b * nq + i, 0)),
        out_shape=jax.ShapeDtypeStruct((B * T, B_H * B_HD), BF16),
        scratch_shapes=[pltpu.VMEM((nchunks, R, kc), F32), pltpu.VMEM((R, LANES), F32),
                        pltpu.VMEM((R, LANES), F32), pltpu.VMEM((R, LANES), F32)],
        compiler_params=_params("arbitrary", "arbitrary"),
        name="attention",
    )(*args)


def _proj_res_kernel(*refs, n_in):
    x_ref, gate_ref = refs[0], refs[1]
    o_refs = refs[2:2 + n_in]
    w_refs = refs[2 + n_in:2 + 2 * n_in]
    out_ref = refs[2 + 2 * n_in]
    wbf_refs = refs[3 + 2 * n_in:]

    @pl.when(pl.program_id(0) == 0)
    def _():
        for w_ref, wbf_ref in zip(w_refs, wbf_refs):
            wbf_ref[...] = w_ref[...].astype(BF16)

    acc = _dot(o_refs[0][...], wbf_refs[0][...])
    for o_ref, wbf_ref in zip(o_refs[1:], wbf_refs[1:]):
        acc = acc + _dot(o_ref[...], wbf_ref[...])
    out_ref[...] = x_ref[...] + gate_ref[...] * acc


def _proj_res(x, mod, part, acts, w, rows_per_group):
    N, D = x.shape
    TM = min(512, rows_per_group)
    n_in = len(acts)
    widths = [a.shape[1] for a in acts]
    offs = np.cumsum([0] + widths[:-1]).tolist()
    in_specs = [pl.BlockSpec((TM, D), lambda i: (i, 0)),
                _mod_spec(part, D, TM, rows_per_group, 0)]
    in_specs += [pl.BlockSpec((TM, wd), lambda i: (i, 0)) for wd in widths]
    in_specs += [pl.BlockSpec((wd, D), functools.partial(lambda i, blk: (blk, 0), blk=off // wd))
                 for wd, off in zip(widths, offs)]
    return pl.pallas_call(
        functools.partial(_proj_res_kernel, n_in=n_in),
        grid=(N // TM,),
        in_specs=in_specs,
        out_specs=pl.BlockSpec((TM, D), lambda i: (i, 0)),
        out_shape=jax.ShapeDtypeStruct((N, D), F32),
        scratch_shapes=[pltpu.VMEM((wd, D), BF16) for wd in widths],
        compiler_params=_params("arbitrary"),
        name="proj_residual",
    )(x, mod, *acts, *([w] * n_in))


def _ffn_kernel(x_ref, g_ref, sh_ref, sc_ref, gate_ref, w1_ref, w3_ref, w2_ref, out_ref, h_scr, acc_scr, *, nf):
    f = pl.program_id(1)

    @pl.when(f == 0)
    def _():
        h_scr[...] = _norm_mod(x_ref[...], g_ref[...], sh_ref[...], sc_ref[...]).astype(BF16)
        acc_scr[...] = jnp.zeros_like(acc_scr)

    h = h_scr[...]
    a = _dot(h, w1_ref[...].astype(BF16))
    b = _dot(h, w3_ref[...].astype(BF16))
    acc_scr[...] += _dot((_silu(a) * b).astype(BF16), w2_ref[...].astype(BF16))

    @pl.when(f == nf - 1)
    def _():
        out_ref[...] = x_ref[...] + gate_ref[...] * acc_scr[...]


def _ffn(x, g, mod, w1, w3, w2, rows_per_group):
    N, D = x.shape
    FF = w1.shape[1]
    TM, TF = min(1024, rows_per_group), 256
    nf = FF // TF
    return pl.pallas_call(
        functools.partial(_ffn_kernel, nf=nf),
        grid=(N // TM, nf),
        in_specs=[pl.BlockSpec((TM, D), lambda i, f: (i, 0)),
                  pl.BlockSpec((1, D), lambda i, f: (0, 0)),
                  _mod_spec(3, D, TM, rows_per_group, 0),
                  _mod_spec(4, D, TM, rows_per_group, 0),
                  _mod_spec(5, D, TM, rows_per_group, 0),
                  pl.BlockSpec((D, TF), lambda i, f: (0, f)),
                  pl.BlockSpec((D, TF), lambda i, f: (0, f)),
                  pl.BlockSpec((TF, D), lambda i, f: (f, 0))],
        out_specs=pl.BlockSpec((TM, D), lambda i, f: (i, 0)),
        out_shape=jax.ShapeDtypeStruct((N, D), F32),
        scratch_shapes=[pltpu.VMEM((TM, D), BF16), pltpu.VMEM((TM, D), F32)],
        compiler_params=_params("arbitrary", "arbitrary"),
        name="ffn",
    )(x, g.reshape(1, D), mod, mod, mod, w1, w3, w2)


MOE_SB = 512
MOE_TRG = 256
MOE_TR = 1024


def _two_stream_specs(shape, ntp, ax=0):
    def idx_p(*g):
        return (jnp.minimum(g[ax], ntp - 1), 0)

    def idx_s(*g):
        return (jnp.maximum(g[ax] - ntp, 0), 0)
    return pl.BlockSpec(shape, idx_p), pl.BlockSpec(shape, idx_s)


def _pool_mod_spec(part, D, TM, ntp, rows_per_group):
    def idx(i, *_):
        return (jnp.where(i < ntp, 0, 1 + ((i - ntp) * TM) // rows_per_group), part, 0, 0)
    return pl.BlockSpec((None, None, 1, D), idx)


def _route_kernel(xp_ref, xs_ref, g_ref, sh_ref, sc_ref, rw_ref, tri_ref, h_ref, info_ref, infot_ref, cum_ref,
                  carry_scr, *, ntp):
    i = pl.program_id(0)

    @pl.when(i == 0)
    def _():
        carry_scr[...] = jnp.zeros_like(carry_scr)

    x = jnp.where(i < ntp, xp_ref[...], xs_ref[...])
    h = _norm_mod(x, g_ref[...], sh_ref[...], sc_ref[...])
    h_ref[...] = h.astype(BF16)
    lane = lax.broadcasted_iota(jnp.int32, (x.shape[0], LANES), 1).astype(F32)
    logits = jnp.dot(h, rw_ref[...], precision=HIGHEST, preferred_element_type=F32)
    logits = jnp.where(lane < N_EXPERTS, logits, -jnp.inf)
    m1 = jnp.max(logits, axis=-1, keepdims=True)
    i1 = jnp.min(jnp.where(logits == m1, lane, float(LANES)), axis=-1, keepdims=True)
    rest = jnp.where(lane == i1, -jnp.inf, logits)
    m2 = jnp.max(rest, axis=-1, keepdims=True)
    i2 = jnp.min(jnp.where(rest == m2, lane, float(LANES)), axis=-1, keepdims=True)
    e2 = jnp.exp(m2 - m1)
    w1 = 1.0 / (1.0 + e2)
    w2 = e2 / (1.0 + e2)
    ind = jnp.where(jnp.logical_or(lane == i1, lane == i2), 1.0, 0.0)
    before = _dot(tri_ref[...], ind.astype(BF16)) + carry_scr[...]
    r1 = jnp.sum(jnp.where(lane == i1, before, 0.0), axis=-1, keepdims=True)
    r2 = jnp.sum(jnp.where(lane == i2, before, 0.0), axis=-1, keepdims=True)
    total = carry_scr[...] + jnp.sum(ind, axis=0, keepdims=True)
    carry_scr[...] = total
    cum_ref[...] = total
    info = jnp.where(lane == 0.0, i1, jnp.where(lane == 1.0, i2, jnp.where(lane == 2.0, w1, jnp.where(
        lane == 3.0, w2, jnp.where(lane == 4.0, r1, jnp.where(lane == 5.0, r2, 0.0))))))
    info_ref[...] = info[:, 0:SUB]
    info_t = jnp.concatenate([info[r:r + LANES, :].T for r in range(0, info.shape[0], LANES)], axis=1)
    infot_ref[...] = info_t[0:SUB, :]


def _moe_route(xp, xs, g, mod, router_w, rows_per_group):
    Np, D = xp.shape
    N = Np + xs.shape[0]
    TM = MOE_SB
    ntp = Np // TM
    nt = N // TM
    rw = jnp.pad(router_w, ((0, 0), (0, LANES - router_w.shape[1])))
    tri = jnp.asarray(np.tril(np.ones((TM, TM), np.float32), -1), dtype=BF16)
    xp_spec, xs_spec = _two_stream_specs((TM, D), ntp)
    return pl.pallas_call(
        functools.partial(_route_kernel, ntp=ntp),
        grid=(nt,),
        in_specs=[xp_spec, xs_spec,
                  pl.BlockSpec((1, D), lambda i: (0, 0)),
                  _pool_mod_spec(3, D, TM, ntp, rows_per_group),
                  _pool_mod_spec(4, D, TM, ntp, rows_per_group),
                  pl.BlockSpec((D, LANES), lambda i: (0, 0)),
                  pl.BlockSpec((TM, TM), lambda i: (0, 0))],
        out_specs=[pl.BlockSpec((TM, D), lambda i: (i, 0)),
                   pl.BlockSpec((TM, SUB), lambda i: (i, 0)),
                   pl.BlockSpec((SUB, TM), lambda i: (0, i)),
                   pl.BlockSpec((None, 1, LANES), lambda i: (i, 0, 0))],
        out_shape=[jax.ShapeDtypeStruct((N, D), BF16),
                   jax.ShapeDtypeStruct((N, SUB), F32),
                   jax.ShapeDtypeStruct((SUB, N), F32),
                   jax.ShapeDtypeStruct((nt, 1, LANES), F32)],
        scratch_shapes=[pltpu.VMEM((1, LANES), F32)],
        compiler_params=_params("arbitrary"),
        name="moe_route",
    )(xp, xs, g.reshape(1, D), mod, mod, rw, tri)


def _moe_plan(info, info_t, cum, N):
    E, SB, TRG, TR = N_EXPERTS, MOE_SB, MOE_TRG, MOE_TR
    NB = N // SB
    rmax = 2 * N + E * TR
    RG, RT = rmax // TRG, rmax // TR
    PMAX = RG + E * NB
    i32 = jnp.int32
    cum_e = cum[:, 0, :E].astype(i32).T
    cnt = cum_e[:, -1]
    tiles = (cnt + TR - 1) // TR
    start = TR * (jnp.cumsum(tiles) - tiles)

    startf = start.astype(F32)
    pos_cols = jnp.concatenate([startf[info[:, 0:2].astype(i32)] + info[:, 4:6], info[:, 2:4],
                                jnp.zeros((N, 4), F32)], axis=1)
    pos_rows = jnp.concatenate([startf[info_t[0:2].astype(i32)] + info_t[4:6], jnp.zeros((6, N), F32)],
                               axis=0)

    def region(row0):
        e = jnp.clip(jnp.sum(row0[:, None] >= start[None, :], axis=1) - 1, 0, E - 1)
        return e, row0 - start[e]

    eq, lo = region(jnp.arange(RG, dtype=i32) * TRG)
    hi = jnp.minimum(lo + TRG, cnt[eq])
    first = jnp.sum(cum_e[eq] <= lo[:, None], axis=1)
    last = jnp.sum(cum_e[eq] < hi[:, None], axis=1)
    nblk = jnp.where(hi > lo, last - first + 1, 0)
    pend = jnp.cumsum(nblk)
    npairs = pend[-1]
    p = jnp.arange(PMAX, dtype=i32)
    valid = p < npairs
    pc = jnp.minimum(p, npairs - 1)
    q_of = jnp.minimum(jnp.sum(pend[None, :] <= pc[:, None], axis=1), RG - 1).astype(i32)
    pstart = pend - nblk
    s_of = (first[q_of] + pc - pstart[q_of]).astype(i32)
    g_first = jnp.logical_and(valid, pc == pstart[q_of]).astype(i32)
    gather_plan = (q_of, s_of, valid.astype(i32), g_first)

    order = jnp.argsort(jnp.where(valid, s_of * RG + q_of, jnp.iinfo(jnp.int32).max))
    s2, q2 = s_of[order], q_of[order]
    s2 = jnp.where(valid, s2, s2[npairs - 1])
    q2 = jnp.where(valid, q2, q2[npairs - 1])
    prev = jnp.concatenate([jnp.full((1,), -1, i32), s2[:-1]])
    nxt = jnp.concatenate([s2[1:], jnp.full((1,), -1, i32)])
    c_first = jnp.logical_and(valid, s2 != prev).astype(i32)
    c_last = jnp.logical_and(valid, jnp.logical_or(s2 != nxt, p == npairs - 1)).astype(i32)
    combine_plan = (s2.astype(i32), q2.astype(i32), valid.astype(i32), c_first, c_last)

    te, tlo = region(jnp.arange(RT, dtype=i32) * TR)
    tvalid = jnp.clip(cnt[te] - tlo, 0, TR)
    last_t = jnp.sum(tiles) - 1
    t_idx = jnp.where(tvalid > 0, jnp.arange(RT, dtype=i32), last_t).astype(i32)
    ffn_plan = (t_idx, te[t_idx].astype(i32), tvalid.astype(i32))
    return pos_cols, pos_rows, gather_plan, combine_plan, ffn_plan, rmax


def _moe_gather_kernel(q_ref, s_ref, valid_ref, first_ref, pos_ref, h_ref, out_ref):
    p = pl.program_id(0)

    @pl.when(valid_ref[p] == 1)
    def _():
        rows, toks = out_ref.shape[0], h_ref.shape[0]
        row = (lax.broadcasted_iota(jnp.int32, (rows, 1), 0) + q_ref[p] * rows).astype(F32)
        hit = jnp.logical_or(pos_ref[0:1, :] == row, pos_ref[1:2, :] == row)
        sel = jnp.where(hit, 1.0, 0.0).astype(BF16)
        @pl.when(first_ref[p] == 1)
        def _():
            out_ref[...] = jnp.zeros_like(out_ref)

        out_ref[...] = out_ref[...] + _dot(sel, h_ref[...]).astype(BF16)


def _moe_gather(h, pos_rows, plan, rmax):
    N, D = h.shape
    pmax = plan[0].shape[0]
    return pl.pallas_call(
        _moe_gather_kernel,
        grid_spec=pltpu.PrefetchScalarGridSpec(
            num_scalar_prefetch=4, grid=(pmax,),
            in_specs=[pl.BlockSpec((SUB, MOE_SB), lambda p, q, s, v, f: (0, s[p])),
                      pl.BlockSpec((MOE_SB, D), lambda p, q, s, v, f: (s[p], 0))],
            out_specs=pl.BlockSpec((MOE_TRG, D), lambda p, q, s, v, f: (q[p], 0))),
        out_shape=jax.ShapeDtypeStruct((rmax, D), BF16),
        compiler_params=_params("arbitrary"),
        name="moe_gather",
    )(*plan, pos_rows, h)


def _moe_ffn_kernel(t_ref, e_ref, nv_ref, x_ref, w1_ref, w3_ref, w2_ref, out_ref, acc_scr, w1b, w3b, w2b, *, nf):
    t = pl.program_id(0)
    f = pl.program_id(1)
    nv = nv_ref[t]

    @pl.when(nv > 0)
    def _():
        w1b[...] = w1_ref[...].astype(BF16)
        w3b[...] = w3_ref[...].astype(BF16)
        w2b[...] = w2_ref[...].astype(BF16)

    def block(rows):
        @pl.when(f == 0)
        def _():
            acc_scr[rows, :] = jnp.zeros((rows.stop - rows.start, acc_scr.shape[1]), F32)

        x = x_ref[rows, :]
        a = _dot(x, w1b[...])
        b = _dot(x, w3b[...])
        acc_scr[rows, :] += _dot((_silu(a) * b).astype(BF16), w2b[...])

        @pl.when(f == nf - 1)
        def _():
            out_ref[rows, :] = acc_scr[rows, :].astype(out_ref.dtype)

    nsub = MOE_TR // MOE_TRG
    full = nv > (nsub - 1) * MOE_TRG

    @pl.when(full)
    def _():
        block(slice(0, MOE_TR))

    for sub in range(nsub - 1):
        @pl.when(jnp.logical_and(jnp.logical_not(full), sub * MOE_TRG < nv))
        def _():
            block(slice(sub * MOE_TRG, (sub + 1) * MOE_TRG))


def _moe_ffn(xs, plan, w1, w3, w2):
    rmax, D = xs.shape
    FF = w1.shape[2]
    TF = 256
    nf = FF // TF
    RT = rmax // MOE_TR

    def fidx(t, f, nv):
        return jnp.where(nv[t] > 0, f, nf - 1)

    return pl.pallas_call(
        functools.partial(_moe_ffn_kernel, nf=nf),
        grid_spec=pltpu.PrefetchScalarGridSpec(
            num_scalar_prefetch=3, grid=(RT, nf),
            in_specs=[pl.BlockSpec((MOE_TR, D), lambda t, f, ti, e, nv: (ti[t], 0)),
                      pl.BlockSpec((None, D, TF), lambda t, f, ti, e, nv: (e[t], 0, fidx(t, f, nv))),
                      pl.BlockSpec((None, D, TF), lambda t, f, ti, e, nv: (e[t], 0, fidx(t, f, nv))),
                      pl.BlockSpec((None, TF, D), lambda t, f, ti, e, nv: (e[t], fidx(t, f, nv), 0))],
            out_specs=pl.BlockSpec((MOE_TR, D), lambda t, f, ti, e, nv: (ti[t], 0)),
            scratch_shapes=[pltpu.VMEM((MOE_TR, D), F32), pltpu.VMEM((D, TF), BF16), pltpu.VMEM((D, TF), BF16),
                            pltpu.VMEM((TF, D), BF16)]),
        out_shape=jax.ShapeDtypeStruct((rmax, D), BF16),
        compiler_params=_params("arbitrary", "arbitrary"),
        name="moe_ffn",
    )(*plan, xs, w1, w3, w2)


def _moe_combine_kernel(s_ref, q_ref, valid_ref, first_ref, last_ref, pos_ref, ys_ref, xp_ref, xs_ref, gate_ref,
                        fg_ref, op_ref, os_ref, acc_scr, *, ntp):
    p = pl.program_id(0)

    @pl.when(valid_ref[p] == 1)
    def _():
        rows = ys_ref.shape[0]
        col = (lax.broadcasted_iota(jnp.int32, (1, rows), 1) + q_ref[p] * rows).astype(F32)
        sel = (jnp.where(pos_ref[:, 0:1] == col, pos_ref[:, 2:3], 0.0)
               + jnp.where(pos_ref[:, 1:2] == col, pos_ref[:, 3:4], 0.0)).astype(BF16)
        @pl.when(first_ref[p] == 1)
        def _():
            acc_scr[...] = jnp.zeros_like(acc_scr)

        acc_scr[...] += _dot(sel, ys_ref[...])

        @pl.when(last_ref[p] == 1)
        def _():
            s = s_ref[p]
            x = jnp.where(s < ntp, xp_ref[...], xs_ref[...])
            y = x + gate_ref[...] * acc_scr[...]
            out = y * lax.rsqrt(jnp.mean(y * y, axis=-1, keepdims=True) + EPS) * fg_ref[...]

            @pl.when(s < ntp)
            def _():
                op_ref[...] = out

            @pl.when(s >= ntp)
            def _():
                os_ref[...] = out


def _moe_combine(ys, pos_cols, plan, xp, xs, mod, final_g, rows_per_group):
    Np, D = xp.shape
    Ns = xs.shape[0]
    SB = MOE_SB
    ntp = Np // SB
    pmax = plan[0].shape[0]

    def blk(p, s, *_):
        return s[p]

    def tok_p(p, s, *_):
        return (jnp.minimum(s[p], ntp - 1), 0)

    def tok_s(p, s, *_):
        return (jnp.maximum(s[p] - ntp, 0), 0)

    def gate_idx(p, s, *_):
        return (jnp.where(s[p] < ntp, 0, 1 + ((s[p] - ntp) * SB) // rows_per_group), 5, 0, 0)

    return pl.pallas_call(
        functools.partial(_moe_combine_kernel, ntp=ntp),
        grid_spec=pltpu.PrefetchScalarGridSpec(
            num_scalar_prefetch=5, grid=(pmax,),
            in_specs=[pl.BlockSpec((SB, SUB), lambda p, s, q, *_: (s[p], 0)),
                      pl.BlockSpec((MOE_TRG, D), lambda p, s, q, *_: (q[p], 0)),
                      pl.BlockSpec((SB, D), tok_p),
                      pl.BlockSpec((SB, D), tok_s),
                      pl.BlockSpec((None, None, 1, D), gate_idx),
                      pl.BlockSpec((1, D), lambda p, *_: (0, 0))],
            out_specs=[pl.BlockSpec((SB, D), tok_p), pl.BlockSpec((SB, D), tok_s)],
            scratch_shapes=[pltpu.VMEM((SB, D), F32)]),
        out_shape=[jax.ShapeDtypeStruct((Np, D), F32), jax.ShapeDtypeStruct((Ns, D), F32)],
        compiler_params=_params("arbitrary"),
        name="moe_combine",
    )(*plan, pos_cols, ys, xp, xs, mod, final_g.reshape(1, D))


def _moe(xp, xs, g, mod, router_w, w1, w3, w2, final_g, rows_per_group):
    N = xp.shape[0] + xs.shape[0]
    h, info, info_t, cum = _moe_route(xp, xs, g, mod, router_w, rows_per_group)
    pos_cols, pos_rows, gather_plan, combine_plan, ffn_plan, rmax = _moe_plan(info, info_t, cum, N)
    x_sorted = _moe_gather(h, pos_rows, gather_plan, rmax)
    y_sorted = _moe_ffn(x_sorted, ffn_plan, w1, w3, w2)
    return _moe_combine(y_sorted, pos_cols, combine_plan, xp, xs, mod, final_g, rows_per_group)


def _gla_levels(C):
    lv, c = [], C // 2
    while c >= SUB:
        lv.append(c)
        c //= 2
    return lv


def _gla_tables(C):
    levels = _gla_levels(C)
    nr = 2 + 2 * len(levels)
    mat = np.zeros((2, nr * C, C), np.float32)
    code = np.zeros((2, C, C), np.int32)
    for d in range(2):
        p = np.arange(C) if d == 0 else C - 1 - np.arange(C)
        pi, pj = p[:, None], p[None, :]
        mat[d, 0:C] = pj <= pi
        mat[d, C:2 * C] = pj > pi
        code[d] = np.where((pj <= pi) & (pi // SUB == pj // SUB), 1, 0)
        for lv, c in enumerate(levels):
            blk = pi // c
            later = blk % 2 == 1
            mat[d, (2 + 2 * lv) * C:(3 + 2 * lv) * C] = later & (pj > blk * c - 1) & (pj <= pi)
            mat[d, (3 + 2 * lv) * C:(4 + 2 * lv) * C] = (~later) & (pj > pi) & (pj <= (blk + 1) * c - 1)
            pair = (pi // (2 * c) == pj // (2 * c)) & (pi // c != pj // c) & (pj <= pi)
            code[d] = np.where(pair, 2 + lv, code[d])
    ones = np.zeros((SUB * LANES, C), np.float32)
    for jj in range(SUB):
        ones[jj * LANES:(jj + 1) * LANES, jj::SUB] = 1.0
    return jnp.asarray(mat, dtype=BF16), jnp.asarray(code), jnp.asarray(ones, dtype=BF16)


def _bcast_sublane(x, jj):
    r, w = x.shape
    x3 = x.reshape(r // SUB, SUB, w)
    return jnp.broadcast_to(x3[:, jj:jj + 1, :], x3.shape).reshape(r, w)


def _t128(x):
    r, w = x.shape
    if w > LANES:
        return jnp.concatenate([x[:, i:i + LANES].T for i in range(0, w, LANES)], axis=0)
    return jnp.concatenate([x[i:i + LANES, :].T for i in range(0, r, LANES)], axis=1)


def _gla_kernel(q_ref, k_ref, v_ref, g_ref, lr_ref, wg_ref, ba_ref, mat_ref, code_ref, ones_ref,
                s0f_ref, s0b_ref, ng_ref, o_ref, sf_ref, sb_ref, st_scr, of_scr, *, n, C):
    d = pl.program_id(1)
    c = pl.program_id(2)
    levels = _gla_levels(C)

    @pl.when(jnp.logical_and(c == 0, d == 0))
    def _():
        for h in range(C_H):
            st_scr[h] = _t128(s0f_ref[h])

    @pl.when(jnp.logical_and(c == 0, d == 1))
    def _():
        for h in range(C_H):
            st_scr[h] = _t128(s0b_ref[h])

    xg = jnp.dot(lr_ref[...], wg_ref[...], precision=HIGHEST, preferred_element_type=F32) + ba_ref[...]
    la = (jnp.minimum(xg, 0.0) - jnp.log1p(jnp.exp(-jnp.abs(xg)))) * (1.0 / C_TAU)
    hi = la.astype(BF16)
    lo = (la - hi.astype(F32)).astype(BF16)
    mat = mat_ref[...]
    cum = _dot(mat, hi) + _dot(mat, lo)
    code = code_ref[...]
    ones = ones_ref[...]

    outs = []
    for h in range(C_H):
        ks = slice(h * C_DK, (h + 1) * C_DK)
        qh = q_ref[:, ks] * (C_DK ** -0.5)
        kh = k_ref[:, ks]
        vh = v_ref[:, h * C_DV:(h + 1) * C_DV]
        vb = vh.astype(BF16)
        b = cum[0:C, ks]
        b_rest = cum[C:2 * C, ks]
        ps = []
        for jj in range(SUB):
            dec = jnp.exp(jnp.minimum(b - _bcast_sublane(b, jj), 0.0))
            ps.append((qh * _bcast_sublane(kh, jj) * dec).astype(BF16))
        att = jnp.where(code == 1, _dot(jnp.concatenate(ps, axis=1), ones), 0.0)
        for lv in range(len(levels)):
            eq = cum[(2 + 2 * lv) * C:(3 + 2 * lv) * C, ks]
            ek = cum[(3 + 2 * lv) * C:(4 + 2 * lv) * C, ks]
            a_lv = _dot_nt((qh * jnp.exp(eq)).astype(BF16), (kh * jnp.exp(ek)).astype(BF16))
            att = jnp.where(code == 2 + lv, a_lv, att)
        st = st_scr[h]
        o = _dot(att.astype(BF16), vb) + _dot_nt((qh * jnp.exp(b)).astype(BF16), st.astype(BF16))
        b_end = b[0:1, :] + b_rest[0:1, :]
        ke = (kh * jnp.exp(b_rest)).astype(BF16)
        st_scr[h] = jnp.exp(b_end) * st + _dot(_t128(vh).astype(BF16), ke)
        outs.append(o)
    o_all = jnp.concatenate(outs, axis=-1)

    @pl.when(d == 0)
    def _():
        of_scr[c] = o_all

    @pl.when(d == 1)
    def _():
        tot = o_all + of_scr[n - 1 - c]
        res = []
        for h in range(C_H):
            sl = slice(h * C_DV, (h + 1) * C_DV)
            t = tot[:, sl]
            y = t * lax.rsqrt(jnp.mean(t * t, axis=-1, keepdims=True) + EPS) * ng_ref[:, sl]
            res.append(y * _silu(g_ref[:, sl]))
        o_ref[...] = jnp.concatenate(res, axis=-1).astype(o_ref.dtype)

    @pl.when(jnp.logical_and(c == n - 1, d == 0))
    def _():
        for h in range(C_H):
            sf_ref[h] = _t128(st_scr[h])

    @pl.when(jnp.logical_and(c == n - 1, d == 1))
    def _():
        for h in range(C_H):
            sb_ref[h] = _t128(st_scr[h])


def _gla(z, B, T, w_a2, b_a, s0f, s0b, norm_g):
    C = GLA_CHUNK
    n = T // C
    HK = C_H * C_DK
    HV = C_H * C_DV
    mat, code, ones = _gla_tables(C)
    nr = mat.shape[1] // C
    wg = jnp.zeros((2, LANES, HK), F32)
    for dr in range(2):
        wg = wg.at[dr, dr * C_RANK:(dr + 1) * C_RANK, :].set(w_a2[dr])

    def row(b, d, c):
        return b * n + c + d * (n - 1 - 2 * c)

    st_spec = pl.BlockSpec((None, C_H, C_DK, C_DV), lambda b, d, c: (b, 0, 0, 0))
    st_shape = jax.ShapeDtypeStruct((B, C_H, C_DK, C_DV), F32)
    return pl.pallas_call(
        functools.partial(_gla_kernel, n=n, C=C),
        grid=(B, 2, n),
        in_specs=[pl.BlockSpec((C, HK), lambda b, d, c: (row(b, d, c), 0)),
                  pl.BlockSpec((C, HK), lambda b, d, c: (row(b, d, c), 1)),
                  pl.BlockSpec((C, HV), lambda b, d, c: (row(b, d, c), 1)),
                  pl.BlockSpec((C, HV), lambda b, d, c: (row(b, d, c), 2)),
                  pl.BlockSpec((C, LANES), lambda b, d, c: (row(b, d, c), (2 * HK + 2 * HV) // LANES)),
                  pl.BlockSpec((None, LANES, HK), lambda b, d, c: (d, 0, 0)),
                  pl.BlockSpec((None, 1, HK), lambda b, d, c: (d, 0, 0)),
                  pl.BlockSpec((None, nr * C, C), lambda b, d, c: (d, 0, 0)),
                  pl.BlockSpec((None, C, C), lambda b, d, c: (d, 0, 0)),
                  pl.BlockSpec((SUB * LANES, C), lambda b, d, c: (0, 0)),
                  st_spec, st_spec,
                  pl.BlockSpec((1, HV), lambda b, d, c: (0, 0))],
        out_specs=[pl.BlockSpec((C, HV), lambda b, d, c: (b * n + (n - 1) - d * c, 0)),
                   st_spec, st_spec],
        out_shape=[jax.ShapeDtypeStruct((B * T, HV), BF16), st_shape, st_shape],
        scratch_shapes=[pltpu.VMEM((C_H, C_DV, C_DK), F32), pltpu.VMEM((n, C, HV), F32)],
        compiler_params=_params("arbitrary", "arbitrary", "arbitrary"),
        name="gla",
    )(z, z, z, z, z, wg, b_a.reshape(2, 1, HK), mat, code, ones, s0f, s0b, norm_g.reshape(1, HV))


def _run_stream(x, B, T, mods, ctx, p):
    N, D = x.shape
    rpg = N // mods[0].shape[0]
    TM = min(1024, rpg)

    z = _norm_mm(x, p['norm1_g'][0], mods[0], (0, 1), p['even_w_in'][0], 1280, TM, rpg)
    if ctx is None:
        s0 = jnp.zeros((B, A_H, A_DK, A_DV), F32)
        a_f0, a_b0, cache_k, cache_v = s0, s0, None, None
    else:
        cache_k, cache_v, a_f0, a_b0 = ctx[0], ctx[1], ctx[2], ctx[3]
    o_a, a_sf, a_sb = _retention(z, B, T, p['a_log_gamma'][0], a_f0, a_b0, p['a_norm_g'][0])
    qpad, k_norm, k_rot, v_bf = _bprep(z, T, p['b_q_g'][0], p['b_k_g'][0], rope=ctx is not None)
    o_b = _attention(qpad, k_rot, v_bf, B, T, cache_k, cache_v)
    x = _proj_res(x, mods[0], 2, [o_a, o_b], p['even_w_out'][0], rpg)
    x = _ffn(x, p['norm2_g'][0], mods[0], p['ff_w1'][0], p['ff_w3'][0], p['ff_w2'][0], rpg)

    w_in = p['odd_w_in'][0]
    pad = (-w_in.shape[1]) % 640
    z1 = _norm_mm(x, p['norm1_g'][1], mods[1], (0, 1), jnp.pad(w_in, ((0, 0), (0, pad))), 640, TM, rpg)
    if ctx is None:
        s0 = jnp.zeros((B, C_H, C_DK, C_DV), F32)
        c_f0, c_b0 = s0, s0
    else:
        c_f0, c_b0 = ctx[4], ctx[5]
    o_c, c_sf, c_sb = _gla(z1, B, T, p['c_w_a2'][0], p['c_b_a'][0], c_f0, c_b0, p['c_norm_g'][0])
    x = _proj_res(x, mods[1], 2, [o_c], p['odd_w_out'][0], rpg)
    v_raw = z[:, EVEN_V_OFF:EVEN_V_OFF + B_HKV * B_HD]
    return x, (k_norm, v_raw, a_sf, a_sb, c_sf, c_sb)


EVEN_V_OFF = A_H * (2 * A_DK + 2 * A_DV) + (B_H + B_HKV) * B_HD


def kernel(x_prompt, x_sample, c, cache_b_k, cache_b_v, state_a_fwd, state_a_bwd, state_c_fwd, state_c_bwd,
           c_ctx, w_mod, b_mod, norm1_g, norm2_g, final_g, even_w_in, even_w_out, a_log_gamma, a_norm_g,
           b_q_g, b_k_g, odd_w_in, c_w_a2, c_b_a, c_norm_g, odd_w_out, ff_w1, ff_w3, ff_w2,
           router_w, moe_w1, moe_w3, moe_w2):
    Bp, Tp, D = x_prompt.shape
    Bs, Ts, _ = x_sample.shape
    L = w_mod.shape[0]
    assert L == 2 and even_w_in.shape[0] == 1 and odd_w_in.shape[0] == 1
    p = dict(norm1_g=norm1_g, norm2_g=norm2_g, final_g=final_g, even_w_in=even_w_in, even_w_out=even_w_out,
             a_log_gamma=a_log_gamma, a_norm_g=a_norm_g, b_q_g=b_q_g, b_k_g=b_k_g, odd_w_in=odd_w_in,
             c_w_a2=c_w_a2, c_b_a=c_b_a, c_norm_g=c_norm_g, odd_w_out=odd_w_out, ff_w1=ff_w1, ff_w3=ff_w3,
             ff_w2=ff_w2, router_w=router_w, moe_w1=moe_w1, moe_w3=moe_w3, moe_w2=moe_w2)

    rows = 8
    conds = jnp.concatenate([c_ctx[None, :], c, jnp.zeros((rows - 1 - Bs, D), F32)], axis=0)
    mod = _modulation(conds, w_mod, b_mod).reshape(L, rows, 6, 1, D)
    mods_p = [mod[l, 0:1] for l in range(L)]
    mods_s = [mod[l, 1:1 + Bs] for l in range(L)]

    x_p, kept = _run_stream(x_prompt.reshape(Bp * Tp, D), Bp, Tp, mods_p, None, p)
    nk = B_HKV * B_HD
    ctx = (cache_b_k[:, 0].reshape(Bs, -1, nk), cache_b_v[:, 0].reshape(Bs, -1, nk),
           state_a_fwd[:, 0], state_a_bwd[:, 0], state_c_fwd[:, 0], state_c_bwd[:, 0])
    x_s, _ = _run_stream(x_sample.reshape(Bs * Ts, D), Bs, Ts, mods_s, ctx, p)
    y_p, y_s = _moe(x_p, x_s, norm2_g[1], mod[1, 0:1 + Bs], router_w[0], moe_w1[0], moe_w3[0], moe_w2[0],
                    final_g, Ts)

    k_norm, v_raw, a_sf, a_sb, c_sf, c_sb = kept
    return (y_p.reshape(Bp, Tp, D), y_s.reshape(Bs, Ts, D),
            k_norm.reshape(Bp, 1, Tp, B_HKV, B_HD), v_raw.reshape(Bp, 1, Tp, B_HKV, B_HD),
            a_sf[:, None], a_sb[:, None], c_sf[:, None], c_sb[:, None])
```

```python
import functools

import numpy as np
import jax
import jax.numpy as jnp
from jax import lax
from jax.experimental import pallas as pl
from jax.experimental.pallas import tpu as pltpu

F32 = jnp.float32
BF16 = jnp.bfloat16
EPS = 1e-6
HIGHEST = lax.Precision.HIGHEST
LOG2E = 1.4426950408889634

VMEM_LIMIT_BYTES = 56 * 1024 * 1024

A_H, A_DK, A_DV = 4, 128, 256
B_H, B_HKV, B_HD = 8, 2, 64
C_H, C_DK, C_DV, C_RANK = 4, 128, 256, 16
C_TAU = 16.0
GRID_W = 64
ROPE_THETA = 10000.0
N_EXPERTS = 8
LANES = 128
SUB = 8
RET_CHUNK = 128
GLA_CHUNK = 128
Q_TILE = 128


def _params(*sem):
    return pltpu.CompilerParams(dimension_semantics=sem, vmem_limit_bytes=VMEM_LIMIT_BYTES)


def _dot(a, b):
    return jnp.dot(a, b, preferred_element_type=F32)


def _dot_nt(a, b):
    return lax.dot_general(a, b, (((1,), (1,)), ((), ())), preferred_element_type=F32)


def _silu(x):
    return x * jax.nn.sigmoid(x)


def _norm_mod(x, g, sh, sc):
    r = lax.rsqrt(jnp.mean(x * x, axis=-1, keepdims=True) + EPS)
    return (x * r * g) * (1.0 + sc) + sh


def _mod_kernel(c_ref, w_ref, b_ref, o_ref):
    c = c_ref[...]
    o_ref[...] = jnp.dot(_silu(c), w_ref[...], precision=HIGHEST, preferred_element_type=F32) + b_ref[...]


def _modulation(conds, w_mod, b_mod):
    L, D, D6 = w_mod.shape
    R = conds.shape[0]
    TN = 1024
    return pl.pallas_call(
        _mod_kernel,
        grid=(L, D6 // TN),
        in_specs=[pl.BlockSpec((R, D), lambda l, j: (0, 0)),
                  pl.BlockSpec((None, D, TN), lambda l, j: (l, 0, j)),
                  pl.BlockSpec((None, 1, TN), lambda l, j: (l, 0, j))],
        out_specs=pl.BlockSpec((None, R, TN), lambda l, j: (l, 0, j)),
        out_shape=jax.ShapeDtypeStruct((L, R, D6), F32),
        compiler_params=_params("arbitrary", "arbitrary"),
        name="modulation",
    )(conds, w_mod, b_mod.reshape(L, 1, D6))


def _mod_spec(part, D, TM, rows_per_group, axis):
    def idx(*g):
        return ((g[axis] * TM) // rows_per_group, part, 0, 0)
    return pl.BlockSpec((None, None, 1, D), idx)


def _norm_mm_kernel(x_ref, g_ref, sh_ref, sc_ref, w_ref, o_ref, wbf_ref):
    @pl.when(pl.program_id(1) == 0)
    def _():
        wbf_ref[...] = w_ref[...].astype(BF16)

    h = _norm_mod(x_ref[...], g_ref[...], sh_ref[...], sc_ref[...])
    o_ref[...] = _dot(h.astype(BF16), wbf_ref[...]).astype(o_ref.dtype)


def _norm_mm(x, g, mod, parts, w, TN, TM, rows_per_group):
    N, D = x.shape
    NO = w.shape[1]
    return pl.pallas_call(
        _norm_mm_kernel,
        grid=(NO // TN, N // TM),
        in_specs=[pl.BlockSpec((TM, D), lambda j, i: (i, 0)),
                  pl.BlockSpec((1, D), lambda j, i: (0, 0)),
                  _mod_spec(parts[0], D, TM, rows_per_group, 1),
                  _mod_spec(parts[1], D, TM, rows_per_group, 1),
                  pl.BlockSpec((D, TN), lambda j, i: (0, j))],
        out_specs=pl.BlockSpec((TM, TN), lambda j, i: (i, j)),
        out_shape=jax.ShapeDtypeStruct((N, NO), F32),
        scratch_shapes=[pltpu.VMEM((D, TN), BF16)],
        compiler_params=_params("arbitrary", "arbitrary"),
        name="norm_mm",
    )(x, g.reshape(1, D), mod, mod, w)


def _ret_kernel(lg_ref, q_ref, k_ref, v_ref, ag_ref, s0f_ref, s0b_ref, ng_ref,
                o_ref, sf_ref, sb_ref, s_scr, of_scr, *, n, C):
    d = pl.program_id(1)
    c = pl.program_id(2)

    @pl.when(jnp.logical_and(c == 0, d == 0))
    def _():
        s_scr[...] = s0f_ref[...]

    @pl.when(jnp.logical_and(c == 0, d == 1))
    def _():
        s_scr[...] = s0b_ref[...]

    df = d.astype(F32)
    sgn = 1.0 - 2.0 * df
    ii = lax.broadcasted_iota(jnp.int32, (C, C), 0).astype(F32)
    jj = lax.broadcasted_iota(jnp.int32, (C, C), 1).astype(F32)
    dd = (ii - jj) * sgn
    feeds = dd >= 0.0
    ddc = jnp.maximum(dd, 0.0)
    ri = lax.broadcasted_iota(jnp.int32, (C, 1), 0).astype(F32)
    pos_q = (ri + 1.0) + df * (C - 2.0 * ri - 1.0)
    pos_k = (C - 1.0 - ri) + df * (2.0 * ri - C + 1.0)
    chunk_len = jnp.full((1, A_DV), float(C), F32)

    outs = []
    for h in range(A_H):
        lg = lg_ref[d, h]
        dmask = jnp.where(feeds, jnp.exp(lg * ddc), 0.0)
        qh = q_ref[:, h * A_DK:(h + 1) * A_DK] * (A_DK ** -0.5)
        kh = k_ref[:, h * A_DK:(h + 1) * A_DK]
        vh = v_ref[:, h * A_DV:(h + 1) * A_DV].astype(BF16)
        s = s_scr[h]
        att = _dot_nt(qh.astype(BF16), kh.astype(BF16)) * dmask
        o = _dot(att.astype(BF16), vh) + _dot((qh * jnp.exp(lg * pos_q)).astype(BF16), s.astype(BF16))
        kd = kh * jnp.exp(lg * pos_k)
        s_scr[h] = jnp.exp(lg * chunk_len) * s + _dot(kd.T.astype(BF16), vh)
        outs.append(o)
    o_all = jnp.concatenate(outs, axis=-1)

    @pl.when(d == 0)
    def _():
        of_scr[c] = o_all

    @pl.when(d == 1)
    def _():
        tot = o_all + of_scr[n - 1 - c]
        res = []
        for h in range(A_H):
            sl = slice(h * A_DV, (h + 1) * A_DV)
            t = tot[:, sl]
            dev = t - jnp.mean(t, axis=-1, keepdims=True)
            y = dev * lax.rsqrt(jnp.mean(dev * dev, axis=-1, keepdims=True) + EPS) * ng_ref[:, sl]
            res.append(y * _silu(ag_ref[:, sl]))
        o_ref[...] = jnp.concatenate(res, axis=-1).astype(o_ref.dtype)

    @pl.when(jnp.logical_and(c == n - 1, d == 0))
    def _():
        sf_ref[...] = s_scr[...]

    @pl.when(jnp.logical_and(c == n - 1, d == 1))
    def _():
        sb_ref[...] = s_scr[...]


def _retention(z, B, T, log_gamma, s0f, s0b, norm_g):
    C = RET_CHUNK
    n = T // C
    HV = A_H * A_DV

    def row(b, d, c):
        return b * n + c + d * (n - 1 - 2 * c)

    st_spec = pl.BlockSpec((None, A_H, A_DK, A_DV), lambda b, d, c: (b, 0, 0, 0))
    st_shape = jax.ShapeDtypeStruct((B, A_H, A_DK, A_DV), F32)
    return pl.pallas_call(
        functools.partial(_ret_kernel, n=n, C=C),
        grid=(B, 2, n),
        in_specs=[pl.BlockSpec(memory_space=pltpu.SMEM),
                  pl.BlockSpec((C, 512), lambda b, d, c: (row(b, d, c), 0)),
                  pl.BlockSpec((C, 512), lambda b, d, c: (row(b, d, c), 1)),
                  pl.BlockSpec((C, HV), lambda b, d, c: (row(b, d, c), 1)),
                  pl.BlockSpec((C, HV), lambda b, d, c: (row(b, d, c), 2)),
                  st_spec, st_spec,
                  pl.BlockSpec((1, HV), lambda b, d, c: (0, 0))],
        out_specs=[pl.BlockSpec((C, HV), lambda b, d, c: (b * n + (n - 1) - d * c, 0)),
                   st_spec, st_spec],
        out_shape=[jax.ShapeDtypeStruct((B * T, HV), BF16), st_shape, st_shape],
        scratch_shapes=[pltpu.VMEM((A_H, A_DK, A_DV), F32), pltpu.VMEM((n, C, HV), F32)],
        compiler_params=_params("arbitrary", "arbitrary", "arbitrary"),
        name="retention",
    )(log_gamma, z, z, z, z, s0f, s0b, norm_g.reshape(1, HV))


def _group_sum_matrix(width, group):
    i = np.arange(width)
    return jnp.asarray((i[:, None] // group == i[None, :] // group).astype(np.float32), dtype=BF16)


def _q_pad_matrix():
    m = np.zeros((B_H * B_HD, B_H * LANES), np.float32)
    g = B_H // B_HKV
    for h in range(B_H):
        for t in range(B_HD):
            m[h * B_HD + t, h * LANES + (h // g) * B_HD + t] = 1.0
    return jnp.asarray(m, dtype=BF16)


def _rope_tables(T):
    rows = T // GRID_W
    row = np.repeat(np.arange(rows, dtype=np.float64), GRID_W)
    col = np.tile(np.arange(GRID_W, dtype=np.float64), rows)
    nq = B_HD // 4
    inv = ROPE_THETA ** (-np.arange(nq, dtype=np.float64) / nq)
    ang = np.concatenate([row[:, None] * inv, col[:, None] * inv], axis=-1)
    cos = np.repeat(np.cos(ang), 2, axis=-1)
    sin = np.repeat(np.sin(ang), 2, axis=-1)
    sign = np.tile(np.array([-1.0, 1.0]), B_HD // 2)
    reps = LANES // B_HD
    return (jnp.asarray(np.tile(cos, (1, reps)), dtype=F32),
            jnp.asarray(np.tile(sin * sign, (1, reps)), dtype=F32))


def _group_rmsnorm(x, gsum, g):
    x2 = x * x
    hi = x2.astype(BF16)
    lo = (x2 - hi.astype(F32)).astype(BF16)
    ss = _dot(hi, gsum) + _dot(lo, gsum)
    return x * lax.rsqrt(ss * (1.0 / B_HD) + EPS) * g


def _rotate_pairs(x, cos, sin_signed):
    n = x.shape[1]
    lane = lax.broadcasted_iota(jnp.int32, x.shape, 1)
    partner = jnp.where(lane % 2 == 0, pltpu.roll(x, n - 1, 1), pltpu.roll(x, 1, 1))
    reps = n // LANES
    if reps > 1:
        cos = jnp.concatenate([cos] * reps, axis=1)
        sin_signed = jnp.concatenate([sin_signed] * reps, axis=1)
    return x * cos + partner * sin_signed


def _bprep_kernel(z_ref, qg_ref, kg_ref, cos_ref, sin_ref, gq_ref, gk_ref, pad_ref,
                  qpad_ref, kn_ref, kr_ref, vb_ref, *, rope):
    nq = B_H * B_HD
    nk = B_HKV * B_HD
    qn = _group_rmsnorm(z_ref[:, 0:nq], gq_ref[...], qg_ref[...])
    kn = _group_rmsnorm(z_ref[:, nq:nq + nk], gk_ref[...], kg_ref[...])
    kn_ref[...] = kn
    if rope:
        qn = _rotate_pairs(qn, cos_ref[...], sin_ref[...])
        kn = _rotate_pairs(kn, cos_ref[...], sin_ref[...])
    kr_ref[...] = kn.astype(BF16)
    vb_ref[...] = z_ref[:, nq + nk:nq + 2 * nk].astype(BF16)
    qs = (qn * (B_HD ** -0.5 * LOG2E)).astype(BF16)
    qpad_ref[...] = _dot(qs, pad_ref[...]).astype(BF16)


def _bprep(z, T, q_g, k_g, rope):
    N = z.shape[0]
    TM = min(512, T)
    nq = B_H * B_HD
    nk = B_HKV * B_HD
    width = nq + 2 * nk
    col = (A_H * (2 * A_DK + 2 * A_DV)) // width
    cos, sin = _rope_tables(T if rope else TM)
    nt = T // TM if rope else 1
    const = lambda i: (0, 0)
    return pl.pallas_call(
        functools.partial(_bprep_kernel, rope=rope),
        grid=(N // TM,),
        in_specs=[pl.BlockSpec((TM, width), lambda i: (i, col)),
                  pl.BlockSpec((1, nq), const),
                  pl.BlockSpec((1, nk), const),
                  pl.BlockSpec((TM, LANES), lambda i: (i % nt, 0)),
                  pl.BlockSpec((TM, LANES), lambda i: (i % nt, 0)),
                  pl.BlockSpec((nq, nq), const),
                  pl.BlockSpec((nk, nk), const),
                  pl.BlockSpec((nq, B_H * LANES), const)],
        out_specs=[pl.BlockSpec((TM, B_H * LANES), lambda i: (i, 0)),
                   pl.BlockSpec((TM, nk), lambda i: (i, 0)),
                   pl.BlockSpec((TM, nk), lambda i: (i, 0)),
                   pl.BlockSpec((TM, nk), lambda i: (i, 0))],
        out_shape=[jax.ShapeDtypeStruct((N, B_H * LANES), BF16),
                   jax.ShapeDtypeStruct((N, nk), F32),
                   jax.ShapeDtypeStruct((N, nk), BF16),
                   jax.ShapeDtypeStruct((N, nk), BF16)],
        compiler_params=_params("arbitrary"),
        name="attn_prep",
    )(z, jnp.tile(q_g, B_H).reshape(1, nq), jnp.tile(k_g, B_HKV).reshape(1, nk), cos, sin,
      _group_sum_matrix(nq, B_HD), _group_sum_matrix(nk, B_HD), _q_pad_matrix())


def _lane_fold(x, op):
    acc = x[:, 0:LANES]
    for j in range(1, x.shape[1] // LANES):
        acc = op(acc, x[:, j * LANES:(j + 1) * LANES])
    return acc


def _attn_kernel(*refs, has_cache, kc, nq):
    if has_cache:
        q_ref, k_ref, v_ref, ck_ref, cv_ref, o_ref, s_scr, m_scr, mprev_scr, l_scr, acc_scr = refs
        ncache = ck_ref.shape[0] // kc
    else:
        q_ref, k_ref, v_ref, o_ref, s_scr, m_scr, mprev_scr, l_scr, acc_scr = refs
        ncache = 0
    i = pl.program_id(1)
    tq = q_ref.shape[0]
    nlat = k_ref.shape[0] // kc
    ngrp = kc // LANES

    def score(c, kblk):
        q = jnp.concatenate([q_ref[:, h * LANES:(h + 1) * LANES] for h in range(B_H)], axis=0)
        s = _dot_nt(q, kblk)
        s_scr[c] = s
        m_scr[...] = jnp.maximum(m_scr[...], _lane_fold(s, jnp.maximum))

    def weight(c, vblk):
        s = s_scr[c]
        mp = mprev_scr[...]
        ps = [jnp.exp2(s[:, j * LANES:(j + 1) * LANES] - mp) for j in range(ngrp)]
        tot = ps[0]
        for pj in ps[1:]:
            tot = tot + pj
        l_scr[...] += tot
        acc_scr[...] += _dot(jnp.concatenate(ps, axis=1).astype(BF16), vblk)

    def run(do_weight, do_score):
        def unit(c, kblk, vblk):
            if do_weight:
                weight(c, vblk())
            if do_score:
                score(c, kblk())

        for c in range(ncache):
            unit(c, lambda: ck_ref[c * kc:(c + 1) * kc, :].astype(BF16),
                 lambda: cv_ref[c * kc:(c + 1) * kc, :].astype(BF16))

        def body(c, carry):
            rows = pl.ds(pl.multiple_of(c * kc, kc), kc)
            unit(ncache + c, lambda: k_ref[rows, :], lambda: v_ref[rows, :])
            return carry
        lax.fori_loop(0, nlat, body, 0)

    @pl.when(i < nq)
    def _():
        m_scr[...] = jnp.full(m_scr.shape, -jnp.inf, F32)

    @pl.when(i > 0)
    def _():
        l_scr[...] = jnp.zeros_like(l_scr)
        acc_scr[...] = jnp.zeros_like(acc_scr)

    @pl.when(i == 0)
    def _():
        run(False, True)

    @pl.when(jnp.logical_and(i > 0, i < nq))
    def _():
        run(True, True)

    @pl.when(i == nq)
    def _():
        run(True, False)

    @pl.when(i > 0)
    def _():
        r_all = acc_scr[...] / jnp.sum(l_scr[...], axis=-1, keepdims=True)
        g = B_H // B_HKV
        lane = lax.broadcasted_iota(jnp.int32, (tq, LANES), 1)
        outs = []
        for j in range(B_H // 2):
            pair = []
            for half in range(2):
                h = 2 * j + half
                r = r_all[h * tq:(h + 1) * tq, :]
                if h // g != half:
                    r = pltpu.roll(r, B_HD, 1)
                pair.append(r)
            outs.append(jnp.where(lane < B_HD, pair[0], pair[1]))
        o_ref[...] = jnp.concatenate(outs, axis=-1).astype(o_ref.dtype)

    @pl.when(i < nq)
    def _():
        mprev_scr[...] = jnp.broadcast_to(jnp.max(m_scr[...], axis=-1, keepdims=True), mprev_scr.shape)


def _attention(qpad, kr, vb, B, T, cache_k, cache_v):
    has_cache = cache_k is not None
    TQ = Q_TILE
    nq = T // TQ
    nk = B_HKV * B_HD
    in_specs = [pl.BlockSpec((TQ, B_H * LANES), lambda b, i: (b * nq + jnp.minimum(i, nq - 1), 0)),
                pl.BlockSpec((T, nk), lambda b, i: (b, 0)),
                pl.BlockSpec((T, nk), lambda b, i: (b, 0))]
    args = [qpad, kr, vb]
    kc = min(512, T)
    nchunks = T // kc
    if has_cache:
        P = cache_k.shape[1]
        assert P % kc == 0
        nchunks += P // kc
        in_specs += [pl.BlockSpec((None, P, nk), lambda b, i: (b, 0, 0))] * 2
        args += [cache_k, cache_v]
    R = B_H * TQ
    return pl.pallas_call(
        functools.partial(_attn_kernel, has_cache=has_cache, kc=kc, nq=nq),
        grid=(B, nq + 1),
        in_specs=in_specs,
        out_specs=pl.BlockSpec((TQ, B_H * B_HD), lambda b, i: (b * nq + jnp.maximum(i - 1, 0), 0)),
        out_shape=jax.ShapeDtypeStruct((B * T, B_H * B_HD), BF16),
        scratch_shapes=[pltpu.VMEM((nchunks, R, kc), F32)] + [pltpu.VMEM((R, LANES), F32)] * 4,
        compiler_params=_params("arbitrary", "arbitrary"),
        name="attention",
    )(*args)


def _proj_res_kernel(*refs, n_in):
    x_ref, gate_ref = refs[0], refs[1]
    o_refs = refs[2:2 + n_in]
    w_refs = refs[2 + n_in:2 + 2 * n_in]
    out_ref = refs[2 + 2 * n_in]
    wbf_refs = refs[3 + 2 * n_in:]

    @pl.when(pl.program_id(0) == 0)
    def _():
        for w_ref, wbf_ref in zip(w_refs, wbf_refs):
            wbf_ref[...] = w_ref[...].astype(BF16)

    acc = _dot(o_refs[0][...], wbf_refs[0][...])
    for o_ref, wbf_ref in zip(o_refs[1:], wbf_refs[1:]):
        acc = acc + _dot(o_ref[...], wbf_ref[...])
    out_ref[...] = x_ref[...] + gate_ref[...] * acc


def _proj_res(x, mod, part, acts, w, rows_per_group):
    N, D = x.shape
    TM = min(512, rows_per_group)
    n_in = len(acts)
    widths = [a.shape[1] for a in acts]
    offs = np.cumsum([0] + widths[:-1]).tolist()
    in_specs = [pl.BlockSpec((TM, D), lambda i: (i, 0)),
                _mod_spec(part, D, TM, rows_per_group, 0)]
    in_specs += [pl.BlockSpec((TM, wd), lambda i: (i, 0)) for wd in widths]
    in_specs += [pl.BlockSpec((wd, D), functools.partial(lambda i, blk: (blk, 0), blk=off // wd))
                 for wd, off in zip(widths, offs)]
    return pl.pallas_call(
        functools.partial(_proj_res_kernel, n_in=n_in),
        grid=(N // TM,),
        in_specs=in_specs,
        out_specs=pl.BlockSpec((TM, D), lambda i: (i, 0)),
        out_shape=jax.ShapeDtypeStruct((N, D), F32),
        scratch_shapes=[pltpu.VMEM((wd, D), BF16) for wd in widths],
        compiler_params=_params("arbitrary"),
        name="proj_residual",
    )(x, mod, *acts, *([w] * n_in))


def _ffn_kernel(x_ref, g_ref, sh_ref, sc_ref, gate_ref, w1_ref, w3_ref, w2_ref, out_ref, h_scr, acc_scr, *, nf):
    f = pl.program_id(1)

    @pl.when(f == 0)
    def _():
        h_scr[...] = _norm_mod(x_ref[...], g_ref[...], sh_ref[...], sc_ref[...]).astype(BF16)
        acc_scr[...] = jnp.zeros_like(acc_scr)

    h = h_scr[...]
    a = _dot(h, w1_ref[...].astype(BF16))
    b = _dot(h, w3_ref[...].astype(BF16))
    acc_scr[...] += _dot((_silu(a) * b).astype(BF16), w2_ref[...].astype(BF16))

    @pl.when(f == nf - 1)
    def _():
        out_ref[...] = x_ref[...] + gate_ref[...] * acc_scr[...]


def _ffn(x, g, mod, w1, w3, w2, rows_per_group):
    N, D = x.shape
    FF = w1.shape[1]
    TM, TF = min(1024, rows_per_group), 256
    nf = FF // TF
    return pl.pallas_call(
        functools.partial(_ffn_kernel, nf=nf),
        grid=(N // TM, nf),
        in_specs=[pl.BlockSpec((TM, D), lambda i, f: (i, 0)),
                  pl.BlockSpec((1, D), lambda i, f: (0, 0)),
                  _mod_spec(3, D, TM, rows_per_group, 0),
                  _mod_spec(4, D, TM, rows_per_group, 0),
                  _mod_spec(5, D, TM, rows_per_group, 0),
                  pl.BlockSpec((D, TF), lambda i, f: (0, f)),
                  pl.BlockSpec((D, TF), lambda i, f: (0, f)),
                  pl.BlockSpec((TF, D), lambda i, f: (f, 0))],
        out_specs=pl.BlockSpec((TM, D), lambda i, f: (i, 0)),
        out_shape=jax.ShapeDtypeStruct((N, D), F32),
        scratch_shapes=[pltpu.VMEM((TM, D), BF16), pltpu.VMEM((TM, D), F32)],
        compiler_params=_params("arbitrary", "arbitrary"),
        name="ffn",
    )(x, g.reshape(1, D), mod, mod, mod, w1, w3, w2)


MOE_SB = 1024
MOE_TRG = 256
MOE_TR = 1024


def _two_stream_specs(shape, ntp, ax=0):
    def idx_p(*g):
        return (jnp.minimum(g[ax], ntp - 1), 0)

    def idx_s(*g):
        return (jnp.maximum(g[ax] - ntp, 0), 0)
    return pl.BlockSpec(shape, idx_p), pl.BlockSpec(shape, idx_s)


def _pool_mod_spec(part, D, TM, ntp, rows_per_group):
    def idx(i, *_):
        return (jnp.where(i < ntp, 0, 1 + ((i - ntp) * TM) // rows_per_group), part, 0, 0)
    return pl.BlockSpec((None, None, 1, D), idx)


def _route_kernel(xp_ref, xs_ref, g_ref, sh_ref, sc_ref, rw_ref, tri_ref, h_ref, info_ref, infot_ref, cum_ref,
                  carry_scr, *, ntp):
    i = pl.program_id(0)

    @pl.when(i == 0)
    def _():
        carry_scr[...] = jnp.zeros_like(carry_scr)

    x = jnp.where(i < ntp, xp_ref[...], xs_ref[...])
    h = _norm_mod(x, g_ref[...], sh_ref[...], sc_ref[...])
    h_ref[...] = h.astype(BF16)
    lane = lax.broadcasted_iota(jnp.int32, (x.shape[0], LANES), 1).astype(F32)
    logits = jnp.dot(h, rw_ref[...], precision=HIGHEST, preferred_element_type=F32)
    logits = jnp.where(lane < N_EXPERTS, logits, -jnp.inf)
    m1 = jnp.max(logits, axis=-1, keepdims=True)
    i1 = jnp.min(jnp.where(logits == m1, lane, float(LANES)), axis=-1, keepdims=True)
    rest = jnp.where(lane == i1, -jnp.inf, logits)
    m2 = jnp.max(rest, axis=-1, keepdims=True)
    i2 = jnp.min(jnp.where(rest == m2, lane, float(LANES)), axis=-1, keepdims=True)
    e2 = jnp.exp(m2 - m1)
    w1 = 1.0 / (1.0 + e2)
    w2 = e2 / (1.0 + e2)
    ind = jnp.where(jnp.logical_or(lane == i1, lane == i2), 1.0, 0.0)
    before = _dot(tri_ref[...], ind.astype(BF16)) + carry_scr[...]
    r1 = jnp.sum(jnp.where(lane == i1, before, 0.0), axis=-1, keepdims=True)
    r2 = jnp.sum(jnp.where(lane == i2, before, 0.0), axis=-1, keepdims=True)
    total = carry_scr[...] + jnp.sum(ind, axis=0, keepdims=True)
    carry_scr[...] = total
    cum_ref[...] = total
    info = jnp.where(lane == 0.0, i1, jnp.where(lane == 1.0, i2, jnp.where(lane == 2.0, w1, jnp.where(
        lane == 3.0, w2, jnp.where(lane == 4.0, r1, jnp.where(lane == 5.0, r2, 0.0))))))
    info_ref[...] = info[:, 0:SUB]
    info_t = jnp.concatenate([info[r:r + LANES, :].T for r in range(0, info.shape[0], LANES)], axis=1)
    infot_ref[...] = info_t[0:SUB, :]


def _moe_route(xp, xs, g, mod, router_w, rows_per_group):
    Np, D = xp.shape
    N = Np + xs.shape[0]
    TM = MOE_SB
    ntp = Np // TM
    nt = N // TM
    rw = jnp.pad(router_w, ((0, 0), (0, LANES - router_w.shape[1])))
    tri = jnp.asarray(np.tril(np.ones((TM, TM), np.float32), -1), dtype=BF16)
    xp_spec, xs_spec = _two_stream_specs((TM, D), ntp)
    return pl.pallas_call(
        functools.partial(_route_kernel, ntp=ntp),
        grid=(nt,),
        in_specs=[xp_spec, xs_spec,
                  pl.BlockSpec((1, D), lambda i: (0, 0)),
                  _pool_mod_spec(3, D, TM, ntp, rows_per_group),
                  _pool_mod_spec(4, D, TM, ntp, rows_per_group),
                  pl.BlockSpec((D, LANES), lambda i: (0, 0)),
                  pl.BlockSpec((TM, TM), lambda i: (0, 0))],
        out_specs=[pl.BlockSpec((TM, D), lambda i: (i, 0)),
                   pl.BlockSpec((TM, SUB), lambda i: (i, 0)),
                   pl.BlockSpec((SUB, TM), lambda i: (0, i)),
                   pl.BlockSpec((None, 1, LANES), lambda i: (i, 0, 0))],
        out_shape=[jax.ShapeDtypeStruct((N, D), BF16),
                   jax.ShapeDtypeStruct((N, SUB), F32),
                   jax.ShapeDtypeStruct((SUB, N), F32),
                   jax.ShapeDtypeStruct((nt, 1, LANES), F32)],
        scratch_shapes=[pltpu.VMEM((1, LANES), F32)],
        compiler_params=_params("arbitrary"),
        name="moe_route",
    )(xp, xs, g.reshape(1, D), mod, mod, rw, tri)


def _moe_plan(info, info_t, cum, N):
    E, SB, TRG, TR = N_EXPERTS, MOE_SB, MOE_TRG, MOE_TR
    NB = N // SB
    rmax = 2 * N + E * TR
    RG, RT = rmax // TRG, rmax // TR
    PMAX = RG + E * NB
    i32 = jnp.int32
    cum_e = cum[:, 0, :E].astype(i32).T
    cnt = cum_e[:, -1]
    tiles = (cnt + TR - 1) // TR
    start = TR * (jnp.cumsum(tiles) - tiles)

    startf = start.astype(F32)

    def region_start(e):
        out = jnp.zeros_like(e)
        for k in range(E):
            out = jnp.where(e == float(k), startf[k], out)
        return out

    pos_cols = jnp.concatenate([region_start(info[:, 0:2]) + info[:, 4:6], info[:, 2:4],
                                jnp.zeros((N, 4), F32)], axis=1)
    pos_rows = jnp.concatenate([region_start(info_t[0:2]) + info_t[4:6], jnp.zeros((6, N), F32)],
                               axis=0)

    def region(row0):
        e = jnp.clip(jnp.sum(row0[:, None] >= start[None, :], axis=1) - 1, 0, E - 1)
        return e, row0 - start[e]

    eq, lo = region(jnp.arange(RG, dtype=i32) * TRG)
    hi = jnp.minimum(lo + TRG, cnt[eq])
    first = jnp.sum(cum_e[eq] <= lo[:, None], axis=1)
    last = jnp.sum(cum_e[eq] < hi[:, None], axis=1)
    nblk = jnp.where(hi > lo, last - first + 1, 0)
    pend = jnp.cumsum(nblk)
    npairs = pend[-1]
    p = jnp.arange(PMAX, dtype=i32)
    valid = p < npairs
    pc = jnp.minimum(p, npairs - 1)
    q_of = jnp.minimum(jnp.sum(pend[None, :] <= pc[:, None], axis=1), RG - 1).astype(i32)
    pstart = pend - nblk
    s_of = (first[q_of] + pc - pstart[q_of]).astype(i32)
    g_first = jnp.logical_and(valid, pc == pstart[q_of]).astype(i32)
    gather_plan = (q_of, s_of, valid.astype(i32), g_first)

    order = jnp.argsort(jnp.where(valid, s_of * RG + q_of, jnp.iinfo(jnp.int32).max))
    s2, q2 = s_of[order], q_of[order]
    s2 = jnp.where(valid, s2, s2[npairs - 1])
    q2 = jnp.where(valid, q2, q2[npairs - 1])
    prev = jnp.concatenate([jnp.full((1,), -1, i32), s2[:-1]])
    nxt = jnp.concatenate([s2[1:], jnp.full((1,), -1, i32)])
    c_first = jnp.logical_and(valid, s2 != prev).astype(i32)
    c_last = jnp.logical_and(valid, jnp.logical_or(s2 != nxt, p == npairs - 1)).astype(i32)
    combine_plan = (s2.astype(i32), q2.astype(i32), valid.astype(i32), c_first, c_last)

    te, tlo = region(jnp.arange(RT, dtype=i32) * TR)
    tvalid = jnp.clip(cnt[te] - tlo, 0, TR)
    last_t = jnp.sum(tiles) - 1
    t_idx = jnp.where(tvalid > 0, jnp.arange(RT, dtype=i32), last_t).astype(i32)
    ffn_plan = (t_idx, te[t_idx].astype(i32), tvalid.astype(i32))
    return pos_cols, pos_rows, gather_plan, combine_plan, ffn_plan, rmax


def _moe_gather_kernel(q_ref, s_ref, valid_ref, first_ref, pos_ref, h_ref, out_ref):
    p = pl.program_id(0)

    @pl.when(valid_ref[p] == 1)
    def _():
        rows, toks = out_ref.shape[0], h_ref.shape[0]
        row = (lax.broadcasted_iota(jnp.int32, (rows, 1), 0) + q_ref[p] * rows).astype(F32)
        hit = jnp.logical_or(pos_ref[0:1, :] == row, pos_ref[1:2, :] == row)
        sel = jnp.where(hit, 1.0, 0.0).astype(BF16)
        @pl.when(first_ref[p] == 1)
        def _():
            out_ref[...] = jnp.zeros_like(out_ref)

        out_ref[...] = out_ref[...] + _dot(sel, h_ref[...]).astype(BF16)


def _moe_gather(h, pos_rows, plan, rmax):
    N, D = h.shape
    pmax = plan[0].shape[0]
    return pl.pallas_call(
        _moe_gather_kernel,
        grid_spec=pltpu.PrefetchScalarGridSpec(
            num_scalar_prefetch=4, grid=(pmax,),
            in_specs=[pl.BlockSpec((SUB, MOE_SB), lambda p, q, s, v, f: (0, s[p])),
                      pl.BlockSpec((MOE_SB, D), lambda p, q, s, v, f: (s[p], 0))],
            out_specs=pl.BlockSpec((MOE_TRG, D), lambda p, q, s, v, f: (q[p], 0))),
        out_shape=jax.ShapeDtypeStruct((rmax, D), BF16),
        compiler_params=_params("arbitrary"),
        name="moe_gather",
    )(*plan, pos_rows, h)


def _moe_ffn_kernel(t_ref, e_ref, nv_ref, x_ref, w1_ref, w3_ref, w2_ref, out_ref, acc_scr, w1b, w3b, w2b, *, nf):
    t = pl.program_id(0)
    f = pl.program_id(1)
    nv = nv_ref[t]

    @pl.when(nv > 0)
    def _():
        w1b[...] = w1_ref[...].astype(BF16)
        w3b[...] = w3_ref[...].astype(BF16)
        w2b[...] = w2_ref[...].astype(BF16)

    def block(rows):
        @pl.when(f == 0)
        def _():
            acc_scr[rows, :] = jnp.zeros((rows.stop - rows.start, acc_scr.shape[1]), F32)

        x = x_ref[rows, :]
        a = _dot(x, w1b[...])
        b = _dot(x, w3b[...])
        acc_scr[rows, :] += _dot((_silu(a) * b).astype(BF16), w2b[...])

        @pl.when(f == nf - 1)
        def _():
            out_ref[rows, :] = acc_scr[rows, :].astype(out_ref.dtype)

    nsub = MOE_TR // MOE_TRG
    full = nv > (nsub - 1) * MOE_TRG

    @pl.when(full)
    def _():
        block(slice(0, MOE_TR))

    for sub in range(nsub - 1):
        @pl.when(jnp.logical_and(jnp.logical_not(full), sub * MOE_TRG < nv))
        def _():
            block(slice(sub * MOE_TRG, (sub + 1) * MOE_TRG))


def _moe_ffn(xs, plan, w1, w3, w2):
    rmax, D = xs.shape
    FF = w1.shape[2]
    TF = 256
    nf = FF // TF
    RT = rmax // MOE_TR

    def fidx(t, f, nv):
        return jnp.where(nv[t] > 0, f, nf - 1)

    return pl.pallas_call(
        functools.partial(_moe_ffn_kernel, nf=nf),
        grid_spec=pltpu.PrefetchScalarGridSpec(
            num_scalar_prefetch=3, grid=(RT, nf),
            in_specs=[pl.BlockSpec((MOE_TR, D), lambda t, f, ti, e, nv: (ti[t], 0)),
                      pl.BlockSpec((None, D, TF), lambda t, f, ti, e, nv: (e[t], 0, fidx(t, f, nv))),
                      pl.BlockSpec((None, D, TF), lambda t, f, ti, e, nv: (e[t], 0, fidx(t, f, nv))),
                      pl.BlockSpec((None, TF, D), lambda t, f, ti, e, nv: (e[t], fidx(t, f, nv), 0))],
            out_specs=pl.BlockSpec((MOE_TR, D), lambda t, f, ti, e, nv: (ti[t], 0)),
            scratch_shapes=[pltpu.VMEM((MOE_TR, D), F32), pltpu.VMEM((D, TF), BF16), pltpu.VMEM((D, TF), BF16),
                            pltpu.VMEM((TF, D), BF16)]),
        out_shape=jax.ShapeDtypeStruct((rmax, D), BF16),
        compiler_params=_params("arbitrary", "arbitrary"),
        name="moe_ffn",
    )(*plan, xs, w1, w3, w2)


def _moe_combine_kernel(s_ref, q_ref, valid_ref, first_ref, last_ref, pos_ref, ys_ref, xp_ref, xs_ref, gate_ref,
                        fg_ref, op_ref, os_ref, acc_scr, *, ntp):
    p = pl.program_id(0)

    @pl.when(valid_ref[p] == 1)
    def _():
        rows = ys_ref.shape[0]
        col = (lax.broadcasted_iota(jnp.int32, (1, rows), 1) + q_ref[p] * rows).astype(F32)
        sel = (jnp.where(pos_ref[:, 0:1] == col, pos_ref[:, 2:3], 0.0)
               + jnp.where(pos_ref[:, 1:2] == col, pos_ref[:, 3:4], 0.0)).astype(BF16)
        @pl.when(first_ref[p] == 1)
        def _():
            acc_scr[...] = jnp.zeros_like(acc_scr)

        acc_scr[...] += _dot(sel, ys_ref[...])

        @pl.when(last_ref[p] == 1)
        def _():
            s = s_ref[p]
            x = jnp.where(s < ntp, xp_ref[...], xs_ref[...])
            y = x + gate_ref[...] * acc_scr[...]
            out = y * lax.rsqrt(jnp.mean(y * y, axis=-1, keepdims=True) + EPS) * fg_ref[...]

            @pl.when(s < ntp)
            def _():
                op_ref[...] = out

            @pl.when(s >= ntp)
            def _():
                os_ref[...] = out


def _moe_combine(ys, pos_cols, plan, xp, xs, mod, final_g, rows_per_group):
    Np, D = xp.shape
    Ns = xs.shape[0]
    SB = MOE_SB
    ntp = Np // SB
    pmax = plan[0].shape[0]

    def blk(p, s, *_):
        return s[p]

    def tok_p(p, s, *_):
        return (jnp.minimum(s[p], ntp - 1), 0)

    def tok_s(p, s, *_):
        return (jnp.maximum(s[p] - ntp, 0), 0)

    def gate_idx(p, s, *_):
        return (jnp.where(s[p] < ntp, 0, 1 + ((s[p] - ntp) * SB) // rows_per_group), 5, 0, 0)

    return pl.pallas_call(
        functools.partial(_moe_combine_kernel, ntp=ntp),
        grid_spec=pltpu.PrefetchScalarGridSpec(
            num_scalar_prefetch=5, grid=(pmax,),
            in_specs=[pl.BlockSpec((SB, SUB), lambda p, s, q, *_: (s[p], 0)),
                      pl.BlockSpec((MOE_TRG, D), lambda p, s, q, *_: (q[p], 0)),
                      pl.BlockSpec((SB, D), tok_p),
                      pl.BlockSpec((SB, D), tok_s),
                      pl.BlockSpec((None, None, 1, D), gate_idx),
                      pl.BlockSpec((1, D), lambda p, *_: (0, 0))],
            out_specs=[pl.BlockSpec((SB, D), tok_p), pl.BlockSpec((SB, D), tok_s)],
            scratch_shapes=[pltpu.VMEM((SB, D), F32)]),
        out_shape=[jax.ShapeDtypeStruct((Np, D), F32), jax.ShapeDtypeStruct((Ns, D), F32)],
        compiler_params=_params("arbitrary"),
        name="moe_combine",
    )(*plan, pos_cols, ys, xp, xs, mod, final_g.reshape(1, D))


def _moe(xp, xs, g, mod, router_w, w1, w3, w2, final_g, rows_per_group):
    N = xp.shape[0] + xs.shape[0]
    h, info, info_t, cum = _moe_route(xp, xs, g, mod, router_w, rows_per_group)
    pos_cols, pos_rows, gather_plan, combine_plan, ffn_plan, rmax = _moe_plan(info, info_t, cum, N)
    x_sorted = _moe_gather(h, pos_rows, gather_plan, rmax)
    y_sorted = _moe_ffn(x_sorted, ffn_plan, w1, w3, w2)
    return _moe_combine(y_sorted, pos_cols, combine_plan, xp, xs, mod, final_g, rows_per_group)


def _gla_levels(C):
    lv, c = [], C // 2
    while c >= SUB:
        lv.append(c)
        c //= 2
    return lv


def _gla_tables(C):
    levels = _gla_levels(C)
    nr = 2 + 2 * len(levels)
    mat = np.zeros((2, nr * C, C), np.float32)
    code = np.zeros((2, C, C), np.int32)
    for d in range(2):
        p = np.arange(C) if d == 0 else C - 1 - np.arange(C)
        pi, pj = p[:, None], p[None, :]
        mat[d, 0:C] = pj <= pi
        mat[d, C:2 * C] = pj > pi
        code[d] = np.where((pj <= pi) & (pi // SUB == pj // SUB), 1, 0)
        for lv, c in enumerate(levels):
            blk = pi // c
            later = blk % 2 == 1
            mat[d, (2 + 2 * lv) * C:(3 + 2 * lv) * C] = later & (pj > blk * c - 1) & (pj <= pi)
            mat[d, (3 + 2 * lv) * C:(4 + 2 * lv) * C] = (~later) & (pj > pi) & (pj <= (blk + 1) * c - 1)
            pair = (pi // (2 * c) == pj // (2 * c)) & (pi // c != pj // c) & (pj <= pi)
            code[d] = np.where(pair, 2 + lv, code[d])
    ones = np.zeros((SUB * LANES, C), np.float32)
    for jj in range(SUB):
        ones[jj * LANES:(jj + 1) * LANES, jj::SUB] = 1.0
    return jnp.asarray(mat, dtype=BF16), jnp.asarray(code), jnp.asarray(ones, dtype=BF16)


def _bcast_sublane(x, jj):
    r, w = x.shape
    x3 = x.reshape(r // SUB, SUB, w)
    return jnp.broadcast_to(x3[:, jj:jj + 1, :], x3.shape).reshape(r, w)


def _t128(x):
    r, w = x.shape
    if w > LANES:
        return jnp.concatenate([x[:, i:i + LANES].T for i in range(0, w, LANES)], axis=0)
    return jnp.concatenate([x[i:i + LANES, :].T for i in range(0, r, LANES)], axis=1)


def _gla_kernel(q_ref, k_ref, v_ref, g_ref, lr_ref, wg_ref, ba_ref, mat_ref, code_ref, ones_ref,
                s0f_ref, s0b_ref, ng_ref, o_ref, sf_ref, sb_ref, st_scr, of_scr, *, n, C):
    d = pl.program_id(1)
    c = pl.program_id(2)
    levels = _gla_levels(C)

    @pl.when(jnp.logical_and(c == 0, d == 0))
    def _():
        for h in range(C_H):
            st_scr[h] = _t128(s0f_ref[h])

    @pl.when(jnp.logical_and(c == 0, d == 1))
    def _():
        for h in range(C_H):
            st_scr[h] = _t128(s0b_ref[h])

    xg = jnp.dot(lr_ref[...], wg_ref[...], precision=HIGHEST, preferred_element_type=F32) + ba_ref[...]
    la = (jnp.minimum(xg, 0.0) - jnp.log1p(jnp.exp(-jnp.abs(xg)))) * (1.0 / C_TAU)
    hi = la.astype(BF16)
    lo = (la - hi.astype(F32)).astype(BF16)
    mat = mat_ref[...]
    cum = _dot(mat, hi) + _dot(mat, lo)
    code = code_ref[...]
    ones = ones_ref[...]

    outs = []
    for h in range(C_H):
        ks = slice(h * C_DK, (h + 1) * C_DK)
        qh = q_ref[:, ks] * (C_DK ** -0.5)
        kh = k_ref[:, ks]
        vh = v_ref[:, h * C_DV:(h + 1) * C_DV]
        vb = vh.astype(BF16)
        b = cum[0:C, ks]
        b_rest = cum[C:2 * C, ks]
        ps = []
        for jj in range(SUB):
            dec = jnp.exp(jnp.minimum(b - _bcast_sublane(b, jj), 0.0))
            ps.append((qh * _bcast_sublane(kh, jj) * dec).astype(BF16))
        att = jnp.where(code == 1, _dot(jnp.concatenate(ps, axis=1), ones), 0.0)
        for lv in range(len(levels)):
            eq = cum[(2 + 2 * lv) * C:(3 + 2 * lv) * C, ks]
            ek = cum[(3 + 2 * lv) * C:(4 + 2 * lv) * C, ks]
            a_lv = _dot_nt((qh * jnp.exp(eq)).astype(BF16), (kh * jnp.exp(ek)).astype(BF16))
            att = jnp.where(code == 2 + lv, a_lv, att)
        st = st_scr[h]
        o = _dot(att.astype(BF16), vb) + _dot_nt((qh * jnp.exp(b)).astype(BF16), st.astype(BF16))
        b_end = b[0:1, :] + b_rest[0:1, :]
        ke = (kh * jnp.exp(b_rest)).astype(BF16)
        st_scr[h] = jnp.exp(b_end) * st + _dot(_t128(vh).astype(BF16), ke)
        outs.append(o)
    o_all = jnp.concatenate(outs, axis=-1)

    @pl.when(d == 0)
    def _():
        of_scr[c] = o_all

    @pl.when(d == 1)
    def _():
        tot = o_all + of_scr[n - 1 - c]
        res = []
        for h in range(C_H):
            sl = slice(h * C_DV, (h + 1) * C_DV)
            t = tot[:, sl]
            y = t * lax.rsqrt(jnp.mean(t * t, axis=-1, keepdims=True) + EPS) * ng_ref[:, sl]
            res.append(y * _silu(g_ref[:, sl]))
        o_ref[...] = jnp.concatenate(res, axis=-1).astype(o_ref.dtype)

    @pl.when(jnp.logical_and(c == n - 1, d == 0))
    def _():
        for h in range(C_H):
            sf_ref[h] = _t128(st_scr[h])

    @pl.when(jnp.logical_and(c == n - 1, d == 1))
    def _():
        for h in range(C_H):
            sb_ref[h] = _t128(st_scr[h])


def _gla(z, B, T, w_a2, b_a, s0f, s0b, norm_g):
    C = GLA_CHUNK
    n = T // C
    HK = C_H * C_DK
    HV = C_H * C_DV
    mat, code, ones = _gla_tables(C)
    nr = mat.shape[1] // C
    wg = jnp.zeros((2, LANES, HK), F32)
    for dr in range(2):
        wg = wg.at[dr, dr * C_RANK:(dr + 1) * C_RANK, :].set(w_a2[dr])

    def row(b, d, c):
        return b * n + c + d * (n - 1 - 2 * c)

    st_spec = pl.BlockSpec((None, C_H, C_DK, C_DV), lambda b, d, c: (b, 0, 0, 0))
    st_shape = jax.ShapeDtypeStruct((B, C_H, C_DK, C_DV), F32)
    return pl.pallas_call(
        functools.partial(_gla_kernel, n=n, C=C),
        grid=(B, 2, n),
        in_specs=[pl.BlockSpec((C, HK), lambda b, d, c: (row(b, d, c), 0)),
                  pl.BlockSpec((C, HK), lambda b, d, c: (row(b, d, c), 1)),
                  pl.BlockSpec((C, HV), lambda b, d, c: (row(b, d, c), 1)),
                  pl.BlockSpec((C, HV), lambda b, d, c: (row(b, d, c), 2)),
                  pl.BlockSpec((C, LANES), lambda b, d, c: (row(b, d, c), (2 * HK + 2 * HV) // LANES)),
                  pl.BlockSpec((None, LANES, HK), lambda b, d, c: (d, 0, 0)),
                  pl.BlockSpec((None, 1, HK), lambda b, d, c: (d, 0, 0)),
                  pl.BlockSpec((None, nr * C, C), lambda b, d, c: (d, 0, 0)),
                  pl.BlockSpec((None, C, C), lambda b, d, c: (d, 0, 0)),
                  pl.BlockSpec((SUB * LANES, C), lambda b, d, c: (0, 0)),
                  st_spec, st_spec,
                  pl.BlockSpec((1, HV), lambda b, d, c: (0, 0))],
        out_specs=[pl.BlockSpec((C, HV), lambda b, d, c: (b * n + (n - 1) - d * c, 0)),
                   st_spec, st_spec],
        out_shape=[jax.ShapeDtypeStruct((B * T, HV), BF16), st_shape, st_shape],
        scratch_shapes=[pltpu.VMEM((C_H, C_DV, C_DK), F32), pltpu.VMEM((n, C, HV), F32)],
        compiler_params=_params("arbitrary", "arbitrary", "arbitrary"),
        name="gla",
    )(z, z, z, z, z, wg, b_a.reshape(2, 1, HK), mat, code, ones, s0f, s0b, norm_g.reshape(1, HV))


def _run_stream(x, B, T, mods, ctx, p):
    N, D = x.shape
    rpg = N // mods[0].shape[0]
    TM = min(1024, rpg)

    z = _norm_mm(x, p['norm1_g'][0], mods[0], (0, 1), p['even_w_in'][0], 1280, TM, rpg)
    if ctx is None:
        s0 = jnp.zeros((B, A_H, A_DK, A_DV), F32)
        a_f0, a_b0, cache_k, cache_v = s0, s0, None, None
    else:
        cache_k, cache_v, a_f0, a_b0 = ctx[0], ctx[1], ctx[2], ctx[3]
    o_a, a_sf, a_sb = _retention(z, B, T, p['a_log_gamma'][0], a_f0, a_b0, p['a_norm_g'][0])
    qpad, k_norm, k_rot, v_bf = _bprep(z, T, p['b_q_g'][0], p['b_k_g'][0], rope=ctx is not None)
    o_b = _attention(qpad, k_rot, v_bf, B, T, cache_k, cache_v)
    x = _proj_res(x, mods[0], 2, [o_a, o_b], p['even_w_out'][0], rpg)
    x = _ffn(x, p['norm2_g'][0], mods[0], p['ff_w1'][0], p['ff_w3'][0], p['ff_w2'][0], rpg)

    w_in = p['odd_w_in'][0]
    pad = (-w_in.shape[1]) % 640
    z1 = _norm_mm(x, p['norm1_g'][1], mods[1], (0, 1), jnp.pad(w_in, ((0, 0), (0, pad))), 640, TM, rpg)
    if ctx is None:
        s0 = jnp.zeros((B, C_H, C_DK, C_DV), F32)
        c_f0, c_b0 = s0, s0
    else:
        c_f0, c_b0 = ctx[4], ctx[5]
    o_c, c_sf, c_sb = _gla(z1, B, T, p['c_w_a2'][0], p['c_b_a'][0], c_f0, c_b0, p['c_norm_g'][0])
    x = _proj_res(x, mods[1], 2, [o_c], p['odd_w_out'][0], rpg)
    v_raw = z[:, EVEN_V_OFF:EVEN_V_OFF + B_HKV * B_HD]
    return x, (k_norm, v_raw, a_sf, a_sb, c_sf, c_sb)


EVEN_V_OFF = A_H * (2 * A_DK + 2 * A_DV) + (B_H + B_HKV) * B_HD


def kernel(x_prompt, x_sample, c, cache_b_k, cache_b_v, state_a_fwd, state_a_bwd, state_c_fwd, state_c_bwd,
           c_ctx, w_mod, b_mod, norm1_g, norm2_g, final_g, even_w_in, even_w_out, a_log_gamma, a_norm_g,
           b_q_g, b_k_g, odd_w_in, c_w_a2, c_b_a, c_norm_g, odd_w_out, ff_w1, ff_w3, ff_w2,
           router_w, moe_w1, moe_w3, moe_w2):
    Bp, Tp, D = x_prompt.shape
    Bs, Ts, _ = x_sample.shape
    L = w_mod.shape[0]
    assert L == 2 and even_w_in.shape[0] == 1 and odd_w_in.shape[0] == 1
    p = dict(norm1_g=norm1_g, norm2_g=norm2_g, final_g=final_g, even_w_in=even_w_in, even_w_out=even_w_out,
             a_log_gamma=a_log_gamma, a_norm_g=a_norm_g, b_q_g=b_q_g, b_k_g=b_k_g, odd_w_in=odd_w_in,
             c_w_a2=c_w_a2, c_b_a=c_b_a, c_norm_g=c_norm_g, odd_w_out=odd_w_out, ff_w1=ff_w1, ff_w3=ff_w3,
             ff_w2=ff_w2, router_w=router_w, moe_w1=moe_w1, moe_w3=moe_w3, moe_w2=moe_w2)

    rows = 8
    conds = jnp.concatenate([c_ctx[None, :], c, jnp.zeros((rows - 1 - Bs, D), F32)], axis=0)
    mod = _modulation(conds, w_mod, b_mod).reshape(L, rows, 6, 1, D)
    mods_p = [mod[l, 0:1] for l in range(L)]
    mods_s = [mod[l, 1:1 + Bs] for l in range(L)]

    x_p, kept = _run_stream(x_prompt.reshape(Bp * Tp, D), Bp, Tp, mods_p, None, p)
    nk = B_HKV * B_HD
    ctx = (cache_b_k[:, 0].reshape(Bs, -1, nk), cache_b_v[:, 0].reshape(Bs, -1, nk),
           state_a_fwd[:, 0], state_a_bwd[:, 0], state_c_fwd[:, 0], state_c_bwd[:, 0])
    x_s, _ = _run_stream(x_sample.reshape(Bs * Ts, D), Bs, Ts, mods_s, ctx, p)
    y_p, y_s = _moe(x_p, x_s, norm2_g[1], mod[1, 0:1 + Bs], router_w[0], moe_w1[0], moe_w3[0], moe_w2[0],
                    final_g, Ts)

    k_norm, v_raw, a_sf, a_sb, c_sf, c_sb = kept
    return (y_p.reshape(Bp, Tp, D), y_s.reshape(Bs, Ts, D),
            k_norm.reshape(Bp, 1, Tp, B_HKV, B_HD), v_raw.reshape(Bp, 1, Tp, B_HKV, B_HD),
            a_sf[:, None], a_sb[:, None], c_sf[:, None], c_sb[:, None])
```

```python
import functools

import numpy as np
import jax
import jax.numpy as jnp
from jax import lax
from jax.experimental import pallas as pl
from jax.experimental.pallas import tpu as pltpu

F32 = jnp.float32
BF16 = jnp.bfloat16
EPS = 1e-6
HIGHEST = lax.Precision.HIGHEST
LOG2E = 1.4426950408889634

VMEM_LIMIT_BYTES = 56 * 1024 * 1024

A_H, A_DK, A_DV = 4, 128, 256
B_H, B_HKV, B_HD = 8, 2, 64
C_H, C_DK, C_DV, C_RANK = 4, 128, 256, 16
C_TAU = 16.0
GRID_W = 64
ROPE_THETA = 10000.0
N_EXPERTS = 8
LANES = 128
SUB = 8
RET_CHUNK = 128
GLA_CHUNK = 128
Q_TILE = 128


def _params(*sem):
    return pltpu.CompilerParams(dimension_semantics=sem, vmem_limit_bytes=VMEM_LIMIT_BYTES)


def _dot(a, b):
    return jnp.dot(a, b, preferred_element_type=F32)


def _dot_nt(a, b):
    return lax.dot_general(a, b, (((1,), (1,)), ((), ())), preferred_element_type=F32)


def _silu(x):
    return x * jax.nn.sigmoid(x)


def _norm_mod(x, g, sh, sc):
    r = lax.rsqrt(jnp.mean(x * x, axis=-1, keepdims=True) + EPS)
    return (x * r * g) * (1.0 + sc) + sh


def _mod_kernel(c_ref, w_ref, b_ref, o_ref):
    c = c_ref[...]
    o_ref[...] = jnp.dot(_silu(c), w_ref[...], precision=HIGHEST, preferred_element_type=F32) + b_ref[...]


def _modulation(conds, w_mod, b_mod):
    L, D, D6 = w_mod.shape
    R = conds.shape[0]
    TN = 1024
    return pl.pallas_call(
        _mod_kernel,
        grid=(L, D6 // TN),
        in_specs=[pl.BlockSpec((R, D), lambda l, j: (0, 0)),
                  pl.BlockSpec((None, D, TN), lambda l, j: (l, 0, j)),
                  pl.BlockSpec((None, 1, TN), lambda l, j: (l, 0, j))],
        out_specs=pl.BlockSpec((None, R, TN), lambda l, j: (l, 0, j)),
        out_shape=jax.ShapeDtypeStruct((L, R, D6), F32),
        compiler_params=_params("arbitrary", "arbitrary"),
        name="modulation",
    )(conds, w_mod, b_mod.reshape(L, 1, D6))


def _mod_spec(part, D, TM, rows_per_group, axis):
    def idx(*g):
        return ((g[axis] * TM) // rows_per_group, part, 0, 0)
    return pl.BlockSpec((None, None, 1, D), idx)


def _norm_mm_kernel(x_ref, g_ref, sh_ref, sc_ref, w_ref, o_ref, h_scr):
    @pl.when(pl.program_id(1) == 0)
    def _():
        h_scr[...] = _norm_mod(x_ref[...], g_ref[...], sh_ref[...], sc_ref[...]).astype(BF16)

    o_ref[...] = _dot(h_scr[...], w_ref[...].astype(BF16)).astype(o_ref.dtype)


def _norm_mm(x, g, mod, parts, w, TN, TM, rows_per_group):
    N, D = x.shape
    NO = w.shape[1]
    return pl.pallas_call(
        _norm_mm_kernel,
        grid=(N // TM, NO // TN),
        in_specs=[pl.BlockSpec((TM, D), lambda i, j: (i, 0)),
                  pl.BlockSpec((1, D), lambda i, j: (0, 0)),
                  _mod_spec(parts[0], D, TM, rows_per_group, 0),
                  _mod_spec(parts[1], D, TM, rows_per_group, 0),
                  pl.BlockSpec((D, TN), lambda i, j: (0, j))],
        out_specs=pl.BlockSpec((TM, TN), lambda i, j: (i, j)),
        out_shape=jax.ShapeDtypeStruct((N, NO), F32),
        scratch_shapes=[pltpu.VMEM((TM, D), BF16)],
        compiler_params=_params("arbitrary", "arbitrary"),
        name="norm_mm",
    )(x, g.reshape(1, D), mod, mod, w)


def _ret_kernel(lg_ref, q_ref, k_ref, v_ref, ag_ref, s0f_ref, s0b_ref, ng_ref,
                o_ref, sf_ref, sb_ref, s_scr, of_scr, *, n, C):
    d = pl.program_id(1)
    c = pl.program_id(2)

    @pl.when(jnp.logical_and(c == 0, d == 0))
    def _():
        s_scr[...] = s0f_ref[...]

    @pl.when(jnp.logical_and(c == 0, d == 1))
    def _():
        s_scr[...] = s0b_ref[...]

    df = d.astype(F32)
    sgn = 1.0 - 2.0 * df
    ii = lax.broadcasted_iota(jnp.int32, (C, C), 0).astype(F32)
    jj = lax.broadcasted_iota(jnp.int32, (C, C), 1).astype(F32)
    dd = (ii - jj) * sgn
    feeds = dd >= 0.0
    ddc = jnp.maximum(dd, 0.0)
    ri = lax.broadcasted_iota(jnp.int32, (C, 1), 0).astype(F32)
    pos_q = (ri + 1.0) + df * (C - 2.0 * ri - 1.0)
    pos_k = (C - 1.0 - ri) + df * (2.0 * ri - C + 1.0)
    chunk_len = jnp.full((1, A_DV), float(C), F32)

    outs = []
    for h in range(A_H):
        lg = lg_ref[d, h]
        dmask = jnp.where(feeds, jnp.exp(lg * ddc), 0.0)
        qh = q_ref[:, h * A_DK:(h + 1) * A_DK] * (A_DK ** -0.5)
        kh = k_ref[:, h * A_DK:(h + 1) * A_DK]
        vh = v_ref[:, h * A_DV:(h + 1) * A_DV].astype(BF16)
        s = s_scr[h]
        att = _dot_nt(qh.astype(BF16), kh.astype(BF16)) * dmask
        o = _dot(att.astype(BF16), vh) + _dot((qh * jnp.exp(lg * pos_q)).astype(BF16), s.astype(BF16))
        kd = kh * jnp.exp(lg * pos_k)
        s_scr[h] = jnp.exp(lg * chunk_len) * s + _dot(kd.T.astype(BF16), vh)
        outs.append(o)
    o_all = jnp.concatenate(outs, axis=-1)

    @pl.when(d == 0)
    def _():
        of_scr[c] = o_all

    @pl.when(d == 1)
    def _():
        tot = o_all + of_scr[n - 1 - c]
        res = []
        for h in range(A_H):
            sl = slice(h * A_DV, (h + 1) * A_DV)
            t = tot[:, sl]
            dev = t - jnp.mean(t, axis=-1, keepdims=True)
            y = dev * lax.rsqrt(jnp.mean(dev * dev, axis=-1, keepdims=True) + EPS) * ng_ref[:, sl]
            res.append(y * _silu(ag_ref[:, sl]))
        o_ref[...] = jnp.concatenate(res, axis=-1).astype(o_ref.dtype)

    @pl.when(jnp.logical_and(c == n - 1, d == 0))
    def _():
        sf_ref[...] = s_scr[...]

    @pl.when(jnp.logical_and(c == n - 1, d == 1))
    def _():
        sb_ref[...] = s_scr[...]


def _retention(z, B, T, log_gamma, s0f, s0b, norm_g):
    C = RET_CHUNK
    n = T // C
    HV = A_H * A_DV

    def row(b, d, c):
        return b * n + c + d * (n - 1 - 2 * c)

    st_spec = pl.BlockSpec((None, A_H, A_DK, A_DV), lambda b, d, c: (b, 0, 0, 0))
    st_shape = jax.ShapeDtypeStruct((B, A_H, A_DK, A_DV), F32)
    return pl.pallas_call(
        functools.partial(_ret_kernel, n=n, C=C),
        grid=(B, 2, n),
        in_specs=[pl.BlockSpec(memory_space=pltpu.SMEM),
                  pl.BlockSpec((C, 512), lambda b, d, c: (row(b, d, c), 0)),
                  pl.BlockSpec((C, 512), lambda b, d, c: (row(b, d, c), 1)),
                  pl.BlockSpec((C, HV), lambda b, d, c: (row(b, d, c), 1)),
                  pl.BlockSpec((C, HV), lambda b, d, c: (row(b, d, c), 2)),
                  st_spec, st_spec,
                  pl.BlockSpec((1, HV), lambda b, d, c: (0, 0))],
        out_specs=[pl.BlockSpec((C, HV), lambda b, d, c: (b * n + (n - 1) - d * c, 0)),
                   st_spec, st_spec],
        out_shape=[jax.ShapeDtypeStruct((B * T, HV), BF16), st_shape, st_shape],
        scratch_shapes=[pltpu.VMEM((A_H, A_DK, A_DV), F32), pltpu.VMEM((n, C, HV), F32)],
        compiler_params=_params("arbitrary", "arbitrary", "arbitrary"),
        name="retention",
    )(log_gamma, z, z, z, z, s0f, s0b, norm_g.reshape(1, HV))


def _group_sum_matrix(width, group):
    i = np.arange(width)
    return jnp.asarray((i[:, None] // group == i[None, :] // group).astype(np.float32), dtype=BF16)


def _q_pad_matrix():
    m = np.zeros((B_H * B_HD, B_H * LANES), np.float32)
    g = B_H // B_HKV
    for h in range(B_H):
        for t in range(B_HD):
            m[h * B_HD + t, h * LANES + (h // g) * B_HD + t] = 1.0
    return jnp.asarray(m, dtype=BF16)


def _rope_tables(T):
    rows = T // GRID_W
    row = np.repeat(np.arange(rows, dtype=np.float64), GRID_W)
    col = np.tile(np.arange(GRID_W, dtype=np.float64), rows)
    nq = B_HD // 4
    inv = ROPE_THETA ** (-np.arange(nq, dtype=np.float64) / nq)
    ang = np.concatenate([row[:, None] * inv, col[:, None] * inv], axis=-1)
    cos = np.repeat(np.cos(ang), 2, axis=-1)
    sin = np.repeat(np.sin(ang), 2, axis=-1)
    sign = np.tile(np.array([-1.0, 1.0]), B_HD // 2)
    reps = LANES // B_HD
    return (jnp.asarray(np.tile(cos, (1, reps)), dtype=F32),
            jnp.asarray(np.tile(sin * sign, (1, reps)), dtype=F32))


def _group_rmsnorm(x, gsum, g):
    x2 = x * x
    hi = x2.astype(BF16)
    lo = (x2 - hi.astype(F32)).astype(BF16)
    ss = _dot(hi, gsum) + _dot(lo, gsum)
    return x * lax.rsqrt(ss * (1.0 / B_HD) + EPS) * g


def _rotate_pairs(x, cos, sin_signed):
    n = x.shape[1]
    lane = lax.broadcasted_iota(jnp.int32, x.shape, 1)
    partner = jnp.where(lane % 2 == 0, pltpu.roll(x, n - 1, 1), pltpu.roll(x, 1, 1))
    reps = n // LANES
    if reps > 1:
        cos = jnp.concatenate([cos] * reps, axis=1)
        sin_signed = jnp.concatenate([sin_signed] * reps, axis=1)
    return x * cos + partner * sin_signed


def _bprep_kernel(z_ref, qg_ref, kg_ref, cos_ref, sin_ref, gq_ref, gk_ref, pad_ref,
                  qpad_ref, kn_ref, kr_ref, vb_ref, *, rope):
    nq = B_H * B_HD
    nk = B_HKV * B_HD
    qn = _group_rmsnorm(z_ref[:, 0:nq], gq_ref[...], qg_ref[...])
    kn = _group_rmsnorm(z_ref[:, nq:nq + nk], gk_ref[...], kg_ref[...])
    kn_ref[...] = kn
    if rope:
        qn = _rotate_pairs(qn, cos_ref[...], sin_ref[...])
        kn = _rotate_pairs(kn, cos_ref[...], sin_ref[...])
    kr_ref[...] = kn.astype(BF16)
    vb_ref[...] = z_ref[:, nq + nk:nq + 2 * nk].astype(BF16)
    qs = (qn * (B_HD ** -0.5 * LOG2E)).astype(BF16)
    qpad_ref[...] = _dot(qs, pad_ref[...]).astype(BF16)


def _bprep(z, T, q_g, k_g, rope):
    N = z.shape[0]
    TM = min(512, T)
    nq = B_H * B_HD
    nk = B_HKV * B_HD
    width = nq + 2 * nk
    col = (A_H * (2 * A_DK + 2 * A_DV)) // width
    cos, sin = _rope_tables(T if rope else TM)
    nt = T // TM if rope else 1
    const = lambda i: (0, 0)
    return pl.pallas_call(
        functools.partial(_bprep_kernel, rope=rope),
        grid=(N // TM,),
        in_specs=[pl.BlockSpec((TM, width), lambda i: (i, col)),
                  pl.BlockSpec((1, nq), const),
                  pl.BlockSpec((1, nk), const),
                  pl.BlockSpec((TM, LANES), lambda i: (i % nt, 0)),
                  pl.BlockSpec((TM, LANES), lambda i: (i % nt, 0)),
                  pl.BlockSpec((nq, nq), const),
                  pl.BlockSpec((nk, nk), const),
                  pl.BlockSpec((nq, B_H * LANES), const)],
        out_specs=[pl.BlockSpec((TM, B_H * LANES), lambda i: (i, 0)),
                   pl.BlockSpec((TM, nk), lambda i: (i, 0)),
                   pl.BlockSpec((TM, nk), lambda i: (i, 0)),
                   pl.BlockSpec((TM, nk), lambda i: (i, 0))],
        out_shape=[jax.ShapeDtypeStruct((N, B_H * LANES), BF16),
                   jax.ShapeDtypeStruct((N, nk), F32),
                   jax.ShapeDtypeStruct((N, nk), BF16),
                   jax.ShapeDtypeStruct((N, nk), BF16)],
        compiler_params=_params("arbitrary"),
        name="attn_prep",
    )(z, jnp.tile(q_g, B_H).reshape(1, nq), jnp.tile(k_g, B_HKV).reshape(1, nk), cos, sin,
      _group_sum_matrix(nq, B_HD), _group_sum_matrix(nk, B_HD), _q_pad_matrix())


def _lane_fold(x, op):
    acc = x[:, 0:LANES]
    for j in range(1, x.shape[1] // LANES):
        acc = op(acc, x[:, j * LANES:(j + 1) * LANES])
    return acc


def _attn_kernel(*refs, has_cache, kc, nq):
    if has_cache:
        q_ref, k_ref, v_ref, ck_ref, cv_ref, o_ref, s_scr, m_scr, mprev_scr, l_scr, acc_scr = refs
        ncache = ck_ref.shape[0] // kc
    else:
        q_ref, k_ref, v_ref, o_ref, s_scr, m_scr, mprev_scr, l_scr, acc_scr = refs
        ncache = 0
    i = pl.program_id(1)
    tq = q_ref.shape[0]
    nlat = k_ref.shape[0] // kc
    ngrp = kc // LANES

    def score(c, kblk):
        q = jnp.concatenate([q_ref[:, h * LANES:(h + 1) * LANES] for h in range(B_H)], axis=0)
        s = _dot_nt(q, kblk)
        s_scr[c] = s
        m_scr[...] = jnp.maximum(m_scr[...], _lane_fold(s, jnp.maximum))

    def weight(c, vblk):
        s = s_scr[c]
        mp = mprev_scr[...]
        ps = [jnp.exp2(s[:, j * LANES:(j + 1) * LANES] - mp) for j in range(ngrp)]
        tot = ps[0]
        for pj in ps[1:]:
            tot = tot + pj
        l_scr[...] += tot
        acc_scr[...] += _dot(jnp.concatenate(ps, axis=1).astype(BF16), vblk)

    def run(do_weight, do_score):
        def unit(c, kblk, vblk):
            if do_weight:
                weight(c, vblk())
            if do_score:
                score(c, kblk())

        for c in range(ncache):
            unit(c, lambda: ck_ref[c * kc:(c + 1) * kc, :].astype(BF16),
                 lambda: cv_ref[c * kc:(c + 1) * kc, :].astype(BF16))

        def body(c, carry):
            rows = pl.ds(pl.multiple_of(c * kc, kc), kc)
            unit(ncache + c, lambda: k_ref[rows, :], lambda: v_ref[rows, :])
            return carry
        lax.fori_loop(0, nlat, body, 0)

    @pl.when(i < nq)
    def _():
        m_scr[...] = jnp.full(m_scr.shape, -jnp.inf, F32)

    @pl.when(i > 0)
    def _():
        l_scr[...] = jnp.zeros_like(l_scr)
        acc_scr[...] = jnp.zeros_like(acc_scr)

    @pl.when(i == 0)
    def _():
        run(False, True)

    @pl.when(jnp.logical_and(i > 0, i < nq))
    def _():
        run(True, True)

    @pl.when(i == nq)
    def _():
        run(True, False)

    @pl.when(i > 0)
    def _():
        r_all = acc_scr[...] / jnp.sum(l_scr[...], axis=-1, keepdims=True)
        g = B_H // B_HKV
        lane = lax.broadcasted_iota(jnp.int32, (tq, LANES), 1)
        outs = []
        for j in range(B_H // 2):
            pair = []
            for half in range(2):
                h = 2 * j + half
                r = r_all[h * tq:(h + 1) * tq, :]
                if h // g != half:
                    r = pltpu.roll(r, B_HD, 1)
                pair.append(r)
            outs.append(jnp.where(lane < B_HD, pair[0], pair[1]))
        o_ref[...] = jnp.concatenate(outs, axis=-1).astype(o_ref.dtype)

    @pl.when(i < nq)
    def _():
        mprev_scr[...] = jnp.broadcast_to(jnp.max(m_scr[...], axis=-1, keepdims=True), mprev_scr.shape)


def _attention(qpad, kr, vb, B, T, cache_k, cache_v):
    has_cache = cache_k is not None
    TQ = Q_TILE
    nq = T // TQ
    nk = B_HKV * B_HD
    in_specs = [pl.BlockSpec((TQ, B_H * LANES), lambda b, i: (b * nq + jnp.minimum(i, nq - 1), 0)),
                pl.BlockSpec((T, nk), lambda b, i: (b, 0)),
                pl.BlockSpec((T, nk), lambda b, i: (b, 0))]
    args = [qpad, kr, vb]
    kc = min(512, T)
    nchunks = T // kc
    if has_cache:
        P = cache_k.shape[1]
        assert P % kc == 0
        nchunks += P // kc
        in_specs += [pl.BlockSpec((None, P, nk), lambda b, i: (b, 0, 0))] * 2
        args += [cache_k, cache_v]
    R = B_H * TQ
    return pl.pallas_call(
        functools.partial(_attn_kernel, has_cache=has_cache, kc=kc, nq=nq),
        grid=(B, nq + 1),
        in_specs=in_specs,
        out_specs=pl.BlockSpec((TQ, B_H * B_HD), lambda b, i: (b * nq + jnp.maximum(i - 1, 0), 0)),
        out_shape=jax.ShapeDtypeStruct((B * T, B_H * B_HD), BF16),
        scratch_shapes=[pltpu.VMEM((nchunks, R, kc), F32)] + [pltpu.VMEM((R, LANES), F32)] * 4,
        compiler_params=_params("arbitrary", "arbitrary"),
        name="attention",
    )(*args)


def _proj_res_kernel(*refs, n_in):
    x_ref, gate_ref = refs[0], refs[1]
    o_refs = refs[2:2 + n_in]
    w_refs = refs[2 + n_in:2 + 2 * n_in]
    out_ref = refs[2 + 2 * n_in]
    wbf_refs = refs[3 + 2 * n_in:]

    @pl.when(pl.program_id(0) == 0)
    def _():
        for w_ref, wbf_ref in zip(w_refs, wbf_refs):
            wbf_ref[...] = w_ref[...].astype(BF16)

    acc = _dot(o_refs[0][...], wbf_refs[0][...])
    for o_ref, wbf_ref in zip(o_refs[1:], wbf_refs[1:]):
        acc = acc + _dot(o_ref[...], wbf_ref[...])
    out_ref[...] = x_ref[...] + gate_ref[...] * acc


def _proj_res(x, mod, part, acts, w, rows_per_group):
    N, D = x.shape
    TM = min(512, rows_per_group)
    n_in = len(acts)
    widths = [a.shape[1] for a in acts]
    offs = np.cumsum([0] + widths[:-1]).tolist()
    in_specs = [pl.BlockSpec((TM, D), lambda i: (i, 0)),
                _mod_spec(part, D, TM, rows_per_group, 0)]
    in_specs += [pl.BlockSpec((TM, wd), lambda i: (i, 0)) for wd in widths]
    in_specs += [pl.BlockSpec((wd, D), functools.partial(lambda i, blk: (blk, 0), blk=off // wd))
                 for wd, off in zip(widths, offs)]
    return pl.pallas_call(
        functools.partial(_proj_res_kernel, n_in=n_in),
        grid=(N // TM,),
        in_specs=in_specs,
        out_specs=pl.BlockSpec((TM, D), lambda i: (i, 0)),
        out_shape=jax.ShapeDtypeStruct((N, D), F32),
        scratch_shapes=[pltpu.VMEM((wd, D), BF16) for wd in widths],
        compiler_params=_params("arbitrary"),
        name="proj_residual",
    )(x, mod, *acts, *([w] * n_in))


def _ffn_kernel(x_ref, g_ref, sh_ref, sc_ref, gate_ref, w1_ref, w3_ref, w2_ref, out_ref, h_scr, acc_scr, *, nf):
    f = pl.program_id(1)

    @pl.when(f == 0)
    def _():
        h_scr[...] = _norm_mod(x_ref[...], g_ref[...], sh_ref[...], sc_ref[...]).astype(BF16)
        acc_scr[...] = jnp.zeros_like(acc_scr)

    h = h_scr[...]
    a = _dot(h, w1_ref[...].astype(BF16))
    b = _dot(h, w3_ref[...].astype(BF16))
    acc_scr[...] += _dot((_silu(a) * b).astype(BF16), w2_ref[...].astype(BF16))

    @pl.when(f == nf - 1)
    def _():
        out_ref[...] = x_ref[...] + gate_ref[...] * acc_scr[...]


def _ffn(x, g, mod, w1, w3, w2, rows_per_group):
    N, D = x.shape
    FF = w1.shape[1]
    TM, TF = min(1024, rows_per_group), 256
    nf = FF // TF
    return pl.pallas_call(
        functools.partial(_ffn_kernel, nf=nf),
        grid=(N // TM, nf),
        in_specs=[pl.BlockSpec((TM, D), lambda i, f: (i, 0)),
                  pl.BlockSpec((1, D), lambda i, f: (0, 0)),
                  _mod_spec(3, D, TM, rows_per_group, 0),
                  _mod_spec(4, D, TM, rows_per_group, 0),
                  _mod_spec(5, D, TM, rows_per_group, 0),
                  pl.BlockSpec((D, TF), lambda i, f: (0, f)),
                  pl.BlockSpec((D, TF), lambda i, f: (0, f)),
                  pl.BlockSpec((TF, D), lambda i, f: (f, 0))],
        out_specs=pl.BlockSpec((TM, D), lambda i, f: (i, 0)),
        out_shape=jax.ShapeDtypeStruct((N, D), F32),
        scratch_shapes=[pltpu.VMEM((TM, D), BF16), pltpu.VMEM((TM, D), F32)],
        compiler_params=_params("arbitrary", "arbitrary"),
        name="ffn",
    )(x, g.reshape(1, D), mod, mod, mod, w1, w3, w2)


MOE_SB = 1024
MOE_TRG = 256
MOE_TR = 1024


def _two_stream_specs(shape, ntp, ax=0):
    def idx_p(*g):
        return (jnp.minimum(g[ax], ntp - 1), 0)

    def idx_s(*g):
        return (jnp.maximum(g[ax] - ntp, 0), 0)
    return pl.BlockSpec(shape, idx_p), pl.BlockSpec(shape, idx_s)


def _pool_mod_spec(part, D, TM, ntp, rows_per_group):
    def idx(i, *_):
        return (jnp.where(i < ntp, 0, 1 + ((i - ntp) * TM) // rows_per_group), part, 0, 0)
    return pl.BlockSpec((None, None, 1, D), idx)


def _route_kernel(xp_ref, xs_ref, g_ref, sh_ref, sc_ref, rw_ref, tri_ref, h_ref, info_ref, infot_ref, cum_ref,
                  carry_scr, *, ntp):
    i = pl.program_id(0)

    @pl.when(i == 0)
    def _():
        carry_scr[...] = jnp.zeros_like(carry_scr)

    x = jnp.where(i < ntp, xp_ref[...], xs_ref[...])
    h = _norm_mod(x, g_ref[...], sh_ref[...], sc_ref[...])
    h_ref[...] = h.astype(BF16)
    lane = lax.broadcasted_iota(jnp.int32, (x.shape[0], LANES), 1).astype(F32)
    logits = jnp.dot(h, rw_ref[...], precision=HIGHEST, preferred_element_type=F32)
    logits = jnp.where(lane < N_EXPERTS, logits, -jnp.inf)
    m1 = jnp.max(logits, axis=-1, keepdims=True)
    i1 = jnp.min(jnp.where(logits == m1, lane, float(LANES)), axis=-1, keepdims=True)
    rest = jnp.where(lane == i1, -jnp.inf, logits)
    m2 = jnp.max(rest, axis=-1, keepdims=True)
    i2 = jnp.min(jnp.where(rest == m2, lane, float(LANES)), axis=-1, keepdims=True)
    e2 = jnp.exp(m2 - m1)
    w1 = 1.0 / (1.0 + e2)
    w2 = e2 / (1.0 + e2)
    ind = jnp.where(jnp.logical_or(lane == i1, lane == i2), 1.0, 0.0)
    before = _dot(tri_ref[...], ind.astype(BF16)) + carry_scr[...]
    r1 = jnp.sum(jnp.where(lane == i1, before, 0.0), axis=-1, keepdims=True)
    r2 = jnp.sum(jnp.where(lane == i2, before, 0.0), axis=-1, keepdims=True)
    total = carry_scr[...] + jnp.sum(ind, axis=0, keepdims=True)
    carry_scr[...] = total
    cum_ref[...] = total
    info = jnp.where(lane == 0.0, i1, jnp.where(lane == 1.0, i2, jnp.where(lane == 2.0, w1, jnp.where(
        lane == 3.0, w2, jnp.where(lane == 4.0, r1, jnp.where(lane == 5.0, r2, 0.0))))))
    info_ref[...] = info[:, 0:SUB]
    info_t = jnp.concatenate([info[r:r + LANES, :].T for r in range(0, info.shape[0], LANES)], axis=1)
    infot_ref[...] = info_t[0:SUB, :]


def _moe_route(xp, xs, g, mod, router_w, rows_per_group):
    Np, D = xp.shape
    N = Np + xs.shape[0]
    TM = MOE_SB
    ntp = Np // TM
    nt = N // TM
    rw = jnp.pad(router_w, ((0, 0), (0, LANES - router_w.shape[1])))
    tri = jnp.asarray(np.tril(np.ones((TM, TM), np.float32), -1), dtype=BF16)
    xp_spec, xs_spec = _two_stream_specs((TM, D), ntp)
    return pl.pallas_call(
        functools.partial(_route_kernel, ntp=ntp),
        grid=(nt,),
        in_specs=[xp_spec, xs_spec,
                  pl.BlockSpec((1, D), lambda i: (0, 0)),
                  _pool_mod_spec(3, D, TM, ntp, rows_per_group),
                  _pool_mod_spec(4, D, TM, ntp, rows_per_group),
                  pl.BlockSpec((D, LANES), lambda i: (0, 0)),
                  pl.BlockSpec((TM, TM), lambda i: (0, 0))],
        out_specs=[pl.BlockSpec((TM, D), lambda i: (i, 0)),
                   pl.BlockSpec((TM, SUB), lambda i: (i, 0)),
                   pl.BlockSpec((SUB, TM), lambda i: (0, i)),
                   pl.BlockSpec((None, 1, LANES), lambda i: (i, 0, 0))],
        out_shape=[jax.ShapeDtypeStruct((N, D), BF16),
                   jax.ShapeDtypeStruct((N, SUB), F32),
                   jax.ShapeDtypeStruct((SUB, N), F32),
                   jax.ShapeDtypeStruct((nt, 1, LANES), F32)],
        scratch_shapes=[pltpu.VMEM((1, LANES), F32)],
        compiler_params=_params("arbitrary"),
        name="moe_route",
    )(xp, xs, g.reshape(1, D), mod, mod, rw, tri)


def _moe_plan(info, info_t, cum, N):
    E, SB, TRG, TR = N_EXPERTS, MOE_SB, MOE_TRG, MOE_TR
    NB = N // SB
    rmax = 2 * N + E * TR
    RG, RT = rmax // TRG, rmax // TR
    PMAX = RG + E * NB
    i32 = jnp.int32
    cum_e = cum[:, 0, :E].astype(i32).T
    cnt = cum_e[:, -1]
    tiles = (cnt + TR - 1) // TR
    start = TR * (jnp.cumsum(tiles) - tiles)

    startf = start.astype(F32)

    def region_start(e):
        out = jnp.zeros_like(e)
        for k in range(E):
            out = jnp.where(e == float(k), startf[k], out)
        return out

    pos_cols = jnp.concatenate([region_start(info[:, 0:2]) + info[:, 4:6], info[:, 2:4],
                                jnp.zeros((N, 4), F32)], axis=1)
    pos_rows = jnp.concatenate([region_start(info_t[0:2]) + info_t[4:6], jnp.zeros((6, N), F32)],
                               axis=0)

    def region(row0):
        e = jnp.clip(jnp.sum(row0[:, None] >= start[None, :], axis=1) - 1, 0, E - 1)
        return e, row0 - start[e]

    eq, lo = region(jnp.arange(RG, dtype=i32) * TRG)
    hi = jnp.minimum(lo + TRG, cnt[eq])
    first = jnp.sum(cum_e[eq] <= lo[:, None], axis=1)
    last = jnp.sum(cum_e[eq] < hi[:, None], axis=1)
    nblk = jnp.where(hi > lo, last - first + 1, 0)
    pend = jnp.cumsum(nblk)
    npairs = pend[-1]
    p = jnp.arange(PMAX, dtype=i32)
    valid = p < npairs
    pc = jnp.minimum(p, npairs - 1)
    q_of = jnp.minimum(jnp.sum(pend[None, :] <= pc[:, None], axis=1), RG - 1).astype(i32)
    pstart = pend - nblk
    s_of = (first[q_of] + pc - pstart[q_of]).astype(i32)
    g_first = jnp.logical_and(valid, pc == pstart[q_of]).astype(i32)
    gather_plan = (q_of, s_of, valid.astype(i32), g_first)

    order = jnp.argsort(jnp.where(valid, s_of * RG + q_of, jnp.iinfo(jnp.int32).max))
    s2, q2 = s_of[order], q_of[order]
    s2 = jnp.where(valid, s2, s2[npairs - 1])
    q2 = jnp.where(valid, q2, q2[npairs - 1])
    prev = jnp.concatenate([jnp.full((1,), -1, i32), s2[:-1]])
    nxt = jnp.concatenate([s2[1:], jnp.full((1,), -1, i32)])
    c_first = jnp.logical_and(valid, s2 != prev).astype(i32)
    c_last = jnp.logical_and(valid, jnp.logical_or(s2 != nxt, p == npairs - 1)).astype(i32)
    combine_plan = (s2.astype(i32), q2.astype(i32), valid.astype(i32), c_first, c_last)

    te, tlo = region(jnp.arange(RT, dtype=i32) * TR)
    tvalid = jnp.clip(cnt[te] - tlo, 0, TR)
    last_t = jnp.sum(tiles) - 1
    t_idx = jnp.where(tvalid > 0, jnp.arange(RT, dtype=i32), last_t).astype(i32)
    ffn_plan = (t_idx, te[t_idx].astype(i32), tvalid.astype(i32))
    return pos_cols, pos_rows, gather_plan, combine_plan, ffn_plan, rmax


def _moe_gather_kernel(q_ref, s_ref, valid_ref, first_ref, pos_ref, h_ref, out_ref):
    p = pl.program_id(0)

    @pl.when(valid_ref[p] == 1)
    def _():
        rows, toks = out_ref.shape[0], h_ref.shape[0]
        row = (lax.broadcasted_iota(jnp.int32, (rows, 1), 0) + q_ref[p] * rows).astype(F32)
        hit = jnp.logical_or(pos_ref[0:1, :] == row, pos_ref[1:2, :] == row)
        sel = jnp.where(hit, 1.0, 0.0).astype(BF16)
        @pl.when(first_ref[p] == 1)
        def _():
            out_ref[...] = jnp.zeros_like(out_ref)

        out_ref[...] = out_ref[...] + _dot(sel, h_ref[...]).astype(BF16)


def _moe_gather(h, pos_rows, plan, rmax):
    N, D = h.shape
    pmax = plan[0].shape[0]
    return pl.pallas_call(
        _moe_gather_kernel,
        grid_spec=pltpu.PrefetchScalarGridSpec(
            num_scalar_prefetch=4, grid=(pmax,),
            in_specs=[pl.BlockSpec((SUB, MOE_SB), lambda p, q, s, v, f: (0, s[p])),
                      pl.BlockSpec((MOE_SB, D), lambda p, q, s, v, f: (s[p], 0))],
            out_specs=pl.BlockSpec((MOE_TRG, D), lambda p, q, s, v, f: (q[p], 0))),
        out_shape=jax.ShapeDtypeStruct((rmax, D), BF16),
        compiler_params=_params("arbitrary"),
        name="moe_gather",
    )(*plan, pos_rows, h)


def _moe_ffn_kernel(t_ref, e_ref, nv_ref, x_ref, w1_ref, w3_ref, w2_ref, out_ref, acc_scr, *, nf):
    t = pl.program_id(0)
    f = pl.program_id(1)
    nv = nv_ref[t]

    def block(rows):
        @pl.when(f == 0)
        def _():
            acc_scr[rows, :] = jnp.zeros((rows.stop - rows.start, acc_scr.shape[1]), F32)

        x = x_ref[rows, :]
        a = _dot(x, w1_ref[...].astype(BF16))
        b = _dot(x, w3_ref[...].astype(BF16))
        acc_scr[rows, :] += _dot((_silu(a) * b).astype(BF16), w2_ref[...].astype(BF16))

        @pl.when(f == nf - 1)
        def _():
            out_ref[rows, :] = acc_scr[rows, :].astype(out_ref.dtype)

    nsub = MOE_TR // MOE_TRG
    full = nv > (nsub - 1) * MOE_TRG

    @pl.when(full)
    def _():
        block(slice(0, MOE_TR))

    for sub in range(nsub - 1):
        @pl.when(jnp.logical_and(jnp.logical_not(full), sub * MOE_TRG < nv))
        def _():
            block(slice(sub * MOE_TRG, (sub + 1) * MOE_TRG))


def _moe_ffn(xs, plan, w1, w3, w2):
    rmax, D = xs.shape
    FF = w1.shape[2]
    TF = 256
    nf = FF // TF
    RT = rmax // MOE_TR

    def fidx(t, f, nv):
        return jnp.where(nv[t] > 0, f, nf - 1)

    return pl.pallas_call(
        functools.partial(_moe_ffn_kernel, nf=nf),
        grid_spec=pltpu.PrefetchScalarGridSpec(
            num_scalar_prefetch=3, grid=(RT, nf),
            in_specs=[pl.BlockSpec((MOE_TR, D), lambda t, f, ti, e, nv: (ti[t], 0)),
                      pl.BlockSpec((None, D, TF), lambda t, f, ti, e, nv: (e[t], 0, fidx(t, f, nv))),
                      pl.BlockSpec((None, D, TF), lambda t, f, ti, e, nv: (e[t], 0, fidx(t, f, nv))),
                      pl.BlockSpec((None, TF, D), lambda t, f, ti, e, nv: (e[t], fidx(t, f, nv), 0))],
            out_specs=pl.BlockSpec((MOE_TR, D), lambda t, f, ti, e, nv: (ti[t], 0)),
            scratch_shapes=[pltpu.VMEM((MOE_TR, D), F32)]),
        out_shape=jax.ShapeDtypeStruct((rmax, D), BF16),
        compiler_params=_params("arbitrary", "arbitrary"),
        name="moe_ffn",
    )(*plan, xs, w1, w3, w2)


def _moe_combine_kernel(s_ref, q_ref, valid_ref, first_ref, last_ref, pos_ref, ys_ref, xp_ref, xs_ref, gate_ref,
                        fg_ref, op_ref, os_ref, acc_scr, *, ntp):
    p = pl.program_id(0)

    @pl.when(valid_ref[p] == 1)
    def _():
        rows = ys_ref.shape[0]
        col = (lax.broadcasted_iota(jnp.int32, (1, rows), 1) + q_ref[p] * rows).astype(F32)
        sel = (jnp.where(pos_ref[:, 0:1] == col, pos_ref[:, 2:3], 0.0)
               + jnp.where(pos_ref[:, 1:2] == col, pos_ref[:, 3:4], 0.0)).astype(BF16)
        @pl.when(first_ref[p] == 1)
        def _():
            acc_scr[...] = jnp.zeros_like(acc_scr)

        acc_scr[...] += _dot(sel, ys_ref[...])

        @pl.when(last_ref[p] == 1)
        def _():
            s = s_ref[p]
            x = jnp.where(s < ntp, xp_ref[...], xs_ref[...])
            y = x + gate_ref[...] * acc_scr[...]
            out = y * lax.rsqrt(jnp.mean(y * y, axis=-1, keepdims=True) + EPS) * fg_ref[...]

            @pl.when(s < ntp)
            def _():
                op_ref[...] = out

            @pl.when(s >= ntp)
            def _():
                os_ref[...] = out


def _moe_combine(ys, pos_cols, plan, xp, xs, mod, final_g, rows_per_group):
    Np, D = xp.shape
    Ns = xs.shape[0]
    SB = MOE_SB
    ntp = Np // SB
    pmax = plan[0].shape[0]

    def blk(p, s, *_):
        return s[p]

    def tok_p(p, s, *_):
        return (jnp.minimum(s[p], ntp - 1), 0)

    def tok_s(p, s, *_):
        return (jnp.maximum(s[p] - ntp, 0), 0)

    def gate_idx(p, s, *_):
        return (jnp.where(s[p] < ntp, 0, 1 + ((s[p] - ntp) * SB) // rows_per_group), 5, 0, 0)

    return pl.pallas_call(
        functools.partial(_moe_combine_kernel, ntp=ntp),
        grid_spec=pltpu.PrefetchScalarGridSpec(
            num_scalar_prefetch=5, grid=(pmax,),
            in_specs=[pl.BlockSpec((SB, SUB), lambda p, s, q, *_: (s[p], 0)),
                      pl.BlockSpec((MOE_TRG, D), lambda p, s, q, *_: (q[p], 0)),
                      pl.BlockSpec((SB, D), tok_p),
                      pl.BlockSpec((SB, D), tok_s),
                      pl.BlockSpec((None, None, 1, D), gate_idx),
                      pl.BlockSpec((1, D), lambda p, *_: (0, 0))],
            out_specs=[pl.BlockSpec((SB, D), tok_p), pl.BlockSpec((SB, D), tok_s)],
            scratch_shapes=[pltpu.VMEM((SB, D), F32)]),
        out_shape=[jax.ShapeDtypeStruct((Np, D), F32), jax.ShapeDtypeStruct((Ns, D), F32)],
        compiler_params=_params("arbitrary"),
        name="moe_combine",
    )(*plan, pos_cols, ys, xp, xs, mod, final_g.reshape(1, D))


def _moe(xp, xs, g, mod, router_w, w1, w3, w2, final_g, rows_per_group):
    N = xp.shape[0] + xs.shape[0]
    h, info, info_t, cum = _moe_route(xp, xs, g, mod, router_w, rows_per_group)
    pos_cols, pos_rows, gather_plan, combine_plan, ffn_plan, rmax = _moe_plan(info, info_t, cum, N)
    x_sorted = _moe_gather(h, pos_rows, gather_plan, rmax)
    y_sorted = _moe_ffn(x_sorted, ffn_plan, w1, w3, w2)
    return _moe_combine(y_sorted, pos_cols, combine_plan, xp, xs, mod, final_g, rows_per_group)


def _gla_levels(C):
    lv, c = [], C // 2
    while c >= SUB:
        lv.append(c)
        c //= 2
    return lv


def _gla_tables(C):
    levels = _gla_levels(C)
    nr = 2 + 2 * len(levels)
    mat = np.zeros((2, nr * C, C), np.float32)
    code = np.zeros((2, C, C), np.int32)
    for d in range(2):
        p = np.arange(C) if d == 0 else C - 1 - np.arange(C)
        pi, pj = p[:, None], p[None, :]
        mat[d, 0:C] = pj <= pi
        mat[d, C:2 * C] = pj > pi
        code[d] = np.where((pj <= pi) & (pi // SUB == pj // SUB), 1, 0)
        for lv, c in enumerate(levels):
            blk = pi // c
            later = blk % 2 == 1
            mat[d, (2 + 2 * lv) * C:(3 + 2 * lv) * C] = later & (pj > blk * c - 1) & (pj <= pi)
            mat[d, (3 + 2 * lv) * C:(4 + 2 * lv) * C] = (~later) & (pj > pi) & (pj <= (blk + 1) * c - 1)
            pair = (pi // (2 * c) == pj // (2 * c)) & (pi // c != pj // c) & (pj <= pi)
            code[d] = np.where(pair, 2 + lv, code[d])
    ones = np.zeros((SUB * LANES, C), np.float32)
    for jj in range(SUB):
        ones[jj * LANES:(jj + 1) * LANES, jj::SUB] = 1.0
    return jnp.asarray(mat, dtype=BF16), jnp.asarray(code), jnp.asarray(ones, dtype=BF16)


def _bcast_sublane(x, jj):
    r, w = x.shape
    x3 = x.reshape(r // SUB, SUB, w)
    return jnp.broadcast_to(x3[:, jj:jj + 1, :], x3.shape).reshape(r, w)


def _t128(x):
    r, w = x.shape
    if w > LANES:
        return jnp.concatenate([x[:, i:i + LANES].T for i in range(0, w, LANES)], axis=0)
    return jnp.concatenate([x[i:i + LANES, :].T for i in range(0, r, LANES)], axis=1)


def _gla_kernel(q_ref, k_ref, v_ref, g_ref, lr_ref, wg_ref, ba_ref, mat_ref, code_ref, ones_ref,
                s0f_ref, s0b_ref, ng_ref, o_ref, sf_ref, sb_ref, st_scr, of_scr, *, n, C):
    d = pl.program_id(1)
    c = pl.program_id(2)
    levels = _gla_levels(C)

    @pl.when(jnp.logical_and(c == 0, d == 0))
    def _():
        for h in range(C_H):
            st_scr[h] = _t128(s0f_ref[h])

    @pl.when(jnp.logical_and(c == 0, d == 1))
    def _():
        for h in range(C_H):
            st_scr[h] = _t128(s0b_ref[h])

    xg = jnp.dot(lr_ref[...], wg_ref[...], precision=HIGHEST, preferred_element_type=F32) + ba_ref[...]
    la = (jnp.minimum(xg, 0.0) - jnp.log1p(jnp.exp(-jnp.abs(xg)))) * (1.0 / C_TAU)
    hi = la.astype(BF16)
    lo = (la - hi.astype(F32)).astype(BF16)
    mat = mat_ref[...]
    cum = _dot(mat, hi) + _dot(mat, lo)
    code = code_ref[...]
    ones = ones_ref[...]

    outs = []
    for h in range(C_H):
        ks = slice(h * C_DK, (h + 1) * C_DK)
        qh = q_ref[:, ks] * (C_DK ** -0.5)
        kh = k_ref[:, ks]
        vh = v_ref[:, h * C_DV:(h + 1) * C_DV]
        vb = vh.astype(BF16)
        b = cum[0:C, ks]
        b_rest = cum[C:2 * C, ks]
        ps = []
        for jj in range(SUB):
            dec = jnp.exp(jnp.minimum(b - _bcast_sublane(b, jj), 0.0))
            ps.append((qh * _bcast_sublane(kh, jj) * dec).astype(BF16))
        att = jnp.where(code == 1, _dot(jnp.concatenate(ps, axis=1), ones), 0.0)
        for lv in range(len(levels)):
            eq = cum[(2 + 2 * lv) * C:(3 + 2 * lv) * C, ks]
            ek = cum[(3 + 2 * lv) * C:(4 + 2 * lv) * C, ks]
            a_lv = _dot_nt((qh * jnp.exp(eq)).astype(BF16), (kh * jnp.exp(ek)).astype(BF16))
            att = jnp.where(code == 2 + lv, a_lv, att)
        st = st_scr[h]
        o = _dot(att.astype(BF16), vb) + _dot_nt((qh * jnp.exp(b)).astype(BF16), st.astype(BF16))
        b_end = b[0:1, :] + b_rest[0:1, :]
        ke = (kh * jnp.exp(b_rest)).astype(BF16)
        st_scr[h] = jnp.exp(b_end) * st + _dot(_t128(vh).astype(BF16), ke)
        outs.append(o)
    o_all = jnp.concatenate(outs, axis=-1)

    @pl.when(d == 0)
    def _():
        of_scr[c] = o_all

    @pl.when(d == 1)
    def _():
        tot = o_all + of_scr[n - 1 - c]
        res = []
        for h in range(C_H):
            sl = slice(h * C_DV, (h + 1) * C_DV)
            t = tot[:, sl]
            y = t * lax.rsqrt(jnp.mean(t * t, axis=-1, keepdims=True) + EPS) * ng_ref[:, sl]
            res.append(y * _silu(g_ref[:, sl]))
        o_ref[...] = jnp.concatenate(res, axis=-1).astype(o_ref.dtype)

    @pl.when(jnp.logical_and(c == n - 1, d == 0))
    def _():
        for h in range(C_H):
            sf_ref[h] = _t128(st_scr[h])

    @pl.when(jnp.logical_and(c == n - 1, d == 1))
    def _():
        for h in range(C_H):
            sb_ref[h] = _t128(st_scr[h])


def _gla(z, B, T, w_a2, b_a, s0f, s0b, norm_g):
    C = GLA_CHUNK
    n = T // C
    HK = C_H * C_DK
    HV = C_H * C_DV
    mat, code, ones = _gla_tables(C)
    nr = mat.shape[1] // C
    wg = jnp.zeros((2, LANES, HK), F32)
    for dr in range(2):
        wg = wg.at[dr, dr * C_RANK:(dr + 1) * C_RANK, :].set(w_a2[dr])

    def row(b, d, c):
        return b * n + c + d * (n - 1 - 2 * c)

    st_spec = pl.BlockSpec((None, C_H, C_DK, C_DV), lambda b, d, c: (b, 0, 0, 0))
    st_shape = jax.ShapeDtypeStruct((B, C_H, C_DK, C_DV), F32)
    return pl.pallas_call(
        functools.partial(_gla_kernel, n=n, C=C),
        grid=(B, 2, n),
        in_specs=[pl.BlockSpec((C, HK), lambda b, d, c: (row(b, d, c), 0)),
                  pl.BlockSpec((C, HK), lambda b, d, c: (row(b, d, c), 1)),
                  pl.BlockSpec((C, HV), lambda b, d, c: (row(b, d, c), 1)),
                  pl.BlockSpec((C, HV), lambda b, d, c: (row(b, d, c), 2)),
                  pl.BlockSpec((C, LANES), lambda b, d, c: (row(b, d, c), (2 * HK + 2 * HV) // LANES)),
                  pl.BlockSpec((None, LANES, HK), lambda b, d, c: (d, 0, 0)),
                  pl.BlockSpec((None, 1, HK), lambda b, d, c: (d, 0, 0)),
                  pl.BlockSpec((None, nr * C, C), lambda b, d, c: (d, 0, 0)),
                  pl.BlockSpec((None, C, C), lambda b, d, c: (d, 0, 0)),
                  pl.BlockSpec((SUB * LANES, C), lambda b, d, c: (0, 0)),
                  st_spec, st_spec,
                  pl.BlockSpec((1, HV), lambda b, d, c: (0, 0))],
        out_specs=[pl.BlockSpec((C, HV), lambda b, d, c: (b * n + (n - 1) - d * c, 0)),
                   st_spec, st_spec],
        out_shape=[jax.ShapeDtypeStruct((B * T, HV), BF16), st_shape, st_shape],
        scratch_shapes=[pltpu.VMEM((C_H, C_DV, C_DK), F32), pltpu.VMEM((n, C, HV), F32)],
        compiler_params=_params("arbitrary", "arbitrary", "arbitrary"),
        name="gla",
    )(z, z, z, z, z, wg, b_a.reshape(2, 1, HK), mat, code, ones, s0f, s0b, norm_g.reshape(1, HV))


def _run_stream(x, B, T, mods, ctx, p):
    N, D = x.shape
    rpg = N // mods[0].shape[0]
    TM = min(1024, rpg)

    z = _norm_mm(x, p['norm1_g'][0], mods[0], (0, 1), p['even_w_in'][0], 1280, TM, rpg)
    if ctx is None:
        s0 = jnp.zeros((B, A_H, A_DK, A_DV), F32)
        a_f0, a_b0, cache_k, cache_v = s0, s0, None, None
    else:
        cache_k, cache_v, a_f0, a_b0 = ctx[0], ctx[1], ctx[2], ctx[3]
    o_a, a_sf, a_sb = _retention(z, B, T, p['a_log_gamma'][0], a_f0, a_b0, p['a_norm_g'][0])
    qpad, k_norm, k_rot, v_bf = _bprep(z, T, p['b_q_g'][0], p['b_k_g'][0], rope=ctx is not None)
    o_b = _attention(qpad, k_rot, v_bf, B, T, cache_k, cache_v)
    x = _proj_res(x, mods[0], 2, [o_a, o_b], p['even_w_out'][0], rpg)
    x = _ffn(x, p['norm2_g'][0], mods[0], p['ff_w1'][0], p['ff_w3'][0], p['ff_w2'][0], rpg)

    w_in = p['odd_w_in'][0]
    pad = (-w_in.shape[1]) % 640
    z1 = _norm_mm(x, p['norm1_g'][1], mods[1], (0, 1), jnp.pad(w_in, ((0, 0), (0, pad))), 640, TM, rpg)
    if ctx is None:
        s0 = jnp.zeros((B, C_H, C_DK, C_DV), F32)
        c_f0, c_b0 = s0, s0
    else:
        c_f0, c_b0 = ctx[4], ctx[5]
    o_c, c_sf, c_sb = _gla(z1, B, T, p['c_w_a2'][0], p['c_b_a'][0], c_f0, c_b0, p['c_norm_g'][0])
    x = _proj_res(x, mods[1], 2, [o_c], p['odd_w_out'][0], rpg)
    v_raw = z[:, EVEN_V_OFF:EVEN_V_OFF + B_HKV * B_HD]
    return x, (k_norm, v_raw, a_sf, a_sb, c_sf, c_sb)


EVEN_V_OFF = A_H * (2 * A_DK + 2 * A_DV) + (B_H + B_HKV) * B_HD


def kernel(x_prompt, x_sample, c, cache_b_k, cache_b_v, state_a_fwd, state_a_bwd, state_c_fwd, state_c_bwd,
           c_ctx, w_mod, b_mod, norm1_g, norm2_g, final_g, even_w_in, even_w_out, a_log_gamma, a_norm_g,
           b_q_g, b_k_g, odd_w_in, c_w_a2, c_b_a, c_norm_g, odd_w_out, ff_w1, ff_w3, ff_w2,
           router_w, moe_w1, moe_w3, moe_w2):
    Bp, Tp, D = x_prompt.shape
    Bs, Ts, _ = x_sample.shape
    L = w_mod.shape[0]
    assert L == 2 and even_w_in.shape[0] == 1 and odd_w_in.shape[0] == 1
    p = dict(norm1_g=norm1_g, norm2_g=norm2_g, final_g=final_g, even_w_in=even_w_in, even_w_out=even_w_out,
             a_log_gamma=a_log_gamma, a_norm_g=a_norm_g, b_q_g=b_q_g, b_k_g=b_k_g, odd_w_in=odd_w_in,
             c_w_a2=c_w_a2, c_b_a=c_b_a, c_norm_g=c_norm_g, odd_w_out=odd_w_out, ff_w1=ff_w1, ff_w3=ff_w3,
             ff_w2=ff_w2, router_w=router_w, moe_w1=moe_w1, moe_w3=moe_w3, moe_w2=moe_w2)

    rows = 8
    conds = jnp.concatenate([c_ctx[None, :], c, jnp.zeros((rows - 1 - Bs, D), F32)], axis=0)
    mod = _modulation(conds, w_mod, b_mod).reshape(L, rows, 6, 1, D)
    mods_p = [mod[l, 0:1] for l in range(L)]
    mods_s = [mod[l, 1:1 + Bs] for l in range(L)]

    x_p, kept = _run_stream(x_prompt.reshape(Bp * Tp, D), Bp, Tp, mods_p, None, p)
    nk = B_HKV * B_HD
    ctx = (cache_b_k[:, 0].reshape(Bs, -1, nk), cache_b_v[:, 0].reshape(Bs, -1, nk),
           state_a_fwd[:, 0], state_a_bwd[:, 0], state_c_fwd[:, 0], state_c_bwd[:, 0])
    x_s, _ = _run_stream(x_sample.reshape(Bs * Ts, D), Bs, Ts, mods_s, ctx, p)
    y_p, y_s = _moe(x_p, x_s, norm2_g[1], mod[1, 0:1 + Bs], router_w[0], moe_w1[0], moe_w3[0], moe_w2[0],
                    final_g, Ts)

    k_norm, v_raw, a_sf, a_sb, c_sf, c_sb = kept
    return (y_p.reshape(Bp, Tp, D), y_s.reshape(Bs, Ts, D),
            k_norm.reshape(Bp, 1, Tp, B_HKV, B_HD), v_raw.reshape(Bp, 1, Tp, B_HKV, B_HD),
            a_sf[:, None], a_sb[:, None], c_sf[:, None], c_sb[:, None])
```

```python
import functools

import numpy as np
import jax
import jax.numpy as jnp
from jax import lax
from jax.experimental import pallas as pl
from jax.experimental.pallas import tpu as pltpu

F32 = jnp.float32
BF16 = jnp.bfloat16
EPS = 1e-6
HIGHEST = lax.Precision.HIGHEST
LOG2E = 1.4426950408889634

VMEM_LIMIT_BYTES = 56 * 1024 * 1024

A_H, A_DK, A_DV = 4, 128, 256
B_H, B_HKV, B_HD = 8, 2, 64
C_H, C_DK, C_DV, C_RANK = 4, 128, 256, 16
C_TAU = 16.0
GRID_W = 64
ROPE_THETA = 10000.0
N_EXPERTS = 8
LANES = 128
SUB = 8
RET_CHUNK = 128
GLA_CHUNK = 128
Q_TILE = 128


def _params(*sem):
    return pltpu.CompilerParams(dimension_semantics=sem, vmem_limit_bytes=VMEM_LIMIT_BYTES)


def _dot(a, b):
    return jnp.dot(a, b, preferred_element_type=F32)


def _dot_nt(a, b):
    return lax.dot_general(a, b, (((1,), (1,)), ((), ())), preferred_element_type=F32)


def _silu(x):
    return x * jax.nn.sigmoid(x)


def _norm_mod(x, g, sh, sc):
    r = lax.rsqrt(jnp.mean(x * x, axis=-1, keepdims=True) + EPS)
    return (x * r * g) * (1.0 + sc) + sh


def _mod_kernel(c_ref, w_ref, b_ref, o_ref):
    c = c_ref[...]
    o_ref[...] = jnp.dot(_silu(c), w_ref[...], precision=HIGHEST, preferred_element_type=F32) + b_ref[...]


def _modulation(conds, w_mod, b_mod):
    L, D, D6 = w_mod.shape
    R = conds.shape[0]
    TN = 1024
    return pl.pallas_call(
        _mod_kernel,
        grid=(L, D6 // TN),
        in_specs=[pl.BlockSpec((R, D), lambda l, j: (0, 0)),
                  pl.BlockSpec((None, D, TN), lambda l, j: (l, 0, j)),
                  pl.BlockSpec((None, 1, TN), lambda l, j: (l, 0, j))],
        out_specs=pl.BlockSpec((None, R, TN), lambda l, j: (l, 0, j)),
        out_shape=jax.ShapeDtypeStruct((L, R, D6), F32),
        compiler_params=_params("arbitrary", "arbitrary"),
        name="modulation",
    )(conds, w_mod, b_mod.reshape(L, 1, D6))


def _mod_spec(part, D, TM, rows_per_group, axis):
    def idx(*g):
        return ((g[axis] * TM) // rows_per_group, part, 0, 0)
    return pl.BlockSpec((None, None, 1, D), idx)


MIX_MAIN = A_H * (2 * A_DK + 2 * A_DV)
MIX_TN = 768


def _norm_mm_kernel(x_ref, g_ref, sh_ref, sc_ref, w_ref, we_ref, o_ref, oe_ref, h_scr, *, nmain):
    j = pl.program_id(1)

    @pl.when(j == 0)
    def _():
        h_scr[...] = _norm_mod(x_ref[...], g_ref[...], sh_ref[...], sc_ref[...]).astype(BF16)

    @pl.when(j < nmain)
    def _():
        o_ref[...] = _dot(h_scr[...], w_ref[...].astype(BF16)).astype(o_ref.dtype)

    @pl.when(j == nmain)
    def _():
        oe_ref[...] = _dot(h_scr[...], we_ref[...].astype(BF16))


def _norm_mm(x, g, mod, parts, w, w_extra, extra_block, TM, rows_per_group):
    N, D = x.shape
    nmain = MIX_MAIN // MIX_TN
    WE = extra_block[0]
    return pl.pallas_call(
        functools.partial(_norm_mm_kernel, nmain=nmain),
        grid=(N // TM, nmain + 1),
        in_specs=[pl.BlockSpec((TM, D), lambda i, j: (i, 0)),
                  pl.BlockSpec((1, D), lambda i, j: (0, 0)),
                  _mod_spec(parts[0], D, TM, rows_per_group, 0),
                  _mod_spec(parts[1], D, TM, rows_per_group, 0),
                  pl.BlockSpec((D, MIX_TN), lambda i, j: (0, jnp.minimum(j, nmain - 1))),
                  pl.BlockSpec((D, WE), lambda i, j: (0, extra_block[1]))],
        out_specs=[pl.BlockSpec((TM, MIX_TN), lambda i, j: (i, jnp.minimum(j, nmain - 1))),
                   pl.BlockSpec((TM, WE), lambda i, j: (i, 0))],
        out_shape=[jax.ShapeDtypeStruct((N, MIX_MAIN), BF16), jax.ShapeDtypeStruct((N, WE), F32)],
        scratch_shapes=[pltpu.VMEM((TM, D), BF16)],
        compiler_params=_params("arbitrary", "arbitrary"),
        name="norm_mm",
    )(x, g.reshape(1, D), mod, mod, w, w_extra)


def _ret_kernel(lg_ref, q_ref, k_ref, v_ref, ag_ref, s0f_ref, s0b_ref, ng_ref,
                o_ref, sf_ref, sb_ref, s_scr, of_scr, *, n, C):
    d = pl.program_id(1)
    c = pl.program_id(2)

    @pl.when(jnp.logical_and(c == 0, d == 0))
    def _():
        s_scr[...] = s0f_ref[...]

    @pl.when(jnp.logical_and(c == 0, d == 1))
    def _():
        s_scr[...] = s0b_ref[...]

    df = d.astype(F32)
    sgn = 1.0 - 2.0 * df
    ii = lax.broadcasted_iota(jnp.int32, (C, C), 0).astype(F32)
    jj = lax.broadcasted_iota(jnp.int32, (C, C), 1).astype(F32)
    dd = (ii - jj) * sgn
    feeds = dd >= 0.0
    ddc = jnp.maximum(dd, 0.0)
    ri = lax.broadcasted_iota(jnp.int32, (C, 1), 0).astype(F32)
    pos_q = (ri + 1.0) + df * (C - 2.0 * ri - 1.0)
    pos_k = (C - 1.0 - ri) + df * (2.0 * ri - C + 1.0)
    chunk_len = jnp.full((1, A_DV), float(C), F32)

    outs = []
    for h in range(A_H):
        lg = lg_ref[d, h]
        dmask = jnp.where(feeds, jnp.exp(lg * ddc), 0.0)
        qh = q_ref[:, h * A_DK:(h + 1) * A_DK].astype(F32) * (A_DK ** -0.5)
        kh = k_ref[:, h * A_DK:(h + 1) * A_DK].astype(F32)
        vh = v_ref[:, h * A_DV:(h + 1) * A_DV].astype(BF16)
        s = s_scr[h]
        att = _dot_nt(qh.astype(BF16), kh.astype(BF16)) * dmask
        o = _dot(att.astype(BF16), vh) + _dot((qh * jnp.exp(lg * pos_q)).astype(BF16), s.astype(BF16))
        kd = kh * jnp.exp(lg * pos_k)
        s_scr[h] = jnp.exp(lg * chunk_len) * s + _dot(kd.T.astype(BF16), vh)
        outs.append(o)
    o_all = jnp.concatenate(outs, axis=-1)

    @pl.when(d == 0)
    def _():
        of_scr[c] = o_all

    @pl.when(d == 1)
    def _():
        tot = o_all + of_scr[n - 1 - c]
        res = []
        for h in range(A_H):
            sl = slice(h * A_DV, (h + 1) * A_DV)
            t = tot[:, sl]
            dev = t - jnp.mean(t, axis=-1, keepdims=True)
            y = dev * lax.rsqrt(jnp.mean(dev * dev, axis=-1, keepdims=True) + EPS) * ng_ref[:, sl]
            res.append(y * _silu(ag_ref[:, sl].astype(F32)))
        o_ref[...] = jnp.concatenate(res, axis=-1).astype(o_ref.dtype)

    @pl.when(jnp.logical_and(c == n - 1, d == 0))
    def _():
        sf_ref[...] = s_scr[...]

    @pl.when(jnp.logical_and(c == n - 1, d == 1))
    def _():
        sb_ref[...] = s_scr[...]


def _retention(z, B, T, log_gamma, s0f, s0b, norm_g):
    C = RET_CHUNK
    n = T // C
    HV = A_H * A_DV

    def row(b, d, c):
        return b * n + c + d * (n - 1 - 2 * c)

    st_spec = pl.BlockSpec((None, A_H, A_DK, A_DV), lambda b, d, c: (b, 0, 0, 0))
    st_shape = jax.ShapeDtypeStruct((B, A_H, A_DK, A_DV), F32)
    return pl.pallas_call(
        functools.partial(_ret_kernel, n=n, C=C),
        grid=(B, 2, n),
        in_specs=[pl.BlockSpec(memory_space=pltpu.SMEM),
                  pl.BlockSpec((C, 512), lambda b, d, c: (row(b, d, c), 0)),
                  pl.BlockSpec((C, 512), lambda b, d, c: (row(b, d, c), 1)),
                  pl.BlockSpec((C, HV), lambda b, d, c: (row(b, d, c), 1)),
                  pl.BlockSpec((C, HV), lambda b, d, c: (row(b, d, c), 2)),
                  st_spec, st_spec,
                  pl.BlockSpec((1, HV), lambda b, d, c: (0, 0))],
        out_specs=[pl.BlockSpec((C, HV), lambda b, d, c: (b * n + (n - 1) - d * c, 0)),
                   st_spec, st_spec],
        out_shape=[jax.ShapeDtypeStruct((B * T, HV), BF16), st_shape, st_shape],
        scratch_shapes=[pltpu.VMEM((A_H, A_DK, A_DV), F32), pltpu.VMEM((n, C, HV), F32)],
        compiler_params=_params("arbitrary", "arbitrary", "arbitrary"),
        name="retention",
    )(log_gamma, z, z, z, z, s0f, s0b, norm_g.reshape(1, HV))


def _group_sum_matrix(width, group):
    i = np.arange(width)
    return jnp.asarray((i[:, None] // group == i[None, :] // group).astype(np.float32), dtype=BF16)


def _q_pad_matrix():
    m = np.zeros((B_H * B_HD, B_H * LANES), np.float32)
    g = B_H // B_HKV
    for h in range(B_H):
        for t in range(B_HD):
            m[h * B_HD + t, h * LANES + (h // g) * B_HD + t] = 1.0
    return jnp.asarray(m, dtype=BF16)


def _rope_tables(T):
    rows = T // GRID_W
    row = np.repeat(np.arange(rows, dtype=np.float64), GRID_W)
    col = np.tile(np.arange(GRID_W, dtype=np.float64), rows)
    nq = B_HD // 4
    inv = ROPE_THETA ** (-np.arange(nq, dtype=np.float64) / nq)
    ang = np.concatenate([row[:, None] * inv, col[:, None] * inv], axis=-1)
    cos = np.repeat(np.cos(ang), 2, axis=-1)
    sin = np.repeat(np.sin(ang), 2, axis=-1)
    sign = np.tile(np.array([-1.0, 1.0]), B_HD // 2)
    reps = LANES // B_HD
    return (jnp.asarray(np.tile(cos, (1, reps)), dtype=F32),
            jnp.asarray(np.tile(sin * sign, (1, reps)), dtype=F32))


def _group_rmsnorm(x, gsum, g):
    x2 = x * x
    hi = x2.astype(BF16)
    lo = (x2 - hi.astype(F32)).astype(BF16)
    ss = _dot(hi, gsum) + _dot(lo, gsum)
    return x * lax.rsqrt(ss * (1.0 / B_HD) + EPS) * g


def _rotate_pairs(x, cos, sin_signed):
    n = x.shape[1]
    lane = lax.broadcasted_iota(jnp.int32, x.shape, 1)
    partner = jnp.where(lane % 2 == 0, pltpu.roll(x, n - 1, 1), pltpu.roll(x, 1, 1))
    reps = n // LANES
    if reps > 1:
        cos = jnp.concatenate([cos] * reps, axis=1)
        sin_signed = jnp.concatenate([sin_signed] * reps, axis=1)
    return x * cos + partner * sin_signed


def _bprep_kernel(z_ref, qg_ref, kg_ref, cos_ref, sin_ref, gq_ref, gk_ref, pad_ref,
                  qpad_ref, kn_ref, kr_ref, vb_ref, *, rope):
    nq = B_H * B_HD
    nk = B_HKV * B_HD
    qn = _group_rmsnorm(z_ref[:, 0:nq], gq_ref[...], qg_ref[...])
    kn = _group_rmsnorm(z_ref[:, nq:nq + nk], gk_ref[...], kg_ref[...])
    kn_ref[...] = kn
    if rope:
        qn = _rotate_pairs(qn, cos_ref[...], sin_ref[...])
        kn = _rotate_pairs(kn, cos_ref[...], sin_ref[...])
    kr_ref[...] = kn.astype(BF16)
    vb_ref[...] = z_ref[:, nq + nk:nq + 2 * nk].astype(BF16)
    qs = (qn * (B_HD ** -0.5 * LOG2E)).astype(BF16)
    qpad_ref[...] = _dot(qs, pad_ref[...]).astype(BF16)


def _bprep(z, T, q_g, k_g, rope):
    N = z.shape[0]
    TM = min(512, T)
    nq = B_H * B_HD
    nk = B_HKV * B_HD
    width = nq + 2 * nk
    assert z.shape[1] == width
    cos, sin = _rope_tables(T if rope else TM)
    nt = T // TM if rope else 1
    const = lambda i: (0, 0)
    return pl.pallas_call(
        functools.partial(_bprep_kernel, rope=rope),
        grid=(N // TM,),
        in_specs=[pl.BlockSpec((TM, width), lambda i: (i, 0)),
                  pl.BlockSpec((1, nq), const),
                  pl.BlockSpec((1, nk), const),
                  pl.BlockSpec((TM, LANES), lambda i: (i % nt, 0)),
                  pl.BlockSpec((TM, LANES), lambda i: (i % nt, 0)),
                  pl.BlockSpec((nq, nq), const),
                  pl.BlockSpec((nk, nk), const),
                  pl.BlockSpec((nq, B_H * LANES), const)],
        out_specs=[pl.BlockSpec((TM, B_H * LANES), lambda i: (i, 0)),
                   pl.BlockSpec((TM, nk), lambda i: (i, 0)),
                   pl.BlockSpec((TM, nk), lambda i: (i, 0)),
                   pl.BlockSpec((TM, nk), lambda i: (i, 0))],
        out_shape=[jax.ShapeDtypeStruct((N, B_H * LANES), BF16),
                   jax.ShapeDtypeStruct((N, nk), F32),
                   jax.ShapeDtypeStruct((N, nk), BF16),
                   jax.ShapeDtypeStruct((N, nk), BF16)],
        compiler_params=_params("arbitrary"),
        name="attn_prep",
    )(z, jnp.tile(q_g, B_H).reshape(1, nq), jnp.tile(k_g, B_HKV).reshape(1, nk), cos, sin,
      _group_sum_matrix(nq, B_HD), _group_sum_matrix(nk, B_HD), _q_pad_matrix())


def _lane_fold(x, op):
    acc = x[:, 0:LANES]
    for j in range(1, x.shape[1] // LANES):
        acc = op(acc, x[:, j * LANES:(j + 1) * LANES])
    return acc


def _attn_kernel(*refs, has_cache, kc, nq):
    if has_cache:
        q_ref, k_ref, v_ref, ck_ref, cv_ref, o_ref, s_scr, m_scr, mprev_scr, l_scr, acc_scr = refs
        ncache = ck_ref.shape[0] // kc
    else:
        q_ref, k_ref, v_ref, o_ref, s_scr, m_scr, mprev_scr, l_scr, acc_scr = refs
        ncache = 0
    i = pl.program_id(1)
    tq = q_ref.shape[0]
    nlat = k_ref.shape[0] // kc
    ngrp = kc // LANES

    def score(c, kblk):
        q = jnp.concatenate([q_ref[:, h * LANES:(h + 1) * LANES] for h in range(B_H)], axis=0)
        s = _dot_nt(q, kblk)
        s_scr[c] = s
        m_scr[...] = jnp.maximum(m_scr[...], _lane_fold(s, jnp.maximum))

    def weight(c, vblk):
        s = s_scr[c]
        mp = mprev_scr[...]
        ps = [jnp.exp2(s[:, j * LANES:(j + 1) * LANES] - mp) for j in range(ngrp)]
        tot = ps[0]
        for pj in ps[1:]:
            tot = tot + pj
        l_scr[...] += tot
        acc_scr[...] += _dot(jnp.concatenate(ps, axis=1).astype(BF16), vblk)

    def run(do_weight, do_score):
        def unit(c, kblk, vblk):
            if do_weight:
                weight(c, vblk())
            if do_score:
                score(c, kblk())

        for c in range(ncache):
            unit(c, lambda: ck_ref[c * kc:(c + 1) * kc, :].astype(BF16),
                 lambda: cv_ref[c * kc:(c + 1) * kc, :].astype(BF16))

        def body(c, carry):
            rows = pl.ds(pl.multiple_of(c * kc, kc), kc)
            unit(ncache + c, lambda: k_ref[rows, :], lambda: v_ref[rows, :])
            return carry
        lax.fori_loop(0, nlat, body, 0)

    @pl.when(i < nq)
    def _():
        m_scr[...] = jnp.full(m_scr.shape, -jnp.inf, F32)

    @pl.when(i > 0)
    def _():
        l_scr[...] = jnp.zeros_like(l_scr)
        acc_scr[...] = jnp.zeros_like(acc_scr)

    @pl.when(i == 0)
    def _():
        run(False, True)

    @pl.when(jnp.logical_and(i > 0, i < nq))
    def _():
        run(True, True)

    @pl.when(i == nq)
    def _():
        run(True, False)

    @pl.when(i > 0)
    def _():
        r_all = acc_scr[...] / jnp.sum(l_scr[...], axis=-1, keepdims=True)
        g = B_H // B_HKV
        lane = lax.broadcasted_iota(jnp.int32, (tq, LANES), 1)
        outs = []
        for j in range(B_H // 2):
            pair = []
            for half in range(2):
                h = 2 * j + half
                r = r_all[h * tq:(h + 1) * tq, :]
                if h // g != half:
                    r = pltpu.roll(r, B_HD, 1)
                pair.append(r)
            outs.append(jnp.where(lane < B_HD, pair[0], pair[1]))
        o_ref[...] = jnp.concatenate(outs, axis=-1).astype(o_ref.dtype)

    @pl.when(i < nq)
    def _():
        mprev_scr[...] = jnp.broadcast_to(jnp.max(m_scr[...], axis=-1, keepdims=True), mprev_scr.shape)


def _attention(qpad, kr, vb, B, T, cache_k, cache_v):
    has_cache = cache_k is not None
    TQ = Q_TILE
    nq = T // TQ
    nk = B_HKV * B_HD
    in_specs = [pl.BlockSpec((TQ, B_H * LANES), lambda b, i: (b * nq + jnp.minimum(i, nq - 1), 0)),
                pl.BlockSpec((T, nk), lambda b, i: (b, 0)),
                pl.BlockSpec((T, nk), lambda b, i: (b, 0))]
    args = [qpad, kr, vb]
    kc = min(512, T)
    nchunks = T // kc
    if has_cache:
        P = cache_k.shape[1]
        assert P % kc == 0
        nchunks += P // kc
        in_specs += [pl.BlockSpec((None, P, nk), lambda b, i: (b, 0, 0))] * 2
        args += [cache_k, cache_v]
    R = B_H * TQ
    return pl.pallas_call(
        functools.partial(_attn_kernel, has_cache=has_cache, kc=kc, nq=nq),
        grid=(B, nq + 1),
        in_specs=in_specs,
        out_specs=pl.BlockSpec((TQ, B_H * B_HD), lambda b, i: (b * nq + jnp.maximum(i - 1, 0), 0)),
        out_shape=jax.ShapeDtypeStruct((B * T, B_H * B_HD), BF16),
        scratch_shapes=[pltpu.VMEM((nchunks, R, kc), F32)] + [pltpu.VMEM((R, LANES), F32)] * 4,
        compiler_params=_params("arbitrary", "arbitrary"),
        name="attention",
    )(*args)


def _proj_res_kernel(*refs, n_in):
    x_ref, gate_ref = refs[0], refs[1]
    o_refs = refs[2:2 + n_in]
    w_refs = refs[2 + n_in:2 + 2 * n_in]
    out_ref = refs[2 + 2 * n_in]
    wbf_refs = refs[3 + 2 * n_in:]

    @pl.when(pl.program_id(0) == 0)
    def _():
        for w_ref, wbf_ref in zip(w_refs, wbf_refs):
            wbf_ref[...] = w_ref[...].astype(BF16)

    acc = _dot(o_refs[0][...], wbf_refs[0][...])
    for o_ref, wbf_ref in zip(o_refs[1:], wbf_refs[1:]):
        acc = acc + _dot(o_ref[...], wbf_ref[...])
    out_ref[...] = x_ref[...] + gate_ref[...] * acc


def _proj_res(x, mod, part, acts, w, rows_per_group):
    N, D = x.shape
    TM = min(512, rows_per_group)
    n_in = len(acts)
    widths = [a.shape[1] for a in acts]
    offs = np.cumsum([0] + widths[:-1]).tolist()
    in_specs = [pl.BlockSpec((TM, D), lambda i: (i, 0)),
                _mod_spec(part, D, TM, rows_per_group, 0)]
    in_specs += [pl.BlockSpec((TM, wd), lambda i: (i, 0)) for wd in widths]
    in_specs += [pl.BlockSpec((wd, D), functools.partial(lambda i, blk: (blk, 0), blk=off // wd))
                 for wd, off in zip(widths, offs)]
    return pl.pallas_call(
        functools.partial(_proj_res_kernel, n_in=n_in),
        grid=(N // TM,),
        in_specs=in_specs,
        out_specs=pl.BlockSpec((TM, D), lambda i: (i, 0)),
        out_shape=jax.ShapeDtypeStruct((N, D), F32),
        scratch_shapes=[pltpu.VMEM((wd, D), BF16) for wd in widths],
        compiler_params=_params("arbitrary"),
        name="proj_residual",
    )(x, mod, *acts, *([w] * n_in))


def _ffn_kernel(x_ref, g_ref, sh_ref, sc_ref, gate_ref, w1_ref, w3_ref, w2_ref, out_ref, h_scr, acc_scr, *, nf):
    f = pl.program_id(1)

    @pl.when(f == 0)
    def _():
        h_scr[...] = _norm_mod(x_ref[...], g_ref[...], sh_ref[...], sc_ref[...]).astype(BF16)
        acc_scr[...] = jnp.zeros_like(acc_scr)

    h = h_scr[...]
    a = _dot(h, w1_ref[...].astype(BF16))
    b = _dot(h, w3_ref[...].astype(BF16))
    acc_scr[...] += _dot((_silu(a) * b).astype(BF16), w2_ref[...].astype(BF16))

    @pl.when(f == nf - 1)
    def _():
        out_ref[...] = x_ref[...] + gate_ref[...] * acc_scr[...]


def _ffn(x, g, mod, w1, w3, w2, rows_per_group):
    N, D = x.shape
    FF = w1.shape[1]
    TM, TF = min(1024, rows_per_group), 256
    nf = FF // TF
    return pl.pallas_call(
        functools.partial(_ffn_kernel, nf=nf),
        grid=(N // TM, nf),
        in_specs=[pl.BlockSpec((TM, D), lambda i, f: (i, 0)),
                  pl.BlockSpec((1, D), lambda i, f: (0, 0)),
                  _mod_spec(3, D, TM, rows_per_group, 0),
                  _mod_spec(4, D, TM, rows_per_group, 0),
                  _mod_spec(5, D, TM, rows_per_group, 0),
                  pl.BlockSpec((D, TF), lambda i, f: (0, f)),
                  pl.BlockSpec((D, TF), lambda i, f: (0, f)),
                  pl.BlockSpec((TF, D), lambda i, f: (f, 0))],
        out_specs=pl.BlockSpec((TM, D), lambda i, f: (i, 0)),
        out_shape=jax.ShapeDtypeStruct((N, D), F32),
        scratch_shapes=[pltpu.VMEM((TM, D), BF16), pltpu.VMEM((TM, D), F32)],
        compiler_params=_params("arbitrary", "arbitrary"),
        name="ffn",
    )(x, g.reshape(1, D), mod, mod, mod, w1, w3, w2)


MOE_SB = 1024
MOE_TRG = 256
MOE_TR = 1024


def _two_stream_specs(shape, ntp, ax=0):
    def idx_p(*g):
        return (jnp.minimum(g[ax], ntp - 1), 0)

    def idx_s(*g):
        return (jnp.maximum(g[ax] - ntp, 0), 0)
    return pl.BlockSpec(shape, idx_p), pl.BlockSpec(shape, idx_s)


def _pool_mod_spec(part, D, TM, ntp, rows_per_group):
    def idx(i, *_):
        return (jnp.where(i < ntp, 0, 1 + ((i - ntp) * TM) // rows_per_group), part, 0, 0)
    return pl.BlockSpec((None, None, 1, D), idx)


def _route_kernel(xp_ref, xs_ref, g_ref, sh_ref, sc_ref, rw_ref, tri_ref, h_ref, info_ref, infot_ref, cum_ref,
                  carry_scr, *, ntp):
    i = pl.program_id(0)

    @pl.when(i == 0)
    def _():
        carry_scr[...] = jnp.zeros_like(carry_scr)

    x = jnp.where(i < ntp, xp_ref[...], xs_ref[...])
    h = _norm_mod(x, g_ref[...], sh_ref[...], sc_ref[...])
    h_ref[...] = h.astype(BF16)
    lane = lax.broadcasted_iota(jnp.int32, (x.shape[0], LANES), 1).astype(F32)
    logits = jnp.dot(h, rw_ref[...], precision=HIGHEST, preferred_element_type=F32)
    logits = jnp.where(lane < N_EXPERTS, logits, -jnp.inf)
    m1 = jnp.max(logits, axis=-1, keepdims=True)
    i1 = jnp.min(jnp.where(logits == m1, lane, float(LANES)), axis=-1, keepdims=True)
    rest = jnp.where(lane == i1, -jnp.inf, logits)
    m2 = jnp.max(rest, axis=-1, keepdims=True)
    i2 = jnp.min(jnp.where(rest == m2, lane, float(LANES)), axis=-1, keepdims=True)
    e2 = jnp.exp(m2 - m1)
    w1 = 1.0 / (1.0 + e2)
    w2 = e2 / (1.0 + e2)
    ind = jnp.where(jnp.logical_or(lane == i1, lane == i2), 1.0, 0.0)
    before = _dot(tri_ref[...], ind.astype(BF16)) + carry_scr[...]
    r1 = jnp.sum(jnp.where(lane == i1, before, 0.0), axis=-1, keepdims=True)
    r2 = jnp.sum(jnp.where(lane == i2, before, 0.0), axis=-1, keepdims=True)
    total = carry_scr[...] + jnp.sum(ind, axis=0, keepdims=True)
    carry_scr[...] = total
    cum_ref[...] = total
    info = jnp.where(lane == 0.0, i1, jnp.where(lane == 1.0, i2, jnp.where(lane == 2.0, w1, jnp.where(
        lane == 3.0, w2, jnp.where(lane == 4.0, r1, jnp.where(lane == 5.0, r2, 0.0))))))
    info_ref[...] = info[:, 0:SUB]
    info_t = jnp.concatenate([info[r:r + LANES, :].T for r in range(0, info.shape[0], LANES)], axis=1)
    infot_ref[...] = info_t[0:SUB, :]


def _moe_route(xp, xs, g, mod, router_w, rows_per_group):
    Np, D = xp.shape
    N = Np + xs.shape[0]
    TM = MOE_SB
    ntp = Np // TM
    nt = N // TM
    rw = jnp.pad(router_w, ((0, 0), (0, LANES - router_w.shape[1])))
    tri = jnp.asarray(np.tril(np.ones((TM, TM), np.float32), -1), dtype=BF16)
    xp_spec, xs_spec = _two_stream_specs((TM, D), ntp)
    return pl.pallas_call(
        functools.partial(_route_kernel, ntp=ntp),
        grid=(nt,),
        in_specs=[xp_spec, xs_spec,
                  pl.BlockSpec((1, D), lambda i: (0, 0)),
                  _pool_mod_spec(3, D, TM, ntp, rows_per_group),
                  _pool_mod_spec(4, D, TM, ntp, rows_per_group),
                  pl.BlockSpec((D, LANES), lambda i: (0, 0)),
                  pl.BlockSpec((TM, TM), lambda i: (0, 0))],
        out_specs=[pl.BlockSpec((TM, D), lambda i: (i, 0)),
                   pl.BlockSpec((TM, SUB), lambda i: (i, 0)),
                   pl.BlockSpec((SUB, TM), lambda i: (0, i)),
                   pl.BlockSpec((None, 1, LANES), lambda i: (i, 0, 0))],
        out_shape=[jax.ShapeDtypeStruct((N, D), BF16),
                   jax.ShapeDtypeStruct((N, SUB), F32),
                   jax.ShapeDtypeStruct((SUB, N), F32),
                   jax.ShapeDtypeStruct((nt, 1, LANES), F32)],
        scratch_shapes=[pltpu.VMEM((1, LANES), F32)],
        compiler_params=_params("arbitrary"),
        name="moe_route",
    )(xp, xs, g.reshape(1, D), mod, mod, rw, tri)


def _moe_plan(info, info_t, cum, N):
    E, SB, TRG, TR = N_EXPERTS, MOE_SB, MOE_TRG, MOE_TR
    NB = N // SB
    rmax = 2 * N + E * TR
    RG, RT = rmax // TRG, rmax // TR
    PMAX = RG + E * NB
    i32 = jnp.int32
    cum_e = cum[:, 0, :E].astype(i32).T
    cnt = cum_e[:, -1]
    tiles = (cnt + TR - 1) // TR
    start = TR * (jnp.cumsum(tiles) - tiles)

    startf = start.astype(F32)

    def region_start(e):
        out = jnp.zeros_like(e)
        for k in range(E):
            out = jnp.where(e == float(k), startf[k], out)
        return out

    pos_cols = jnp.concatenate([region_start(info[:, 0:2]) + info[:, 4:6], info[:, 2:4],
                                jnp.zeros((N, 4), F32)], axis=1)
    pos_rows = jnp.concatenate([region_start(info_t[0:2]) + info_t[4:6], jnp.zeros((6, N), F32)],
                               axis=0)

    def region(row0):
        e = jnp.clip(jnp.sum(row0[:, None] >= start[None, :], axis=1) - 1, 0, E - 1)
        return e, row0 - start[e]

    eq, lo = region(jnp.arange(RG, dtype=i32) * TRG)
    hi = jnp.minimum(lo + TRG, cnt[eq])
    first = jnp.sum(cum_e[eq] <= lo[:, None], axis=1)
    last = jnp.sum(cum_e[eq] < hi[:, None], axis=1)
    nblk = jnp.where(hi > lo, last - first + 1, 0)
    pend = jnp.cumsum(nblk)
    npairs = pend[-1]
    p = jnp.arange(PMAX, dtype=i32)
    valid = p < npairs
    pc = jnp.minimum(p, npairs - 1)
    q_of = jnp.minimum(jnp.sum(pend[None, :] <= pc[:, None], axis=1), RG - 1).astype(i32)
    pstart = pend - nblk
    s_of = (first[q_of] + pc - pstart[q_of]).astype(i32)
    g_first = jnp.logical_and(valid, pc == pstart[q_of]).astype(i32)
    gather_plan = (q_of, s_of, valid.astype(i32), g_first)

    order = jnp.argsort(jnp.where(valid, s_of * RG + q_of, jnp.iinfo(jnp.int32).max))
    s2, q2 = s_of[order], q_of[order]
    s2 = jnp.where(valid, s2, s2[npairs - 1])
    q2 = jnp.where(valid, q2, q2[npairs - 1])
    prev = jnp.concatenate([jnp.full((1,), -1, i32), s2[:-1]])
    nxt = jnp.concatenate([s2[1:], jnp.full((1,), -1, i32)])
    c_first = jnp.logical_and(valid, s2 != prev).astype(i32)
    c_last = jnp.logical_and(valid, jnp.logical_or(s2 != nxt, p == npairs - 1)).astype(i32)
    combine_plan = (s2.astype(i32), q2.astype(i32), valid.astype(i32), c_first, c_last)

    te, tlo = region(jnp.arange(RT, dtype=i32) * TR)
    tvalid = jnp.clip(cnt[te] - tlo, 0, TR)
    last_t = jnp.sum(tiles) - 1
    t_idx = jnp.where(tvalid > 0, jnp.arange(RT, dtype=i32), last_t).astype(i32)
    ffn_plan = (t_idx, te[t_idx].astype(i32), tvalid.astype(i32))
    return pos_cols, pos_rows, gather_plan, combine_plan, ffn_plan, rmax


def _moe_gather_kernel(q_ref, s_ref, valid_ref, first_ref, pos_ref, h_ref, out_ref):
    p = pl.program_id(0)

    @pl.when(valid_ref[p] == 1)
    def _():
        rows, toks = out_ref.shape[0], h_ref.shape[0]
        row = (lax.broadcasted_iota(jnp.int32, (rows, 1), 0) + q_ref[p] * rows).astype(F32)
        hit = jnp.logical_or(pos_ref[0:1, :] == row, pos_ref[1:2, :] == row)
        sel = jnp.where(hit, 1.0, 0.0).astype(BF16)
        @pl.when(first_ref[p] == 1)
        def _():
            out_ref[...] = jnp.zeros_like(out_ref)

        out_ref[...] = out_ref[...] + _dot(sel, h_ref[...]).astype(BF16)


def _moe_gather(h, pos_rows, plan, rmax):
    N, D = h.shape
    pmax = plan[0].shape[0]
    return pl.pallas_call(
        _moe_gather_kernel,
        grid_spec=pltpu.PrefetchScalarGridSpec(
            num_scalar_prefetch=4, grid=(pmax,),
            in_specs=[pl.BlockSpec((SUB, MOE_SB), lambda p, q, s, v, f: (0, s[p])),
                      pl.BlockSpec((MOE_SB, D), lambda p, q, s, v, f: (s[p], 0))],
            out_specs=pl.BlockSpec((MOE_TRG, D), lambda p, q, s, v, f: (q[p], 0))),
        out_shape=jax.ShapeDtypeStruct((rmax, D), BF16),
        compiler_params=_params("arbitrary"),
        name="moe_gather",
    )(*plan, pos_rows, h)


def _moe_ffn_kernel(t_ref, e_ref, nv_ref, x_ref, w1_ref, w3_ref, w2_ref, out_ref, acc_scr, *, nf):
    t = pl.program_id(0)
    f = pl.program_id(1)
    nv = nv_ref[t]

    def block(rows):
        @pl.when(f == 0)
        def _():
            acc_scr[rows, :] = jnp.zeros((rows.stop - rows.start, acc_scr.shape[1]), F32)

        x = x_ref[rows, :]
        a = _dot(x, w1_ref[...].astype(BF16))
        b = _dot(x, w3_ref[...].astype(BF16))
        acc_scr[rows, :] += _dot((_silu(a) * b).astype(BF16), w2_ref[...].astype(BF16))

        @pl.when(f == nf - 1)
        def _():
            out_ref[rows, :] = acc_scr[rows, :].astype(out_ref.dtype)

    nsub = MOE_TR // MOE_TRG
    full = nv > (nsub - 1) * MOE_TRG

    @pl.when(full)
    def _():
        block(slice(0, MOE_TR))

    for sub in range(nsub - 1):
        @pl.when(jnp.logical_and(jnp.logical_not(full), sub * MOE_TRG < nv))
        def _():
            block(slice(sub * MOE_TRG, (sub + 1) * MOE_TRG))


def _moe_ffn(xs, plan, w1, w3, w2):
    rmax, D = xs.shape
    FF = w1.shape[2]
    TF = 256
    nf = FF // TF
    RT = rmax // MOE_TR

    def fidx(t, f, nv):
        return jnp.where(nv[t] > 0, f, nf - 1)

    return pl.pallas_call(
        functools.partial(_moe_ffn_kernel, nf=nf),
        grid_spec=pltpu.PrefetchScalarGridSpec(
            num_scalar_prefetch=3, grid=(RT, nf),
            in_specs=[pl.BlockSpec((MOE_TR, D), lambda t, f, ti, e, nv: (ti[t], 0)),
                      pl.BlockSpec((None, D, TF), lambda t, f, ti, e, nv: (e[t], 0, fidx(t, f, nv))),
                      pl.BlockSpec((None, D, TF), lambda t, f, ti, e, nv: (e[t], 0, fidx(t, f, nv))),
                      pl.BlockSpec((None, TF, D), lambda t, f, ti, e, nv: (e[t], fidx(t, f, nv), 0))],
            out_specs=pl.BlockSpec((MOE_TR, D), lambda t, f, ti, e, nv: (ti[t], 0)),
            scratch_shapes=[pltpu.VMEM((MOE_TR, D), F32)]),
        out_shape=jax.ShapeDtypeStruct((rmax, D), BF16),
        compiler_params=_params("arbitrary", "arbitrary"),
        name="moe_ffn",
    )(*plan, xs, w1, w3, w2)


def _moe_combine_kernel(s_ref, q_ref, valid_ref, first_ref, last_ref, pos_ref, ys_ref, xp_ref, xs_ref, gate_ref,
                        fg_ref, op_ref, os_ref, acc_scr, *, ntp):
    p = pl.program_id(0)

    @pl.when(valid_ref[p] == 1)
    def _():
        rows = ys_ref.shape[0]
        col = (lax.broadcasted_iota(jnp.int32, (1, rows), 1) + q_ref[p] * rows).astype(F32)
        sel = (jnp.where(pos_ref[:, 0:1] == col, pos_ref[:, 2:3], 0.0)
               + jnp.where(pos_ref[:, 1:2] == col, pos_ref[:, 3:4], 0.0)).astype(BF16)
        @pl.when(first_ref[p] == 1)
        def _():
            acc_scr[...] = jnp.zeros_like(acc_scr)

        acc_scr[...] += _dot(sel, ys_ref[...])

        @pl.when(last_ref[p] == 1)
        def _():
            s = s_ref[p]
            x = jnp.where(s < ntp, xp_ref[...], xs_ref[...])
            y = x + gate_ref[...] * acc_scr[...]
            out = y * lax.rsqrt(jnp.mean(y * y, axis=-1, keepdims=True) + EPS) * fg_ref[...]

            @pl.when(s < ntp)
            def _():
                op_ref[...] = out

            @pl.when(s >= ntp)
            def _():
                os_ref[...] = out


def _moe_combine(ys, pos_cols, plan, xp, xs, mod, final_g, rows_per_group):
    Np, D = xp.shape
    Ns = xs.shape[0]
    SB = MOE_SB
    ntp = Np // SB
    pmax = plan[0].shape[0]

    def blk(p, s, *_):
        return s[p]

    def tok_p(p, s, *_):
        return (jnp.minimum(s[p], ntp - 1), 0)

    def tok_s(p, s, *_):
        return (jnp.maximum(s[p] - ntp, 0), 0)

    def gate_idx(p, s, *_):
        return (jnp.where(s[p] < ntp, 0, 1 + ((s[p] - ntp) * SB) // rows_per_group), 5, 0, 0)

    return pl.pallas_call(
        functools.partial(_moe_combine_kernel, ntp=ntp),
        grid_spec=pltpu.PrefetchScalarGridSpec(
            num_scalar_prefetch=5, grid=(pmax,),
            in_specs=[pl.BlockSpec((SB, SUB), lambda p, s, q, *_: (s[p], 0)),
                      pl.BlockSpec((MOE_TRG, D), lambda p, s, q, *_: (q[p], 0)),
                      pl.BlockSpec((SB, D), tok_p),
                      pl.BlockSpec((SB, D), tok_s),
                      pl.BlockSpec((None, None, 1, D), gate_idx),
                      pl.BlockSpec((1, D), lambda p, *_: (0, 0))],
            out_specs=[pl.BlockSpec((SB, D), tok_p), pl.BlockSpec((SB, D), tok_s)],
            scratch_shapes=[pltpu.VMEM((SB, D), F32)]),
        out_shape=[jax.ShapeDtypeStruct((Np, D), F32), jax.ShapeDtypeStruct((Ns, D), F32)],
        compiler_params=_params("arbitrary"),
        name="moe_combine",
    )(*plan, pos_cols, ys, xp, xs, mod, final_g.reshape(1, D))


def _moe(xp, xs, g, mod, router_w, w1, w3, w2, final_g, rows_per_group):
    N = xp.shape[0] + xs.shape[0]
    h, info, info_t, cum = _moe_route(xp, xs, g, mod, router_w, rows_per_group)
    pos_cols, pos_rows, gather_plan, combine_plan, ffn_plan, rmax = _moe_plan(info, info_t, cum, N)
    x_sorted = _moe_gather(h, pos_rows, gather_plan, rmax)
    y_sorted = _moe_ffn(x_sorted, ffn_plan, w1, w3, w2)
    return _moe_combine(y_sorted, pos_cols, combine_plan, xp, xs, mod, final_g, rows_per_group)


def _gla_levels(C):
    lv, c = [], C // 2
    while c >= SUB:
        lv.append(c)
        c //= 2
    return lv


def _gla_tables(C):
    levels = _gla_levels(C)
    nr = 2 + 2 * len(levels)
    mat = np.zeros((2, nr * C, C), np.float32)
    code = np.zeros((2, C, C), np.int32)
    for d in range(2):
        p = np.arange(C) if d == 0 else C - 1 - np.arange(C)
        pi, pj = p[:, None], p[None, :]
        mat[d, 0:C] = pj <= pi
        mat[d, C:2 * C] = pj > pi
        code[d] = np.where((pj <= pi) & (pi // SUB == pj // SUB), 1, 0)
        for lv, c in enumerate(levels):
            blk = pi // c
            later = blk % 2 == 1
            mat[d, (2 + 2 * lv) * C:(3 + 2 * lv) * C] = later & (pj > blk * c - 1) & (pj <= pi)
            mat[d, (3 + 2 * lv) * C:(4 + 2 * lv) * C] = (~later) & (pj > pi) & (pj <= (blk + 1) * c - 1)
            pair = (pi // (2 * c) == pj // (2 * c)) & (pi // c != pj // c) & (pj <= pi)
            code[d] = np.where(pair, 2 + lv, code[d])
    ones = np.zeros((SUB * LANES, C), np.float32)
    for jj in range(SUB):
        ones[jj * LANES:(jj + 1) * LANES, jj::SUB] = 1.0
    return jnp.asarray(mat, dtype=BF16), jnp.asarray(code), jnp.asarray(ones, dtype=BF16)


def _bcast_sublane(x, jj):
    r, w = x.shape
    x3 = x.reshape(r // SUB, SUB, w)
    return jnp.broadcast_to(x3[:, jj:jj + 1, :], x3.shape).reshape(r, w)


def _t128(x):
    r, w = x.shape
    if w > LANES:
        return jnp.concatenate([x[:, i:i + LANES].T for i in range(0, w, LANES)], axis=0)
    return jnp.concatenate([x[i:i + LANES, :].T for i in range(0, r, LANES)], axis=1)


def _gla_kernel(q_ref, k_ref, v_ref, g_ref, lr_ref, wg_ref, ba_ref, mat_ref, code_ref, ones_ref,
                s0f_ref, s0b_ref, ng_ref, o_ref, sf_ref, sb_ref, st_scr, of_scr, *, n, C):
    d = pl.program_id(1)
    c = pl.program_id(2)
    levels = _gla_levels(C)

    @pl.when(jnp.logical_and(c == 0, d == 0))
    def _():
        for h in range(C_H):
            st_scr[h] = _t128(s0f_ref[h])

    @pl.when(jnp.logical_and(c == 0, d == 1))
    def _():
        for h in range(C_H):
            st_scr[h] = _t128(s0b_ref[h])

    xg = jnp.dot(lr_ref[...], wg_ref[...], precision=HIGHEST, preferred_element_type=F32) + ba_ref[...]
    la = (jnp.minimum(xg, 0.0) - jnp.log1p(jnp.exp(-jnp.abs(xg)))) * (1.0 / C_TAU)
    hi = la.astype(BF16)
    lo = (la - hi.astype(F32)).astype(BF16)
    mat = mat_ref[...]
    cum = _dot(mat, hi) + _dot(mat, lo)
    code = code_ref[...]
    ones = ones_ref[...]

    outs = []
    for h in range(C_H):
        ks = slice(h * C_DK, (h + 1) * C_DK)
        qh = q_ref[:, ks].astype(F32) * (C_DK ** -0.5)
        kh = k_ref[:, ks].astype(F32)
        vh = v_ref[:, h * C_DV:(h + 1) * C_DV].astype(F32)
        vb = vh.astype(BF16)
        b = cum[0:C, ks]
        b_rest = cum[C:2 * C, ks]
        ps = []
        for jj in range(SUB):
            dec = jnp.exp(jnp.minimum(b - _bcast_sublane(b, jj), 0.0))
            ps.append((qh * _bcast_sublane(kh, jj) * dec).astype(BF16))
        att = jnp.where(code == 1, _dot(jnp.concatenate(ps, axis=1), ones), 0.0)
        for lv in range(len(levels)):
            eq = cum[(2 + 2 * lv) * C:(3 + 2 * lv) * C, ks]
            ek = cum[(3 + 2 * lv) * C:(4 + 2 * lv) * C, ks]
            a_lv = _dot_nt((qh * jnp.exp(eq)).astype(BF16), (kh * jnp.exp(ek)).astype(BF16))
            att = jnp.where(code == 2 + lv, a_lv, att)
        st = st_scr[h]
        o = _dot(att.astype(BF16), vb) + _dot_nt((qh * jnp.exp(b)).astype(BF16), st.astype(BF16))
        b_end = b[0:1, :] + b_rest[0:1, :]
        ke = (kh * jnp.exp(b_rest)).astype(BF16)
        st_scr[h] = jnp.exp(b_end) * st + _dot(_t128(vh).astype(BF16), ke)
        outs.append(o)
    o_all = jnp.concatenate(outs, axis=-1)

    @pl.when(d == 0)
    def _():
        of_scr[c] = o_all

    @pl.when(d == 1)
    def _():
        tot = o_all + of_scr[n - 1 - c]
        res = []
        for h in range(C_H):
            sl = slice(h * C_DV, (h + 1) * C_DV)
            t = tot[:, sl]
            y = t * lax.rsqrt(jnp.mean(t * t, axis=-1, keepdims=True) + EPS) * ng_ref[:, sl]
            res.append(y * _silu(g_ref[:, sl].astype(F32)))
        o_ref[...] = jnp.concatenate(res, axis=-1).astype(o_ref.dtype)

    @pl.when(jnp.logical_and(c == n - 1, d == 0))
    def _():
        for h in range(C_H):
            sf_ref[h] = _t128(st_scr[h])

    @pl.when(jnp.logical_and(c == n - 1, d == 1))
    def _():
        for h in range(C_H):
            sb_ref[h] = _t128(st_scr[h])


def _gla(z, zg, B, T, w_a2, b_a, s0f, s0b, norm_g):
    C = GLA_CHUNK
    n = T // C
    HK = C_H * C_DK
    HV = C_H * C_DV
    mat, code, ones = _gla_tables(C)
    nr = mat.shape[1] // C
    wg = jnp.zeros((2, LANES, HK), F32)
    for dr in range(2):
        wg = wg.at[dr, dr * C_RANK:(dr + 1) * C_RANK, :].set(w_a2[dr])

    def row(b, d, c):
        return b * n + c + d * (n - 1 - 2 * c)

    st_spec = pl.BlockSpec((None, C_H, C_DK, C_DV), lambda b, d, c: (b, 0, 0, 0))
    st_shape = jax.ShapeDtypeStruct((B, C_H, C_DK, C_DV), F32)
    return pl.pallas_call(
        functools.partial(_gla_kernel, n=n, C=C),
        grid=(B, 2, n),
        in_specs=[pl.BlockSpec((C, HK), lambda b, d, c: (row(b, d, c), 0)),
                  pl.BlockSpec((C, HK), lambda b, d, c: (row(b, d, c), 1)),
                  pl.BlockSpec((C, HV), lambda b, d, c: (row(b, d, c), 1)),
                  pl.BlockSpec((C, HV), lambda b, d, c: (row(b, d, c), 2)),
                  pl.BlockSpec((C, LANES), lambda b, d, c: (row(b, d, c), 0)),
                  pl.BlockSpec((None, LANES, HK), lambda b, d, c: (d, 0, 0)),
                  pl.BlockSpec((None, 1, HK), lambda b, d, c: (d, 0, 0)),
                  pl.BlockSpec((None, nr * C, C), lambda b, d, c: (d, 0, 0)),
                  pl.BlockSpec((None, C, C), lambda b, d, c: (d, 0, 0)),
                  pl.BlockSpec((SUB * LANES, C), lambda b, d, c: (0, 0)),
                  st_spec, st_spec,
                  pl.BlockSpec((1, HV), lambda b, d, c: (0, 0))],
        out_specs=[pl.BlockSpec((C, HV), lambda b, d, c: (b * n + (n - 1) - d * c, 0)),
                   st_spec, st_spec],
        out_shape=[jax.ShapeDtypeStruct((B * T, HV), BF16), st_shape, st_shape],
        scratch_shapes=[pltpu.VMEM((C_H, C_DV, C_DK), F32), pltpu.VMEM((n, C, HV), F32)],
        compiler_params=_params("arbitrary", "arbitrary", "arbitrary"),
        name="gla",
    )(z, z, z, z, zg, wg, b_a.reshape(2, 1, HK), mat, code, ones, s0f, s0b, norm_g.reshape(1, HV))


def _run_stream(x, B, T, mods, ctx, p):
    N, D = x.shape
    rpg = N // mods[0].shape[0]
    TM = min(2048, rpg)
    nb = (B_H + 2 * B_HKV) * B_HD

    w_in = p['even_w_in'][0]
    z, zb = _norm_mm(x, p['norm1_g'][0], mods[0], (0, 1), w_in, w_in, (nb, MIX_MAIN // nb), TM, rpg)
    if ctx is None:
        s0 = jnp.zeros((B, A_H, A_DK, A_DV), F32)
        a_f0, a_b0, cache_k, cache_v = s0, s0, None, None
    else:
        cache_k, cache_v, a_f0, a_b0 = ctx[0], ctx[1], ctx[2], ctx[3]
    o_a, a_sf, a_sb = _retention(z, B, T, p['a_log_gamma'][0], a_f0, a_b0, p['a_norm_g'][0])
    qpad, k_norm, k_rot, v_bf = _bprep(zb, T, p['b_q_g'][0], p['b_k_g'][0], rope=ctx is not None)
    o_b = _attention(qpad, k_rot, v_bf, B, T, cache_k, cache_v)
    x = _proj_res(x, mods[0], 2, [o_a, o_b], p['even_w_out'][0], rpg)
    x = _ffn(x, p['norm2_g'][0], mods[0], p['ff_w1'][0], p['ff_w3'][0], p['ff_w2'][0], rpg)

    w_in = p['odd_w_in'][0]
    w_gate = jnp.pad(w_in[:, MIX_MAIN:], ((0, 0), (0, LANES - 2 * C_RANK)))
    z1, z1g = _norm_mm(x, p['norm1_g'][1], mods[1], (0, 1), w_in, w_gate, (LANES, 0), TM, rpg)
    if ctx is None:
        s0 = jnp.zeros((B, C_H, C_DK, C_DV), F32)
        c_f0, c_b0 = s0, s0
    else:
        c_f0, c_b0 = ctx[4], ctx[5]
    o_c, c_sf, c_sb = _gla(z1, z1g, B, T, p['c_w_a2'][0], p['c_b_a'][0], c_f0, c_b0, p['c_norm_g'][0])
    x = _proj_res(x, mods[1], 2, [o_c], p['odd_w_out'][0], rpg)
    v_raw = zb[:, (B_H + B_HKV) * B_HD:]
    return x, (k_norm, v_raw, a_sf, a_sb, c_sf, c_sb)


def kernel(x_prompt, x_sample, c, cache_b_k, cache_b_v, state_a_fwd, state_a_bwd, state_c_fwd, state_c_bwd,
           c_ctx, w_mod, b_mod, norm1_g, norm2_g, final_g, even_w_in, even_w_out, a_log_gamma, a_norm_g,
           b_q_g, b_k_g, odd_w_in, c_w_a2, c_b_a, c_norm_g, odd_w_out, ff_w1, ff_w3, ff_w2,
           router_w, moe_w1, moe_w3, moe_w2):
    Bp, Tp, D = x_prompt.shape
    Bs, Ts, _ = x_sample.shape
    L = w_mod.shape[0]
    assert L == 2 and even_w_in.shape[0] == 1 and odd_w_in.shape[0] == 1
    p = dict(norm1_g=norm1_g, norm2_g=norm2_g, final_g=final_g, even_w_in=even_w_in, even_w_out=even_w_out,
             a_log_gamma=a_log_gamma, a_norm_g=a_norm_g, b_q_g=b_q_g, b_k_g=b_k_g, odd_w_in=odd_w_in,
             c_w_a2=c_w_a2, c_b_a=c_b_a, c_norm_g=c_norm_g, odd_w_out=odd_w_out, ff_w1=ff_w1, ff_w3=ff_w3,
             ff_w2=ff_w2, router_w=router_w, moe_w1=moe_w1, moe_w3=moe_w3, moe_w2=moe_w2)

    rows = 8
    conds = jnp.concatenate([c_ctx[None, :], c, jnp.zeros((rows - 1 - Bs, D), F32)], axis=0)
    mod = _modulation(conds, w_mod, b_mod).reshape(L, rows, 6, 1, D)
    mods_p = [mod[l, 0:1] for l in range(L)]
    mods_s = [mod[l, 1:1 + Bs] for l in range(L)]

    x_p, kept = _run_stream(x_prompt.reshape(Bp * Tp, D), Bp, Tp, mods_p, None, p)
    nk = B_HKV * B_HD
    ctx = (cache_b_k[:, 0].reshape(Bs, -1, nk), cache_b_v[:, 0].reshape(Bs, -1, nk),
           state_a_fwd[:, 0], state_a_bwd[:, 0], state_c_fwd[:, 0], state_c_bwd[:, 0])
    x_s, _ = _run_stream(x_sample.reshape(Bs * Ts, D), Bs, Ts, mods_s, ctx, p)
    y_p, y_s = _moe(x_p, x_s, norm2_g[1], mod[1, 0:1 + Bs], router_w[0], moe_w1[0], moe_w3[0], moe_w2[0],
                    final_g, Ts)

    k_norm, v_raw, a_sf, a_sb, c_sf, c_sb = kept
    return (y_p.reshape(Bp, Tp, D), y_s.reshape(Bs, Ts, D),
            k_norm.reshape(Bp, 1, Tp, B_HKV, B_HD), v_raw.reshape(Bp, 1, Tp, B_HKV, B_HD),
            a_sf[:, None], a_sb[:, None], c_sf[:, None], c_sb[:, None])
```

```python
import functools

import numpy as np
import jax
import jax.numpy as jnp
from jax import lax
from jax.experimental import pallas as pl
from jax.experimental.pallas import tpu as pltpu

F32 = jnp.float32
BF16 = jnp.bfloat16
EPS = 1e-6
HIGHEST = lax.Precision.HIGHEST
LOG2E = 1.4426950408889634

VMEM_LIMIT_BYTES = 56 * 1024 * 1024

A_H, A_DK, A_DV = 4, 128, 256
B_H, B_HKV, B_HD = 8, 2, 64
C_H, C_DK, C_DV, C_RANK = 4, 128, 256, 16
C_TAU = 16.0
GRID_W = 64
ROPE_THETA = 10000.0
N_EXPERTS = 8
LANES = 128
SUB = 8
RET_CHUNK = 128
GLA_CHUNK = 128
Q_TILE = 128


def _params(*sem):
    return pltpu.CompilerParams(dimension_semantics=sem, vmem_limit_bytes=VMEM_LIMIT_BYTES)


def _dot(a, b):
    return jnp.dot(a, b, preferred_element_type=F32)


def _dot_nt(a, b):
    return lax.dot_general(a, b, (((1,), (1,)), ((), ())), preferred_element_type=F32)


def _silu(x):
    return x * jax.nn.sigmoid(x)


def _norm_mod(x, g, sh, sc):
    r = lax.rsqrt(jnp.mean(x * x, axis=-1, keepdims=True) + EPS)
    return (x * r * g) * (1.0 + sc) + sh


def _mod_kernel(c_ref, w_ref, b_ref, o_ref):
    c = c_ref[...]
    o_ref[...] = jnp.dot(_silu(c), w_ref[...], precision=HIGHEST, preferred_element_type=F32) + b_ref[...]


def _modulation(conds, w_mod, b_mod):
    L, D, D6 = w_mod.shape
    R = conds.shape[0]
    TN = 1024
    return pl.pallas_call(
        _mod_kernel,
        grid=(L, D6 // TN),
        in_specs=[pl.BlockSpec((R, D), lambda l, j: (0, 0)),
                  pl.BlockSpec((None, D, TN), lambda l, j: (l, 0, j)),
                  pl.BlockSpec((None, 1, TN), lambda l, j: (l, 0, j))],
        out_specs=pl.BlockSpec((None, R, TN), lambda l, j: (l, 0, j)),
        out_shape=jax.ShapeDtypeStruct((L, R, D6), F32),
        compiler_params=_params("arbitrary", "arbitrary"),
        name="modulation",
    )(conds, w_mod, b_mod.reshape(L, 1, D6))


def _mod_spec(part, D, TM, rows_per_group, axis):
    def idx(*g):
        return ((g[axis] * TM) // rows_per_group, part, 0, 0)
    return pl.BlockSpec((None, None, 1, D), idx)


MIX_MAIN = A_H * (2 * A_DK + 2 * A_DV)
MIX_TN = 768


def _norm_mm_kernel(x_ref, g_ref, sh_ref, sc_ref, w_ref, we_ref, o_ref, oe_ref, h_scr, *, nmain):
    j = pl.program_id(1)

    @pl.when(j == 0)
    def _():
        h_scr[...] = _norm_mod(x_ref[...], g_ref[...], sh_ref[...], sc_ref[...]).astype(BF16)

    @pl.when(j < nmain)
    def _():
        o_ref[...] = _dot(h_scr[...], w_ref[...].astype(BF16)).astype(o_ref.dtype)

    @pl.when(j == nmain)
    def _():
        oe_ref[...] = _dot(h_scr[...], we_ref[...].astype(BF16))


def _norm_mm(x, g, mod, parts, w, w_extra, extra_block, TM, rows_per_group):
    N, D = x.shape
    nmain = MIX_MAIN // MIX_TN
    WE = extra_block[0]
    return pl.pallas_call(
        functools.partial(_norm_mm_kernel, nmain=nmain),
        grid=(N // TM, nmain + 1),
        in_specs=[pl.BlockSpec((TM, D), lambda i, j: (i, 0)),
                  pl.BlockSpec((1, D), lambda i, j: (0, 0)),
                  _mod_spec(parts[0], D, TM, rows_per_group, 0),
                  _mod_spec(parts[1], D, TM, rows_per_group, 0),
                  pl.BlockSpec((D, MIX_TN), lambda i, j: (0, jnp.minimum(j, nmain - 1))),
                  pl.BlockSpec((D, WE), lambda i, j: (0, extra_block[1]))],
        out_specs=[pl.BlockSpec((TM, MIX_TN), lambda i, j: (i, jnp.minimum(j, nmain - 1))),
                   pl.BlockSpec((TM, WE), lambda i, j: (i, 0))],
        out_shape=[jax.ShapeDtypeStruct((N, MIX_MAIN), BF16), jax.ShapeDtypeStruct((N, WE), F32)],
        scratch_shapes=[pltpu.VMEM((TM, D), BF16)],
        compiler_params=_params("arbitrary", "arbitrary"),
        name="norm_mm",
    )(x, g.reshape(1, D), mod, mod, w, w_extra)


def _ret_kernel(lg_ref, q_ref, k_ref, v_ref, ag_ref, s0f_ref, s0b_ref, ng_ref,
                o_ref, sf_ref, sb_ref, s_scr, of_scr, *, n, C):
    d = pl.program_id(1)
    c = pl.program_id(2)

    @pl.when(jnp.logical_and(c == 0, d == 0))
    def _():
        s_scr[...] = s0f_ref[...]

    @pl.when(jnp.logical_and(c == 0, d == 1))
    def _():
        s_scr[...] = s0b_ref[...]

    df = d.astype(F32)
    sgn = 1.0 - 2.0 * df
    ii = lax.broadcasted_iota(jnp.int32, (C, C), 0).astype(F32)
    jj = lax.broadcasted_iota(jnp.int32, (C, C), 1).astype(F32)
    dd = (ii - jj) * sgn
    feeds = dd >= 0.0
    ddc = jnp.maximum(dd, 0.0)
    ri = lax.broadcasted_iota(jnp.int32, (C, 1), 0).astype(F32)
    pos_q = (ri + 1.0) + df * (C - 2.0 * ri - 1.0)
    pos_k = (C - 1.0 - ri) + df * (2.0 * ri - C + 1.0)
    chunk_len = jnp.full((1, A_DV), float(C), F32)

    outs = []
    for h in range(A_H):
        lg = lg_ref[d, h]
        dmask = jnp.where(feeds, jnp.exp2(lg * ddc), 0.0)
        qh = q_ref[:, h * A_DK:(h + 1) * A_DK].astype(F32) * (A_DK ** -0.5)
        kh = k_ref[:, h * A_DK:(h + 1) * A_DK].astype(F32)
        vh = v_ref[:, h * A_DV:(h + 1) * A_DV].astype(BF16)
        s = s_scr[h]
        att = _dot_nt(qh.astype(BF16), kh.astype(BF16)) * dmask
        o = _dot(att.astype(BF16), vh) + _dot((qh * jnp.exp2(lg * pos_q)).astype(BF16), s.astype(BF16))
        kd = kh * jnp.exp2(lg * pos_k)
        s_scr[h] = jnp.exp2(lg * chunk_len) * s + _dot(kd.T.astype(BF16), vh)
        outs.append(o)
    o_all = jnp.concatenate(outs, axis=-1)

    @pl.when(d == 0)
    def _():
        of_scr[c] = o_all

    @pl.when(d == 1)
    def _():
        tot = o_all + of_scr[n - 1 - c]
        res = []
        for h in range(A_H):
            sl = slice(h * A_DV, (h + 1) * A_DV)
            t = tot[:, sl]
            dev = t - jnp.mean(t, axis=-1, keepdims=True)
            y = dev * lax.rsqrt(jnp.mean(dev * dev, axis=-1, keepdims=True) + EPS) * ng_ref[:, sl]
            res.append(y * _silu(ag_ref[:, sl].astype(F32)))
        o_ref[...] = jnp.concatenate(res, axis=-1).astype(o_ref.dtype)

    @pl.when(jnp.logical_and(c == n - 1, d == 0))
    def _():
        sf_ref[...] = s_scr[...]

    @pl.when(jnp.logical_and(c == n - 1, d == 1))
    def _():
        sb_ref[...] = s_scr[...]


def _retention(z, B, T, log_gamma, s0f, s0b, norm_g):
    C = RET_CHUNK
    n = T // C
    HV = A_H * A_DV

    def row(b, d, c):
        return b * n + c + d * (n - 1 - 2 * c)

    st_spec = pl.BlockSpec((None, A_H, A_DK, A_DV), lambda b, d, c: (b, 0, 0, 0))
    st_shape = jax.ShapeDtypeStruct((B, A_H, A_DK, A_DV), F32)
    return pl.pallas_call(
        functools.partial(_ret_kernel, n=n, C=C),
        grid=(B, 2, n),
        in_specs=[pl.BlockSpec(memory_space=pltpu.SMEM),
                  pl.BlockSpec((C, 512), lambda b, d, c: (row(b, d, c), 0)),
                  pl.BlockSpec((C, 512), lambda b, d, c: (row(b, d, c), 1)),
                  pl.BlockSpec((C, HV), lambda b, d, c: (row(b, d, c), 1)),
                  pl.BlockSpec((C, HV), lambda b, d, c: (row(b, d, c), 2)),
                  st_spec, st_spec,
                  pl.BlockSpec((1, HV), lambda b, d, c: (0, 0))],
        out_specs=[pl.BlockSpec((C, HV), lambda b, d, c: (b * n + (n - 1) - d * c, 0)),
                   st_spec, st_spec],
        out_shape=[jax.ShapeDtypeStruct((B * T, HV), BF16), st_shape, st_shape],
        scratch_shapes=[pltpu.VMEM((A_H, A_DK, A_DV), F32), pltpu.VMEM((n, C, HV), F32)],
        compiler_params=_params("arbitrary", "arbitrary", "arbitrary"),
        name="retention",
    )(log_gamma * LOG2E, z, z, z, z, s0f, s0b, norm_g.reshape(1, HV))


def _group_sum_matrix(width, group):
    i = np.arange(width)
    return jnp.asarray((i[:, None] // group == i[None, :] // group).astype(np.float32), dtype=BF16)


def _q_pad_matrix():
    m = np.zeros((B_H * B_HD, B_H * LANES), np.float32)
    g = B_H // B_HKV
    for h in range(B_H):
        for t in range(B_HD):
            m[h * B_HD + t, h * LANES + (h // g) * B_HD + t] = 1.0
    return jnp.asarray(m, dtype=BF16)


def _rope_tables(T):
    rows = T // GRID_W
    row = np.repeat(np.arange(rows, dtype=np.float64), GRID_W)
    col = np.tile(np.arange(GRID_W, dtype=np.float64), rows)
    nq = B_HD // 4
    inv = ROPE_THETA ** (-np.arange(nq, dtype=np.float64) / nq)
    ang = np.concatenate([row[:, None] * inv, col[:, None] * inv], axis=-1)
    cos = np.repeat(np.cos(ang), 2, axis=-1)
    sin = np.repeat(np.sin(ang), 2, axis=-1)
    sign = np.tile(np.array([-1.0, 1.0]), B_HD // 2)
    reps = LANES // B_HD
    return (jnp.asarray(np.tile(cos, (1, reps)), dtype=F32),
            jnp.asarray(np.tile(sin * sign, (1, reps)), dtype=F32))


def _group_rmsnorm(x, gsum, g):
    x2 = x * x
    hi = x2.astype(BF16)
    lo = (x2 - hi.astype(F32)).astype(BF16)
    ss = _dot(hi, gsum) + _dot(lo, gsum)
    return x * lax.rsqrt(ss * (1.0 / B_HD) + EPS) * g


def _rotate_pairs(x, cos, sin_signed):
    n = x.shape[1]
    lane = lax.broadcasted_iota(jnp.int32, x.shape, 1)
    partner = jnp.where(lane % 2 == 0, pltpu.roll(x, n - 1, 1), pltpu.roll(x, 1, 1))
    reps = n // LANES
    if reps > 1:
        cos = jnp.concatenate([cos] * reps, axis=1)
        sin_signed = jnp.concatenate([sin_signed] * reps, axis=1)
    return x * cos + partner * sin_signed


def _bprep_kernel(z_ref, qg_ref, kg_ref, cos_ref, sin_ref, gq_ref, gk_ref, pad_ref,
                  qpad_ref, kn_ref, kr_ref, vb_ref, *, rope):
    nq = B_H * B_HD
    nk = B_HKV * B_HD
    qn = _group_rmsnorm(z_ref[:, 0:nq], gq_ref[...], qg_ref[...])
    kn = _group_rmsnorm(z_ref[:, nq:nq + nk], gk_ref[...], kg_ref[...])
    kn_ref[...] = kn
    if rope:
        qn = _rotate_pairs(qn, cos_ref[...], sin_ref[...])
        kn = _rotate_pairs(kn, cos_ref[...], sin_ref[...])
    kr_ref[...] = kn.astype(BF16)
    vb_ref[...] = z_ref[:, nq + nk:nq + 2 * nk].astype(BF16)
    qs = (qn * (B_HD ** -0.5 * LOG2E)).astype(BF16)
    qpad_ref[...] = _dot(qs, pad_ref[...]).astype(BF16)


def _bprep(z, T, q_g, k_g, rope):
    N = z.shape[0]
    TM = min(512, T)
    nq = B_H * B_HD
    nk = B_HKV * B_HD
    width = nq + 2 * nk
    assert z.shape[1] == width
    cos, sin = _rope_tables(T if rope else TM)
    nt = T // TM if rope else 1
    const = lambda i: (0, 0)
    return pl.pallas_call(
        functools.partial(_bprep_kernel, rope=rope),
        grid=(N // TM,),
        in_specs=[pl.BlockSpec((TM, width), lambda i: (i, 0)),
                  pl.BlockSpec((1, nq), const),
                  pl.BlockSpec((1, nk), const),
                  pl.BlockSpec((TM, LANES), lambda i: (i % nt, 0)),
                  pl.BlockSpec((TM, LANES), lambda i: (i % nt, 0)),
                  pl.BlockSpec((nq, nq), const),
                  pl.BlockSpec((nk, nk), const),
                  pl.BlockSpec((nq, B_H * LANES), const)],
        out_specs=[pl.BlockSpec((TM, B_H * LANES), lambda i: (i, 0)),
                   pl.BlockSpec((TM, nk), lambda i: (i, 0)),
                   pl.BlockSpec((TM, nk), lambda i: (i, 0)),
                   pl.BlockSpec((TM, nk), lambda i: (i, 0))],
        out_shape=[jax.ShapeDtypeStruct((N, B_H * LANES), BF16),
                   jax.ShapeDtypeStruct((N, nk), F32),
                   jax.ShapeDtypeStruct((N, nk), BF16),
                   jax.ShapeDtypeStruct((N, nk), BF16)],
        compiler_params=_params("arbitrary"),
        name="attn_prep",
    )(z, jnp.tile(q_g, B_H).reshape(1, nq), jnp.tile(k_g, B_HKV).reshape(1, nk), cos, sin,
      _group_sum_matrix(nq, B_HD), _group_sum_matrix(nk, B_HD), _q_pad_matrix())


def _lane_fold(x, op):
    acc = x[:, 0:LANES]
    for j in range(1, x.shape[1] // LANES):
        acc = op(acc, x[:, j * LANES:(j + 1) * LANES])
    return acc


def _attn_kernel(*refs, has_cache, kc, nq):
    if has_cache:
        q_ref, k_ref, v_ref, ck_ref, cv_ref, o_ref, s_scr, m_scr, mprev_scr, l_scr, acc_scr = refs
        ncache = ck_ref.shape[0] // kc
    else:
        q_ref, k_ref, v_ref, o_ref, s_scr, m_scr, mprev_scr, l_scr, acc_scr = refs
        ncache = 0
    i = pl.program_id(1)
    tq = q_ref.shape[0]
    nlat = k_ref.shape[0] // kc
    ngrp = kc // LANES

    def score(c, kblk):
        q = jnp.concatenate([q_ref[:, h * LANES:(h + 1) * LANES] for h in range(B_H)], axis=0)
        s = _dot_nt(q, kblk)
        s_scr[c] = s
        m_scr[...] = jnp.maximum(m_scr[...], _lane_fold(s, jnp.maximum))

    def weight(c, vblk):
        s = s_scr[c]
        mp = mprev_scr[...]
        ps = [jnp.exp2(s[:, j * LANES:(j + 1) * LANES] - mp) for j in range(ngrp)]
        tot = ps[0]
        for pj in ps[1:]:
            tot = tot + pj
        l_scr[...] += tot
        acc_scr[...] += _dot(jnp.concatenate(ps, axis=1).astype(BF16), vblk)

    def run(do_weight, do_score):
        def unit(c, kblk, vblk):
            if do_weight:
                weight(c, vblk())
            if do_score:
                score(c, kblk())

        for c in range(ncache):
            unit(c, lambda: ck_ref[c * kc:(c + 1) * kc, :].astype(BF16),
                 lambda: cv_ref[c * kc:(c + 1) * kc, :].astype(BF16))

        def body(c, carry):
            rows = pl.ds(pl.multiple_of(c * kc, kc), kc)
            unit(ncache + c, lambda: k_ref[rows, :], lambda: v_ref[rows, :])
            return carry
        lax.fori_loop(0, nlat, body, 0)

    @pl.when(i < nq)
    def _():
        m_scr[...] = jnp.full(m_scr.shape, -jnp.inf, F32)

    @pl.when(i > 0)
    def _():
        l_scr[...] = jnp.zeros_like(l_scr)
        acc_scr[...] = jnp.zeros_like(acc_scr)

    @pl.when(i == 0)
    def _():
        run(False, True)

    @pl.when(jnp.logical_and(i > 0, i < nq))
    def _():
        run(True, True)

    @pl.when(i == nq)
    def _():
        run(True, False)

    @pl.when(i > 0)
    def _():
        r_all = acc_scr[...] / jnp.sum(l_scr[...], axis=-1, keepdims=True)
        g = B_H // B_HKV
        lane = lax.broadcasted_iota(jnp.int32, (tq, LANES), 1)
        outs = []
        for j in range(B_H // 2):
            pair = []
            for half in range(2):
                h = 2 * j + half
                r = r_all[h * tq:(h + 1) * tq, :]
                if h // g != half:
                    r = pltpu.roll(r, B_HD, 1)
                pair.append(r)
            outs.append(jnp.where(lane < B_HD, pair[0], pair[1]))
        o_ref[...] = jnp.concatenate(outs, axis=-1).astype(o_ref.dtype)

    @pl.when(i < nq)
    def _():
        mprev_scr[...] = jnp.broadcast_to(jnp.max(m_scr[...], axis=-1, keepdims=True), mprev_scr.shape)


def _attention(qpad, kr, vb, B, T, cache_k, cache_v):
    has_cache = cache_k is not None
    TQ = Q_TILE
    nq = T // TQ
    nk = B_HKV * B_HD
    in_specs = [pl.BlockSpec((TQ, B_H * LANES), lambda b, i: (b * nq + jnp.minimum(i, nq - 1), 0)),
                pl.BlockSpec((T, nk), lambda b, i: (b, 0)),
                pl.BlockSpec((T, nk), lambda b, i: (b, 0))]
    args = [qpad, kr, vb]
    kc = min(512, T)
    nchunks = T // kc
    if has_cache:
        P = cache_k.shape[1]
        assert P % kc == 0
        nchunks += P // kc
        in_specs += [pl.BlockSpec((None, P, nk), lambda b, i: (b, 0, 0))] * 2
        args += [cache_k, cache_v]
    R = B_H * TQ
    return pl.pallas_call(
        functools.partial(_attn_kernel, has_cache=has_cache, kc=kc, nq=nq),
        grid=(B, nq + 1),
        in_specs=in_specs,
        out_specs=pl.BlockSpec((TQ, B_H * B_HD), lambda b, i: (b * nq + jnp.maximum(i - 1, 0), 0)),
        out_shape=jax.ShapeDtypeStruct((B * T, B_H * B_HD), BF16),
        scratch_shapes=[pltpu.VMEM((nchunks, R, kc), F32)] + [pltpu.VMEM((R, LANES), F32)] * 4,
        compiler_params=_params("arbitrary", "arbitrary"),
        name="attention",
    )(*args)


def _proj_res_kernel(*refs, n_in):
    x_ref, gate_ref = refs[0], refs[1]
    o_refs = refs[2:2 + n_in]
    w_refs = refs[2 + n_in:2 + 2 * n_in]
    out_ref = refs[2 + 2 * n_in]
    wbf_refs = refs[3 + 2 * n_in:]

    @pl.when(pl.program_id(0) == 0)
    def _():
        for w_ref, wbf_ref in zip(w_refs, wbf_refs):
            wbf_ref[...] = w_ref[...].astype(BF16)

    acc = _dot(o_refs[0][...], wbf_refs[0][...])
    for o_ref, wbf_ref in zip(o_refs[1:], wbf_refs[1:]):
        acc = acc + _dot(o_ref[...], wbf_ref[...])
    out_ref[...] = x_ref[...] + gate_ref[...] * acc


def _proj_res(x, mod, part, acts, w, rows_per_group):
    N, D = x.shape
    TM = min(512, rows_per_group)
    n_in = len(acts)
    widths = [a.shape[1] for a in acts]
    offs = np.cumsum([0] + widths[:-1]).tolist()
    in_specs = [pl.BlockSpec((TM, D), lambda i: (i, 0)),
                _mod_spec(part, D, TM, rows_per_group, 0)]
    in_specs += [pl.BlockSpec((TM, wd), lambda i: (i, 0)) for wd in widths]
    in_specs += [pl.BlockSpec((wd, D), functools.partial(lambda i, blk: (blk, 0), blk=off // wd))
                 for wd, off in zip(widths, offs)]
    return pl.pallas_call(
        functools.partial(_proj_res_kernel, n_in=n_in),
        grid=(N // TM,),
        in_specs=in_specs,
        out_specs=pl.BlockSpec((TM, D), lambda i: (i, 0)),
        out_shape=jax.ShapeDtypeStruct((N, D), F32),
        scratch_shapes=[pltpu.VMEM((wd, D), BF16) for wd in widths],
        compiler_params=_params("arbitrary"),
        name="proj_residual",
    )(x, mod, *acts, *([w] * n_in))


def _ffn_kernel(x_ref, g_ref, sh_ref, sc_ref, gate_ref, w1_ref, w3_ref, w2_ref, out_ref, h_scr, acc_scr, *, nf):
    f = pl.program_id(1)

    @pl.when(f == 0)
    def _():
        h_scr[...] = _norm_mod(x_ref[...], g_ref[...], sh_ref[...], sc_ref[...]).astype(BF16)
        acc_scr[...] = jnp.zeros_like(acc_scr)

    h = h_scr[...]
    a = _dot(h, w1_ref[...].astype(BF16))
    b = _dot(h, w3_ref[...].astype(BF16))
    acc_scr[...] += _dot((_silu(a) * b).astype(BF16), w2_ref[...].astype(BF16))

    @pl.when(f == nf - 1)
    def _():
        out_ref[...] = x_ref[...] + gate_ref[...] * acc_scr[...]


def _ffn(x, g, mod, w1, w3, w2, rows_per_group):
    N, D = x.shape
    FF = w1.shape[1]
    TM, TF = min(1024, rows_per_group), 256
    nf = FF // TF
    return pl.pallas_call(
        functools.partial(_ffn_kernel, nf=nf),
        grid=(N // TM, nf),
        in_specs=[pl.BlockSpec((TM, D), lambda i, f: (i, 0)),
                  pl.BlockSpec((1, D), lambda i, f: (0, 0)),
                  _mod_spec(3, D, TM, rows_per_group, 0),
                  _mod_spec(4, D, TM, rows_per_group, 0),
                  _mod_spec(5, D, TM, rows_per_group, 0),
                  pl.BlockSpec((D, TF), lambda i, f: (0, f)),
                  pl.BlockSpec((D, TF), lambda i, f: (0, f)),
                  pl.BlockSpec((TF, D), lambda i, f: (f, 0))],
        out_specs=pl.BlockSpec((TM, D), lambda i, f: (i, 0)),
        out_shape=jax.ShapeDtypeStruct((N, D), F32),
        scratch_shapes=[pltpu.VMEM((TM, D), BF16), pltpu.VMEM((TM, D), F32)],
        compiler_params=_params("arbitrary", "arbitrary"),
        name="ffn",
    )(x, g.reshape(1, D), mod, mod, mod, w1, w3, w2)


MOE_SB = 1024
MOE_TRG = 256
MOE_TR = 1024


def _two_stream_specs(shape, ntp, ax=0):
    def idx_p(*g):
        return (jnp.minimum(g[ax], ntp - 1), 0)

    def idx_s(*g):
        return (jnp.maximum(g[ax] - ntp, 0), 0)
    return pl.BlockSpec(shape, idx_p), pl.BlockSpec(shape, idx_s)


def _pool_mod_spec(part, D, TM, ntp, rows_per_group):
    def idx(i, *_):
        return (jnp.where(i < ntp, 0, 1 + ((i - ntp) * TM) // rows_per_group), part, 0, 0)
    return pl.BlockSpec((None, None, 1, D), idx)


def _route_kernel(xp_ref, xs_ref, g_ref, sh_ref, sc_ref, rw_ref, tri_ref, h_ref, info_ref, infot_ref, cum_ref,
                  carry_scr, *, ntp):
    i = pl.program_id(0)

    @pl.when(i == 0)
    def _():
        carry_scr[...] = jnp.zeros_like(carry_scr)

    x = jnp.where(i < ntp, xp_ref[...], xs_ref[...])
    h = _norm_mod(x, g_ref[...], sh_ref[...], sc_ref[...])
    h_ref[...] = h.astype(BF16)
    lane = lax.broadcasted_iota(jnp.int32, (x.shape[0], LANES), 1).astype(F32)
    logits = jnp.dot(h, rw_ref[...], precision=HIGHEST, preferred_element_type=F32)
    logits = jnp.where(lane < N_EXPERTS, logits, -jnp.inf)
    m1 = jnp.max(logits, axis=-1, keepdims=True)
    i1 = jnp.min(jnp.where(logits == m1, lane, float(LANES)), axis=-1, keepdims=True)
    rest = jnp.where(lane == i1, -jnp.inf, logits)
    m2 = jnp.max(rest, axis=-1, keepdims=True)
    i2 = jnp.min(jnp.where(rest == m2, lane, float(LANES)), axis=-1, keepdims=True)
    e2 = jnp.exp(m2 - m1)
    w1 = 1.0 / (1.0 + e2)
    w2 = e2 / (1.0 + e2)
    ind = jnp.where(jnp.logical_or(lane == i1, lane == i2), 1.0, 0.0)
    before = _dot(tri_ref[...], ind.astype(BF16)) + carry_scr[...]
    r1 = jnp.sum(jnp.where(lane == i1, before, 0.0), axis=-1, keepdims=True)
    r2 = jnp.sum(jnp.where(lane == i2, before, 0.0), axis=-1, keepdims=True)
    total = carry_scr[...] + jnp.sum(ind, axis=0, keepdims=True)
    carry_scr[...] = total
    cum_ref[...] = total
    info = jnp.where(lane == 0.0, i1, jnp.where(lane == 1.0, i2, jnp.where(lane == 2.0, w1, jnp.where(
        lane == 3.0, w2, jnp.where(lane == 4.0, r1, jnp.where(lane == 5.0, r2, 0.0))))))
    info_ref[...] = info[:, 0:SUB]
    info_t = jnp.concatenate([info[r:r + LANES, :].T for r in range(0, info.shape[0], LANES)], axis=1)
    infot_ref[...] = info_t[0:SUB, :]


def _moe_route(xp, xs, g, mod, router_w, rows_per_group):
    Np, D = xp.shape
    N = Np + xs.shape[0]
    TM = MOE_SB
    ntp = Np // TM
    nt = N // TM
    rw = jnp.pad(router_w, ((0, 0), (0, LANES - router_w.shape[1])))
    tri = jnp.asarray(np.tril(np.ones((TM, TM), np.float32), -1), dtype=BF16)
    xp_spec, xs_spec = _two_stream_specs((TM, D), ntp)
    return pl.pallas_call(
        functools.partial(_route_kernel, ntp=ntp),
        grid=(nt,),
        in_specs=[xp_spec, xs_spec,
                  pl.BlockSpec((1, D), lambda i: (0, 0)),
                  _pool_mod_spec(3, D, TM, ntp, rows_per_group),
                  _pool_mod_spec(4, D, TM, ntp, rows_per_group),
                  pl.BlockSpec((D, LANES), lambda i: (0, 0)),
                  pl.BlockSpec((TM, TM), lambda i: (0, 0))],
        out_specs=[pl.BlockSpec((TM, D), lambda i: (i, 0)),
                   pl.BlockSpec((TM, SUB), lambda i: (i, 0)),
                   pl.BlockSpec((SUB, TM), lambda i: (0, i)),
                   pl.BlockSpec((None, 1, LANES), lambda i: (i, 0, 0))],
        out_shape=[jax.ShapeDtypeStruct((N, D), BF16),
                   jax.ShapeDtypeStruct((N, SUB), F32),
                   jax.ShapeDtypeStruct((SUB, N), F32),
                   jax.ShapeDtypeStruct((nt, 1, LANES), F32)],
        scratch_shapes=[pltpu.VMEM((1, LANES), F32)],
        compiler_params=_params("arbitrary"),
        name="moe_route",
    )(xp, xs, g.reshape(1, D), mod, mod, rw, tri)


def _moe_plan(info, info_t, cum, N):
    E, SB, TRG, TR = N_EXPERTS, MOE_SB, MOE_TRG, MOE_TR
    NB = N // SB
    rmax = 2 * N + E * TR
    RG, RT = rmax // TRG, rmax // TR
    PMAX = RG + E * NB
    i32 = jnp.int32
    cum_e = cum[:, 0, :E].astype(i32).T
    cnt = cum_e[:, -1]
    tiles = (cnt + TR - 1) // TR
    start = TR * (jnp.cumsum(tiles) - tiles)

    startf = start.astype(F32)

    def region_start(e):
        out = jnp.zeros_like(e)
        for k in range(E):
            out = jnp.where(e == float(k), startf[k], out)
        return out

    pos_cols = jnp.concatenate([region_start(info[:, 0:2]) + info[:, 4:6], info[:, 2:4],
                                jnp.zeros((N, 4), F32)], axis=1)
    pos_rows = jnp.concatenate([region_start(info_t[0:2]) + info_t[4:6], jnp.zeros((6, N), F32)],
                               axis=0)

    def region(row0):
        e = jnp.clip(jnp.sum(row0[:, None] >= start[None, :], axis=1) - 1, 0, E - 1)
        return e, row0 - start[e]

    eq, lo = region(jnp.arange(RG, dtype=i32) * TRG)
    hi = jnp.minimum(lo + TRG, cnt[eq])
    first = jnp.sum(cum_e[eq] <= lo[:, None], axis=1)
    last = jnp.sum(cum_e[eq] < hi[:, None], axis=1)
    nblk = jnp.where(hi > lo, last - first + 1, 0)
    pend = jnp.cumsum(nblk)
    npairs = pend[-1]
    p = jnp.arange(PMAX, dtype=i32)
    valid = p < npairs
    pc = jnp.minimum(p, npairs - 1)
    q_of = jnp.minimum(jnp.sum(pend[None, :] <= pc[:, None], axis=1), RG - 1).astype(i32)
    pstart = pend - nblk
    s_of = (first[q_of] + pc - pstart[q_of]).astype(i32)
    g_first = jnp.logical_and(valid, pc == pstart[q_of]).astype(i32)
    gather_plan = (q_of, s_of, valid.astype(i32), g_first)

    order = jnp.argsort(jnp.where(valid, s_of * RG + q_of, jnp.iinfo(jnp.int32).max))
    s2, q2 = s_of[order], q_of[order]
    s2 = jnp.where(valid, s2, s2[npairs - 1])
    q2 = jnp.where(valid, q2, q2[npairs - 1])
    prev = jnp.concatenate([jnp.full((1,), -1, i32), s2[:-1]])
    nxt = jnp.concatenate([s2[1:], jnp.full((1,), -1, i32)])
    c_first = jnp.logical_and(valid, s2 != prev).astype(i32)
    c_last = jnp.logical_and(valid, jnp.logical_or(s2 != nxt, p == npairs - 1)).astype(i32)
    combine_plan = (s2.astype(i32), q2.astype(i32), valid.astype(i32), c_first, c_last)

    te, tlo = region(jnp.arange(RT, dtype=i32) * TR)
    tvalid = jnp.clip(cnt[te] - tlo, 0, TR)
    last_t = jnp.sum(tiles) - 1
    t_idx = jnp.where(tvalid > 0, jnp.arange(RT, dtype=i32), last_t).astype(i32)
    ffn_plan = (t_idx, te[t_idx].astype(i32), tvalid.astype(i32))
    return pos_cols, pos_rows, gather_plan, combine_plan, ffn_plan, rmax


def _moe_gather_kernel(q_ref, s_ref, valid_ref, first_ref, pos_ref, h_ref, out_ref):
    p = pl.program_id(0)

    @pl.when(valid_ref[p] == 1)
    def _():
        rows, toks = out_ref.shape[0], h_ref.shape[0]
        row = (lax.broadcasted_iota(jnp.int32, (rows, 1), 0) + q_ref[p] * rows).astype(F32)
        hit = jnp.logical_or(pos_ref[0:1, :] == row, pos_ref[1:2, :] == row)
        sel = jnp.where(hit, 1.0, 0.0).astype(BF16)
        @pl.when(first_ref[p] == 1)
        def _():
            out_ref[...] = jnp.zeros_like(out_ref)

        out_ref[...] = out_ref[...] + _dot(sel, h_ref[...]).astype(BF16)


def _moe_gather(h, pos_rows, plan, rmax):
    N, D = h.shape
    pmax = plan[0].shape[0]
    return pl.pallas_call(
        _moe_gather_kernel,
        grid_spec=pltpu.PrefetchScalarGridSpec(
            num_scalar_prefetch=4, grid=(pmax,),
            in_specs=[pl.BlockSpec((SUB, MOE_SB), lambda p, q, s, v, f: (0, s[p])),
                      pl.BlockSpec((MOE_SB, D), lambda p, q, s, v, f: (s[p], 0))],
            out_specs=pl.BlockSpec((MOE_TRG, D), lambda p, q, s, v, f: (q[p], 0))),
        out_shape=jax.ShapeDtypeStruct((rmax, D), BF16),
        compiler_params=_params("arbitrary"),
        name="moe_gather",
    )(*plan, pos_rows, h)


def _moe_ffn_kernel(t_ref, e_ref, nv_ref, x_ref, w1_ref, w3_ref, w2_ref, out_ref, acc_scr, *, nf):
    t = pl.program_id(0)
    f = pl.program_id(1)
    nv = nv_ref[t]

    def block(rows):
        @pl.when(f == 0)
        def _():
            acc_scr[rows, :] = jnp.zeros((rows.stop - rows.start, acc_scr.shape[1]), F32)

        x = x_ref[rows, :]
        a = _dot(x, w1_ref[...].astype(BF16))
        b = _dot(x, w3_ref[...].astype(BF16))
        acc_scr[rows, :] += _dot((_silu(a) * b).astype(BF16), w2_ref[...].astype(BF16))

        @pl.when(f == nf - 1)
        def _():
            out_ref[rows, :] = acc_scr[rows, :].astype(out_ref.dtype)

    nsub = MOE_TR // MOE_TRG
    full = nv > (nsub - 1) * MOE_TRG

    @pl.when(full)
    def _():
        block(slice(0, MOE_TR))

    for sub in range(nsub - 1):
        @pl.when(jnp.logical_and(jnp.logical_not(full), sub * MOE_TRG < nv))
        def _():
            block(slice(sub * MOE_TRG, (sub + 1) * MOE_TRG))


def _moe_ffn(xs, plan, w1, w3, w2):
    rmax, D = xs.shape
    FF = w1.shape[2]
    TF = 256
    nf = FF // TF
    RT = rmax // MOE_TR

    def fidx(t, f, nv):
        return jnp.where(nv[t] > 0, f, nf - 1)

    return pl.pallas_call(
        functools.partial(_moe_ffn_kernel, nf=nf),
        grid_spec=pltpu.PrefetchScalarGridSpec(
            num_scalar_prefetch=3, grid=(RT, nf),
            in_specs=[pl.BlockSpec((MOE_TR, D), lambda t, f, ti, e, nv: (ti[t], 0)),
                      pl.BlockSpec((None, D, TF), lambda t, f, ti, e, nv: (e[t], 0, fidx(t, f, nv))),
                      pl.BlockSpec((None, D, TF), lambda t, f, ti, e, nv: (e[t], 0, fidx(t, f, nv))),
                      pl.BlockSpec((None, TF, D), lambda t, f, ti, e, nv: (e[t], fidx(t, f, nv), 0))],
            out_specs=pl.BlockSpec((MOE_TR, D), lambda t, f, ti, e, nv: (ti[t], 0)),
            scratch_shapes=[pltpu.VMEM((MOE_TR, D), F32)]),
        out_shape=jax.ShapeDtypeStruct((rmax, D), BF16),
        compiler_params=_params("arbitrary", "arbitrary"),
        name="moe_ffn",
    )(*plan, xs, w1, w3, w2)


def _moe_combine_kernel(s_ref, q_ref, valid_ref, first_ref, last_ref, pos_ref, ys_ref, xp_ref, xs_ref, gate_ref,
                        fg_ref, op_ref, os_ref, acc_scr, *, ntp):
    p = pl.program_id(0)

    @pl.when(valid_ref[p] == 1)
    def _():
        rows = ys_ref.shape[0]
        col = (lax.broadcasted_iota(jnp.int32, (1, rows), 1) + q_ref[p] * rows).astype(F32)
        sel = (jnp.where(pos_ref[:, 0:1] == col, pos_ref[:, 2:3], 0.0)
               + jnp.where(pos_ref[:, 1:2] == col, pos_ref[:, 3:4], 0.0)).astype(BF16)
        @pl.when(first_ref[p] == 1)
        def _():
            acc_scr[...] = jnp.zeros_like(acc_scr)

        acc_scr[...] += _dot(sel, ys_ref[...])

        @pl.when(last_ref[p] == 1)
        def _():
            s = s_ref[p]
            x = jnp.where(s < ntp, xp_ref[...], xs_ref[...])
            y = x + gate_ref[...] * acc_scr[...]
            out = y * lax.rsqrt(jnp.mean(y * y, axis=-1, keepdims=True) + EPS) * fg_ref[...]

            @pl.when(s < ntp)
            def _():
                op_ref[...] = out

            @pl.when(s >= ntp)
            def _():
                os_ref[...] = out


def _moe_combine(ys, pos_cols, plan, xp, xs, mod, final_g, rows_per_group):
    Np, D = xp.shape
    Ns = xs.shape[0]
    SB = MOE_SB
    ntp = Np // SB
    pmax = plan[0].shape[0]

    def blk(p, s, *_):
        return s[p]

    def tok_p(p, s, *_):
        return (jnp.minimum(s[p], ntp - 1), 0)

    def tok_s(p, s, *_):
        return (jnp.maximum(s[p] - ntp, 0), 0)

    def gate_idx(p, s, *_):
        return (jnp.where(s[p] < ntp, 0, 1 + ((s[p] - ntp) * SB) // rows_per_group), 5, 0, 0)

    return pl.pallas_call(
        functools.partial(_moe_combine_kernel, ntp=ntp),
        grid_spec=pltpu.PrefetchScalarGridSpec(
            num_scalar_prefetch=5, grid=(pmax,),
            in_specs=[pl.BlockSpec((SB, SUB), lambda p, s, q, *_: (s[p], 0)),
                      pl.BlockSpec((MOE_TRG, D), lambda p, s, q, *_: (q[p], 0)),
                      pl.BlockSpec((SB, D), tok_p),
                      pl.BlockSpec((SB, D), tok_s),
                      pl.BlockSpec((None, None, 1, D), gate_idx),
                      pl.BlockSpec((1, D), lambda p, *_: (0, 0))],
            out_specs=[pl.BlockSpec((SB, D), tok_p), pl.BlockSpec((SB, D), tok_s)],
            scratch_shapes=[pltpu.VMEM((SB, D), F32)]),
        out_shape=[jax.ShapeDtypeStruct((Np, D), F32), jax.ShapeDtypeStruct((Ns, D), F32)],
        compiler_params=_params("arbitrary"),
        name="moe_combine",
    )(*plan, pos_cols, ys, xp, xs, mod, final_g.reshape(1, D))


def _moe(xp, xs, g, mod, router_w, w1, w3, w2, final_g, rows_per_group):
    N = xp.shape[0] + xs.shape[0]
    h, info, info_t, cum = _moe_route(xp, xs, g, mod, router_w, rows_per_group)
    pos_cols, pos_rows, gather_plan, combine_plan, ffn_plan, rmax = _moe_plan(info, info_t, cum, N)
    x_sorted = _moe_gather(h, pos_rows, gather_plan, rmax)
    y_sorted = _moe_ffn(x_sorted, ffn_plan, w1, w3, w2)
    return _moe_combine(y_sorted, pos_cols, combine_plan, xp, xs, mod, final_g, rows_per_group)


def _gla_levels(C):
    lv, c = [], C // 2
    while c >= SUB:
        lv.append(c)
        c //= 2
    return lv


def _gla_tables(C):
    levels = _gla_levels(C)
    nr = 2 + 2 * len(levels)
    mat = np.zeros((2, nr * C, C), np.float32)
    code = np.zeros((2, C, C), np.int32)
    for d in range(2):
        p = np.arange(C) if d == 0 else C - 1 - np.arange(C)
        pi, pj = p[:, None], p[None, :]
        mat[d, 0:C] = pj <= pi
        mat[d, C:2 * C] = pj > pi
        code[d] = np.where((pj <= pi) & (pi // SUB == pj // SUB), 1, 0)
        for lv, c in enumerate(levels):
            blk = pi // c
            later = blk % 2 == 1
            mat[d, (2 + 2 * lv) * C:(3 + 2 * lv) * C] = later & (pj > blk * c - 1) & (pj <= pi)
            mat[d, (3 + 2 * lv) * C:(4 + 2 * lv) * C] = (~later) & (pj > pi) & (pj <= (blk + 1) * c - 1)
            pair = (pi // (2 * c) == pj // (2 * c)) & (pi // c != pj // c) & (pj <= pi)
            code[d] = np.where(pair, 2 + lv, code[d])
    ones = np.zeros((SUB * LANES, C), np.float32)
    for jj in range(SUB):
        ones[jj * LANES:(jj + 1) * LANES, jj::SUB] = 1.0
    return jnp.asarray(mat, dtype=BF16), jnp.asarray(code), jnp.asarray(ones, dtype=BF16)


def _bcast_sublane(x, jj):
    r, w = x.shape
    x3 = x.reshape(r // SUB, SUB, w)
    return jnp.broadcast_to(x3[:, jj:jj + 1, :], x3.shape).reshape(r, w)


def _t128(x):
    r, w = x.shape
    if w > LANES:
        return jnp.concatenate([x[:, i:i + LANES].T for i in range(0, w, LANES)], axis=0)
    return jnp.concatenate([x[i:i + LANES, :].T for i in range(0, r, LANES)], axis=1)


def _gla_kernel(q_ref, k_ref, v_ref, g_ref, lr_ref, wg_ref, ba_ref, mat_ref, code_ref, ones_ref,
                s0f_ref, s0b_ref, ng_ref, o_ref, sf_ref, sb_ref, st_scr, of_scr, *, n, C, G):
    d = pl.program_id(1)
    c = pl.program_id(2)
    levels = _gla_levels(C)

    @pl.when(jnp.logical_and(c == 0, d == 0))
    def _():
        for bb in range(G):
            for h in range(C_H):
                st_scr[bb, h] = _t128(s0f_ref[bb, h])

    @pl.when(jnp.logical_and(c == 0, d == 1))
    def _():
        for bb in range(G):
            for h in range(C_H):
                st_scr[bb, h] = _t128(s0b_ref[bb, h])

    mat = mat_ref[...]
    code = code_ref[...]
    ones = ones_ref[...]
    o_all = []
    for bb in range(G):
        xg = jnp.dot(lr_ref[bb], wg_ref[...], precision=HIGHEST, preferred_element_type=F32) + ba_ref[...]
        la = (jnp.minimum(xg, 0.0) - jnp.log1p(jnp.exp(-jnp.abs(xg)))) * (LOG2E / C_TAU)
        hi = la.astype(BF16)
        lo = (la - hi.astype(F32)).astype(BF16)
        cum = _dot(mat, hi) + _dot(mat, lo)

        outs = []
        for h in range(C_H):
            ks = slice(h * C_DK, (h + 1) * C_DK)
            qh = q_ref[bb, :, ks].astype(F32) * (C_DK ** -0.5)
            kh = k_ref[bb, :, ks].astype(F32)
            vh = v_ref[bb, :, h * C_DV:(h + 1) * C_DV].astype(F32)
            vb = vh.astype(BF16)
            b = cum[0:C, ks]
            b_rest = cum[C:2 * C, ks]
            ps = []
            for jj in range(SUB):
                dec = jnp.exp2(jnp.minimum(b - _bcast_sublane(b, jj), 0.0))
                ps.append((qh * _bcast_sublane(kh, jj) * dec).astype(BF16))
            att = jnp.where(code == 1, _dot(jnp.concatenate(ps, axis=1), ones), 0.0)
            for lv in range(len(levels)):
                eq = cum[(2 + 2 * lv) * C:(3 + 2 * lv) * C, ks]
                ek = cum[(3 + 2 * lv) * C:(4 + 2 * lv) * C, ks]
                a_lv = _dot_nt((qh * jnp.exp2(eq)).astype(BF16), (kh * jnp.exp2(ek)).astype(BF16))
                att = jnp.where(code == 2 + lv, a_lv, att)
            st = st_scr[bb, h]
            o = _dot(att.astype(BF16), vb) + _dot_nt((qh * jnp.exp2(b)).astype(BF16), st.astype(BF16))
            b_end = b[0:1, :] + b_rest[0:1, :]
            ke = (kh * jnp.exp2(b_rest)).astype(BF16)
            st_scr[bb, h] = jnp.exp2(b_end) * st + _dot(_t128(vh).astype(BF16), ke)
            outs.append(o)
        o_all.append(jnp.concatenate(outs, axis=-1))

    @pl.when(d == 0)
    def _():
        for bb in range(G):
            of_scr[bb, c] = o_all[bb]

    @pl.when(d == 1)
    def _():
        for bb in range(G):
            tot = o_all[bb] + of_scr[bb, n - 1 - c]
            res = []
            for h in range(C_H):
                sl = slice(h * C_DV, (h + 1) * C_DV)
                t = tot[:, sl]
                y = t * lax.rsqrt(jnp.mean(t * t, axis=-1, keepdims=True) + EPS) * ng_ref[:, sl]
                res.append(y * _silu(g_ref[bb, :, sl].astype(F32)))
            o_ref[bb] = jnp.concatenate(res, axis=-1).astype(o_ref.dtype)

    @pl.when(jnp.logical_and(c == n - 1, d == 0))
    def _():
        for bb in range(G):
            for h in range(C_H):
                sf_ref[bb, h] = _t128(st_scr[bb, h])

    @pl.when(jnp.logical_and(c == n - 1, d == 1))
    def _():
        for bb in range(G):
            for h in range(C_H):
                sb_ref[bb, h] = _t128(st_scr[bb, h])


def _gla(z, zg, B, T, w_a2, b_a, s0f, s0b, norm_g):
    C = GLA_CHUNK
    G = 2
    n = T // C
    HK = C_H * C_DK
    HV = C_H * C_DV
    mat, code, ones = _gla_tables(C)
    nr = mat.shape[1] // C
    wg = jnp.zeros((2, LANES, HK), F32)
    for dr in range(2):
        wg = wg.at[dr, dr * C_RANK:(dr + 1) * C_RANK, :].set(w_a2[dr])
    z3 = z.reshape(B, T, z.shape[1])
    zg3 = zg.reshape(B, T, zg.shape[1])

    def chunk(d, c):
        return c + d * (n - 1 - 2 * c)

    st_spec = pl.BlockSpec((G, C_H, C_DK, C_DV), lambda b, d, c: (b, 0, 0, 0))
    st_shape = jax.ShapeDtypeStruct((B, C_H, C_DK, C_DV), F32)
    o, sf, sb = pl.pallas_call(
        functools.partial(_gla_kernel, n=n, C=C, G=G),
        grid=(B // G, 2, n),
        in_specs=[pl.BlockSpec((G, C, HK), lambda b, d, c: (b, chunk(d, c), 0)),
                  pl.BlockSpec((G, C, HK), lambda b, d, c: (b, chunk(d, c), 1)),
                  pl.BlockSpec((G, C, HV), lambda b, d, c: (b, chunk(d, c), 1)),
                  pl.BlockSpec((G, C, HV), lambda b, d, c: (b, chunk(d, c), 2)),
                  pl.BlockSpec((G, C, LANES), lambda b, d, c: (b, chunk(d, c), 0)),
                  pl.BlockSpec((None, LANES, HK), lambda b, d, c: (d, 0, 0)),
                  pl.BlockSpec((None, 1, HK), lambda b, d, c: (d, 0, 0)),
                  pl.BlockSpec((None, nr * C, C), lambda b, d, c: (d, 0, 0)),
                  pl.BlockSpec((None, C, C), lambda b, d, c: (d, 0, 0)),
                  pl.BlockSpec((SUB * LANES, C), lambda b, d, c: (0, 0)),
                  st_spec, st_spec,
                  pl.BlockSpec((1, HV), lambda b, d, c: (0, 0))],
        out_specs=[pl.BlockSpec((G, C, HV), lambda b, d, c: (b, (n - 1) - d * c, 0)),
                   st_spec, st_spec],
        out_shape=[jax.ShapeDtypeStruct((B, T, HV), BF16), st_shape, st_shape],
        scratch_shapes=[pltpu.VMEM((G, C_H, C_DV, C_DK), F32), pltpu.VMEM((G, n, C, HV), F32)],
        compiler_params=_params("arbitrary", "arbitrary", "arbitrary"),
        name="gla",
    )(z3, z3, z3, z3, zg3, wg, b_a.reshape(2, 1, HK), mat, code, ones, s0f, s0b, norm_g.reshape(1, HV))
    return o.reshape(B * T, HV), sf, sb


def _run_stream(x, B, T, mods, ctx, p):
    N, D = x.shape
    rpg = N // mods[0].shape[0]
    TM = min(2048, rpg)
    nb = (B_H + 2 * B_HKV) * B_HD

    w_in = p['even_w_in'][0]
    z, zb = _norm_mm(x, p['norm1_g'][0], mods[0], (0, 1), w_in, w_in, (nb, MIX_MAIN // nb), TM, rpg)
    if ctx is None:
        s0 = jnp.zeros((B, A_H, A_DK, A_DV), F32)
        a_f0, a_b0, cache_k, cache_v = s0, s0, None, None
    else:
        cache_k, cache_v, a_f0, a_b0 = ctx[0], ctx[1], ctx[2], ctx[3]
    o_a, a_sf, a_sb = _retention(z, B, T, p['a_log_gamma'][0], a_f0, a_b0, p['a_norm_g'][0])
    qpad, k_norm, k_rot, v_bf = _bprep(zb, T, p['b_q_g'][0], p['b_k_g'][0], rope=ctx is not None)
    o_b = _attention(qpad, k_rot, v_bf, B, T, cache_k, cache_v)
    x = _proj_res(x, mods[0], 2, [o_a, o_b], p['even_w_out'][0], rpg)
    x = _ffn(x, p['norm2_g'][0], mods[0], p['ff_w1'][0], p['ff_w3'][0], p['ff_w2'][0], rpg)

    w_in = p['odd_w_in'][0]
    w_gate = jnp.pad(w_in[:, MIX_MAIN:], ((0, 0), (0, LANES - 2 * C_RANK)))
    z1, z1g = _norm_mm(x, p['norm1_g'][1], mods[1], (0, 1), w_in, w_gate, (LANES, 0), TM, rpg)
    if ctx is None:
        s0 = jnp.zeros((B, C_H, C_DK, C_DV), F32)
        c_f0, c_b0 = s0, s0
    else:
        c_f0, c_b0 = ctx[4], ctx[5]
    o_c, c_sf, c_sb = _gla(z1, z1g, B, T, p['c_w_a2'][0], p['c_b_a'][0], c_f0, c_b0, p['c_norm_g'][0])
    x = _proj_res(x, mods[1], 2, [o_c], p['odd_w_out'][0], rpg)
    v_raw = zb[:, (B_H + B_HKV) * B_HD:]
    return x, (k_norm, v_raw, a_sf, a_sb, c_sf, c_sb)


def kernel(x_prompt, x_sample, c, cache_b_k, cache_b_v, state_a_fwd, state_a_bwd, state_c_fwd, state_c_bwd,
           c_ctx, w_mod, b_mod, norm1_g, norm2_g, final_g, even_w_in, even_w_out, a_log_gamma, a_norm_g,
           b_q_g, b_k_g, odd_w_in, c_w_a2, c_b_a, c_norm_g, odd_w_out, ff_w1, ff_w3, ff_w2,
           router_w, moe_w1, moe_w3, moe_w2):
    Bp, Tp, D = x_prompt.shape
    Bs, Ts, _ = x_sample.shape
    L = w_mod.shape[0]
    assert L == 2 and even_w_in.shape[0] == 1 and odd_w_in.shape[0] == 1
    p = dict(norm1_g=norm1_g, norm2_g=norm2_g, final_g=final_g, even_w_in=even_w_in, even_w_out=even_w_out,
             a_log_gamma=a_log_gamma, a_norm_g=a_norm_g, b_q_g=b_q_g, b_k_g=b_k_g, odd_w_in=odd_w_in,
             c_w_a2=c_w_a2, c_b_a=c_b_a, c_norm_g=c_norm_g, odd_w_out=odd_w_out, ff_w1=ff_w1, ff_w3=ff_w3,
             ff_w2=ff_w2, router_w=router_w, moe_w1=moe_w1, moe_w3=moe_w3, moe_w2=moe_w2)

    rows = 8
    conds = jnp.concatenate([c_ctx[None, :], c, jnp.zeros((rows - 1 - Bs, D), F32)], axis=0)
    mod = _modulation(conds, w_mod, b_mod).reshape(L, rows, 6, 1, D)
    mods_p = [mod[l, 0:1] for l in range(L)]
    mods_s = [mod[l, 1:1 + Bs] for l in range(L)]

    x_p, kept = _run_stream(x_prompt.reshape(Bp * Tp, D), Bp, Tp, mods_p, None, p)
    nk = B_HKV * B_HD
    ctx = (cache_b_k[:, 0].reshape(Bs, -1, nk), cache_b_v[:, 0].reshape(Bs, -1, nk),
           state_a_fwd[:, 0], state_a_bwd[:, 0], state_c_fwd[:, 0], state_c_bwd[:, 0])
    x_s, _ = _run_stream(x_sample.reshape(Bs * Ts, D), Bs, Ts, mods_s, ctx, p)
    y_p, y_s = _moe(x_p, x_s, norm2_g[1], mod[1, 0:1 + Bs], router_w[0], moe_w1[0], moe_w3[0], moe_w2[0],
                    final_g, Ts)

    k_norm, v_raw, a_sf, a_sb, c_sf, c_sb = kept
    return (y_p.reshape(Bp, Tp, D), y_s.reshape(Bs, Ts, D),
            k_norm.reshape(Bp, 1, Tp, B_HKV, B_HD), v_raw.reshape(Bp, 1, Tp, B_HKV, B_HD),
            a_sf[:, None], a_sb[:, None], c_sf[:, None], c_sb[:, None])
```

```python
import functools

import numpy as np
import jax
import jax.numpy as jnp
from jax import lax
from jax.experimental import pallas as pl
from jax.experimental.pallas import tpu as pltpu

F32 = jnp.float32
BF16 = jnp.bfloat16
EPS = 1e-6
HIGHEST = lax.Precision.HIGHEST
LOG2E = 1.4426950408889634

VMEM_LIMIT_BYTES = 56 * 1024 * 1024

A_H, A_DK, A_DV = 4, 128, 256
B_H, B_HKV, B_HD = 8, 2, 64
C_H, C_DK, C_DV, C_RANK = 4, 128, 256, 16
C_TAU = 16.0
GRID_W = 64
ROPE_THETA = 10000.0
N_EXPERTS = 8
LANES = 128
SUB = 8
RET_CHUNK = 128
GLA_CHUNK = 128
Q_TILE = 128


def _params(*sem):
    return pltpu.CompilerParams(dimension_semantics=sem, vmem_limit_bytes=VMEM_LIMIT_BYTES)


def _dot(a, b):
    return jnp.dot(a, b, preferred_element_type=F32)


def _dot_nt(a, b):
    return lax.dot_general(a, b, (((1,), (1,)), ((), ())), preferred_element_type=F32)


def _silu(x):
    return x * jax.nn.sigmoid(x)


def _norm_mod(x, g, sh, sc):
    r = lax.rsqrt(jnp.mean(x * x, axis=-1, keepdims=True) + EPS)
    return (x * r * g) * (1.0 + sc) + sh


def _mod_kernel(c_ref, w_ref, b_ref, o_ref):
    c = c_ref[...]
    o_ref[...] = jnp.dot(_silu(c), w_ref[...], precision=HIGHEST, preferred_element_type=F32) + b_ref[...]


def _modulation(conds, w_mod, b_mod):
    L, D, D6 = w_mod.shape
    R = conds.shape[0]
    TN = 1024
    return pl.pallas_call(
        _mod_kernel,
        grid=(L, D6 // TN),
        in_specs=[pl.BlockSpec((R, D), lambda l, j: (0, 0)),
                  pl.BlockSpec((None, D, TN), lambda l, j: (l, 0, j)),
                  pl.BlockSpec((None, 1, TN), lambda l, j: (l, 0, j))],
        out_specs=pl.BlockSpec((None, R, TN), lambda l, j: (l, 0, j)),
        out_shape=jax.ShapeDtypeStruct((L, R, D6), F32),
        compiler_params=_params("arbitrary", "arbitrary"),
        name="modulation",
    )(conds, w_mod, b_mod.reshape(L, 1, D6))


def _mod_spec(part, D, TM, rows_per_group, axis):
    def idx(*g):
        return ((g[axis] * TM) // rows_per_group, part, 0, 0)
    return pl.BlockSpec((None, None, 1, D), idx)


MIX_MAIN = A_H * (2 * A_DK + 2 * A_DV)
MIX_TN = 768


def _norm_mm_kernel(x_ref, g_ref, sh_ref, sc_ref, w_ref, we_ref, o_ref, oe_ref, h_scr, *, nmain):
    j = pl.program_id(1)

    @pl.when(j == 0)
    def _():
        h_scr[...] = _norm_mod(x_ref[...], g_ref[...], sh_ref[...], sc_ref[...]).astype(BF16)

    @pl.when(j < nmain)
    def _():
        o_ref[...] = _dot(h_scr[...], w_ref[...].astype(BF16)).astype(o_ref.dtype)

    @pl.when(j == nmain)
    def _():
        oe_ref[...] = _dot(h_scr[...], we_ref[...].astype(BF16))


def _norm_mm(x, g, mod, parts, w, w_extra, extra_block, TM, rows_per_group):
    N, D = x.shape
    nmain = MIX_MAIN // MIX_TN
    WE = extra_block[0]
    return pl.pallas_call(
        functools.partial(_norm_mm_kernel, nmain=nmain),
        grid=(N // TM, nmain + 1),
        in_specs=[pl.BlockSpec((TM, D), lambda i, j: (i, 0)),
                  pl.BlockSpec((1, D), lambda i, j: (0, 0)),
                  _mod_spec(parts[0], D, TM, rows_per_group, 0),
                  _mod_spec(parts[1], D, TM, rows_per_group, 0),
                  pl.BlockSpec((D, MIX_TN), lambda i, j: (0, jnp.minimum(j, nmain - 1))),
                  pl.BlockSpec((D, WE), lambda i, j: (0, extra_block[1]))],
        out_specs=[pl.BlockSpec((TM, MIX_TN), lambda i, j: (i, jnp.minimum(j, nmain - 1))),
                   pl.BlockSpec((TM, WE), lambda i, j: (i, 0))],
        out_shape=[jax.ShapeDtypeStruct((N, MIX_MAIN), BF16), jax.ShapeDtypeStruct((N, WE), F32)],
        scratch_shapes=[pltpu.VMEM((TM, D), BF16)],
        compiler_params=_params("arbitrary", "arbitrary"),
        name="norm_mm",
    )(x, g.reshape(1, D), mod, mod, w, w_extra)


def _ret_kernel(lg_ref, q_ref, k_ref, v_ref, ag_ref, s0f_ref, s0b_ref, ng_ref,
                o_ref, sf_ref, sb_ref, s_scr, of_scr, *, n, C):
    d = pl.program_id(1)
    c = pl.program_id(2)

    @pl.when(jnp.logical_and(c == 0, d == 0))
    def _():
        s_scr[...] = s0f_ref[...]

    @pl.when(jnp.logical_and(c == 0, d == 1))
    def _():
        s_scr[...] = s0b_ref[...]

    df = d.astype(F32)
    sgn = 1.0 - 2.0 * df
    ii = lax.broadcasted_iota(jnp.int32, (C, C), 0).astype(F32)
    jj = lax.broadcasted_iota(jnp.int32, (C, C), 1).astype(F32)
    dd = (ii - jj) * sgn
    feeds = dd >= 0.0
    ddc = jnp.maximum(dd, 0.0)
    ri = lax.broadcasted_iota(jnp.int32, (C, 1), 0).astype(F32)
    pos_q = (ri + 1.0) + df * (C - 2.0 * ri - 1.0)
    pos_k = (C - 1.0 - ri) + df * (2.0 * ri - C + 1.0)
    chunk_len = jnp.full((1, A_DV), float(C), F32)

    outs = []
    for h in range(A_H):
        lg = lg_ref[d, h]
        dmask = jnp.where(feeds, jnp.exp2(lg * ddc), 0.0)
        qh = q_ref[:, h * A_DK:(h + 1) * A_DK].astype(F32) * (A_DK ** -0.5)
        kh = k_ref[:, h * A_DK:(h + 1) * A_DK].astype(F32)
        vh = v_ref[:, h * A_DV:(h + 1) * A_DV].astype(BF16)
        s = s_scr[h]
        att = _dot_nt(qh.astype(BF16), kh.astype(BF16)) * dmask
        o = _dot(att.astype(BF16), vh) + _dot((qh * jnp.exp2(lg * pos_q)).astype(BF16), s.astype(BF16))
        kd = kh * jnp.exp2(lg * pos_k)
        s_scr[h] = jnp.exp2(lg * chunk_len) * s + _dot(kd.T.astype(BF16), vh)
        outs.append(o)
    o_all = jnp.concatenate(outs, axis=-1)

    @pl.when(d == 0)
    def _():
        of_scr[c] = o_all

    @pl.when(d == 1)
    def _():
        tot = o_all + of_scr[n - 1 - c]
        res = []
        for h in range(A_H):
            sl = slice(h * A_DV, (h + 1) * A_DV)
            t = tot[:, sl]
            dev = t - jnp.mean(t, axis=-1, keepdims=True)
            y = dev * lax.rsqrt(jnp.mean(dev * dev, axis=-1, keepdims=True) + EPS) * ng_ref[:, sl]
            res.append(y * _silu(ag_ref[:, sl].astype(F32)))
        o_ref[...] = jnp.concatenate(res, axis=-1).astype(o_ref.dtype)

    @pl.when(jnp.logical_and(c == n - 1, d == 0))
    def _():
        sf_ref[...] = s_scr[...]

    @pl.when(jnp.logical_and(c == n - 1, d == 1))
    def _():
        sb_ref[...] = s_scr[...]


def _retention(z, B, T, log_gamma, s0f, s0b, norm_g):
    C = RET_CHUNK
    n = T // C
    HV = A_H * A_DV

    def row(b, d, c):
        return b * n + c + d * (n - 1 - 2 * c)

    st_spec = pl.BlockSpec((None, A_H, A_DK, A_DV), lambda b, d, c: (b, 0, 0, 0))
    st_shape = jax.ShapeDtypeStruct((B, A_H, A_DK, A_DV), F32)
    return pl.pallas_call(
        functools.partial(_ret_kernel, n=n, C=C),
        grid=(B, 2, n),
        in_specs=[pl.BlockSpec(memory_space=pltpu.SMEM),
                  pl.BlockSpec((C, 512), lambda b, d, c: (row(b, d, c), 0)),
                  pl.BlockSpec((C, 512), lambda b, d, c: (row(b, d, c), 1)),
                  pl.BlockSpec((C, HV), lambda b, d, c: (row(b, d, c), 1)),
                  pl.BlockSpec((C, HV), lambda b, d, c: (row(b, d, c), 2)),
                  st_spec, st_spec,
                  pl.BlockSpec((1, HV), lambda b, d, c: (0, 0))],
        out_specs=[pl.BlockSpec((C, HV), lambda b, d, c: (b * n + (n - 1) - d * c, 0)),
                   st_spec, st_spec],
        out_shape=[jax.ShapeDtypeStruct((B * T, HV), BF16), st_shape, st_shape],
        scratch_shapes=[pltpu.VMEM((A_H, A_DK, A_DV), F32), pltpu.VMEM((n, C, HV), F32)],
        compiler_params=_params("arbitrary", "arbitrary", "arbitrary"),
        name="retention",
    )(log_gamma * LOG2E, z, z, z, z, s0f, s0b, norm_g.reshape(1, HV))


def _group_sum_matrix(width, group):
    i = np.arange(width)
    return jnp.asarray((i[:, None] // group == i[None, :] // group).astype(np.float32), dtype=BF16)


def _q_pad_matrix():
    m = np.zeros((B_H * B_HD, B_H * LANES), np.float32)
    g = B_H // B_HKV
    for h in range(B_H):
        for t in range(B_HD):
            m[h * B_HD + t, h * LANES + (h // g) * B_HD + t] = 1.0
    return jnp.asarray(m, dtype=BF16)


def _rope_tables(T):
    rows = T // GRID_W
    row = np.repeat(np.arange(rows, dtype=np.float64), GRID_W)
    col = np.tile(np.arange(GRID_W, dtype=np.float64), rows)
    nq = B_HD // 4
    inv = ROPE_THETA ** (-np.arange(nq, dtype=np.float64) / nq)
    ang = np.concatenate([row[:, None] * inv, col[:, None] * inv], axis=-1)
    cos = np.repeat(np.cos(ang), 2, axis=-1)
    sin = np.repeat(np.sin(ang), 2, axis=-1)
    sign = np.tile(np.array([-1.0, 1.0]), B_HD // 2)
    reps = LANES // B_HD
    return (jnp.asarray(np.tile(cos, (1, reps)), dtype=F32),
            jnp.asarray(np.tile(sin * sign, (1, reps)), dtype=F32))


def _group_rmsnorm(x, gsum, g):
    x2 = x * x
    hi = x2.astype(BF16)
    lo = (x2 - hi.astype(F32)).astype(BF16)
    ss = _dot(hi, gsum) + _dot(lo, gsum)
    return x * lax.rsqrt(ss * (1.0 / B_HD) + EPS) * g


def _rotate_pairs(x, cos, sin_signed):
    n = x.shape[1]
    lane = lax.broadcasted_iota(jnp.int32, x.shape, 1)
    partner = jnp.where(lane % 2 == 0, pltpu.roll(x, n - 1, 1), pltpu.roll(x, 1, 1))
    reps = n // LANES
    if reps > 1:
        cos = jnp.concatenate([cos] * reps, axis=1)
        sin_signed = jnp.concatenate([sin_signed] * reps, axis=1)
    return x * cos + partner * sin_signed


def _bprep_kernel(z_ref, qg_ref, kg_ref, cos_ref, sin_ref, gq_ref, gk_ref, pad_ref,
                  qpad_ref, kn_ref, kr_ref, vb_ref, *, rope):
    nq = B_H * B_HD
    nk = B_HKV * B_HD
    qn = _group_rmsnorm(z_ref[:, 0:nq], gq_ref[...], qg_ref[...])
    kn = _group_rmsnorm(z_ref[:, nq:nq + nk], gk_ref[...], kg_ref[...])
    kn_ref[...] = kn
    if rope:
        qn = _rotate_pairs(qn, cos_ref[...], sin_ref[...])
        kn = _rotate_pairs(kn, cos_ref[...], sin_ref[...])
    kr_ref[...] = kn.astype(BF16)
    vb_ref[...] = z_ref[:, nq + nk:nq + 2 * nk].astype(BF16)
    qs = (qn * (B_HD ** -0.5 * LOG2E)).astype(BF16)
    qpad_ref[...] = _dot(qs, pad_ref[...]).astype(BF16)


def _bprep(z, T, q_g, k_g, rope):
    N = z.shape[0]
    TM = min(512, T)
    nq = B_H * B_HD
    nk = B_HKV * B_HD
    width = nq + 2 * nk
    assert z.shape[1] == width
    cos, sin = _rope_tables(T if rope else TM)
    nt = T // TM if rope else 1
    const = lambda i: (0, 0)
    return pl.pallas_call(
        functools.partial(_bprep_kernel, rope=rope),
        grid=(N // TM,),
        in_specs=[pl.BlockSpec((TM, width), lambda i: (i, 0)),
                  pl.BlockSpec((1, nq), const),
                  pl.BlockSpec((1, nk), const),
                  pl.BlockSpec((TM, LANES), lambda i: (i % nt, 0)),
                  pl.BlockSpec((TM, LANES), lambda i: (i % nt, 0)),
                  pl.BlockSpec((nq, nq), const),
                  pl.BlockSpec((nk, nk), const),
                  pl.BlockSpec((nq, B_H * LANES), const)],
        out_specs=[pl.BlockSpec((TM, B_H * LANES), lambda i: (i, 0)),
                   pl.BlockSpec((TM, nk), lambda i: (i, 0)),
                   pl.BlockSpec((TM, nk), lambda i: (i, 0)),
                   pl.BlockSpec((TM, nk), lambda i: (i, 0))],
        out_shape=[jax.ShapeDtypeStruct((N, B_H * LANES), BF16),
                   jax.ShapeDtypeStruct((N, nk), F32),
                   jax.ShapeDtypeStruct((N, nk), BF16),
                   jax.ShapeDtypeStruct((N, nk), BF16)],
        compiler_params=_params("arbitrary"),
        name="attn_prep",
    )(z, jnp.tile(q_g, B_H).reshape(1, nq), jnp.tile(k_g, B_HKV).reshape(1, nk), cos, sin,
      _group_sum_matrix(nq, B_HD), _group_sum_matrix(nk, B_HD), _q_pad_matrix())


def _lane_fold(x, op):
    acc = x[:, 0:LANES]
    for j in range(1, x.shape[1] // LANES):
        acc = op(acc, x[:, j * LANES:(j + 1) * LANES])
    return acc


def _attn_kernel(*refs, has_cache, kc, nq):
    if has_cache:
        q_ref, k_ref, v_ref, ck_ref, cv_ref, o_ref, s_scr, m_scr, mprev_scr, l_scr, acc_scr = refs
        kcc = min(kc, ck_ref.shape[0])
        ncache = ck_ref.shape[0] // kcc
    else:
        q_ref, k_ref, v_ref, o_ref, s_scr, m_scr, mprev_scr, l_scr, acc_scr = refs
        kcc, ncache = kc, 0
    i = pl.program_id(1)
    tq = q_ref.shape[0]
    nlat = k_ref.shape[0] // kc

    def score(c, kblk):
        q = jnp.concatenate([q_ref[:, h * LANES:(h + 1) * LANES] for h in range(B_H)], axis=0)
        s = _dot_nt(q, kblk)
        s_scr[c, :, 0:kblk.shape[0]] = s
        m_scr[...] = jnp.maximum(m_scr[...], _lane_fold(s, jnp.maximum))

    def weight(c, vblk):
        s = s_scr[c, :, 0:vblk.shape[0]]
        mp = mprev_scr[...]
        ps = [jnp.exp2(s[:, j * LANES:(j + 1) * LANES] - mp) for j in range(vblk.shape[0] // LANES)]
        tot = ps[0]
        for pj in ps[1:]:
            tot = tot + pj
        l_scr[...] += tot
        acc_scr[...] += _dot(jnp.concatenate(ps, axis=1).astype(BF16), vblk)

    def run(do_weight, do_score):
        def unit(c, kblk, vblk):
            if do_weight:
                weight(c, vblk())
            if do_score:
                score(c, kblk())

        for c in range(ncache):
            unit(c, lambda: ck_ref[c * kcc:(c + 1) * kcc, :].astype(BF16),
                 lambda: cv_ref[c * kcc:(c + 1) * kcc, :].astype(BF16))

        def body(c, carry):
            rows = pl.ds(pl.multiple_of(c * kc, kc), kc)
            unit(ncache + c, lambda: k_ref[rows, :], lambda: v_ref[rows, :])
            return carry
        lax.fori_loop(0, nlat, body, 0)

    @pl.when(i < nq)
    def _():
        m_scr[...] = jnp.full(m_scr.shape, -jnp.inf, F32)

    @pl.when(i > 0)
    def _():
        l_scr[...] = jnp.zeros_like(l_scr)
        acc_scr[...] = jnp.zeros_like(acc_scr)

    @pl.when(i == 0)
    def _():
        run(False, True)

    @pl.when(jnp.logical_and(i > 0, i < nq))
    def _():
        run(True, True)

    @pl.when(i == nq)
    def _():
        run(True, False)

    @pl.when(i > 0)
    def _():
        r_all = acc_scr[...] / jnp.sum(l_scr[...], axis=-1, keepdims=True)
        g = B_H // B_HKV
        lane = lax.broadcasted_iota(jnp.int32, (tq, LANES), 1)
        outs = []
        for j in range(B_H // 2):
            pair = []
            for half in range(2):
                h = 2 * j + half
                r = r_all[h * tq:(h + 1) * tq, :]
                if h // g != half:
                    r = pltpu.roll(r, B_HD, 1)
                pair.append(r)
            outs.append(jnp.where(lane < B_HD, pair[0], pair[1]))
        o_ref[...] = jnp.concatenate(outs, axis=-1).astype(o_ref.dtype)

    @pl.when(i < nq)
    def _():
        mprev_scr[...] = jnp.broadcast_to(jnp.max(m_scr[...], axis=-1, keepdims=True), mprev_scr.shape)


def _attention(qpad, kr, vb, B, T, cache_k, cache_v):
    has_cache = cache_k is not None
    TQ = Q_TILE
    nq = T // TQ
    nk = B_HKV * B_HD
    in_specs = [pl.BlockSpec((TQ, B_H * LANES), lambda b, i: (b * nq + jnp.minimum(i, nq - 1), 0)),
                pl.BlockSpec((T, nk), lambda b, i: (b, 0)),
                pl.BlockSpec((T, nk), lambda b, i: (b, 0))]
    args = [qpad, kr, vb]
    kc = min(1024, T)
    nchunks = T // kc
    if has_cache:
        P = cache_k.shape[1]
        assert P % min(kc, P) == 0
        nchunks += P // min(kc, P)
        in_specs += [pl.BlockSpec((None, P, nk), lambda b, i: (b, 0, 0))] * 2
        args += [cache_k, cache_v]
    R = B_H * TQ
    return pl.pallas_call(
        functools.partial(_attn_kernel, has_cache=has_cache, kc=kc, nq=nq),
        grid=(B, nq + 1),
        in_specs=in_specs,
        out_specs=pl.BlockSpec((TQ, B_H * B_HD), lambda b, i: (b * nq + jnp.maximum(i - 1, 0), 0)),
        out_shape=jax.ShapeDtypeStruct((B * T, B_H * B_HD), BF16),
        scratch_shapes=[pltpu.VMEM((nchunks, R, kc), F32)] + [pltpu.VMEM((R, LANES), F32)] * 4,
        compiler_params=_params("arbitrary", "arbitrary"),
        name="attention",
    )(*args)


def _proj_res_kernel(*refs, n_in):
    x_ref, gate_ref = refs[0], refs[1]
    o_refs = refs[2:2 + n_in]
    w_refs = refs[2 + n_in:2 + 2 * n_in]
    out_ref = refs[2 + 2 * n_in]
    wbf_refs = refs[3 + 2 * n_in:]

    @pl.when(pl.program_id(0) == 0)
    def _():
        for w_ref, wbf_ref in zip(w_refs, wbf_refs):
            wbf_ref[...] = w_ref[...].astype(BF16)

    acc = _dot(o_refs[0][...], wbf_refs[0][...])
    for o_ref, wbf_ref in zip(o_refs[1:], wbf_refs[1:]):
        acc = acc + _dot(o_ref[...], wbf_ref[...])
    out_ref[...] = x_ref[...] + gate_ref[...] * acc


def _proj_res(x, mod, part, acts, w, rows_per_group):
    N, D = x.shape
    TM = min(512, rows_per_group)
    n_in = len(acts)
    widths = [a.shape[1] for a in acts]
    offs = np.cumsum([0] + widths[:-1]).tolist()
    in_specs = [pl.BlockSpec((TM, D), lambda i: (i, 0)),
                _mod_spec(part, D, TM, rows_per_group, 0)]
    in_specs += [pl.BlockSpec((TM, wd), lambda i: (i, 0)) for wd in widths]
    in_specs += [pl.BlockSpec((wd, D), functools.partial(lambda i, blk: (blk, 0), blk=off // wd))
                 for wd, off in zip(widths, offs)]
    return pl.pallas_call(
        functools.partial(_proj_res_kernel, n_in=n_in),
        grid=(N // TM,),
        in_specs=in_specs,
        out_specs=pl.BlockSpec((TM, D), lambda i: (i, 0)),
        out_shape=jax.ShapeDtypeStruct((N, D), F32),
        scratch_shapes=[pltpu.VMEM((wd, D), BF16) for wd in widths],
        compiler_params=_params("arbitrary"),
        name="proj_residual",
    )(x, mod, *acts, *([w] * n_in))


def _ffn_kernel(x_ref, g_ref, sh_ref, sc_ref, gate_ref, w1_ref, w3_ref, w2_ref, out_ref, h_scr, acc_scr, *, nf):
    f = pl.program_id(1)

    @pl.when(f == 0)
    def _():
        h_scr[...] = _norm_mod(x_ref[...], g_ref[...], sh_ref[...], sc_ref[...]).astype(BF16)
        acc_scr[...] = jnp.zeros_like(acc_scr)

    h = h_scr[...]
    a = _dot(h, w1_ref[...].astype(BF16))
    b = _dot(h, w3_ref[...].astype(BF16))
    acc_scr[...] += _dot((_silu(a) * b).astype(BF16), w2_ref[...].astype(BF16))

    @pl.when(f == nf - 1)
    def _():
        out_ref[...] = x_ref[...] + gate_ref[...] * acc_scr[...]


def _ffn(x, g, mod, w1, w3, w2, rows_per_group):
    N, D = x.shape
    FF = w1.shape[1]
    TM, TF = min(1024, rows_per_group), 256
    nf = FF // TF
    return pl.pallas_call(
        functools.partial(_ffn_kernel, nf=nf),
        grid=(N // TM, nf),
        in_specs=[pl.BlockSpec((TM, D), lambda i, f: (i, 0)),
                  pl.BlockSpec((1, D), lambda i, f: (0, 0)),
                  _mod_spec(3, D, TM, rows_per_group, 0),
                  _mod_spec(4, D, TM, rows_per_group, 0),
                  _mod_spec(5, D, TM, rows_per_group, 0),
                  pl.BlockSpec((D, TF), lambda i, f: (0, f)),
                  pl.BlockSpec((D, TF), lambda i, f: (0, f)),
                  pl.BlockSpec((TF, D), lambda i, f: (f, 0))],
        out_specs=pl.BlockSpec((TM, D), lambda i, f: (i, 0)),
        out_shape=jax.ShapeDtypeStruct((N, D), F32),
        scratch_shapes=[pltpu.VMEM((TM, D), BF16), pltpu.VMEM((TM, D), F32)],
        compiler_params=_params("arbitrary", "arbitrary"),
        name="ffn",
    )(x, g.reshape(1, D), mod, mod, mod, w1, w3, w2)


MOE_SB = 1024
MOE_TRG = 256
MOE_TR = 1024
MOE_CG = 4


def _two_stream_specs(shape, ntp, ax=0):
    def idx_p(*g):
        return (jnp.minimum(g[ax], ntp - 1), 0)

    def idx_s(*g):
        return (jnp.maximum(g[ax] - ntp, 0), 0)
    return pl.BlockSpec(shape, idx_p), pl.BlockSpec(shape, idx_s)


def _pool_mod_spec(part, D, TM, ntp, rows_per_group):
    def idx(i, *_):
        return (jnp.where(i < ntp, 0, 1 + ((i - ntp) * TM) // rows_per_group), part, 0, 0)
    return pl.BlockSpec((None, None, 1, D), idx)


def _route_kernel(xp_ref, xs_ref, g_ref, sh_ref, sc_ref, rw_ref, tri_ref, h_ref, info_ref, infot_ref, cum_ref,
                  carry_scr, *, ntp):
    i = pl.program_id(0)

    @pl.when(i == 0)
    def _():
        carry_scr[...] = jnp.zeros_like(carry_scr)

    x = jnp.where(i < ntp, xp_ref[...], xs_ref[...])
    h = _norm_mod(x, g_ref[...], sh_ref[...], sc_ref[...])
    h_ref[...] = h.astype(BF16)
    lane = lax.broadcasted_iota(jnp.int32, (x.shape[0], LANES), 1).astype(F32)
    logits = jnp.dot(h, rw_ref[...], precision=HIGHEST, preferred_element_type=F32)
    logits = jnp.where(lane < N_EXPERTS, logits, -jnp.inf)
    m1 = jnp.max(logits, axis=-1, keepdims=True)
    i1 = jnp.min(jnp.where(logits == m1, lane, float(LANES)), axis=-1, keepdims=True)
    rest = jnp.where(lane == i1, -jnp.inf, logits)
    m2 = jnp.max(rest, axis=-1, keepdims=True)
    i2 = jnp.min(jnp.where(rest == m2, lane, float(LANES)), axis=-1, keepdims=True)
    e2 = jnp.exp(m2 - m1)
    w1 = 1.0 / (1.0 + e2)
    w2 = e2 / (1.0 + e2)
    ind = jnp.where(jnp.logical_or(lane == i1, lane == i2), 1.0, 0.0)
    before = _dot(tri_ref[...], ind.astype(BF16)) + carry_scr[...]
    r1 = jnp.sum(jnp.where(lane == i1, before, 0.0), axis=-1, keepdims=True)
    r2 = jnp.sum(jnp.where(lane == i2, before, 0.0), axis=-1, keepdims=True)
    total = carry_scr[...] + jnp.sum(ind, axis=0, keepdims=True)
    carry_scr[...] = total
    cum_ref[...] = total
    info = jnp.where(lane == 0.0, i1, jnp.where(lane == 1.0, i2, jnp.where(lane == 2.0, w1, jnp.where(
        lane == 3.0, w2, jnp.where(lane == 4.0, r1, jnp.where(lane == 5.0, r2, 0.0))))))
    info_ref[...] = info[:, 0:SUB]
    info_t = jnp.concatenate([info[r:r + LANES, :].T for r in range(0, info.shape[0], LANES)], axis=1)
    infot_ref[...] = info_t[0:SUB, :]


def _moe_route(xp, xs, g, mod, router_w, rows_per_group):
    Np, D = xp.shape
    N = Np + xs.shape[0]
    TM = MOE_SB
    ntp = Np // TM
    nt = N // TM
    rw = jnp.pad(router_w, ((0, 0), (0, LANES - router_w.shape[1])))
    tri = jnp.asarray(np.tril(np.ones((TM, TM), np.float32), -1), dtype=BF16)
    xp_spec, xs_spec = _two_stream_specs((TM, D), ntp)
    return pl.pallas_call(
        functools.partial(_route_kernel, ntp=ntp),
        grid=(nt,),
        in_specs=[xp_spec, xs_spec,
                  pl.BlockSpec((1, D), lambda i: (0, 0)),
                  _pool_mod_spec(3, D, TM, ntp, rows_per_group),
                  _pool_mod_spec(4, D, TM, ntp, rows_per_group),
                  pl.BlockSpec((D, LANES), lambda i: (0, 0)),
                  pl.BlockSpec((TM, TM), lambda i: (0, 0))],
        out_specs=[pl.BlockSpec((TM, D), lambda i: (i, 0)),
                   pl.BlockSpec((TM, SUB), lambda i: (i, 0)),
                   pl.BlockSpec((SUB, TM), lambda i: (0, i)),
                   pl.BlockSpec((None, 1, LANES), lambda i: (i, 0, 0))],
        out_shape=[jax.ShapeDtypeStruct((N, D), BF16),
                   jax.ShapeDtypeStruct((N, SUB), F32),
                   jax.ShapeDtypeStruct((SUB, N), F32),
                   jax.ShapeDtypeStruct((nt, 1, LANES), F32)],
        scratch_shapes=[pltpu.VMEM((1, LANES), F32)],
        compiler_params=_params("arbitrary"),
        name="moe_route",
    )(xp, xs, g.reshape(1, D), mod, mod, rw, tri)


def _moe_plan(info, info_t, cum, N):
    E, SB, TRG, TR = N_EXPERTS, MOE_SB, MOE_TRG, MOE_TR
    NB = N // SB
    rmax = 2 * N + E * TR
    RG, RT = rmax // TRG, rmax // TR
    PMAX = RG + E * NB
    i32 = jnp.int32
    cum_e = cum[:, 0, :E].astype(i32).T
    cnt = cum_e[:, -1]
    tiles = (cnt + TR - 1) // TR
    start = TR * (jnp.cumsum(tiles) - tiles)

    startf = start.astype(F32)

    def region_start(e):
        out = jnp.zeros_like(e)
        for k in range(E):
            out = jnp.where(e == float(k), startf[k], out)
        return out

    pos_cols = jnp.concatenate([region_start(info[:, 0:2]) + info[:, 4:6], info[:, 2:4],
                                jnp.zeros((N, 4), F32)], axis=1)
    pos_rows = jnp.concatenate([region_start(info_t[0:2]) + info_t[4:6], jnp.zeros((6, N), F32)],
                               axis=0)

    def region(row0):
        e = jnp.clip(jnp.sum(row0[:, None] >= start[None, :], axis=1) - 1, 0, E - 1)
        return e, row0 - start[e]

    eq, lo = region(jnp.arange(RG, dtype=i32) * TRG)
    hi = jnp.minimum(lo + TRG, cnt[eq])
    first = jnp.sum(cum_e[eq] <= lo[:, None], axis=1)
    last = jnp.sum(cum_e[eq] < hi[:, None], axis=1)
    nblk = jnp.where(hi > lo, last - first + 1, 0)
    pend = jnp.cumsum(nblk)
    npairs = pend[-1]
    p = jnp.arange(PMAX, dtype=i32)
    valid = p < npairs
    pc = jnp.minimum(p, npairs - 1)
    q_of = jnp.minimum(jnp.sum(pend[None, :] <= pc[:, None], axis=1), RG - 1).astype(i32)
    pstart = pend - nblk
    s_of = (first[q_of] + pc - pstart[q_of]).astype(i32)
    g_first = jnp.logical_and(valid, pc == pstart[q_of]).astype(i32)
    gather_plan = (q_of, s_of, valid.astype(i32), g_first)

    order = jnp.argsort(jnp.where(valid, s_of * RG + q_of, jnp.iinfo(jnp.int32).max))
    s2, q2 = s_of[order], q_of[order]
    CG = MOE_CG
    blocks = jnp.arange(NB, dtype=i32)
    per_blk = jnp.sum(jnp.logical_and(valid[None, :], s2[None, :] == blocks[:, None]), axis=1)
    pb_end = jnp.cumsum(per_blk)
    pb_start = pb_end - per_blk
    nsteps = (per_blk + CG - 1) // CG
    st_end = jnp.cumsum(nsteps)
    total = st_end[-1]
    SMAX = (PMAX + CG - 1) // CG + NB
    j = jnp.arange(SMAX, dtype=i32)
    step_ok = j < total
    jc = jnp.minimum(j, total - 1)
    blk = jnp.minimum(jnp.sum(st_end[None, :] <= jc[:, None], axis=1), NB - 1).astype(i32)
    grp = jc - (st_end - nsteps)[blk]
    slot_p = pb_start[blk][:, None] + CG * grp[:, None] + jnp.arange(CG, dtype=i32)[None, :]
    slot_ok = jnp.logical_and(slot_p < pb_end[blk][:, None], step_ok[:, None])
    slot_q = jnp.where(slot_ok, q2[jnp.minimum(slot_p, npairs - 1)], q2[pb_start[blk]][:, None])
    c_first = jnp.logical_and(step_ok, grp == 0).astype(i32)
    c_last = jnp.logical_and(step_ok, grp == nsteps[blk] - 1).astype(i32)
    combine_plan = (blk, slot_q.reshape(-1).astype(i32), slot_ok.reshape(-1).astype(i32), step_ok.astype(i32),
                    c_first, c_last)

    te, tlo = region(jnp.arange(RT, dtype=i32) * TR)
    tvalid = jnp.clip(cnt[te] - tlo, 0, TR)
    last_t = jnp.sum(tiles) - 1
    t_idx = jnp.where(tvalid > 0, jnp.arange(RT, dtype=i32), last_t).astype(i32)
    ffn_plan = (t_idx, te[t_idx].astype(i32), tvalid.astype(i32))
    return pos_cols, pos_rows, gather_plan, combine_plan, ffn_plan, rmax


def _moe_gather_kernel(q_ref, s_ref, valid_ref, first_ref, pos_ref, h_ref, out_ref):
    p = pl.program_id(0)

    @pl.when(valid_ref[p] == 1)
    def _():
        rows, toks = out_ref.shape[0], h_ref.shape[0]
        row = (lax.broadcasted_iota(jnp.int32, (rows, 1), 0) + q_ref[p] * rows).astype(F32)
        hit = jnp.logical_or(pos_ref[0:1, :] == row, pos_ref[1:2, :] == row)
        sel = jnp.where(hit, 1.0, 0.0).astype(BF16)
        @pl.when(first_ref[p] == 1)
        def _():
            out_ref[...] = jnp.zeros_like(out_ref)

        out_ref[...] = out_ref[...] + _dot(sel, h_ref[...]).astype(BF16)


def _moe_gather(h, pos_rows, plan, rmax):
    N, D = h.shape
    pmax = plan[0].shape[0]
    return pl.pallas_call(
        _moe_gather_kernel,
        grid_spec=pltpu.PrefetchScalarGridSpec(
            num_scalar_prefetch=4, grid=(pmax,),
            in_specs=[pl.BlockSpec((SUB, MOE_SB), lambda p, q, s, v, f: (0, s[p])),
                      pl.BlockSpec((MOE_SB, D), lambda p, q, s, v, f: (s[p], 0))],
            out_specs=pl.BlockSpec((MOE_TRG, D), lambda p, q, s, v, f: (q[p], 0))),
        out_shape=jax.ShapeDtypeStruct((rmax, D), BF16),
        compiler_params=_params("arbitrary"),
        name="moe_gather",
    )(*plan, pos_rows, h)


def _moe_ffn_kernel(t_ref, e_ref, nv_ref, x_ref, w1_ref, w3_ref, w2_ref, out_ref, acc_scr, *, nf):
    t = pl.program_id(0)
    f = pl.program_id(1)
    nv = nv_ref[t]

    def block(rows):
        @pl.when(f == 0)
        def _():
            acc_scr[rows, :] = jnp.zeros((rows.stop - rows.start, acc_scr.shape[1]), F32)

        x = x_ref[rows, :]
        a = _dot(x, w1_ref[...].astype(BF16))
        b = _dot(x, w3_ref[...].astype(BF16))
        acc_scr[rows, :] += _dot((_silu(a) * b).astype(BF16), w2_ref[...].astype(BF16))

        @pl.when(f == nf - 1)
        def _():
            out_ref[rows, :] = acc_scr[rows, :].astype(out_ref.dtype)

    nsub = MOE_TR // MOE_TRG
    full = nv > (nsub - 1) * MOE_TRG

    @pl.when(full)
    def _():
        block(slice(0, MOE_TR))

    for sub in range(nsub - 1):
        @pl.when(jnp.logical_and(jnp.logical_not(full), sub * MOE_TRG < nv))
        def _():
            block(slice(sub * MOE_TRG, (sub + 1) * MOE_TRG))


def _moe_ffn(xs, plan, w1, w3, w2):
    rmax, D = xs.shape
    FF = w1.shape[2]
    TF = 256
    nf = FF // TF
    RT = rmax // MOE_TR

    def fidx(t, f, nv):
        return jnp.where(nv[t] > 0, f, nf - 1)

    return pl.pallas_call(
        functools.partial(_moe_ffn_kernel, nf=nf),
        grid_spec=pltpu.PrefetchScalarGridSpec(
            num_scalar_prefetch=3, grid=(RT, nf),
            in_specs=[pl.BlockSpec((MOE_TR, D), lambda t, f, ti, e, nv: (ti[t], 0)),
                      pl.BlockSpec((None, D, TF), lambda t, f, ti, e, nv: (e[t], 0, fidx(t, f, nv))),
                      pl.BlockSpec((None, D, TF), lambda t, f, ti, e, nv: (e[t], 0, fidx(t, f, nv))),
                      pl.BlockSpec((None, TF, D), lambda t, f, ti, e, nv: (e[t], fidx(t, f, nv), 0))],
            out_specs=pl.BlockSpec((MOE_TR, D), lambda t, f, ti, e, nv: (ti[t], 0)),
            scratch_shapes=[pltpu.VMEM((MOE_TR, D), F32)]),
        out_shape=jax.ShapeDtypeStruct((rmax, D), BF16),
        compiler_params=_params("arbitrary", "arbitrary"),
        name="moe_ffn",
    )(*plan, xs, w1, w3, w2)


def _moe_combine_kernel(s_ref, q_ref, slot_ok_ref, valid_ref, first_ref, last_ref, pos_ref, *refs, ntp):
    ys_refs = refs[:MOE_CG]
    xp_ref, xs_ref, gate_ref, fg_ref, op_ref, os_ref, acc_scr = refs[MOE_CG:]
    p = pl.program_id(0)

    @pl.when(valid_ref[p] == 1)
    def _():
        rows = ys_refs[0].shape[0]
        sels = []
        for k in range(MOE_CG):
            col = (lax.broadcasted_iota(jnp.int32, (1, rows), 1) + q_ref[MOE_CG * p + k] * rows).astype(F32)
            col = jnp.where(slot_ok_ref[MOE_CG * p + k] == 1, col, -1.0)
            sels.append((jnp.where(pos_ref[:, 0:1] == col, pos_ref[:, 2:3], 0.0)
                         + jnp.where(pos_ref[:, 1:2] == col, pos_ref[:, 3:4], 0.0)).astype(BF16))
        sel = jnp.concatenate(sels, axis=1)
        ys = jnp.concatenate([r[...] for r in ys_refs], axis=0)

        @pl.when(first_ref[p] == 1)
        def _():
            acc_scr[...] = jnp.zeros_like(acc_scr)

        acc_scr[...] += _dot(sel, ys)

        @pl.when(last_ref[p] == 1)
        def _():
            s = s_ref[p]
            x = jnp.where(s < ntp, xp_ref[...], xs_ref[...])
            y = x + gate_ref[...] * acc_scr[...]
            out = y * lax.rsqrt(jnp.mean(y * y, axis=-1, keepdims=True) + EPS) * fg_ref[...]

            @pl.when(s < ntp)
            def _():
                op_ref[...] = out

            @pl.when(s >= ntp)
            def _():
                os_ref[...] = out


def _moe_combine(ys, pos_cols, plan, xp, xs, mod, final_g, rows_per_group):
    Np, D = xp.shape
    Ns = xs.shape[0]
    SB = MOE_SB
    ntp = Np // SB
    nsteps = plan[0].shape[0]

    def tile_spec(k):
        return pl.BlockSpec((MOE_TRG, D), lambda p, s, q, *_: (q[MOE_CG * p + k], 0))

    def tok_p(p, s, *_):
        return (jnp.minimum(s[p], ntp - 1), 0)

    def tok_s(p, s, *_):
        return (jnp.maximum(s[p] - ntp, 0), 0)

    def gate_idx(p, s, *_):
        return (jnp.where(s[p] < ntp, 0, 1 + ((s[p] - ntp) * SB) // rows_per_group), 5, 0, 0)

    return pl.pallas_call(
        functools.partial(_moe_combine_kernel, ntp=ntp),
        grid_spec=pltpu.PrefetchScalarGridSpec(
            num_scalar_prefetch=6, grid=(nsteps,),
            in_specs=[pl.BlockSpec((SB, SUB), lambda p, s, q, *_: (s[p], 0))]
            + [tile_spec(k) for k in range(MOE_CG)]
            + [pl.BlockSpec((SB, D), tok_p),
                      pl.BlockSpec((SB, D), tok_s),
                      pl.BlockSpec((None, None, 1, D), gate_idx),
                      pl.BlockSpec((1, D), lambda p, *_: (0, 0))],
            out_specs=[pl.BlockSpec((SB, D), tok_p), pl.BlockSpec((SB, D), tok_s)],
            scratch_shapes=[pltpu.VMEM((SB, D), F32)]),
        out_shape=[jax.ShapeDtypeStruct((Np, D), F32), jax.ShapeDtypeStruct((Ns, D), F32)],
        compiler_params=_params("arbitrary"),
        name="moe_combine",
    )(*plan, pos_cols, *([ys] * MOE_CG), xp, xs, mod, final_g.reshape(1, D))


def _moe(xp, xs, g, mod, router_w, w1, w3, w2, final_g, rows_per_group):
    N = xp.shape[0] + xs.shape[0]
    h, info, info_t, cum = _moe_route(xp, xs, g, mod, router_w, rows_per_group)
    pos_cols, pos_rows, gather_plan, combine_plan, ffn_plan, rmax = _moe_plan(info, info_t, cum, N)
    x_sorted = _moe_gather(h, pos_rows, gather_plan, rmax)
    y_sorted = _moe_ffn(x_sorted, ffn_plan, w1, w3, w2)
    return _moe_combine(y_sorted, pos_cols, combine_plan, xp, xs, mod, final_g, rows_per_group)


def _gla_levels(C):
    lv, c = [], C // 2
    while c >= SUB:
        lv.append(c)
        c //= 2
    return lv


def _gla_tables(C):
    levels = _gla_levels(C)
    nr = 2 + 2 * len(levels)
    mat = np.zeros((2, nr * C, C), np.float32)
    code = np.zeros((2, C, C), np.int32)
    for d in range(2):
        p = np.arange(C) if d == 0 else C - 1 - np.arange(C)
        pi, pj = p[:, None], p[None, :]
        mat[d, 0:C] = pj <= pi
        mat[d, C:2 * C] = pj > pi
        code[d] = np.where((pj <= pi) & (pi // SUB == pj // SUB), 1, 0)
        for lv, c in enumerate(levels):
            blk = pi // c
            later = blk % 2 == 1
            mat[d, (2 + 2 * lv) * C:(3 + 2 * lv) * C] = later & (pj > blk * c - 1) & (pj <= pi)
            mat[d, (3 + 2 * lv) * C:(4 + 2 * lv) * C] = (~later) & (pj > pi) & (pj <= (blk + 1) * c - 1)
            pair = (pi // (2 * c) == pj // (2 * c)) & (pi // c != pj // c) & (pj <= pi)
            code[d] = np.where(pair, 2 + lv, code[d])
    ones = np.zeros((SUB * LANES, C), np.float32)
    for jj in range(SUB):
        ones[jj * LANES:(jj + 1) * LANES, jj::SUB] = 1.0
    return jnp.asarray(mat, dtype=BF16), jnp.asarray(code), jnp.asarray(ones, dtype=BF16)


def _bcast_sublane(x, jj):
    r, w = x.shape
    x3 = x.reshape(r // SUB, SUB, w)
    return jnp.broadcast_to(x3[:, jj:jj + 1, :], x3.shape).reshape(r, w)


def _t128(x):
    r, w = x.shape
    if w > LANES:
        return jnp.concatenate([x[:, i:i + LANES].T for i in range(0, w, LANES)], axis=0)
    return jnp.concatenate([x[i:i + LANES, :].T for i in range(0, r, LANES)], axis=1)


def _gla_kernel(q_ref, k_ref, v_ref, g_ref, lr_ref, wg_ref, ba_ref, mat_ref, code_ref, ones_ref,
                s0f_ref, s0b_ref, ng_ref, o_ref, sf_ref, sb_ref, st_scr, of_scr, *, n, C, G):
    d = pl.program_id(1)
    c = pl.program_id(2)
    levels = _gla_levels(C)

    @pl.when(jnp.logical_and(c == 0, d == 0))
    def _():
        for bb in range(G):
            for h in range(C_H):
                st_scr[bb, h] = _t128(s0f_ref[bb, h])

    @pl.when(jnp.logical_and(c == 0, d == 1))
    def _():
        for bb in range(G):
            for h in range(C_H):
                st_scr[bb, h] = _t128(s0b_ref[bb, h])

    mat = mat_ref[...]
    code = code_ref[...]
    ones = ones_ref[...]
    o_all = []
    for bb in range(G):
        xg = jnp.dot(lr_ref[bb], wg_ref[...], precision=HIGHEST, preferred_element_type=F32) + ba_ref[...]
        la = (jnp.minimum(xg, 0.0) - jnp.log1p(jnp.exp(-jnp.abs(xg)))) * (LOG2E / C_TAU)
        hi = la.astype(BF16)
        lo = (la - hi.astype(F32)).astype(BF16)
        cum = _dot(mat, hi) + _dot(mat, lo)

        outs = []
        for h in range(C_H):
            ks = slice(h * C_DK, (h + 1) * C_DK)
            qh = q_ref[bb, :, ks].astype(F32) * (C_DK ** -0.5)
            kh = k_ref[bb, :, ks].astype(F32)
            vh = v_ref[bb, :, h * C_DV:(h + 1) * C_DV].astype(F32)
            vb = vh.astype(BF16)
            b = cum[0:C, ks]
            b_rest = cum[C:2 * C, ks]
            ps = []
            for jj in range(SUB):
                dec = jnp.exp2(jnp.minimum(b - _bcast_sublane(b, jj), 0.0))
                ps.append((qh * _bcast_sublane(kh, jj) * dec).astype(BF16))
            att = jnp.where(code == 1, _dot(jnp.concatenate(ps, axis=1), ones), 0.0)
            for lv in range(len(levels)):
                eq = cum[(2 + 2 * lv) * C:(3 + 2 * lv) * C, ks]
                ek = cum[(3 + 2 * lv) * C:(4 + 2 * lv) * C, ks]
                a_lv = _dot_nt((qh * jnp.exp2(eq)).astype(BF16), (kh * jnp.exp2(ek)).astype(BF16))
                att = jnp.where(code == 2 + lv, a_lv, att)
            st = st_scr[bb, h]
            o = _dot(att.astype(BF16), vb) + _dot_nt((qh * jnp.exp2(b)).astype(BF16), st.astype(BF16))
            b_end = b[0:1, :] + b_rest[0:1, :]
            ke = (kh * jnp.exp2(b_rest)).astype(BF16)
            st_scr[bb, h] = jnp.exp2(b_end) * st + _dot(_t128(vh).astype(BF16), ke)
            outs.append(o)
        o_all.append(jnp.concatenate(outs, axis=-1))

    @pl.when(d == 0)
    def _():
        for bb in range(G):
            of_scr[bb, c] = o_all[bb]

    @pl.when(d == 1)
    def _():
        for bb in range(G):
            tot = o_all[bb] + of_scr[bb, n - 1 - c]
            res = []
            for h in range(C_H):
                sl = slice(h * C_DV, (h + 1) * C_DV)
                t = tot[:, sl]
                y = t * lax.rsqrt(jnp.mean(t * t, axis=-1, keepdims=True) + EPS) * ng_ref[:, sl]
                res.append(y * _silu(g_ref[bb, :, sl].astype(F32)))
            o_ref[bb] = jnp.concatenate(res, axis=-1).astype(o_ref.dtype)

    @pl.when(jnp.logical_and(c == n - 1, d == 0))
    def _():
        for bb in range(G):
            for h in range(C_H):
                sf_ref[bb, h] = _t128(st_scr[bb, h])

    @pl.when(jnp.logical_and(c == n - 1, d == 1))
    def _():
        for bb in range(G):
            for h in range(C_H):
                sb_ref[bb, h] = _t128(st_scr[bb, h])


def _gla(z, zg, B, T, w_a2, b_a, s0f, s0b, norm_g):
    C = GLA_CHUNK
    G = 2
    assert B % G == 0 and T % C == 0
    n = T // C
    HK = C_H * C_DK
    HV = C_H * C_DV
    mat, code, ones = _gla_tables(C)
    nr = mat.shape[1] // C
    wg = jnp.zeros((2, LANES, HK), F32)
    for dr in range(2):
        wg = wg.at[dr, dr * C_RANK:(dr + 1) * C_RANK, :].set(w_a2[dr])
    z3 = z.reshape(B, T, z.shape[1])
    zg3 = zg.reshape(B, T, zg.shape[1])

    def chunk(d, c):
        return c + d * (n - 1 - 2 * c)

    st_spec = pl.BlockSpec((G, C_H, C_DK, C_DV), lambda b, d, c: (b, 0, 0, 0))
    st_shape = jax.ShapeDtypeStruct((B, C_H, C_DK, C_DV), F32)
    o, sf, sb = pl.pallas_call(
        functools.partial(_gla_kernel, n=n, C=C, G=G),
        grid=(B // G, 2, n),
        in_specs=[pl.BlockSpec((G, C, HK), lambda b, d, c: (b, chunk(d, c), 0)),
                  pl.BlockSpec((G, C, HK), lambda b, d, c: (b, chunk(d, c), 1)),
                  pl.BlockSpec((G, C, HV), lambda b, d, c: (b, chunk(d, c), 1)),
                  pl.BlockSpec((G, C, HV), lambda b, d, c: (b, chunk(d, c), 2)),
                  pl.BlockSpec((G, C, LANES), lambda b, d, c: (b, chunk(d, c), 0)),
                  pl.BlockSpec((None, LANES, HK), lambda b, d, c: (d, 0, 0)),
                  pl.BlockSpec((None, 1, HK), lambda b, d, c: (d, 0, 0)),
                  pl.BlockSpec((None, nr * C, C), lambda b, d, c: (d, 0, 0)),
                  pl.BlockSpec((None, C, C), lambda b, d, c: (d, 0, 0)),
                  pl.BlockSpec((SUB * LANES, C), lambda b, d, c: (0, 0)),
                  st_spec, st_spec,
                  pl.BlockSpec((1, HV), lambda b, d, c: (0, 0))],
        out_specs=[pl.BlockSpec((G, C, HV), lambda b, d, c: (b, (n - 1) - d * c, 0)),
                   st_spec, st_spec],
        out_shape=[jax.ShapeDtypeStruct((B, T, HV), BF16), st_shape, st_shape],
        scratch_shapes=[pltpu.VMEM((G, C_H, C_DV, C_DK), F32), pltpu.VMEM((G, n, C, HV), F32)],
        compiler_params=_params("arbitrary", "arbitrary", "arbitrary"),
        name="gla",
    )(z3, z3, z3, z3, zg3, wg, b_a.reshape(2, 1, HK), mat, code, ones, s0f, s0b, norm_g.reshape(1, HV))
    return o.reshape(B * T, HV), sf, sb


def _run_stream(x, B, T, mods, ctx, p):
    N, D = x.shape
    rpg = N // mods[0].shape[0]
    TM = min(2048, rpg)
    nb = (B_H + 2 * B_HKV) * B_HD

    w_in = p['even_w_in'][0]
    z, zb = _norm_mm(x, p['norm1_g'][0], mods[0], (0, 1), w_in, w_in, (nb, MIX_MAIN // nb), TM, rpg)
    if ctx is None:
        s0 = jnp.zeros((B, A_H, A_DK, A_DV), F32)
        a_f0, a_b0, cache_k, cache_v = s0, s0, None, None
    else:
        cache_k, cache_v, a_f0, a_b0 = ctx[0], ctx[1], ctx[2], ctx[3]
    o_a, a_sf, a_sb = _retention(z, B, T, p['a_log_gamma'][0], a_f0, a_b0, p['a_norm_g'][0])
    qpad, k_norm, k_rot, v_bf = _bprep(zb, T, p['b_q_g'][0], p['b_k_g'][0], rope=ctx is not None)
    o_b = _attention(qpad, k_rot, v_bf, B, T, cache_k, cache_v)
    x = _proj_res(x, mods[0], 2, [o_a, o_b], p['even_w_out'][0], rpg)
    x = _ffn(x, p['norm2_g'][0], mods[0], p['ff_w1'][0], p['ff_w3'][0], p['ff_w2'][0], rpg)

    w_in = p['odd_w_in'][0]
    w_gate = jnp.pad(w_in[:, MIX_MAIN:], ((0, 0), (0, LANES - 2 * C_RANK)))
    z1, z1g = _norm_mm(x, p['norm1_g'][1], mods[1], (0, 1), w_in, w_gate, (LANES, 0), TM, rpg)
    if ctx is None:
        s0 = jnp.zeros((B, C_H, C_DK, C_DV), F32)
        c_f0, c_b0 = s0, s0
    else:
        c_f0, c_b0 = ctx[4], ctx[5]
    o_c, c_sf, c_sb = _gla(z1, z1g, B, T, p['c_w_a2'][0], p['c_b_a'][0], c_f0, c_b0, p['c_norm_g'][0])
    x = _proj_res(x, mods[1], 2, [o_c], p['odd_w_out'][0], rpg)
    v_raw = zb[:, (B_H + B_HKV) * B_HD:]
    return x, (k_norm, v_raw, a_sf, a_sb, c_sf, c_sb)


def kernel(x_prompt, x_sample, c, cache_b_k, cache_b_v, state_a_fwd, state_a_bwd, state_c_fwd, state_c_bwd,
           c_ctx, w_mod, b_mod, norm1_g, norm2_g, final_g, even_w_in, even_w_out, a_log_gamma, a_norm_g,
           b_q_g, b_k_g, odd_w_in, c_w_a2, c_b_a, c_norm_g, odd_w_out, ff_w1, ff_w3, ff_w2,
           router_w, moe_w1, moe_w3, moe_w2):
    Bp, Tp, D = x_prompt.shape
    Bs, Ts, _ = x_sample.shape
    L = w_mod.shape[0]
    assert L == 2 and even_w_in.shape[0] == 1 and odd_w_in.shape[0] == 1
    p = dict(norm1_g=norm1_g, norm2_g=norm2_g, final_g=final_g, even_w_in=even_w_in, even_w_out=even_w_out,
             a_log_gamma=a_log_gamma, a_norm_g=a_norm_g, b_q_g=b_q_g, b_k_g=b_k_g, odd_w_in=odd_w_in,
             c_w_a2=c_w_a2, c_b_a=c_b_a, c_norm_g=c_norm_g, odd_w_out=odd_w_out, ff_w1=ff_w1, ff_w3=ff_w3,
             ff_w2=ff_w2, router_w=router_w, moe_w1=moe_w1, moe_w3=moe_w3, moe_w2=moe_w2)

    rows = 8
    conds = jnp.concatenate([c_ctx[None, :], c, jnp.zeros((rows - 1 - Bs, D), F32)], axis=0)
    mod = _modulation(conds, w_mod, b_mod).reshape(L, rows, 6, 1, D)
    mods_p = [mod[l, 0:1] for l in range(L)]
    mods_s = [mod[l, 1:1 + Bs] for l in range(L)]

    x_p, kept = _run_stream(x_prompt.reshape(Bp * Tp, D), Bp, Tp, mods_p, None, p)
    nk = B_HKV * B_HD
    ctx = (cache_b_k[:, 0].reshape(Bs, -1, nk), cache_b_v[:, 0].reshape(Bs, -1, nk),
           state_a_fwd[:, 0], state_a_bwd[:, 0], state_c_fwd[:, 0], state_c_bwd[:, 0])
    x_s, _ = _run_stream(x_sample.reshape(Bs * Ts, D), Bs, Ts, mods_s, ctx, p)
    y_p, y_s = _moe(x_p, x_s, norm2_g[1], mod[1, 0:1 + Bs], router_w[0], moe_w1[0], moe_w3[0], moe_w2[0],
                    final_g, Ts)

    k_norm, v_raw, a_sf, a_sb, c_sf, c_sb = kept
    return (y_p.reshape(Bp, Tp, D), y_s.reshape(Bs, Ts, D),
            k_norm.reshape(Bp, 1, Tp, B_HKV, B_HD), v_raw.reshape(Bp, 1, Tp, B_HKV, B_HD),
            a_sf[:, None], a_sb[:, None], c_sf[:, None], c_sb[:, None])
```

```python
import functools

import numpy as np
import jax
import jax.numpy as jnp
from jax import lax
from jax.experimental import pallas as pl
from jax.experimental.pallas import tpu as pltpu

F32 = jnp.float32
BF16 = jnp.bfloat16
EPS = 1e-6
HIGHEST = lax.Precision.HIGHEST
LOG2E = 1.4426950408889634

VMEM_LIMIT_BYTES = 56 * 1024 * 1024

A_H, A_DK, A_DV = 4, 128, 256
B_H, B_HKV, B_HD = 8, 2, 64
C_H, C_DK, C_DV, C_RANK = 4, 128, 256, 16
C_TAU = 16.0
GRID_W = 64
ROPE_THETA = 10000.0
N_EXPERTS = 8
LANES = 128
SUB = 8
RET_CHUNK = 128
GLA_CHUNK = 128
Q_TILE = 128


def _params(*sem):
    return pltpu.CompilerParams(dimension_semantics=sem, vmem_limit_bytes=VMEM_LIMIT_BYTES)


def _dot(a, b):
    return jnp.dot(a, b, preferred_element_type=F32)


def _dot_nt(a, b):
    return lax.dot_general(a, b, (((1,), (1,)), ((), ())), preferred_element_type=F32)


def _silu(x):
    return x * jax.nn.sigmoid(x)


def _norm_mod(x, g, sh, sc):
    r = lax.rsqrt(jnp.mean(x * x, axis=-1, keepdims=True) + EPS)
    return (x * r * g) * (1.0 + sc) + sh


def _mod_kernel(c_ref, w_ref, b_ref, o_ref):
    c = c_ref[...]
    o_ref[...] = jnp.dot(_silu(c), w_ref[...], precision=HIGHEST, preferred_element_type=F32) + b_ref[...]


def _modulation(conds, w_mod, b_mod):
    L, D, D6 = w_mod.shape
    R = conds.shape[0]
    TN = 1024
    return pl.pallas_call(
        _mod_kernel,
        grid=(L, D6 // TN),
        in_specs=[pl.BlockSpec((R, D), lambda l, j: (0, 0)),
                  pl.BlockSpec((None, D, TN), lambda l, j: (l, 0, j)),
                  pl.BlockSpec((None, 1, TN), lambda l, j: (l, 0, j))],
        out_specs=pl.BlockSpec((None, R, TN), lambda l, j: (l, 0, j)),
        out_shape=jax.ShapeDtypeStruct((L, R, D6), F32),
        compiler_params=_params("arbitrary", "arbitrary"),
        name="modulation",
    )(conds, w_mod, b_mod.reshape(L, 1, D6))


def _mod_spec(part, D, TM, rows_per_group, axis):
    def idx(*g):
        return ((g[axis] * TM) // rows_per_group, part, 0, 0)
    return pl.BlockSpec((None, None, 1, D), idx)


MIX_MAIN = A_H * (2 * A_DK + 2 * A_DV)
MIX_TN = 768


def _norm_mm_kernel(x_ref, g_ref, sh_ref, sc_ref, w_ref, we_ref, o_ref, oe_ref, h_scr, *, nmain):
    j = pl.program_id(1)

    @pl.when(j == 0)
    def _():
        h_scr[...] = _norm_mod(x_ref[...], g_ref[...], sh_ref[...], sc_ref[...]).astype(BF16)

    @pl.when(j < nmain)
    def _():
        o_ref[...] = _dot(h_scr[...], w_ref[...].astype(BF16)).astype(o_ref.dtype)

    @pl.when(j == nmain)
    def _():
        oe_ref[...] = _dot(h_scr[...], we_ref[...].astype(BF16))


def _norm_mm(x, g, mod, parts, w, w_extra, extra_block, TM, rows_per_group):
    N, D = x.shape
    nmain = MIX_MAIN // MIX_TN
    WE = extra_block[0]
    return pl.pallas_call(
        functools.partial(_norm_mm_kernel, nmain=nmain),
        grid=(N // TM, nmain + 1),
        in_specs=[pl.BlockSpec((TM, D), lambda i, j: (i, 0)),
                  pl.BlockSpec((1, D), lambda i, j: (0, 0)),
                  _mod_spec(parts[0], D, TM, rows_per_group, 0),
                  _mod_spec(parts[1], D, TM, rows_per_group, 0),
                  pl.BlockSpec((D, MIX_TN), lambda i, j: (0, jnp.minimum(j, nmain - 1))),
                  pl.BlockSpec((D, WE), lambda i, j: (0, extra_block[1]))],
        out_specs=[pl.BlockSpec((TM, MIX_TN), lambda i, j: (i, jnp.minimum(j, nmain - 1))),
                   pl.BlockSpec((TM, WE), lambda i, j: (i, 0))],
        out_shape=[jax.ShapeDtypeStruct((N, MIX_MAIN), BF16), jax.ShapeDtypeStruct((N, WE), F32)],
        scratch_shapes=[pltpu.VMEM((TM, D), BF16)],
        compiler_params=_params("arbitrary", "arbitrary"),
        name="norm_mm",
    )(x, g.reshape(1, D), mod, mod, w, w_extra)


def _ret_kernel(lg_ref, q_ref, k_ref, v_ref, ag_ref, s0f_ref, s0b_ref, ng_ref,
                o_ref, sf_ref, sb_ref, s_scr, of_scr, *, n, C):
    d = pl.program_id(1)
    c = pl.program_id(2)

    @pl.when(jnp.logical_and(c == 0, d == 0))
    def _():
        s_scr[...] = s0f_ref[...]

    @pl.when(jnp.logical_and(c == 0, d == 1))
    def _():
        s_scr[...] = s0b_ref[...]

    df = d.astype(F32)
    sgn = 1.0 - 2.0 * df
    ii = lax.broadcasted_iota(jnp.int32, (C, C), 0).astype(F32)
    jj = lax.broadcasted_iota(jnp.int32, (C, C), 1).astype(F32)
    dd = (ii - jj) * sgn
    feeds = dd >= 0.0
    ddc = jnp.maximum(dd, 0.0)
    ri = lax.broadcasted_iota(jnp.int32, (C, 1), 0).astype(F32)
    pos_q = (ri + 1.0) + df * (C - 2.0 * ri - 1.0)
    pos_k = (C - 1.0 - ri) + df * (2.0 * ri - C + 1.0)
    chunk_len = jnp.full((1, A_DV), float(C), F32)

    outs = []
    for h in range(A_H):
        lg = lg_ref[d, h]
        dmask = jnp.where(feeds, jnp.exp2(lg * ddc), 0.0)
        qh = q_ref[:, h * A_DK:(h + 1) * A_DK].astype(F32) * (A_DK ** -0.5)
        kh = k_ref[:, h * A_DK:(h + 1) * A_DK].astype(F32)
        vh = v_ref[:, h * A_DV:(h + 1) * A_DV].astype(BF16)
        s = s_scr[h]
        att = _dot_nt(qh.astype(BF16), kh.astype(BF16)) * dmask
        o = _dot(att.astype(BF16), vh) + _dot((qh * jnp.exp2(lg * pos_q)).astype(BF16), s.astype(BF16))
        kd = kh * jnp.exp2(lg * pos_k)
        s_scr[h] = jnp.exp2(lg * chunk_len) * s + _dot(kd.T.astype(BF16), vh)
        outs.append(o)
    o_all = jnp.concatenate(outs, axis=-1)

    @pl.when(d == 0)
    def _():
        of_scr[c] = o_all

    @pl.when(d == 1)
    def _():
        tot = o_all + of_scr[n - 1 - c]
        res = []
        for h in range(A_H):
            sl = slice(h * A_DV, (h + 1) * A_DV)
            t = tot[:, sl]
            dev = t - jnp.mean(t, axis=-1, keepdims=True)
            y = dev * lax.rsqrt(jnp.mean(dev * dev, axis=-1, keepdims=True) + EPS) * ng_ref[:, sl]
            res.append(y * _silu(ag_ref[:, sl].astype(F32)))
        o_ref[...] = jnp.concatenate(res, axis=-1).astype(o_ref.dtype)

    @pl.when(jnp.logical_and(c == n - 1, d == 0))
    def _():
        sf_ref[...] = s_scr[...]

    @pl.when(jnp.logical_and(c == n - 1, d == 1))
    def _():
        sb_ref[...] = s_scr[...]


def _retention(z, B, T, log_gamma, s0f, s0b, norm_g):
    C = RET_CHUNK
    n = T // C
    HV = A_H * A_DV

    def row(b, d, c):
        return b * n + c + d * (n - 1 - 2 * c)

    st_spec = pl.BlockSpec((None, A_H, A_DK, A_DV), lambda b, d, c: (b, 0, 0, 0))
    st_shape = jax.ShapeDtypeStruct((B, A_H, A_DK, A_DV), F32)
    return pl.pallas_call(
        functools.partial(_ret_kernel, n=n, C=C),
        grid=(B, 2, n),
        in_specs=[pl.BlockSpec(memory_space=pltpu.SMEM),
                  pl.BlockSpec((C, 512), lambda b, d, c: (row(b, d, c), 0)),
                  pl.BlockSpec((C, 512), lambda b, d, c: (row(b, d, c), 1)),
                  pl.BlockSpec((C, HV), lambda b, d, c: (row(b, d, c), 1)),
                  pl.BlockSpec((C, HV), lambda b, d, c: (row(b, d, c), 2)),
                  st_spec, st_spec,
                  pl.BlockSpec((1, HV), lambda b, d, c: (0, 0))],
        out_specs=[pl.BlockSpec((C, HV), lambda b, d, c: (b * n + (n - 1) - d * c, 0)),
                   st_spec, st_spec],
        out_shape=[jax.ShapeDtypeStruct((B * T, HV), BF16), st_shape, st_shape],
        scratch_shapes=[pltpu.VMEM((A_H, A_DK, A_DV), F32), pltpu.VMEM((n, C, HV), F32)],
        compiler_params=_params("arbitrary", "arbitrary", "arbitrary"),
        name="retention",
    )(log_gamma * LOG2E, z, z, z, z, s0f, s0b, norm_g.reshape(1, HV))


def _group_sum_matrix(width, group):
    i = np.arange(width)
    return jnp.asarray((i[:, None] // group == i[None, :] // group).astype(np.float32), dtype=BF16)


def _q_pad_matrix():
    m = np.zeros((B_H * B_HD, B_H * LANES), np.float32)
    g = B_H // B_HKV
    for h in range(B_H):
        for t in range(B_HD):
            m[h * B_HD + t, h * LANES + (h // g) * B_HD + t] = 1.0
    return jnp.asarray(m, dtype=BF16)


def _rope_tables(T):
    rows = T // GRID_W
    row = np.repeat(np.arange(rows, dtype=np.float64), GRID_W)
    col = np.tile(np.arange(GRID_W, dtype=np.float64), rows)
    nq = B_HD // 4
    inv = ROPE_THETA ** (-np.arange(nq, dtype=np.float64) / nq)
    ang = np.concatenate([row[:, None] * inv, col[:, None] * inv], axis=-1)
    cos = np.repeat(np.cos(ang), 2, axis=-1)
    sin = np.repeat(np.sin(ang), 2, axis=-1)
    sign = np.tile(np.array([-1.0, 1.0]), B_HD // 2)
    reps = LANES // B_HD
    return (jnp.asarray(np.tile(cos, (1, reps)), dtype=F32),
            jnp.asarray(np.tile(sin * sign, (1, reps)), dtype=F32))


def _group_rmsnorm(x, gsum, g):
    x2 = x * x
    hi = x2.astype(BF16)
    lo = (x2 - hi.astype(F32)).astype(BF16)
    ss = _dot(hi, gsum) + _dot(lo, gsum)
    return x * lax.rsqrt(ss * (1.0 / B_HD) + EPS) * g


def _rotate_pairs(x, cos, sin_signed):
    n = x.shape[1]
    lane = lax.broadcasted_iota(jnp.int32, x.shape, 1)
    partner = jnp.where(lane % 2 == 0, pltpu.roll(x, n - 1, 1), pltpu.roll(x, 1, 1))
    reps = n // LANES
    if reps > 1:
        cos = jnp.concatenate([cos] * reps, axis=1)
        sin_signed = jnp.concatenate([sin_signed] * reps, axis=1)
    return x * cos + partner * sin_signed


def _bprep_kernel(z_ref, qg_ref, kg_ref, cos_ref, sin_ref, gq_ref, gk_ref, pad_ref,
                  qpad_ref, kn_ref, kr_ref, vb_ref, *, rope):
    nq = B_H * B_HD
    nk = B_HKV * B_HD
    qn = _group_rmsnorm(z_ref[:, 0:nq], gq_ref[...], qg_ref[...])
    kn = _group_rmsnorm(z_ref[:, nq:nq + nk], gk_ref[...], kg_ref[...])
    kn_ref[...] = kn
    if rope:
        qn = _rotate_pairs(qn, cos_ref[...], sin_ref[...])
        kn = _rotate_pairs(kn, cos_ref[...], sin_ref[...])
    kr_ref[...] = kn.astype(BF16)
    vb_ref[...] = z_ref[:, nq + nk:nq + 2 * nk].astype(BF16)
    qs = (qn * (B_HD ** -0.5 * LOG2E)).astype(BF16)
    qpad_ref[...] = _dot(qs, pad_ref[...]).astype(BF16)


def _bprep(z, T, q_g, k_g, rope):
    N = z.shape[0]
    TM = min(512, T)
    nq = B_H * B_HD
    nk = B_HKV * B_HD
    width = nq + 2 * nk
    assert z.shape[1] == width
    cos, sin = _rope_tables(T if rope else TM)
    nt = T // TM if rope else 1
    const = lambda i: (0, 0)
    return pl.pallas_call(
        functools.partial(_bprep_kernel, rope=rope),
        grid=(N // TM,),
        in_specs=[pl.BlockSpec((TM, width), lambda i: (i, 0)),
                  pl.BlockSpec((1, nq), const),
                  pl.BlockSpec((1, nk), const),
                  pl.BlockSpec((TM, LANES), lambda i: (i % nt, 0)),
                  pl.BlockSpec((TM, LANES), lambda i: (i % nt, 0)),
                  pl.BlockSpec((nq, nq), const),
                  pl.BlockSpec((nk, nk), const),
                  pl.BlockSpec((nq, B_H * LANES), const)],
        out_specs=[pl.BlockSpec((TM, B_H * LANES), lambda i: (i, 0)),
                   pl.BlockSpec((TM, nk), lambda i: (i, 0)),
                   pl.BlockSpec((TM, nk), lambda i: (i, 0)),
                   pl.BlockSpec((TM, nk), lambda i: (i, 0))],
        out_shape=[jax.ShapeDtypeStruct((N, B_H * LANES), BF16),
                   jax.ShapeDtypeStruct((N, nk), F32),
                   jax.ShapeDtypeStruct((N, nk), BF16),
                   jax.ShapeDtypeStruct((N, nk), BF16)],
        compiler_params=_params("arbitrary"),
        name="attn_prep",
    )(z, jnp.tile(q_g, B_H).reshape(1, nq), jnp.tile(k_g, B_HKV).reshape(1, nk), cos, sin,
      _group_sum_matrix(nq, B_HD), _group_sum_matrix(nk, B_HD), _q_pad_matrix())


def _lane_fold(x, op):
    acc = x[:, 0:LANES]
    for j in range(1, x.shape[1] // LANES):
        acc = op(acc, x[:, j * LANES:(j + 1) * LANES])
    return acc


def _attn_kernel(*refs, has_cache, kc, nq):
    if has_cache:
        q_ref, k_ref, v_ref, ck_ref, cv_ref, o_ref, s_scr, m_scr, mprev_scr, l_scr, acc_scr = refs
        kcc = min(kc, ck_ref.shape[0])
        ncache = ck_ref.shape[0] // kcc
    else:
        q_ref, k_ref, v_ref, o_ref, s_scr, m_scr, mprev_scr, l_scr, acc_scr = refs
        kcc, ncache = kc, 0
    i = pl.program_id(1)
    tq = q_ref.shape[0]
    nlat = k_ref.shape[0] // kc

    def score(c, kblk):
        q = jnp.concatenate([q_ref[:, h * LANES:(h + 1) * LANES] for h in range(B_H)], axis=0)
        s = _dot_nt(q, kblk)
        s_scr[c, :, 0:kblk.shape[0]] = s
        m_scr[...] = jnp.maximum(m_scr[...], _lane_fold(s, jnp.maximum))

    def weight(c, vblk):
        s = s_scr[c, :, 0:vblk.shape[0]]
        mp = mprev_scr[...]
        ps = [jnp.exp2(s[:, j * LANES:(j + 1) * LANES] - mp) for j in range(vblk.shape[0] // LANES)]
        tot = ps[0]
        for pj in ps[1:]:
            tot = tot + pj
        l_scr[...] += tot
        acc_scr[...] += _dot(jnp.concatenate(ps, axis=1).astype(BF16), vblk)

    def run(do_weight, do_score):
        def unit(c, kblk, vblk):
            if do_weight:
                weight(c, vblk())
            if do_score:
                score(c, kblk())

        for c in range(ncache):
            unit(c, lambda: ck_ref[c * kcc:(c + 1) * kcc, :].astype(BF16),
                 lambda: cv_ref[c * kcc:(c + 1) * kcc, :].astype(BF16))

        def body(c, carry):
            rows = pl.ds(pl.multiple_of(c * kc, kc), kc)
            unit(ncache + c, lambda: k_ref[rows, :], lambda: v_ref[rows, :])
            return carry
        lax.fori_loop(0, nlat, body, 0)

    @pl.when(i < nq)
    def _():
        m_scr[...] = jnp.full(m_scr.shape, -jnp.inf, F32)

    @pl.when(i > 0)
    def _():
        l_scr[...] = jnp.zeros_like(l_scr)
        acc_scr[...] = jnp.zeros_like(acc_scr)

    @pl.when(i == 0)
    def _():
        run(False, True)

    @pl.when(jnp.logical_and(i > 0, i < nq))
    def _():
        run(True, True)

    @pl.when(i == nq)
    def _():
        run(True, False)

    @pl.when(i > 0)
    def _():
        r_all = acc_scr[...] / jnp.sum(l_scr[...], axis=-1, keepdims=True)
        g = B_H // B_HKV
        lane = lax.broadcasted_iota(jnp.int32, (tq, LANES), 1)
        outs = []
        for j in range(B_H // 2):
            pair = []
            for half in range(2):
                h = 2 * j + half
                r = r_all[h * tq:(h + 1) * tq, :]
                if h // g != half:
                    r = pltpu.roll(r, B_HD, 1)
                pair.append(r)
            outs.append(jnp.where(lane < B_HD, pair[0], pair[1]))
        o_ref[...] = jnp.concatenate(outs, axis=-1).astype(o_ref.dtype)

    @pl.when(i < nq)
    def _():
        mprev_scr[...] = jnp.broadcast_to(jnp.max(m_scr[...], axis=-1, keepdims=True), mprev_scr.shape)


def _attention(qpad, kr, vb, B, T, cache_k, cache_v):
    has_cache = cache_k is not None
    TQ = Q_TILE
    nq = T // TQ
    nk = B_HKV * B_HD
    in_specs = [pl.BlockSpec((TQ, B_H * LANES), lambda b, i: (b * nq + jnp.minimum(i, nq - 1), 0)),
                pl.BlockSpec((T, nk), lambda b, i: (b, 0)),
                pl.BlockSpec((T, nk), lambda b, i: (b, 0))]
    args = [qpad, kr, vb]
    kc = min(1024, T)
    nchunks = T // kc
    if has_cache:
        P = cache_k.shape[1]
        assert P % min(kc, P) == 0
        nchunks += P // min(kc, P)
        in_specs += [pl.BlockSpec((None, P, nk), lambda b, i: (b, 0, 0))] * 2
        args += [cache_k, cache_v]
    R = B_H * TQ
    return pl.pallas_call(
        functools.partial(_attn_kernel, has_cache=has_cache, kc=kc, nq=nq),
        grid=(B, nq + 1),
        in_specs=in_specs,
        out_specs=pl.BlockSpec((TQ, B_H * B_HD), lambda b, i: (b * nq + jnp.maximum(i - 1, 0), 0)),
        out_shape=jax.ShapeDtypeStruct((B * T, B_H * B_HD), BF16),
        scratch_shapes=[pltpu.VMEM((nchunks, R, kc), F32)] + [pltpu.VMEM((R, LANES), F32)] * 4,
        compiler_params=_params("arbitrary", "arbitrary"),
        name="attention",
    )(*args)


def _proj_res_kernel(*refs, n_in):
    x_ref, gate_ref = refs[0], refs[1]
    o_refs = refs[2:2 + n_in]
    w_refs = refs[2 + n_in:2 + 2 * n_in]
    out_ref = refs[2 + 2 * n_in]
    wbf_refs = refs[3 + 2 * n_in:]

    @pl.when(pl.program_id(0) == 0)
    def _():
        for w_ref, wbf_ref in zip(w_refs, wbf_refs):
            wbf_ref[...] = w_ref[...].astype(BF16)

    acc = _dot(o_refs[0][...], wbf_refs[0][...])
    for o_ref, wbf_ref in zip(o_refs[1:], wbf_refs[1:]):
        acc = acc + _dot(o_ref[...], wbf_ref[...])
    out_ref[...] = x_ref[...] + gate_ref[...] * acc


def _proj_res(x, mod, part, acts, w, rows_per_group):
    N, D = x.shape
    TM = min(512, rows_per_group)
    n_in = len(acts)
    widths = [a.shape[1] for a in acts]
    offs = np.cumsum([0] + widths[:-1]).tolist()
    in_specs = [pl.BlockSpec((TM, D), lambda i: (i, 0)),
                _mod_spec(part, D, TM, rows_per_group, 0)]
    in_specs += [pl.BlockSpec((TM, wd), lambda i: (i, 0)) for wd in widths]
    in_specs += [pl.BlockSpec((wd, D), functools.partial(lambda i, blk: (blk, 0), blk=off // wd))
                 for wd, off in zip(widths, offs)]
    return pl.pallas_call(
        functools.partial(_proj_res_kernel, n_in=n_in),
        grid=(N // TM,),
        in_specs=in_specs,
        out_specs=pl.BlockSpec((TM, D), lambda i: (i, 0)),
        out_shape=jax.ShapeDtypeStruct((N, D), F32),
        scratch_shapes=[pltpu.VMEM((wd, D), BF16) for wd in widths],
        compiler_params=_params("arbitrary"),
        name="proj_residual",
    )(x, mod, *acts, *([w] * n_in))


def _ffn_kernel(x_ref, g_ref, sh_ref, sc_ref, gate_ref, w1_ref, w3_ref, w2_ref, out_ref, h_scr, acc_scr, *, nf):
    f = pl.program_id(1)

    @pl.when(f == 0)
    def _():
        h_scr[...] = _norm_mod(x_ref[...], g_ref[...], sh_ref[...], sc_ref[...]).astype(BF16)
        acc_scr[...] = jnp.zeros_like(acc_scr)

    h = h_scr[...]
    a = _dot(h, w1_ref[...].astype(BF16))
    b = _dot(h, w3_ref[...].astype(BF16))
    acc_scr[...] += _dot((_silu(a) * b).astype(BF16), w2_ref[...].astype(BF16))

    @pl.when(f == nf - 1)
    def _():
        out_ref[...] = x_ref[...] + gate_ref[...] * acc_scr[...]


def _ffn(x, g, mod, w1, w3, w2, rows_per_group):
    N, D = x.shape
    FF = w1.shape[1]
    TM, TF = min(1024, rows_per_group), 256
    nf = FF // TF
    return pl.pallas_call(
        functools.partial(_ffn_kernel, nf=nf),
        grid=(N // TM, nf),
        in_specs=[pl.BlockSpec((TM, D), lambda i, f: (i, 0)),
                  pl.BlockSpec((1, D), lambda i, f: (0, 0)),
                  _mod_spec(3, D, TM, rows_per_group, 0),
                  _mod_spec(4, D, TM, rows_per_group, 0),
                  _mod_spec(5, D, TM, rows_per_group, 0),
                  pl.BlockSpec((D, TF), lambda i, f: (0, f)),
                  pl.BlockSpec((D, TF), lambda i, f: (0, f)),
                  pl.BlockSpec((TF, D), lambda i, f: (f, 0))],
        out_specs=pl.BlockSpec((TM, D), lambda i, f: (i, 0)),
        out_shape=jax.ShapeDtypeStruct((N, D), F32),
        scratch_shapes=[pltpu.VMEM((TM, D), BF16), pltpu.VMEM((TM, D), F32)],
        compiler_params=_params("arbitrary", "arbitrary"),
        name="ffn",
    )(x, g.reshape(1, D), mod, mod, mod, w1, w3, w2)


MOE_SB = 1024
MOE_TRG = 256
MOE_TR = 2048
MOE_CG = 4


def _two_stream_specs(shape, ntp, ax=0):
    def idx_p(*g):
        return (jnp.minimum(g[ax], ntp - 1), 0)

    def idx_s(*g):
        return (jnp.maximum(g[ax] - ntp, 0), 0)
    return pl.BlockSpec(shape, idx_p), pl.BlockSpec(shape, idx_s)


def _pool_mod_spec(part, D, TM, ntp, rows_per_group):
    def idx(i, *_):
        return (jnp.where(i < ntp, 0, 1 + ((i - ntp) * TM) // rows_per_group), part, 0, 0)
    return pl.BlockSpec((None, None, 1, D), idx)


def _route_kernel(xp_ref, xs_ref, g_ref, sh_ref, sc_ref, rw_ref, tri_ref, h_ref, info_ref, infot_ref, cum_ref,
                  carry_scr, *, ntp):
    i = pl.program_id(0)

    @pl.when(i == 0)
    def _():
        carry_scr[...] = jnp.zeros_like(carry_scr)

    x = jnp.where(i < ntp, xp_ref[...], xs_ref[...])
    h = _norm_mod(x, g_ref[...], sh_ref[...], sc_ref[...])
    h_ref[...] = h.astype(BF16)
    lane = lax.broadcasted_iota(jnp.int32, (x.shape[0], LANES), 1).astype(F32)
    logits = jnp.dot(h, rw_ref[...], precision=HIGHEST, preferred_element_type=F32)
    logits = jnp.where(lane < N_EXPERTS, logits, -jnp.inf)
    m1 = jnp.max(logits, axis=-1, keepdims=True)
    i1 = jnp.min(jnp.where(logits == m1, lane, float(LANES)), axis=-1, keepdims=True)
    rest = jnp.where(lane == i1, -jnp.inf, logits)
    m2 = jnp.max(rest, axis=-1, keepdims=True)
    i2 = jnp.min(jnp.where(rest == m2, lane, float(LANES)), axis=-1, keepdims=True)
    e2 = jnp.exp(m2 - m1)
    w1 = 1.0 / (1.0 + e2)
    w2 = e2 / (1.0 + e2)
    ind = jnp.where(jnp.logical_or(lane == i1, lane == i2), 1.0, 0.0)
    before = _dot(tri_ref[...], ind.astype(BF16)) + carry_scr[...]
    r1 = jnp.sum(jnp.where(lane == i1, before, 0.0), axis=-1, keepdims=True)
    r2 = jnp.sum(jnp.where(lane == i2, before, 0.0), axis=-1, keepdims=True)
    total = carry_scr[...] + jnp.sum(ind, axis=0, keepdims=True)
    carry_scr[...] = total
    cum_ref[...] = total
    info = jnp.where(lane == 0.0, i1, jnp.where(lane == 1.0, i2, jnp.where(lane == 2.0, w1, jnp.where(
        lane == 3.0, w2, jnp.where(lane == 4.0, r1, jnp.where(lane == 5.0, r2, 0.0))))))
    info_ref[...] = info[:, 0:SUB]
    info_t = jnp.concatenate([info[r:r + LANES, :].T for r in range(0, info.shape[0], LANES)], axis=1)
    infot_ref[...] = info_t[0:SUB, :]


def _moe_route(xp, xs, g, mod, router_w, rows_per_group):
    Np, D = xp.shape
    N = Np + xs.shape[0]
    TM = MOE_SB
    ntp = Np // TM
    nt = N // TM
    rw = jnp.pad(router_w, ((0, 0), (0, LANES - router_w.shape[1])))
    tri = jnp.asarray(np.tril(np.ones((TM, TM), np.float32), -1), dtype=BF16)
    xp_spec, xs_spec = _two_stream_specs((TM, D), ntp)
    return pl.pallas_call(
        functools.partial(_route_kernel, ntp=ntp),
        grid=(nt,),
        in_specs=[xp_spec, xs_spec,
                  pl.BlockSpec((1, D), lambda i: (0, 0)),
                  _pool_mod_spec(3, D, TM, ntp, rows_per_group),
                  _pool_mod_spec(4, D, TM, ntp, rows_per_group),
                  pl.BlockSpec((D, LANES), lambda i: (0, 0)),
                  pl.BlockSpec((TM, TM), lambda i: (0, 0))],
        out_specs=[pl.BlockSpec((TM, D), lambda i: (i, 0)),
                   pl.BlockSpec((TM, SUB), lambda i: (i, 0)),
                   pl.BlockSpec((SUB, TM), lambda i: (0, i)),
                   pl.BlockSpec((None, 1, LANES), lambda i: (i, 0, 0))],
        out_shape=[jax.ShapeDtypeStruct((N, D), BF16),
                   jax.ShapeDtypeStruct((N, SUB), F32),
                   jax.ShapeDtypeStruct((SUB, N), F32),
                   jax.ShapeDtypeStruct((nt, 1, LANES), F32)],
        scratch_shapes=[pltpu.VMEM((1, LANES), F32)],
        compiler_params=_params("arbitrary"),
        name="moe_route",
    )(xp, xs, g.reshape(1, D), mod, mod, rw, tri)


def _moe_plan(info, info_t, cum, N):
    E, SB, TRG, TR = N_EXPERTS, MOE_SB, MOE_TRG, MOE_TR
    NB = N // SB
    rmax = 2 * N + E * TR
    RG, RT = rmax // TRG, rmax // TR
    PMAX = RG + E * NB
    i32 = jnp.int32
    cum_e = cum[:, 0, :E].astype(i32).T
    cnt = cum_e[:, -1]
    tiles = (cnt + TR - 1) // TR
    start = TR * (jnp.cumsum(tiles) - tiles)

    startf = start.astype(F32)

    def region_start(e):
        out = jnp.zeros_like(e)
        for k in range(E):
            out = jnp.where(e == float(k), startf[k], out)
        return out

    pos_cols = jnp.concatenate([region_start(info[:, 0:2]) + info[:, 4:6], info[:, 2:4],
                                jnp.zeros((N, 4), F32)], axis=1)
    pos_rows = jnp.concatenate([region_start(info_t[0:2]) + info_t[4:6], jnp.zeros((6, N), F32)],
                               axis=0)

    def region(row0):
        e = jnp.clip(jnp.sum(row0[:, None] >= start[None, :], axis=1) - 1, 0, E - 1)
        return e, row0 - start[e]

    eq, lo = region(jnp.arange(RG, dtype=i32) * TRG)
    hi = jnp.minimum(lo + TRG, cnt[eq])
    first = jnp.sum(cum_e[eq] <= lo[:, None], axis=1)
    last = jnp.sum(cum_e[eq] < hi[:, None], axis=1)
    nblk = jnp.where(hi > lo, last - first + 1, 0)
    pend = jnp.cumsum(nblk)
    npairs = pend[-1]
    p = jnp.arange(PMAX, dtype=i32)
    valid = p < npairs
    pc = jnp.minimum(p, npairs - 1)
    q_of = jnp.minimum(jnp.sum(pend[None, :] <= pc[:, None], axis=1), RG - 1).astype(i32)
    pstart = pend - nblk
    s_of = (first[q_of] + pc - pstart[q_of]).astype(i32)
    g_first = jnp.logical_and(valid, pc == pstart[q_of]).astype(i32)
    gather_plan = (q_of, s_of, valid.astype(i32), g_first)

    order = jnp.argsort(jnp.where(valid, s_of * RG + q_of, jnp.iinfo(jnp.int32).max))
    s2, q2 = s_of[order], q_of[order]
    CG = MOE_CG
    blocks = jnp.arange(NB, dtype=i32)
    per_blk = jnp.sum(jnp.logical_and(valid[None, :], s2[None, :] == blocks[:, None]), axis=1)
    pb_end = jnp.cumsum(per_blk)
    pb_start = pb_end - per_blk
    nsteps = (per_blk + CG - 1) // CG
    st_end = jnp.cumsum(nsteps)
    total = st_end[-1]
    SMAX = (PMAX + CG - 1) // CG + NB
    j = jnp.arange(SMAX, dtype=i32)
    step_ok = j < total
    jc = jnp.minimum(j, total - 1)
    blk = jnp.minimum(jnp.sum(st_end[None, :] <= jc[:, None], axis=1), NB - 1).astype(i32)
    grp = jc - (st_end - nsteps)[blk]
    slot_p = pb_start[blk][:, None] + CG * grp[:, None] + jnp.arange(CG, dtype=i32)[None, :]
    slot_ok = jnp.logical_and(slot_p < pb_end[blk][:, None], step_ok[:, None])
    slot_q = jnp.where(slot_ok, q2[jnp.minimum(slot_p, npairs - 1)], q2[pb_start[blk]][:, None])
    c_first = jnp.logical_and(step_ok, grp == 0).astype(i32)
    c_last = jnp.logical_and(step_ok, grp == nsteps[blk] - 1).astype(i32)
    combine_plan = (blk, slot_q.reshape(-1).astype(i32), slot_ok.reshape(-1).astype(i32), step_ok.astype(i32),
                    c_first, c_last)

    te, tlo = region(jnp.arange(RT, dtype=i32) * TR)
    tvalid = jnp.clip(cnt[te] - tlo, 0, TR)
    last_t = jnp.sum(tiles) - 1
    t_idx = jnp.where(tvalid > 0, jnp.arange(RT, dtype=i32), last_t).astype(i32)
    ffn_plan = (t_idx, te[t_idx].astype(i32), tvalid.astype(i32))
    return pos_cols, pos_rows, gather_plan, combine_plan, ffn_plan, rmax


def _moe_gather_kernel(q_ref, s_ref, valid_ref, first_ref, pos_ref, h_ref, out_ref):
    p = pl.program_id(0)

    @pl.when(valid_ref[p] == 1)
    def _():
        rows, toks = out_ref.shape[0], h_ref.shape[0]
        row = (lax.broadcasted_iota(jnp.int32, (rows, 1), 0) + q_ref[p] * rows).astype(F32)
        hit = jnp.logical_or(pos_ref[0:1, :] == row, pos_ref[1:2, :] == row)
        sel = jnp.where(hit, 1.0, 0.0).astype(BF16)
        @pl.when(first_ref[p] == 1)
        def _():
            out_ref[...] = jnp.zeros_like(out_ref)

        out_ref[...] = out_ref[...] + _dot(sel, h_ref[...]).astype(BF16)


def _moe_gather(h, pos_rows, plan, rmax):
    N, D = h.shape
    pmax = plan[0].shape[0]
    return pl.pallas_call(
        _moe_gather_kernel,
        grid_spec=pltpu.PrefetchScalarGridSpec(
            num_scalar_prefetch=4, grid=(pmax,),
            in_specs=[pl.BlockSpec((SUB, MOE_SB), lambda p, q, s, v, f: (0, s[p])),
                      pl.BlockSpec((MOE_SB, D), lambda p, q, s, v, f: (s[p], 0))],
            out_specs=pl.BlockSpec((MOE_TRG, D), lambda p, q, s, v, f: (q[p], 0))),
        out_shape=jax.ShapeDtypeStruct((rmax, D), BF16),
        compiler_params=_params("arbitrary"),
        name="moe_gather",
    )(*plan, pos_rows, h)


def _moe_ffn_kernel(t_ref, e_ref, nv_ref, x_ref, w1_ref, w3_ref, w2_ref, out_ref, acc_scr, *, nf):
    t = pl.program_id(0)
    f = pl.program_id(1)
    nv = nv_ref[t]

    def block(start, size):
        rows = pl.ds(start, size)

        @pl.when(f == 0)
        def _():
            acc_scr[rows, :] = jnp.zeros((size, acc_scr.shape[1]), F32)

        x = x_ref[rows, :]
        a = _dot(x, w1_ref[...].astype(BF16))
        b = _dot(x, w3_ref[...].astype(BF16))
        acc_scr[rows, :] += _dot((_silu(a) * b).astype(BF16), w2_ref[...].astype(BF16))

        @pl.when(f == nf - 1)
        def _():
            out_ref[rows, :] = acc_scr[rows, :].astype(out_ref.dtype)

    nsub = MOE_TR // MOE_TRG
    used = (nv + MOE_TRG - 1) // MOE_TRG

    @pl.when(used == nsub)
    def _():
        block(0, MOE_TR)

    @pl.when(jnp.logical_and(used > 0, used < nsub))
    def _():
        start = jnp.int32(0)
        size = MOE_TR // 2
        while size >= MOE_TRG:
            has = (used & (size // MOE_TRG)) != 0

            @pl.when(has)
            def _(start=start, size=size):
                block(pl.multiple_of(start, MOE_TRG), size)

            start = start + jnp.where(has, size, 0)
            size //= 2


def _moe_ffn(xs, plan, w1, w3, w2):
    rmax, D = xs.shape
    FF = w1.shape[2]
    TF = 256
    nf = FF // TF
    RT = rmax // MOE_TR

    def fidx(t, f, nv):
        return jnp.where(nv[t] > 0, f, nf - 1)

    return pl.pallas_call(
        functools.partial(_moe_ffn_kernel, nf=nf),
        grid_spec=pltpu.PrefetchScalarGridSpec(
            num_scalar_prefetch=3, grid=(RT, nf),
            in_specs=[pl.BlockSpec((MOE_TR, D), lambda t, f, ti, e, nv: (ti[t], 0)),
                      pl.BlockSpec((None, D, TF), lambda t, f, ti, e, nv: (e[t], 0, fidx(t, f, nv))),
                      pl.BlockSpec((None, D, TF), lambda t, f, ti, e, nv: (e[t], 0, fidx(t, f, nv))),
                      pl.BlockSpec((None, TF, D), lambda t, f, ti, e, nv: (e[t], fidx(t, f, nv), 0))],
            out_specs=pl.BlockSpec((MOE_TR, D), lambda t, f, ti, e, nv: (ti[t], 0)),
            scratch_shapes=[pltpu.VMEM((MOE_TR, D), F32)]),
        out_shape=jax.ShapeDtypeStruct((rmax, D), BF16),
        compiler_params=_params("arbitrary", "arbitrary"),
        name="moe_ffn",
    )(*plan, xs, w1, w3, w2)


def _moe_combine_kernel(s_ref, q_ref, slot_ok_ref, valid_ref, first_ref, last_ref, pos_ref, *refs, ntp):
    ys_refs = refs[:MOE_CG]
    xp_ref, xs_ref, gate_ref, fg_ref, op_ref, os_ref, acc_scr = refs[MOE_CG:]
    p = pl.program_id(0)

    @pl.when(valid_ref[p] == 1)
    def _():
        rows = ys_refs[0].shape[0]
        sels = []
        for k in range(MOE_CG):
            col = (lax.broadcasted_iota(jnp.int32, (1, rows), 1) + q_ref[MOE_CG * p + k] * rows).astype(F32)
            col = jnp.where(slot_ok_ref[MOE_CG * p + k] == 1, col, -1.0)
            sels.append((jnp.where(pos_ref[:, 0:1] == col, pos_ref[:, 2:3], 0.0)
                         + jnp.where(pos_ref[:, 1:2] == col, pos_ref[:, 3:4], 0.0)).astype(BF16))
        sel = jnp.concatenate(sels, axis=1)
        ys = jnp.concatenate([r[...] for r in ys_refs], axis=0)

        @pl.when(first_ref[p] == 1)
        def _():
            acc_scr[...] = jnp.zeros_like(acc_scr)

        acc_scr[...] += _dot(sel, ys)

        @pl.when(last_ref[p] == 1)
        def _():
            s = s_ref[p]
            x = jnp.where(s < ntp, xp_ref[...], xs_ref[...])
            y = x + gate_ref[...] * acc_scr[...]
            out = y * lax.rsqrt(jnp.mean(y * y, axis=-1, keepdims=True) + EPS) * fg_ref[...]

            @pl.when(s < ntp)
            def _():
                op_ref[...] = out

            @pl.when(s >= ntp)
            def _():
                os_ref[...] = out


def _moe_combine(ys, pos_cols, plan, xp, xs, mod, final_g, rows_per_group):
    Np, D = xp.shape
    Ns = xs.shape[0]
    SB = MOE_SB
    ntp = Np // SB
    nsteps = plan[0].shape[0]

    def tile_spec(k):
        return pl.BlockSpec((MOE_TRG, D), lambda p, s, q, *_: (q[MOE_CG * p + k], 0))

    def tok_p(p, s, *_):
        return (jnp.minimum(s[p], ntp - 1), 0)

    def tok_s(p, s, *_):
        return (jnp.maximum(s[p] - ntp, 0), 0)

    def gate_idx(p, s, *_):
        return (jnp.where(s[p] < ntp, 0, 1 + ((s[p] - ntp) * SB) // rows_per_group), 5, 0, 0)

    return pl.pallas_call(
        functools.partial(_moe_combine_kernel, ntp=ntp),
        grid_spec=pltpu.PrefetchScalarGridSpec(
            num_scalar_prefetch=6, grid=(nsteps,),
            in_specs=[pl.BlockSpec((SB, SUB), lambda p, s, q, *_: (s[p], 0))]
            + [tile_spec(k) for k in range(MOE_CG)]
            + [pl.BlockSpec((SB, D), tok_p),
                      pl.BlockSpec((SB, D), tok_s),
                      pl.BlockSpec((None, None, 1, D), gate_idx),
                      pl.BlockSpec((1, D), lambda p, *_: (0, 0))],
            out_specs=[pl.BlockSpec((SB, D), tok_p), pl.BlockSpec((SB, D), tok_s)],
            scratch_shapes=[pltpu.VMEM((SB, D), F32)]),
        out_shape=[jax.ShapeDtypeStruct((Np, D), F32), jax.ShapeDtypeStruct((Ns, D), F32)],
        compiler_params=_params("arbitrary"),
        name="moe_combine",
    )(*plan, pos_cols, *([ys] * MOE_CG), xp, xs, mod, final_g.reshape(1, D))


def _moe(xp, xs, g, mod, router_w, w1, w3, w2, final_g, rows_per_group):
    N = xp.shape[0] + xs.shape[0]
    h, info, info_t, cum = _moe_route(xp, xs, g, mod, router_w, rows_per_group)
    pos_cols, pos_rows, gather_plan, combine_plan, ffn_plan, rmax = _moe_plan(info, info_t, cum, N)
    x_sorted = _moe_gather(h, pos_rows, gather_plan, rmax)
    y_sorted = _moe_ffn(x_sorted, ffn_plan, w1, w3, w2)
    return _moe_combine(y_sorted, pos_cols, combine_plan, xp, xs, mod, final_g, rows_per_group)


def _gla_levels(C):
    lv, c = [], C // 2
    while c >= SUB:
        lv.append(c)
        c //= 2
    return lv


def _gla_tables(C):
    levels = _gla_levels(C)
    nr = 2 + 2 * len(levels)
    mat = np.zeros((2, nr * C, C), np.float32)
    code = np.zeros((2, C, C), np.int32)
    for d in range(2):
        p = np.arange(C) if d == 0 else C - 1 - np.arange(C)
        pi, pj = p[:, None], p[None, :]
        mat[d, 0:C] = pj <= pi
        mat[d, C:2 * C] = pj > pi
        code[d] = np.where((pj <= pi) & (pi // SUB == pj // SUB), 1, 0)
        for lv, c in enumerate(levels):
            blk = pi // c
            later = blk % 2 == 1
            mat[d, (2 + 2 * lv) * C:(3 + 2 * lv) * C] = later & (pj > blk * c - 1) & (pj <= pi)
            mat[d, (3 + 2 * lv) * C:(4 + 2 * lv) * C] = (~later) & (pj > pi) & (pj <= (blk + 1) * c - 1)
            pair = (pi // (2 * c) == pj // (2 * c)) & (pi // c != pj // c) & (pj <= pi)
            code[d] = np.where(pair, 2 + lv, code[d])
    ones = np.zeros((SUB * LANES, C), np.float32)
    for jj in range(SUB):
        ones[jj * LANES:(jj + 1) * LANES, jj::SUB] = 1.0
    return jnp.asarray(mat, dtype=BF16), jnp.asarray(code), jnp.asarray(ones, dtype=BF16)


def _bcast_sublane(x, jj):
    r, w = x.shape
    x3 = x.reshape(r // SUB, SUB, w)
    return jnp.broadcast_to(x3[:, jj:jj + 1, :], x3.shape).reshape(r, w)


def _t128(x):
    r, w = x.shape
    if w > LANES:
        return jnp.concatenate([x[:, i:i + LANES].T for i in range(0, w, LANES)], axis=0)
    return jnp.concatenate([x[i:i + LANES, :].T for i in range(0, r, LANES)], axis=1)


def _gla_kernel(q_ref, k_ref, v_ref, g_ref, lr_ref, wg_ref, ba_ref, mat_ref, code_ref, ones_ref,
                s0f_ref, s0b_ref, ng_ref, o_ref, sf_ref, sb_ref, st_scr, of_scr, *, n, C, G):
    d = pl.program_id(1)
    c = pl.program_id(2)
    levels = _gla_levels(C)

    @pl.when(jnp.logical_and(c == 0, d == 0))
    def _():
        for bb in range(G):
            for h in range(C_H):
                st_scr[bb, h] = _t128(s0f_ref[bb, h])

    @pl.when(jnp.logical_and(c == 0, d == 1))
    def _():
        for bb in range(G):
            for h in range(C_H):
                st_scr[bb, h] = _t128(s0b_ref[bb, h])

    mat = mat_ref[...]
    code = code_ref[...]
    ones = ones_ref[...]
    cums = []
    for bb in range(G):
        xg = jnp.dot(lr_ref[bb], wg_ref[...], precision=HIGHEST, preferred_element_type=F32) + ba_ref[...]
        la = (jnp.minimum(xg, 0.0) - jnp.log1p(jnp.exp(-jnp.abs(xg)))) * (LOG2E / C_TAU)
        hi = la.astype(BF16)
        lo = (la - hi.astype(F32)).astype(BF16)
        cums.append(_dot(mat, hi) + _dot(mat, lo))

    def prepare(bb, h):
        cum = cums[bb]
        ks = slice(h * C_DK, (h + 1) * C_DK)
        qh = q_ref[bb, :, ks].astype(F32) * (C_DK ** -0.5)
        kh = k_ref[bb, :, ks].astype(F32)
        b = cum[0:C, ks]
        b_rest = cum[C:2 * C, ks]
        ps = []
        for jj in range(SUB):
            dec = jnp.exp2(jnp.minimum(b - _bcast_sublane(b, jj), 0.0))
            ps.append((qh * _bcast_sublane(kh, jj) * dec).astype(BF16))
        lv_ops = []
        for lv in range(len(levels)):
            eq = cum[(2 + 2 * lv) * C:(3 + 2 * lv) * C, ks]
            ek = cum[(3 + 2 * lv) * C:(4 + 2 * lv) * C, ks]
            lv_ops.append(((qh * jnp.exp2(eq)).astype(BF16), (kh * jnp.exp2(ek)).astype(BF16)))
        qe = (qh * jnp.exp2(b)).astype(BF16)
        ke = (kh * jnp.exp2(b_rest)).astype(BF16)
        e_end = jnp.exp2(b[0:1, :] + b_rest[0:1, :])
        return jnp.concatenate(ps, axis=1), lv_ops, qe, ke, e_end

    def contract(bb, h, prep):
        pcat, lv_ops, qe, ke, e_end = prep
        vh = v_ref[bb, :, h * C_DV:(h + 1) * C_DV].astype(F32)
        att = jnp.where(code == 1, _dot(pcat, ones), 0.0)
        for lv, (qs, ks_) in enumerate(lv_ops):
            att = jnp.where(code == 2 + lv, _dot_nt(qs, ks_), att)
        st = st_scr[bb, h]
        o = _dot(att.astype(BF16), vh.astype(BF16)) + _dot_nt(qe, st.astype(BF16))
        st_scr[bb, h] = e_end * st + _dot(_t128(vh).astype(BF16), ke)
        return o

    units = [(bb, h) for h in range(C_H) for bb in range(G)]
    outs = {}
    prep = prepare(*units[0])
    for idx, (bb, h) in enumerate(units):
        nxt = prepare(*units[idx + 1]) if idx + 1 < len(units) else None
        outs[(bb, h)] = contract(bb, h, prep)
        prep = nxt
    o_all = [jnp.concatenate([outs[(bb, h)] for h in range(C_H)], axis=-1) for bb in range(G)]

    @pl.when(d == 0)
    def _():
        for bb in range(G):
            of_scr[bb, c] = o_all[bb]

    @pl.when(d == 1)
    def _():
        for bb in range(G):
            tot = o_all[bb] + of_scr[bb, n - 1 - c]
            res = []
            for h in range(C_H):
                sl = slice(h * C_DV, (h + 1) * C_DV)
                t = tot[:, sl]
                y = t * lax.rsqrt(jnp.mean(t * t, axis=-1, keepdims=True) + EPS) * ng_ref[:, sl]
                res.append(y * _silu(g_ref[bb, :, sl].astype(F32)))
            o_ref[bb] = jnp.concatenate(res, axis=-1).astype(o_ref.dtype)

    @pl.when(jnp.logical_and(c == n - 1, d == 0))
    def _():
        for bb in range(G):
            for h in range(C_H):
                sf_ref[bb, h] = _t128(st_scr[bb, h])

    @pl.when(jnp.logical_and(c == n - 1, d == 1))
    def _():
        for bb in range(G):
            for h in range(C_H):
                sb_ref[bb, h] = _t128(st_scr[bb, h])


def _gla(z, zg, B, T, w_a2, b_a, s0f, s0b, norm_g):
    C = GLA_CHUNK
    G = 2
    assert B % G == 0 and T % C == 0
    n = T // C
    HK = C_H * C_DK
    HV = C_H * C_DV
    mat, code, ones = _gla_tables(C)
    nr = mat.shape[1] // C
    wg = jnp.zeros((2, LANES, HK), F32)
    for dr in range(2):
        wg = wg.at[dr, dr * C_RANK:(dr + 1) * C_RANK, :].set(w_a2[dr])
    z3 = z.reshape(B, T, z.shape[1])
    zg3 = zg.reshape(B, T, zg.shape[1])

    def chunk(d, c):
        return c + d * (n - 1 - 2 * c)

    st_spec = pl.BlockSpec((G, C_H, C_DK, C_DV), lambda b, d, c: (b, 0, 0, 0))
    st_shape = jax.ShapeDtypeStruct((B, C_H, C_DK, C_DV), F32)
    o, sf, sb = pl.pallas_call(
        functools.partial(_gla_kernel, n=n, C=C, G=G),
        grid=(B // G, 2, n),
        in_specs=[pl.BlockSpec((G, C, HK), lambda b, d, c: (b, chunk(d, c), 0)),
                  pl.BlockSpec((G, C, HK), lambda b, d, c: (b, chunk(d, c), 1)),
                  pl.BlockSpec((G, C, HV), lambda b, d, c: (b, chunk(d, c), 1)),
                  pl.BlockSpec((G, C, HV), lambda b, d, c: (b, chunk(d, c), 2)),
                  pl.BlockSpec((G, C, LANES), lambda b, d, c: (b, chunk(d, c), 0)),
                  pl.BlockSpec((None, LANES, HK), lambda b, d, c: (d, 0, 0)),
                  pl.BlockSpec((None, 1, HK), lambda b, d, c: (d, 0, 0)),
                  pl.BlockSpec((None, nr * C, C), lambda b, d, c: (d, 0, 0)),
                  pl.BlockSpec((None, C, C), lambda b, d, c: (d, 0, 0)),
                  pl.BlockSpec((SUB * LANES, C), lambda b, d, c: (0, 0)),
                  st_spec, st_spec,
                  pl.BlockSpec((1, HV), lambda b, d, c: (0, 0))],
        out_specs=[pl.BlockSpec((G, C, HV), lambda b, d, c: (b, (n - 1) - d * c, 0)),
                   st_spec, st_spec],
        out_shape=[jax.ShapeDtypeStruct((B, T, HV), BF16), st_shape, st_shape],
        scratch_shapes=[pltpu.VMEM((G, C_H, C_DV, C_DK), F32), pltpu.VMEM((G, n, C, HV), F32)],
        compiler_params=_params("arbitrary", "arbitrary", "arbitrary"),
        name="gla",
    )(z3, z3, z3, z3, zg3, wg, b_a.reshape(2, 1, HK), mat, code, ones, s0f, s0b, norm_g.reshape(1, HV))
    return o.reshape(B * T, HV), sf, sb


def _run_stream(x, B, T, mods, ctx, p):
    N, D = x.shape
    rpg = N // mods[0].shape[0]
    TM = min(2048, rpg)
    nb = (B_H + 2 * B_HKV) * B_HD

    w_in = p['even_w_in'][0]
    z, zb = _norm_mm(x, p['norm1_g'][0], mods[0], (0, 1), w_in, w_in, (nb, MIX_MAIN // nb), TM, rpg)
    if ctx is None:
        s0 = jnp.zeros((B, A_H, A_DK, A_DV), F32)
        a_f0, a_b0, cache_k, cache_v = s0, s0, None, None
    else:
        cache_k, cache_v, a_f0, a_b0 = ctx[0], ctx[1], ctx[2], ctx[3]
    o_a, a_sf, a_sb = _retention(z, B, T, p['a_log_gamma'][0], a_f0, a_b0, p['a_norm_g'][0])
    qpad, k_norm, k_rot, v_bf = _bprep(zb, T, p['b_q_g'][0], p['b_k_g'][0], rope=ctx is not None)
    o_b = _attention(qpad, k_rot, v_bf, B, T, cache_k, cache_v)
    x = _proj_res(x, mods[0], 2, [o_a, o_b], p['even_w_out'][0], rpg)
    x = _ffn(x, p['norm2_g'][0], mods[0], p['ff_w1'][0], p['ff_w3'][0], p['ff_w2'][0], rpg)

    w_in = p['odd_w_in'][0]
    w_gate = jnp.pad(w_in[:, MIX_MAIN:], ((0, 0), (0, LANES - 2 * C_RANK)))
    z1, z1g = _norm_mm(x, p['norm1_g'][1], mods[1], (0, 1), w_in, w_gate, (LANES, 0), TM, rpg)
    if ctx is None:
        s0 = jnp.zeros((B, C_H, C_DK, C_DV), F32)
        c_f0, c_b0 = s0, s0
    else:
        c_f0, c_b0 = ctx[4], ctx[5]
    o_c, c_sf, c_sb = _gla(z1, z1g, B, T, p['c_w_a2'][0], p['c_b_a'][0], c_f0, c_b0, p['c_norm_g'][0])
    x = _proj_res(x, mods[1], 2, [o_c], p['odd_w_out'][0], rpg)
    v_raw = zb[:, (B_H + B_HKV) * B_HD:]
    return x, (k_norm, v_raw, a_sf, a_sb, c_sf, c_sb)


def kernel(x_prompt, x_sample, c, cache_b_k, cache_b_v, state_a_fwd, state_a_bwd, state_c_fwd, state_c_bwd,
           c_ctx, w_mod, b_mod, norm1_g, norm2_g, final_g, even_w_in, even_w_out, a_log_gamma, a_norm_g,
           b_q_g, b_k_g, odd_w_in, c_w_a2, c_b_a, c_norm_g, odd_w_out, ff_w1, ff_w3, ff_w2,
           router_w, moe_w1, moe_w3, moe_w2):
    Bp, Tp, D = x_prompt.shape
    Bs, Ts, _ = x_sample.shape
    L = w_mod.shape[0]
    assert L == 2 and even_w_in.shape[0] == 1 and odd_w_in.shape[0] == 1
    p = dict(norm1_g=norm1_g, norm2_g=norm2_g, final_g=final_g, even_w_in=even_w_in, even_w_out=even_w_out,
             a_log_gamma=a_log_gamma, a_norm_g=a_norm_g, b_q_g=b_q_g, b_k_g=b_k_g, odd_w_in=odd_w_in,
             c_w_a2=c_w_a2, c_b_a=c_b_a, c_norm_g=c_norm_g, odd_w_out=odd_w_out, ff_w1=ff_w1, ff_w3=ff_w3,
             ff_w2=ff_w2, router_w=router_w, moe_w1=moe_w1, moe_w3=moe_w3, moe_w2=moe_w2)

    rows = 8
    conds = jnp.concatenate([c_ctx[None, :], c, jnp.zeros((rows - 1 - Bs, D), F32)], axis=0)
    mod = _modulation(conds, w_mod, b_mod).reshape(L, rows, 6, 1, D)
    mods_p = [mod[l, 0:1] for l in range(L)]
    mods_s = [mod[l, 1:1 + Bs] for l in range(L)]

    x_p, kept = _run_stream(x_prompt.reshape(Bp * Tp, D), Bp, Tp, mods_p, None, p)
    nk = B_HKV * B_HD
    ctx = (cache_b_k[:, 0].reshape(Bs, -1, nk), cache_b_v[:, 0].reshape(Bs, -1, nk),
           state_a_fwd[:, 0], state_a_bwd[:, 0], state_c_fwd[:, 0], state_c_bwd[:, 0])
    x_s, _ = _run_stream(x_sample.reshape(Bs * Ts, D), Bs, Ts, mods_s, ctx, p)
    y_p, y_s = _moe(x_p, x_s, norm2_g[1], mod[1, 0:1 + Bs], router_w[0], moe_w1[0], moe_w3[0], moe_w2[0],
                    final_g, Ts)

    k_norm, v_raw, a_sf, a_sb, c_sf, c_sb = kept
    return (y_p.reshape(Bp, Tp, D), y_s.reshape(Bs, Ts, D),
            k_norm.reshape(Bp, 1, Tp, B_HKV, B_HD), v_raw.reshape(Bp, 1, Tp, B_HKV, B_HD),
            a_sf[:, None], a_sb[:, None], c_sf[:, None], c_sb[:, None])
```

```python
import functools

import numpy as np
import jax
import jax.numpy as jnp
from jax import lax
from jax.experimental import pallas as pl
from jax.experimental.pallas import tpu as pltpu

F32 = jnp.float32
BF16 = jnp.bfloat16
EPS = 1e-6
HIGHEST = lax.Precision.HIGHEST
LOG2E = 1.4426950408889634

VMEM_LIMIT_BYTES = 56 * 1024 * 1024

A_H, A_DK, A_DV = 4, 128, 256
B_H, B_HKV, B_HD = 8, 2, 64
C_H, C_DK, C_DV, C_RANK = 4, 128, 256, 16
C_TAU = 16.0
GRID_W = 64
ROPE_THETA = 10000.0
N_EXPERTS = 8
LANES = 128
SUB = 8
RET_CHUNK = 128
GLA_CHUNK = 128
Q_TILE = 128


def _params(*sem):
    return pltpu.CompilerParams(dimension_semantics=sem, vmem_limit_bytes=VMEM_LIMIT_BYTES)


def _dot(a, b):
    return jnp.dot(a, b, preferred_element_type=F32)


def _dot_nt(a, b):
    return lax.dot_general(a, b, (((1,), (1,)), ((), ())), preferred_element_type=F32)


def _silu(x):
    return x * jax.nn.sigmoid(x)


def _norm_mod(x, g, sh, sc):
    r = lax.rsqrt(jnp.mean(x * x, axis=-1, keepdims=True) + EPS)
    return (x * r * g) * (1.0 + sc) + sh


def _mod_kernel(c_ref, w_ref, b_ref, o_ref):
    c = c_ref[...]
    o_ref[...] = jnp.dot(_silu(c), w_ref[...], precision=HIGHEST, preferred_element_type=F32) + b_ref[...]


def _modulation(conds, w_mod, b_mod):
    L, D, D6 = w_mod.shape
    R = conds.shape[0]
    TN = 1024
    return pl.pallas_call(
        _mod_kernel,
        grid=(L, D6 // TN),
        in_specs=[pl.BlockSpec((R, D), lambda l, j: (0, 0)),
                  pl.BlockSpec((None, D, TN), lambda l, j: (l, 0, j)),
                  pl.BlockSpec((None, 1, TN), lambda l, j: (l, 0, j))],
        out_specs=pl.BlockSpec((None, R, TN), lambda l, j: (l, 0, j)),
        out_shape=jax.ShapeDtypeStruct((L, R, D6), F32),
        compiler_params=_params("arbitrary", "arbitrary"),
        name="modulation",
    )(conds, w_mod, b_mod.reshape(L, 1, D6))


def _mod_spec(part, D, TM, rows_per_group, axis):
    def idx(*g):
        return ((g[axis] * TM) // rows_per_group, part, 0, 0)
    return pl.BlockSpec((None, None, 1, D), idx)


MIX_MAIN = A_H * (2 * A_DK + 2 * A_DV)
MIX_TN = 768


def _norm_mm_kernel(x_ref, g_ref, sh_ref, sc_ref, w_ref, we_ref, o_ref, oe_ref, h_scr, *, nmain):
    j = pl.program_id(1)

    @pl.when(j == 0)
    def _():
        h_scr[...] = _norm_mod(x_ref[...], g_ref[...], sh_ref[...], sc_ref[...]).astype(BF16)

    @pl.when(j < nmain)
    def _():
        o_ref[...] = _dot(h_scr[...], w_ref[...].astype(BF16)).astype(o_ref.dtype)

    @pl.when(j == nmain)
    def _():
        oe_ref[...] = _dot(h_scr[...], we_ref[...].astype(BF16))


def _norm_mm(x, g, mod, parts, w, w_extra, extra_block, TM, rows_per_group):
    N, D = x.shape
    nmain = MIX_MAIN // MIX_TN
    WE = extra_block[0]
    return pl.pallas_call(
        functools.partial(_norm_mm_kernel, nmain=nmain),
        grid=(N // TM, nmain + 1),
        in_specs=[pl.BlockSpec((TM, D), lambda i, j: (i, 0)),
                  pl.BlockSpec((1, D), lambda i, j: (0, 0)),
                  _mod_spec(parts[0], D, TM, rows_per_group, 0),
                  _mod_spec(parts[1], D, TM, rows_per_group, 0),
                  pl.BlockSpec((D, MIX_TN), lambda i, j: (0, jnp.minimum(j, nmain - 1))),
                  pl.BlockSpec((D, WE), lambda i, j: (0, extra_block[1]))],
        out_specs=[pl.BlockSpec((TM, MIX_TN), lambda i, j: (i, jnp.minimum(j, nmain - 1))),
                   pl.BlockSpec((TM, WE), lambda i, j: (i, 0))],
        out_shape=[jax.ShapeDtypeStruct((N, MIX_MAIN), BF16), jax.ShapeDtypeStruct((N, WE), F32)],
        scratch_shapes=[pltpu.VMEM((TM, D), BF16)],
        compiler_params=_params("arbitrary", "arbitrary"),
        name="norm_mm",
    )(x, g.reshape(1, D), mod, mod, w, w_extra)


def _ret_kernel(lg_ref, q_ref, k_ref, v_ref, ag_ref, s0f_ref, s0b_ref, ng_ref,
                o_ref, sf_ref, sb_ref, s_scr, of_scr, *, n, C):
    d = pl.program_id(1)
    c = pl.program_id(2)

    @pl.when(jnp.logical_and(c == 0, d == 0))
    def _():
        s_scr[...] = s0f_ref[...]

    @pl.when(jnp.logical_and(c == 0, d == 1))
    def _():
        s_scr[...] = s0b_ref[...]

    df = d.astype(F32)
    sgn = 1.0 - 2.0 * df
    ii = lax.broadcasted_iota(jnp.int32, (C, C), 0).astype(F32)
    jj = lax.broadcasted_iota(jnp.int32, (C, C), 1).astype(F32)
    dd = (ii - jj) * sgn
    feeds = dd >= 0.0
    ddc = jnp.maximum(dd, 0.0)
    ri = lax.broadcasted_iota(jnp.int32, (C, 1), 0).astype(F32)
    pos_q = (ri + 1.0) + df * (C - 2.0 * ri - 1.0)
    pos_k = (C - 1.0 - ri) + df * (2.0 * ri - C + 1.0)
    chunk_len = jnp.full((1, A_DV), float(C), F32)

    outs = []
    for h in range(A_H):
        lg = lg_ref[d, h]
        dmask = jnp.where(feeds, jnp.exp2(lg * ddc), 0.0)
        qh = q_ref[:, h * A_DK:(h + 1) * A_DK].astype(F32) * (A_DK ** -0.5)
        kh = k_ref[:, h * A_DK:(h + 1) * A_DK].astype(F32)
        vh = v_ref[:, h * A_DV:(h + 1) * A_DV].astype(BF16)
        s = s_scr[h]
        att = _dot_nt(qh.astype(BF16), kh.astype(BF16)) * dmask
        o = _dot(att.astype(BF16), vh) + _dot((qh * jnp.exp2(lg * pos_q)).astype(BF16), s.astype(BF16))
        kd = kh * jnp.exp2(lg * pos_k)
        s_scr[h] = jnp.exp2(lg * chunk_len) * s + _dot(kd.T.astype(BF16), vh)
        outs.append(o)
    o_all = jnp.concatenate(outs, axis=-1)

    @pl.when(d == 0)
    def _():
        of_scr[c] = o_all

    @pl.when(d == 1)
    def _():
        tot = o_all + of_scr[n - 1 - c]
        res = []
        for h in range(A_H):
            sl = slice(h * A_DV, (h + 1) * A_DV)
            t = tot[:, sl]
            dev = t - jnp.mean(t, axis=-1, keepdims=True)
            y = dev * lax.rsqrt(jnp.mean(dev * dev, axis=-1, keepdims=True) + EPS) * ng_ref[:, sl]
            res.append(y * _silu(ag_ref[:, sl].astype(F32)))
        o_ref[...] = jnp.concatenate(res, axis=-1).astype(o_ref.dtype)

    @pl.when(jnp.logical_and(c == n - 1, d == 0))
    def _():
        sf_ref[...] = s_scr[...]

    @pl.when(jnp.logical_and(c == n - 1, d == 1))
    def _():
        sb_ref[...] = s_scr[...]


def _retention(z, B, T, log_gamma, s0f, s0b, norm_g):
    C = RET_CHUNK
    n = T // C
    HV = A_H * A_DV

    def row(b, d, c):
        return b * n + c + d * (n - 1 - 2 * c)

    st_spec = pl.BlockSpec((None, A_H, A_DK, A_DV), lambda b, d, c: (b, 0, 0, 0))
    st_shape = jax.ShapeDtypeStruct((B, A_H, A_DK, A_DV), F32)
    return pl.pallas_call(
        functools.partial(_ret_kernel, n=n, C=C),
        grid=(B, 2, n),
        in_specs=[pl.BlockSpec(memory_space=pltpu.SMEM),
                  pl.BlockSpec((C, 512), lambda b, d, c: (row(b, d, c), 0)),
                  pl.BlockSpec((C, 512), lambda b, d, c: (row(b, d, c), 1)),
                  pl.BlockSpec((C, HV), lambda b, d, c: (row(b, d, c), 1)),
                  pl.BlockSpec((C, HV), lambda b, d, c: (row(b, d, c), 2)),
                  st_spec, st_spec,
                  pl.BlockSpec((1, HV), lambda b, d, c: (0, 0))],
        out_specs=[pl.BlockSpec((C, HV), lambda b, d, c: (b * n + (n - 1) - d * c, 0)),
                   st_spec, st_spec],
        out_shape=[jax.ShapeDtypeStruct((B * T, HV), BF16), st_shape, st_shape],
        scratch_shapes=[pltpu.VMEM((A_H, A_DK, A_DV), F32), pltpu.VMEM((n, C, HV), F32)],
        compiler_params=_params("arbitrary", "arbitrary", "arbitrary"),
        name="retention",
    )(log_gamma * LOG2E, z, z, z, z, s0f, s0b, norm_g.reshape(1, HV))


def _group_sum_matrix(width, group):
    i = np.arange(width)
    return jnp.asarray((i[:, None] // group == i[None, :] // group).astype(np.float32), dtype=BF16)


def _q_pad_matrix():
    m = np.zeros((B_H * B_HD, B_H * LANES), np.float32)
    g = B_H // B_HKV
    for h in range(B_H):
        for t in range(B_HD):
            m[h * B_HD + t, h * LANES + (h // g) * B_HD + t] = 1.0
    return jnp.asarray(m, dtype=BF16)


def _rope_tables(T):
    rows = T // GRID_W
    row = np.repeat(np.arange(rows, dtype=np.float64), GRID_W)
    col = np.tile(np.arange(GRID_W, dtype=np.float64), rows)
    nq = B_HD // 4
    inv = ROPE_THETA ** (-np.arange(nq, dtype=np.float64) / nq)
    ang = np.concatenate([row[:, None] * inv, col[:, None] * inv], axis=-1)
    cos = np.repeat(np.cos(ang), 2, axis=-1)
    sin = np.repeat(np.sin(ang), 2, axis=-1)
    sign = np.tile(np.array([-1.0, 1.0]), B_HD // 2)
    reps = LANES // B_HD
    return (jnp.asarray(np.tile(cos, (1, reps)), dtype=F32),
            jnp.asarray(np.tile(sin * sign, (1, reps)), dtype=F32))


def _group_rmsnorm(x, gsum, g):
    x2 = x * x
    hi = x2.astype(BF16)
    lo = (x2 - hi.astype(F32)).astype(BF16)
    ss = _dot(hi, gsum) + _dot(lo, gsum)
    return x * lax.rsqrt(ss * (1.0 / B_HD) + EPS) * g


def _rotate_pairs(x, cos, sin_signed):
    n = x.shape[1]
    lane = lax.broadcasted_iota(jnp.int32, x.shape, 1)
    partner = jnp.where(lane % 2 == 0, pltpu.roll(x, n - 1, 1), pltpu.roll(x, 1, 1))
    reps = n // LANES
    if reps > 1:
        cos = jnp.concatenate([cos] * reps, axis=1)
        sin_signed = jnp.concatenate([sin_signed] * reps, axis=1)
    return x * cos + partner * sin_signed


def _bprep_kernel(z_ref, qg_ref, kg_ref, cos_ref, sin_ref, gq_ref, gk_ref, pad_ref,
                  qpad_ref, kn_ref, kr_ref, vb_ref, *, rope):
    nq = B_H * B_HD
    nk = B_HKV * B_HD
    qn = _group_rmsnorm(z_ref[:, 0:nq], gq_ref[...], qg_ref[...])
    kn = _group_rmsnorm(z_ref[:, nq:nq + nk], gk_ref[...], kg_ref[...])
    kn_ref[...] = kn
    if rope:
        qn = _rotate_pairs(qn, cos_ref[...], sin_ref[...])
        kn = _rotate_pairs(kn, cos_ref[...], sin_ref[...])
    kr_ref[...] = kn.astype(BF16)
    vb_ref[...] = z_ref[:, nq + nk:nq + 2 * nk].astype(BF16)
    qs = (qn * (B_HD ** -0.5 * LOG2E)).astype(BF16)
    qpad_ref[...] = _dot(qs, pad_ref[...]).astype(BF16)


def _bprep(z, T, q_g, k_g, rope):
    N = z.shape[0]
    TM = min(512, T)
    nq = B_H * B_HD
    nk = B_HKV * B_HD
    width = nq + 2 * nk
    assert z.shape[1] == width
    cos, sin = _rope_tables(T if rope else TM)
    nt = T // TM if rope else 1
    const = lambda i: (0, 0)
    return pl.pallas_call(
        functools.partial(_bprep_kernel, rope=rope),
        grid=(N // TM,),
        in_specs=[pl.BlockSpec((TM, width), lambda i: (i, 0)),
                  pl.BlockSpec((1, nq), const),
                  pl.BlockSpec((1, nk), const),
                  pl.BlockSpec((TM, LANES), lambda i: (i % nt, 0)),
                  pl.BlockSpec((TM, LANES), lambda i: (i % nt, 0)),
                  pl.BlockSpec((nq, nq), const),
                  pl.BlockSpec((nk, nk), const),
                  pl.BlockSpec((nq, B_H * LANES), const)],
        out_specs=[pl.BlockSpec((TM, B_H * LANES), lambda i: (i, 0)),
                   pl.BlockSpec((TM, nk), lambda i: (i, 0)),
                   pl.BlockSpec((TM, nk), lambda i: (i, 0)),
                   pl.BlockSpec((TM, nk), lambda i: (i, 0))],
        out_shape=[jax.ShapeDtypeStruct((N, B_H * LANES), BF16),
                   jax.ShapeDtypeStruct((N, nk), F32),
                   jax.ShapeDtypeStruct((N, nk), BF16),
                   jax.ShapeDtypeStruct((N, nk), BF16)],
        compiler_params=_params("arbitrary"),
        name="attn_prep",
    )(z, jnp.tile(q_g, B_H).reshape(1, nq), jnp.tile(k_g, B_HKV).reshape(1, nk), cos, sin,
      _group_sum_matrix(nq, B_HD), _group_sum_matrix(nk, B_HD), _q_pad_matrix())


def _lane_fold(x, op):
    acc = x[:, 0:LANES]
    for j in range(1, x.shape[1] // LANES):
        acc = op(acc, x[:, j * LANES:(j + 1) * LANES])
    return acc


def _attn_kernel(*refs, has_cache, kc, nq):
    if has_cache:
        q_ref, k_ref, v_ref, ck_ref, cv_ref, o_ref, s_scr, m_scr, mprev_scr, l_scr, acc_scr = refs
        kcc = min(kc, ck_ref.shape[0])
        ncache = ck_ref.shape[0] // kcc
    else:
        q_ref, k_ref, v_ref, o_ref, s_scr, m_scr, mprev_scr, l_scr, acc_scr = refs
        kcc, ncache = kc, 0
    i = pl.program_id(1)
    tq = q_ref.shape[0]
    nlat = k_ref.shape[0] // kc

    def score(c, kblk):
        q = jnp.concatenate([q_ref[:, h * LANES:(h + 1) * LANES] for h in range(B_H)], axis=0)
        s = _dot_nt(q, kblk)
        s_scr[c, :, 0:kblk.shape[0]] = s
        m_scr[...] = jnp.maximum(m_scr[...], _lane_fold(s, jnp.maximum))

    def weight(c, vblk):
        s = s_scr[c, :, 0:vblk.shape[0]]
        mp = mprev_scr[...]
        ps = [jnp.exp2(s[:, j * LANES:(j + 1) * LANES] - mp) for j in range(vblk.shape[0] // LANES)]
        tot = ps[0]
        for pj in ps[1:]:
            tot = tot + pj
        l_scr[...] += tot
        acc_scr[...] += _dot(jnp.concatenate(ps, axis=1).astype(BF16), vblk)

    def run(do_weight, do_score):
        def unit(c, kblk, vblk):
            if do_weight:
                weight(c, vblk())
            if do_score:
                score(c, kblk())

        for c in range(ncache):
            unit(c, lambda: ck_ref[c * kcc:(c + 1) * kcc, :].astype(BF16),
                 lambda: cv_ref[c * kcc:(c + 1) * kcc, :].astype(BF16))

        def body(c, carry):
            rows = pl.ds(pl.multiple_of(c * kc, kc), kc)
            unit(ncache + c, lambda: k_ref[rows, :], lambda: v_ref[rows, :])
            return carry
        lax.fori_loop(0, nlat, body, 0)

    @pl.when(i < nq)
    def _():
        m_scr[...] = jnp.full(m_scr.shape, -jnp.inf, F32)

    @pl.when(i > 0)
    def _():
        l_scr[...] = jnp.zeros_like(l_scr)
        acc_scr[...] = jnp.zeros_like(acc_scr)

    @pl.when(i == 0)
    def _():
        run(False, True)

    @pl.when(jnp.logical_and(i > 0, i < nq))
    def _():
        run(True, True)

    @pl.when(i == nq)
    def _():
        run(True, False)

    @pl.when(i > 0)
    def _():
        r_all = acc_scr[...] / jnp.sum(l_scr[...], axis=-1, keepdims=True)
        g = B_H // B_HKV
        lane = lax.broadcasted_iota(jnp.int32, (tq, LANES), 1)
        outs = []
        for j in range(B_H // 2):
            pair = []
            for half in range(2):
                h = 2 * j + half
                r = r_all[h * tq:(h + 1) * tq, :]
                if h // g != half:
                    r = pltpu.roll(r, B_HD, 1)
                pair.append(r)
            outs.append(jnp.where(lane < B_HD, pair[0], pair[1]))
        o_ref[...] = jnp.concatenate(outs, axis=-1).astype(o_ref.dtype)

    @pl.when(i < nq)
    def _():
        mprev_scr[...] = jnp.broadcast_to(jnp.max(m_scr[...], axis=-1, keepdims=True), mprev_scr.shape)


def _attention(qpad, kr, vb, B, T, cache_k, cache_v):
    has_cache = cache_k is not None
    TQ = Q_TILE
    nq = T // TQ
    nk = B_HKV * B_HD
    in_specs = [pl.BlockSpec((TQ, B_H * LANES), lambda b, i: (b * nq + jnp.minimum(i, nq - 1), 0)),
                pl.BlockSpec((T, nk), lambda b, i: (b, 0)),
                pl.BlockSpec((T, nk), lambda b, i: (b, 0))]
    args = [qpad, kr, vb]
    kc = min(1024, T)
    nchunks = T // kc
    if has_cache:
        P = cache_k.shape[1]
        assert P % min(kc, P) == 0
        nchunks += P // min(kc, P)
        in_specs += [pl.BlockSpec((None, P, nk), lambda b, i: (b, 0, 0))] * 2
        args += [cache_k, cache_v]
    R = B_H * TQ
    return pl.pallas_call(
        functools.partial(_attn_kernel, has_cache=has_cache, kc=kc, nq=nq),
        grid=(B, nq + 1),
        in_specs=in_specs,
        out_specs=pl.BlockSpec((TQ, B_H * B_HD), lambda b, i: (b * nq + jnp.maximum(i - 1, 0), 0)),
        out_shape=jax.ShapeDtypeStruct((B * T, B_H * B_HD), BF16),
        scratch_shapes=[pltpu.VMEM((nchunks, R, kc), F32)] + [pltpu.VMEM((R, LANES), F32)] * 4,
        compiler_params=_params("arbitrary", "arbitrary"),
        name="attention",
    )(*args)


def _proj_res_kernel(*refs, n_in):
    x_ref, gate_ref = refs[0], refs[1]
    o_refs = refs[2:2 + n_in]
    w_refs = refs[2 + n_in:2 + 2 * n_in]
    out_ref = refs[2 + 2 * n_in]
    wbf_refs = refs[3 + 2 * n_in:]

    @pl.when(pl.program_id(0) == 0)
    def _():
        for w_ref, wbf_ref in zip(w_refs, wbf_refs):
            wbf_ref[...] = w_ref[...].astype(BF16)

    acc = _dot(o_refs[0][...], wbf_refs[0][...])
    for o_ref, wbf_ref in zip(o_refs[1:], wbf_refs[1:]):
        acc = acc + _dot(o_ref[...], wbf_ref[...])
    out_ref[...] = x_ref[...] + gate_ref[...] * acc


def _proj_res(x, mod, part, acts, w, rows_per_group):
    N, D = x.shape
    TM = min(512, rows_per_group)
    n_in = len(acts)
    widths = [a.shape[1] for a in acts]
    offs = np.cumsum([0] + widths[:-1]).tolist()
    in_specs = [pl.BlockSpec((TM, D), lambda i: (i, 0)),
                _mod_spec(part, D, TM, rows_per_group, 0)]
    in_specs += [pl.BlockSpec((TM, wd), lambda i: (i, 0)) for wd in widths]
    in_specs += [pl.BlockSpec((wd, D), functools.partial(lambda i, blk: (blk, 0), blk=off // wd))
                 for wd, off in zip(widths, offs)]
    return pl.pallas_call(
        functools.partial(_proj_res_kernel, n_in=n_in),
        grid=(N // TM,),
        in_specs=in_specs,
        out_specs=pl.BlockSpec((TM, D), lambda i: (i, 0)),
        out_shape=jax.ShapeDtypeStruct((N, D), F32),
        scratch_shapes=[pltpu.VMEM((wd, D), BF16) for wd in widths],
        compiler_params=_params("arbitrary"),
        name="proj_residual",
    )(x, mod, *acts, *([w] * n_in))


def _ffn_kernel(x_ref, g_ref, sh_ref, sc_ref, gate_ref, w1_ref, w3_ref, w2_ref, out_ref, h_scr, acc_scr, *, nf):
    f = pl.program_id(1)

    @pl.when(f == 0)
    def _():
        h_scr[...] = _norm_mod(x_ref[...], g_ref[...], sh_ref[...], sc_ref[...]).astype(BF16)
        acc_scr[...] = jnp.zeros_like(acc_scr)

    h = h_scr[...]
    a = _dot(h, w1_ref[...].astype(BF16))
    b = _dot(h, w3_ref[...].astype(BF16))
    acc_scr[...] += _dot((_silu(a) * b).astype(BF16), w2_ref[...].astype(BF16))

    @pl.when(f == nf - 1)
    def _():
        out_ref[...] = x_ref[...] + gate_ref[...] * acc_scr[...]


def _ffn(x, g, mod, w1, w3, w2, rows_per_group):
    N, D = x.shape
    FF = w1.shape[1]
    TM, TF = min(1024, rows_per_group), 256
    nf = FF // TF
    return pl.pallas_call(
        functools.partial(_ffn_kernel, nf=nf),
        grid=(N // TM, nf),
        in_specs=[pl.BlockSpec((TM, D), lambda i, f: (i, 0)),
                  pl.BlockSpec((1, D), lambda i, f: (0, 0)),
                  _mod_spec(3, D, TM, rows_per_group, 0),
                  _mod_spec(4, D, TM, rows_per_group, 0),
                  _mod_spec(5, D, TM, rows_per_group, 0),
                  pl.BlockSpec((D, TF), lambda i, f: (0, f)),
                  pl.BlockSpec((D, TF), lambda i, f: (0, f)),
                  pl.BlockSpec((TF, D), lambda i, f: (f, 0))],
        out_specs=pl.BlockSpec((TM, D), lambda i, f: (i, 0)),
        out_shape=jax.ShapeDtypeStruct((N, D), F32),
        scratch_shapes=[pltpu.VMEM((TM, D), BF16), pltpu.VMEM((TM, D), F32)],
        compiler_params=_params("arbitrary", "arbitrary"),
        name="ffn",
    )(x, g.reshape(1, D), mod, mod, mod, w1, w3, w2)


MOE_SB = 1024
MOE_SBG = 512
MOE_GG = 4
MOE_TRG = 256
MOE_TR = 2048
MOE_CG = 4


def _two_stream_specs(shape, ntp, ax=0):
    def idx_p(*g):
        return (jnp.minimum(g[ax], ntp - 1), 0)

    def idx_s(*g):
        return (jnp.maximum(g[ax] - ntp, 0), 0)
    return pl.BlockSpec(shape, idx_p), pl.BlockSpec(shape, idx_s)


def _pool_mod_spec(part, D, TM, ntp, rows_per_group):
    def idx(i, *_):
        return (jnp.where(i < ntp, 0, 1 + ((i - ntp) * TM) // rows_per_group), part, 0, 0)
    return pl.BlockSpec((None, None, 1, D), idx)


def _route_kernel(xp_ref, xs_ref, g_ref, sh_ref, sc_ref, rw_ref, tri_ref, h_ref, info_ref, infot_ref, cum_ref,
                  carry_scr, *, ntp):
    i = pl.program_id(0)

    @pl.when(i == 0)
    def _():
        carry_scr[...] = jnp.zeros_like(carry_scr)

    x = jnp.where(i < ntp, xp_ref[...], xs_ref[...])
    h = _norm_mod(x, g_ref[...], sh_ref[...], sc_ref[...])
    h_ref[...] = h.astype(BF16)
    lane = lax.broadcasted_iota(jnp.int32, (x.shape[0], LANES), 1).astype(F32)
    logits = jnp.dot(h, rw_ref[...], precision=HIGHEST, preferred_element_type=F32)
    logits = jnp.where(lane < N_EXPERTS, logits, -jnp.inf)
    m1 = jnp.max(logits, axis=-1, keepdims=True)
    i1 = jnp.min(jnp.where(logits == m1, lane, float(LANES)), axis=-1, keepdims=True)
    rest = jnp.where(lane == i1, -jnp.inf, logits)
    m2 = jnp.max(rest, axis=-1, keepdims=True)
    i2 = jnp.min(jnp.where(rest == m2, lane, float(LANES)), axis=-1, keepdims=True)
    e2 = jnp.exp(m2 - m1)
    w1 = 1.0 / (1.0 + e2)
    w2 = e2 / (1.0 + e2)
    ind = jnp.where(jnp.logical_or(lane == i1, lane == i2), 1.0, 0.0)
    before = _dot(tri_ref[...], ind.astype(BF16)) + carry_scr[...]
    r1 = jnp.sum(jnp.where(lane == i1, before, 0.0), axis=-1, keepdims=True)
    r2 = jnp.sum(jnp.where(lane == i2, before, 0.0), axis=-1, keepdims=True)
    total = carry_scr[...] + jnp.sum(ind, axis=0, keepdims=True)
    carry_scr[...] = total
    for part in range(1, MOE_SB // MOE_SBG):
        cum_ref[part - 1] = before[part * MOE_SBG:part * MOE_SBG + 1, :]
    cum_ref[MOE_SB // MOE_SBG - 1] = total
    info = jnp.where(lane == 0.0, i1, jnp.where(lane == 1.0, i2, jnp.where(lane == 2.0, w1, jnp.where(
        lane == 3.0, w2, jnp.where(lane == 4.0, r1, jnp.where(lane == 5.0, r2, 0.0))))))
    info_ref[...] = info[:, 0:SUB]
    info_t = jnp.concatenate([info[r:r + LANES, :].T for r in range(0, info.shape[0], LANES)], axis=1)
    infot_ref[...] = info_t[0:SUB, :]


def _moe_route(xp, xs, g, mod, router_w, rows_per_group):
    Np, D = xp.shape
    N = Np + xs.shape[0]
    TM = MOE_SB
    ntp = Np // TM
    nt = N // TM
    rw = jnp.pad(router_w, ((0, 0), (0, LANES - router_w.shape[1])))
    tri = jnp.asarray(np.tril(np.ones((TM, TM), np.float32), -1), dtype=BF16)
    xp_spec, xs_spec = _two_stream_specs((TM, D), ntp)
    return pl.pallas_call(
        functools.partial(_route_kernel, ntp=ntp),
        grid=(nt,),
        in_specs=[xp_spec, xs_spec,
                  pl.BlockSpec((1, D), lambda i: (0, 0)),
                  _pool_mod_spec(3, D, TM, ntp, rows_per_group),
                  _pool_mod_spec(4, D, TM, ntp, rows_per_group),
                  pl.BlockSpec((D, LANES), lambda i: (0, 0)),
                  pl.BlockSpec((TM, TM), lambda i: (0, 0))],
        out_specs=[pl.BlockSpec((TM, D), lambda i: (i, 0)),
                   pl.BlockSpec((TM, SUB), lambda i: (i, 0)),
                   pl.BlockSpec((SUB, TM), lambda i: (0, i)),
                   pl.BlockSpec((MOE_SB // MOE_SBG, 1, LANES), lambda i: (i, 0, 0))],
        out_shape=[jax.ShapeDtypeStruct((N, D), BF16),
                   jax.ShapeDtypeStruct((N, SUB), F32),
                   jax.ShapeDtypeStruct((SUB, N), F32),
                   jax.ShapeDtypeStruct((nt * (MOE_SB // MOE_SBG), 1, LANES), F32)],
        scratch_shapes=[pltpu.VMEM((1, LANES), F32)],
        compiler_params=_params("arbitrary"),
        name="moe_route",
    )(xp, xs, g.reshape(1, D), mod, mod, rw, tri)


def _moe_plan(info, info_t, cum, N):
    E, SB, TRG, TR = N_EXPERTS, MOE_SB, MOE_TRG, MOE_TR
    NB = N // SB
    rmax = 2 * N + E * TR
    RG, RT = rmax // TRG, rmax // TR
    PMAX = RG + E * NB
    i32 = jnp.int32
    parts = SB // MOE_SBG
    cum_g = cum[:, 0, :E].astype(i32).T
    cum_e = cum_g[:, parts - 1::parts]
    cnt = cum_e[:, -1]
    tiles = (cnt + TR - 1) // TR
    start = TR * (jnp.cumsum(tiles) - tiles)

    startf = start.astype(F32)

    def region_start(e):
        out = jnp.zeros_like(e)
        for k in range(E):
            out = jnp.where(e == float(k), startf[k], out)
        return out

    pos_cols = jnp.concatenate([region_start(info[:, 0:2]) + info[:, 4:6], info[:, 2:4],
                                jnp.zeros((N, 4), F32)], axis=1)
    pos_rows = jnp.concatenate([region_start(info_t[0:2]) + info_t[4:6], jnp.zeros((6, N), F32)],
                               axis=0)

    def region(row0):
        e = jnp.clip(jnp.sum(row0[:, None] >= start[None, :], axis=1) - 1, 0, E - 1)
        return e, row0 - start[e]

    eq, lo = region(jnp.arange(RG, dtype=i32) * TRG)
    hi = jnp.minimum(lo + TRG, cnt[eq])
    first = jnp.sum(cum_e[eq] <= lo[:, None], axis=1)
    last = jnp.sum(cum_e[eq] < hi[:, None], axis=1)
    nblk = jnp.where(hi > lo, last - first + 1, 0)
    pend = jnp.cumsum(nblk)
    npairs = pend[-1]
    p = jnp.arange(PMAX, dtype=i32)
    valid = p < npairs
    pc = jnp.minimum(p, npairs - 1)
    q_of = jnp.minimum(jnp.sum(pend[None, :] <= pc[:, None], axis=1), RG - 1).astype(i32)
    pstart = pend - nblk
    s_of = (first[q_of] + pc - pstart[q_of]).astype(i32)

    GG = MOE_GG
    first_g = jnp.sum(cum_g[eq] <= lo[:, None], axis=1)
    last_g = jnp.sum(cum_g[eq] < hi[:, None], axis=1)
    nblk_g = jnp.where(hi > lo, last_g - first_g + 1, 0)
    nst = (nblk_g + GG - 1) // GG
    gst_end = jnp.cumsum(nst)
    gtotal = gst_end[-1]
    smax_g = (RG + E * NB * parts + (GG - 1) * RG) // GG + 1
    jg = jnp.arange(smax_g, dtype=i32)
    g_ok = jg < gtotal
    jgc = jnp.minimum(jg, gtotal - 1)
    tq = jnp.minimum(jnp.sum(gst_end[None, :] <= jgc[:, None], axis=1), RG - 1).astype(i32)
    gg = jgc - (gst_end - nst)[tq]
    last_part = first_g[tq] + nblk_g[tq] - 1
    g_parts = jnp.minimum((first_g[tq] + GG * gg)[:, None] + jnp.arange(GG, dtype=i32)[None, :], last_part[:, None])
    g_slots = jnp.where(g_ok, jnp.clip(nblk_g[tq] - GG * gg, 0, GG), 0)
    g_first = jnp.logical_and(g_ok, gg == 0)
    gather_plan = (tq, g_parts.reshape(-1).astype(i32), g_slots.astype(i32), g_first.astype(i32))

    order = jnp.argsort(jnp.where(valid, s_of * RG + q_of, jnp.iinfo(jnp.int32).max))
    s2, q2 = s_of[order], q_of[order]
    CG = MOE_CG
    blocks = jnp.arange(NB, dtype=i32)
    per_blk = jnp.sum(jnp.logical_and(valid[None, :], s2[None, :] == blocks[:, None]), axis=1)
    pb_end = jnp.cumsum(per_blk)
    pb_start = pb_end - per_blk
    nsteps = (per_blk + CG - 1) // CG
    st_end = jnp.cumsum(nsteps)
    total = st_end[-1]
    SMAX = (PMAX + CG - 1) // CG + NB
    j = jnp.arange(SMAX, dtype=i32)
    step_ok = j < total
    jc = jnp.minimum(j, total - 1)
    blk = jnp.minimum(jnp.sum(st_end[None, :] <= jc[:, None], axis=1), NB - 1).astype(i32)
    grp = jc - (st_end - nsteps)[blk]
    slot_p = pb_start[blk][:, None] + CG * grp[:, None] + jnp.arange(CG, dtype=i32)[None, :]
    slot_ok = jnp.logical_and(slot_p < pb_end[blk][:, None], step_ok[:, None])
    slot_q = jnp.where(slot_ok, q2[jnp.minimum(slot_p, npairs - 1)], q2[pb_start[blk]][:, None])
    c_first = jnp.logical_and(step_ok, grp == 0).astype(i32)
    c_last = jnp.logical_and(step_ok, grp == nsteps[blk] - 1).astype(i32)
    combine_plan = (blk, slot_q.reshape(-1).astype(i32), slot_ok.reshape(-1).astype(i32), step_ok.astype(i32),
                    c_first, c_last)

    te, tlo = region(jnp.arange(RT, dtype=i32) * TR)
    tvalid = jnp.clip(cnt[te] - tlo, 0, TR)
    last_t = jnp.sum(tiles) - 1
    t_idx = jnp.where(tvalid > 0, jnp.arange(RT, dtype=i32), last_t).astype(i32)
    ffn_plan = (t_idx, te[t_idx].astype(i32), tvalid.astype(i32))
    return pos_cols, pos_rows, gather_plan, combine_plan, ffn_plan, rmax


def _moe_gather_kernel(q_ref, s_ref, slots_ref, first_ref, *refs):
    pos_refs, h_refs, out_ref = refs[:MOE_GG], refs[MOE_GG:2 * MOE_GG], refs[2 * MOE_GG]
    p = pl.program_id(0)
    rows = out_ref.shape[0]

    @pl.when(first_ref[p] == 1)
    def _():
        out_ref[...] = jnp.zeros_like(out_ref)

    for ns in range(1, MOE_GG + 1):
        @pl.when(slots_ref[p] == ns)
        def _(ns=ns):
            row = (lax.broadcasted_iota(jnp.int32, (rows, 1), 0) + q_ref[p] * rows).astype(F32)
            sels = []
            for k in range(ns):
                hit = jnp.logical_or(pos_refs[k][0:1, :] == row, pos_refs[k][1:2, :] == row)
                sels.append(jnp.where(hit, 1.0, 0.0).astype(BF16))
            sel = sels[0] if ns == 1 else jnp.concatenate(sels, axis=1)
            hs = h_refs[0][...] if ns == 1 else jnp.concatenate([h_refs[k][...] for k in range(ns)], axis=0)
            out_ref[...] = out_ref[...] + _dot(sel, hs).astype(BF16)


def _moe_gather(h, pos_rows, plan, rmax):
    N, D = h.shape
    nsteps = plan[0].shape[0]

    def pos_spec(k):
        return pl.BlockSpec((SUB, MOE_SBG), lambda p, q, s, *_: (0, s[MOE_GG * p + k]))

    def tok_spec(k):
        return pl.BlockSpec((MOE_SBG, D), lambda p, q, s, *_: (s[MOE_GG * p + k], 0))

    return pl.pallas_call(
        _moe_gather_kernel,
        grid_spec=pltpu.PrefetchScalarGridSpec(
            num_scalar_prefetch=4, grid=(nsteps,),
            in_specs=[pos_spec(k) for k in range(MOE_GG)] + [tok_spec(k) for k in range(MOE_GG)],
            out_specs=pl.BlockSpec((MOE_TRG, D), lambda p, q, *_: (q[p], 0))),
        out_shape=jax.ShapeDtypeStruct((rmax, D), BF16),
        compiler_params=_params("arbitrary"),
        name="moe_gather",
    )(*plan, *([pos_rows] * MOE_GG), *([h] * MOE_GG))


def _moe_ffn_kernel(t_ref, e_ref, nv_ref, x_ref, w1_ref, w3_ref, w2_ref, out_ref, acc_scr, *, nf):
    t = pl.program_id(0)
    f = pl.program_id(1)
    nv = nv_ref[t]

    def block(start, size):
        rows = pl.ds(start, size)

        @pl.when(f == 0)
        def _():
            acc_scr[rows, :] = jnp.zeros((size, acc_scr.shape[1]), F32)

        x = x_ref[rows, :]
        a = _dot(x, w1_ref[...].astype(BF16))
        b = _dot(x, w3_ref[...].astype(BF16))
        acc_scr[rows, :] += _dot((_silu(a) * b).astype(BF16), w2_ref[...].astype(BF16))

        @pl.when(f == nf - 1)
        def _():
            out_ref[rows, :] = acc_scr[rows, :].astype(out_ref.dtype)

    nsub = MOE_TR // MOE_TRG
    used = (nv + MOE_TRG - 1) // MOE_TRG

    @pl.when(used == nsub)
    def _():
        block(0, MOE_TR)

    @pl.when(jnp.logical_and(used > 0, used < nsub))
    def _():
        start = jnp.int32(0)
        size = MOE_TR // 2
        while size >= MOE_TRG:
            has = (used & (size // MOE_TRG)) != 0

            @pl.when(has)
            def _(start=start, size=size):
                block(pl.multiple_of(start, MOE_TRG), size)

            start = start + jnp.where(has, size, 0)
            size //= 2


def _moe_ffn(xs, plan, w1, w3, w2):
    rmax, D = xs.shape
    FF = w1.shape[2]
    TF = 256
    nf = FF // TF
    RT = rmax // MOE_TR

    def fidx(t, f, nv):
        return jnp.where(nv[t] > 0, f, nf - 1)

    return pl.pallas_call(
        functools.partial(_moe_ffn_kernel, nf=nf),
        grid_spec=pltpu.PrefetchScalarGridSpec(
            num_scalar_prefetch=3, grid=(RT, nf),
            in_specs=[pl.BlockSpec((MOE_TR, D), lambda t, f, ti, e, nv: (ti[t], 0)),
                      pl.BlockSpec((None, D, TF), lambda t, f, ti, e, nv: (e[t], 0, fidx(t, f, nv))),
                      pl.BlockSpec((None, D, TF), lambda t, f, ti, e, nv: (e[t], 0, fidx(t, f, nv))),
                      pl.BlockSpec((None, TF, D), lambda t, f, ti, e, nv: (e[t], fidx(t, f, nv), 0))],
            out_specs=pl.BlockSpec((MOE_TR, D), lambda t, f, ti, e, nv: (ti[t], 0)),
            scratch_shapes=[pltpu.VMEM((MOE_TR, D), F32)]),
        out_shape=jax.ShapeDtypeStruct((rmax, D), BF16),
        compiler_params=_params("arbitrary", "arbitrary"),
        name="moe_ffn",
    )(*plan, xs, w1, w3, w2)


def _moe_combine_kernel(s_ref, q_ref, slot_ok_ref, valid_ref, first_ref, last_ref, pos_ref, *refs, ntp):
    ys_refs = refs[:MOE_CG]
    xp_ref, xs_ref, gate_ref, fg_ref, op_ref, os_ref, acc_scr = refs[MOE_CG:]
    p = pl.program_id(0)

    @pl.when(valid_ref[p] == 1)
    def _():
        rows = ys_refs[0].shape[0]
        sels = []
        for k in range(MOE_CG):
            col = (lax.broadcasted_iota(jnp.int32, (1, rows), 1) + q_ref[MOE_CG * p + k] * rows).astype(F32)
            col = jnp.where(slot_ok_ref[MOE_CG * p + k] == 1, col, -1.0)
            sels.append((jnp.where(pos_ref[:, 0:1] == col, pos_ref[:, 2:3], 0.0)
                         + jnp.where(pos_ref[:, 1:2] == col, pos_ref[:, 3:4], 0.0)).astype(BF16))
        sel = jnp.concatenate(sels, axis=1)
        ys = jnp.concatenate([r[...] for r in ys_refs], axis=0)

        @pl.when(first_ref[p] == 1)
        def _():
            acc_scr[...] = jnp.zeros_like(acc_scr)

        acc_scr[...] += _dot(sel, ys)

        @pl.when(last_ref[p] == 1)
        def _():
            s = s_ref[p]
            x = jnp.where(s < ntp, xp_ref[...], xs_ref[...])
            y = x + gate_ref[...] * acc_scr[...]
            out = y * lax.rsqrt(jnp.mean(y * y, axis=-1, keepdims=True) + EPS) * fg_ref[...]

            @pl.when(s < ntp)
            def _():
                op_ref[...] = out

            @pl.when(s >= ntp)
            def _():
                os_ref[...] = out


def _moe_combine(ys, pos_cols, plan, xp, xs, mod, final_g, rows_per_group):
    Np, D = xp.shape
    Ns = xs.shape[0]
    SB = MOE_SB
    ntp = Np // SB
    nsteps = plan[0].shape[0]

    def tile_spec(k):
        return pl.BlockSpec((MOE_TRG, D), lambda p, s, q, *_: (q[MOE_CG * p + k], 0))

    def tok_p(p, s, *_):
        return (jnp.minimum(s[p], ntp - 1), 0)

    def tok_s(p, s, *_):
        return (jnp.maximum(s[p] - ntp, 0), 0)

    def gate_idx(p, s, *_):
        return (jnp.where(s[p] < ntp, 0, 1 + ((s[p] - ntp) * SB) // rows_per_group), 5, 0, 0)

    return pl.pallas_call(
        functools.partial(_moe_combine_kernel, ntp=ntp),
        grid_spec=pltpu.PrefetchScalarGridSpec(
            num_scalar_prefetch=6, grid=(nsteps,),
            in_specs=[pl.BlockSpec((SB, SUB), lambda p, s, q, *_: (s[p], 0))]
            + [tile_spec(k) for k in range(MOE_CG)]
            + [pl.BlockSpec((SB, D), tok_p),
                      pl.BlockSpec((SB, D), tok_s),
                      pl.BlockSpec((None, None, 1, D), gate_idx),
                      pl.BlockSpec((1, D), lambda p, *_: (0, 0))],
            out_specs=[pl.BlockSpec((SB, D), tok_p), pl.BlockSpec((SB, D), tok_s)],
            scratch_shapes=[pltpu.VMEM((SB, D), F32)]),
        out_shape=[jax.ShapeDtypeStruct((Np, D), F32), jax.ShapeDtypeStruct((Ns, D), F32)],
        compiler_params=_params("arbitrary"),
        name="moe_combine",
    )(*plan, pos_cols, *([ys] * MOE_CG), xp, xs, mod, final_g.reshape(1, D))


def _moe(xp, xs, g, mod, router_w, w1, w3, w2, final_g, rows_per_group):
    N = xp.shape[0] + xs.shape[0]
    h, info, info_t, cum = _moe_route(xp, xs, g, mod, router_w, rows_per_group)
    pos_cols, pos_rows, gather_plan, combine_plan, ffn_plan, rmax = _moe_plan(info, info_t, cum, N)
    x_sorted = _moe_gather(h, pos_rows, gather_plan, rmax)
    y_sorted = _moe_ffn(x_sorted, ffn_plan, w1, w3, w2)
    return _moe_combine(y_sorted, pos_cols, combine_plan, xp, xs, mod, final_g, rows_per_group)


def _gla_levels(C):
    lv, c = [], C // 2
    while c >= SUB:
        lv.append(c)
        c //= 2
    return lv


def _gla_tables(C):
    levels = _gla_levels(C)
    nr = 2 + 2 * len(levels)
    mat = np.zeros((2, nr * C, C), np.float32)
    code = np.zeros((2, C, C), np.int32)
    for d in range(2):
        p = np.arange(C) if d == 0 else C - 1 - np.arange(C)
        pi, pj = p[:, None], p[None, :]
        mat[d, 0:C] = pj <= pi
        mat[d, C:2 * C] = pj > pi
        code[d] = np.where((pj <= pi) & (pi // SUB == pj // SUB), 1, 0)
        for lv, c in enumerate(levels):
            blk = pi // c
            later = blk % 2 == 1
            mat[d, (2 + 2 * lv) * C:(3 + 2 * lv) * C] = later & (pj > blk * c - 1) & (pj <= pi)
            mat[d, (3 + 2 * lv) * C:(4 + 2 * lv) * C] = (~later) & (pj > pi) & (pj <= (blk + 1) * c - 1)
            pair = (pi // (2 * c) == pj // (2 * c)) & (pi // c != pj // c) & (pj <= pi)
            code[d] = np.where(pair, 2 + lv, code[d])
    ones = np.zeros((SUB * LANES, C), np.float32)
    for jj in range(SUB):
        ones[jj * LANES:(jj + 1) * LANES, jj::SUB] = 1.0
    return jnp.asarray(mat, dtype=BF16), jnp.asarray(code), jnp.asarray(ones, dtype=BF16)


def _bcast_sublane(x, jj):
    r, w = x.shape
    x3 = x.reshape(r // SUB, SUB, w)
    return jnp.broadcast_to(x3[:, jj:jj + 1, :], x3.shape).reshape(r, w)


def _t128(x):
    r, w = x.shape
    if w > LANES:
        return jnp.concatenate([x[:, i:i + LANES].T for i in range(0, w, LANES)], axis=0)
    return jnp.concatenate([x[i:i + LANES, :].T for i in range(0, r, LANES)], axis=1)


def _gla_kernel(q_ref, k_ref, v_ref, g_ref, lr_ref, wg_ref, ba_ref, mat_ref, code_ref, ones_ref,
                s0f_ref, s0b_ref, ng_ref, o_ref, sf_ref, sb_ref, st_scr, of_scr, *, n, C, G):
    d = pl.program_id(1)
    c = pl.program_id(2)
    levels = _gla_levels(C)

    @pl.when(jnp.logical_and(c == 0, d == 0))
    def _():
        for bb in range(G):
            for h in range(C_H):
                st_scr[bb, h] = _t128(s0f_ref[bb, h])

    @pl.when(jnp.logical_and(c == 0, d == 1))
    def _():
        for bb in range(G):
            for h in range(C_H):
                st_scr[bb, h] = _t128(s0b_ref[bb, h])

    mat = mat_ref[...]
    code = code_ref[...]
    ones = ones_ref[...]
    cums = []
    for bb in range(G):
        xg = jnp.dot(lr_ref[bb], wg_ref[...], precision=HIGHEST, preferred_element_type=F32) + ba_ref[...]
        la = (jnp.minimum(xg, 0.0) - jnp.log1p(jnp.exp(-jnp.abs(xg)))) * (LOG2E / C_TAU)
        hi = la.astype(BF16)
        lo = (la - hi.astype(F32)).astype(BF16)
        cums.append(_dot(mat, hi) + _dot(mat, lo))

    def prepare(bb, h):
        cum = cums[bb]
        ks = slice(h * C_DK, (h + 1) * C_DK)
        qh = q_ref[bb, :, ks].astype(F32) * (C_DK ** -0.5)
        kh = k_ref[bb, :, ks].astype(F32)
        b = cum[0:C, ks]
        b_rest = cum[C:2 * C, ks]
        ps = []
        for jj in range(SUB):
            dec = jnp.exp2(jnp.minimum(b - _bcast_sublane(b, jj), 0.0))
            ps.append((qh * _bcast_sublane(kh, jj) * dec).astype(BF16))
        lv_ops = []
        for lv in range(len(levels)):
            eq = cum[(2 + 2 * lv) * C:(3 + 2 * lv) * C, ks]
            ek = cum[(3 + 2 * lv) * C:(4 + 2 * lv) * C, ks]
            lv_ops.append(((qh * jnp.exp2(eq)).astype(BF16), (kh * jnp.exp2(ek)).astype(BF16)))
        qe = (qh * jnp.exp2(b)).astype(BF16)
        ke = (kh * jnp.exp2(b_rest)).astype(BF16)
        e_end = jnp.exp2(b[0:1, :] + b_rest[0:1, :])
        return jnp.concatenate(ps, axis=1), lv_ops, qe, ke, e_end

    def contract(bb, h, prep):
        pcat, lv_ops, qe, ke, e_end = prep
        vh = v_ref[bb, :, h * C_DV:(h + 1) * C_DV].astype(F32)
        att = jnp.where(code == 1, _dot(pcat, ones), 0.0)
        for lv, (qs, ks_) in enumerate(lv_ops):
            att = jnp.where(code == 2 + lv, _dot_nt(qs, ks_), att)
        st = st_scr[bb, h]
        o = _dot(att.astype(BF16), vh.astype(BF16)) + _dot_nt(qe, st.astype(BF16))
        st_scr[bb, h] = e_end * st + _dot(_t128(vh).astype(BF16), ke)
        return o

    units = [(bb, h) for h in range(C_H) for bb in range(G)]
    outs = {}
    prep = prepare(*units[0])
    for idx, (bb, h) in enumerate(units):
        nxt = prepare(*units[idx + 1]) if idx + 1 < len(units) else None
        outs[(bb, h)] = contract(bb, h, prep)
        prep = nxt
    o_all = [jnp.concatenate([outs[(bb, h)] for h in range(C_H)], axis=-1) for bb in range(G)]

    @pl.when(d == 0)
    def _():
        for bb in range(G):
            of_scr[bb, c] = o_all[bb]

    @pl.when(d == 1)
    def _():
        for bb in range(G):
            tot = o_all[bb] + of_scr[bb, n - 1 - c]
            res = []
            for h in range(C_H):
                sl = slice(h * C_DV, (h + 1) * C_DV)
                t = tot[:, sl]
                y = t * lax.rsqrt(jnp.mean(t * t, axis=-1, keepdims=True) + EPS) * ng_ref[:, sl]
                res.append(y * _silu(g_ref[bb, :, sl].astype(F32)))
            o_ref[bb] = jnp.concatenate(res, axis=-1).astype(o_ref.dtype)

    @pl.when(jnp.logical_and(c == n - 1, d == 0))
    def _():
        for bb in range(G):
            for h in range(C_H):
                sf_ref[bb, h] = _t128(st_scr[bb, h])

    @pl.when(jnp.logical_and(c == n - 1, d == 1))
    def _():
        for bb in range(G):
            for h in range(C_H):
                sb_ref[bb, h] = _t128(st_scr[bb, h])


def _gla(z, zg, B, T, w_a2, b_a, s0f, s0b, norm_g):
    C = GLA_CHUNK
    G = 2
    assert B % G == 0 and T % C == 0
    n = T // C
    HK = C_H * C_DK
    HV = C_H * C_DV
    mat, code, ones = _gla_tables(C)
    nr = mat.shape[1] // C
    wg = jnp.zeros((2, LANES, HK), F32)
    for dr in range(2):
        wg = wg.at[dr, dr * C_RANK:(dr + 1) * C_RANK, :].set(w_a2[dr])
    z3 = z.reshape(B, T, z.shape[1])
    zg3 = zg.reshape(B, T, zg.shape[1])

    def chunk(d, c):
        return c + d * (n - 1 - 2 * c)

    st_spec = pl.BlockSpec((G, C_H, C_DK, C_DV), lambda b, d, c: (b, 0, 0, 0))
    st_shape = jax.ShapeDtypeStruct((B, C_H, C_DK, C_DV), F32)
    o, sf, sb = pl.pallas_call(
        functools.partial(_gla_kernel, n=n, C=C, G=G),
        grid=(B // G, 2, n),
        in_specs=[pl.BlockSpec((G, C, HK), lambda b, d, c: (b, chunk(d, c), 0)),
                  pl.BlockSpec((G, C, HK), lambda b, d, c: (b, chunk(d, c), 1)),
                  pl.BlockSpec((G, C, HV), lambda b, d, c: (b, chunk(d, c), 1)),
                  pl.BlockSpec((G, C, HV), lambda b, d, c: (b, chunk(d, c), 2)),
                  pl.BlockSpec((G, C, LANES), lambda b, d, c: (b, chunk(d, c), 0)),
                  pl.BlockSpec((None, LANES, HK), lambda b, d, c: (d, 0, 0)),
                  pl.BlockSpec((None, 1, HK), lambda b, d, c: (d, 0, 0)),
                  pl.BlockSpec((None, nr * C, C), lambda b, d, c: (d, 0, 0)),
                  pl.BlockSpec((None, C, C), lambda b, d, c: (d, 0, 0)),
                  pl.BlockSpec((SUB * LANES, C), lambda b, d, c: (0, 0)),
                  st_spec, st_spec,
                  pl.BlockSpec((1, HV), lambda b, d, c: (0, 0))],
        out_specs=[pl.BlockSpec((G, C, HV), lambda b, d, c: (b, (n - 1) - d * c, 0)),
                   st_spec, st_spec],
        out_shape=[jax.ShapeDtypeStruct((B, T, HV), BF16), st_shape, st_shape],
        scratch_shapes=[pltpu.VMEM((G, C_H, C_DV, C_DK), F32), pltpu.VMEM((G, n, C, HV), F32)],
        compiler_params=_params("arbitrary", "arbitrary", "arbitrary"),
        name="gla",
    )(z3, z3, z3, z3, zg3, wg, b_a.reshape(2, 1, HK), mat, code, ones, s0f, s0b, norm_g.reshape(1, HV))
    return o.reshape(B * T, HV), sf, sb


def _run_stream(x, B, T, mods, ctx, p):
    N, D = x.shape
    rpg = N // mods[0].shape[0]
    TM = min(2048, rpg)
    nb = (B_H + 2 * B_HKV) * B_HD

    w_in = p['even_w_in'][0]
    z, zb = _norm_mm(x, p['norm1_g'][0], mods[0], (0, 1), w_in, w_in, (nb, MIX_MAIN // nb), TM, rpg)
    if ctx is None:
        s0 = jnp.zeros((B, A_H, A_DK, A_DV), F32)
        a_f0, a_b0, cache_k, cache_v = s0, s0, None, None
    else:
        cache_k, cache_v, a_f0, a_b0 = ctx[0], ctx[1], ctx[2], ctx[3]
    o_a, a_sf, a_sb = _retention(z, B, T, p['a_log_gamma'][0], a_f0, a_b0, p['a_norm_g'][0])
    qpad, k_norm, k_rot, v_bf = _bprep(zb, T, p['b_q_g'][0], p['b_k_g'][0], rope=ctx is not None)
    o_b = _attention(qpad, k_rot, v_bf, B, T, cache_k, cache_v)
    x = _proj_res(x, mods[0], 2, [o_a, o_b], p['even_w_out'][0], rpg)
    x = _ffn(x, p['norm2_g'][0], mods[0], p['ff_w1'][0], p['ff_w3'][0], p['ff_w2'][0], rpg)

    w_in = p['odd_w_in'][0]
    w_gate = jnp.pad(w_in[:, MIX_MAIN:], ((0, 0), (0, LANES - 2 * C_RANK)))
    z1, z1g = _norm_mm(x, p['norm1_g'][1], mods[1], (0, 1), w_in, w_gate, (LANES, 0), TM, rpg)
    if ctx is None:
        s0 = jnp.zeros((B, C_H, C_DK, C_DV), F32)
        c_f0, c_b0 = s0, s0
    else:
        c_f0, c_b0 = ctx[4], ctx[5]
    o_c, c_sf, c_sb = _gla(z1, z1g, B, T, p['c_w_a2'][0], p['c_b_a'][0], c_f0, c_b0, p['c_norm_g'][0])
    x = _proj_res(x, mods[1], 2, [o_c], p['odd_w_out'][0], rpg)
    v_raw = zb[:, (B_H + B_HKV) * B_HD:]
    return x, (k_norm, v_raw, a_sf, a_sb, c_sf, c_sb)


def kernel(x_prompt, x_sample, c, cache_b_k, cache_b_v, state_a_fwd, state_a_bwd, state_c_fwd, state_c_bwd,
           c_ctx, w_mod, b_mod, norm1_g, norm2_g, final_g, even_w_in, even_w_out, a_log_gamma, a_norm_g,
           b_q_g, b_k_g, odd_w_in, c_w_a2, c_b_a, c_norm_g, odd_w_out, ff_w1, ff_w3, ff_w2,
           router_w, moe_w1, moe_w3, moe_w2):
    Bp, Tp, D = x_prompt.shape
    Bs, Ts, _ = x_sample.shape
    L = w_mod.shape[0]
    assert L == 2 and even_w_in.shape[0] == 1 and odd_w_in.shape[0] == 1
    p = dict(norm1_g=norm1_g, norm2_g=norm2_g, final_g=final_g, even_w_in=even_w_in, even_w_out=even_w_out,
             a_log_gamma=a_log_gamma, a_norm_g=a_norm_g, b_q_g=b_q_g, b_k_g=b_k_g, odd_w_in=odd_w_in,
             c_w_a2=c_w_a2, c_b_a=c_b_a, c_norm_g=c_norm_g, odd_w_out=odd_w_out, ff_w1=ff_w1, ff_w3=ff_w3,
             ff_w2=ff_w2, router_w=router_w, moe_w1=moe_w1, moe_w3=moe_w3, moe_w2=moe_w2)

    rows = 8
    conds = jnp.concatenate([c_ctx[None, :], c, jnp.zeros((rows - 1 - Bs, D), F32)], axis=0)
    mod = _modulation(conds, w_mod, b_mod).reshape(L, rows, 6, 1, D)
    mods_p = [mod[l, 0:1] for l in range(L)]
    mods_s = [mod[l, 1:1 + Bs] for l in range(L)]

    x_p, kept = _run_stream(x_prompt.reshape(Bp * Tp, D), Bp, Tp, mods_p, None, p)
    nk = B_HKV * B_HD
    ctx = (cache_b_k[:, 0].reshape(Bs, -1, nk), cache_b_v[:, 0].reshape(Bs, -1, nk),
           state_a_fwd[:, 0], state_a_bwd[:, 0], state_c_fwd[:, 0], state_c_bwd[:, 0])
    x_s, _ = _run_stream(x_sample.reshape(Bs * Ts, D), Bs, Ts, mods_s, ctx, p)
    y_p, y_s = _moe(x_p, x_s, norm2_g[1], mod[1, 0:1 + Bs], router_w[0], moe_w1[0], moe_w3[0], moe_w2[0],
                    final_g, Ts)

    k_norm, v_raw, a_sf, a_sb, c_sf, c_sb = kept
    return (y_p.reshape(Bp, Tp, D), y_s.reshape(Bs, Ts, D),
            k_norm.reshape(Bp, 1, Tp, B_HKV, B_HD), v_raw.reshape(Bp, 1, Tp, B_HKV, B_HD),
            a_sf[:, None], a_sb[:, None], c_sf[:, None], c_sb[:, None])
```

```python
import functools

import numpy as np
import jax
import jax.numpy as jnp
from jax import lax
from jax.experimental import pallas as pl
from jax.experimental.pallas import tpu as pltpu

F32 = jnp.float32
BF16 = jnp.bfloat16
EPS = 1e-6
HIGHEST = lax.Precision.HIGHEST
LOG2E = 1.4426950408889634

VMEM_LIMIT_BYTES = 56 * 1024 * 1024

A_H, A_DK, A_DV = 4, 128, 256
B_H, B_HKV, B_HD = 8, 2, 64
C_H, C_DK, C_DV, C_RANK = 4, 128, 256, 16
C_TAU = 16.0
GRID_W = 64
ROPE_THETA = 10000.0
N_EXPERTS = 8
LANES = 128
SUB = 8
RET_CHUNK = 128
GLA_CHUNK = 128
Q_TILE = 128


def _params(*sem):
    return pltpu.CompilerParams(dimension_semantics=sem, vmem_limit_bytes=VMEM_LIMIT_BYTES)


def _dot(a, b):
    return jnp.dot(a, b, preferred_element_type=F32)


def _dot_nt(a, b):
    return lax.dot_general(a, b, (((1,), (1,)), ((), ())), preferred_element_type=F32)


def _silu(x):
    return x * jax.nn.sigmoid(x)


def _norm_mod(x, g, sh, sc):
    r = lax.rsqrt(jnp.mean(x * x, axis=-1, keepdims=True) + EPS)
    return (x * r * g) * (1.0 + sc) + sh


def _mod_kernel(c_ref, w_ref, b_ref, o_ref):
    c = c_ref[...]
    o_ref[...] = jnp.dot(_silu(c), w_ref[...], precision=HIGHEST, preferred_element_type=F32) + b_ref[...]


def _modulation(conds, w_mod, b_mod):
    L, D, D6 = w_mod.shape
    R = conds.shape[0]
    TN = 1024
    return pl.pallas_call(
        _mod_kernel,
        grid=(L, D6 // TN),
        in_specs=[pl.BlockSpec((R, D), lambda l, j: (0, 0)),
                  pl.BlockSpec((None, D, TN), lambda l, j: (l, 0, j)),
                  pl.BlockSpec((None, 1, TN), lambda l, j: (l, 0, j))],
        out_specs=pl.BlockSpec((None, R, TN), lambda l, j: (l, 0, j)),
        out_shape=jax.ShapeDtypeStruct((L, R, D6), F32),
        compiler_params=_params("arbitrary", "arbitrary"),
        name="modulation",
    )(conds, w_mod, b_mod.reshape(L, 1, D6))


def _mod_spec(part, D, TM, rows_per_group, axis):
    def idx(*g):
        return ((g[axis] * TM) // rows_per_group, part, 0, 0)
    return pl.BlockSpec((None, None, 1, D), idx)


MIX_MAIN = A_H * (2 * A_DK + 2 * A_DV)
MIX_TN = 768


def _norm_mm_kernel(x_ref, g_ref, sh_ref, sc_ref, w_ref, we_ref, o_ref, oe_ref, h_scr, *, nmain):
    j = pl.program_id(1)

    @pl.when(j == 0)
    def _():
        h_scr[...] = _norm_mod(x_ref[...], g_ref[...], sh_ref[...], sc_ref[...]).astype(BF16)

    @pl.when(j < nmain)
    def _():
        o_ref[...] = _dot(h_scr[...], w_ref[...].astype(BF16)).astype(o_ref.dtype)

    @pl.when(j == nmain)
    def _():
        oe_ref[...] = _dot(h_scr[...], we_ref[...].astype(BF16))


def _norm_mm(x, g, mod, parts, w, w_extra, extra_block, TM, rows_per_group):
    N, D = x.shape
    nmain = MIX_MAIN // MIX_TN
    WE = extra_block[0]
    return pl.pallas_call(
        functools.partial(_norm_mm_kernel, nmain=nmain),
        grid=(N // TM, nmain + 1),
        in_specs=[pl.BlockSpec((TM, D), lambda i, j: (i, 0)),
                  pl.BlockSpec((1, D), lambda i, j: (0, 0)),
                  _mod_spec(parts[0], D, TM, rows_per_group, 0),
                  _mod_spec(parts[1], D, TM, rows_per_group, 0),
                  pl.BlockSpec((D, MIX_TN), lambda i, j: (0, jnp.minimum(j, nmain - 1))),
                  pl.BlockSpec((D, WE), lambda i, j: (0, extra_block[1]))],
        out_specs=[pl.BlockSpec((TM, MIX_TN), lambda i, j: (i, jnp.minimum(j, nmain - 1))),
                   pl.BlockSpec((TM, WE), lambda i, j: (i, 0))],
        out_shape=[jax.ShapeDtypeStruct((N, MIX_MAIN), BF16), jax.ShapeDtypeStruct((N, WE), F32)],
        scratch_shapes=[pltpu.VMEM((TM, D), BF16)],
        compiler_params=_params("arbitrary", "arbitrary"),
        name="norm_mm",
    )(x, g.reshape(1, D), mod, mod, w, w_extra)


def _ret_kernel(lg_ref, q_ref, k_ref, v_ref, ag_ref, s0f_ref, s0b_ref, ng_ref,
                o_ref, sf_ref, sb_ref, s_scr, of_scr, *, n, C):
    d = pl.program_id(1)
    c = pl.program_id(2)

    @pl.when(jnp.logical_and(c == 0, d == 0))
    def _():
        s_scr[...] = s0f_ref[...]

    @pl.when(jnp.logical_and(c == 0, d == 1))
    def _():
        s_scr[...] = s0b_ref[...]

    df = d.astype(F32)
    sgn = 1.0 - 2.0 * df
    ii = lax.broadcasted_iota(jnp.int32, (C, C), 0).astype(F32)
    jj = lax.broadcasted_iota(jnp.int32, (C, C), 1).astype(F32)
    dd = (ii - jj) * sgn
    feeds = dd >= 0.0
    ddc = jnp.maximum(dd, 0.0)
    ri = lax.broadcasted_iota(jnp.int32, (C, 1), 0).astype(F32)
    pos_q = (ri + 1.0) + df * (C - 2.0 * ri - 1.0)
    pos_k = (C - 1.0 - ri) + df * (2.0 * ri - C + 1.0)
    chunk_len = jnp.full((1, A_DV), float(C), F32)

    outs = []
    for h in range(A_H):
        lg = lg_ref[d, h]
        dmask = jnp.where(feeds, jnp.exp2(lg * ddc), 0.0)
        qh = q_ref[:, h * A_DK:(h + 1) * A_DK].astype(F32) * (A_DK ** -0.5)
        kh = k_ref[:, h * A_DK:(h + 1) * A_DK].astype(F32)
        vh = v_ref[:, h * A_DV:(h + 1) * A_DV].astype(BF16)
        s = s_scr[h]
        att = _dot_nt(qh.astype(BF16), kh.astype(BF16)) * dmask
        o = _dot(att.astype(BF16), vh) + _dot((qh * jnp.exp2(lg * pos_q)).astype(BF16), s.astype(BF16))
        kd = kh * jnp.exp2(lg * pos_k)
        s_scr[h] = jnp.exp2(lg * chunk_len) * s + _dot(kd.T.astype(BF16), vh)
        outs.append(o)
    o_all = jnp.concatenate(outs, axis=-1)

    @pl.when(d == 0)
    def _():
        of_scr[c] = o_all

    @pl.when(d == 1)
    def _():
        tot = o_all + of_scr[n - 1 - c]
        res = []
        for h in range(A_H):
            sl = slice(h * A_DV, (h + 1) * A_DV)
            t = tot[:, sl]
            dev = t - jnp.mean(t, axis=-1, keepdims=True)
            y = dev * lax.rsqrt(jnp.mean(dev * dev, axis=-1, keepdims=True) + EPS) * ng_ref[:, sl]
            res.append(y * _silu(ag_ref[:, sl].astype(F32)))
        o_ref[...] = jnp.concatenate(res, axis=-1).astype(o_ref.dtype)

    @pl.when(jnp.logical_and(c == n - 1, d == 0))
    def _():
        sf_ref[...] = s_scr[...]

    @pl.when(jnp.logical_and(c == n - 1, d == 1))
    def _():
        sb_ref[...] = s_scr[...]


def _retention(z, B, T, log_gamma, s0f, s0b, norm_g):
    C = RET_CHUNK
    n = T // C
    HV = A_H * A_DV

    def row(b, d, c):
        return b * n + c + d * (n - 1 - 2 * c)

    st_spec = pl.BlockSpec((None, A_H, A_DK, A_DV), lambda b, d, c: (b, 0, 0, 0))
    st_shape = jax.ShapeDtypeStruct((B, A_H, A_DK, A_DV), F32)
    return pl.pallas_call(
        functools.partial(_ret_kernel, n=n, C=C),
        grid=(B, 2, n),
        in_specs=[pl.BlockSpec(memory_space=pltpu.SMEM),
                  pl.BlockSpec((C, 512), lambda b, d, c: (row(b, d, c), 0)),
                  pl.BlockSpec((C, 512), lambda b, d, c: (row(b, d, c), 1)),
                  pl.BlockSpec((C, HV), lambda b, d, c: (row(b, d, c), 1)),
                  pl.BlockSpec((C, HV), lambda b, d, c: (row(b, d, c), 2)),
                  st_spec, st_spec,
                  pl.BlockSpec((1, HV), lambda b, d, c: (0, 0))],
        out_specs=[pl.BlockSpec((C, HV), lambda b, d, c: (b * n + (n - 1) - d * c, 0)),
                   st_spec, st_spec],
        out_shape=[jax.ShapeDtypeStruct((B * T, HV), BF16), st_shape, st_shape],
        scratch_shapes=[pltpu.VMEM((A_H, A_DK, A_DV), F32), pltpu.VMEM((n, C, HV), F32)],
        compiler_params=_params("arbitrary", "arbitrary", "arbitrary"),
        name="retention",
    )(log_gamma * LOG2E, z, z, z, z, s0f, s0b, norm_g.reshape(1, HV))


def _group_sum_matrix(width, group):
    i = np.arange(width)
    return jnp.asarray((i[:, None] // group == i[None, :] // group).astype(np.float32), dtype=BF16)


def _q_pad_matrix():
    m = np.zeros((B_H * B_HD, B_H * LANES), np.float32)
    g = B_H // B_HKV
    for h in range(B_H):
        for t in range(B_HD):
            m[h * B_HD + t, h * LANES + (h // g) * B_HD + t] = 1.0
    return jnp.asarray(m, dtype=BF16)


def _rope_tables(T):
    rows = T // GRID_W
    row = np.repeat(np.arange(rows, dtype=np.float64), GRID_W)
    col = np.tile(np.arange(GRID_W, dtype=np.float64), rows)
    nq = B_HD // 4
    inv = ROPE_THETA ** (-np.arange(nq, dtype=np.float64) / nq)
    ang = np.concatenate([row[:, None] * inv, col[:, None] * inv], axis=-1)
    cos = np.repeat(np.cos(ang), 2, axis=-1)
    sin = np.repeat(np.sin(ang), 2, axis=-1)
    sign = np.tile(np.array([-1.0, 1.0]), B_HD // 2)
    reps = LANES // B_HD
    return (jnp.asarray(np.tile(cos, (1, reps)), dtype=F32),
            jnp.asarray(np.tile(sin * sign, (1, reps)), dtype=F32))


def _group_rmsnorm(x, gsum, g):
    x2 = x * x
    hi = x2.astype(BF16)
    lo = (x2 - hi.astype(F32)).astype(BF16)
    ss = _dot(hi, gsum) + _dot(lo, gsum)
    return x * lax.rsqrt(ss * (1.0 / B_HD) + EPS) * g


def _rotate_pairs(x, cos, sin_signed):
    n = x.shape[1]
    lane = lax.broadcasted_iota(jnp.int32, x.shape, 1)
    partner = jnp.where(lane % 2 == 0, pltpu.roll(x, n - 1, 1), pltpu.roll(x, 1, 1))
    reps = n // LANES
    if reps > 1:
        cos = jnp.concatenate([cos] * reps, axis=1)
        sin_signed = jnp.concatenate([sin_signed] * reps, axis=1)
    return x * cos + partner * sin_signed


def _bprep_kernel(z_ref, qg_ref, kg_ref, cos_ref, sin_ref, gq_ref, gk_ref, pad_ref,
                  qpad_ref, kn_ref, kr_ref, vb_ref, *, rope):
    nq = B_H * B_HD
    nk = B_HKV * B_HD
    qn = _group_rmsnorm(z_ref[:, 0:nq], gq_ref[...], qg_ref[...])
    kn = _group_rmsnorm(z_ref[:, nq:nq + nk], gk_ref[...], kg_ref[...])
    kn_ref[...] = kn
    if rope:
        qn = _rotate_pairs(qn, cos_ref[...], sin_ref[...])
        kn = _rotate_pairs(kn, cos_ref[...], sin_ref[...])
    kr_ref[...] = kn.astype(BF16)
    vb_ref[...] = z_ref[:, nq + nk:nq + 2 * nk].astype(BF16)
    qs = (qn * (B_HD ** -0.5 * LOG2E)).astype(BF16)
    qpad_ref[...] = _dot(qs, pad_ref[...]).astype(BF16)


def _bprep(z, T, q_g, k_g, rope):
    N = z.shape[0]
    TM = min(512, T)
    nq = B_H * B_HD
    nk = B_HKV * B_HD
    width = nq + 2 * nk
    assert z.shape[1] == width
    cos, sin = _rope_tables(T if rope else TM)
    nt = T // TM if rope else 1
    const = lambda i: (0, 0)
    return pl.pallas_call(
        functools.partial(_bprep_kernel, rope=rope),
        grid=(N // TM,),
        in_specs=[pl.BlockSpec((TM, width), lambda i: (i, 0)),
                  pl.BlockSpec((1, nq), const),
                  pl.BlockSpec((1, nk), const),
                  pl.BlockSpec((TM, LANES), lambda i: (i % nt, 0)),
                  pl.BlockSpec((TM, LANES), lambda i: (i % nt, 0)),
                  pl.BlockSpec((nq, nq), const),
                  pl.BlockSpec((nk, nk), const),
                  pl.BlockSpec((nq, B_H * LANES), const)],
        out_specs=[pl.BlockSpec((TM, B_H * LANES), lambda i: (i, 0)),
                   pl.BlockSpec((TM, nk), lambda i: (i, 0)),
                   pl.BlockSpec((TM, nk), lambda i: (i, 0)),
                   pl.BlockSpec((TM, nk), lambda i: (i, 0))],
        out_shape=[jax.ShapeDtypeStruct((N, B_H * LANES), BF16),
                   jax.ShapeDtypeStruct((N, nk), F32),
                   jax.ShapeDtypeStruct((N, nk), BF16),
                   jax.ShapeDtypeStruct((N, nk), BF16)],
        compiler_params=_params("arbitrary"),
        name="attn_prep",
    )(z, jnp.tile(q_g, B_H).reshape(1, nq), jnp.tile(k_g, B_HKV).reshape(1, nk), cos, sin,
      _group_sum_matrix(nq, B_HD), _group_sum_matrix(nk, B_HD), _q_pad_matrix())


def _lane_fold(x, op):
    acc = x[:, 0:LANES]
    for j in range(1, x.shape[1] // LANES):
        acc = op(acc, x[:, j * LANES:(j + 1) * LANES])
    return acc


def _attn_kernel(*refs, has_cache, kc, nq):
    if has_cache:
        q_ref, k_ref, v_ref, ck_ref, cv_ref, o_ref, s_scr, m_scr, mprev_scr, l_scr, acc_scr = refs
        kcc = min(kc, ck_ref.shape[0])
        ncache = ck_ref.shape[0] // kcc
    else:
        q_ref, k_ref, v_ref, o_ref, s_scr, m_scr, mprev_scr, l_scr, acc_scr = refs
        kcc, ncache = kc, 0
    i = pl.program_id(1)
    tq = q_ref.shape[0]
    nlat = k_ref.shape[0] // kc

    def score(c, kblk):
        q = jnp.concatenate([q_ref[:, h * LANES:(h + 1) * LANES] for h in range(B_H)], axis=0)
        s = _dot_nt(q, kblk)
        s_scr[c, :, 0:kblk.shape[0]] = s
        m_scr[...] = jnp.maximum(m_scr[...], _lane_fold(s, jnp.maximum))

    def weight(c, vblk):
        s = s_scr[c, :, 0:vblk.shape[0]]
        mp = mprev_scr[...]
        ps = [jnp.exp2(s[:, j * LANES:(j + 1) * LANES] - mp) for j in range(vblk.shape[0] // LANES)]
        tot = ps[0]
        for pj in ps[1:]:
            tot = tot + pj
        l_scr[...] += tot
        acc_scr[...] += _dot(jnp.concatenate(ps, axis=1).astype(BF16), vblk)

    def run(do_weight, do_score):
        def unit(c, kblk, vblk):
            if do_weight:
                weight(c, vblk())
            if do_score:
                score(c, kblk())

        for c in range(ncache):
            unit(c, lambda: ck_ref[c * kcc:(c + 1) * kcc, :].astype(BF16),
                 lambda: cv_ref[c * kcc:(c + 1) * kcc, :].astype(BF16))

        def body(c, carry):
            rows = pl.ds(pl.multiple_of(c * kc, kc), kc)
            unit(ncache + c, lambda: k_ref[rows, :], lambda: v_ref[rows, :])
            return carry
        lax.fori_loop(0, nlat, body, 0)

    @pl.when(i < nq)
    def _():
        m_scr[...] = jnp.full(m_scr.shape, -jnp.inf, F32)

    @pl.when(i > 0)
    def _():
        l_scr[...] = jnp.zeros_like(l_scr)
        acc_scr[...] = jnp.zeros_like(acc_scr)

    @pl.when(i == 0)
    def _():
        run(False, True)

    @pl.when(jnp.logical_and(i > 0, i < nq))
    def _():
        run(True, True)

    @pl.when(i == nq)
    def _():
        run(True, False)

    @pl.when(i > 0)
    def _():
        r_all = acc_scr[...] / jnp.sum(l_scr[...], axis=-1, keepdims=True)
        g = B_H // B_HKV
        lane = lax.broadcasted_iota(jnp.int32, (tq, LANES), 1)
        outs = []
        for j in range(B_H // 2):
            pair = []
            for half in range(2):
                h = 2 * j + half
                r = r_all[h * tq:(h + 1) * tq, :]
                if h // g != half:
                    r = pltpu.roll(r, B_HD, 1)
                pair.append(r)
            outs.append(jnp.where(lane < B_HD, pair[0], pair[1]))
        o_ref[...] = jnp.concatenate(outs, axis=-1).astype(o_ref.dtype)

    @pl.when(i < nq)
    def _():
        mprev_scr[...] = jnp.broadcast_to(jnp.max(m_scr[...], axis=-1, keepdims=True), mprev_scr.shape)


def _attention(qpad, kr, vb, B, T, cache_k, cache_v):
    has_cache = cache_k is not None
    TQ = Q_TILE
    nq = T // TQ
    nk = B_HKV * B_HD
    in_specs = [pl.BlockSpec((TQ, B_H * LANES), lambda b, i: (b * nq + jnp.minimum(i, nq - 1), 0)),
                pl.BlockSpec((T, nk), lambda b, i: (b, 0)),
                pl.BlockSpec((T, nk), lambda b, i: (b, 0))]
    args = [qpad, kr, vb]
    kc = min(1024, T)
    nchunks = T // kc
    if has_cache:
        P = cache_k.shape[1]
        assert P % min(kc, P) == 0
        nchunks += P // min(kc, P)
        in_specs += [pl.BlockSpec((None, P, nk), lambda b, i: (b, 0, 0))] * 2
        args += [cache_k, cache_v]
    R = B_H * TQ
    return pl.pallas_call(
        functools.partial(_attn_kernel, has_cache=has_cache, kc=kc, nq=nq),
        grid=(B, nq + 1),
        in_specs=in_specs,
        out_specs=pl.BlockSpec((TQ, B_H * B_HD), lambda b, i: (b * nq + jnp.maximum(i - 1, 0), 0)),
        out_shape=jax.ShapeDtypeStruct((B * T, B_H * B_HD), BF16),
        scratch_shapes=[pltpu.VMEM((nchunks, R, kc), F32)] + [pltpu.VMEM((R, LANES), F32)] * 4,
        compiler_params=_params("arbitrary", "arbitrary"),
        name="attention",
    )(*args)


def _proj_res_kernel(*refs, n_in):
    x_ref, gate_ref = refs[0], refs[1]
    o_refs = refs[2:2 + n_in]
    w_refs = refs[2 + n_in:2 + 2 * n_in]
    out_ref = refs[2 + 2 * n_in]
    wbf_refs = refs[3 + 2 * n_in:]

    @pl.when(pl.program_id(0) == 0)
    def _():
        for w_ref, wbf_ref in zip(w_refs, wbf_refs):
            wbf_ref[...] = w_ref[...].astype(BF16)

    acc = _dot(o_refs[0][...], wbf_refs[0][...])
    for o_ref, wbf_ref in zip(o_refs[1:], wbf_refs[1:]):
        acc = acc + _dot(o_ref[...], wbf_ref[...])
    out_ref[...] = x_ref[...] + gate_ref[...] * acc


def _proj_res(x, mod, part, acts, w, rows_per_group):
    N, D = x.shape
    TM = min(1024, rows_per_group)
    n_in = len(acts)
    widths = [a.shape[1] for a in acts]
    offs = np.cumsum([0] + widths[:-1]).tolist()
    in_specs = [pl.BlockSpec((TM, D), lambda i: (i, 0)),
                _mod_spec(part, D, TM, rows_per_group, 0)]
    in_specs += [pl.BlockSpec((TM, wd), lambda i: (i, 0)) for wd in widths]
    in_specs += [pl.BlockSpec((wd, D), functools.partial(lambda i, blk: (blk, 0), blk=off // wd))
                 for wd, off in zip(widths, offs)]
    return pl.pallas_call(
        functools.partial(_proj_res_kernel, n_in=n_in),
        grid=(N // TM,),
        in_specs=in_specs,
        out_specs=pl.BlockSpec((TM, D), lambda i: (i, 0)),
        out_shape=jax.ShapeDtypeStruct((N, D), F32),
        scratch_shapes=[pltpu.VMEM((wd, D), BF16) for wd in widths],
        compiler_params=_params("arbitrary"),
        name="proj_residual",
    )(x, mod, *acts, *([w] * n_in))


def _ffn_kernel(x_ref, g_ref, sh_ref, sc_ref, gate_ref, w1_ref, w3_ref, w2_ref, out_ref, h_scr, acc_scr, *, nf):
    f = pl.program_id(1)

    @pl.when(f == 0)
    def _():
        h_scr[...] = _norm_mod(x_ref[...], g_ref[...], sh_ref[...], sc_ref[...]).astype(BF16)
        acc_scr[...] = jnp.zeros_like(acc_scr)

    h = h_scr[...]
    a = _dot(h, w1_ref[...].astype(BF16))
    b = _dot(h, w3_ref[...].astype(BF16))
    acc_scr[...] += _dot((_silu(a) * b).astype(BF16), w2_ref[...].astype(BF16))

    @pl.when(f == nf - 1)
    def _():
        out_ref[...] = x_ref[...] + gate_ref[...] * acc_scr[...]


def _ffn(x, g, mod, w1, w3, w2, rows_per_group):
    N, D = x.shape
    FF = w1.shape[1]
    TM, TF = min(1024, rows_per_group), 256
    nf = FF // TF
    return pl.pallas_call(
        functools.partial(_ffn_kernel, nf=nf),
        grid=(N // TM, nf),
        in_specs=[pl.BlockSpec((TM, D), lambda i, f: (i, 0)),
                  pl.BlockSpec((1, D), lambda i, f: (0, 0)),
                  _mod_spec(3, D, TM, rows_per_group, 0),
                  _mod_spec(4, D, TM, rows_per_group, 0),
                  _mod_spec(5, D, TM, rows_per_group, 0),
                  pl.BlockSpec((D, TF), lambda i, f: (0, f)),
                  pl.BlockSpec((D, TF), lambda i, f: (0, f)),
                  pl.BlockSpec((TF, D), lambda i, f: (f, 0))],
        out_specs=pl.BlockSpec((TM, D), lambda i, f: (i, 0)),
        out_shape=jax.ShapeDtypeStruct((N, D), F32),
        scratch_shapes=[pltpu.VMEM((TM, D), BF16), pltpu.VMEM((TM, D), F32)],
        compiler_params=_params("arbitrary", "arbitrary"),
        name="ffn",
    )(x, g.reshape(1, D), mod, mod, mod, w1, w3, w2)


MOE_SB = 1024
MOE_SBG = 512
MOE_GG = 4
MOE_TRG = 256
MOE_TR = 2048
MOE_CG = 4


def _two_stream_specs(shape, ntp, ax=0):
    def idx_p(*g):
        return (jnp.minimum(g[ax], ntp - 1), 0)

    def idx_s(*g):
        return (jnp.maximum(g[ax] - ntp, 0), 0)
    return pl.BlockSpec(shape, idx_p), pl.BlockSpec(shape, idx_s)


def _pool_mod_spec(part, D, TM, ntp, rows_per_group):
    def idx(i, *_):
        return (jnp.where(i < ntp, 0, 1 + ((i - ntp) * TM) // rows_per_group), part, 0, 0)
    return pl.BlockSpec((None, None, 1, D), idx)


def _route_kernel(xp_ref, xs_ref, g_ref, sh_ref, sc_ref, rw_ref, tri_ref, h_ref, info_ref, infot_ref, cum_ref,
                  carry_scr, *, ntp):
    i = pl.program_id(0)

    @pl.when(i == 0)
    def _():
        carry_scr[...] = jnp.zeros_like(carry_scr)

    x = jnp.where(i < ntp, xp_ref[...], xs_ref[...])
    h = _norm_mod(x, g_ref[...], sh_ref[...], sc_ref[...])
    h_ref[...] = h.astype(BF16)
    lane = lax.broadcasted_iota(jnp.int32, (x.shape[0], LANES), 1).astype(F32)
    rw = rw_ref[...]
    h_hi = h.astype(BF16)
    h_lo = (h - h_hi.astype(F32)).astype(BF16)
    rw_hi = rw.astype(BF16)
    rw_lo = (rw - rw_hi.astype(F32)).astype(BF16)
    logits = _dot(h_hi, rw_hi) + (_dot(h_hi, rw_lo) + _dot(h_lo, rw_hi))
    logits = jnp.where(lane < N_EXPERTS, logits, -jnp.inf)
    m1 = jnp.max(logits, axis=-1, keepdims=True)
    i1 = jnp.min(jnp.where(logits == m1, lane, float(LANES)), axis=-1, keepdims=True)
    rest = jnp.where(lane == i1, -jnp.inf, logits)
    m2 = jnp.max(rest, axis=-1, keepdims=True)
    i2 = jnp.min(jnp.where(rest == m2, lane, float(LANES)), axis=-1, keepdims=True)
    e2 = jnp.exp(m2 - m1)
    w1 = 1.0 / (1.0 + e2)
    w2 = e2 / (1.0 + e2)
    ind = jnp.where(jnp.logical_or(lane == i1, lane == i2), 1.0, 0.0)
    before = _dot(tri_ref[...], ind.astype(BF16)) + carry_scr[...]
    r1 = jnp.sum(jnp.where(lane == i1, before, 0.0), axis=-1, keepdims=True)
    r2 = jnp.sum(jnp.where(lane == i2, before, 0.0), axis=-1, keepdims=True)
    total = carry_scr[...] + jnp.sum(ind, axis=0, keepdims=True)
    carry_scr[...] = total
    for part in range(1, MOE_SB // MOE_SBG):
        cum_ref[part - 1] = before[part * MOE_SBG:part * MOE_SBG + 1, :]
    cum_ref[MOE_SB // MOE_SBG - 1] = total
    info = jnp.where(lane == 0.0, i1, jnp.where(lane == 1.0, i2, jnp.where(lane == 2.0, w1, jnp.where(
        lane == 3.0, w2, jnp.where(lane == 4.0, r1, jnp.where(lane == 5.0, r2, 0.0))))))
    info_ref[...] = info[:, 0:SUB]
    info_t = jnp.concatenate([info[r:r + LANES, :].T for r in range(0, info.shape[0], LANES)], axis=1)
    infot_ref[...] = info_t[0:SUB, :]


def _moe_route(xp, xs, g, mod, router_w, rows_per_group):
    Np, D = xp.shape
    N = Np + xs.shape[0]
    TM = MOE_SB
    ntp = Np // TM
    nt = N // TM
    rw = jnp.pad(router_w, ((0, 0), (0, LANES - router_w.shape[1])))
    tri = jnp.asarray(np.tril(np.ones((TM, TM), np.float32), -1), dtype=BF16)
    xp_spec, xs_spec = _two_stream_specs((TM, D), ntp)
    return pl.pallas_call(
        functools.partial(_route_kernel, ntp=ntp),
        grid=(nt,),
        in_specs=[xp_spec, xs_spec,
                  pl.BlockSpec((1, D), lambda i: (0, 0)),
                  _pool_mod_spec(3, D, TM, ntp, rows_per_group),
                  _pool_mod_spec(4, D, TM, ntp, rows_per_group),
                  pl.BlockSpec((D, LANES), lambda i: (0, 0)),
                  pl.BlockSpec((TM, TM), lambda i: (0, 0))],
        out_specs=[pl.BlockSpec((TM, D), lambda i: (i, 0)),
                   pl.BlockSpec((TM, SUB), lambda i: (i, 0)),
                   pl.BlockSpec((SUB, TM), lambda i: (0, i)),
                   pl.BlockSpec((MOE_SB // MOE_SBG, 1, LANES), lambda i: (i, 0, 0))],
        out_shape=[jax.ShapeDtypeStruct((N, D), BF16),
                   jax.ShapeDtypeStruct((N, SUB), F32),
                   jax.ShapeDtypeStruct((SUB, N), F32),
                   jax.ShapeDtypeStruct((nt * (MOE_SB // MOE_SBG), 1, LANES), F32)],
        scratch_shapes=[pltpu.VMEM((1, LANES), F32)],
        compiler_params=_params("arbitrary"),
        name="moe_route",
    )(xp, xs, g.reshape(1, D), mod, mod, rw, tri)


def _moe_plan(info, info_t, cum, N):
    E, SB, TRG, TR = N_EXPERTS, MOE_SB, MOE_TRG, MOE_TR
    NB = N // SB
    rmax = 2 * N + E * TR
    RG, RT = rmax // TRG, rmax // TR
    PMAX = RG + E * NB
    i32 = jnp.int32
    parts = SB // MOE_SBG
    cum_g = cum[:, 0, :E].astype(i32).T
    cum_e = cum_g[:, parts - 1::parts]
    cnt = cum_e[:, -1]
    tiles = (cnt + TR - 1) // TR
    start = TR * (jnp.cumsum(tiles) - tiles)

    startf = start.astype(F32)

    def region_start(e):
        out = jnp.zeros_like(e)
        for k in range(E):
            out = jnp.where(e == float(k), startf[k], out)
        return out

    pos_cols = jnp.concatenate([region_start(info[:, 0:2]) + info[:, 4:6], info[:, 2:4],
                                jnp.zeros((N, 4), F32)], axis=1)
    pos_rows = jnp.concatenate([region_start(info_t[0:2]) + info_t[4:6], jnp.zeros((6, N), F32)],
                               axis=0)

    def region(row0):
        e = jnp.clip(jnp.sum(row0[:, None] >= start[None, :], axis=1) - 1, 0, E - 1)
        return e, row0 - start[e]

    eq, lo = region(jnp.arange(RG, dtype=i32) * TRG)
    hi = jnp.minimum(lo + TRG, cnt[eq])
    first = jnp.sum(cum_e[eq] <= lo[:, None], axis=1)
    last = jnp.sum(cum_e[eq] < hi[:, None], axis=1)
    nblk = jnp.where(hi > lo, last - first + 1, 0)
    pend = jnp.cumsum(nblk)
    npairs = pend[-1]
    p = jnp.arange(PMAX, dtype=i32)
    valid = p < npairs
    pc = jnp.minimum(p, npairs - 1)
    q_of = jnp.minimum(jnp.sum(pend[None, :] <= pc[:, None], axis=1), RG - 1).astype(i32)
    pstart = pend - nblk
    s_of = (first[q_of] + pc - pstart[q_of]).astype(i32)

    GG = MOE_GG
    first_g = jnp.sum(cum_g[eq] <= lo[:, None], axis=1)
    last_g = jnp.sum(cum_g[eq] < hi[:, None], axis=1)
    nblk_g = jnp.where(hi > lo, last_g - first_g + 1, 0)
    nst = (nblk_g + GG - 1) // GG
    gst_end = jnp.cumsum(nst)
    gtotal = gst_end[-1]
    smax_g = (RG + E * NB * parts + (GG - 1) * RG) // GG + 1
    jg = jnp.arange(smax_g, dtype=i32)
    g_ok = jg < gtotal
    jgc = jnp.minimum(jg, gtotal - 1)
    tq = jnp.minimum(jnp.sum(gst_end[None, :] <= jgc[:, None], axis=1), RG - 1).astype(i32)
    gg = jgc - (gst_end - nst)[tq]
    last_part = first_g[tq] + nblk_g[tq] - 1
    g_parts = jnp.minimum((first_g[tq] + GG * gg)[:, None] + jnp.arange(GG, dtype=i32)[None, :], last_part[:, None])
    g_slots = jnp.where(g_ok, jnp.clip(nblk_g[tq] - GG * gg, 0, GG), 0)
    g_first = jnp.logical_and(g_ok, gg == 0)
    gather_plan = (tq, g_parts.reshape(-1).astype(i32), g_slots.astype(i32), g_first.astype(i32))

    order = jnp.argsort(jnp.where(valid, s_of * RG + q_of, jnp.iinfo(jnp.int32).max))
    s2, q2 = s_of[order], q_of[order]
    CG = MOE_CG
    blocks = jnp.arange(NB, dtype=i32)
    per_blk = jnp.sum(jnp.logical_and(valid[None, :], s2[None, :] == blocks[:, None]), axis=1)
    pb_end = jnp.cumsum(per_blk)
    pb_start = pb_end - per_blk
    nsteps = (per_blk + CG - 1) // CG
    st_end = jnp.cumsum(nsteps)
    total = st_end[-1]
    SMAX = (PMAX + CG - 1) // CG + NB
    j = jnp.arange(SMAX, dtype=i32)
    step_ok = j < total
    jc = jnp.minimum(j, total - 1)
    blk = jnp.minimum(jnp.sum(st_end[None, :] <= jc[:, None], axis=1), NB - 1).astype(i32)
    grp = jc - (st_end - nsteps)[blk]
    slot_p = pb_start[blk][:, None] + CG * grp[:, None] + jnp.arange(CG, dtype=i32)[None, :]
    slot_ok = jnp.logical_and(slot_p < pb_end[blk][:, None], step_ok[:, None])
    slot_q = jnp.where(slot_ok, q2[jnp.minimum(slot_p, npairs - 1)], q2[pb_start[blk]][:, None])
    c_first = jnp.logical_and(step_ok, grp == 0).astype(i32)
    c_last = jnp.logical_and(step_ok, grp == nsteps[blk] - 1).astype(i32)
    combine_plan = (blk, slot_q.reshape(-1).astype(i32), slot_ok.reshape(-1).astype(i32), step_ok.astype(i32),
                    c_first, c_last)

    te, tlo = region(jnp.arange(RT, dtype=i32) * TR)
    tvalid = jnp.clip(cnt[te] - tlo, 0, TR)
    last_t = jnp.sum(tiles) - 1
    t_idx = jnp.where(tvalid > 0, jnp.arange(RT, dtype=i32), last_t).astype(i32)
    ffn_plan = (t_idx, te[t_idx].astype(i32), tvalid.astype(i32))
    return pos_cols, pos_rows, gather_plan, combine_plan, ffn_plan, rmax


def _moe_gather_kernel(q_ref, s_ref, slots_ref, first_ref, *refs):
    pos_refs, h_refs, out_ref = refs[:MOE_GG], refs[MOE_GG:2 * MOE_GG], refs[2 * MOE_GG]
    p = pl.program_id(0)
    rows = out_ref.shape[0]

    @pl.when(first_ref[p] == 1)
    def _():
        out_ref[...] = jnp.zeros_like(out_ref)

    for ns in range(1, MOE_GG + 1):
        @pl.when(slots_ref[p] == ns)
        def _(ns=ns):
            row = (lax.broadcasted_iota(jnp.int32, (rows, 1), 0) + q_ref[p] * rows).astype(F32)
            sels = []
            for k in range(ns):
                hit = jnp.logical_or(pos_refs[k][0:1, :] == row, pos_refs[k][1:2, :] == row)
                sels.append(jnp.where(hit, 1.0, 0.0).astype(BF16))
            sel = sels[0] if ns == 1 else jnp.concatenate(sels, axis=1)
            hs = h_refs[0][...] if ns == 1 else jnp.concatenate([h_refs[k][...] for k in range(ns)], axis=0)
            out_ref[...] = out_ref[...] + _dot(sel, hs).astype(BF16)


def _moe_gather(h, pos_rows, plan, rmax):
    N, D = h.shape
    nsteps = plan[0].shape[0]

    def pos_spec(k):
        return pl.BlockSpec((SUB, MOE_SBG), lambda p, q, s, *_: (0, s[MOE_GG * p + k]))

    def tok_spec(k):
        return pl.BlockSpec((MOE_SBG, D), lambda p, q, s, *_: (s[MOE_GG * p + k], 0))

    return pl.pallas_call(
        _moe_gather_kernel,
        grid_spec=pltpu.PrefetchScalarGridSpec(
            num_scalar_prefetch=4, grid=(nsteps,),
            in_specs=[pos_spec(k) for k in range(MOE_GG)] + [tok_spec(k) for k in range(MOE_GG)],
            out_specs=pl.BlockSpec((MOE_TRG, D), lambda p, q, *_: (q[p], 0))),
        out_shape=jax.ShapeDtypeStruct((rmax, D), BF16),
        compiler_params=_params("arbitrary"),
        name="moe_gather",
    )(*plan, *([pos_rows] * MOE_GG), *([h] * MOE_GG))


def _moe_ffn_kernel(t_ref, e_ref, nv_ref, x_ref, w1_ref, w3_ref, w2_ref, out_ref, acc_scr, *, nf):
    t = pl.program_id(0)
    f = pl.program_id(1)
    nv = nv_ref[t]

    def block(start, size):
        rows = pl.ds(start, size)

        @pl.when(f == 0)
        def _():
            acc_scr[rows, :] = jnp.zeros((size, acc_scr.shape[1]), F32)

        x = x_ref[rows, :]
        a = _dot(x, w1_ref[...].astype(BF16))
        b = _dot(x, w3_ref[...].astype(BF16))
        acc_scr[rows, :] += _dot((_silu(a) * b).astype(BF16), w2_ref[...].astype(BF16))

        @pl.when(f == nf - 1)
        def _():
            out_ref[rows, :] = acc_scr[rows, :].astype(out_ref.dtype)

    nsub = MOE_TR // MOE_TRG
    used = (nv + MOE_TRG - 1) // MOE_TRG

    @pl.when(used == nsub)
    def _():
        block(0, MOE_TR)

    @pl.when(jnp.logical_and(used > 0, used < nsub))
    def _():
        start = jnp.int32(0)
        size = MOE_TR // 2
        while size >= MOE_TRG:
            has = (used & (size // MOE_TRG)) != 0

            @pl.when(has)
            def _(start=start, size=size):
                block(pl.multiple_of(start, MOE_TRG), size)

            start = start + jnp.where(has, size, 0)
            size //= 2


def _moe_ffn(xs, plan, w1, w3, w2):
    rmax, D = xs.shape
    FF = w1.shape[2]
    TF = 256
    nf = FF // TF
    RT = rmax // MOE_TR

    def fidx(t, f, nv):
        return jnp.where(nv[t] > 0, f, nf - 1)

    return pl.pallas_call(
        functools.partial(_moe_ffn_kernel, nf=nf),
        grid_spec=pltpu.PrefetchScalarGridSpec(
            num_scalar_prefetch=3, grid=(RT, nf),
            in_specs=[pl.BlockSpec((MOE_TR, D), lambda t, f, ti, e, nv: (ti[t], 0)),
                      pl.BlockSpec((None, D, TF), lambda t, f, ti, e, nv: (e[t], 0, fidx(t, f, nv))),
                      pl.BlockSpec((None, D, TF), lambda t, f, ti, e, nv: (e[t], 0, fidx(t, f, nv))),
                      pl.BlockSpec((None, TF, D), lambda t, f, ti, e, nv: (e[t], fidx(t, f, nv), 0))],
            out_specs=pl.BlockSpec((MOE_TR, D), lambda t, f, ti, e, nv: (ti[t], 0)),
            scratch_shapes=[pltpu.VMEM((MOE_TR, D), F32)]),
        out_shape=jax.ShapeDtypeStruct((rmax, D), BF16),
        compiler_params=_params("arbitrary", "arbitrary"),
        name="moe_ffn",
    )(*plan, xs, w1, w3, w2)


def _moe_combine_kernel(s_ref, q_ref, slot_ok_ref, valid_ref, first_ref, last_ref, pos_ref, *refs, ntp):
    ys_refs = refs[:MOE_CG]
    xp_ref, xs_ref, gate_ref, fg_ref, op_ref, os_ref, acc_scr = refs[MOE_CG:]
    p = pl.program_id(0)

    @pl.when(valid_ref[p] == 1)
    def _():
        rows = ys_refs[0].shape[0]
        sels = []
        for k in range(MOE_CG):
            col = (lax.broadcasted_iota(jnp.int32, (1, rows), 1) + q_ref[MOE_CG * p + k] * rows).astype(F32)
            col = jnp.where(slot_ok_ref[MOE_CG * p + k] == 1, col, -1.0)
            sels.append((jnp.where(pos_ref[:, 0:1] == col, pos_ref[:, 2:3], 0.0)
                         + jnp.where(pos_ref[:, 1:2] == col, pos_ref[:, 3:4], 0.0)).astype(BF16))
        sel = jnp.concatenate(sels, axis=1)
        ys = jnp.concatenate([r[...] for r in ys_refs], axis=0)

        @pl.when(first_ref[p] == 1)
        def _():
            acc_scr[...] = jnp.zeros_like(acc_scr)

        acc_scr[...] += _dot(sel, ys)

        @pl.when(last_ref[p] == 1)
        def _():
            s = s_ref[p]
            x = jnp.where(s < ntp, xp_ref[...], xs_ref[...])
            y = x + gate_ref[...] * acc_scr[...]
            out = y * lax.rsqrt(jnp.mean(y * y, axis=-1, keepdims=True) + EPS) * fg_ref[...]

            @pl.when(s < ntp)
            def _():
                op_ref[...] = out

            @pl.when(s >= ntp)
            def _():
                os_ref[...] = out


def _moe_combine(ys, pos_cols, plan, xp, xs, mod, final_g, rows_per_group):
    Np, D = xp.shape
    Ns = xs.shape[0]
    SB = MOE_SB
    ntp = Np // SB
    nsteps = plan[0].shape[0]

    def tile_spec(k):
        return pl.BlockSpec((MOE_TRG, D), lambda p, s, q, *_: (q[MOE_CG * p + k], 0))

    def tok_p(p, s, *_):
        return (jnp.minimum(s[p], ntp - 1), 0)

    def tok_s(p, s, *_):
        return (jnp.maximum(s[p] - ntp, 0), 0)

    def gate_idx(p, s, *_):
        return (jnp.where(s[p] < ntp, 0, 1 + ((s[p] - ntp) * SB) // rows_per_group), 5, 0, 0)

    return pl.pallas_call(
        functools.partial(_moe_combine_kernel, ntp=ntp),
        grid_spec=pltpu.PrefetchScalarGridSpec(
            num_scalar_prefetch=6, grid=(nsteps,),
            in_specs=[pl.BlockSpec((SB, SUB), lambda p, s, q, *_: (s[p], 0))]
            + [tile_spec(k) for k in range(MOE_CG)]
            + [pl.BlockSpec((SB, D), tok_p),
                      pl.BlockSpec((SB, D), tok_s),
                      pl.BlockSpec((None, None, 1, D), gate_idx),
                      pl.BlockSpec((1, D), lambda p, *_: (0, 0))],
            out_specs=[pl.BlockSpec((SB, D), tok_p), pl.BlockSpec((SB, D), tok_s)],
            scratch_shapes=[pltpu.VMEM((SB, D), F32)]),
        out_shape=[jax.ShapeDtypeStruct((Np, D), F32), jax.ShapeDtypeStruct((Ns, D), F32)],
        compiler_params=_params("arbitrary"),
        name="moe_combine",
    )(*plan, pos_cols, *([ys] * MOE_CG), xp, xs, mod, final_g.reshape(1, D))


def _moe(xp, xs, g, mod, router_w, w1, w3, w2, final_g, rows_per_group):
    N = xp.shape[0] + xs.shape[0]
    h, info, info_t, cum = _moe_route(xp, xs, g, mod, router_w, rows_per_group)
    pos_cols, pos_rows, gather_plan, combine_plan, ffn_plan, rmax = _moe_plan(info, info_t, cum, N)
    x_sorted = _moe_gather(h, pos_rows, gather_plan, rmax)
    y_sorted = _moe_ffn(x_sorted, ffn_plan, w1, w3, w2)
    return _moe_combine(y_sorted, pos_cols, combine_plan, xp, xs, mod, final_g, rows_per_group)


def _gla_levels(C):
    lv, c = [], C // 2
    while c >= SUB:
        lv.append(c)
        c //= 2
    return lv


def _gla_tables(C):
    levels = _gla_levels(C)
    nr = 2 + 2 * len(levels)
    mat = np.zeros((2, nr * C, C), np.float32)
    code = np.zeros((2, C, C), np.int32)
    for d in range(2):
        p = np.arange(C) if d == 0 else C - 1 - np.arange(C)
        pi, pj = p[:, None], p[None, :]
        mat[d, 0:C] = pj <= pi
        mat[d, C:2 * C] = pj > pi
        code[d] = np.where((pj <= pi) & (pi // SUB == pj // SUB), 1, 0)
        for lv, c in enumerate(levels):
            blk = pi // c
            later = blk % 2 == 1
            mat[d, (2 + 2 * lv) * C:(3 + 2 * lv) * C] = later & (pj > blk * c - 1) & (pj <= pi)
            mat[d, (3 + 2 * lv) * C:(4 + 2 * lv) * C] = (~later) & (pj > pi) & (pj <= (blk + 1) * c - 1)
            pair = (pi // (2 * c) == pj // (2 * c)) & (pi // c != pj // c) & (pj <= pi)
            code[d] = np.where(pair, 2 + lv, code[d])
    ones = np.zeros((SUB * LANES, C), np.float32)
    for jj in range(SUB):
        ones[jj * LANES:(jj + 1) * LANES, jj::SUB] = 1.0
    return jnp.asarray(mat, dtype=BF16), jnp.asarray(code), jnp.asarray(ones, dtype=BF16)


def _bcast_sublane(x, jj):
    r, w = x.shape
    x3 = x.reshape(r // SUB, SUB, w)
    return jnp.broadcast_to(x3[:, jj:jj + 1, :], x3.shape).reshape(r, w)


def _t128(x):
    r, w = x.shape
    if w > LANES:
        return jnp.concatenate([x[:, i:i + LANES].T for i in range(0, w, LANES)], axis=0)
    return jnp.concatenate([x[i:i + LANES, :].T for i in range(0, r, LANES)], axis=1)


def _gla_kernel(q_ref, k_ref, v_ref, g_ref, lr_ref, wg_ref, ba_ref, mat_ref, code_ref, ones_ref,
                s0f_ref, s0b_ref, ng_ref, o_ref, sf_ref, sb_ref, st_scr, of_scr, *, n, C, G):
    d = pl.program_id(1)
    c = pl.program_id(2)
    levels = _gla_levels(C)

    @pl.when(jnp.logical_and(c == 0, d == 0))
    def _():
        for bb in range(G):
            for h in range(C_H):
                st_scr[bb, h] = _t128(s0f_ref[bb, h])

    @pl.when(jnp.logical_and(c == 0, d == 1))
    def _():
        for bb in range(G):
            for h in range(C_H):
                st_scr[bb, h] = _t128(s0b_ref[bb, h])

    mat = mat_ref[...]
    code = code_ref[...]
    ones = ones_ref[...]
    cums = []
    for bb in range(G):
        xg = jnp.dot(lr_ref[bb], wg_ref[...], precision=HIGHEST, preferred_element_type=F32) + ba_ref[...]
        la = (jnp.minimum(xg, 0.0) - jnp.log1p(jnp.exp(-jnp.abs(xg)))) * (LOG2E / C_TAU)
        hi = la.astype(BF16)
        lo = (la - hi.astype(F32)).astype(BF16)
        cums.append(_dot(mat, hi) + _dot(mat, lo))

    def prepare(bb, h):
        cum = cums[bb]
        ks = slice(h * C_DK, (h + 1) * C_DK)
        qh = q_ref[bb, :, ks].astype(F32) * (C_DK ** -0.5)
        kh = k_ref[bb, :, ks].astype(F32)
        b = cum[0:C, ks]
        b_rest = cum[C:2 * C, ks]
        ps = []
        for jj in range(SUB):
            dec = jnp.exp2(jnp.minimum(b - _bcast_sublane(b, jj), 0.0))
            ps.append((qh * _bcast_sublane(kh, jj) * dec).astype(BF16))
        lv_ops = []
        for lv in range(len(levels)):
            eq = cum[(2 + 2 * lv) * C:(3 + 2 * lv) * C, ks]
            ek = cum[(3 + 2 * lv) * C:(4 + 2 * lv) * C, ks]
            lv_ops.append(((qh * jnp.exp2(eq)).astype(BF16), (kh * jnp.exp2(ek)).astype(BF16)))
        qe = (qh * jnp.exp2(b)).astype(BF16)
        ke = (kh * jnp.exp2(b_rest)).astype(BF16)
        e_end = jnp.exp2(b[0:1, :] + b_rest[0:1, :])
        return jnp.concatenate(ps, axis=1), lv_ops, qe, ke, e_end

    def contract(bb, h, prep):
        pcat, lv_ops, qe, ke, e_end = prep
        vh = v_ref[bb, :, h * C_DV:(h + 1) * C_DV].astype(F32)
        att = jnp.where(code == 1, _dot(pcat, ones), 0.0)
        for lv, (qs, ks_) in enumerate(lv_ops):
            att = jnp.where(code == 2 + lv, _dot_nt(qs, ks_), att)
        st = st_scr[bb, h]
        o = _dot(att.astype(BF16), vh.astype(BF16)) + _dot_nt(qe, st.astype(BF16))
        st_scr[bb, h] = e_end * st + _dot(_t128(vh).astype(BF16), ke)
        return o

    units = [(bb, h) for h in range(C_H) for bb in range(G)]
    outs = {}
    prep = prepare(*units[0])
    for idx, (bb, h) in enumerate(units):
        nxt = prepare(*units[idx + 1]) if idx + 1 < len(units) else None
        outs[(bb, h)] = contract(bb, h, prep)
        prep = nxt
    o_all = [jnp.concatenate([outs[(bb, h)] for h in range(C_H)], axis=-1) for bb in range(G)]

    @pl.when(d == 0)
    def _():
        for bb in range(G):
            of_scr[bb, c] = o_all[bb]

    @pl.when(d == 1)
    def _():
        for bb in range(G):
            tot = o_all[bb] + of_scr[bb, n - 1 - c]
            res = []
            for h in range(C_H):
                sl = slice(h * C_DV, (h + 1) * C_DV)
                t = tot[:, sl]
                y = t * lax.rsqrt(jnp.mean(t * t, axis=-1, keepdims=True) + EPS) * ng_ref[:, sl]
                res.append(y * _silu(g_ref[bb, :, sl].astype(F32)))
            o_ref[bb] = jnp.concatenate(res, axis=-1).astype(o_ref.dtype)

    @pl.when(jnp.logical_and(c == n - 1, d == 0))
    def _():
        for bb in range(G):
            for h in range(C_H):
                sf_ref[bb, h] = _t128(st_scr[bb, h])

    @pl.when(jnp.logical_and(c == n - 1, d == 1))
    def _():
        for bb in range(G):
            for h in range(C_H):
                sb_ref[bb, h] = _t128(st_scr[bb, h])


def _gla(z, zg, B, T, w_a2, b_a, s0f, s0b, norm_g):
    C = GLA_CHUNK
    G = 2
    assert B % G == 0 and T % C == 0
    n = T // C
    HK = C_H * C_DK
    HV = C_H * C_DV
    mat, code, ones = _gla_tables(C)
    nr = mat.shape[1] // C
    wg = jnp.zeros((2, LANES, HK), F32)
    for dr in range(2):
        wg = wg.at[dr, dr * C_RANK:(dr + 1) * C_RANK, :].set(w_a2[dr])
    z3 = z.reshape(B, T, z.shape[1])
    zg3 = zg.reshape(B, T, zg.shape[1])

    def chunk(d, c):
        return c + d * (n - 1 - 2 * c)

    st_spec = pl.BlockSpec((G, C_H, C_DK, C_DV), lambda b, d, c: (b, 0, 0, 0))
    st_shape = jax.ShapeDtypeStruct((B, C_H, C_DK, C_DV), F32)
    o, sf, sb = pl.pallas_call(
        functools.partial(_gla_kernel, n=n, C=C, G=G),
        grid=(B // G, 2, n),
        in_specs=[pl.BlockSpec((G, C, HK), lambda b, d, c: (b, chunk(d, c), 0)),
                  pl.BlockSpec((G, C, HK), lambda b, d, c: (b, chunk(d, c), 1)),
                  pl.BlockSpec((G, C, HV), lambda b, d, c: (b, chunk(d, c), 1)),
                  pl.BlockSpec((G, C, HV), lambda b, d, c: (b, chunk(d, c), 2)),
                  pl.BlockSpec((G, C, LANES), lambda b, d, c: (b, chunk(d, c), 0)),
                  pl.BlockSpec((None, LANES, HK), lambda b, d, c: (d, 0, 0)),
                  pl.BlockSpec((None, 1, HK), lambda b, d, c: (d, 0, 0)),
                  pl.BlockSpec((None, nr * C, C), lambda b, d, c: (d, 0, 0)),
                  pl.BlockSpec((None, C, C), lambda b, d, c: (d, 0, 0)),
                  pl.BlockSpec((SUB * LANES, C), lambda b, d, c: (0, 0)),
                  st_spec, st_spec,
                  pl.BlockSpec((1, HV), lambda b, d, c: (0, 0))],
        out_specs=[pl.BlockSpec((G, C, HV), lambda b, d, c: (b, (n - 1) - d * c, 0)),
                   st_spec, st_spec],
        out_shape=[jax.ShapeDtypeStruct((B, T, HV), BF16), st_shape, st_shape],
        scratch_shapes=[pltpu.VMEM((G, C_H, C_DV, C_DK), F32), pltpu.VMEM((G, n, C, HV), F32)],
        compiler_params=_params("arbitrary", "arbitrary", "arbitrary"),
        name="gla",
    )(z3, z3, z3, z3, zg3, wg, b_a.reshape(2, 1, HK), mat, code, ones, s0f, s0b, norm_g.reshape(1, HV))
    return o.reshape(B * T, HV), sf, sb


def _run_stream(x, B, T, mods, ctx, p):
    N, D = x.shape
    rpg = N // mods[0].shape[0]
    TM = min(2048, rpg)
    nb = (B_H + 2 * B_HKV) * B_HD

    w_in = p['even_w_in'][0]
    z, zb = _norm_mm(x, p['norm1_g'][0], mods[0], (0, 1), w_in, w_in, (nb, MIX_MAIN // nb), TM, rpg)
    if ctx is None:
        s0 = jnp.zeros((B, A_H, A_DK, A_DV), F32)
        a_f0, a_b0, cache_k, cache_v = s0, s0, None, None
    else:
        cache_k, cache_v, a_f0, a_b0 = ctx[0], ctx[1], ctx[2], ctx[3]
    o_a, a_sf, a_sb = _retention(z, B, T, p['a_log_gamma'][0], a_f0, a_b0, p['a_norm_g'][0])
    qpad, k_norm, k_rot, v_bf = _bprep(zb, T, p['b_q_g'][0], p['b_k_g'][0], rope=ctx is not None)
    o_b = _attention(qpad, k_rot, v_bf, B, T, cache_k, cache_v)
    x = _proj_res(x, mods[0], 2, [o_a, o_b], p['even_w_out'][0], rpg)
    x = _ffn(x, p['norm2_g'][0], mods[0], p['ff_w1'][0], p['ff_w3'][0], p['ff_w2'][0], rpg)

    w_in = p['odd_w_in'][0]
    w_gate = jnp.pad(w_in[:, MIX_MAIN:], ((0, 0), (0, LANES - 2 * C_RANK)))
    z1, z1g = _norm_mm(x, p['norm1_g'][1], mods[1], (0, 1), w_in, w_gate, (LANES, 0), TM, rpg)
    if ctx is None:
        s0 = jnp.zeros((B, C_H, C_DK, C_DV), F32)
        c_f0, c_b0 = s0, s0
    else:
        c_f0, c_b0 = ctx[4], ctx[5]
    o_c, c_sf, c_sb = _gla(z1, z1g, B, T, p['c_w_a2'][0], p['c_b_a'][0], c_f0, c_b0, p['c_norm_g'][0])
    x = _proj_res(x, mods[1], 2, [o_c], p['odd_w_out'][0], rpg)
    v_raw = zb[:, (B_H + B_HKV) * B_HD:]
    return x, (k_norm, v_raw, a_sf, a_sb, c_sf, c_sb)


def kernel(x_prompt, x_sample, c, cache_b_k, cache_b_v, state_a_fwd, state_a_bwd, state_c_fwd, state_c_bwd,
           c_ctx, w_mod, b_mod, norm1_g, norm2_g, final_g, even_w_in, even_w_out, a_log_gamma, a_norm_g,
           b_q_g, b_k_g, odd_w_in, c_w_a2, c_b_a, c_norm_g, odd_w_out, ff_w1, ff_w3, ff_w2,
           router_w, moe_w1, moe_w3, moe_w2):
    Bp, Tp, D = x_prompt.shape
    Bs, Ts, _ = x_sample.shape
    L = w_mod.shape[0]
    assert L == 2 and even_w_in.shape[0] == 1 and odd_w_in.shape[0] == 1
    p = dict(norm1_g=norm1_g, norm2_g=norm2_g, final_g=final_g, even_w_in=even_w_in, even_w_out=even_w_out,
             a_log_gamma=a_log_gamma, a_norm_g=a_norm_g, b_q_g=b_q_g, b_k_g=b_k_g, odd_w_in=odd_w_in,
             c_w_a2=c_w_a2, c_b_a=c_b_a, c_norm_g=c_norm_g, odd_w_out=odd_w_out, ff_w1=ff_w1, ff_w3=ff_w3,
             ff_w2=ff_w2, router_w=router_w, moe_w1=moe_w1, moe_w3=moe_w3, moe_w2=moe_w2)

    rows = 8
    conds = jnp.concatenate([c_ctx[None, :], c, jnp.zeros((rows - 1 - Bs, D), F32)], axis=0)
    mod = _modulation(conds, w_mod, b_mod).reshape(L, rows, 6, 1, D)
    mods_p = [mod[l, 0:1] for l in range(L)]
    mods_s = [mod[l, 1:1 + Bs] for l in range(L)]

    x_p, kept = _run_stream(x_prompt.reshape(Bp * Tp, D), Bp, Tp, mods_p, None, p)
    nk = B_HKV * B_HD
    ctx = (cache_b_k[:, 0].reshape(Bs, -1, nk), cache_b_v[:, 0].reshape(Bs, -1, nk),
           state_a_fwd[:, 0], state_a_bwd[:, 0], state_c_fwd[:, 0], state_c_bwd[:, 0])
    x_s, _ = _run_stream(x_sample.reshape(Bs * Ts, D), Bs, Ts, mods_s, ctx, p)
    y_p, y_s = _moe(x_p, x_s, norm2_g[1], mod[1, 0:1 + Bs], router_w[0], moe_w1[0], moe_w3[0], moe_w2[0],
                    final_g, Ts)

    k_norm, v_raw, a_sf, a_sb, c_sf, c_sb = kept
    return (y_p.reshape(Bp, Tp, D), y_s.reshape(Bs, Ts, D),
            k_norm.reshape(Bp, 1, Tp, B_HKV, B_HD), v_raw.reshape(Bp, 1, Tp, B_HKV, B_HD),
            a_sf[:, None], a_sb[:, None], c_sf[:, None], c_sb[:, None])
```

```python
import functools

import numpy as np
import jax
import jax.numpy as jnp
from jax import lax
from jax.experimental import pallas as pl
from jax.experimental.pallas import tpu as pltpu

F32 = jnp.float32
BF16 = jnp.bfloat16
EPS = 1e-6
HIGHEST = lax.Precision.HIGHEST
LOG2E = 1.4426950408889634

VMEM_LIMIT_BYTES = 56 * 1024 * 1024

A_H, A_DK, A_DV = 4, 128, 256
B_H, B_HKV, B_HD = 8, 2, 64
C_H, C_DK, C_DV, C_RANK = 4, 128, 256, 16
C_TAU = 16.0
GRID_W = 64
ROPE_THETA = 10000.0
N_EXPERTS = 8
LANES = 128
SUB = 8
RET_CHUNK = 128
GLA_CHUNK = 128
Q_TILE = 128


def _params(*sem):
    return pltpu.CompilerParams(dimension_semantics=sem, vmem_limit_bytes=VMEM_LIMIT_BYTES)


def _dot(a, b):
    return jnp.dot(a, b, preferred_element_type=F32)


def _dot_nt(a, b):
    return lax.dot_general(a, b, (((1,), (1,)), ((), ())), preferred_element_type=F32)


def _silu(x):
    return x * jax.nn.sigmoid(x)


def _norm_mod(x, g, sh, sc):
    r = lax.rsqrt(jnp.mean(x * x, axis=-1, keepdims=True) + EPS)
    return (x * r * g) * (1.0 + sc) + sh


def _mod_kernel(c_ref, w_ref, b_ref, o_ref):
    c = c_ref[...]
    o_ref[...] = jnp.dot(_silu(c), w_ref[...], precision=HIGHEST, preferred_element_type=F32) + b_ref[...]


def _modulation(conds, w_mod, b_mod):
    L, D, D6 = w_mod.shape
    R = conds.shape[0]
    TN = 1024
    return pl.pallas_call(
        _mod_kernel,
        grid=(L, D6 // TN),
        in_specs=[pl.BlockSpec((R, D), lambda l, j: (0, 0)),
                  pl.BlockSpec((None, D, TN), lambda l, j: (l, 0, j)),
                  pl.BlockSpec((None, 1, TN), lambda l, j: (l, 0, j))],
        out_specs=pl.BlockSpec((None, R, TN), lambda l, j: (l, 0, j)),
        out_shape=jax.ShapeDtypeStruct((L, R, D6), F32),
        compiler_params=_params("arbitrary", "arbitrary"),
        name="modulation",
    )(conds, w_mod, b_mod.reshape(L, 1, D6))


def _mod_spec(part, D, TM, rows_per_group, axis):
    def idx(*g):
        return ((g[axis] * TM) // rows_per_group, part, 0, 0)
    return pl.BlockSpec((None, None, 1, D), idx)


MIX_MAIN = A_H * (2 * A_DK + 2 * A_DV)
MIX_TN = 768


def _norm_mm_kernel(x_ref, g_ref, sh_ref, sc_ref, w_ref, we_ref, o_ref, oe_ref, h_scr, *, nmain):
    j = pl.program_id(1)

    @pl.when(j == 0)
    def _():
        h_scr[...] = _norm_mod(x_ref[...], g_ref[...], sh_ref[...], sc_ref[...]).astype(BF16)

    @pl.when(j < nmain)
    def _():
        o_ref[...] = _dot(h_scr[...], w_ref[...].astype(BF16)).astype(o_ref.dtype)

    @pl.when(j == nmain)
    def _():
        oe_ref[...] = _dot(h_scr[...], we_ref[...].astype(BF16))


def _norm_mm(x, g, mod, parts, w, w_extra, extra_block, TM, rows_per_group):
    N, D = x.shape
    nmain = MIX_MAIN // MIX_TN
    WE = extra_block[0]
    return pl.pallas_call(
        functools.partial(_norm_mm_kernel, nmain=nmain),
        grid=(N // TM, nmain + 1),
        in_specs=[pl.BlockSpec((TM, D), lambda i, j: (i, 0)),
                  pl.BlockSpec((1, D), lambda i, j: (0, 0)),
                  _mod_spec(parts[0], D, TM, rows_per_group, 0),
                  _mod_spec(parts[1], D, TM, rows_per_group, 0),
                  pl.BlockSpec((D, MIX_TN), lambda i, j: (0, jnp.minimum(j, nmain - 1))),
                  pl.BlockSpec((D, WE), lambda i, j: (0, extra_block[1]))],
        out_specs=[pl.BlockSpec((TM, MIX_TN), lambda i, j: (i, jnp.minimum(j, nmain - 1))),
                   pl.BlockSpec((TM, WE), lambda i, j: (i, 0))],
        out_shape=[jax.ShapeDtypeStruct((N, MIX_MAIN), BF16), jax.ShapeDtypeStruct((N, WE), F32)],
        scratch_shapes=[pltpu.VMEM((TM, D), BF16)],
        compiler_params=_params("arbitrary", "arbitrary"),
        name="norm_mm",
    )(x, g.reshape(1, D), mod, mod, w, w_extra)


def _ret_kernel(lg_ref, q_ref, k_ref, v_ref, ag_ref, s0f_ref, s0b_ref, ng_ref,
                o_ref, sf_ref, sb_ref, s_scr, of_scr, *, n, C, G):
    d = pl.program_id(1)
    c = pl.program_id(2)

    @pl.when(jnp.logical_and(c == 0, d == 0))
    def _():
        s_scr[...] = s0f_ref[...]

    @pl.when(jnp.logical_and(c == 0, d == 1))
    def _():
        s_scr[...] = s0b_ref[...]

    df = d.astype(F32)
    sgn = 1.0 - 2.0 * df
    ii = lax.broadcasted_iota(jnp.int32, (C, C), 0).astype(F32)
    jj = lax.broadcasted_iota(jnp.int32, (C, C), 1).astype(F32)
    dd = (ii - jj) * sgn
    feeds = dd >= 0.0
    ddc = jnp.maximum(dd, 0.0)
    ri = lax.broadcasted_iota(jnp.int32, (C, 1), 0).astype(F32)
    pos_q = (ri + 1.0) + df * (C - 2.0 * ri - 1.0)
    pos_k = (C - 1.0 - ri) + df * (2.0 * ri - C + 1.0)
    chunk_len = jnp.full((1, A_DV), float(C), F32)

    outs = [[] for _ in range(G)]
    for h in range(A_H):
        lg = lg_ref[d, h]
        dmask = jnp.where(feeds, jnp.exp2(lg * ddc), 0.0)
        q_dec = jnp.exp2(lg * pos_q)
        k_dec = jnp.exp2(lg * pos_k)
        c_dec = jnp.exp2(lg * chunk_len)
        for bb in range(G):
            qh = q_ref[bb, :, h * A_DK:(h + 1) * A_DK].astype(F32) * (A_DK ** -0.5)
            kh = k_ref[bb, :, h * A_DK:(h + 1) * A_DK].astype(F32)
            vh = v_ref[bb, :, h * A_DV:(h + 1) * A_DV].astype(BF16)
            s = s_scr[bb, h]
            att = _dot_nt(qh.astype(BF16), kh.astype(BF16)) * dmask
            o = _dot(att.astype(BF16), vh) + _dot((qh * q_dec).astype(BF16), s.astype(BF16))
            kd = kh * k_dec
            s_scr[bb, h] = c_dec * s + _dot(kd.T.astype(BF16), vh)
            outs[bb].append(o)
    o_all = [jnp.concatenate(o, axis=-1) for o in outs]

    @pl.when(d == 0)
    def _():
        for bb in range(G):
            of_scr[bb, c] = o_all[bb]

    @pl.when(d == 1)
    def _():
        for bb in range(G):
            tot = o_all[bb] + of_scr[bb, n - 1 - c]
            res = []
            for h in range(A_H):
                sl = slice(h * A_DV, (h + 1) * A_DV)
                t = tot[:, sl]
                dev = t - jnp.mean(t, axis=-1, keepdims=True)
                y = dev * lax.rsqrt(jnp.mean(dev * dev, axis=-1, keepdims=True) + EPS) * ng_ref[:, sl]
                res.append(y * _silu(ag_ref[bb, :, sl].astype(F32)))
            o_ref[bb] = jnp.concatenate(res, axis=-1).astype(o_ref.dtype)

    @pl.when(jnp.logical_and(c == n - 1, d == 0))
    def _():
        sf_ref[...] = s_scr[...]

    @pl.when(jnp.logical_and(c == n - 1, d == 1))
    def _():
        sb_ref[...] = s_scr[...]


def _retention(z, B, T, log_gamma, s0f, s0b, norm_g):
    C = RET_CHUNK
    G = 2
    assert B % G == 0 and T % C == 0
    n = T // C
    HK = A_H * A_DK
    HV = A_H * A_DV
    z3 = z.reshape(B, T, z.shape[1])

    def chunk(d, c):
        return c + d * (n - 1 - 2 * c)

    st_spec = pl.BlockSpec((G, A_H, A_DK, A_DV), lambda b, d, c: (b, 0, 0, 0))
    st_shape = jax.ShapeDtypeStruct((B, A_H, A_DK, A_DV), F32)
    o, sf, sb = pl.pallas_call(
        functools.partial(_ret_kernel, n=n, C=C, G=G),
        grid=(B // G, 2, n),
        in_specs=[pl.BlockSpec(memory_space=pltpu.SMEM),
                  pl.BlockSpec((G, C, HK), lambda b, d, c: (b, chunk(d, c), 0)),
                  pl.BlockSpec((G, C, HK), lambda b, d, c: (b, chunk(d, c), 1)),
                  pl.BlockSpec((G, C, HV), lambda b, d, c: (b, chunk(d, c), 1)),
                  pl.BlockSpec((G, C, HV), lambda b, d, c: (b, chunk(d, c), 2)),
                  st_spec, st_spec,
                  pl.BlockSpec((1, HV), lambda b, d, c: (0, 0))],
        out_specs=[pl.BlockSpec((G, C, HV), lambda b, d, c: (b, (n - 1) - d * c, 0)),
                   st_spec, st_spec],
        out_shape=[jax.ShapeDtypeStruct((B, T, HV), BF16), st_shape, st_shape],
        scratch_shapes=[pltpu.VMEM((G, A_H, A_DK, A_DV), F32), pltpu.VMEM((G, n, C, HV), F32)],
        compiler_params=_params("arbitrary", "arbitrary", "arbitrary"),
        name="retention",
    )(log_gamma * LOG2E, z3, z3, z3, z3, s0f, s0b, norm_g.reshape(1, HV))
    return o.reshape(B * T, HV), sf, sb


def _group_sum_matrix(width, group):
    i = np.arange(width)
    return jnp.asarray((i[:, None] // group == i[None, :] // group).astype(np.float32), dtype=BF16)


def _q_pad_matrix():
    m = np.zeros((B_H * B_HD, B_H * LANES), np.float32)
    g = B_H // B_HKV
    for h in range(B_H):
        for t in range(B_HD):
            m[h * B_HD + t, h * LANES + (h // g) * B_HD + t] = 1.0
    return jnp.asarray(m, dtype=BF16)


def _rope_tables(T):
    rows = T // GRID_W
    row = np.repeat(np.arange(rows, dtype=np.float64), GRID_W)
    col = np.tile(np.arange(GRID_W, dtype=np.float64), rows)
    nq = B_HD // 4
    inv = ROPE_THETA ** (-np.arange(nq, dtype=np.float64) / nq)
    ang = np.concatenate([row[:, None] * inv, col[:, None] * inv], axis=-1)
    cos = np.repeat(np.cos(ang), 2, axis=-1)
    sin = np.repeat(np.sin(ang), 2, axis=-1)
    sign = np.tile(np.array([-1.0, 1.0]), B_HD // 2)
    reps = LANES // B_HD
    return (jnp.asarray(np.tile(cos, (1, reps)), dtype=F32),
            jnp.asarray(np.tile(sin * sign, (1, reps)), dtype=F32))


def _group_rmsnorm(x, gsum, g):
    x2 = x * x
    hi = x2.astype(BF16)
    lo = (x2 - hi.astype(F32)).astype(BF16)
    ss = _dot(hi, gsum) + _dot(lo, gsum)
    return x * lax.rsqrt(ss * (1.0 / B_HD) + EPS) * g


def _rotate_pairs(x, cos, sin_signed):
    n = x.shape[1]
    lane = lax.broadcasted_iota(jnp.int32, x.shape, 1)
    partner = jnp.where(lane % 2 == 0, pltpu.roll(x, n - 1, 1), pltpu.roll(x, 1, 1))
    reps = n // LANES
    if reps > 1:
        cos = jnp.concatenate([cos] * reps, axis=1)
        sin_signed = jnp.concatenate([sin_signed] * reps, axis=1)
    return x * cos + partner * sin_signed


def _bprep_kernel(z_ref, qg_ref, kg_ref, cos_ref, sin_ref, gq_ref, gk_ref, pad_ref,
                  qpad_ref, kn_ref, kr_ref, vb_ref, *, rope):
    nq = B_H * B_HD
    nk = B_HKV * B_HD
    qn = _group_rmsnorm(z_ref[:, 0:nq], gq_ref[...], qg_ref[...])
    kn = _group_rmsnorm(z_ref[:, nq:nq + nk], gk_ref[...], kg_ref[...])
    kn_ref[...] = kn
    if rope:
        qn = _rotate_pairs(qn, cos_ref[...], sin_ref[...])
        kn = _rotate_pairs(kn, cos_ref[...], sin_ref[...])
    kr_ref[...] = kn.astype(BF16)
    vb_ref[...] = z_ref[:, nq + nk:nq + 2 * nk].astype(BF16)
    qs = (qn * (B_HD ** -0.5 * LOG2E)).astype(BF16)
    qpad_ref[...] = _dot(qs, pad_ref[...]).astype(BF16)


def _bprep(z, T, q_g, k_g, rope):
    N = z.shape[0]
    TM = min(512, T)
    nq = B_H * B_HD
    nk = B_HKV * B_HD
    width = nq + 2 * nk
    assert z.shape[1] == width
    cos, sin = _rope_tables(T if rope else TM)
    nt = T // TM if rope else 1
    const = lambda i: (0, 0)
    return pl.pallas_call(
        functools.partial(_bprep_kernel, rope=rope),
        grid=(N // TM,),
        in_specs=[pl.BlockSpec((TM, width), lambda i: (i, 0)),
                  pl.BlockSpec((1, nq), const),
                  pl.BlockSpec((1, nk), const),
                  pl.BlockSpec((TM, LANES), lambda i: (i % nt, 0)),
                  pl.BlockSpec((TM, LANES), lambda i: (i % nt, 0)),
                  pl.BlockSpec((nq, nq), const),
                  pl.BlockSpec((nk, nk), const),
                  pl.BlockSpec((nq, B_H * LANES), const)],
        out_specs=[pl.BlockSpec((TM, B_H * LANES), lambda i: (i, 0)),
                   pl.BlockSpec((TM, nk), lambda i: (i, 0)),
                   pl.BlockSpec((TM, nk), lambda i: (i, 0)),
                   pl.BlockSpec((TM, nk), lambda i: (i, 0))],
        out_shape=[jax.ShapeDtypeStruct((N, B_H * LANES), BF16),
                   jax.ShapeDtypeStruct((N, nk), F32),
                   jax.ShapeDtypeStruct((N, nk), BF16),
                   jax.ShapeDtypeStruct((N, nk), BF16)],
        compiler_params=_params("arbitrary"),
        name="attn_prep",
    )(z, jnp.tile(q_g, B_H).reshape(1, nq), jnp.tile(k_g, B_HKV).reshape(1, nk), cos, sin,
      _group_sum_matrix(nq, B_HD), _group_sum_matrix(nk, B_HD), _q_pad_matrix())


def _lane_fold(x, op):
    acc = x[:, 0:LANES]
    for j in range(1, x.shape[1] // LANES):
        acc = op(acc, x[:, j * LANES:(j + 1) * LANES])
    return acc


def _attn_kernel(*refs, has_cache, kc, nq):
    if has_cache:
        q_ref, k_ref, v_ref, ck_ref, cv_ref, o_ref, s_scr, m_scr, mprev_scr, l_scr, acc_scr = refs
        kcc = min(kc, ck_ref.shape[0])
        ncache = ck_ref.shape[0] // kcc
    else:
        q_ref, k_ref, v_ref, o_ref, s_scr, m_scr, mprev_scr, l_scr, acc_scr = refs
        kcc, ncache = kc, 0
    i = pl.program_id(1)
    tq = q_ref.shape[0]
    nlat = k_ref.shape[0] // kc

    def score(c, kblk):
        q = jnp.concatenate([q_ref[:, h * LANES:(h + 1) * LANES] for h in range(B_H)], axis=0)
        s = _dot_nt(q, kblk)
        s_scr[c, :, 0:kblk.shape[0]] = s
        m_scr[...] = jnp.maximum(m_scr[...], _lane_fold(s, jnp.maximum))

    def weight(c, vblk):
        s = s_scr[c, :, 0:vblk.shape[0]]
        mp = mprev_scr[...]
        ps = [jnp.exp2(s[:, j * LANES:(j + 1) * LANES] - mp) for j in range(vblk.shape[0] // LANES)]
        tot = ps[0]
        for pj in ps[1:]:
            tot = tot + pj
        l_scr[...] += tot
        acc_scr[...] += _dot(jnp.concatenate(ps, axis=1).astype(BF16), vblk)

    def run(do_weight, do_score):
        def unit(c, kblk, vblk):
            if do_weight:
                weight(c, vblk())
            if do_score:
                score(c, kblk())

        for c in range(ncache):
            unit(c, lambda: ck_ref[c * kcc:(c + 1) * kcc, :].astype(BF16),
                 lambda: cv_ref[c * kcc:(c + 1) * kcc, :].astype(BF16))

        def body(c, carry):
            rows = pl.ds(pl.multiple_of(c * kc, kc), kc)
            unit(ncache + c, lambda: k_ref[rows, :], lambda: v_ref[rows, :])
            return carry
        lax.fori_loop(0, nlat, body, 0)

    @pl.when(i < nq)
    def _():
        m_scr[...] = jnp.full(m_scr.shape, -jnp.inf, F32)

    @pl.when(i > 0)
    def _():
        l_scr[...] = jnp.zeros_like(l_scr)
        acc_scr[...] = jnp.zeros_like(acc_scr)

    @pl.when(i == 0)
    def _():
        run(False, True)

    @pl.when(jnp.logical_and(i > 0, i < nq))
    def _():
        run(True, True)

    @pl.when(i == nq)
    def _():
        run(True, False)

    @pl.when(i > 0)
    def _():
        r_all = acc_scr[...] / jnp.sum(l_scr[...], axis=-1, keepdims=True)
        g = B_H // B_HKV
        lane = lax.broadcasted_iota(jnp.int32, (tq, LANES), 1)
        outs = []
        for j in range(B_H // 2):
            pair = []
            for half in range(2):
                h = 2 * j + half
                r = r_all[h * tq:(h + 1) * tq, :]
                if h // g != half:
                    r = pltpu.roll(r, B_HD, 1)
                pair.append(r)
            outs.append(jnp.where(lane < B_HD, pair[0], pair[1]))
        o_ref[...] = jnp.concatenate(outs, axis=-1).astype(o_ref.dtype)

    @pl.when(i < nq)
    def _():
        mprev_scr[...] = jnp.broadcast_to(jnp.max(m_scr[...], axis=-1, keepdims=True), mprev_scr.shape)


def _attention(qpad, kr, vb, B, T, cache_k, cache_v):
    has_cache = cache_k is not None
    TQ = Q_TILE
    nq = T // TQ
    nk = B_HKV * B_HD
    in_specs = [pl.BlockSpec((TQ, B_H * LANES), lambda b, i: (b * nq + jnp.minimum(i, nq - 1), 0)),
                pl.BlockSpec((T, nk), lambda b, i: (b, 0)),
                pl.BlockSpec((T, nk), lambda b, i: (b, 0))]
    args = [qpad, kr, vb]
    kc = min(1024, T)
    nchunks = T // kc
    if has_cache:
        P = cache_k.shape[1]
        assert P % min(kc, P) == 0
        nchunks += P // min(kc, P)
        in_specs += [pl.BlockSpec((None, P, nk), lambda b, i: (b, 0, 0))] * 2
        args += [cache_k, cache_v]
    R = B_H * TQ
    return pl.pallas_call(
        functools.partial(_attn_kernel, has_cache=has_cache, kc=kc, nq=nq),
        grid=(B, nq + 1),
        in_specs=in_specs,
        out_specs=pl.BlockSpec((TQ, B_H * B_HD), lambda b, i: (b * nq + jnp.maximum(i - 1, 0), 0)),
        out_shape=jax.ShapeDtypeStruct((B * T, B_H * B_HD), BF16),
        scratch_shapes=[pltpu.VMEM((nchunks, R, kc), F32)] + [pltpu.VMEM((R, LANES), F32)] * 4,
        compiler_params=_params("arbitrary", "arbitrary"),
        name="attention",
    )(*args)


def _proj_res_kernel(*refs, n_in):
    x_ref, gate_ref = refs[0], refs[1]
    o_refs = refs[2:2 + n_in]
    w_refs = refs[2 + n_in:2 + 2 * n_in]
    out_ref = refs[2 + 2 * n_in]
    wbf_refs = refs[3 + 2 * n_in:]

    @pl.when(pl.program_id(0) == 0)
    def _():
        for w_ref, wbf_ref in zip(w_refs, wbf_refs):
            wbf_ref[...] = w_ref[...].astype(BF16)

    acc = _dot(o_refs[0][...], wbf_refs[0][...])
    for o_ref, wbf_ref in zip(o_refs[1:], wbf_refs[1:]):
        acc = acc + _dot(o_ref[...], wbf_ref[...])
    out_ref[...] = x_ref[...] + gate_ref[...] * acc


def _proj_res(x, mod, part, acts, w, rows_per_group):
    N, D = x.shape
    TM = min(1024, rows_per_group)
    n_in = len(acts)
    widths = [a.shape[1] for a in acts]
    offs = np.cumsum([0] + widths[:-1]).tolist()
    in_specs = [pl.BlockSpec((TM, D), lambda i: (i, 0)),
                _mod_spec(part, D, TM, rows_per_group, 0)]
    in_specs += [pl.BlockSpec((TM, wd), lambda i: (i, 0)) for wd in widths]
    in_specs += [pl.BlockSpec((wd, D), functools.partial(lambda i, blk: (blk, 0), blk=off // wd))
                 for wd, off in zip(widths, offs)]
    return pl.pallas_call(
        functools.partial(_proj_res_kernel, n_in=n_in),
        grid=(N // TM,),
        in_specs=in_specs,
        out_specs=pl.BlockSpec((TM, D), lambda i: (i, 0)),
        out_shape=jax.ShapeDtypeStruct((N, D), F32),
        scratch_shapes=[pltpu.VMEM((wd, D), BF16) for wd in widths],
        compiler_params=_params("arbitrary"),
        name="proj_residual",
    )(x, mod, *acts, *([w] * n_in))


def _ffn_kernel(x_ref, g_ref, sh_ref, sc_ref, gate_ref, w1_ref, w3_ref, w2_ref, out_ref, h_scr, acc_scr, *, nf):
    f = pl.program_id(1)

    @pl.when(f == 0)
    def _():
        h_scr[...] = _norm_mod(x_ref[...], g_ref[...], sh_ref[...], sc_ref[...]).astype(BF16)
        acc_scr[...] = jnp.zeros_like(acc_scr)

    h = h_scr[...]
    a = _dot(h, w1_ref[...].astype(BF16))
    b = _dot(h, w3_ref[...].astype(BF16))
    acc_scr[...] += _dot((_silu(a) * b).astype(BF16), w2_ref[...].astype(BF16))

    @pl.when(f == nf - 1)
    def _():
        out_ref[...] = x_ref[...] + gate_ref[...] * acc_scr[...]


def _ffn(x, g, mod, w1, w3, w2, rows_per_group):
    N, D = x.shape
    FF = w1.shape[1]
    TM, TF = min(1024, rows_per_group), 256
    nf = FF // TF
    return pl.pallas_call(
        functools.partial(_ffn_kernel, nf=nf),
        grid=(N // TM, nf),
        in_specs=[pl.BlockSpec((TM, D), lambda i, f: (i, 0)),
                  pl.BlockSpec((1, D), lambda i, f: (0, 0)),
                  _mod_spec(3, D, TM, rows_per_group, 0),
                  _mod_spec(4, D, TM, rows_per_group, 0),
                  _mod_spec(5, D, TM, rows_per_group, 0),
                  pl.BlockSpec((D, TF), lambda i, f: (0, f)),
                  pl.BlockSpec((D, TF), lambda i, f: (0, f)),
                  pl.BlockSpec((TF, D), lambda i, f: (f, 0))],
        out_specs=pl.BlockSpec((TM, D), lambda i, f: (i, 0)),
        out_shape=jax.ShapeDtypeStruct((N, D), F32),
        scratch_shapes=[pltpu.VMEM((TM, D), BF16), pltpu.VMEM((TM, D), F32)],
        compiler_params=_params("arbitrary", "arbitrary"),
        name="ffn",
    )(x, g.reshape(1, D), mod, mod, mod, w1, w3, w2)


MOE_SB = 1024
MOE_SBG = 512
MOE_GG = 4
MOE_TRG = 256
MOE_TR = 2048
MOE_CG = 4


def _two_stream_specs(shape, ntp, ax=0):
    def idx_p(*g):
        return (jnp.minimum(g[ax], ntp - 1), 0)

    def idx_s(*g):
        return (jnp.maximum(g[ax] - ntp, 0), 0)
    return pl.BlockSpec(shape, idx_p), pl.BlockSpec(shape, idx_s)


def _pool_mod_spec(part, D, TM, ntp, rows_per_group):
    def idx(i, *_):
        return (jnp.where(i < ntp, 0, 1 + ((i - ntp) * TM) // rows_per_group), part, 0, 0)
    return pl.BlockSpec((None, None, 1, D), idx)


def _route_kernel(xp_ref, xs_ref, g_ref, sh_ref, sc_ref, rw_ref, tri_ref, h_ref, info_ref, infot_ref, cum_ref,
                  carry_scr, *, ntp):
    i = pl.program_id(0)

    @pl.when(i == 0)
    def _():
        carry_scr[...] = jnp.zeros_like(carry_scr)

    x = jnp.where(i < ntp, xp_ref[...], xs_ref[...])
    h = _norm_mod(x, g_ref[...], sh_ref[...], sc_ref[...])
    h_ref[...] = h.astype(BF16)
    lane = lax.broadcasted_iota(jnp.int32, (x.shape[0], LANES), 1).astype(F32)
    rw = rw_ref[...]
    h_hi = h.astype(BF16)
    h_lo = (h - h_hi.astype(F32)).astype(BF16)
    rw_hi = rw.astype(BF16)
    rw_lo = (rw - rw_hi.astype(F32)).astype(BF16)
    logits = _dot(h_hi, rw_hi) + (_dot(h_hi, rw_lo) + _dot(h_lo, rw_hi))
    logits = jnp.where(lane < N_EXPERTS, logits, -jnp.inf)
    m1 = jnp.max(logits, axis=-1, keepdims=True)
    i1 = jnp.min(jnp.where(logits == m1, lane, float(LANES)), axis=-1, keepdims=True)
    rest = jnp.where(lane == i1, -jnp.inf, logits)
    m2 = jnp.max(rest, axis=-1, keepdims=True)
    i2 = jnp.min(jnp.where(rest == m2, lane, float(LANES)), axis=-1, keepdims=True)
    e2 = jnp.exp(m2 - m1)
    w1 = 1.0 / (1.0 + e2)
    w2 = e2 / (1.0 + e2)
    ind = jnp.where(jnp.logical_or(lane == i1, lane == i2), 1.0, 0.0)
    before = _dot(tri_ref[...], ind.astype(BF16)) + carry_scr[...]
    r1 = jnp.sum(jnp.where(lane == i1, before, 0.0), axis=-1, keepdims=True)
    r2 = jnp.sum(jnp.where(lane == i2, before, 0.0), axis=-1, keepdims=True)
    total = carry_scr[...] + jnp.sum(ind, axis=0, keepdims=True)
    carry_scr[...] = total
    for part in range(1, MOE_SB // MOE_SBG):
        cum_ref[part - 1] = before[part * MOE_SBG:part * MOE_SBG + 1, :]
    cum_ref[MOE_SB // MOE_SBG - 1] = total
    info = jnp.where(lane == 0.0, i1, jnp.where(lane == 1.0, i2, jnp.where(lane == 2.0, w1, jnp.where(
        lane == 3.0, w2, jnp.where(lane == 4.0, r1, jnp.where(lane == 5.0, r2, 0.0))))))
    info_ref[...] = info[:, 0:SUB]
    info_t = jnp.concatenate([info[r:r + LANES, :].T for r in range(0, info.shape[0], LANES)], axis=1)
    infot_ref[...] = info_t[0:SUB, :]


def _moe_route(xp, xs, g, mod, router_w, rows_per_group):
    Np, D = xp.shape
    N = Np + xs.shape[0]
    TM = MOE_SB
    ntp = Np // TM
    nt = N // TM
    rw = jnp.pad(router_w, ((0, 0), (0, LANES - router_w.shape[1])))
    tri = jnp.asarray(np.tril(np.ones((TM, TM), np.float32), -1), dtype=BF16)
    xp_spec, xs_spec = _two_stream_specs((TM, D), ntp)
    return pl.pallas_call(
        functools.partial(_route_kernel, ntp=ntp),
        grid=(nt,),
        in_specs=[xp_spec, xs_spec,
                  pl.BlockSpec((1, D), lambda i: (0, 0)),
                  _pool_mod_spec(3, D, TM, ntp, rows_per_group),
                  _pool_mod_spec(4, D, TM, ntp, rows_per_group),
                  pl.BlockSpec((D, LANES), lambda i: (0, 0)),
                  pl.BlockSpec((TM, TM), lambda i: (0, 0))],
        out_specs=[pl.BlockSpec((TM, D), lambda i: (i, 0)),
                   pl.BlockSpec((TM, SUB), lambda i: (i, 0)),
                   pl.BlockSpec((SUB, TM), lambda i: (0, i)),
                   pl.BlockSpec((MOE_SB // MOE_SBG, 1, LANES), lambda i: (i, 0, 0))],
        out_shape=[jax.ShapeDtypeStruct((N, D), BF16),
                   jax.ShapeDtypeStruct((N, SUB), F32),
                   jax.ShapeDtypeStruct((SUB, N), F32),
                   jax.ShapeDtypeStruct((nt * (MOE_SB // MOE_SBG), 1, LANES), F32)],
        scratch_shapes=[pltpu.VMEM((1, LANES), F32)],
        compiler_params=_params("arbitrary"),
        name="moe_route",
    )(xp, xs, g.reshape(1, D), mod, mod, rw, tri)


def _moe_plan(info, info_t, cum, N):
    E, SB, TRG, TR = N_EXPERTS, MOE_SB, MOE_TRG, MOE_TR
    NB = N // SB
    rmax = 2 * N + E * TR
    RG, RT = rmax // TRG, rmax // TR
    PMAX = RG + E * NB
    i32 = jnp.int32
    parts = SB // MOE_SBG
    cum_g = cum[:, 0, :E].astype(i32).T
    cum_e = cum_g[:, parts - 1::parts]
    cnt = cum_e[:, -1]
    tiles = (cnt + TR - 1) // TR
    start = TR * (jnp.cumsum(tiles) - tiles)

    startf = start.astype(F32)

    def region_start(e):
        out = jnp.zeros_like(e)
        for k in range(E):
            out = jnp.where(e == float(k), startf[k], out)
        return out

    pos_cols = jnp.concatenate([region_start(info[:, 0:2]) + info[:, 4:6], info[:, 2:4],
                                jnp.zeros((N, 4), F32)], axis=1)
    pos_rows = jnp.concatenate([region_start(info_t[0:2]) + info_t[4:6], jnp.zeros((6, N), F32)],
                               axis=0)

    def region(row0):
        e = jnp.clip(jnp.sum(row0[:, None] >= start[None, :], axis=1) - 1, 0, E - 1)
        return e, row0 - start[e]

    eq, lo = region(jnp.arange(RG, dtype=i32) * TRG)
    hi = jnp.minimum(lo + TRG, cnt[eq])
    first = jnp.sum(cum_e[eq] <= lo[:, None], axis=1)
    last = jnp.sum(cum_e[eq] < hi[:, None], axis=1)
    nblk = jnp.where(hi > lo, last - first + 1, 0)
    pend = jnp.cumsum(nblk)
    npairs = pend[-1]
    p = jnp.arange(PMAX, dtype=i32)
    valid = p < npairs
    pc = jnp.minimum(p, npairs - 1)
    q_of = jnp.minimum(jnp.sum(pend[None, :] <= pc[:, None], axis=1), RG - 1).astype(i32)
    pstart = pend - nblk
    s_of = (first[q_of] + pc - pstart[q_of]).astype(i32)

    GG = MOE_GG
    first_g = jnp.sum(cum_g[eq] <= lo[:, None], axis=1)
    last_g = jnp.sum(cum_g[eq] < hi[:, None], axis=1)
    nblk_g = jnp.where(hi > lo, last_g - first_g + 1, 0)
    nst = (nblk_g + GG - 1) // GG
    gst_end = jnp.cumsum(nst)
    gtotal = gst_end[-1]
    smax_g = (RG + E * NB * parts + (GG - 1) * RG) // GG + 1
    jg = jnp.arange(smax_g, dtype=i32)
    g_ok = jg < gtotal
    jgc = jnp.minimum(jg, gtotal - 1)
    tq = jnp.minimum(jnp.sum(gst_end[None, :] <= jgc[:, None], axis=1), RG - 1).astype(i32)
    gg = jgc - (gst_end - nst)[tq]
    last_part = first_g[tq] + nblk_g[tq] - 1
    g_parts = jnp.minimum((first_g[tq] + GG * gg)[:, None] + jnp.arange(GG, dtype=i32)[None, :], last_part[:, None])
    g_slots = jnp.where(g_ok, jnp.clip(nblk_g[tq] - GG * gg, 0, GG), 0)
    g_first = jnp.logical_and(g_ok, gg == 0)
    gather_plan = (tq, g_parts.reshape(-1).astype(i32), g_slots.astype(i32), g_first.astype(i32))

    order = jnp.argsort(jnp.where(valid, s_of * RG + q_of, jnp.iinfo(jnp.int32).max))
    s2, q2 = s_of[order], q_of[order]
    CG = MOE_CG
    blocks = jnp.arange(NB, dtype=i32)
    per_blk = jnp.sum(jnp.logical_and(valid[None, :], s2[None, :] == blocks[:, None]), axis=1)
    pb_end = jnp.cumsum(per_blk)
    pb_start = pb_end - per_blk
    nsteps = (per_blk + CG - 1) // CG
    st_end = jnp.cumsum(nsteps)
    total = st_end[-1]
    SMAX = (PMAX + CG - 1) // CG + NB
    j = jnp.arange(SMAX, dtype=i32)
    step_ok = j < total
    jc = jnp.minimum(j, total - 1)
    blk = jnp.minimum(jnp.sum(st_end[None, :] <= jc[:, None], axis=1), NB - 1).astype(i32)
    grp = jc - (st_end - nsteps)[blk]
    slot_p = pb_start[blk][:, None] + CG * grp[:, None] + jnp.arange(CG, dtype=i32)[None, :]
    slot_ok = jnp.logical_and(slot_p < pb_end[blk][:, None], step_ok[:, None])
    slot_q = jnp.where(slot_ok, q2[jnp.minimum(slot_p, npairs - 1)], q2[pb_start[blk]][:, None])
    c_first = jnp.logical_and(step_ok, grp == 0).astype(i32)
    c_last = jnp.logical_and(step_ok, grp == nsteps[blk] - 1).astype(i32)
    combine_plan = (blk, slot_q.reshape(-1).astype(i32), slot_ok.reshape(-1).astype(i32), step_ok.astype(i32),
                    c_first, c_last)

    te, tlo = region(jnp.arange(RT, dtype=i32) * TR)
    tvalid = jnp.clip(cnt[te] - tlo, 0, TR)
    last_t = jnp.sum(tiles) - 1
    t_idx = jnp.where(tvalid > 0, jnp.arange(RT, dtype=i32), last_t).astype(i32)
    ffn_plan = (t_idx, te[t_idx].astype(i32), tvalid.astype(i32))
    return pos_cols, pos_rows, gather_plan, combine_plan, ffn_plan, rmax


def _moe_gather_kernel(q_ref, s_ref, slots_ref, first_ref, *refs):
    pos_refs, h_refs, out_ref = refs[:MOE_GG], refs[MOE_GG:2 * MOE_GG], refs[2 * MOE_GG]
    p = pl.program_id(0)
    rows = out_ref.shape[0]

    @pl.when(first_ref[p] == 1)
    def _():
        out_ref[...] = jnp.zeros_like(out_ref)

    for ns in range(1, MOE_GG + 1):
        @pl.when(slots_ref[p] == ns)
        def _(ns=ns):
            row = (lax.broadcasted_iota(jnp.int32, (rows, 1), 0) + q_ref[p] * rows).astype(F32)
            sels = []
            for k in range(ns):
                hit = jnp.logical_or(pos_refs[k][0:1, :] == row, pos_refs[k][1:2, :] == row)
                sels.append(jnp.where(hit, 1.0, 0.0).astype(BF16))
            sel = sels[0] if ns == 1 else jnp.concatenate(sels, axis=1)
            hs = h_refs[0][...] if ns == 1 else jnp.concatenate([h_refs[k][...] for k in range(ns)], axis=0)
            out_ref[...] = out_ref[...] + _dot(sel, hs).astype(BF16)


def _moe_gather(h, pos_rows, plan, rmax):
    N, D = h.shape
    nsteps = plan[0].shape[0]

    def pos_spec(k):
        return pl.BlockSpec((SUB, MOE_SBG), lambda p, q, s, *_: (0, s[MOE_GG * p + k]))

    def tok_spec(k):
        return pl.BlockSpec((MOE_SBG, D), lambda p, q, s, *_: (s[MOE_GG * p + k], 0))

    return pl.pallas_call(
        _moe_gather_kernel,
        grid_spec=pltpu.PrefetchScalarGridSpec(
            num_scalar_prefetch=4, grid=(nsteps,),
            in_specs=[pos_spec(k) for k in range(MOE_GG)] + [tok_spec(k) for k in range(MOE_GG)],
            out_specs=pl.BlockSpec((MOE_TRG, D), lambda p, q, *_: (q[p], 0))),
        out_shape=jax.ShapeDtypeStruct((rmax, D), BF16),
        compiler_params=_params("arbitrary"),
        name="moe_gather",
    )(*plan, *([pos_rows] * MOE_GG), *([h] * MOE_GG))


def _moe_ffn_kernel(t_ref, e_ref, nv_ref, x_ref, w1_ref, w3_ref, w2_ref, out_ref, acc_scr, *, nf):
    t = pl.program_id(0)
    f = pl.program_id(1)
    nv = nv_ref[t]

    def block(start, size):
        rows = pl.ds(start, size)

        @pl.when(f == 0)
        def _():
            acc_scr[rows, :] = jnp.zeros((size, acc_scr.shape[1]), F32)

        x = x_ref[rows, :]
        a = _dot(x, w1_ref[...].astype(BF16))
        b = _dot(x, w3_ref[...].astype(BF16))
        acc_scr[rows, :] += _dot((_silu(a) * b).astype(BF16), w2_ref[...].astype(BF16))

        @pl.when(f == nf - 1)
        def _():
            out_ref[rows, :] = acc_scr[rows, :].astype(out_ref.dtype)

    nsub = MOE_TR // MOE_TRG
    used = (nv + MOE_TRG - 1) // MOE_TRG

    @pl.when(used == nsub)
    def _():
        block(0, MOE_TR)

    @pl.when(jnp.logical_and(used > 0, used < nsub))
    def _():
        start = jnp.int32(0)
        size = MOE_TR // 2
        while size >= MOE_TRG:
            has = (used & (size // MOE_TRG)) != 0

            @pl.when(has)
            def _(start=start, size=size):
                block(pl.multiple_of(start, MOE_TRG), size)

            start = start + jnp.where(has, size, 0)
            size //= 2


def _moe_ffn(xs, plan, w1, w3, w2):
    rmax, D = xs.shape
    FF = w1.shape[2]
    TF = 256
    nf = FF // TF
    RT = rmax // MOE_TR

    def fidx(t, f, nv):
        return jnp.where(nv[t] > 0, f, nf - 1)

    return pl.pallas_call(
        functools.partial(_moe_ffn_kernel, nf=nf),
        grid_spec=pltpu.PrefetchScalarGridSpec(
            num_scalar_prefetch=3, grid=(RT, nf),
            in_specs=[pl.BlockSpec((MOE_TR, D), lambda t, f, ti, e, nv: (ti[t], 0)),
                      pl.BlockSpec((None, D, TF), lambda t, f, ti, e, nv: (e[t], 0, fidx(t, f, nv))),
                      pl.BlockSpec((None, D, TF), lambda t, f, ti, e, nv: (e[t], 0, fidx(t, f, nv))),
                      pl.BlockSpec((None, TF, D), lambda t, f, ti, e, nv: (e[t], fidx(t, f, nv), 0))],
            out_specs=pl.BlockSpec((MOE_TR, D), lambda t, f, ti, e, nv: (ti[t], 0)),
            scratch_shapes=[pltpu.VMEM((MOE_TR, D), F32)]),
        out_shape=jax.ShapeDtypeStruct((rmax, D), BF16),
        compiler_params=_params("arbitrary", "arbitrary"),
        name="moe_ffn",
    )(*plan, xs, w1, w3, w2)


def _moe_combine_kernel(s_ref, q_ref, slot_ok_ref, valid_ref, first_ref, last_ref, pos_ref, *refs, ntp):
    ys_refs = refs[:MOE_CG]
    xp_ref, xs_ref, gate_ref, fg_ref, op_ref, os_ref, acc_scr = refs[MOE_CG:]
    p = pl.program_id(0)

    @pl.when(valid_ref[p] == 1)
    def _():
        rows = ys_refs[0].shape[0]
        sels = []
        for k in range(MOE_CG):
            col = (lax.broadcasted_iota(jnp.int32, (1, rows), 1) + q_ref[MOE_CG * p + k] * rows).astype(F32)
            col = jnp.where(slot_ok_ref[MOE_CG * p + k] == 1, col, -1.0)
            sels.append((jnp.where(pos_ref[:, 0:1] == col, pos_ref[:, 2:3], 0.0)
                         + jnp.where(pos_ref[:, 1:2] == col, pos_ref[:, 3:4], 0.0)).astype(BF16))
        sel = jnp.concatenate(sels, axis=1)
        ys = jnp.concatenate([r[...] for r in ys_refs], axis=0)

        @pl.when(first_ref[p] == 1)
        def _():
            acc_scr[...] = jnp.zeros_like(acc_scr)

        acc_scr[...] += _dot(sel, ys)

        @pl.when(last_ref[p] == 1)
        def _():
            s = s_ref[p]
            x = jnp.where(s < ntp, xp_ref[...], xs_ref[...])
            y = x + gate_ref[...] * acc_scr[...]
            out = y * lax.rsqrt(jnp.mean(y * y, axis=-1, keepdims=True) + EPS) * fg_ref[...]

            @pl.when(s < ntp)
            def _():
                op_ref[...] = out

            @pl.when(s >= ntp)
            def _():
                os_ref[...] = out


def _moe_combine(ys, pos_cols, plan, xp, xs, mod, final_g, rows_per_group):
    Np, D = xp.shape
    Ns = xs.shape[0]
    SB = MOE_SB
    ntp = Np // SB
    nsteps = plan[0].shape[0]

    def tile_spec(k):
        return pl.BlockSpec((MOE_TRG, D), lambda p, s, q, *_: (q[MOE_CG * p + k], 0))

    def tok_p(p, s, *_):
        return (jnp.minimum(s[p], ntp - 1), 0)

    def tok_s(p, s, *_):
        return (jnp.maximum(s[p] - ntp, 0), 0)

    def gate_idx(p, s, *_):
        return (jnp.where(s[p] < ntp, 0, 1 + ((s[p] - ntp) * SB) // rows_per_group), 5, 0, 0)

    return pl.pallas_call(
        functools.partial(_moe_combine_kernel, ntp=ntp),
        grid_spec=pltpu.PrefetchScalarGridSpec(
            num_scalar_prefetch=6, grid=(nsteps,),
            in_specs=[pl.BlockSpec((SB, SUB), lambda p, s, q, *_: (s[p], 0))]
            + [tile_spec(k) for k in range(MOE_CG)]
            + [pl.BlockSpec((SB, D), tok_p),
                      pl.BlockSpec((SB, D), tok_s),
                      pl.BlockSpec((None, None, 1, D), gate_idx),
                      pl.BlockSpec((1, D), lambda p, *_: (0, 0))],
            out_specs=[pl.BlockSpec((SB, D), tok_p), pl.BlockSpec((SB, D), tok_s)],
            scratch_shapes=[pltpu.VMEM((SB, D), F32)]),
        out_shape=[jax.ShapeDtypeStruct((Np, D), F32), jax.ShapeDtypeStruct((Ns, D), F32)],
        compiler_params=_params("arbitrary"),
        name="moe_combine",
    )(*plan, pos_cols, *([ys] * MOE_CG), xp, xs, mod, final_g.reshape(1, D))


def _moe(xp, xs, g, mod, router_w, w1, w3, w2, final_g, rows_per_group):
    N = xp.shape[0] + xs.shape[0]
    h, info, info_t, cum = _moe_route(xp, xs, g, mod, router_w, rows_per_group)
    pos_cols, pos_rows, gather_plan, combine_plan, ffn_plan, rmax = _moe_plan(info, info_t, cum, N)
    x_sorted = _moe_gather(h, pos_rows, gather_plan, rmax)
    y_sorted = _moe_ffn(x_sorted, ffn_plan, w1, w3, w2)
    return _moe_combine(y_sorted, pos_cols, combine_plan, xp, xs, mod, final_g, rows_per_group)


def _gla_levels(C):
    lv, c = [], C // 2
    while c >= SUB:
        lv.append(c)
        c //= 2
    return lv


def _gla_tables(C):
    levels = _gla_levels(C)
    nr = 2 + 2 * len(levels)
    mat = np.zeros((2, nr * C, C), np.float32)
    code = np.zeros((2, C, C), np.int32)
    for d in range(2):
        p = np.arange(C) if d == 0 else C - 1 - np.arange(C)
        pi, pj = p[:, None], p[None, :]
        mat[d, 0:C] = pj <= pi
        mat[d, C:2 * C] = pj > pi
        code[d] = np.where((pj <= pi) & (pi // SUB == pj // SUB), 1, 0)
        for lv, c in enumerate(levels):
            blk = pi // c
            later = blk % 2 == 1
            mat[d, (2 + 2 * lv) * C:(3 + 2 * lv) * C] = later & (pj > blk * c - 1) & (pj <= pi)
            mat[d, (3 + 2 * lv) * C:(4 + 2 * lv) * C] = (~later) & (pj > pi) & (pj <= (blk + 1) * c - 1)
            pair = (pi // (2 * c) == pj // (2 * c)) & (pi // c != pj // c) & (pj <= pi)
            code[d] = np.where(pair, 2 + lv, code[d])
    ones = np.zeros((SUB * LANES, C), np.float32)
    for jj in range(SUB):
        ones[jj * LANES:(jj + 1) * LANES, jj::SUB] = 1.0
    return jnp.asarray(mat, dtype=BF16), jnp.asarray(code), jnp.asarray(ones, dtype=BF16)


def _bcast_sublane(x, jj):
    r, w = x.shape
    x3 = x.reshape(r // SUB, SUB, w)
    return jnp.broadcast_to(x3[:, jj:jj + 1, :], x3.shape).reshape(r, w)


def _t128(x):
    r, w = x.shape
    if w > LANES:
        return jnp.concatenate([x[:, i:i + LANES].T for i in range(0, w, LANES)], axis=0)
    return jnp.concatenate([x[i:i + LANES, :].T for i in range(0, r, LANES)], axis=1)


def _gla_kernel(q_ref, k_ref, v_ref, g_ref, lr_ref, wg_ref, ba_ref, mat_ref, code_ref, ones_ref,
                s0f_ref, s0b_ref, ng_ref, o_ref, sf_ref, sb_ref, st_scr, of_scr, *, n, C, G):
    d = pl.program_id(1)
    c = pl.program_id(2)
    levels = _gla_levels(C)

    @pl.when(jnp.logical_and(c == 0, d == 0))
    def _():
        for bb in range(G):
            for h in range(C_H):
                st_scr[bb, h] = _t128(s0f_ref[bb, h])

    @pl.when(jnp.logical_and(c == 0, d == 1))
    def _():
        for bb in range(G):
            for h in range(C_H):
                st_scr[bb, h] = _t128(s0b_ref[bb, h])

    mat = mat_ref[...]
    code = code_ref[...]
    ones = ones_ref[...]
    cums = []
    for bb in range(G):
        xg = jnp.dot(lr_ref[bb], wg_ref[...], precision=HIGHEST, preferred_element_type=F32) + ba_ref[...]
        la = (jnp.minimum(xg, 0.0) - jnp.log1p(jnp.exp(-jnp.abs(xg)))) * (LOG2E / C_TAU)
        hi = la.astype(BF16)
        lo = (la - hi.astype(F32)).astype(BF16)
        cums.append(_dot(mat, hi) + _dot(mat, lo))

    def prepare(bb, h):
        cum = cums[bb]
        ks = slice(h * C_DK, (h + 1) * C_DK)
        qh = q_ref[bb, :, ks].astype(F32) * (C_DK ** -0.5)
        kh = k_ref[bb, :, ks].astype(F32)
        b = cum[0:C, ks]
        b_rest = cum[C:2 * C, ks]
        ps = []
        for jj in range(SUB):
            dec = jnp.exp2(jnp.minimum(b - _bcast_sublane(b, jj), 0.0))
            ps.append((qh * _bcast_sublane(kh, jj) * dec).astype(BF16))
        lv_ops = []
        for lv in range(len(levels)):
            eq = cum[(2 + 2 * lv) * C:(3 + 2 * lv) * C, ks]
            ek = cum[(3 + 2 * lv) * C:(4 + 2 * lv) * C, ks]
            lv_ops.append(((qh * jnp.exp2(eq)).astype(BF16), (kh * jnp.exp2(ek)).astype(BF16)))
        qe = (qh * jnp.exp2(b)).astype(BF16)
        ke = (kh * jnp.exp2(b_rest)).astype(BF16)
        e_end = jnp.exp2(b[0:1, :] + b_rest[0:1, :])
        return jnp.concatenate(ps, axis=1), lv_ops, qe, ke, e_end

    def contract(bb, h, prep):
        pcat, lv_ops, qe, ke, e_end = prep
        vh = v_ref[bb, :, h * C_DV:(h + 1) * C_DV].astype(F32)
        att = jnp.where(code == 1, _dot(pcat, ones), 0.0)
        for lv, (qs, ks_) in enumerate(lv_ops):
            att = jnp.where(code == 2 + lv, _dot_nt(qs, ks_), att)
        st = st_scr[bb, h]
        o = _dot(att.astype(BF16), vh.astype(BF16)) + _dot_nt(qe, st.astype(BF16))
        st_scr[bb, h] = e_end * st + _dot(_t128(vh).astype(BF16), ke)
        return o

    units = [(bb, h) for h in range(C_H) for bb in range(G)]
    outs = {}
    prep = prepare(*units[0])
    for idx, (bb, h) in enumerate(units):
        nxt = prepare(*units[idx + 1]) if idx + 1 < len(units) else None
        outs[(bb, h)] = contract(bb, h, prep)
        prep = nxt
    o_all = [jnp.concatenate([outs[(bb, h)] for h in range(C_H)], axis=-1) for bb in range(G)]

    @pl.when(d == 0)
    def _():
        for bb in range(G):
            of_scr[bb, c] = o_all[bb]

    @pl.when(d == 1)
    def _():
        for bb in range(G):
            tot = o_all[bb] + of_scr[bb, n - 1 - c]
            res = []
            for h in range(C_H):
                sl = slice(h * C_DV, (h + 1) * C_DV)
                t = tot[:, sl]
                y = t * lax.rsqrt(jnp.mean(t * t, axis=-1, keepdims=True) + EPS) * ng_ref[:, sl]
                res.append(y * _silu(g_ref[bb, :, sl].astype(F32)))
            o_ref[bb] = jnp.concatenate(res, axis=-1).astype(o_ref.dtype)

    @pl.when(jnp.logical_and(c == n - 1, d == 0))
    def _():
        for bb in range(G):
            for h in range(C_H):
                sf_ref[bb, h] = _t128(st_scr[bb, h])

    @pl.when(jnp.logical_and(c == n - 1, d == 1))
    def _():
        for bb in range(G):
            for h in range(C_H):
                sb_ref[bb, h] = _t128(st_scr[bb, h])


def _gla(z, zg, B, T, w_a2, b_a, s0f, s0b, norm_g):
    C = GLA_CHUNK
    G = 2
    assert B % G == 0 and T % C == 0
    n = T // C
    HK = C_H * C_DK
    HV = C_H * C_DV
    mat, code, ones = _gla_tables(C)
    nr = mat.shape[1] // C
    wg = jnp.zeros((2, LANES, HK), F32)
    for dr in range(2):
        wg = wg.at[dr, dr * C_RANK:(dr + 1) * C_RANK, :].set(w_a2[dr])
    z3 = z.reshape(B, T, z.shape[1])
    zg3 = zg.reshape(B, T, zg.shape[1])

    def chunk(d, c):
        return c + d * (n - 1 - 2 * c)

    st_spec = pl.BlockSpec((G, C_H, C_DK, C_DV), lambda b, d, c: (b, 0, 0, 0))
    st_shape = jax.ShapeDtypeStruct((B, C_H, C_DK, C_DV), F32)
    o, sf, sb = pl.pallas_call(
        functools.partial(_gla_kernel, n=n, C=C, G=G),
        grid=(B // G, 2, n),
        in_specs=[pl.BlockSpec((G, C, HK), lambda b, d, c: (b, chunk(d, c), 0)),
                  pl.BlockSpec((G, C, HK), lambda b, d, c: (b, chunk(d, c), 1)),
                  pl.BlockSpec((G, C, HV), lambda b, d, c: (b, chunk(d, c), 1)),
                  pl.BlockSpec((G, C, HV), lambda b, d, c: (b, chunk(d, c), 2)),
                  pl.BlockSpec((G, C, LANES), lambda b, d, c: (b, chunk(d, c), 0)),
                  pl.BlockSpec((None, LANES, HK), lambda b, d, c: (d, 0, 0)),
                  pl.BlockSpec((None, 1, HK), lambda b, d, c: (d, 0, 0)),
                  pl.BlockSpec((None, nr * C, C), lambda b, d, c: (d, 0, 0)),
                  pl.BlockSpec((None, C, C), lambda b, d, c: (d, 0, 0)),
                  pl.BlockSpec((SUB * LANES, C), lambda b, d, c: (0, 0)),
                  st_spec, st_spec,
                  pl.BlockSpec((1, HV), lambda b, d, c: (0, 0))],
        out_specs=[pl.BlockSpec((G, C, HV), lambda b, d, c: (b, (n - 1) - d * c, 0)),
                   st_spec, st_spec],
        out_shape=[jax.ShapeDtypeStruct((B, T, HV), BF16), st_shape, st_shape],
        scratch_shapes=[pltpu.VMEM((G, C_H, C_DV, C_DK), F32), pltpu.VMEM((G, n, C, HV), F32)],
        compiler_params=_params("arbitrary", "arbitrary", "arbitrary"),
        name="gla",
    )(z3, z3, z3, z3, zg3, wg, b_a.reshape(2, 1, HK), mat, code, ones, s0f, s0b, norm_g.reshape(1, HV))
    return o.reshape(B * T, HV), sf, sb


def _run_stream(x, B, T, mods, ctx, p):
    N, D = x.shape
    rpg = N // mods[0].shape[0]
    TM = min(2048, rpg)
    nb = (B_H + 2 * B_HKV) * B_HD

    w_in = p['even_w_in'][0]
    z, zb = _norm_mm(x, p['norm1_g'][0], mods[0], (0, 1), w_in, w_in, (nb, MIX_MAIN // nb), TM, rpg)
    if ctx is None:
        s0 = jnp.zeros((B, A_H, A_DK, A_DV), F32)
        a_f0, a_b0, cache_k, cache_v = s0, s0, None, None
    else:
        cache_k, cache_v, a_f0, a_b0 = ctx[0], ctx[1], ctx[2], ctx[3]
    o_a, a_sf, a_sb = _retention(z, B, T, p['a_log_gamma'][0], a_f0, a_b0, p['a_norm_g'][0])
    qpad, k_norm, k_rot, v_bf = _bprep(zb, T, p['b_q_g'][0], p['b_k_g'][0], rope=ctx is not None)
    o_b = _attention(qpad, k_rot, v_bf, B, T, cache_k, cache_v)
    x = _proj_res(x, mods[0], 2, [o_a, o_b], p['even_w_out'][0], rpg)
    x = _ffn(x, p['norm2_g'][0], mods[0], p['ff_w1'][0], p['ff_w3'][0], p['ff_w2'][0], rpg)

    w_in = p['odd_w_in'][0]
    w_gate = jnp.pad(w_in[:, MIX_MAIN:], ((0, 0), (0, LANES - 2 * C_RANK)))
    z1, z1g = _norm_mm(x, p['norm1_g'][1], mods[1], (0, 1), w_in, w_gate, (LANES, 0), TM, rpg)
    if ctx is None:
        s0 = jnp.zeros((B, C_H, C_DK, C_DV), F32)
        c_f0, c_b0 = s0, s0
    else:
        c_f0, c_b0 = ctx[4], ctx[5]
    o_c, c_sf, c_sb = _gla(z1, z1g, B, T, p['c_w_a2'][0], p['c_b_a'][0], c_f0, c_b0, p['c_norm_g'][0])
    x = _proj_res(x, mods[1], 2, [o_c], p['odd_w_out'][0], rpg)
    v_raw = zb[:, (B_H + B_HKV) * B_HD:]
    return x, (k_norm, v_raw, a_sf, a_sb, c_sf, c_sb)


def kernel(x_prompt, x_sample, c, cache_b_k, cache_b_v, state_a_fwd, state_a_bwd, state_c_fwd, state_c_bwd,
           c_ctx, w_mod, b_mod, norm1_g, norm2_g, final_g, even_w_in, even_w_out, a_log_gamma, a_norm_g,
           b_q_g, b_k_g, odd_w_in, c_w_a2, c_b_a, c_norm_g, odd_w_out, ff_w1, ff_w3, ff_w2,
           router_w, moe_w1, moe_w3, moe_w2):
    Bp, Tp, D = x_prompt.shape
    Bs, Ts, _ = x_sample.shape
    L = w_mod.shape[0]
    assert L == 2 and even_w_in.shape[0] == 1 and odd_w_in.shape[0] == 1
    p = dict(norm1_g=norm1_g, norm2_g=norm2_g, final_g=final_g, even_w_in=even_w_in, even_w_out=even_w_out,
             a_log_gamma=a_log_gamma, a_norm_g=a_norm_g, b_q_g=b_q_g, b_k_g=b_k_g, odd_w_in=odd_w_in,
             c_w_a2=c_w_a2, c_b_a=c_b_a, c_norm_g=c_norm_g, odd_w_out=odd_w_out, ff_w1=ff_w1, ff_w3=ff_w3,
             ff_w2=ff_w2, router_w=router_w, moe_w1=moe_w1, moe_w3=moe_w3, moe_w2=moe_w2)

    rows = 8
    conds = jnp.concatenate([c_ctx[None, :], c, jnp.zeros((rows - 1 - Bs, D), F32)], axis=0)
    mod = _modulation(conds, w_mod, b_mod).reshape(L, rows, 6, 1, D)
    mods_p = [mod[l, 0:1] for l in range(L)]
    mods_s = [mod[l, 1:1 + Bs] for l in range(L)]

    x_p, kept = _run_stream(x_prompt.reshape(Bp * Tp, D), Bp, Tp, mods_p, None, p)
    nk = B_HKV * B_HD
    ctx = (cache_b_k[:, 0].reshape(Bs, -1, nk), cache_b_v[:, 0].reshape(Bs, -1, nk),
           state_a_fwd[:, 0], state_a_bwd[:, 0], state_c_fwd[:, 0], state_c_bwd[:, 0])
    x_s, _ = _run_stream(x_sample.reshape(Bs * Ts, D), Bs, Ts, mods_s, ctx, p)
    y_p, y_s = _moe(x_p, x_s, norm2_g[1], mod[1, 0:1 + Bs], router_w[0], moe_w1[0], moe_w3[0], moe_w2[0],
                    final_g, Ts)

    k_norm, v_raw, a_sf, a_sb, c_sf, c_sb = kept
    return (y_p.reshape(Bp, Tp, D), y_s.reshape(Bs, Ts, D),
            k_norm.reshape(Bp, 1, Tp, B_HKV, B_HD), v_raw.reshape(Bp, 1, Tp, B_HKV, B_HD),
            a_sf[:, None], a_sb[:, None], c_sf[:, None], c_sb[:, None])
```

```python
import functools

import numpy as np
import jax
import jax.numpy as jnp
from jax import lax
from jax.experimental import pallas as pl
from jax.experimental.pallas import tpu as pltpu

F32 = jnp.float32
BF16 = jnp.bfloat16
EPS = 1e-6
HIGHEST = lax.Precision.HIGHEST
LOG2E = 1.4426950408889634

VMEM_LIMIT_BYTES = 56 * 1024 * 1024

A_H, A_DK, A_DV = 4, 128, 256
B_H, B_HKV, B_HD = 8, 2, 64
C_H, C_DK, C_DV, C_RANK = 4, 128, 256, 16
C_TAU = 16.0
GRID_W = 64
ROPE_THETA = 10000.0
N_EXPERTS = 8
LANES = 128
SUB = 8
RET_CHUNK = 128
GLA_CHUNK = 128
Q_TILE = 128
SCAN_ROWS_MAX = 4
SCAN_FWD_BYTES = 32 * 1024 * 1024


def _rows_per_step(B, T, width):
    g = SCAN_ROWS_MAX
    while g > 1 and (B % g or g * T * width * 4 > SCAN_FWD_BYTES):
        g //= 2
    return g


def _params(*sem):
    return pltpu.CompilerParams(dimension_semantics=sem, vmem_limit_bytes=VMEM_LIMIT_BYTES)


def _dot(a, b):
    return jnp.dot(a, b, preferred_element_type=F32)


def _dot_nt(a, b):
    return lax.dot_general(a, b, (((1,), (1,)), ((), ())), preferred_element_type=F32)


def _silu(x):
    return x * jax.nn.sigmoid(x)


def _norm_mod(x, g, sh, sc):
    r = lax.rsqrt(jnp.mean(x * x, axis=-1, keepdims=True) + EPS)
    return (x * r * g) * (1.0 + sc) + sh


def _mod_kernel(c_ref, w_ref, b_ref, o_ref):
    c = c_ref[...]
    o_ref[...] = jnp.dot(_silu(c), w_ref[...], precision=HIGHEST, preferred_element_type=F32) + b_ref[...]


def _modulation(conds, w_mod, b_mod):
    L, D, D6 = w_mod.shape
    R = conds.shape[0]
    TN = 1024
    return pl.pallas_call(
        _mod_kernel,
        grid=(L, D6 // TN),
        in_specs=[pl.BlockSpec((R, D), lambda l, j: (0, 0)),
                  pl.BlockSpec((None, D, TN), lambda l, j: (l, 0, j)),
                  pl.BlockSpec((None, 1, TN), lambda l, j: (l, 0, j))],
        out_specs=pl.BlockSpec((None, R, TN), lambda l, j: (l, 0, j)),
        out_shape=jax.ShapeDtypeStruct((L, R, D6), F32),
        compiler_params=_params("arbitrary", "arbitrary"),
        name="modulation",
    )(conds, w_mod, b_mod.reshape(L, 1, D6))


def _mod_spec(part, D, TM, rows_per_group, axis):
    def idx(*g):
        return ((g[axis] * TM) // rows_per_group, part, 0, 0)
    return pl.BlockSpec((None, None, 1, D), idx)


MIX_MAIN = A_H * (2 * A_DK + 2 * A_DV)
MIX_TN = 768


def _norm_mm_kernel(x_ref, g_ref, sh_ref, sc_ref, w_ref, we_ref, o_ref, oe_ref, h_scr, *, nmain):
    j = pl.program_id(1)

    @pl.when(j == 0)
    def _():
        h_scr[...] = _norm_mod(x_ref[...], g_ref[...], sh_ref[...], sc_ref[...]).astype(BF16)

    @pl.when(j < nmain)
    def _():
        o_ref[...] = _dot(h_scr[...], w_ref[...].astype(BF16)).astype(o_ref.dtype)

    @pl.when(j == nmain)
    def _():
        oe_ref[...] = _dot(h_scr[...], we_ref[...].astype(BF16))


def _norm_mm(x, g, mod, parts, w, w_extra, extra_block, TM, rows_per_group):
    N, D = x.shape
    nmain = MIX_MAIN // MIX_TN
    WE = extra_block[0]
    return pl.pallas_call(
        functools.partial(_norm_mm_kernel, nmain=nmain),
        grid=(N // TM, nmain + 1),
        in_specs=[pl.BlockSpec((TM, D), lambda i, j: (i, 0)),
                  pl.BlockSpec((1, D), lambda i, j: (0, 0)),
                  _mod_spec(parts[0], D, TM, rows_per_group, 0),
                  _mod_spec(parts[1], D, TM, rows_per_group, 0),
                  pl.BlockSpec((D, MIX_TN), lambda i, j: (0, jnp.minimum(j, nmain - 1))),
                  pl.BlockSpec((D, WE), lambda i, j: (0, extra_block[1]))],
        out_specs=[pl.BlockSpec((TM, MIX_TN), lambda i, j: (i, jnp.minimum(j, nmain - 1))),
                   pl.BlockSpec((TM, WE), lambda i, j: (i, 0))],
        out_shape=[jax.ShapeDtypeStruct((N, MIX_MAIN), BF16), jax.ShapeDtypeStruct((N, WE), F32)],
        scratch_shapes=[pltpu.VMEM((TM, D), BF16)],
        compiler_params=_params("arbitrary", "arbitrary"),
        name="norm_mm",
    )(x, g.reshape(1, D), mod, mod, w, w_extra)


def _ret_kernel(lg_ref, q_ref, k_ref, v_ref, ag_ref, s0f_ref, s0b_ref, ng_ref,
                o_ref, sf_ref, sb_ref, s_scr, of_scr, *, n, C, G):
    d = pl.program_id(1)
    c = pl.program_id(2)

    @pl.when(jnp.logical_and(c == 0, d == 0))
    def _():
        s_scr[...] = s0f_ref[...]

    @pl.when(jnp.logical_and(c == 0, d == 1))
    def _():
        s_scr[...] = s0b_ref[...]

    df = d.astype(F32)
    sgn = 1.0 - 2.0 * df
    ii = lax.broadcasted_iota(jnp.int32, (C, C), 0).astype(F32)
    jj = lax.broadcasted_iota(jnp.int32, (C, C), 1).astype(F32)
    dd = (ii - jj) * sgn
    feeds = dd >= 0.0
    ddc = jnp.maximum(dd, 0.0)
    ri = lax.broadcasted_iota(jnp.int32, (C, 1), 0).astype(F32)
    pos_q = (ri + 1.0) + df * (C - 2.0 * ri - 1.0)
    pos_k = (C - 1.0 - ri) + df * (2.0 * ri - C + 1.0)
    chunk_len = jnp.full((1, A_DV), float(C), F32)

    outs = [[] for _ in range(G)]
    for h in range(A_H):
        lg = lg_ref[d, h]
        dmask = jnp.where(feeds, jnp.exp2(lg * ddc), 0.0)
        q_dec = jnp.exp2(lg * pos_q)
        k_dec = jnp.exp2(lg * pos_k)
        c_dec = jnp.exp2(lg * chunk_len)
        for bb in range(G):
            qh = q_ref[bb, :, h * A_DK:(h + 1) * A_DK].astype(F32) * (A_DK ** -0.5)
            kh = k_ref[bb, :, h * A_DK:(h + 1) * A_DK].astype(F32)
            vh = v_ref[bb, :, h * A_DV:(h + 1) * A_DV].astype(BF16)
            s = s_scr[bb, h]
            att = _dot_nt(qh.astype(BF16), kh.astype(BF16)) * dmask
            o = _dot(att.astype(BF16), vh) + _dot((qh * q_dec).astype(BF16), s.astype(BF16))
            kd = kh * k_dec
            s_scr[bb, h] = c_dec * s + _dot(kd.T.astype(BF16), vh)
            outs[bb].append(o)
    o_all = [jnp.concatenate(o, axis=-1) for o in outs]

    @pl.when(d == 0)
    def _():
        for bb in range(G):
            of_scr[bb, c] = o_all[bb]

    @pl.when(d == 1)
    def _():
        for bb in range(G):
            tot = o_all[bb] + of_scr[bb, n - 1 - c]
            res = []
            for h in range(A_H):
                sl = slice(h * A_DV, (h + 1) * A_DV)
                t = tot[:, sl]
                dev = t - jnp.mean(t, axis=-1, keepdims=True)
                y = dev * lax.rsqrt(jnp.mean(dev * dev, axis=-1, keepdims=True) + EPS) * ng_ref[:, sl]
                res.append(y * _silu(ag_ref[bb, :, sl].astype(F32)))
            o_ref[bb] = jnp.concatenate(res, axis=-1).astype(o_ref.dtype)

    @pl.when(jnp.logical_and(c == n - 1, d == 0))
    def _():
        sf_ref[...] = s_scr[...]

    @pl.when(jnp.logical_and(c == n - 1, d == 1))
    def _():
        sb_ref[...] = s_scr[...]


def _retention(z, B, T, log_gamma, s0f, s0b, norm_g):
    C = RET_CHUNK
    G = _rows_per_step(B, T, A_H * A_DV)
    assert B % G == 0 and T % C == 0
    n = T // C
    HK = A_H * A_DK
    HV = A_H * A_DV
    z3 = z.reshape(B, T, z.shape[1])

    def chunk(d, c):
        return c + d * (n - 1 - 2 * c)

    st_spec = pl.BlockSpec((G, A_H, A_DK, A_DV), lambda b, d, c: (b, 0, 0, 0))
    st_shape = jax.ShapeDtypeStruct((B, A_H, A_DK, A_DV), F32)
    o, sf, sb = pl.pallas_call(
        functools.partial(_ret_kernel, n=n, C=C, G=G),
        grid=(B // G, 2, n),
        in_specs=[pl.BlockSpec(memory_space=pltpu.SMEM),
                  pl.BlockSpec((G, C, HK), lambda b, d, c: (b, chunk(d, c), 0)),
                  pl.BlockSpec((G, C, HK), lambda b, d, c: (b, chunk(d, c), 1)),
                  pl.BlockSpec((G, C, HV), lambda b, d, c: (b, chunk(d, c), 1)),
                  pl.BlockSpec((G, C, HV), lambda b, d, c: (b, chunk(d, c), 2)),
                  st_spec, st_spec,
                  pl.BlockSpec((1, HV), lambda b, d, c: (0, 0))],
        out_specs=[pl.BlockSpec((G, C, HV), lambda b, d, c: (b, (n - 1) - d * c, 0)),
                   st_spec, st_spec],
        out_shape=[jax.ShapeDtypeStruct((B, T, HV), BF16), st_shape, st_shape],
        scratch_shapes=[pltpu.VMEM((G, A_H, A_DK, A_DV), F32), pltpu.VMEM((G, n, C, HV), F32)],
        compiler_params=_params("arbitrary", "arbitrary", "arbitrary"),
        name="retention",
    )(log_gamma * LOG2E, z3, z3, z3, z3, s0f, s0b, norm_g.reshape(1, HV))
    return o.reshape(B * T, HV), sf, sb


def _group_sum_matrix(width, group):
    i = np.arange(width)
    return jnp.asarray((i[:, None] // group == i[None, :] // group).astype(np.float32), dtype=BF16)


def _q_pad_matrix():
    m = np.zeros((B_H * B_HD, B_H * LANES), np.float32)
    g = B_H // B_HKV
    for h in range(B_H):
        for t in range(B_HD):
            m[h * B_HD + t, h * LANES + (h // g) * B_HD + t] = 1.0
    return jnp.asarray(m, dtype=BF16)


def _rope_tables(T):
    rows = T // GRID_W
    row = np.repeat(np.arange(rows, dtype=np.float64), GRID_W)
    col = np.tile(np.arange(GRID_W, dtype=np.float64), rows)
    nq = B_HD // 4
    inv = ROPE_THETA ** (-np.arange(nq, dtype=np.float64) / nq)
    ang = np.concatenate([row[:, None] * inv, col[:, None] * inv], axis=-1)
    cos = np.repeat(np.cos(ang), 2, axis=-1)
    sin = np.repeat(np.sin(ang), 2, axis=-1)
    sign = np.tile(np.array([-1.0, 1.0]), B_HD // 2)
    reps = LANES // B_HD
    return (jnp.asarray(np.tile(cos, (1, reps)), dtype=F32),
            jnp.asarray(np.tile(sin * sign, (1, reps)), dtype=F32))


def _group_rmsnorm(x, gsum, g):
    x2 = x * x
    hi = x2.astype(BF16)
    lo = (x2 - hi.astype(F32)).astype(BF16)
    ss = _dot(hi, gsum) + _dot(lo, gsum)
    return x * lax.rsqrt(ss * (1.0 / B_HD) + EPS) * g


def _rotate_pairs(x, cos, sin_signed):
    n = x.shape[1]
    lane = lax.broadcasted_iota(jnp.int32, x.shape, 1)
    partner = jnp.where(lane % 2 == 0, pltpu.roll(x, n - 1, 1), pltpu.roll(x, 1, 1))
    reps = n // LANES
    if reps > 1:
        cos = jnp.concatenate([cos] * reps, axis=1)
        sin_signed = jnp.concatenate([sin_signed] * reps, axis=1)
    return x * cos + partner * sin_signed


def _bprep_kernel(z_ref, qg_ref, kg_ref, cos_ref, sin_ref, gq_ref, gk_ref, pad_ref,
                  qpad_ref, kn_ref, kr_ref, vb_ref, *, rope):
    nq = B_H * B_HD
    nk = B_HKV * B_HD
    qn = _group_rmsnorm(z_ref[:, 0:nq], gq_ref[...], qg_ref[...])
    kn = _group_rmsnorm(z_ref[:, nq:nq + nk], gk_ref[...], kg_ref[...])
    kn_ref[...] = kn
    if rope:
        qn = _rotate_pairs(qn, cos_ref[...], sin_ref[...])
        kn = _rotate_pairs(kn, cos_ref[...], sin_ref[...])
    kr_ref[...] = kn.astype(BF16)
    vb_ref[...] = z_ref[:, nq + nk:nq + 2 * nk].astype(BF16)
    qs = (qn * (B_HD ** -0.5 * LOG2E)).astype(BF16)
    qpad_ref[...] = _dot(qs, pad_ref[...]).astype(BF16)


def _bprep(z, T, q_g, k_g, rope):
    N = z.shape[0]
    TM = min(512, T)
    nq = B_H * B_HD
    nk = B_HKV * B_HD
    width = nq + 2 * nk
    assert z.shape[1] == width
    cos, sin = _rope_tables(T if rope else TM)
    nt = T // TM if rope else 1
    const = lambda i: (0, 0)
    return pl.pallas_call(
        functools.partial(_bprep_kernel, rope=rope),
        grid=(N // TM,),
        in_specs=[pl.BlockSpec((TM, width), lambda i: (i, 0)),
                  pl.BlockSpec((1, nq), const),
                  pl.BlockSpec((1, nk), const),
                  pl.BlockSpec((TM, LANES), lambda i: (i % nt, 0)),
                  pl.BlockSpec((TM, LANES), lambda i: (i % nt, 0)),
                  pl.BlockSpec((nq, nq), const),
                  pl.BlockSpec((nk, nk), const),
                  pl.BlockSpec((nq, B_H * LANES), const)],
        out_specs=[pl.BlockSpec((TM, B_H * LANES), lambda i: (i, 0)),
                   pl.BlockSpec((TM, nk), lambda i: (i, 0)),
                   pl.BlockSpec((TM, nk), lambda i: (i, 0)),
                   pl.BlockSpec((TM, nk), lambda i: (i, 0))],
        out_shape=[jax.ShapeDtypeStruct((N, B_H * LANES), BF16),
                   jax.ShapeDtypeStruct((N, nk), F32),
                   jax.ShapeDtypeStruct((N, nk), BF16),
                   jax.ShapeDtypeStruct((N, nk), BF16)],
        compiler_params=_params("arbitrary"),
        name="attn_prep",
    )(z, jnp.tile(q_g, B_H).reshape(1, nq), jnp.tile(k_g, B_HKV).reshape(1, nk), cos, sin,
      _group_sum_matrix(nq, B_HD), _group_sum_matrix(nk, B_HD), _q_pad_matrix())


def _lane_fold(x, op):
    acc = x[:, 0:LANES]
    for j in range(1, x.shape[1] // LANES):
        acc = op(acc, x[:, j * LANES:(j + 1) * LANES])
    return acc


def _attn_kernel(*refs, has_cache, kc, nq):
    if has_cache:
        q_ref, k_ref, v_ref, ck_ref, cv_ref, o_ref, s_scr, m_scr, mprev_scr, l_scr, acc_scr = refs
        kcc = min(kc, ck_ref.shape[0])
        ncache = ck_ref.shape[0] // kcc
    else:
        q_ref, k_ref, v_ref, o_ref, s_scr, m_scr, mprev_scr, l_scr, acc_scr = refs
        kcc, ncache = kc, 0
    i = pl.program_id(1)
    tq = q_ref.shape[0]
    nlat = k_ref.shape[0] // kc

    def score(c, kblk):
        q = jnp.concatenate([q_ref[:, h * LANES:(h + 1) * LANES] for h in range(B_H)], axis=0)
        s = _dot_nt(q, kblk)
        s_scr[c, :, 0:kblk.shape[0]] = s
        m_scr[...] = jnp.maximum(m_scr[...], _lane_fold(s, jnp.maximum))

    def weight(c, vblk):
        s = s_scr[c, :, 0:vblk.shape[0]]
        mp = mprev_scr[...]
        ps = [jnp.exp2(s[:, j * LANES:(j + 1) * LANES] - mp) for j in range(vblk.shape[0] // LANES)]
        tot = ps[0]
        for pj in ps[1:]:
            tot = tot + pj
        l_scr[...] += tot
        acc_scr[...] += _dot(jnp.concatenate(ps, axis=1).astype(BF16), vblk)

    def run(do_weight, do_score):
        def unit(c, kblk, vblk):
            if do_weight:
                weight(c, vblk())
            if do_score:
                score(c, kblk())

        for c in range(ncache):
            unit(c, lambda: ck_ref[c * kcc:(c + 1) * kcc, :].astype(BF16),
                 lambda: cv_ref[c * kcc:(c + 1) * kcc, :].astype(BF16))

        def body(c, carry):
            rows = pl.ds(pl.multiple_of(c * kc, kc), kc)
            unit(ncache + c, lambda: k_ref[rows, :], lambda: v_ref[rows, :])
            return carry
        lax.fori_loop(0, nlat, body, 0)

    @pl.when(i < nq)
    def _():
        m_scr[...] = jnp.full(m_scr.shape, -jnp.inf, F32)

    @pl.when(i > 0)
    def _():
        l_scr[...] = jnp.zeros_like(l_scr)
        acc_scr[...] = jnp.zeros_like(acc_scr)

    @pl.when(i == 0)
    def _():
        run(False, True)

    @pl.when(jnp.logical_and(i > 0, i < nq))
    def _():
        run(True, True)

    @pl.when(i == nq)
    def _():
        run(True, False)

    @pl.when(i > 0)
    def _():
        r_all = acc_scr[...] / jnp.sum(l_scr[...], axis=-1, keepdims=True)
        g = B_H // B_HKV
        lane = lax.broadcasted_iota(jnp.int32, (tq, LANES), 1)
        outs = []
        for j in range(B_H // 2):
            pair = []
            for half in range(2):
                h = 2 * j + half
                r = r_all[h * tq:(h + 1) * tq, :]
                if h // g != half:
                    r = pltpu.roll(r, B_HD, 1)
                pair.append(r)
            outs.append(jnp.where(lane < B_HD, pair[0], pair[1]))
        o_ref[...] = jnp.concatenate(outs, axis=-1).astype(o_ref.dtype)

    @pl.when(i < nq)
    def _():
        mprev_scr[...] = jnp.broadcast_to(jnp.max(m_scr[...], axis=-1, keepdims=True), mprev_scr.shape)


def _attention(qpad, kr, vb, B, T, cache_k, cache_v):
    has_cache = cache_k is not None
    TQ = Q_TILE
    nq = T // TQ
    nk = B_HKV * B_HD
    in_specs = [pl.BlockSpec((TQ, B_H * LANES), lambda b, i: (b * nq + jnp.minimum(i, nq - 1), 0)),
                pl.BlockSpec((T, nk), lambda b, i: (b, 0)),
                pl.BlockSpec((T, nk), lambda b, i: (b, 0))]
    args = [qpad, kr, vb]
    kc = min(1024, T)
    nchunks = T // kc
    if has_cache:
        P = cache_k.shape[1]
        assert P % min(kc, P) == 0
        nchunks += P // min(kc, P)
        in_specs += [pl.BlockSpec((None, P, nk), lambda b, i: (b, 0, 0))] * 2
        args += [cache_k, cache_v]
    R = B_H * TQ
    return pl.pallas_call(
        functools.partial(_attn_kernel, has_cache=has_cache, kc=kc, nq=nq),
        grid=(B, nq + 1),
        in_specs=in_specs,
        out_specs=pl.BlockSpec((TQ, B_H * B_HD), lambda b, i: (b * nq + jnp.maximum(i - 1, 0), 0)),
        out_shape=jax.ShapeDtypeStruct((B * T, B_H * B_HD), BF16),
        scratch_shapes=[pltpu.VMEM((nchunks, R, kc), F32)] + [pltpu.VMEM((R, LANES), F32)] * 4,
        compiler_params=_params("arbitrary", "arbitrary"),
        name="attention",
    )(*args)


def _proj_res_kernel(*refs, n_in):
    x_ref, gate_ref = refs[0], refs[1]
    o_refs = refs[2:2 + n_in]
    w_refs = refs[2 + n_in:2 + 2 * n_in]
    out_ref = refs[2 + 2 * n_in]
    wbf_refs = refs[3 + 2 * n_in:]

    @pl.when(pl.program_id(0) == 0)
    def _():
        for w_ref, wbf_ref in zip(w_refs, wbf_refs):
            wbf_ref[...] = w_ref[...].astype(BF16)

    acc = _dot(o_refs[0][...], wbf_refs[0][...])
    for o_ref, wbf_ref in zip(o_refs[1:], wbf_refs[1:]):
        acc = acc + _dot(o_ref[...], wbf_ref[...])
    out_ref[...] = x_ref[...] + gate_ref[...] * acc


def _proj_res(x, mod, part, acts, w, rows_per_group):
    N, D = x.shape
    TM = min(1024, rows_per_group)
    n_in = len(acts)
    widths = [a.shape[1] for a in acts]
    offs = np.cumsum([0] + widths[:-1]).tolist()
    in_specs = [pl.BlockSpec((TM, D), lambda i: (i, 0)),
                _mod_spec(part, D, TM, rows_per_group, 0)]
    in_specs += [pl.BlockSpec((TM, wd), lambda i: (i, 0)) for wd in widths]
    in_specs += [pl.BlockSpec((wd, D), functools.partial(lambda i, blk: (blk, 0), blk=off // wd))
                 for wd, off in zip(widths, offs)]
    return pl.pallas_call(
        functools.partial(_proj_res_kernel, n_in=n_in),
        grid=(N // TM,),
        in_specs=in_specs,
        out_specs=pl.BlockSpec((TM, D), lambda i: (i, 0)),
        out_shape=jax.ShapeDtypeStruct((N, D), F32),
        scratch_shapes=[pltpu.VMEM((wd, D), BF16) for wd in widths],
        compiler_params=_params("arbitrary"),
        name="proj_residual",
    )(x, mod, *acts, *([w] * n_in))


def _ffn_kernel(x_ref, g_ref, sh_ref, sc_ref, gate_ref, w1_ref, w3_ref, w2_ref, out_ref, h_scr, acc_scr, *, nf):
    f = pl.program_id(1)

    @pl.when(f == 0)
    def _():
        h_scr[...] = _norm_mod(x_ref[...], g_ref[...], sh_ref[...], sc_ref[...]).astype(BF16)
        acc_scr[...] = jnp.zeros_like(acc_scr)

    h = h_scr[...]
    a = _dot(h, w1_ref[...])
    b = _dot(h, w3_ref[...])
    acc_scr[...] += _dot((_silu(a) * b).astype(BF16), w2_ref[...])

    @pl.when(f == nf - 1)
    def _():
        out_ref[...] = x_ref[...] + gate_ref[...] * acc_scr[...]


FFN_TF = 256


def _repack_kernel(w_ref, o_ref):
    o_ref[...] = w_ref[...].astype(o_ref.dtype)


def _repack_cols(w, tf):
    K, F = w.shape
    return pl.pallas_call(
        _repack_kernel,
        grid=(F // tf,),
        in_specs=[pl.BlockSpec((K, tf), lambda f: (0, f))],
        out_specs=pl.BlockSpec((None, K, tf), lambda f: (f, 0, 0)),
        out_shape=jax.ShapeDtypeStruct((F // tf, K, tf), BF16),
        compiler_params=_params("arbitrary"),
        name="repack_cols",
    )(w)


def _repack_rows(w, tf):
    F, K = w.shape
    return pl.pallas_call(
        _repack_kernel,
        grid=(F // tf,),
        in_specs=[pl.BlockSpec((tf, K), lambda f: (f, 0))],
        out_specs=pl.BlockSpec((None, tf, K), lambda f: (f, 0, 0)),
        out_shape=jax.ShapeDtypeStruct((F // tf, tf, K), BF16),
        compiler_params=_params("arbitrary"),
        name="repack_rows",
    )(w)


def _ffn(x, g, mod, w1, w3, w2, rows_per_group):
    N, D = x.shape
    nf, _, TF = w1.shape
    TM = min(1024, rows_per_group)
    return pl.pallas_call(
        functools.partial(_ffn_kernel, nf=nf),
        grid=(N // TM, nf),
        in_specs=[pl.BlockSpec((TM, D), lambda i, f: (i, 0)),
                  pl.BlockSpec((1, D), lambda i, f: (0, 0)),
                  _mod_spec(3, D, TM, rows_per_group, 0),
                  _mod_spec(4, D, TM, rows_per_group, 0),
                  _mod_spec(5, D, TM, rows_per_group, 0),
                  pl.BlockSpec((None, D, TF), lambda i, f: (f, 0, 0)),
                  pl.BlockSpec((None, D, TF), lambda i, f: (f, 0, 0)),
                  pl.BlockSpec((None, TF, D), lambda i, f: (f, 0, 0))],
        out_specs=pl.BlockSpec((TM, D), lambda i, f: (i, 0)),
        out_shape=jax.ShapeDtypeStruct((N, D), F32),
        scratch_shapes=[pltpu.VMEM((TM, D), BF16), pltpu.VMEM((TM, D), F32)],
        compiler_params=_params("arbitrary", "arbitrary"),
        name="ffn",
    )(x, g.reshape(1, D), mod, mod, mod, w1, w3, w2)


MOE_SB = 1024
MOE_SBG = 512
MOE_GG = 4
MOE_TRG = 256
MOE_TR = 2048
MOE_CG = 4


def _two_stream_specs(shape, ntp, ax=0):
    def idx_p(*g):
        return (jnp.minimum(g[ax], ntp - 1), 0)

    def idx_s(*g):
        return (jnp.maximum(g[ax] - ntp, 0), 0)
    return pl.BlockSpec(shape, idx_p), pl.BlockSpec(shape, idx_s)


def _pool_mod_spec(part, D, TM, ntp, rows_per_group):
    def idx(i, *_):
        return (jnp.where(i < ntp, 0, 1 + ((i - ntp) * TM) // rows_per_group), part, 0, 0)
    return pl.BlockSpec((None, None, 1, D), idx)


def _route_kernel(xp_ref, xs_ref, g_ref, sh_ref, sc_ref, rw_ref, tri_ref, h_ref, info_ref, infot_ref, cum_ref,
                  carry_scr, *, ntp):
    i = pl.program_id(0)

    @pl.when(i == 0)
    def _():
        carry_scr[...] = jnp.zeros_like(carry_scr)

    x = jnp.where(i < ntp, xp_ref[...], xs_ref[...])
    h = _norm_mod(x, g_ref[...], sh_ref[...], sc_ref[...])
    h_ref[...] = h.astype(BF16)
    lane = lax.broadcasted_iota(jnp.int32, (x.shape[0], LANES), 1).astype(F32)
    rw = rw_ref[...]
    h_hi = h.astype(BF16)
    h_lo = (h - h_hi.astype(F32)).astype(BF16)
    rw_hi = rw.astype(BF16)
    rw_lo = (rw - rw_hi.astype(F32)).astype(BF16)
    logits = _dot(h_hi, rw_hi) + (_dot(h_hi, rw_lo) + _dot(h_lo, rw_hi))
    logits = jnp.where(lane < N_EXPERTS, logits, -jnp.inf)
    m1 = jnp.max(logits, axis=-1, keepdims=True)
    i1 = jnp.min(jnp.where(logits == m1, lane, float(LANES)), axis=-1, keepdims=True)
    rest = jnp.where(lane == i1, -jnp.inf, logits)
    m2 = jnp.max(rest, axis=-1, keepdims=True)
    i2 = jnp.min(jnp.where(rest == m2, lane, float(LANES)), axis=-1, keepdims=True)
    e2 = jnp.exp(m2 - m1)
    w1 = 1.0 / (1.0 + e2)
    w2 = e2 / (1.0 + e2)
    ind = jnp.where(jnp.logical_or(lane == i1, lane == i2), 1.0, 0.0)
    before = _dot(tri_ref[...], ind.astype(BF16)) + carry_scr[...]
    r1 = jnp.sum(jnp.where(lane == i1, before, 0.0), axis=-1, keepdims=True)
    r2 = jnp.sum(jnp.where(lane == i2, before, 0.0), axis=-1, keepdims=True)
    total = carry_scr[...] + jnp.sum(ind, axis=0, keepdims=True)
    carry_scr[...] = total
    for part in range(1, MOE_SB // MOE_SBG):
        cum_ref[part - 1] = before[part * MOE_SBG:part * MOE_SBG + 1, :]
    cum_ref[MOE_SB // MOE_SBG - 1] = total
    info = jnp.where(lane == 0.0, i1, jnp.where(lane == 1.0, i2, jnp.where(lane == 2.0, w1, jnp.where(
        lane == 3.0, w2, jnp.where(lane == 4.0, r1, jnp.where(lane == 5.0, r2, 0.0))))))
    info_ref[...] = info[:, 0:SUB]
    info_t = jnp.concatenate([info[r:r + LANES, :].T for r in range(0, info.shape[0], LANES)], axis=1)
    infot_ref[...] = info_t[0:SUB, :]


def _moe_route(xp, xs, g, mod, router_w, rows_per_group):
    Np, D = xp.shape
    N = Np + xs.shape[0]
    TM = MOE_SB
    ntp = Np // TM
    nt = N // TM
    rw = jnp.pad(router_w, ((0, 0), (0, LANES - router_w.shape[1])))
    tri = jnp.asarray(np.tril(np.ones((TM, TM), np.float32), -1), dtype=BF16)
    xp_spec, xs_spec = _two_stream_specs((TM, D), ntp)
    return pl.pallas_call(
        functools.partial(_route_kernel, ntp=ntp),
        grid=(nt,),
        in_specs=[xp_spec, xs_spec,
                  pl.BlockSpec((1, D), lambda i: (0, 0)),
                  _pool_mod_spec(3, D, TM, ntp, rows_per_group),
                  _pool_mod_spec(4, D, TM, ntp, rows_per_group),
                  pl.BlockSpec((D, LANES), lambda i: (0, 0)),
                  pl.BlockSpec((TM, TM), lambda i: (0, 0))],
        out_specs=[pl.BlockSpec((TM, D), lambda i: (i, 0)),
                   pl.BlockSpec((TM, SUB), lambda i: (i, 0)),
                   pl.BlockSpec((SUB, TM), lambda i: (0, i)),
                   pl.BlockSpec((MOE_SB // MOE_SBG, 1, LANES), lambda i: (i, 0, 0))],
        out_shape=[jax.ShapeDtypeStruct((N, D), BF16),
                   jax.ShapeDtypeStruct((N, SUB), F32),
                   jax.ShapeDtypeStruct((SUB, N), F32),
                   jax.ShapeDtypeStruct((nt * (MOE_SB // MOE_SBG), 1, LANES), F32)],
        scratch_shapes=[pltpu.VMEM((1, LANES), F32)],
        compiler_params=_params("arbitrary"),
        name="moe_route",
    )(xp, xs, g.reshape(1, D), mod, mod, rw, tri)


def _moe_plan(info, info_t, cum, N):
    E, SB, TRG, TR = N_EXPERTS, MOE_SB, MOE_TRG, MOE_TR
    NB = N // SB
    rmax = 2 * N + E * TR
    RG, RT = rmax // TRG, rmax // TR
    PMAX = RG + E * NB
    i32 = jnp.int32
    parts = SB // MOE_SBG
    cum_g = cum[:, 0, :E].astype(i32).T
    cum_e = cum_g[:, parts - 1::parts]
    cnt = cum_e[:, -1]
    tiles = (cnt + TR - 1) // TR
    start = TR * (jnp.cumsum(tiles) - tiles)

    startf = start.astype(F32)

    def region_start(e):
        out = jnp.zeros_like(e)
        for k in range(E):
            out = jnp.where(e == float(k), startf[k], out)
        return out

    pos_cols = jnp.concatenate([region_start(info[:, 0:2]) + info[:, 4:6], info[:, 2:4],
                                jnp.zeros((N, 4), F32)], axis=1)
    pos_rows = jnp.concatenate([region_start(info_t[0:2]) + info_t[4:6], jnp.zeros((6, N), F32)],
                               axis=0)

    def region(row0):
        e = jnp.clip(jnp.sum(row0[:, None] >= start[None, :], axis=1) - 1, 0, E - 1)
        return e, row0 - start[e]

    eq, lo = region(jnp.arange(RG, dtype=i32) * TRG)
    hi = jnp.minimum(lo + TRG, cnt[eq])
    first = jnp.sum(cum_e[eq] <= lo[:, None], axis=1)
    last = jnp.sum(cum_e[eq] < hi[:, None], axis=1)
    nblk = jnp.where(hi > lo, last - first + 1, 0)
    pend = jnp.cumsum(nblk)
    npairs = pend[-1]
    p = jnp.arange(PMAX, dtype=i32)
    valid = p < npairs
    pc = jnp.minimum(p, npairs - 1)
    q_of = jnp.minimum(jnp.sum(pend[None, :] <= pc[:, None], axis=1), RG - 1).astype(i32)
    pstart = pend - nblk
    s_of = (first[q_of] + pc - pstart[q_of]).astype(i32)

    GG = MOE_GG
    first_g = jnp.sum(cum_g[eq] <= lo[:, None], axis=1)
    last_g = jnp.sum(cum_g[eq] < hi[:, None], axis=1)
    nblk_g = jnp.where(hi > lo, last_g - first_g + 1, 0)
    nst = (nblk_g + GG - 1) // GG
    gst_end = jnp.cumsum(nst)
    gtotal = gst_end[-1]
    smax_g = (RG + E * NB * parts + (GG - 1) * RG) // GG + 1
    jg = jnp.arange(smax_g, dtype=i32)
    g_ok = jg < gtotal
    jgc = jnp.minimum(jg, gtotal - 1)
    tq = jnp.minimum(jnp.sum(gst_end[None, :] <= jgc[:, None], axis=1), RG - 1).astype(i32)
    gg = jgc - (gst_end - nst)[tq]
    last_part = first_g[tq] + nblk_g[tq] - 1
    g_parts = jnp.minimum((first_g[tq] + GG * gg)[:, None] + jnp.arange(GG, dtype=i32)[None, :], last_part[:, None])
    g_slots = jnp.where(g_ok, jnp.clip(nblk_g[tq] - GG * gg, 0, GG), 0)
    g_first = jnp.logical_and(g_ok, gg == 0)
    gather_plan = (tq, g_parts.reshape(-1).astype(i32), g_slots.astype(i32), g_first.astype(i32))

    order = jnp.argsort(jnp.where(valid, s_of * RG + q_of, jnp.iinfo(jnp.int32).max))
    s2, q2 = s_of[order], q_of[order]
    CG = MOE_CG
    blocks = jnp.arange(NB, dtype=i32)
    per_blk = jnp.sum(jnp.logical_and(valid[None, :], s2[None, :] == blocks[:, None]), axis=1)
    pb_end = jnp.cumsum(per_blk)
    pb_start = pb_end - per_blk
    nsteps = (per_blk + CG - 1) // CG
    st_end = jnp.cumsum(nsteps)
    total = st_end[-1]
    SMAX = (PMAX + CG - 1) // CG + NB
    j = jnp.arange(SMAX, dtype=i32)
    step_ok = j < total
    jc = jnp.minimum(j, total - 1)
    blk = jnp.minimum(jnp.sum(st_end[None, :] <= jc[:, None], axis=1), NB - 1).astype(i32)
    grp = jc - (st_end - nsteps)[blk]
    slot_p = pb_start[blk][:, None] + CG * grp[:, None] + jnp.arange(CG, dtype=i32)[None, :]
    slot_ok = jnp.logical_and(slot_p < pb_end[blk][:, None], step_ok[:, None])
    slot_q = jnp.where(slot_ok, q2[jnp.minimum(slot_p, npairs - 1)], q2[pb_start[blk]][:, None])
    c_first = jnp.logical_and(step_ok, grp == 0).astype(i32)
    c_last = jnp.logical_and(step_ok, grp == nsteps[blk] - 1).astype(i32)
    combine_plan = (blk, slot_q.reshape(-1).astype(i32), slot_ok.reshape(-1).astype(i32), step_ok.astype(i32),
                    c_first, c_last)

    te, tlo = region(jnp.arange(RT, dtype=i32) * TR)
    tvalid = jnp.clip(cnt[te] - tlo, 0, TR)
    last_t = jnp.sum(tiles) - 1
    t_idx = jnp.where(tvalid > 0, jnp.arange(RT, dtype=i32), last_t).astype(i32)
    ffn_plan = (t_idx, te[t_idx].astype(i32), tvalid.astype(i32))
    return pos_cols, pos_rows, gather_plan, combine_plan, ffn_plan, rmax


def _moe_gather_kernel(q_ref, s_ref, slots_ref, first_ref, *refs):
    pos_refs, h_refs, out_ref = refs[:MOE_GG], refs[MOE_GG:2 * MOE_GG], refs[2 * MOE_GG]
    p = pl.program_id(0)
    rows = out_ref.shape[0]

    @pl.when(first_ref[p] == 1)
    def _():
        out_ref[...] = jnp.zeros_like(out_ref)

    for ns in range(1, MOE_GG + 1):
        @pl.when(slots_ref[p] == ns)
        def _(ns=ns):
            row = (lax.broadcasted_iota(jnp.int32, (rows, 1), 0) + q_ref[p] * rows).astype(F32)
            sels = []
            for k in range(ns):
                hit = jnp.logical_or(pos_refs[k][0:1, :] == row, pos_refs[k][1:2, :] == row)
                sels.append(jnp.where(hit, 1.0, 0.0).astype(BF16))
            sel = sels[0] if ns == 1 else jnp.concatenate(sels, axis=1)
            hs = h_refs[0][...] if ns == 1 else jnp.concatenate([h_refs[k][...] for k in range(ns)], axis=0)
            out_ref[...] = out_ref[...] + _dot(sel, hs).astype(BF16)


def _moe_gather(h, pos_rows, plan, rmax):
    N, D = h.shape
    nsteps = plan[0].shape[0]

    def pos_spec(k):
        return pl.BlockSpec((SUB, MOE_SBG), lambda p, q, s, *_: (0, s[MOE_GG * p + k]))

    def tok_spec(k):
        return pl.BlockSpec((MOE_SBG, D), lambda p, q, s, *_: (s[MOE_GG * p + k], 0))

    return pl.pallas_call(
        _moe_gather_kernel,
        grid_spec=pltpu.PrefetchScalarGridSpec(
            num_scalar_prefetch=4, grid=(nsteps,),
            in_specs=[pos_spec(k) for k in range(MOE_GG)] + [tok_spec(k) for k in range(MOE_GG)],
            out_specs=pl.BlockSpec((MOE_TRG, D), lambda p, q, *_: (q[p], 0))),
        out_shape=jax.ShapeDtypeStruct((rmax, D), BF16),
        compiler_params=_params("arbitrary"),
        name="moe_gather",
    )(*plan, *([pos_rows] * MOE_GG), *([h] * MOE_GG))


def _moe_ffn_kernel(t_ref, e_ref, nv_ref, x_ref, w1_ref, w3_ref, w2_ref, out_ref, acc_scr, *, nf):
    t = pl.program_id(0)
    f = pl.program_id(1)
    nv = nv_ref[t]

    def block(start, size):
        rows = pl.ds(start, size)

        @pl.when(f == 0)
        def _():
            acc_scr[rows, :] = jnp.zeros((size, acc_scr.shape[1]), F32)

        x = x_ref[rows, :]
        a = _dot(x, w1_ref[...].astype(BF16))
        b = _dot(x, w3_ref[...].astype(BF16))
        acc_scr[rows, :] += _dot((_silu(a) * b).astype(BF16), w2_ref[...].astype(BF16))

        @pl.when(f == nf - 1)
        def _():
            out_ref[rows, :] = acc_scr[rows, :].astype(out_ref.dtype)

    nsub = MOE_TR // MOE_TRG
    used = (nv + MOE_TRG - 1) // MOE_TRG

    @pl.when(used == nsub)
    def _():
        block(0, MOE_TR)

    @pl.when(jnp.logical_and(used > 0, used < nsub))
    def _():
        start = jnp.int32(0)
        size = MOE_TR // 2
        while size >= MOE_TRG:
            has = (used & (size // MOE_TRG)) != 0

            @pl.when(has)
            def _(start=start, size=size):
                block(pl.multiple_of(start, MOE_TRG), size)

            start = start + jnp.where(has, size, 0)
            size //= 2


def _moe_ffn(xs, plan, w1, w3, w2):
    rmax, D = xs.shape
    FF = w1.shape[2]
    TF = 256
    nf = FF // TF
    RT = rmax // MOE_TR

    def fidx(t, f, nv):
        return jnp.where(nv[t] > 0, f, nf - 1)

    return pl.pallas_call(
        functools.partial(_moe_ffn_kernel, nf=nf),
        grid_spec=pltpu.PrefetchScalarGridSpec(
            num_scalar_prefetch=3, grid=(RT, nf),
            in_specs=[pl.BlockSpec((MOE_TR, D), lambda t, f, ti, e, nv: (ti[t], 0)),
                      pl.BlockSpec((None, D, TF), lambda t, f, ti, e, nv: (e[t], 0, fidx(t, f, nv))),
                      pl.BlockSpec((None, D, TF), lambda t, f, ti, e, nv: (e[t], 0, fidx(t, f, nv))),
                      pl.BlockSpec((None, TF, D), lambda t, f, ti, e, nv: (e[t], fidx(t, f, nv), 0))],
            out_specs=pl.BlockSpec((MOE_TR, D), lambda t, f, ti, e, nv: (ti[t], 0)),
            scratch_shapes=[pltpu.VMEM((MOE_TR, D), F32)]),
        out_shape=jax.ShapeDtypeStruct((rmax, D), BF16),
        compiler_params=_params("arbitrary", "arbitrary"),
        name="moe_ffn",
    )(*plan, xs, w1, w3, w2)


def _moe_combine_kernel(s_ref, q_ref, slot_ok_ref, valid_ref, first_ref, last_ref, pos_ref, *refs, ntp):
    ys_refs = refs[:MOE_CG]
    xp_ref, xs_ref, gate_ref, fg_ref, op_ref, os_ref, acc_scr = refs[MOE_CG:]
    p = pl.program_id(0)

    @pl.when(valid_ref[p] == 1)
    def _():
        rows = ys_refs[0].shape[0]
        sels = []
        for k in range(MOE_CG):
            col = (lax.broadcasted_iota(jnp.int32, (1, rows), 1) + q_ref[MOE_CG * p + k] * rows).astype(F32)
            col = jnp.where(slot_ok_ref[MOE_CG * p + k] == 1, col, -1.0)
            sels.append((jnp.where(pos_ref[:, 0:1] == col, pos_ref[:, 2:3], 0.0)
                         + jnp.where(pos_ref[:, 1:2] == col, pos_ref[:, 3:4], 0.0)).astype(BF16))
        sel = jnp.concatenate(sels, axis=1)
        ys = jnp.concatenate([r[...] for r in ys_refs], axis=0)

        @pl.when(first_ref[p] == 1)
        def _():
            acc_scr[...] = jnp.zeros_like(acc_scr)

        acc_scr[...] += _dot(sel, ys)

        @pl.when(last_ref[p] == 1)
        def _():
            s = s_ref[p]
            x = jnp.where(s < ntp, xp_ref[...], xs_ref[...])
            y = x + gate_ref[...] * acc_scr[...]
            out = y * lax.rsqrt(jnp.mean(y * y, axis=-1, keepdims=True) + EPS) * fg_ref[...]

            @pl.when(s < ntp)
            def _():
                op_ref[...] = out

            @pl.when(s >= ntp)
            def _():
                os_ref[...] = out


def _moe_combine(ys, pos_cols, plan, xp, xs, mod, final_g, rows_per_group):
    Np, D = xp.shape
    Ns = xs.shape[0]
    SB = MOE_SB
    ntp = Np // SB
    nsteps = plan[0].shape[0]

    def tile_spec(k):
        return pl.BlockSpec((MOE_TRG, D), lambda p, s, q, *_: (q[MOE_CG * p + k], 0))

    def tok_p(p, s, *_):
        return (jnp.minimum(s[p], ntp - 1), 0)

    def tok_s(p, s, *_):
        return (jnp.maximum(s[p] - ntp, 0), 0)

    def gate_idx(p, s, *_):
        return (jnp.where(s[p] < ntp, 0, 1 + ((s[p] - ntp) * SB) // rows_per_group), 5, 0, 0)

    return pl.pallas_call(
        functools.partial(_moe_combine_kernel, ntp=ntp),
        grid_spec=pltpu.PrefetchScalarGridSpec(
            num_scalar_prefetch=6, grid=(nsteps,),
            in_specs=[pl.BlockSpec((SB, SUB), lambda p, s, q, *_: (s[p], 0))]
            + [tile_spec(k) for k in range(MOE_CG)]
            + [pl.BlockSpec((SB, D), tok_p),
                      pl.BlockSpec((SB, D), tok_s),
                      pl.BlockSpec((None, None, 1, D), gate_idx),
                      pl.BlockSpec((1, D), lambda p, *_: (0, 0))],
            out_specs=[pl.BlockSpec((SB, D), tok_p), pl.BlockSpec((SB, D), tok_s)],
            scratch_shapes=[pltpu.VMEM((SB, D), F32)]),
        out_shape=[jax.ShapeDtypeStruct((Np, D), F32), jax.ShapeDtypeStruct((Ns, D), F32)],
        compiler_params=_params("arbitrary"),
        name="moe_combine",
    )(*plan, pos_cols, *([ys] * MOE_CG), xp, xs, mod, final_g.reshape(1, D))


def _moe(xp, xs, g, mod, router_w, w1, w3, w2, final_g, rows_per_group):
    N = xp.shape[0] + xs.shape[0]
    h, info, info_t, cum = _moe_route(xp, xs, g, mod, router_w, rows_per_group)
    pos_cols, pos_rows, gather_plan, combine_plan, ffn_plan, rmax = _moe_plan(info, info_t, cum, N)
    x_sorted = _moe_gather(h, pos_rows, gather_plan, rmax)
    y_sorted = _moe_ffn(x_sorted, ffn_plan, w1, w3, w2)
    return _moe_combine(y_sorted, pos_cols, combine_plan, xp, xs, mod, final_g, rows_per_group)


def _gla_levels(C):
    lv, c = [], C // 2
    while c >= SUB:
        lv.append(c)
        c //= 2
    return lv


def _gla_tables(C):
    levels = _gla_levels(C)
    nr = 2 + 2 * len(levels)
    mat = np.zeros((2, nr * C, C), np.float32)
    code = np.zeros((2, C, C), np.int32)
    for d in range(2):
        p = np.arange(C) if d == 0 else C - 1 - np.arange(C)
        pi, pj = p[:, None], p[None, :]
        mat[d, 0:C] = pj <= pi
        mat[d, C:2 * C] = pj > pi
        code[d] = np.where((pj <= pi) & (pi // SUB == pj // SUB), 1, 0)
        for lv, c in enumerate(levels):
            blk = pi // c
            later = blk % 2 == 1
            mat[d, (2 + 2 * lv) * C:(3 + 2 * lv) * C] = later & (pj > blk * c - 1) & (pj <= pi)
            mat[d, (3 + 2 * lv) * C:(4 + 2 * lv) * C] = (~later) & (pj > pi) & (pj <= (blk + 1) * c - 1)
            pair = (pi // (2 * c) == pj // (2 * c)) & (pi // c != pj // c) & (pj <= pi)
            code[d] = np.where(pair, 2 + lv, code[d])
    ones = np.zeros((SUB * LANES, C), np.float32)
    for jj in range(SUB):
        ones[jj * LANES:(jj + 1) * LANES, jj::SUB] = 1.0
    return jnp.asarray(mat, dtype=BF16), jnp.asarray(code), jnp.asarray(ones, dtype=BF16)


def _bcast_sublane(x, jj):
    r, w = x.shape
    x3 = x.reshape(r // SUB, SUB, w)
    return jnp.broadcast_to(x3[:, jj:jj + 1, :], x3.shape).reshape(r, w)


def _t128(x):
    r, w = x.shape
    if w > LANES:
        return jnp.concatenate([x[:, i:i + LANES].T for i in range(0, w, LANES)], axis=0)
    return jnp.concatenate([x[i:i + LANES, :].T for i in range(0, r, LANES)], axis=1)


def _gla_kernel(q_ref, k_ref, v_ref, g_ref, lr_ref, wg_ref, ba_ref, mat_ref, code_ref, ones_ref,
                s0f_ref, s0b_ref, ng_ref, o_ref, sf_ref, sb_ref, st_scr, of_scr, *, n, C, G):
    d = pl.program_id(1)
    c = pl.program_id(2)
    levels = _gla_levels(C)

    @pl.when(jnp.logical_and(c == 0, d == 0))
    def _():
        for bb in range(G):
            for h in range(C_H):
                st_scr[bb, h] = _t128(s0f_ref[bb, h])

    @pl.when(jnp.logical_and(c == 0, d == 1))
    def _():
        for bb in range(G):
            for h in range(C_H):
                st_scr[bb, h] = _t128(s0b_ref[bb, h])

    mat = mat_ref[...]
    code = code_ref[...]
    ones = ones_ref[...]
    cums = []
    for bb in range(G):
        xg = jnp.dot(lr_ref[bb], wg_ref[...], precision=HIGHEST, preferred_element_type=F32) + ba_ref[...]
        la = (jnp.minimum(xg, 0.0) - jnp.log1p(jnp.exp(-jnp.abs(xg)))) * (LOG2E / C_TAU)
        hi = la.astype(BF16)
        lo = (la - hi.astype(F32)).astype(BF16)
        cums.append(_dot(mat, hi) + _dot(mat, lo))

    def prepare(bb, h):
        cum = cums[bb]
        ks = slice(h * C_DK, (h + 1) * C_DK)
        qh = q_ref[bb, :, ks].astype(F32) * (C_DK ** -0.5)
        kh = k_ref[bb, :, ks].astype(F32)
        b = cum[0:C, ks]
        b_rest = cum[C:2 * C, ks]
        ps = []
        for jj in range(SUB):
            dec = jnp.exp2(jnp.minimum(b - _bcast_sublane(b, jj), 0.0))
            ps.append((qh * _bcast_sublane(kh, jj) * dec).astype(BF16))
        lv_ops = []
        for lv in range(len(levels)):
            eq = cum[(2 + 2 * lv) * C:(3 + 2 * lv) * C, ks]
            ek = cum[(3 + 2 * lv) * C:(4 + 2 * lv) * C, ks]
            lv_ops.append(((qh * jnp.exp2(eq)).astype(BF16), (kh * jnp.exp2(ek)).astype(BF16)))
        qe = (qh * jnp.exp2(b)).astype(BF16)
        ke = (kh * jnp.exp2(b_rest)).astype(BF16)
        e_end = jnp.exp2(b[0:1, :] + b_rest[0:1, :])
        return jnp.concatenate(ps, axis=1), lv_ops, qe, ke, e_end

    def contract(bb, h, prep):
        pcat, lv_ops, qe, ke, e_end = prep
        vh = v_ref[bb, :, h * C_DV:(h + 1) * C_DV].astype(F32)
        att = jnp.where(code == 1, _dot(pcat, ones), 0.0)
        for lv, (qs, ks_) in enumerate(lv_ops):
            att = jnp.where(code == 2 + lv, _dot_nt(qs, ks_), att)
        st = st_scr[bb, h]
        o = _dot(att.astype(BF16), vh.astype(BF16)) + _dot_nt(qe, st.astype(BF16))
        st_scr[bb, h] = e_end * st + _dot(_t128(vh).astype(BF16), ke)
        return o

    units = [(bb, h) for h in range(C_H) for bb in range(G)]
    outs = {}
    prep = prepare(*units[0])
    for idx, (bb, h) in enumerate(units):
        nxt = prepare(*units[idx + 1]) if idx + 1 < len(units) else None
        outs[(bb, h)] = contract(bb, h, prep)
        prep = nxt
    o_all = [jnp.concatenate([outs[(bb, h)] for h in range(C_H)], axis=-1) for bb in range(G)]

    @pl.when(d == 0)
    def _():
        for bb in range(G):
            of_scr[bb, c] = o_all[bb]

    @pl.when(d == 1)
    def _():
        for bb in range(G):
            tot = o_all[bb] + of_scr[bb, n - 1 - c]
            res = []
            for h in range(C_H):
                sl = slice(h * C_DV, (h + 1) * C_DV)
                t = tot[:, sl]
                y = t * lax.rsqrt(jnp.mean(t * t, axis=-1, keepdims=True) + EPS) * ng_ref[:, sl]
                res.append(y * _silu(g_ref[bb, :, sl].astype(F32)))
            o_ref[bb] = jnp.concatenate(res, axis=-1).astype(o_ref.dtype)

    @pl.when(jnp.logical_and(c == n - 1, d == 0))
    def _():
        for bb in range(G):
            for h in range(C_H):
                sf_ref[bb, h] = _t128(st_scr[bb, h])

    @pl.when(jnp.logical_and(c == n - 1, d == 1))
    def _():
        for bb in range(G):
            for h in range(C_H):
                sb_ref[bb, h] = _t128(st_scr[bb, h])


def _gla(z, zg, B, T, w_a2, b_a, s0f, s0b, norm_g):
    C = GLA_CHUNK
    G = _rows_per_step(B, T, C_H * C_DV)
    assert B % G == 0 and T % C == 0
    n = T // C
    HK = C_H * C_DK
    HV = C_H * C_DV
    mat, code, ones = _gla_tables(C)
    nr = mat.shape[1] // C
    wg = jnp.zeros((2, LANES, HK), F32)
    for dr in range(2):
        wg = wg.at[dr, dr * C_RANK:(dr + 1) * C_RANK, :].set(w_a2[dr])
    z3 = z.reshape(B, T, z.shape[1])
    zg3 = zg.reshape(B, T, zg.shape[1])

    def chunk(d, c):
        return c + d * (n - 1 - 2 * c)

    st_spec = pl.BlockSpec((G, C_H, C_DK, C_DV), lambda b, d, c: (b, 0, 0, 0))
    st_shape = jax.ShapeDtypeStruct((B, C_H, C_DK, C_DV), F32)
    o, sf, sb = pl.pallas_call(
        functools.partial(_gla_kernel, n=n, C=C, G=G),
        grid=(B // G, 2, n),
        in_specs=[pl.BlockSpec((G, C, HK), lambda b, d, c: (b, chunk(d, c), 0)),
                  pl.BlockSpec((G, C, HK), lambda b, d, c: (b, chunk(d, c), 1)),
                  pl.BlockSpec((G, C, HV), lambda b, d, c: (b, chunk(d, c), 1)),
                  pl.BlockSpec((G, C, HV), lambda b, d, c: (b, chunk(d, c), 2)),
                  pl.BlockSpec((G, C, LANES), lambda b, d, c: (b, chunk(d, c), 0)),
                  pl.BlockSpec((None, LANES, HK), lambda b, d, c: (d, 0, 0)),
                  pl.BlockSpec((None, 1, HK), lambda b, d, c: (d, 0, 0)),
                  pl.BlockSpec((None, nr * C, C), lambda b, d, c: (d, 0, 0)),
                  pl.BlockSpec((None, C, C), lambda b, d, c: (d, 0, 0)),
                  pl.BlockSpec((SUB * LANES, C), lambda b, d, c: (0, 0)),
                  st_spec, st_spec,
                  pl.BlockSpec((1, HV), lambda b, d, c: (0, 0))],
        out_specs=[pl.BlockSpec((G, C, HV), lambda b, d, c: (b, (n - 1) - d * c, 0)),
                   st_spec, st_spec],
        out_shape=[jax.ShapeDtypeStruct((B, T, HV), BF16), st_shape, st_shape],
        scratch_shapes=[pltpu.VMEM((G, C_H, C_DV, C_DK), F32), pltpu.VMEM((G, n, C, HV), F32)],
        compiler_params=_params("arbitrary", "arbitrary", "arbitrary"),
        name="gla",
    )(z3, z3, z3, z3, zg3, wg, b_a.reshape(2, 1, HK), mat, code, ones, s0f, s0b, norm_g.reshape(1, HV))
    return o.reshape(B * T, HV), sf, sb


def _run_stream(x, B, T, mods, ctx, p):
    N, D = x.shape
    rpg = N // mods[0].shape[0]
    TM = min(2048, rpg)
    nb = (B_H + 2 * B_HKV) * B_HD

    w_in = p['even_w_in'][0]
    z, zb = _norm_mm(x, p['norm1_g'][0], mods[0], (0, 1), w_in, w_in, (nb, MIX_MAIN // nb), TM, rpg)
    if ctx is None:
        s0 = jnp.zeros((B, A_H, A_DK, A_DV), F32)
        a_f0, a_b0, cache_k, cache_v = s0, s0, None, None
    else:
        cache_k, cache_v, a_f0, a_b0 = ctx[0], ctx[1], ctx[2], ctx[3]
    o_a, a_sf, a_sb = _retention(z, B, T, p['a_log_gamma'][0], a_f0, a_b0, p['a_norm_g'][0])
    qpad, k_norm, k_rot, v_bf = _bprep(zb, T, p['b_q_g'][0], p['b_k_g'][0], rope=ctx is not None)
    o_b = _attention(qpad, k_rot, v_bf, B, T, cache_k, cache_v)
    x = _proj_res(x, mods[0], 2, [o_a, o_b], p['even_w_out'][0], rpg)
    x = _ffn(x, p['norm2_g'][0], mods[0], p['ff_w1'], p['ff_w3'], p['ff_w2'], rpg)

    w_in = p['odd_w_in'][0]
    w_gate = jnp.pad(w_in[:, MIX_MAIN:], ((0, 0), (0, LANES - 2 * C_RANK)))
    z1, z1g = _norm_mm(x, p['norm1_g'][1], mods[1], (0, 1), w_in, w_gate, (LANES, 0), TM, rpg)
    if ctx is None:
        s0 = jnp.zeros((B, C_H, C_DK, C_DV), F32)
        c_f0, c_b0 = s0, s0
    else:
        c_f0, c_b0 = ctx[4], ctx[5]
    o_c, c_sf, c_sb = _gla(z1, z1g, B, T, p['c_w_a2'][0], p['c_b_a'][0], c_f0, c_b0, p['c_norm_g'][0])
    x = _proj_res(x, mods[1], 2, [o_c], p['odd_w_out'][0], rpg)
    v_raw = zb[:, (B_H + B_HKV) * B_HD:]
    return x, (k_norm, v_raw, a_sf, a_sb, c_sf, c_sb)


def kernel(x_prompt, x_sample, c, cache_b_k, cache_b_v, state_a_fwd, state_a_bwd, state_c_fwd, state_c_bwd,
           c_ctx, w_mod, b_mod, norm1_g, norm2_g, final_g, even_w_in, even_w_out, a_log_gamma, a_norm_g,
           b_q_g, b_k_g, odd_w_in, c_w_a2, c_b_a, c_norm_g, odd_w_out, ff_w1, ff_w3, ff_w2,
           router_w, moe_w1, moe_w3, moe_w2):
    Bp, Tp, D = x_prompt.shape
    Bs, Ts, _ = x_sample.shape
    L = w_mod.shape[0]
    assert L == 2 and even_w_in.shape[0] == 1 and odd_w_in.shape[0] == 1
    p = dict(norm1_g=norm1_g, norm2_g=norm2_g, final_g=final_g, even_w_in=even_w_in, even_w_out=even_w_out,
             a_log_gamma=a_log_gamma, a_norm_g=a_norm_g, b_q_g=b_q_g, b_k_g=b_k_g, odd_w_in=odd_w_in,
             c_w_a2=c_w_a2, c_b_a=c_b_a, c_norm_g=c_norm_g, odd_w_out=odd_w_out,
             ff_w1=_repack_cols(ff_w1[0], FFN_TF), ff_w3=_repack_cols(ff_w3[0], FFN_TF),
             ff_w2=_repack_rows(ff_w2[0], FFN_TF), router_w=router_w, moe_w1=moe_w1, moe_w3=moe_w3, moe_w2=moe_w2)

    rows = 8
    conds = jnp.concatenate([c_ctx[None, :], c, jnp.zeros((rows - 1 - Bs, D), F32)], axis=0)
    mod = _modulation(conds, w_mod, b_mod).reshape(L, rows, 6, 1, D)
    mods_p = [mod[l, 0:1] for l in range(L)]
    mods_s = [mod[l, 1:1 + Bs] for l in range(L)]

    x_p, kept = _run_stream(x_prompt.reshape(Bp * Tp, D), Bp, Tp, mods_p, None, p)
    nk = B_HKV * B_HD
    ctx = (cache_b_k[:, 0].reshape(Bs, -1, nk), cache_b_v[:, 0].reshape(Bs, -1, nk),
           state_a_fwd[:, 0], state_a_bwd[:, 0], state_c_fwd[:, 0], state_c_bwd[:, 0])
    x_s, _ = _run_stream(x_sample.reshape(Bs * Ts, D), Bs, Ts, mods_s, ctx, p)
    y_p, y_s = _moe(x_p, x_s, norm2_g[1], mod[1, 0:1 + Bs], router_w[0], moe_w1[0], moe_w3[0], moe_w2[0],
                    final_g, Ts)

    k_norm, v_raw, a_sf, a_sb, c_sf, c_sb = kept
    return (y_p.reshape(Bp, Tp, D), y_s.reshape(Bs, Ts, D),
            k_norm.reshape(Bp, 1, Tp, B_HKV, B_HD), v_raw.reshape(Bp, 1, Tp, B_HKV, B_HD),
            a_sf[:, None], a_sb[:, None], c_sf[:, None], c_sb[:, None])
```

```python
import functools

import numpy as np
import jax
import jax.numpy as jnp
from jax import lax
from jax.experimental import pallas as pl
from jax.experimental.pallas import tpu as pltpu

F32 = jnp.float32
BF16 = jnp.bfloat16
EPS = 1e-6
HIGHEST = lax.Precision.HIGHEST
LOG2E = 1.4426950408889634

VMEM_LIMIT_BYTES = 56 * 1024 * 1024

A_H, A_DK, A_DV = 4, 128, 256
B_H, B_HKV, B_HD = 8, 2, 64
C_H, C_DK, C_DV, C_RANK = 4, 128, 256, 16
C_TAU = 16.0
GRID_W = 64
ROPE_THETA = 10000.0
N_EXPERTS = 8
LANES = 128
SUB = 8
RET_CHUNK = 128
GLA_CHUNK = 128
Q_TILE = 128
SCAN_ROWS_MAX = 4
SCAN_FWD_BYTES = 32 * 1024 * 1024


def _rows_per_step(B, T, width):
    g = SCAN_ROWS_MAX
    while g > 1 and (B % g or g * T * width * 4 > SCAN_FWD_BYTES):
        g //= 2
    return g


def _params(*sem):
    return pltpu.CompilerParams(dimension_semantics=sem, vmem_limit_bytes=VMEM_LIMIT_BYTES)


def _dot(a, b):
    return jnp.dot(a, b, preferred_element_type=F32)


def _dot_nt(a, b):
    return lax.dot_general(a, b, (((1,), (1,)), ((), ())), preferred_element_type=F32)


def _silu(x):
    return x * jax.nn.sigmoid(x)


def _norm_mod(x, g, sh, sc):
    r = lax.rsqrt(jnp.mean(x * x, axis=-1, keepdims=True) + EPS)
    return (x * r * g) * (1.0 + sc) + sh


def _mod_kernel(c_ref, w_ref, b_ref, o_ref):
    c = c_ref[...]
    o_ref[...] = jnp.dot(_silu(c), w_ref[...], precision=HIGHEST, preferred_element_type=F32) + b_ref[...]


def _modulation(conds, w_mod, b_mod):
    L, D, D6 = w_mod.shape
    R = conds.shape[0]
    TN = 1024
    return pl.pallas_call(
        _mod_kernel,
        grid=(L, D6 // TN),
        in_specs=[pl.BlockSpec((R, D), lambda l, j: (0, 0)),
                  pl.BlockSpec((None, D, TN), lambda l, j: (l, 0, j)),
                  pl.BlockSpec((None, 1, TN), lambda l, j: (l, 0, j))],
        out_specs=pl.BlockSpec((None, R, TN), lambda l, j: (l, 0, j)),
        out_shape=jax.ShapeDtypeStruct((L, R, D6), F32),
        compiler_params=_params("arbitrary", "arbitrary"),
        name="modulation",
    )(conds, w_mod, b_mod.reshape(L, 1, D6))


def _mod_spec(part, D, TM, rows_per_group, axis):
    def idx(*g):
        return ((g[axis] * TM) // rows_per_group, part, 0, 0)
    return pl.BlockSpec((None, None, 1, D), idx)


MIX_MAIN = A_H * (2 * A_DK + 2 * A_DV)
MIX_TN = 768


def _norm_mm_kernel(x_ref, g_ref, sh_ref, sc_ref, w_ref, we_ref, o_ref, oe_ref, h_scr, *, nmain):
    j = pl.program_id(1)

    @pl.when(j == 0)
    def _():
        h_scr[...] = _norm_mod(x_ref[...], g_ref[...], sh_ref[...], sc_ref[...]).astype(BF16)

    @pl.when(j < nmain)
    def _():
        o_ref[...] = _dot(h_scr[...], w_ref[...].astype(BF16)).astype(o_ref.dtype)

    @pl.when(j == nmain)
    def _():
        oe_ref[...] = _dot(h_scr[...], we_ref[...].astype(BF16))


def _norm_mm(x, g, mod, parts, w, w_extra, extra_block, TM, rows_per_group):
    N, D = x.shape
    nmain = MIX_MAIN // MIX_TN
    WE = extra_block[0]
    return pl.pallas_call(
        functools.partial(_norm_mm_kernel, nmain=nmain),
        grid=(N // TM, nmain + 1),
        in_specs=[pl.BlockSpec((TM, D), lambda i, j: (i, 0)),
                  pl.BlockSpec((1, D), lambda i, j: (0, 0)),
                  _mod_spec(parts[0], D, TM, rows_per_group, 0),
                  _mod_spec(parts[1], D, TM, rows_per_group, 0),
                  pl.BlockSpec((D, MIX_TN), lambda i, j: (0, jnp.minimum(j, nmain - 1))),
                  pl.BlockSpec((D, WE), lambda i, j: (0, extra_block[1]))],
        out_specs=[pl.BlockSpec((TM, MIX_TN), lambda i, j: (i, jnp.minimum(j, nmain - 1))),
                   pl.BlockSpec((TM, WE), lambda i, j: (i, 0))],
        out_shape=[jax.ShapeDtypeStruct((N, MIX_MAIN), BF16), jax.ShapeDtypeStruct((N, WE), F32)],
        scratch_shapes=[pltpu.VMEM((TM, D), BF16)],
        compiler_params=_params("arbitrary", "arbitrary"),
        name="norm_mm",
    )(x, g.reshape(1, D), mod, mod, w, w_extra)


def _ret_kernel(lg_ref, q_ref, k_ref, v_ref, ag_ref, s0f_ref, s0b_ref, ng_ref,
                o_ref, sf_ref, sb_ref, s_scr, of_scr, *, n, C, G):
    d = pl.program_id(1)
    c = pl.program_id(2)

    @pl.when(jnp.logical_and(c == 0, d == 0))
    def _():
        s_scr[...] = s0f_ref[...]

    @pl.when(jnp.logical_and(c == 0, d == 1))
    def _():
        s_scr[...] = s0b_ref[...]

    df = d.astype(F32)
    sgn = 1.0 - 2.0 * df
    ii = lax.broadcasted_iota(jnp.int32, (C, C), 0).astype(F32)
    jj = lax.broadcasted_iota(jnp.int32, (C, C), 1).astype(F32)
    dd = (ii - jj) * sgn
    feeds = dd >= 0.0
    ddc = jnp.maximum(dd, 0.0)
    ri = lax.broadcasted_iota(jnp.int32, (C, 1), 0).astype(F32)
    pos_q = (ri + 1.0) + df * (C - 2.0 * ri - 1.0)
    pos_k = (C - 1.0 - ri) + df * (2.0 * ri - C + 1.0)
    chunk_len = jnp.full((1, A_DV), float(C), F32)

    outs = [[] for _ in range(G)]
    for h in range(A_H):
        lg = lg_ref[d, h]
        dmask = jnp.where(feeds, jnp.exp2(lg * ddc), 0.0)
        q_dec = jnp.exp2(lg * pos_q)
        k_dec = jnp.exp2(lg * pos_k)
        c_dec = jnp.exp2(lg * chunk_len)
        for bb in range(G):
            qh = q_ref[bb, :, h * A_DK:(h + 1) * A_DK].astype(F32) * (A_DK ** -0.5)
            kh = k_ref[bb, :, h * A_DK:(h + 1) * A_DK].astype(F32)
            vh = v_ref[bb, :, h * A_DV:(h + 1) * A_DV].astype(BF16)
            s = s_scr[bb, h]
            att = _dot_nt(qh.astype(BF16), kh.astype(BF16)) * dmask
            o = _dot(att.astype(BF16), vh) + _dot((qh * q_dec).astype(BF16), s.astype(BF16))
            kd = kh * k_dec
            s_scr[bb, h] = c_dec * s + _dot(kd.T.astype(BF16), vh)
            outs[bb].append(o)
    o_all = [jnp.concatenate(o, axis=-1) for o in outs]

    @pl.when(d == 0)
    def _():
        for bb in range(G):
            of_scr[bb, c] = o_all[bb]

    @pl.when(d == 1)
    def _():
        for bb in range(G):
            tot = o_all[bb] + of_scr[bb, n - 1 - c]
            res = []
            for h in range(A_H):
                sl = slice(h * A_DV, (h + 1) * A_DV)
                t = tot[:, sl]
                dev = t - jnp.mean(t, axis=-1, keepdims=True)
                y = dev * lax.rsqrt(jnp.mean(dev * dev, axis=-1, keepdims=True) + EPS) * ng_ref[:, sl]
                res.append(y * _silu(ag_ref[bb, :, sl].astype(F32)))
            o_ref[bb] = jnp.concatenate(res, axis=-1).astype(o_ref.dtype)

    @pl.when(jnp.logical_and(c == n - 1, d == 0))
    def _():
        sf_ref[...] = s_scr[...]

    @pl.when(jnp.logical_and(c == n - 1, d == 1))
    def _():
        sb_ref[...] = s_scr[...]


def _retention(z, B, T, log_gamma, s0f, s0b, norm_g):
    C = RET_CHUNK
    G = _rows_per_step(B, T, A_H * A_DV)
    assert B % G == 0 and T % C == 0
    n = T // C
    HK = A_H * A_DK
    HV = A_H * A_DV
    z3 = z.reshape(B, T, z.shape[1])

    def chunk(d, c):
        return c + d * (n - 1 - 2 * c)

    st_spec = pl.BlockSpec((G, A_H, A_DK, A_DV), lambda b, d, c: (b, 0, 0, 0))
    st_shape = jax.ShapeDtypeStruct((B, A_H, A_DK, A_DV), F32)
    o, sf, sb = pl.pallas_call(
        functools.partial(_ret_kernel, n=n, C=C, G=G),
        grid=(B // G, 2, n),
        in_specs=[pl.BlockSpec(memory_space=pltpu.SMEM),
                  pl.BlockSpec((G, C, HK), lambda b, d, c: (b, chunk(d, c), 0)),
                  pl.BlockSpec((G, C, HK), lambda b, d, c: (b, chunk(d, c), 1)),
                  pl.BlockSpec((G, C, HV), lambda b, d, c: (b, chunk(d, c), 1)),
                  pl.BlockSpec((G, C, HV), lambda b, d, c: (b, chunk(d, c), 2)),
                  st_spec, st_spec,
                  pl.BlockSpec((1, HV), lambda b, d, c: (0, 0))],
        out_specs=[pl.BlockSpec((G, C, HV), lambda b, d, c: (b, (n - 1) - d * c, 0)),
                   st_spec, st_spec],
        out_shape=[jax.ShapeDtypeStruct((B, T, HV), BF16), st_shape, st_shape],
        scratch_shapes=[pltpu.VMEM((G, A_H, A_DK, A_DV), F32), pltpu.VMEM((G, n, C, HV), F32)],
        compiler_params=_params("arbitrary", "arbitrary", "arbitrary"),
        name="retention",
    )(log_gamma * LOG2E, z3, z3, z3, z3, s0f, s0b, norm_g.reshape(1, HV))
    return o.reshape(B * T, HV), sf, sb


def _group_sum_matrix(width, group):
    i = np.arange(width)
    return jnp.asarray((i[:, None] // group == i[None, :] // group).astype(np.float32), dtype=BF16)


def _q_pad_matrix():
    m = np.zeros((B_H * B_HD, B_H * LANES), np.float32)
    g = B_H // B_HKV
    for h in range(B_H):
        for t in range(B_HD):
            m[h * B_HD + t, h * LANES + (h // g) * B_HD + t] = 1.0
    return jnp.asarray(m, dtype=BF16)


def _rope_tables(T):
    rows = T // GRID_W
    row = np.repeat(np.arange(rows, dtype=np.float64), GRID_W)
    col = np.tile(np.arange(GRID_W, dtype=np.float64), rows)
    nq = B_HD // 4
    inv = ROPE_THETA ** (-np.arange(nq, dtype=np.float64) / nq)
    ang = np.concatenate([row[:, None] * inv, col[:, None] * inv], axis=-1)
    cos = np.repeat(np.cos(ang), 2, axis=-1)
    sin = np.repeat(np.sin(ang), 2, axis=-1)
    sign = np.tile(np.array([-1.0, 1.0]), B_HD // 2)
    reps = LANES // B_HD
    return (jnp.asarray(np.tile(cos, (1, reps)), dtype=F32),
            jnp.asarray(np.tile(sin * sign, (1, reps)), dtype=F32))


def _group_rmsnorm(x, gsum, g):
    x2 = x * x
    hi = x2.astype(BF16)
    lo = (x2 - hi.astype(F32)).astype(BF16)
    ss = _dot(hi, gsum) + _dot(lo, gsum)
    return x * lax.rsqrt(ss * (1.0 / B_HD) + EPS) * g


def _rotate_pairs(x, cos, sin_signed):
    n = x.shape[1]
    lane = lax.broadcasted_iota(jnp.int32, x.shape, 1)
    partner = jnp.where(lane % 2 == 0, pltpu.roll(x, n - 1, 1), pltpu.roll(x, 1, 1))
    reps = n // LANES
    if reps > 1:
        cos = jnp.concatenate([cos] * reps, axis=1)
        sin_signed = jnp.concatenate([sin_signed] * reps, axis=1)
    return x * cos + partner * sin_signed


def _bprep_kernel(z_ref, qg_ref, kg_ref, cos_ref, sin_ref, gq_ref, gk_ref, pad_ref,
                  qpad_ref, kn_ref, kr_ref, vb_ref, *, rope):
    nq = B_H * B_HD
    nk = B_HKV * B_HD
    qn = _group_rmsnorm(z_ref[:, 0:nq], gq_ref[...], qg_ref[...])
    kn = _group_rmsnorm(z_ref[:, nq:nq + nk], gk_ref[...], kg_ref[...])
    kn_ref[...] = kn
    if rope:
        qn = _rotate_pairs(qn, cos_ref[...], sin_ref[...])
        kn = _rotate_pairs(kn, cos_ref[...], sin_ref[...])
    kr_ref[...] = kn.astype(BF16)
    vb_ref[...] = z_ref[:, nq + nk:nq + 2 * nk].astype(BF16)
    qs = (qn * (B_HD ** -0.5 * LOG2E)).astype(BF16)
    qpad_ref[...] = _dot(qs, pad_ref[...]).astype(BF16)


def _bprep(z, T, q_g, k_g, rope):
    N = z.shape[0]
    TM = min(512, T)
    nq = B_H * B_HD
    nk = B_HKV * B_HD
    width = nq + 2 * nk
    assert z.shape[1] == width
    cos, sin = _rope_tables(T if rope else TM)
    nt = T // TM if rope else 1
    const = lambda i: (0, 0)
    return pl.pallas_call(
        functools.partial(_bprep_kernel, rope=rope),
        grid=(N // TM,),
        in_specs=[pl.BlockSpec((TM, width), lambda i: (i, 0)),
                  pl.BlockSpec((1, nq), const),
                  pl.BlockSpec((1, nk), const),
                  pl.BlockSpec((TM, LANES), lambda i: (i % nt, 0)),
                  pl.BlockSpec((TM, LANES), lambda i: (i % nt, 0)),
                  pl.BlockSpec((nq, nq), const),
                  pl.BlockSpec((nk, nk), const),
                  pl.BlockSpec((nq, B_H * LANES), const)],
        out_specs=[pl.BlockSpec((TM, B_H * LANES), lambda i: (i, 0)),
                   pl.BlockSpec((TM, nk), lambda i: (i, 0)),
                   pl.BlockSpec((TM, nk), lambda i: (i, 0)),
                   pl.BlockSpec((TM, nk), lambda i: (i, 0))],
        out_shape=[jax.ShapeDtypeStruct((N, B_H * LANES), BF16),
                   jax.ShapeDtypeStruct((N, nk), F32),
                   jax.ShapeDtypeStruct((N, nk), BF16),
                   jax.ShapeDtypeStruct((N, nk), BF16)],
        compiler_params=_params("arbitrary"),
        name="attn_prep",
    )(z, jnp.tile(q_g, B_H).reshape(1, nq), jnp.tile(k_g, B_HKV).reshape(1, nk), cos, sin,
      _group_sum_matrix(nq, B_HD), _group_sum_matrix(nk, B_HD), _q_pad_matrix())


def _lane_fold(x, op):
    acc = x[:, 0:LANES]
    for j in range(1, x.shape[1] // LANES):
        acc = op(acc, x[:, j * LANES:(j + 1) * LANES])
    return acc


def _attn_kernel(*refs, has_cache, kc, nq):
    if has_cache:
        q_ref, k_ref, v_ref, ck_ref, cv_ref, o_ref, s_scr, m_scr, mprev_scr, l_scr, acc_scr = refs
        kcc = min(kc, ck_ref.shape[0])
        ncache = ck_ref.shape[0] // kcc
    else:
        q_ref, k_ref, v_ref, o_ref, s_scr, m_scr, mprev_scr, l_scr, acc_scr = refs
        kcc, ncache = kc, 0
    i = pl.program_id(1)
    tq = q_ref.shape[0]
    nlat = k_ref.shape[0] // kc

    def score(c, kblk):
        q = jnp.concatenate([q_ref[:, h * LANES:(h + 1) * LANES] for h in range(B_H)], axis=0)
        s = _dot_nt(q, kblk)
        s_scr[c, :, 0:kblk.shape[0]] = s
        m_scr[...] = jnp.maximum(m_scr[...], _lane_fold(s, jnp.maximum))

    def weight(c, vblk):
        s = s_scr[c, :, 0:vblk.shape[0]]
        mp = mprev_scr[...]
        ps = [jnp.exp2(s[:, j * LANES:(j + 1) * LANES] - mp) for j in range(vblk.shape[0] // LANES)]
        tot = ps[0]
        for pj in ps[1:]:
            tot = tot + pj
        l_scr[...] += tot
        acc_scr[...] += _dot(jnp.concatenate(ps, axis=1).astype(BF16), vblk)

    def run(do_weight, do_score):
        def unit(c, kblk, vblk):
            if do_weight:
                weight(c, vblk())
            if do_score:
                score(c, kblk())

        for c in range(ncache):
            unit(c, lambda: ck_ref[c * kcc:(c + 1) * kcc, :].astype(BF16),
                 lambda: cv_ref[c * kcc:(c + 1) * kcc, :].astype(BF16))

        def body(c, carry):
            rows = pl.ds(pl.multiple_of(c * kc, kc), kc)
            unit(ncache + c, lambda: k_ref[rows, :], lambda: v_ref[rows, :])
            return carry
        lax.fori_loop(0, nlat, body, 0)

    @pl.when(i < nq)
    def _():
        m_scr[...] = jnp.full(m_scr.shape, -jnp.inf, F32)

    @pl.when(i > 0)
    def _():
        l_scr[...] = jnp.zeros_like(l_scr)
        acc_scr[...] = jnp.zeros_like(acc_scr)

    @pl.when(i == 0)
    def _():
        run(False, True)

    @pl.when(jnp.logical_and(i > 0, i < nq))
    def _():
        run(True, True)

    @pl.when(i == nq)
    def _():
        run(True, False)

    @pl.when(i > 0)
    def _():
        r_all = acc_scr[...] / jnp.sum(l_scr[...], axis=-1, keepdims=True)
        g = B_H // B_HKV
        lane = lax.broadcasted_iota(jnp.int32, (tq, LANES), 1)
        outs = []
        for j in range(B_H // 2):
            pair = []
            for half in range(2):
                h = 2 * j + half
                r = r_all[h * tq:(h + 1) * tq, :]
                if h // g != half:
                    r = pltpu.roll(r, B_HD, 1)
                pair.append(r)
            outs.append(jnp.where(lane < B_HD, pair[0], pair[1]))
        o_ref[...] = jnp.concatenate(outs, axis=-1).astype(o_ref.dtype)

    @pl.when(i < nq)
    def _():
        mprev_scr[...] = jnp.broadcast_to(jnp.max(m_scr[...], axis=-1, keepdims=True), mprev_scr.shape)


def _attention(qpad, kr, vb, B, T, cache_k, cache_v):
    has_cache = cache_k is not None
    TQ = Q_TILE
    nq = T // TQ
    nk = B_HKV * B_HD
    in_specs = [pl.BlockSpec((TQ, B_H * LANES), lambda b, i: (b * nq + jnp.minimum(i, nq - 1), 0)),
                pl.BlockSpec((T, nk), lambda b, i: (b, 0)),
                pl.BlockSpec((T, nk), lambda b, i: (b, 0))]
    args = [qpad, kr, vb]
    kc = min(1024, T)
    nchunks = T // kc
    if has_cache:
        P = cache_k.shape[1]
        assert P % min(kc, P) == 0
        nchunks += P // min(kc, P)
        in_specs += [pl.BlockSpec((None, P, nk), lambda b, i: (b, 0, 0))] * 2
        args += [cache_k, cache_v]
    R = B_H * TQ
    return pl.pallas_call(
        functools.partial(_attn_kernel, has_cache=has_cache, kc=kc, nq=nq),
        grid=(B, nq + 1),
        in_specs=in_specs,
        out_specs=pl.BlockSpec((TQ, B_H * B_HD), lambda b, i: (b * nq + jnp.maximum(i - 1, 0), 0)),
        out_shape=jax.ShapeDtypeStruct((B * T, B_H * B_HD), BF16),
        scratch_shapes=[pltpu.VMEM((nchunks, R, kc), F32)] + [pltpu.VMEM((R, LANES), F32)] * 4,
        compiler_params=_params("arbitrary", "arbitrary"),
        name="attention",
    )(*args)


def _proj_res_kernel(*refs, n_in):
    x_ref, gate_ref = refs[0], refs[1]
    o_refs = refs[2:2 + n_in]
    w_refs = refs[2 + n_in:2 + 2 * n_in]
    out_ref = refs[2 + 2 * n_in]
    wbf_refs = refs[3 + 2 * n_in:]

    @pl.when(pl.program_id(0) == 0)
    def _():
        for w_ref, wbf_ref in zip(w_refs, wbf_refs):
            wbf_ref[...] = w_ref[...].astype(BF16)

    acc = _dot(o_refs[0][...], wbf_refs[0][...])
    for o_ref, wbf_ref in zip(o_refs[1:], wbf_refs[1:]):
        acc = acc + _dot(o_ref[...], wbf_ref[...])
    out_ref[...] = x_ref[...] + gate_ref[...] * acc


def _proj_res(x, mod, part, acts, w, rows_per_group):
    N, D = x.shape
    TM = min(1024, rows_per_group)
    n_in = len(acts)
    widths = [a.shape[1] for a in acts]
    offs = np.cumsum([0] + widths[:-1]).tolist()
    in_specs = [pl.BlockSpec((TM, D), lambda i: (i, 0)),
                _mod_spec(part, D, TM, rows_per_group, 0)]
    in_specs += [pl.BlockSpec((TM, wd), lambda i: (i, 0)) for wd in widths]
    in_specs += [pl.BlockSpec((wd, D), functools.partial(lambda i, blk: (blk, 0), blk=off // wd))
                 for wd, off in zip(widths, offs)]
    return pl.pallas_call(
        functools.partial(_proj_res_kernel, n_in=n_in),
        grid=(N // TM,),
        in_specs=in_specs,
        out_specs=pl.BlockSpec((TM, D), lambda i: (i, 0)),
        out_shape=jax.ShapeDtypeStruct((N, D), F32),
        scratch_shapes=[pltpu.VMEM((wd, D), BF16) for wd in widths],
        compiler_params=_params("arbitrary"),
        name="proj_residual",
    )(x, mod, *acts, *([w] * n_in))


def _ffn_kernel(x_ref, g_ref, sh_ref, sc_ref, gate_ref, w1_ref, w3_ref, w2_ref, out_ref, h_scr, acc_scr, *, nf):
    f = pl.program_id(1)

    @pl.when(f == 0)
    def _():
        h_scr[...] = _norm_mod(x_ref[...], g_ref[...], sh_ref[...], sc_ref[...]).astype(BF16)
        acc_scr[...] = jnp.zeros_like(acc_scr)

    h = h_scr[...]
    a = _dot(h, w1_ref[...].astype(BF16))
    b = _dot(h, w3_ref[...].astype(BF16))
    acc_scr[...] += _dot((_silu(a) * b).astype(BF16), w2_ref[...].astype(BF16))

    @pl.when(f == nf - 1)
    def _():
        out_ref[...] = x_ref[...] + gate_ref[...] * acc_scr[...]


def _ffn(x, g, mod, w1, w3, w2, rows_per_group):
    N, D = x.shape
    FF = w1.shape[1]
    TM, TF = min(1024, rows_per_group), 256
    nf = FF // TF
    return pl.pallas_call(
        functools.partial(_ffn_kernel, nf=nf),
        grid=(N // TM, nf),
        in_specs=[pl.BlockSpec((TM, D), lambda i, f: (i, 0)),
                  pl.BlockSpec((1, D), lambda i, f: (0, 0)),
                  _mod_spec(3, D, TM, rows_per_group, 0),
                  _mod_spec(4, D, TM, rows_per_group, 0),
                  _mod_spec(5, D, TM, rows_per_group, 0),
                  pl.BlockSpec((D, TF), lambda i, f: (0, f)),
                  pl.BlockSpec((D, TF), lambda i, f: (0, f)),
                  pl.BlockSpec((TF, D), lambda i, f: (f, 0))],
        out_specs=pl.BlockSpec((TM, D), lambda i, f: (i, 0)),
        out_shape=jax.ShapeDtypeStruct((N, D), F32),
        scratch_shapes=[pltpu.VMEM((TM, D), BF16), pltpu.VMEM((TM, D), F32)],
        compiler_params=_params("arbitrary", "arbitrary"),
        name="ffn",
    )(x, g.reshape(1, D), mod, mod, mod, w1, w3, w2)


MOE_SB = 1024
MOE_SBG = 512
MOE_GG = 4
MOE_TRG = 256
MOE_TR = 2048
MOE_CG = 4


def _two_stream_specs(shape, ntp, ax=0):
    def idx_p(*g):
        return (jnp.minimum(g[ax], ntp - 1), 0)

    def idx_s(*g):
        return (jnp.maximum(g[ax] - ntp, 0), 0)
    return pl.BlockSpec(shape, idx_p), pl.BlockSpec(shape, idx_s)


def _pool_mod_spec(part, D, TM, ntp, rows_per_group):
    def idx(i, *_):
        return (jnp.where(i < ntp, 0, 1 + ((i - ntp) * TM) // rows_per_group), part, 0, 0)
    return pl.BlockSpec((None, None, 1, D), idx)


def _route_kernel(xp_ref, xs_ref, g_ref, sh_ref, sc_ref, rw_ref, tri_ref, h_ref, info_ref, infot_ref, cum_ref,
                  carry_scr, *, ntp):
    i = pl.program_id(0)

    @pl.when(i == 0)
    def _():
        carry_scr[...] = jnp.zeros_like(carry_scr)

    x = jnp.where(i < ntp, xp_ref[...], xs_ref[...])
    h = _norm_mod(x, g_ref[...], sh_ref[...], sc_ref[...])
    h_ref[...] = h.astype(BF16)
    lane = lax.broadcasted_iota(jnp.int32, (x.shape[0], LANES), 1).astype(F32)
    rw = rw_ref[...]
    h_hi = h.astype(BF16)
    h_lo = (h - h_hi.astype(F32)).astype(BF16)
    rw_hi = rw.astype(BF16)
    rw_lo = (rw - rw_hi.astype(F32)).astype(BF16)
    logits = _dot(h_hi, rw_hi) + (_dot(h_hi, rw_lo) + _dot(h_lo, rw_hi))
    logits = jnp.where(lane < N_EXPERTS, logits, -jnp.inf)
    m1 = jnp.max(logits, axis=-1, keepdims=True)
    i1 = jnp.min(jnp.where(logits == m1, lane, float(LANES)), axis=-1, keepdims=True)
    rest = jnp.where(lane == i1, -jnp.inf, logits)
    m2 = jnp.max(rest, axis=-1, keepdims=True)
    i2 = jnp.min(jnp.where(rest == m2, lane, float(LANES)), axis=-1, keepdims=True)
    e2 = jnp.exp(m2 - m1)
    w1 = 1.0 / (1.0 + e2)
    w2 = e2 / (1.0 + e2)
    ind = jnp.where(jnp.logical_or(lane == i1, lane == i2), 1.0, 0.0)
    before = _dot(tri_ref[...], ind.astype(BF16)) + carry_scr[...]
    r1 = jnp.sum(jnp.where(lane == i1, before, 0.0), axis=-1, keepdims=True)
    r2 = jnp.sum(jnp.where(lane == i2, before, 0.0), axis=-1, keepdims=True)
    total = carry_scr[...] + jnp.sum(ind, axis=0, keepdims=True)
    carry_scr[...] = total
    for part in range(1, MOE_SB // MOE_SBG):
        cum_ref[part - 1] = before[part * MOE_SBG:part * MOE_SBG + 1, :]
    cum_ref[MOE_SB // MOE_SBG - 1] = total
    info = jnp.where(lane == 0.0, i1, jnp.where(lane == 1.0, i2, jnp.where(lane == 2.0, w1, jnp.where(
        lane == 3.0, w2, jnp.where(lane == 4.0, r1, jnp.where(lane == 5.0, r2, 0.0))))))
    info_ref[...] = info[:, 0:SUB]
    info_t = jnp.concatenate([info[r:r + LANES, :].T for r in range(0, info.shape[0], LANES)], axis=1)
    infot_ref[...] = info_t[0:SUB, :]


def _moe_route(xp, xs, g, mod, router_w, rows_per_group):
    Np, D = xp.shape
    N = Np + xs.shape[0]
    TM = MOE_SB
    ntp = Np // TM
    nt = N // TM
    rw = jnp.pad(router_w, ((0, 0), (0, LANES - router_w.shape[1])))
    tri = jnp.asarray(np.tril(np.ones((TM, TM), np.float32), -1), dtype=BF16)
    xp_spec, xs_spec = _two_stream_specs((TM, D), ntp)
    return pl.pallas_call(
        functools.partial(_route_kernel, ntp=ntp),
        grid=(nt,),
        in_specs=[xp_spec, xs_spec,
                  pl.BlockSpec((1, D), lambda i: (0, 0)),
                  _pool_mod_spec(3, D, TM, ntp, rows_per_group),
                  _pool_mod_spec(4, D, TM, ntp, rows_per_group),
                  pl.BlockSpec((D, LANES), lambda i: (0, 0)),
                  pl.BlockSpec((TM, TM), lambda i: (0, 0))],
        out_specs=[pl.BlockSpec((TM, D), lambda i: (i, 0)),
                   pl.BlockSpec((TM, SUB), lambda i: (i, 0)),
                   pl.BlockSpec((SUB, TM), lambda i: (0, i)),
                   pl.BlockSpec((MOE_SB // MOE_SBG, 1, LANES), lambda i: (i, 0, 0))],
        out_shape=[jax.ShapeDtypeStruct((N, D), BF16),
                   jax.ShapeDtypeStruct((N, SUB), F32),
                   jax.ShapeDtypeStruct((SUB, N), F32),
                   jax.ShapeDtypeStruct((nt * (MOE_SB // MOE_SBG), 1, LANES), F32)],
        scratch_shapes=[pltpu.VMEM((1, LANES), F32)],
        compiler_params=_params("arbitrary"),
        name="moe_route",
    )(xp, xs, g.reshape(1, D), mod, mod, rw, tri)


def _moe_plan(info, info_t, cum, N):
    E, SB, TRG, TR = N_EXPERTS, MOE_SB, MOE_TRG, MOE_TR
    NB = N // SB
    rmax = 2 * N + E * TR
    RG, RT = rmax // TRG, rmax // TR
    PMAX = RG + E * NB
    i32 = jnp.int32
    parts = SB // MOE_SBG
    cum_g = cum[:, 0, :E].astype(i32).T
    cum_e = cum_g[:, parts - 1::parts]
    cnt = cum_e[:, -1]
    tiles = (cnt + TR - 1) // TR
    start = TR * (jnp.cumsum(tiles) - tiles)

    startf = start.astype(F32)

    def region_start(e):
        out = jnp.zeros_like(e)
        for k in range(E):
            out = jnp.where(e == float(k), startf[k], out)
        return out

    pos_cols = jnp.concatenate([region_start(info[:, 0:2]) + info[:, 4:6], info[:, 2:4],
                                jnp.zeros((N, 4), F32)], axis=1)
    pos_rows = jnp.concatenate([region_start(info_t[0:2]) + info_t[4:6], jnp.zeros((6, N), F32)],
                               axis=0)

    def region(row0):
        e = jnp.clip(jnp.sum(row0[:, None] >= start[None, :], axis=1) - 1, 0, E - 1)
        return e, row0 - start[e]

    eq, lo = region(jnp.arange(RG, dtype=i32) * TRG)
    hi = jnp.minimum(lo + TRG, cnt[eq])
    first = jnp.sum(cum_e[eq] <= lo[:, None], axis=1)
    last = jnp.sum(cum_e[eq] < hi[:, None], axis=1)
    nblk = jnp.where(hi > lo, last - first + 1, 0)
    pend = jnp.cumsum(nblk)
    npairs = pend[-1]
    p = jnp.arange(PMAX, dtype=i32)
    valid = p < npairs
    pc = jnp.minimum(p, npairs - 1)
    q_of = jnp.minimum(jnp.sum(pend[None, :] <= pc[:, None], axis=1), RG - 1).astype(i32)
    pstart = pend - nblk
    s_of = (first[q_of] + pc - pstart[q_of]).astype(i32)

    GG = MOE_GG
    first_g = jnp.sum(cum_g[eq] <= lo[:, None], axis=1)
    last_g = jnp.sum(cum_g[eq] < hi[:, None], axis=1)
    nblk_g = jnp.where(hi > lo, last_g - first_g + 1, 0)
    nst = (nblk_g + GG - 1) // GG
    gst_end = jnp.cumsum(nst)
    gtotal = gst_end[-1]
    smax_g = (RG + E * NB * parts + (GG - 1) * RG) // GG + 1
    jg = jnp.arange(smax_g, dtype=i32)
    g_ok = jg < gtotal
    jgc = jnp.minimum(jg, gtotal - 1)
    tq = jnp.minimum(jnp.sum(gst_end[None, :] <= jgc[:, None], axis=1), RG - 1).astype(i32)
    gg = jgc - (gst_end - nst)[tq]
    last_part = first_g[tq] + nblk_g[tq] - 1
    g_parts = jnp.minimum((first_g[tq] + GG * gg)[:, None] + jnp.arange(GG, dtype=i32)[None, :], last_part[:, None])
    g_slots = jnp.where(g_ok, jnp.clip(nblk_g[tq] - GG * gg, 0, GG), 0)
    g_first = jnp.logical_and(g_ok, gg == 0)
    gather_plan = (tq, g_parts.reshape(-1).astype(i32), g_slots.astype(i32), g_first.astype(i32))

    order = jnp.argsort(jnp.where(valid, s_of * RG + q_of, jnp.iinfo(jnp.int32).max))
    s2, q2 = s_of[order], q_of[order]
    CG = MOE_CG
    blocks = jnp.arange(NB, dtype=i32)
    per_blk = jnp.sum(jnp.logical_and(valid[None, :], s2[None, :] == blocks[:, None]), axis=1)
    pb_end = jnp.cumsum(per_blk)
    pb_start = pb_end - per_blk
    nsteps = (per_blk + CG - 1) // CG
    st_end = jnp.cumsum(nsteps)
    total = st_end[-1]
    SMAX = (PMAX + CG - 1) // CG + NB
    j = jnp.arange(SMAX, dtype=i32)
    step_ok = j < total
    jc = jnp.minimum(j, total - 1)
    blk = jnp.minimum(jnp.sum(st_end[None, :] <= jc[:, None], axis=1), NB - 1).astype(i32)
    grp = jc - (st_end - nsteps)[blk]
    slot_p = pb_start[blk][:, None] + CG * grp[:, None] + jnp.arange(CG, dtype=i32)[None, :]
    slot_ok = jnp.logical_and(slot_p < pb_end[blk][:, None], step_ok[:, None])
    slot_q = jnp.where(slot_ok, q2[jnp.minimum(slot_p, npairs - 1)], q2[pb_start[blk]][:, None])
    c_slots = jnp.sum(slot_ok, axis=1).astype(i32)
    c_first = jnp.logical_and(step_ok, grp == 0).astype(i32)
    c_last = jnp.logical_and(step_ok, grp == nsteps[blk] - 1).astype(i32)
    combine_plan = (blk, slot_q.reshape(-1).astype(i32), c_slots, c_first, c_last)

    te, tlo = region(jnp.arange(RT, dtype=i32) * TR)
    tvalid = jnp.clip(cnt[te] - tlo, 0, TR)
    last_t = jnp.sum(tiles) - 1
    t_idx = jnp.where(tvalid > 0, jnp.arange(RT, dtype=i32), last_t).astype(i32)
    ffn_plan = (t_idx, te[t_idx].astype(i32), tvalid.astype(i32))
    return pos_cols, pos_rows, gather_plan, combine_plan, ffn_plan, rmax


def _moe_gather_kernel(q_ref, s_ref, slots_ref, first_ref, *refs):
    pos_refs, h_refs, out_ref = refs[:MOE_GG], refs[MOE_GG:2 * MOE_GG], refs[2 * MOE_GG]
    p = pl.program_id(0)
    rows = out_ref.shape[0]

    @pl.when(first_ref[p] == 1)
    def _():
        out_ref[...] = jnp.zeros_like(out_ref)

    for ns in range(1, MOE_GG + 1):
        @pl.when(slots_ref[p] == ns)
        def _(ns=ns):
            row = (lax.broadcasted_iota(jnp.int32, (rows, 1), 0) + q_ref[p] * rows).astype(F32)
            sels = []
            for k in range(ns):
                hit = jnp.logical_or(pos_refs[k][0:1, :] == row, pos_refs[k][1:2, :] == row)
                sels.append(jnp.where(hit, 1.0, 0.0).astype(BF16))
            sel = sels[0] if ns == 1 else jnp.concatenate(sels, axis=1)
            hs = h_refs[0][...] if ns == 1 else jnp.concatenate([h_refs[k][...] for k in range(ns)], axis=0)
            out_ref[...] = out_ref[...] + _dot(sel, hs).astype(BF16)


def _moe_gather(h, pos_rows, plan, rmax):
    N, D = h.shape
    nsteps = plan[0].shape[0]

    def pos_spec(k):
        return pl.BlockSpec((SUB, MOE_SBG), lambda p, q, s, *_: (0, s[MOE_GG * p + k]))

    def tok_spec(k):
        return pl.BlockSpec((MOE_SBG, D), lambda p, q, s, *_: (s[MOE_GG * p + k], 0))

    return pl.pallas_call(
        _moe_gather_kernel,
        grid_spec=pltpu.PrefetchScalarGridSpec(
            num_scalar_prefetch=4, grid=(nsteps,),
            in_specs=[pos_spec(k) for k in range(MOE_GG)] + [tok_spec(k) for k in range(MOE_GG)],
            out_specs=pl.BlockSpec((MOE_TRG, D), lambda p, q, *_: (q[p], 0))),
        out_shape=jax.ShapeDtypeStruct((rmax, D), BF16),
        compiler_params=_params("arbitrary"),
        name="moe_gather",
    )(*plan, *([pos_rows] * MOE_GG), *([h] * MOE_GG))


def _moe_ffn_kernel(t_ref, e_ref, nv_ref, x_ref, w1_ref, w3_ref, w2_ref, out_ref, acc_scr, *, nf):
    t = pl.program_id(0)
    f = pl.program_id(1)
    nv = nv_ref[t]

    def block(start, size):
        rows = pl.ds(start, size)

        @pl.when(f == 0)
        def _():
            acc_scr[rows, :] = jnp.zeros((size, acc_scr.shape[1]), F32)

        x = x_ref[rows, :]
        a = _dot(x, w1_ref[...].astype(BF16))
        b = _dot(x, w3_ref[...].astype(BF16))
        acc_scr[rows, :] += _dot((_silu(a) * b).astype(BF16), w2_ref[...].astype(BF16))

        @pl.when(f == nf - 1)
        def _():
            out_ref[rows, :] = acc_scr[rows, :].astype(out_ref.dtype)

    nsub = MOE_TR // MOE_TRG
    used = (nv + MOE_TRG - 1) // MOE_TRG

    @pl.when(used == nsub)
    def _():
        block(0, MOE_TR)

    @pl.when(jnp.logical_and(used > 0, used < nsub))
    def _():
        start = jnp.int32(0)
        size = MOE_TR // 2
        while size >= MOE_TRG:
            has = (used & (size // MOE_TRG)) != 0

            @pl.when(has)
            def _(start=start, size=size):
                block(pl.multiple_of(start, MOE_TRG), size)

            start = start + jnp.where(has, size, 0)
            size //= 2


def _moe_ffn(xs, plan, w1, w3, w2):
    rmax, D = xs.shape
    FF = w1.shape[2]
    TF = 256
    nf = FF // TF
    RT = rmax // MOE_TR

    def fidx(t, f, nv):
        return jnp.where(nv[t] > 0, f, nf - 1)

    return pl.pallas_call(
        functools.partial(_moe_ffn_kernel, nf=nf),
        grid_spec=pltpu.PrefetchScalarGridSpec(
            num_scalar_prefetch=3, grid=(RT, nf),
            in_specs=[pl.BlockSpec((MOE_TR, D), lambda t, f, ti, e, nv: (ti[t], 0)),
                      pl.BlockSpec((None, D, TF), lambda t, f, ti, e, nv: (e[t], 0, fidx(t, f, nv))),
                      pl.BlockSpec((None, D, TF), lambda t, f, ti, e, nv: (e[t], 0, fidx(t, f, nv))),
                      pl.BlockSpec((None, TF, D), lambda t, f, ti, e, nv: (e[t], fidx(t, f, nv), 0))],
            out_specs=pl.BlockSpec((MOE_TR, D), lambda t, f, ti, e, nv: (ti[t], 0)),
            scratch_shapes=[pltpu.VMEM((MOE_TR, D), F32)]),
        out_shape=jax.ShapeDtypeStruct((rmax, D), BF16),
        compiler_params=_params("arbitrary", "arbitrary"),
        name="moe_ffn",
    )(*plan, xs, w1, w3, w2)


def _moe_combine_kernel(s_ref, q_ref, slots_ref, first_ref, last_ref, pos_ref, *refs, ntp):
    ys_refs = refs[:MOE_CG]
    xp_ref, xs_ref, gate_ref, fg_ref, op_ref, os_ref, acc_scr = refs[MOE_CG:]
    p = pl.program_id(0)
    rows = ys_refs[0].shape[0]

    @pl.when(first_ref[p] == 1)
    def _():
        acc_scr[...] = jnp.zeros_like(acc_scr)

    for ns in range(1, MOE_CG + 1):
        @pl.when(slots_ref[p] == ns)
        def _(ns=ns):
            sels = []
            for k in range(ns):
                col = (lax.broadcasted_iota(jnp.int32, (1, rows), 1) + q_ref[MOE_CG * p + k] * rows).astype(F32)
                sels.append((jnp.where(pos_ref[:, 0:1] == col, pos_ref[:, 2:3], 0.0)
                             + jnp.where(pos_ref[:, 1:2] == col, pos_ref[:, 3:4], 0.0)).astype(BF16))
            sel = sels[0] if ns == 1 else jnp.concatenate(sels, axis=1)
            ys = ys_refs[0][...] if ns == 1 else jnp.concatenate([ys_refs[k][...] for k in range(ns)], axis=0)
            acc_scr[...] += _dot(sel, ys)

    @pl.when(last_ref[p] == 1)
    def _():
        s = s_ref[p]
        x = jnp.where(s < ntp, xp_ref[...], xs_ref[...])
        y = x + gate_ref[...] * acc_scr[...]
        out = y * lax.rsqrt(jnp.mean(y * y, axis=-1, keepdims=True) + EPS) * fg_ref[...]

        @pl.when(s < ntp)
        def _():
            op_ref[...] = out

        @pl.when(s >= ntp)
        def _():
            os_ref[...] = out


def _moe_combine(ys, pos_cols, plan, xp, xs, mod, final_g, rows_per_group):
    Np, D = xp.shape
    Ns = xs.shape[0]
    SB = MOE_SB
    ntp = Np // SB
    nsteps = plan[0].shape[0]

    def tile_spec(k):
        return pl.BlockSpec((MOE_TRG, D), lambda p, s, q, *_: (q[MOE_CG * p + k], 0))

    def tok_p(p, s, *_):
        return (jnp.minimum(s[p], ntp - 1), 0)

    def tok_s(p, s, *_):
        return (jnp.maximum(s[p] - ntp, 0), 0)

    def gate_idx(p, s, *_):
        return (jnp.where(s[p] < ntp, 0, 1 + ((s[p] - ntp) * SB) // rows_per_group), 5, 0, 0)

    return pl.pallas_call(
        functools.partial(_moe_combine_kernel, ntp=ntp),
        grid_spec=pltpu.PrefetchScalarGridSpec(
            num_scalar_prefetch=5, grid=(nsteps,),
            in_specs=[pl.BlockSpec((SB, SUB), lambda p, s, q, *_: (s[p], 0))]
            + [tile_spec(k) for k in range(MOE_CG)]
            + [pl.BlockSpec((SB, D), tok_p),
                      pl.BlockSpec((SB, D), tok_s),
                      pl.BlockSpec((None, None, 1, D), gate_idx),
                      pl.BlockSpec((1, D), lambda p, *_: (0, 0))],
            out_specs=[pl.BlockSpec((SB, D), tok_p), pl.BlockSpec((SB, D), tok_s)],
            scratch_shapes=[pltpu.VMEM((SB, D), F32)]),
        out_shape=[jax.ShapeDtypeStruct((Np, D), F32), jax.ShapeDtypeStruct((Ns, D), F32)],
        compiler_params=_params("arbitrary"),
        name="moe_combine",
    )(*plan, pos_cols, *([ys] * MOE_CG), xp, xs, mod, final_g.reshape(1, D))


def _moe(xp, xs, g, mod, router_w, w1, w3, w2, final_g, rows_per_group):
    N = xp.shape[0] + xs.shape[0]
    h, info, info_t, cum = _moe_route(xp, xs, g, mod, router_w, rows_per_group)
    pos_cols, pos_rows, gather_plan, combine_plan, ffn_plan, rmax = _moe_plan(info, info_t, cum, N)
    x_sorted = _moe_gather(h, pos_rows, gather_plan, rmax)
    y_sorted = _moe_ffn(x_sorted, ffn_plan, w1, w3, w2)
    return _moe_combine(y_sorted, pos_cols, combine_plan, xp, xs, mod, final_g, rows_per_group)


def _gla_levels(C):
    lv, c = [], C // 2
    while c >= SUB:
        lv.append(c)
        c //= 2
    return lv


def _gla_tables(C):
    levels = _gla_levels(C)
    nr = 2 + 2 * len(levels)
    mat = np.zeros((2, nr * C, C), np.float32)
    code = np.zeros((2, C, C), np.int32)
    for d in range(2):
        p = np.arange(C) if d == 0 else C - 1 - np.arange(C)
        pi, pj = p[:, None], p[None, :]
        mat[d, 0:C] = pj <= pi
        mat[d, C:2 * C] = pj > pi
        code[d] = np.where((pj <= pi) & (pi // SUB == pj // SUB), 1, 0)
        for lv, c in enumerate(levels):
            blk = pi // c
            later = blk % 2 == 1
            mat[d, (2 + 2 * lv) * C:(3 + 2 * lv) * C] = later & (pj > blk * c - 1) & (pj <= pi)
            mat[d, (3 + 2 * lv) * C:(4 + 2 * lv) * C] = (~later) & (pj > pi) & (pj <= (blk + 1) * c - 1)
            pair = (pi // (2 * c) == pj // (2 * c)) & (pi // c != pj // c) & (pj <= pi)
            code[d] = np.where(pair, 2 + lv, code[d])
    ones = np.zeros((SUB * LANES, C), np.float32)
    for jj in range(SUB):
        ones[jj * LANES:(jj + 1) * LANES, jj::SUB] = 1.0
    return jnp.asarray(mat, dtype=BF16), jnp.asarray(code), jnp.asarray(ones, dtype=BF16)


def _bcast_sublane(x, jj):
    r, w = x.shape
    x3 = x.reshape(r // SUB, SUB, w)
    return jnp.broadcast_to(x3[:, jj:jj + 1, :], x3.shape).reshape(r, w)


def _t128(x):
    r, w = x.shape
    if w > LANES:
        return jnp.concatenate([x[:, i:i + LANES].T for i in range(0, w, LANES)], axis=0)
    return jnp.concatenate([x[i:i + LANES, :].T for i in range(0, r, LANES)], axis=1)


def _gla_kernel(q_ref, k_ref, v_ref, g_ref, lr_ref, wg_ref, ba_ref, mat_ref, code_ref, ones_ref,
                s0f_ref, s0b_ref, ng_ref, o_ref, sf_ref, sb_ref, st_scr, of_scr, *, n, C, G):
    d = pl.program_id(1)
    c = pl.program_id(2)
    levels = _gla_levels(C)

    @pl.when(jnp.logical_and(c == 0, d == 0))
    def _():
        for bb in range(G):
            for h in range(C_H):
                st_scr[bb, h] = _t128(s0f_ref[bb, h])

    @pl.when(jnp.logical_and(c == 0, d == 1))
    def _():
        for bb in range(G):
            for h in range(C_H):
                st_scr[bb, h] = _t128(s0b_ref[bb, h])

    mat = mat_ref[...]
    code = code_ref[...]
    ones = ones_ref[...]
    cums = []
    for bb in range(G):
        xg = jnp.dot(lr_ref[bb], wg_ref[...], precision=HIGHEST, preferred_element_type=F32) + ba_ref[...]
        la = (jnp.minimum(xg, 0.0) - jnp.log1p(jnp.exp(-jnp.abs(xg)))) * (LOG2E / C_TAU)
        hi = la.astype(BF16)
        lo = (la - hi.astype(F32)).astype(BF16)
        cums.append(_dot(mat, hi) + _dot(mat, lo))

    def prepare(bb, h):
        cum = cums[bb]
        ks = slice(h * C_DK, (h + 1) * C_DK)
        qh = q_ref[bb, :, ks].astype(F32) * (C_DK ** -0.5)
        kh = k_ref[bb, :, ks].astype(F32)
        b = cum[0:C, ks]
        b_rest = cum[C:2 * C, ks]
        ps = []
        for jj in range(SUB):
            dec = jnp.exp2(jnp.minimum(b - _bcast_sublane(b, jj), 0.0))
            ps.append((qh * _bcast_sublane(kh, jj) * dec).astype(BF16))
        lv_ops = []
        for lv in range(len(levels)):
            eq = cum[(2 + 2 * lv) * C:(3 + 2 * lv) * C, ks]
            ek = cum[(3 + 2 * lv) * C:(4 + 2 * lv) * C, ks]
            lv_ops.append(((qh * jnp.exp2(eq)).astype(BF16), (kh * jnp.exp2(ek)).astype(BF16)))
        qe = (qh * jnp.exp2(b)).astype(BF16)
        ke = (kh * jnp.exp2(b_rest)).astype(BF16)
        e_end = jnp.exp2(b[0:1, :] + b_rest[0:1, :])
        return jnp.concatenate(ps, axis=1), lv_ops, qe, ke, e_end

    def contract(bb, h, prep):
        pcat, lv_ops, qe, ke, e_end = prep
        vh = v_ref[bb, :, h * C_DV:(h + 1) * C_DV].astype(F32)
        att = jnp.where(code == 1, _dot(pcat, ones), 0.0)
        for lv, (qs, ks_) in enumerate(lv_ops):
            att = jnp.where(code == 2 + lv, _dot_nt(qs, ks_), att)
        st = st_scr[bb, h]
        o = _dot(att.astype(BF16), vh.astype(BF16)) + _dot_nt(qe, st.astype(BF16))
        st_scr[bb, h] = e_end * st + _dot(_t128(vh).astype(BF16), ke)
        return o

    units = [(bb, h) for h in range(C_H) for bb in range(G)]
    outs = {}
    prep = prepare(*units[0])
    for idx, (bb, h) in enumerate(units):
        nxt = prepare(*units[idx + 1]) if idx + 1 < len(units) else None
        outs[(bb, h)] = contract(bb, h, prep)
        prep = nxt
    o_all = [jnp.concatenate([outs[(bb, h)] for h in range(C_H)], axis=-1) for bb in range(G)]

    @pl.when(d == 0)
    def _():
        for bb in range(G):
            of_scr[bb, c] = o_all[bb]

    @pl.when(d == 1)
    def _():
        for bb in range(G):
            tot = o_all[bb] + of_scr[bb, n - 1 - c]
            res = []
            for h in range(C_H):
                sl = slice(h * C_DV, (h + 1) * C_DV)
                t = tot[:, sl]
                y = t * lax.rsqrt(jnp.mean(t * t, axis=-1, keepdims=True) + EPS) * ng_ref[:, sl]
                res.append(y * _silu(g_ref[bb, :, sl].astype(F32)))
            o_ref[bb] = jnp.concatenate(res, axis=-1).astype(o_ref.dtype)

    @pl.when(jnp.logical_and(c == n - 1, d == 0))
    def _():
        for bb in range(G):
            for h in range(C_H):
                sf_ref[bb, h] = _t128(st_scr[bb, h])

    @pl.when(jnp.logical_and(c == n - 1, d == 1))
    def _():
        for bb in range(G):
            for h in range(C_H):
                sb_ref[bb, h] = _t128(st_scr[bb, h])


def _gla(z, zg, B, T, w_a2, b_a, s0f, s0b, norm_g):
    C = GLA_CHUNK
    G = _rows_per_step(B, T, C_H * C_DV)
    assert B % G == 0 and T % C == 0
    n = T // C
    HK = C_H * C_DK
    HV = C_H * C_DV
    mat, code, ones = _gla_tables(C)
    nr = mat.shape[1] // C
    wg = jnp.zeros((2, LANES, HK), F32)
    for dr in range(2):
        wg = wg.at[dr, dr * C_RANK:(dr + 1) * C_RANK, :].set(w_a2[dr])
    z3 = z.reshape(B, T, z.shape[1])
    zg3 = zg.reshape(B, T, zg.shape[1])

    def chunk(d, c):
        return c + d * (n - 1 - 2 * c)

    st_spec = pl.BlockSpec((G, C_H, C_DK, C_DV), lambda b, d, c: (b, 0, 0, 0))
    st_shape = jax.ShapeDtypeStruct((B, C_H, C_DK, C_DV), F32)
    o, sf, sb = pl.pallas_call(
        functools.partial(_gla_kernel, n=n, C=C, G=G),
        grid=(B // G, 2, n),
        in_specs=[pl.BlockSpec((G, C, HK), lambda b, d, c: (b, chunk(d, c), 0)),
                  pl.BlockSpec((G, C, HK), lambda b, d, c: (b, chunk(d, c), 1)),
                  pl.BlockSpec((G, C, HV), lambda b, d, c: (b, chunk(d, c), 1)),
                  pl.BlockSpec((G, C, HV), lambda b, d, c: (b, chunk(d, c), 2)),
                  pl.BlockSpec((G, C, LANES), lambda b, d, c: (b, chunk(d, c), 0)),
                  pl.BlockSpec((None, LANES, HK), lambda b, d, c: (d, 0, 0)),
                  pl.BlockSpec((None, 1, HK), lambda b, d, c: (d, 0, 0)),
                  pl.BlockSpec((None, nr * C, C), lambda b, d, c: (d, 0, 0)),
                  pl.BlockSpec((None, C, C), lambda b, d, c: (d, 0, 0)),
                  pl.BlockSpec((SUB * LANES, C), lambda b, d, c: (0, 0)),
                  st_spec, st_spec,
                  pl.BlockSpec((1, HV), lambda b, d, c: (0, 0))],
        out_specs=[pl.BlockSpec((G, C, HV), lambda b, d, c: (b, (n - 1) - d * c, 0)),
                   st_spec, st_spec],
        out_shape=[jax.ShapeDtypeStruct((B, T, HV), BF16), st_shape, st_shape],
        scratch_shapes=[pltpu.VMEM((G, C_H, C_DV, C_DK), F32), pltpu.VMEM((G, n, C, HV), F32)],
        compiler_params=_params("arbitrary", "arbitrary", "arbitrary"),
        name="gla",
    )(z3, z3, z3, z3, zg3, wg, b_a.reshape(2, 1, HK), mat, code, ones, s0f, s0b, norm_g.reshape(1, HV))
    return o.reshape(B * T, HV), sf, sb


def _run_stream(x, B, T, mods, ctx, p):
    N, D = x.shape
    rpg = N // mods[0].shape[0]
    TM = min(2048, rpg)
    nb = (B_H + 2 * B_HKV) * B_HD

    w_in = p['even_w_in'][0]
    z, zb = _norm_mm(x, p['norm1_g'][0], mods[0], (0, 1), w_in, w_in, (nb, MIX_MAIN // nb), TM, rpg)
    if ctx is None:
        s0 = jnp.zeros((B, A_H, A_DK, A_DV), F32)
        a_f0, a_b0, cache_k, cache_v = s0, s0, None, None
    else:
        cache_k, cache_v, a_f0, a_b0 = ctx[0], ctx[1], ctx[2], ctx[3]
    o_a, a_sf, a_sb = _retention(z, B, T, p['a_log_gamma'][0], a_f0, a_b0, p['a_norm_g'][0])
    qpad, k_norm, k_rot, v_bf = _bprep(zb, T, p['b_q_g'][0], p['b_k_g'][0], rope=ctx is not None)
    o_b = _attention(qpad, k_rot, v_bf, B, T, cache_k, cache_v)
    x = _proj_res(x, mods[0], 2, [o_a, o_b], p['even_w_out'][0], rpg)
    x = _ffn(x, p['norm2_g'][0], mods[0], p['ff_w1'][0], p['ff_w3'][0], p['ff_w2'][0], rpg)

    w_in = p['odd_w_in'][0]
    w_gate = jnp.pad(w_in[:, MIX_MAIN:], ((0, 0), (0, LANES - 2 * C_RANK)))
    z1, z1g = _norm_mm(x, p['norm1_g'][1], mods[1], (0, 1), w_in, w_gate, (LANES, 0), TM, rpg)
    if ctx is None:
        s0 = jnp.zeros((B, C_H, C_DK, C_DV), F32)
        c_f0, c_b0 = s0, s0
    else:
        c_f0, c_b0 = ctx[4], ctx[5]
    o_c, c_sf, c_sb = _gla(z1, z1g, B, T, p['c_w_a2'][0], p['c_b_a'][0], c_f0, c_b0, p['c_norm_g'][0])
    x = _proj_res(x, mods[1], 2, [o_c], p['odd_w_out'][0], rpg)
    v_raw = zb[:, (B_H + B_HKV) * B_HD:]
    return x, (k_norm, v_raw, a_sf, a_sb, c_sf, c_sb)


def kernel(x_prompt, x_sample, c, cache_b_k, cache_b_v, state_a_fwd, state_a_bwd, state_c_fwd, state_c_bwd,
           c_ctx, w_mod, b_mod, norm1_g, norm2_g, final_g, even_w_in, even_w_out, a_log_gamma, a_norm_g,
           b_q_g, b_k_g, odd_w_in, c_w_a2, c_b_a, c_norm_g, odd_w_out, ff_w1, ff_w3, ff_w2,
           router_w, moe_w1, moe_w3, moe_w2):
    Bp, Tp, D = x_prompt.shape
    Bs, Ts, _ = x_sample.shape
    L = w_mod.shape[0]
    assert L == 2 and even_w_in.shape[0] == 1 and odd_w_in.shape[0] == 1
    p = dict(norm1_g=norm1_g, norm2_g=norm2_g, final_g=final_g, even_w_in=even_w_in, even_w_out=even_w_out,
             a_log_gamma=a_log_gamma, a_norm_g=a_norm_g, b_q_g=b_q_g, b_k_g=b_k_g, odd_w_in=odd_w_in,
             c_w_a2=c_w_a2, c_b_a=c_b_a, c_norm_g=c_norm_g, odd_w_out=odd_w_out, ff_w1=ff_w1, ff_w3=ff_w3,
             ff_w2=ff_w2, router_w=router_w, moe_w1=moe_w1, moe_w3=moe_w3, moe_w2=moe_w2)

    rows = 8
    conds = jnp.concatenate([c_ctx[None, :], c, jnp.zeros((rows - 1 - Bs, D), F32)], axis=0)
    mod = _modulation(conds, w_mod, b_mod).reshape(L, rows, 6, 1, D)
    mods_p = [mod[l, 0:1] for l in range(L)]
    mods_s = [mod[l, 1:1 + Bs] for l in range(L)]

    x_p, kept = _run_stream(x_prompt.reshape(Bp * Tp, D), Bp, Tp, mods_p, None, p)
    nk = B_HKV * B_HD
    ctx = (cache_b_k[:, 0].reshape(Bs, -1, nk), cache_b_v[:, 0].reshape(Bs, -1, nk),
           state_a_fwd[:, 0], state_a_bwd[:, 0], state_c_fwd[:, 0], state_c_bwd[:, 0])
    x_s, _ = _run_stream(x_sample.reshape(Bs * Ts, D), Bs, Ts, mods_s, ctx, p)
    y_p, y_s = _moe(x_p, x_s, norm2_g[1], mod[1, 0:1 + Bs], router_w[0], moe_w1[0], moe_w3[0], moe_w2[0],
                    final_g, Ts)

    k_norm, v_raw, a_sf, a_sb, c_sf, c_sb = kept
    return (y_p.reshape(Bp, Tp, D), y_s.reshape(Bs, Ts, D),
            k_norm.reshape(Bp, 1, Tp, B_HKV, B_HD), v_raw.reshape(Bp, 1, Tp, B_HKV, B_HD),
            a_sf[:, None], a_sb[:, None], c_sf[:, None], c_sb[:, None])
```

```python
import functools

import numpy as np
import jax
import jax.numpy as jnp
from jax import lax
from jax.experimental import pallas as pl
from jax.experimental.pallas import tpu as pltpu

F32 = jnp.float32
BF16 = jnp.bfloat16
EPS = 1e-6
HIGHEST = lax.Precision.HIGHEST
LOG2E = 1.4426950408889634

VMEM_LIMIT_BYTES = 56 * 1024 * 1024

A_H, A_DK, A_DV = 4, 128, 256
B_H, B_HKV, B_HD = 8, 2, 64
C_H, C_DK, C_DV, C_RANK = 4, 128, 256, 16
C_TAU = 16.0
GRID_W = 64
ROPE_THETA = 10000.0
N_EXPERTS = 8
LANES = 128
SUB = 8
RET_CHUNK = 128
GLA_CHUNK = 128
Q_TILE = 128
SCAN_ROWS_MAX = 4
SCAN_FWD_BYTES = 32 * 1024 * 1024


def _rows_per_step(B, T, width):
    g = SCAN_ROWS_MAX
    while g > 1 and (B % g or g * T * width * 4 > SCAN_FWD_BYTES):
        g //= 2
    return g


def _params(*sem):
    return pltpu.CompilerParams(dimension_semantics=sem, vmem_limit_bytes=VMEM_LIMIT_BYTES)


def _dot(a, b):
    return jnp.dot(a, b, preferred_element_type=F32)


def _dot_nt(a, b):
    return lax.dot_general(a, b, (((1,), (1,)), ((), ())), preferred_element_type=F32)


def _silu(x):
    return x * jax.nn.sigmoid(x)


def _norm_mod(x, g, sh, sc):
    r = lax.rsqrt(jnp.mean(x * x, axis=-1, keepdims=True) + EPS)
    return (x * r * g) * (1.0 + sc) + sh


def _mod_kernel(c_ref, w_ref, b_ref, o_ref):
    c = c_ref[...]
    o_ref[...] = jnp.dot(_silu(c), w_ref[...], precision=HIGHEST, preferred_element_type=F32) + b_ref[...]


def _modulation(conds, w_mod, b_mod):
    L, D, D6 = w_mod.shape
    R = conds.shape[0]
    TN = 1024
    return pl.pallas_call(
        _mod_kernel,
        grid=(L, D6 // TN),
        in_specs=[pl.BlockSpec((R, D), lambda l, j: (0, 0)),
                  pl.BlockSpec((None, D, TN), lambda l, j: (l, 0, j)),
                  pl.BlockSpec((None, 1, TN), lambda l, j: (l, 0, j))],
        out_specs=pl.BlockSpec((None, R, TN), lambda l, j: (l, 0, j)),
        out_shape=jax.ShapeDtypeStruct((L, R, D6), F32),
        compiler_params=_params("arbitrary", "arbitrary"),
        name="modulation",
    )(conds, w_mod, b_mod.reshape(L, 1, D6))


def _mod_spec(part, D, TM, rows_per_group, axis):
    def idx(*g):
        return ((g[axis] * TM) // rows_per_group, part, 0, 0)
    return pl.BlockSpec((None, None, 1, D), idx)


MIX_MAIN = A_H * (2 * A_DK + 2 * A_DV)
MIX_TN = 768


def _norm_mm_kernel(x_ref, g_ref, sh_ref, sc_ref, w_ref, we_ref, o_ref, oe_ref, h_scr, *, nmain):
    j = pl.program_id(1)

    @pl.when(j == 0)
    def _():
        h_scr[...] = _norm_mod(x_ref[...], g_ref[...], sh_ref[...], sc_ref[...]).astype(BF16)

    @pl.when(j < nmain)
    def _():
        o_ref[...] = _dot(h_scr[...], w_ref[...].astype(BF16)).astype(o_ref.dtype)

    @pl.when(j == nmain)
    def _():
        oe_ref[...] = _dot(h_scr[...], we_ref[...].astype(BF16))


def _norm_mm(x, g, mod, parts, w, w_extra, extra_block, TM, rows_per_group):
    N, D = x.shape
    nmain = MIX_MAIN // MIX_TN
    WE = extra_block[0]
    return pl.pallas_call(
        functools.partial(_norm_mm_kernel, nmain=nmain),
        grid=(N // TM, nmain + 1),
        in_specs=[pl.BlockSpec((TM, D), lambda i, j: (i, 0)),
                  pl.BlockSpec((1, D), lambda i, j: (0, 0)),
                  _mod_spec(parts[0], D, TM, rows_per_group, 0),
                  _mod_spec(parts[1], D, TM, rows_per_group, 0),
                  pl.BlockSpec((D, MIX_TN), lambda i, j: (0, jnp.minimum(j, nmain - 1))),
                  pl.BlockSpec((D, WE), lambda i, j: (0, extra_block[1]))],
        out_specs=[pl.BlockSpec((TM, MIX_TN), lambda i, j: (i, jnp.minimum(j, nmain - 1))),
                   pl.BlockSpec((TM, WE), lambda i, j: (i, 0))],
        out_shape=[jax.ShapeDtypeStruct((N, MIX_MAIN), BF16), jax.ShapeDtypeStruct((N, WE), F32)],
        scratch_shapes=[pltpu.VMEM((TM, D), BF16)],
        compiler_params=_params("arbitrary", "arbitrary"),
        name="norm_mm",
    )(x, g.reshape(1, D), mod, mod, w, w_extra)


def _ret_kernel(lg_ref, q_ref, k_ref, v_ref, ag_ref, s0f_ref, s0b_ref, ng_ref,
                o_ref, sf_ref, sb_ref, s_scr, of_scr, *, n, C, G):
    d = pl.program_id(1)
    c = pl.program_id(2)

    @pl.when(jnp.logical_and(c == 0, d == 0))
    def _():
        s_scr[...] = s0f_ref[...]

    @pl.when(jnp.logical_and(c == 0, d == 1))
    def _():
        s_scr[...] = s0b_ref[...]

    df = d.astype(F32)
    sgn = 1.0 - 2.0 * df
    ii = lax.broadcasted_iota(jnp.int32, (C, C), 0).astype(F32)
    jj = lax.broadcasted_iota(jnp.int32, (C, C), 1).astype(F32)
    dd = (ii - jj) * sgn
    feeds = dd >= 0.0
    ddc = jnp.maximum(dd, 0.0)
    ri = lax.broadcasted_iota(jnp.int32, (C, 1), 0).astype(F32)
    pos_q = (ri + 1.0) + df * (C - 2.0 * ri - 1.0)
    pos_k = (C - 1.0 - ri) + df * (2.0 * ri - C + 1.0)
    chunk_len = jnp.full((1, A_DV), float(C), F32)

    outs = [[] for _ in range(G)]
    for h in range(A_H):
        lg = lg_ref[d, h]
        dmask = jnp.where(feeds, jnp.exp2(lg * ddc), 0.0)
        q_dec = jnp.exp2(lg * pos_q)
        k_dec = jnp.exp2(lg * pos_k)
        c_dec = jnp.exp2(lg * chunk_len)
        for bb in range(G):
            qh = q_ref[bb, :, h * A_DK:(h + 1) * A_DK].astype(F32) * (A_DK ** -0.5)
            kh = k_ref[bb, :, h * A_DK:(h + 1) * A_DK].astype(F32)
            vh = v_ref[bb, :, h * A_DV:(h + 1) * A_DV].astype(BF16)
            s = s_scr[bb, h]
            att = _dot_nt(qh.astype(BF16), kh.astype(BF16)) * dmask
            o = _dot(att.astype(BF16), vh) + _dot((qh * q_dec).astype(BF16), s.astype(BF16))
            kd = kh * k_dec
            s_scr[bb, h] = c_dec * s + _dot(kd.T.astype(BF16), vh)
            outs[bb].append(o)
    o_all = [jnp.concatenate(o, axis=-1) for o in outs]

    @pl.when(d == 0)
    def _():
        for bb in range(G):
            of_scr[bb, c] = o_all[bb]

    @pl.when(d == 1)
    def _():
        for bb in range(G):
            tot = o_all[bb] + of_scr[bb, n - 1 - c]
            res = []
            for h in range(A_H):
                sl = slice(h * A_DV, (h + 1) * A_DV)
                t = tot[:, sl]
                dev = t - jnp.mean(t, axis=-1, keepdims=True)
                y = dev * lax.rsqrt(jnp.mean(dev * dev, axis=-1, keepdims=True) + EPS) * ng_ref[:, sl]
                res.append(y * _silu(ag_ref[bb, :, sl].astype(F32)))
            o_ref[bb] = jnp.concatenate(res, axis=-1).astype(o_ref.dtype)

    @pl.when(jnp.logical_and(c == n - 1, d == 0))
    def _():
        sf_ref[...] = s_scr[...]

    @pl.when(jnp.logical_and(c == n - 1, d == 1))
    def _():
        sb_ref[...] = s_scr[...]


def _retention(z, B, T, log_gamma, s0f, s0b, norm_g):
    C = RET_CHUNK
    G = _rows_per_step(B, T, A_H * A_DV)
    assert B % G == 0 and T % C == 0
    n = T // C
    HK = A_H * A_DK
    HV = A_H * A_DV
    z3 = z.reshape(B, T, z.shape[1])

    def chunk(d, c):
        return c + d * (n - 1 - 2 * c)

    st_spec = pl.BlockSpec((G, A_H, A_DK, A_DV), lambda b, d, c: (b, 0, 0, 0))
    st_shape = jax.ShapeDtypeStruct((B, A_H, A_DK, A_DV), F32)
    o, sf, sb = pl.pallas_call(
        functools.partial(_ret_kernel, n=n, C=C, G=G),
        grid=(B // G, 2, n),
        in_specs=[pl.BlockSpec(memory_space=pltpu.SMEM),
                  pl.BlockSpec((G, C, HK), lambda b, d, c: (b, chunk(d, c), 0)),
                  pl.BlockSpec((G, C, HK), lambda b, d, c: (b, chunk(d, c), 1)),
                  pl.BlockSpec((G, C, HV), lambda b, d, c: (b, chunk(d, c), 1)),
                  pl.BlockSpec((G, C, HV), lambda b, d, c: (b, chunk(d, c), 2)),
                  st_spec, st_spec,
                  pl.BlockSpec((1, HV), lambda b, d, c: (0, 0))],
        out_specs=[pl.BlockSpec((G, C, HV), lambda b, d, c: (b, (n - 1) - d * c, 0)),
                   st_spec, st_spec],
        out_shape=[jax.ShapeDtypeStruct((B, T, HV), BF16), st_shape, st_shape],
        scratch_shapes=[pltpu.VMEM((G, A_H, A_DK, A_DV), F32), pltpu.VMEM((G, n, C, HV), F32)],
        compiler_params=_params("arbitrary", "arbitrary", "arbitrary"),
        name="retention",
    )(log_gamma * LOG2E, z3, z3, z3, z3, s0f, s0b, norm_g.reshape(1, HV))
    return o.reshape(B * T, HV), sf, sb


def _group_sum_matrix(width, group):
    i = np.arange(width)
    return jnp.asarray((i[:, None] // group == i[None, :] // group).astype(np.float32), dtype=BF16)


def _q_pad_matrix():
    m = np.zeros((B_H * B_HD, B_H * LANES), np.float32)
    g = B_H // B_HKV
    for h in range(B_H):
        for t in range(B_HD):
            m[h * B_HD + t, h * LANES + (h // g) * B_HD + t] = 1.0
    return jnp.asarray(m, dtype=BF16)


def _rope_tables(T):
    rows = T // GRID_W
    row = np.repeat(np.arange(rows, dtype=np.float64), GRID_W)
    col = np.tile(np.arange(GRID_W, dtype=np.float64), rows)
    nq = B_HD // 4
    inv = ROPE_THETA ** (-np.arange(nq, dtype=np.float64) / nq)
    ang = np.concatenate([row[:, None] * inv, col[:, None] * inv], axis=-1)
    cos = np.repeat(np.cos(ang), 2, axis=-1)
    sin = np.repeat(np.sin(ang), 2, axis=-1)
    sign = np.tile(np.array([-1.0, 1.0]), B_HD // 2)
    reps = LANES // B_HD
    return (jnp.asarray(np.tile(cos, (1, reps)), dtype=F32),
            jnp.asarray(np.tile(sin * sign, (1, reps)), dtype=F32))


def _group_rmsnorm(x, gsum, g):
    x2 = x * x
    hi = x2.astype(BF16)
    lo = (x2 - hi.astype(F32)).astype(BF16)
    ss = _dot(hi, gsum) + _dot(lo, gsum)
    return x * lax.rsqrt(ss * (1.0 / B_HD) + EPS) * g


def _rotate_pairs(x, cos, sin_signed):
    n = x.shape[1]
    lane = lax.broadcasted_iota(jnp.int32, x.shape, 1)
    partner = jnp.where(lane % 2 == 0, pltpu.roll(x, n - 1, 1), pltpu.roll(x, 1, 1))
    reps = n // LANES
    if reps > 1:
        cos = jnp.concatenate([cos] * reps, axis=1)
        sin_signed = jnp.concatenate([sin_signed] * reps, axis=1)
    return x * cos + partner * sin_signed


def _bprep_kernel(z_ref, qg_ref, kg_ref, cos_ref, sin_ref, gq_ref, gk_ref, pad_ref,
                  qpad_ref, kn_ref, kr_ref, vb_ref, *, rope):
    nq = B_H * B_HD
    nk = B_HKV * B_HD
    qn = _group_rmsnorm(z_ref[:, 0:nq], gq_ref[...], qg_ref[...])
    kn = _group_rmsnorm(z_ref[:, nq:nq + nk], gk_ref[...], kg_ref[...])
    kn_ref[...] = kn
    if rope:
        qn = _rotate_pairs(qn, cos_ref[...], sin_ref[...])
        kn = _rotate_pairs(kn, cos_ref[...], sin_ref[...])
    kr_ref[...] = kn.astype(BF16)
    vb_ref[...] = z_ref[:, nq + nk:nq + 2 * nk].astype(BF16)
    qs = (qn * (B_HD ** -0.5 * LOG2E)).astype(BF16)
    qpad_ref[...] = _dot(qs, pad_ref[...]).astype(BF16)


def _bprep(z, T, q_g, k_g, rope):
    N = z.shape[0]
    TM = min(512, T)
    nq = B_H * B_HD
    nk = B_HKV * B_HD
    width = nq + 2 * nk
    assert z.shape[1] == width
    cos, sin = _rope_tables(T if rope else TM)
    nt = T // TM if rope else 1
    const = lambda i: (0, 0)
    return pl.pallas_call(
        functools.partial(_bprep_kernel, rope=rope),
        grid=(N // TM,),
        in_specs=[pl.BlockSpec((TM, width), lambda i: (i, 0)),
                  pl.BlockSpec((1, nq), const),
                  pl.BlockSpec((1, nk), const),
                  pl.BlockSpec((TM, LANES), lambda i: (i % nt, 0)),
                  pl.BlockSpec((TM, LANES), lambda i: (i % nt, 0)),
                  pl.BlockSpec((nq, nq), const),
                  pl.BlockSpec((nk, nk), const),
                  pl.BlockSpec((nq, B_H * LANES), const)],
        out_specs=[pl.BlockSpec((TM, B_H * LANES), lambda i: (i, 0)),
                   pl.BlockSpec((TM, nk), lambda i: (i, 0)),
                   pl.BlockSpec((TM, nk), lambda i: (i, 0)),
                   pl.BlockSpec((TM, nk), lambda i: (i, 0))],
        out_shape=[jax.ShapeDtypeStruct((N, B_H * LANES), BF16),
                   jax.ShapeDtypeStruct((N, nk), F32),
                   jax.ShapeDtypeStruct((N, nk), BF16),
                   jax.ShapeDtypeStruct((N, nk), BF16)],
        compiler_params=_params("arbitrary"),
        name="attn_prep",
    )(z, jnp.tile(q_g, B_H).reshape(1, nq), jnp.tile(k_g, B_HKV).reshape(1, nk), cos, sin,
      _group_sum_matrix(nq, B_HD), _group_sum_matrix(nk, B_HD), _q_pad_matrix())


def _lane_fold(x, op):
    acc = x[:, 0:LANES]
    for j in range(1, x.shape[1] // LANES):
        acc = op(acc, x[:, j * LANES:(j + 1) * LANES])
    return acc


def _attn_kernel(*refs, has_cache, kc, nq):
    if has_cache:
        q_ref, k_ref, v_ref, ck_ref, cv_ref, o_ref, s_scr, m_scr, mprev_scr, l_scr, acc_scr = refs
        kcc = min(kc, ck_ref.shape[0])
        ncache = ck_ref.shape[0] // kcc
    else:
        q_ref, k_ref, v_ref, o_ref, s_scr, m_scr, mprev_scr, l_scr, acc_scr = refs
        kcc, ncache = kc, 0
    i = pl.program_id(1)
    tq = q_ref.shape[0]
    nlat = k_ref.shape[0] // kc

    def score(c, kblk):
        q = jnp.concatenate([q_ref[:, h * LANES:(h + 1) * LANES] for h in range(B_H)], axis=0)
        s = _dot_nt(q, kblk)
        s_scr[c, :, 0:kblk.shape[0]] = s
        m_scr[...] = jnp.maximum(m_scr[...], _lane_fold(s, jnp.maximum))

    def weight(c, vblk):
        s = s_scr[c, :, 0:vblk.shape[0]]
        mp = mprev_scr[...]
        ps = [jnp.exp2(s[:, j * LANES:(j + 1) * LANES] - mp) for j in range(vblk.shape[0] // LANES)]
        tot = ps[0]
        for pj in ps[1:]:
            tot = tot + pj
        l_scr[...] += tot
        acc_scr[...] += _dot(jnp.concatenate(ps, axis=1).astype(BF16), vblk)

    def run(do_weight, do_score):
        def unit(c, kblk, vblk):
            if do_weight:
                weight(c, vblk())
            if do_score:
                score(c, kblk())

        for c in range(ncache):
            unit(c, lambda: ck_ref[c * kcc:(c + 1) * kcc, :].astype(BF16),
                 lambda: cv_ref[c * kcc:(c + 1) * kcc, :].astype(BF16))

        def body(c, carry):
            rows = pl.ds(pl.multiple_of(c * kc, kc), kc)
            unit(ncache + c, lambda: k_ref[rows, :], lambda: v_ref[rows, :])
            return carry
        lax.fori_loop(0, nlat, body, 0)

    @pl.when(i < nq)
    def _():
        m_scr[...] = jnp.full(m_scr.shape, -jnp.inf, F32)

    @pl.when(i > 0)
    def _():
        l_scr[...] = jnp.zeros_like(l_scr)
        acc_scr[...] = jnp.zeros_like(acc_scr)

    @pl.when(i == 0)
    def _():
        run(False, True)

    @pl.when(jnp.logical_and(i > 0, i < nq))
    def _():
        run(True, True)

    @pl.when(i == nq)
    def _():
        run(True, False)

    @pl.when(i > 0)
    def _():
        r_all = acc_scr[...] / jnp.sum(l_scr[...], axis=-1, keepdims=True)
        g = B_H // B_HKV
        lane = lax.broadcasted_iota(jnp.int32, (tq, LANES), 1)
        outs = []
        for j in range(B_H // 2):
            pair = []
            for half in range(2):
                h = 2 * j + half
                r = r_all[h * tq:(h + 1) * tq, :]
                if h // g != half:
                    r = pltpu.roll(r, B_HD, 1)
                pair.append(r)
            outs.append(jnp.where(lane < B_HD, pair[0], pair[1]))
        o_ref[...] = jnp.concatenate(outs, axis=-1).astype(o_ref.dtype)

    @pl.when(i < nq)
    def _():
        mprev_scr[...] = jnp.broadcast_to(jnp.max(m_scr[...], axis=-1, keepdims=True), mprev_scr.shape)


def _attention(qpad, kr, vb, B, T, cache_k, cache_v):
    has_cache = cache_k is not None
    TQ = Q_TILE
    nq = T // TQ
    nk = B_HKV * B_HD
    in_specs = [pl.BlockSpec((TQ, B_H * LANES), lambda b, i: (b * nq + jnp.minimum(i, nq - 1), 0)),
                pl.BlockSpec((T, nk), lambda b, i: (b, 0)),
                pl.BlockSpec((T, nk), lambda b, i: (b, 0))]
    args = [qpad, kr, vb]
    kc = min(1024, T)
    nchunks = T // kc
    if has_cache:
        P = cache_k.shape[1]
        assert P % min(kc, P) == 0
        nchunks += P // min(kc, P)
        in_specs += [pl.BlockSpec((None, P, nk), lambda b, i: (b, 0, 0))] * 2
        args += [cache_k, cache_v]
    R = B_H * TQ
    return pl.pallas_call(
        functools.partial(_attn_kernel, has_cache=has_cache, kc=kc, nq=nq),
        grid=(B, nq + 1),
        in_specs=in_specs,
        out_specs=pl.BlockSpec((TQ, B_H * B_HD), lambda b, i: (b * nq + jnp.maximum(i - 1, 0), 0)),
        out_shape=jax.ShapeDtypeStruct((B * T, B_H * B_HD), BF16),
        scratch_shapes=[pltpu.VMEM((nchunks, R, kc), F32)] + [pltpu.VMEM((R, LANES), F32)] * 4,
        compiler_params=_params("arbitrary", "arbitrary"),
        name="attention",
    )(*args)


def _proj_res_kernel(*refs, n_in):
    x_ref, gate_ref = refs[0], refs[1]
    o_refs = refs[2:2 + n_in]
    w_refs = refs[2 + n_in:2 + 2 * n_in]
    out_ref = refs[2 + 2 * n_in]
    wbf_refs = refs[3 + 2 * n_in:]

    @pl.when(pl.program_id(0) == 0)
    def _():
        for w_ref, wbf_ref in zip(w_refs, wbf_refs):
            wbf_ref[...] = w_ref[...].astype(BF16)

    acc = _dot(o_refs[0][...], wbf_refs[0][...])
    for o_ref, wbf_ref in zip(o_refs[1:], wbf_refs[1:]):
        acc = acc + _dot(o_ref[...], wbf_ref[...])
    out_ref[...] = x_ref[...] + gate_ref[...] * acc


def _proj_res(x, mod, part, acts, w, rows_per_group):
    N, D = x.shape
    TM = min(1024, rows_per_group)
    n_in = len(acts)
    widths = [a.shape[1] for a in acts]
    offs = np.cumsum([0] + widths[:-1]).tolist()
    in_specs = [pl.BlockSpec((TM, D), lambda i: (i, 0)),
                _mod_spec(part, D, TM, rows_per_group, 0)]
    in_specs += [pl.BlockSpec((TM, wd), lambda i: (i, 0)) for wd in widths]
    in_specs += [pl.BlockSpec((wd, D), functools.partial(lambda i, blk: (blk, 0), blk=off // wd))
                 for wd, off in zip(widths, offs)]
    return pl.pallas_call(
        functools.partial(_proj_res_kernel, n_in=n_in),
        grid=(N // TM,),
        in_specs=in_specs,
        out_specs=pl.BlockSpec((TM, D), lambda i: (i, 0)),
        out_shape=jax.ShapeDtypeStruct((N, D), F32),
        scratch_shapes=[pltpu.VMEM((wd, D), BF16) for wd in widths],
        compiler_params=_params("arbitrary"),
        name="proj_residual",
    )(x, mod, *acts, *([w] * n_in))


def _ffn_kernel(x_ref, g_ref, sh_ref, sc_ref, gate_ref, w1_ref, w3_ref, w2_ref, out_ref, h_scr, acc_scr, *, nf):
    f = pl.program_id(1)

    @pl.when(f == 0)
    def _():
        h_scr[...] = _norm_mod(x_ref[...], g_ref[...], sh_ref[...], sc_ref[...]).astype(BF16)
        acc_scr[...] = jnp.zeros_like(acc_scr)

    h = h_scr[...]
    a = _dot(h, w1_ref[...].astype(BF16))
    b = _dot(h, w3_ref[...].astype(BF16))
    acc_scr[...] += _dot((_silu(a) * b).astype(BF16), w2_ref[...].astype(BF16))

    @pl.when(f == nf - 1)
    def _():
        out_ref[...] = x_ref[...] + gate_ref[...] * acc_scr[...]


def _ffn(x, g, mod, w1, w3, w2, rows_per_group):
    N, D = x.shape
    FF = w1.shape[1]
    TM, TF = min(1024, rows_per_group), 256
    nf = FF // TF
    return pl.pallas_call(
        functools.partial(_ffn_kernel, nf=nf),
        grid=(N // TM, nf),
        in_specs=[pl.BlockSpec((TM, D), lambda i, f: (i, 0)),
                  pl.BlockSpec((1, D), lambda i, f: (0, 0)),
                  _mod_spec(3, D, TM, rows_per_group, 0),
                  _mod_spec(4, D, TM, rows_per_group, 0),
                  _mod_spec(5, D, TM, rows_per_group, 0),
                  pl.BlockSpec((D, TF), lambda i, f: (0, f)),
                  pl.BlockSpec((D, TF), lambda i, f: (0, f)),
                  pl.BlockSpec((TF, D), lambda i, f: (f, 0))],
        out_specs=pl.BlockSpec((TM, D), lambda i, f: (i, 0)),
        out_shape=jax.ShapeDtypeStruct((N, D), F32),
        scratch_shapes=[pltpu.VMEM((TM, D), BF16), pltpu.VMEM((TM, D), F32)],
        compiler_params=_params("arbitrary", "arbitrary"),
        name="ffn",
    )(x, g.reshape(1, D), mod, mod, mod, w1, w3, w2)


MOE_SB = 1024
MOE_SBG = 512
MOE_GG = 4
MOE_TRG = 256
MOE_TR = 2048
MOE_CG = 4


def _two_stream_specs(shape, ntp, ax=0):
    def idx_p(*g):
        return (jnp.minimum(g[ax], ntp - 1), 0)

    def idx_s(*g):
        return (jnp.maximum(g[ax] - ntp, 0), 0)
    return pl.BlockSpec(shape, idx_p), pl.BlockSpec(shape, idx_s)


def _pool_mod_spec(part, D, TM, ntp, rows_per_group):
    def idx(i, *_):
        return (jnp.where(i < ntp, 0, 1 + ((i - ntp) * TM) // rows_per_group), part, 0, 0)
    return pl.BlockSpec((None, None, 1, D), idx)


def _route_kernel(xp_ref, xs_ref, g_ref, sh_ref, sc_ref, rw_ref, tri_ref, h_ref, info_ref, infot_ref, cum_ref,
                  carry_scr, *, ntp):
    i = pl.program_id(0)

    @pl.when(i == 0)
    def _():
        carry_scr[...] = jnp.zeros_like(carry_scr)

    x = jnp.where(i < ntp, xp_ref[...], xs_ref[...])
    h = _norm_mod(x, g_ref[...], sh_ref[...], sc_ref[...])
    h_ref[...] = h.astype(BF16)
    lane = lax.broadcasted_iota(jnp.int32, (x.shape[0], LANES), 1).astype(F32)
    rw = rw_ref[...]
    h_hi = h.astype(BF16)
    h_lo = (h - h_hi.astype(F32)).astype(BF16)
    rw_hi = rw.astype(BF16)
    rw_lo = (rw - rw_hi.astype(F32)).astype(BF16)
    logits = _dot(h_hi, rw_hi) + (_dot(h_hi, rw_lo) + _dot(h_lo, rw_hi))
    logits = jnp.where(lane < N_EXPERTS, logits, -jnp.inf)
    m1 = jnp.max(logits, axis=-1, keepdims=True)
    i1 = jnp.min(jnp.where(logits == m1, lane, float(LANES)), axis=-1, keepdims=True)
    rest = jnp.where(lane == i1, -jnp.inf, logits)
    m2 = jnp.max(rest, axis=-1, keepdims=True)
    i2 = jnp.min(jnp.where(rest == m2, lane, float(LANES)), axis=-1, keepdims=True)
    e2 = jnp.exp(m2 - m1)
    w1 = 1.0 / (1.0 + e2)
    w2 = e2 / (1.0 + e2)
    ind = jnp.where(jnp.logical_or(lane == i1, lane == i2), 1.0, 0.0)
    before = _dot(tri_ref[...], ind.astype(BF16)) + carry_scr[...]
    r1 = jnp.sum(jnp.where(lane == i1, before, 0.0), axis=-1, keepdims=True)
    r2 = jnp.sum(jnp.where(lane == i2, before, 0.0), axis=-1, keepdims=True)
    total = carry_scr[...] + jnp.sum(ind, axis=0, keepdims=True)
    carry_scr[...] = total
    for part in range(1, MOE_SB // MOE_SBG):
        cum_ref[part - 1] = before[part * MOE_SBG:part * MOE_SBG + 1, :]
    cum_ref[MOE_SB // MOE_SBG - 1] = total
    info = jnp.where(lane == 0.0, i1, jnp.where(lane == 1.0, i2, jnp.where(lane == 2.0, w1, jnp.where(
        lane == 3.0, w2, jnp.where(lane == 4.0, r1, jnp.where(lane == 5.0, r2, 0.0))))))
    info_ref[...] = info[:, 0:SUB]
    info_t = jnp.concatenate([info[r:r + LANES, :].T for r in range(0, info.shape[0], LANES)], axis=1)
    infot_ref[...] = info_t[0:SUB, :]


def _moe_route(xp, xs, g, mod, router_w, rows_per_group):
    Np, D = xp.shape
    N = Np + xs.shape[0]
    TM = MOE_SB
    ntp = Np // TM
    nt = N // TM
    rw = jnp.pad(router_w, ((0, 0), (0, LANES - router_w.shape[1])))
    tri = jnp.asarray(np.tril(np.ones((TM, TM), np.float32), -1), dtype=BF16)
    xp_spec, xs_spec = _two_stream_specs((TM, D), ntp)
    return pl.pallas_call(
        functools.partial(_route_kernel, ntp=ntp),
        grid=(nt,),
        in_specs=[xp_spec, xs_spec,
                  pl.BlockSpec((1, D), lambda i: (0, 0)),
                  _pool_mod_spec(3, D, TM, ntp, rows_per_group),
                  _pool_mod_spec(4, D, TM, ntp, rows_per_group),
                  pl.BlockSpec((D, LANES), lambda i: (0, 0)),
                  pl.BlockSpec((TM, TM), lambda i: (0, 0))],
        out_specs=[pl.BlockSpec((TM, D), lambda i: (i, 0)),
                   pl.BlockSpec((TM, SUB), lambda i: (i, 0)),
                   pl.BlockSpec((SUB, TM), lambda i: (0, i)),
                   pl.BlockSpec((MOE_SB // MOE_SBG, 1, LANES), lambda i: (i, 0, 0))],
        out_shape=[jax.ShapeDtypeStruct((N, D), BF16),
                   jax.ShapeDtypeStruct((N, SUB), F32),
                   jax.ShapeDtypeStruct((SUB, N), F32),
                   jax.ShapeDtypeStruct((nt * (MOE_SB // MOE_SBG), 1, LANES), F32)],
        scratch_shapes=[pltpu.VMEM((1, LANES), F32)],
        compiler_params=_params("arbitrary"),
        name="moe_route",
    )(xp, xs, g.reshape(1, D), mod, mod, rw, tri)


def _hold_unused(idx, used):
    steps, slots = idx.shape
    read = jnp.arange(slots, dtype=jnp.int32)[None, :] < used[:, None]
    step = jnp.arange(steps, dtype=jnp.int32)[:, None]
    last = lax.cummax(jnp.where(read, step, -1), axis=0)
    held = jnp.take_along_axis(idx, jnp.maximum(last, 0), axis=0)
    return jnp.where(last >= 0, held, 0)


def _moe_plan(info, info_t, cum, N):
    E, SB, TRG, TR = N_EXPERTS, MOE_SB, MOE_TRG, MOE_TR
    NB = N // SB
    rmax = 2 * N + E * TR
    RG, RT = rmax // TRG, rmax // TR
    PMAX = RG + E * NB
    i32 = jnp.int32
    parts = SB // MOE_SBG
    cum_g = cum[:, 0, :E].astype(i32).T
    cum_e = cum_g[:, parts - 1::parts]
    cnt = cum_e[:, -1]
    tiles = (cnt + TR - 1) // TR
    start = TR * (jnp.cumsum(tiles) - tiles)

    startf = start.astype(F32)

    def region_start(e):
        out = jnp.zeros_like(e)
        for k in range(E):
            out = jnp.where(e == float(k), startf[k], out)
        return out

    pos_cols = jnp.concatenate([region_start(info[:, 0:2]) + info[:, 4:6], info[:, 2:4],
                                jnp.zeros((N, 4), F32)], axis=1)
    pos_rows = jnp.concatenate([region_start(info_t[0:2]) + info_t[4:6], jnp.zeros((6, N), F32)],
                               axis=0)

    def region(row0):
        e = jnp.clip(jnp.sum(row0[:, None] >= start[None, :], axis=1) - 1, 0, E - 1)
        return e, row0 - start[e]

    eq, lo = region(jnp.arange(RG, dtype=i32) * TRG)
    hi = jnp.minimum(lo + TRG, cnt[eq])
    first = jnp.sum(cum_e[eq] <= lo[:, None], axis=1)
    last = jnp.sum(cum_e[eq] < hi[:, None], axis=1)
    nblk = jnp.where(hi > lo, last - first + 1, 0)
    pend = jnp.cumsum(nblk)
    npairs = pend[-1]
    p = jnp.arange(PMAX, dtype=i32)
    valid = p < npairs
    pc = jnp.minimum(p, npairs - 1)
    q_of = jnp.minimum(jnp.sum(pend[None, :] <= pc[:, None], axis=1), RG - 1).astype(i32)
    pstart = pend - nblk
    s_of = (first[q_of] + pc - pstart[q_of]).astype(i32)

    GG = MOE_GG
    first_g = jnp.sum(cum_g[eq] <= lo[:, None], axis=1)
    last_g = jnp.sum(cum_g[eq] < hi[:, None], axis=1)
    nblk_g = jnp.where(hi > lo, last_g - first_g + 1, 0)
    nst = (nblk_g + GG - 1) // GG
    gst_end = jnp.cumsum(nst)
    gtotal = gst_end[-1]
    smax_g = (RG + E * NB * parts + (GG - 1) * RG) // GG + 1
    jg = jnp.arange(smax_g, dtype=i32)
    g_ok = jg < gtotal
    jgc = jnp.minimum(jg, gtotal - 1)
    tq = jnp.minimum(jnp.sum(gst_end[None, :] <= jgc[:, None], axis=1), RG - 1).astype(i32)
    gg = jgc - (gst_end - nst)[tq]
    g_slots = jnp.where(g_ok, jnp.clip(nblk_g[tq] - GG * gg, 0, GG), 0)
    g_parts = _hold_unused((first_g[tq] + GG * gg)[:, None] + jnp.arange(GG, dtype=i32)[None, :], g_slots)
    g_first = jnp.logical_and(g_ok, gg == 0)
    gather_plan = (tq, g_parts.reshape(-1).astype(i32), g_slots.astype(i32), g_first.astype(i32))

    order = jnp.argsort(jnp.where(valid, s_of * RG + q_of, jnp.iinfo(jnp.int32).max))
    s2, q2 = s_of[order], q_of[order]
    CG = MOE_CG
    blocks = jnp.arange(NB, dtype=i32)
    per_blk = jnp.sum(jnp.logical_and(valid[None, :], s2[None, :] == blocks[:, None]), axis=1)
    pb_end = jnp.cumsum(per_blk)
    pb_start = pb_end - per_blk
    nsteps = (per_blk + CG - 1) // CG
    st_end = jnp.cumsum(nsteps)
    total = st_end[-1]
    SMAX = (PMAX + CG - 1) // CG + NB
    j = jnp.arange(SMAX, dtype=i32)
    step_ok = j < total
    jc = jnp.minimum(j, total - 1)
    blk = jnp.minimum(jnp.sum(st_end[None, :] <= jc[:, None], axis=1), NB - 1).astype(i32)
    grp = jc - (st_end - nsteps)[blk]
    slot_p = pb_start[blk][:, None] + CG * grp[:, None] + jnp.arange(CG, dtype=i32)[None, :]
    slot_ok = jnp.logical_and(slot_p < pb_end[blk][:, None], step_ok[:, None])
    c_slots = jnp.sum(slot_ok, axis=1).astype(i32)
    slot_q = _hold_unused(q2[jnp.minimum(slot_p, npairs - 1)], c_slots)
    c_first = jnp.logical_and(step_ok, grp == 0).astype(i32)
    c_last = jnp.logical_and(step_ok, grp == nsteps[blk] - 1).astype(i32)
    combine_plan = (blk, slot_q.reshape(-1).astype(i32), c_slots, c_first, c_last)

    te, tlo = region(jnp.arange(RT, dtype=i32) * TR)
    tvalid = jnp.clip(cnt[te] - tlo, 0, TR)
    last_t = jnp.sum(tiles) - 1
    t_idx = jnp.where(tvalid > 0, jnp.arange(RT, dtype=i32), last_t).astype(i32)
    ffn_plan = (t_idx, te[t_idx].astype(i32), tvalid.astype(i32))
    return pos_cols, pos_rows, gather_plan, combine_plan, ffn_plan, rmax


def _moe_gather_kernel(q_ref, s_ref, slots_ref, first_ref, *refs):
    pos_refs, h_refs, out_ref = refs[:MOE_GG], refs[MOE_GG:2 * MOE_GG], refs[2 * MOE_GG]
    p = pl.program_id(0)
    rows = out_ref.shape[0]

    @pl.when(first_ref[p] == 1)
    def _():
        out_ref[...] = jnp.zeros_like(out_ref)

    for ns in range(1, MOE_GG + 1):
        @pl.when(slots_ref[p] == ns)
        def _(ns=ns):
            row = (lax.broadcasted_iota(jnp.int32, (rows, 1), 0) + q_ref[p] * rows).astype(F32)
            sels = []
            for k in range(ns):
                hit = jnp.logical_or(pos_refs[k][0:1, :] == row, pos_refs[k][1:2, :] == row)
                sels.append(jnp.where(hit, 1.0, 0.0).astype(BF16))
            sel = sels[0] if ns == 1 else jnp.concatenate(sels, axis=1)
            hs = h_refs[0][...] if ns == 1 else jnp.concatenate([h_refs[k][...] for k in range(ns)], axis=0)
            out_ref[...] = out_ref[...] + _dot(sel, hs).astype(BF16)


def _moe_gather(h, pos_rows, plan, rmax):
    N, D = h.shape
    nsteps = plan[0].shape[0]

    def pos_spec(k):
        return pl.BlockSpec((SUB, MOE_SBG), lambda p, q, s, *_: (0, s[MOE_GG * p + k]))

    def tok_spec(k):
        return pl.BlockSpec((MOE_SBG, D), lambda p, q, s, *_: (s[MOE_GG * p + k], 0))

    return pl.pallas_call(
        _moe_gather_kernel,
        grid_spec=pltpu.PrefetchScalarGridSpec(
            num_scalar_prefetch=4, grid=(nsteps,),
            in_specs=[pos_spec(k) for k in range(MOE_GG)] + [tok_spec(k) for k in range(MOE_GG)],
            out_specs=pl.BlockSpec((MOE_TRG, D), lambda p, q, *_: (q[p], 0))),
        out_shape=jax.ShapeDtypeStruct((rmax, D), BF16),
        compiler_params=_params("arbitrary"),
        name="moe_gather",
    )(*plan, *([pos_rows] * MOE_GG), *([h] * MOE_GG))


def _moe_ffn_kernel(t_ref, e_ref, nv_ref, x_ref, w1_ref, w3_ref, w2_ref, out_ref, acc_scr, *, nf):
    t = pl.program_id(0)
    f = pl.program_id(1)
    nv = nv_ref[t]

    def block(start, size):
        rows = pl.ds(start, size)

        @pl.when(f == 0)
        def _():
            acc_scr[rows, :] = jnp.zeros((size, acc_scr.shape[1]), F32)

        x = x_ref[rows, :]
        a = _dot(x, w1_ref[...].astype(BF16))
        b = _dot(x, w3_ref[...].astype(BF16))
        acc_scr[rows, :] += _dot((_silu(a) * b).astype(BF16), w2_ref[...].astype(BF16))

        @pl.when(f == nf - 1)
        def _():
            out_ref[rows, :] = acc_scr[rows, :].astype(out_ref.dtype)

    nsub = MOE_TR // MOE_TRG
    used = (nv + MOE_TRG - 1) // MOE_TRG

    @pl.when(used == nsub)
    def _():
        block(0, MOE_TR)

    @pl.when(jnp.logical_and(used > 0, used < nsub))
    def _():
        start = jnp.int32(0)
        size = MOE_TR // 2
        while size >= MOE_TRG:
            has = (used & (size // MOE_TRG)) != 0

            @pl.when(has)
            def _(start=start, size=size):
                block(pl.multiple_of(start, MOE_TRG), size)

            start = start + jnp.where(has, size, 0)
            size //= 2


def _moe_ffn(xs, plan, w1, w3, w2):
    rmax, D = xs.shape
    FF = w1.shape[2]
    TF = 256
    nf = FF // TF
    RT = rmax // MOE_TR

    def fidx(t, f, nv):
        return jnp.where(nv[t] > 0, f, nf - 1)

    return pl.pallas_call(
        functools.partial(_moe_ffn_kernel, nf=nf),
        grid_spec=pltpu.PrefetchScalarGridSpec(
            num_scalar_prefetch=3, grid=(RT, nf),
            in_specs=[pl.BlockSpec((MOE_TR, D), lambda t, f, ti, e, nv: (ti[t], 0)),
                      pl.BlockSpec((None, D, TF), lambda t, f, ti, e, nv: (e[t], 0, fidx(t, f, nv))),
                      pl.BlockSpec((None, D, TF), lambda t, f, ti, e, nv: (e[t], 0, fidx(t, f, nv))),
                      pl.BlockSpec((None, TF, D), lambda t, f, ti, e, nv: (e[t], fidx(t, f, nv), 0))],
            out_specs=pl.BlockSpec((MOE_TR, D), lambda t, f, ti, e, nv: (ti[t], 0)),
            scratch_shapes=[pltpu.VMEM((MOE_TR, D), F32)]),
        out_shape=jax.ShapeDtypeStruct((rmax, D), BF16),
        compiler_params=_params("arbitrary", "arbitrary"),
        name="moe_ffn",
    )(*plan, xs, w1, w3, w2)


def _moe_combine_kernel(s_ref, q_ref, slots_ref, first_ref, last_ref, pos_ref, *refs, ntp):
    ys_refs = refs[:MOE_CG]
    xp_ref, xs_ref, gate_ref, fg_ref, op_ref, os_ref, acc_scr = refs[MOE_CG:]
    p = pl.program_id(0)
    rows = ys_refs[0].shape[0]

    @pl.when(first_ref[p] == 1)
    def _():
        acc_scr[...] = jnp.zeros_like(acc_scr)

    for ns in range(1, MOE_CG + 1):
        @pl.when(slots_ref[p] == ns)
        def _(ns=ns):
            sels = []
            for k in range(ns):
                col = (lax.broadcasted_iota(jnp.int32, (1, rows), 1) + q_ref[MOE_CG * p + k] * rows).astype(F32)
                sels.append((jnp.where(pos_ref[:, 0:1] == col, pos_ref[:, 2:3], 0.0)
                             + jnp.where(pos_ref[:, 1:2] == col, pos_ref[:, 3:4], 0.0)).astype(BF16))
            sel = sels[0] if ns == 1 else jnp.concatenate(sels, axis=1)
            ys = ys_refs[0][...] if ns == 1 else jnp.concatenate([ys_refs[k][...] for k in range(ns)], axis=0)
            acc_scr[...] += _dot(sel, ys)

    @pl.when(last_ref[p] == 1)
    def _():
        s = s_ref[p]
        x = jnp.where(s < ntp, xp_ref[...], xs_ref[...])
        y = x + gate_ref[...] * acc_scr[...]
        out = y * lax.rsqrt(jnp.mean(y * y, axis=-1, keepdims=True) + EPS) * fg_ref[...]

        @pl.when(s < ntp)
        def _():
            op_ref[...] = out

        @pl.when(s >= ntp)
        def _():
            os_ref[...] = out


def _moe_combine(ys, pos_cols, plan, xp, xs, mod, final_g, rows_per_group):
    Np, D = xp.shape
    Ns = xs.shape[0]
    SB = MOE_SB
    ntp = Np // SB
    nsteps = plan[0].shape[0]

    def tile_spec(k):
        return pl.BlockSpec((MOE_TRG, D), lambda p, s, q, *_: (q[MOE_CG * p + k], 0))

    def tok_p(p, s, *_):
        return (jnp.minimum(s[p], ntp - 1), 0)

    def tok_s(p, s, *_):
        return (jnp.maximum(s[p] - ntp, 0), 0)

    def gate_idx(p, s, *_):
        return (jnp.where(s[p] < ntp, 0, 1 + ((s[p] - ntp) * SB) // rows_per_group), 5, 0, 0)

    return pl.pallas_call(
        functools.partial(_moe_combine_kernel, ntp=ntp),
        grid_spec=pltpu.PrefetchScalarGridSpec(
            num_scalar_prefetch=5, grid=(nsteps,),
            in_specs=[pl.BlockSpec((SB, SUB), lambda p, s, q, *_: (s[p], 0))]
            + [tile_spec(k) for k in range(MOE_CG)]
            + [pl.BlockSpec((SB, D), tok_p),
                      pl.BlockSpec((SB, D), tok_s),
                      pl.BlockSpec((None, None, 1, D), gate_idx),
                      pl.BlockSpec((1, D), lambda p, *_: (0, 0))],
            out_specs=[pl.BlockSpec((SB, D), tok_p), pl.BlockSpec((SB, D), tok_s)],
            scratch_shapes=[pltpu.VMEM((SB, D), F32)]),
        out_shape=[jax.ShapeDtypeStruct((Np, D), F32), jax.ShapeDtypeStruct((Ns, D), F32)],
        compiler_params=_params("arbitrary"),
        name="moe_combine",
    )(*plan, pos_cols, *([ys] * MOE_CG), xp, xs, mod, final_g.reshape(1, D))


def _moe(xp, xs, g, mod, router_w, w1, w3, w2, final_g, rows_per_group):
    N = xp.shape[0] + xs.shape[0]
    h, info, info_t, cum = _moe_route(xp, xs, g, mod, router_w, rows_per_group)
    pos_cols, pos_rows, gather_plan, combine_plan, ffn_plan, rmax = _moe_plan(info, info_t, cum, N)
    x_sorted = _moe_gather(h, pos_rows, gather_plan, rmax)
    y_sorted = _moe_ffn(x_sorted, ffn_plan, w1, w3, w2)
    return _moe_combine(y_sorted, pos_cols, combine_plan, xp, xs, mod, final_g, rows_per_group)


def _gla_levels(C):
    lv, c = [], C // 2
    while c >= SUB:
        lv.append(c)
        c //= 2
    return lv


def _gla_tables(C):
    levels = _gla_levels(C)
    nr = 2 + 2 * len(levels)
    mat = np.zeros((2, nr * C, C), np.float32)
    code = np.zeros((2, C, C), np.int32)
    for d in range(2):
        p = np.arange(C) if d == 0 else C - 1 - np.arange(C)
        pi, pj = p[:, None], p[None, :]
        mat[d, 0:C] = pj <= pi
        mat[d, C:2 * C] = pj > pi
        code[d] = np.where((pj <= pi) & (pi // SUB == pj // SUB), 1, 0)
        for lv, c in enumerate(levels):
            blk = pi // c
            later = blk % 2 == 1
            mat[d, (2 + 2 * lv) * C:(3 + 2 * lv) * C] = later & (pj > blk * c - 1) & (pj <= pi)
            mat[d, (3 + 2 * lv) * C:(4 + 2 * lv) * C] = (~later) & (pj > pi) & (pj <= (blk + 1) * c - 1)
            pair = (pi // (2 * c) == pj // (2 * c)) & (pi // c != pj // c) & (pj <= pi)
            code[d] = np.where(pair, 2 + lv, code[d])
    ones = np.zeros((SUB * LANES, C), np.float32)
    for jj in range(SUB):
        ones[jj * LANES:(jj + 1) * LANES, jj::SUB] = 1.0
    return jnp.asarray(mat, dtype=BF16), jnp.asarray(code), jnp.asarray(ones, dtype=BF16)


def _bcast_sublane(x, jj):
    r, w = x.shape
    x3 = x.reshape(r // SUB, SUB, w)
    return jnp.broadcast_to(x3[:, jj:jj + 1, :], x3.shape).reshape(r, w)


def _t128(x):
    r, w = x.shape
    if w > LANES:
        return jnp.concatenate([x[:, i:i + LANES].T for i in range(0, w, LANES)], axis=0)
    return jnp.concatenate([x[i:i + LANES, :].T for i in range(0, r, LANES)], axis=1)


def _gla_kernel(q_ref, k_ref, v_ref, g_ref, lr_ref, wg_ref, ba_ref, mat_ref, code_ref, ones_ref,
                s0f_ref, s0b_ref, ng_ref, o_ref, sf_ref, sb_ref, st_scr, of_scr, *, n, C, G):
    d = pl.program_id(1)
    c = pl.program_id(2)
    levels = _gla_levels(C)

    @pl.when(jnp.logical_and(c == 0, d == 0))
    def _():
        for bb in range(G):
            for h in range(C_H):
                st_scr[bb, h] = _t128(s0f_ref[bb, h])

    @pl.when(jnp.logical_and(c == 0, d == 1))
    def _():
        for bb in range(G):
            for h in range(C_H):
                st_scr[bb, h] = _t128(s0b_ref[bb, h])

    mat = mat_ref[...]
    code = code_ref[...]
    ones = ones_ref[...]
    cums = []
    for bb in range(G):
        xg = jnp.dot(lr_ref[bb], wg_ref[...], precision=HIGHEST, preferred_element_type=F32) + ba_ref[...]
        la = (jnp.minimum(xg, 0.0) - jnp.log1p(jnp.exp(-jnp.abs(xg)))) * (LOG2E / C_TAU)
        hi = la.astype(BF16)
        lo = (la - hi.astype(F32)).astype(BF16)
        cums.append(_dot(mat, hi) + _dot(mat, lo))

    def prepare(bb, h):
        cum = cums[bb]
        ks = slice(h * C_DK, (h + 1) * C_DK)
        qh = q_ref[bb, :, ks].astype(F32) * (C_DK ** -0.5)
        kh = k_ref[bb, :, ks].astype(F32)
        b = cum[0:C, ks]
        b_rest = cum[C:2 * C, ks]
        ps = []
        for jj in range(SUB):
            dec = jnp.exp2(jnp.minimum(b - _bcast_sublane(b, jj), 0.0))
            ps.append((qh * _bcast_sublane(kh, jj) * dec).astype(BF16))
        lv_ops = []
        for lv in range(len(levels)):
            eq = cum[(2 + 2 * lv) * C:(3 + 2 * lv) * C, ks]
            ek = cum[(3 + 2 * lv) * C:(4 + 2 * lv) * C, ks]
            lv_ops.append(((qh * jnp.exp2(eq)).astype(BF16), (kh * jnp.exp2(ek)).astype(BF16)))
        qe = (qh * jnp.exp2(b)).astype(BF16)
        ke = (kh * jnp.exp2(b_rest)).astype(BF16)
        e_end = jnp.exp2(b[0:1, :] + b_rest[0:1, :])
        return jnp.concatenate(ps, axis=1), lv_ops, qe, ke, e_end

    def contract(bb, h, prep):
        pcat, lv_ops, qe, ke, e_end = prep
        vh = v_ref[bb, :, h * C_DV:(h + 1) * C_DV].astype(F32)
        att = jnp.where(code == 1, _dot(pcat, ones), 0.0)
        for lv, (qs, ks_) in enumerate(lv_ops):
            att = jnp.where(code == 2 + lv, _dot_nt(qs, ks_), att)
        st = st_scr[bb, h]
        o = _dot(att.astype(BF16), vh.astype(BF16)) + _dot_nt(qe, st.astype(BF16))
        st_scr[bb, h] = e_end * st + _dot(_t128(vh).astype(BF16), ke)
        return o

    units = [(bb, h) for h in range(C_H) for bb in range(G)]
    outs = {}
    prep = prepare(*units[0])
    for idx, (bb, h) in enumerate(units):
        nxt = prepare(*units[idx + 1]) if idx + 1 < len(units) else None
        outs[(bb, h)] = contract(bb, h, prep)
        prep = nxt
    o_all = [jnp.concatenate([outs[(bb, h)] for h in range(C_H)], axis=-1) for bb in range(G)]

    @pl.when(d == 0)
    def _():
        for bb in range(G):
            of_scr[bb, c] = o_all[bb]

    @pl.when(d == 1)
    def _():
        for bb in range(G):
            tot = o_all[bb] + of_scr[bb, n - 1 - c]
            res = []
            for h in range(C_H):
                sl = slice(h * C_DV, (h + 1) * C_DV)
                t = tot[:, sl]
                y = t * lax.rsqrt(jnp.mean(t * t, axis=-1, keepdims=True) + EPS) * ng_ref[:, sl]
                res.append(y * _silu(g_ref[bb, :, sl].astype(F32)))
            o_ref[bb] = jnp.concatenate(res, axis=-1).astype(o_ref.dtype)

    @pl.when(jnp.logical_and(c == n - 1, d == 0))
    def _():
        for bb in range(G):
            for h in range(C_H):
                sf_ref[bb, h] = _t128(st_scr[bb, h])

    @pl.when(jnp.logical_and(c == n - 1, d == 1))
    def _():
        for bb in range(G):
            for h in range(C_H):
                sb_ref[bb, h] = _t128(st_scr[bb, h])


def _gla(z, zg, B, T, w_a2, b_a, s0f, s0b, norm_g):
    C = GLA_CHUNK
    G = _rows_per_step(B, T, C_H * C_DV)
    assert B % G == 0 and T % C == 0
    n = T // C
    HK = C_H * C_DK
    HV = C_H * C_DV
    mat, code, ones = _gla_tables(C)
    nr = mat.shape[1] // C
    wg = jnp.zeros((2, LANES, HK), F32)
    for dr in range(2):
        wg = wg.at[dr, dr * C_RANK:(dr + 1) * C_RANK, :].set(w_a2[dr])
    z3 = z.reshape(B, T, z.shape[1])
    zg3 = zg.reshape(B, T, zg.shape[1])

    def chunk(d, c):
        return c + d * (n - 1 - 2 * c)

    st_spec = pl.BlockSpec((G, C_H, C_DK, C_DV), lambda b, d, c: (b, 0, 0, 0))
    st_shape = jax.ShapeDtypeStruct((B, C_H, C_DK, C_DV), F32)
    o, sf, sb = pl.pallas_call(
        functools.partial(_gla_kernel, n=n, C=C, G=G),
        grid=(B // G, 2, n),
        in_specs=[pl.BlockSpec((G, C, HK), lambda b, d, c: (b, chunk(d, c), 0)),
                  pl.BlockSpec((G, C, HK), lambda b, d, c: (b, chunk(d, c), 1)),
                  pl.BlockSpec((G, C, HV), lambda b, d, c: (b, chunk(d, c), 1)),
                  pl.BlockSpec((G, C, HV), lambda b, d, c: (b, chunk(d, c), 2)),
                  pl.BlockSpec((G, C, LANES), lambda b, d, c: (b, chunk(d, c), 0)),
                  pl.BlockSpec((None, LANES, HK), lambda b, d, c: (d, 0, 0)),
                  pl.BlockSpec((None, 1, HK), lambda b, d, c: (d, 0, 0)),
                  pl.BlockSpec((None, nr * C, C), lambda b, d, c: (d, 0, 0)),
                  pl.BlockSpec((None, C, C), lambda b, d, c: (d, 0, 0)),
                  pl.BlockSpec((SUB * LANES, C), lambda b, d, c: (0, 0)),
                  st_spec, st_spec,
                  pl.BlockSpec((1, HV), lambda b, d, c: (0, 0))],
        out_specs=[pl.BlockSpec((G, C, HV), lambda b, d, c: (b, (n - 1) - d * c, 0)),
                   st_spec, st_spec],
        out_shape=[jax.ShapeDtypeStruct((B, T, HV), BF16), st_shape, st_shape],
        scratch_shapes=[pltpu.VMEM((G, C_H, C_DV, C_DK), F32), pltpu.VMEM((G, n, C, HV), F32)],
        compiler_params=_params("arbitrary", "arbitrary", "arbitrary"),
        name="gla",
    )(z3, z3, z3, z3, zg3, wg, b_a.reshape(2, 1, HK), mat, code, ones, s0f, s0b, norm_g.reshape(1, HV))
    return o.reshape(B * T, HV), sf, sb


def _run_stream(x, B, T, mods, ctx, p):
    N, D = x.shape
    rpg = N // mods[0].shape[0]
    TM = min(2048, rpg)
    nb = (B_H + 2 * B_HKV) * B_HD

    w_in = p['even_w_in'][0]
    z, zb = _norm_mm(x, p['norm1_g'][0], mods[0], (0, 1), w_in, w_in, (nb, MIX_MAIN // nb), TM, rpg)
    if ctx is None:
        s0 = jnp.zeros((B, A_H, A_DK, A_DV), F32)
        a_f0, a_b0, cache_k, cache_v = s0, s0, None, None
    else:
        cache_k, cache_v, a_f0, a_b0 = ctx[0], ctx[1], ctx[2], ctx[3]
    o_a, a_sf, a_sb = _retention(z, B, T, p['a_log_gamma'][0], a_f0, a_b0, p['a_norm_g'][0])
    qpad, k_norm, k_rot, v_bf = _bprep(zb, T, p['b_q_g'][0], p['b_k_g'][0], rope=ctx is not None)
    o_b = _attention(qpad, k_rot, v_bf, B, T, cache_k, cache_v)
    x = _proj_res(x, mods[0], 2, [o_a, o_b], p['even_w_out'][0], rpg)
    x = _ffn(x, p['norm2_g'][0], mods[0], p['ff_w1'][0], p['ff_w3'][0], p['ff_w2'][0], rpg)

    w_in = p['odd_w_in'][0]
    w_gate = jnp.pad(w_in[:, MIX_MAIN:], ((0, 0), (0, LANES - 2 * C_RANK)))
    z1, z1g = _norm_mm(x, p['norm1_g'][1], mods[1], (0, 1), w_in, w_gate, (LANES, 0), TM, rpg)
    if ctx is None:
        s0 = jnp.zeros((B, C_H, C_DK, C_DV), F32)
        c_f0, c_b0 = s0, s0
    else:
        c_f0, c_b0 = ctx[4], ctx[5]
    o_c, c_sf, c_sb = _gla(z1, z1g, B, T, p['c_w_a2'][0], p['c_b_a'][0], c_f0, c_b0, p['c_norm_g'][0])
    x = _proj_res(x, mods[1], 2, [o_c], p['odd_w_out'][0], rpg)
    v_raw = zb[:, (B_H + B_HKV) * B_HD:]
    return x, (k_norm, v_raw, a_sf, a_sb, c_sf, c_sb)


def kernel(x_prompt, x_sample, c, cache_b_k, cache_b_v, state_a_fwd, state_a_bwd, state_c_fwd, state_c_bwd,
           c_ctx, w_mod, b_mod, norm1_g, norm2_g, final_g, even_w_in, even_w_out, a_log_gamma, a_norm_g,
           b_q_g, b_k_g, odd_w_in, c_w_a2, c_b_a, c_norm_g, odd_w_out, ff_w1, ff_w3, ff_w2,
           router_w, moe_w1, moe_w3, moe_w2):
    Bp, Tp, D = x_prompt.shape
    Bs, Ts, _ = x_sample.shape
    L = w_mod.shape[0]
    assert L == 2 and even_w_in.shape[0] == 1 and odd_w_in.shape[0] == 1
    p = dict(norm1_g=norm1_g, norm2_g=norm2_g, final_g=final_g, even_w_in=even_w_in, even_w_out=even_w_out,
             a_log_gamma=a_log_gamma, a_norm_g=a_norm_g, b_q_g=b_q_g, b_k_g=b_k_g, odd_w_in=odd_w_in,
             c_w_a2=c_w_a2, c_b_a=c_b_a, c_norm_g=c_norm_g, odd_w_out=odd_w_out, ff_w1=ff_w1, ff_w3=ff_w3,
             ff_w2=ff_w2, router_w=router_w, moe_w1=moe_w1, moe_w3=moe_w3, moe_w2=moe_w2)

    rows = 8
    conds = jnp.concatenate([c_ctx[None, :], c, jnp.zeros((rows - 1 - Bs, D), F32)], axis=0)
    mod = _modulation(conds, w_mod, b_mod).reshape(L, rows, 6, 1, D)
    mods_p = [mod[l, 0:1] for l in range(L)]
    mods_s = [mod[l, 1:1 + Bs] for l in range(L)]

    x_p, kept = _run_stream(x_prompt.reshape(Bp * Tp, D), Bp, Tp, mods_p, None, p)
    nk = B_HKV * B_HD
    ctx = (cache_b_k[:, 0].reshape(Bs, -1, nk), cache_b_v[:, 0].reshape(Bs, -1, nk),
           state_a_fwd[:, 0], state_a_bwd[:, 0], state_c_fwd[:, 0], state_c_bwd[:, 0])
    x_s, _ = _run_stream(x_sample.reshape(Bs * Ts, D), Bs, Ts, mods_s, ctx, p)
    y_p, y_s = _moe(x_p, x_s, norm2_g[1], mod[1, 0:1 + Bs], router_w[0], moe_w1[0], moe_w3[0], moe_w2[0],
                    final_g, Ts)

    k_norm, v_raw, a_sf, a_sb, c_sf, c_sb = kept
    return (y_p.reshape(Bp, Tp, D), y_s.reshape(Bs, Ts, D),
            k_norm.reshape(Bp, 1, Tp, B_HKV, B_HD), v_raw.reshape(Bp, 1, Tp, B_HKV, B_HD),
            a_sf[:, None], a_sb[:, None], c_sf[:, None], c_sb[:, None])
```

```python
import functools

import numpy as np
import jax
import jax.numpy as jnp
from jax import lax
from jax.experimental import pallas as pl
from jax.experimental.pallas import tpu as pltpu

F32 = jnp.float32
BF16 = jnp.bfloat16
EPS = 1e-6
HIGHEST = lax.Precision.HIGHEST
LOG2E = 1.4426950408889634

VMEM_LIMIT_BYTES = 56 * 1024 * 1024

A_H, A_DK, A_DV = 4, 128, 256
B_H, B_HKV, B_HD = 8, 2, 64
C_H, C_DK, C_DV, C_RANK = 4, 128, 256, 16
C_TAU = 16.0
GRID_W = 64
ROPE_THETA = 10000.0
N_EXPERTS = 8
LANES = 128
SUB = 8
RET_CHUNK = 128
GLA_CHUNK = 128
Q_TILE = 128
SCAN_ROWS_MAX = 4
SCAN_FWD_BYTES = 32 * 1024 * 1024


def _rows_per_step(B, T, width):
    g = SCAN_ROWS_MAX
    while g > 1 and (B % g or g * T * width * 4 > SCAN_FWD_BYTES):
        g //= 2
    return g


def _params(*sem):
    return pltpu.CompilerParams(dimension_semantics=sem, vmem_limit_bytes=VMEM_LIMIT_BYTES)


def _dot(a, b):
    return jnp.dot(a, b, preferred_element_type=F32)


def _dot_nt(a, b):
    return lax.dot_general(a, b, (((1,), (1,)), ((), ())), preferred_element_type=F32)


def _silu(x):
    return x * jax.nn.sigmoid(x)


def _norm_mod(x, g, sh, sc):
    r = lax.rsqrt(jnp.mean(x * x, axis=-1, keepdims=True) + EPS)
    return (x * r * g) * (1.0 + sc) + sh


def _mod_kernel(c_ref, w_ref, b_ref, o_ref):
    c = c_ref[...]
    o_ref[...] = jnp.dot(_silu(c), w_ref[...], precision=HIGHEST, preferred_element_type=F32) + b_ref[...]


def _modulation(conds, w_mod, b_mod):
    L, D, D6 = w_mod.shape
    R = conds.shape[0]
    TN = 1024
    return pl.pallas_call(
        _mod_kernel,
        grid=(L, D6 // TN),
        in_specs=[pl.BlockSpec((R, D), lambda l, j: (0, 0)),
                  pl.BlockSpec((None, D, TN), lambda l, j: (l, 0, j)),
                  pl.BlockSpec((None, 1, TN), lambda l, j: (l, 0, j))],
        out_specs=pl.BlockSpec((None, R, TN), lambda l, j: (l, 0, j)),
        out_shape=jax.ShapeDtypeStruct((L, R, D6), F32),
        compiler_params=_params("arbitrary", "arbitrary"),
        name="modulation",
    )(conds, w_mod, b_mod.reshape(L, 1, D6))


def _mod_spec(part, D, TM, rows_per_group, axis):
    def idx(*g):
        return ((g[axis] * TM) // rows_per_group, part, 0, 0)
    return pl.BlockSpec((None, None, 1, D), idx)


MIX_MAIN = A_H * (2 * A_DK + 2 * A_DV)
MIX_TN = 768


def _norm_mm_kernel(x_ref, g_ref, sh_ref, sc_ref, w_ref, we_ref, o_ref, oe_ref, h_scr, *, nmain):
    j = pl.program_id(1)

    @pl.when(j == 0)
    def _():
        h_scr[...] = _norm_mod(x_ref[...], g_ref[...], sh_ref[...], sc_ref[...]).astype(BF16)

    @pl.when(j < nmain)
    def _():
        o_ref[...] = _dot(h_scr[...], w_ref[...].astype(BF16)).astype(o_ref.dtype)

    @pl.when(j == nmain)
    def _():
        oe_ref[...] = _dot(h_scr[...], we_ref[...].astype(BF16))


def _norm_mm(x, g, mod, parts, w, w_extra, extra_block, TM, rows_per_group):
    N, D = x.shape
    nmain = MIX_MAIN // MIX_TN
    WE = extra_block[0]
    return pl.pallas_call(
        functools.partial(_norm_mm_kernel, nmain=nmain),
        grid=(N // TM, nmain + 1),
        in_specs=[pl.BlockSpec((TM, D), lambda i, j: (i, 0)),
                  pl.BlockSpec((1, D), lambda i, j: (0, 0)),
                  _mod_spec(parts[0], D, TM, rows_per_group, 0),
                  _mod_spec(parts[1], D, TM, rows_per_group, 0),
                  pl.BlockSpec((D, MIX_TN), lambda i, j: (0, jnp.minimum(j, nmain - 1))),
                  pl.BlockSpec((D, WE), lambda i, j: (0, extra_block[1]))],
        out_specs=[pl.BlockSpec((TM, MIX_TN), lambda i, j: (i, jnp.minimum(j, nmain - 1))),
                   pl.BlockSpec((TM, WE), lambda i, j: (i, 0))],
        out_shape=[jax.ShapeDtypeStruct((N, MIX_MAIN), BF16), jax.ShapeDtypeStruct((N, WE), F32)],
        scratch_shapes=[pltpu.VMEM((TM, D), BF16)],
        compiler_params=_params("arbitrary", "arbitrary"),
        name="norm_mm",
    )(x, g.reshape(1, D), mod, mod, w, w_extra)


def _ret_kernel(lg_ref, q_ref, k_ref, v_ref, ag_ref, s0f_ref, s0b_ref, ng_ref,
                o_ref, sf_ref, sb_ref, s_scr, of_scr, *, n, C, G):
    d = pl.program_id(1)
    c = pl.program_id(2)

    @pl.when(jnp.logical_and(c == 0, d == 0))
    def _():
        s_scr[...] = s0f_ref[...]

    @pl.when(jnp.logical_and(c == 0, d == 1))
    def _():
        s_scr[...] = s0b_ref[...]

    df = d.astype(F32)
    sgn = 1.0 - 2.0 * df
    ii = lax.broadcasted_iota(jnp.int32, (C, C), 0).astype(F32)
    jj = lax.broadcasted_iota(jnp.int32, (C, C), 1).astype(F32)
    dd = (ii - jj) * sgn
    feeds = dd >= 0.0
    ddc = jnp.maximum(dd, 0.0)
    ri = lax.broadcasted_iota(jnp.int32, (C, 1), 0).astype(F32)
    pos_q = (ri + 1.0) + df * (C - 2.0 * ri - 1.0)
    pos_k = (C - 1.0 - ri) + df * (2.0 * ri - C + 1.0)
    chunk_len = jnp.full((1, A_DV), float(C), F32)

    outs = [[] for _ in range(G)]
    for h in range(A_H):
        lg = lg_ref[d, h]
        dmask = jnp.where(feeds, jnp.exp2(lg * ddc), 0.0)
        q_dec = jnp.exp2(lg * pos_q)
        k_dec = jnp.exp2(lg * pos_k)
        c_dec = jnp.exp2(lg * chunk_len)
        for bb in range(G):
            qh = q_ref[bb, :, h * A_DK:(h + 1) * A_DK].astype(F32) * (A_DK ** -0.5)
            kh = k_ref[bb, :, h * A_DK:(h + 1) * A_DK].astype(F32)
            vh = v_ref[bb, :, h * A_DV:(h + 1) * A_DV].astype(BF16)
            s = s_scr[bb, h]
            att = _dot_nt(qh.astype(BF16), kh.astype(BF16)) * dmask
            o = _dot(att.astype(BF16), vh) + _dot((qh * q_dec).astype(BF16), s.astype(BF16))
            kd = kh * k_dec
            s_scr[bb, h] = c_dec * s + _dot(kd.T.astype(BF16), vh)
            outs[bb].append(o)
    o_all = [jnp.concatenate(o, axis=-1) for o in outs]

    @pl.when(d == 0)
    def _():
        for bb in range(G):
            of_scr[bb, c] = o_all[bb]

    @pl.when(d == 1)
    def _():
        for bb in range(G):
            tot = o_all[bb] + of_scr[bb, n - 1 - c]
            res = []
            for h in range(A_H):
                sl = slice(h * A_DV, (h + 1) * A_DV)
                t = tot[:, sl]
                dev = t - jnp.mean(t, axis=-1, keepdims=True)
                y = dev * lax.rsqrt(jnp.mean(dev * dev, axis=-1, keepdims=True) + EPS) * ng_ref[:, sl]
                res.append(y * _silu(ag_ref[bb, :, sl].astype(F32)))
            o_ref[bb] = jnp.concatenate(res, axis=-1).astype(o_ref.dtype)

    @pl.when(jnp.logical_and(c == n - 1, d == 0))
    def _():
        sf_ref[...] = s_scr[...]

    @pl.when(jnp.logical_and(c == n - 1, d == 1))
    def _():
        sb_ref[...] = s_scr[...]


def _retention(z, B, T, log_gamma, s0f, s0b, norm_g):
    C = RET_CHUNK
    G = _rows_per_step(B, T, A_H * A_DV)
    assert B % G == 0 and T % C == 0
    n = T // C
    HK = A_H * A_DK
    HV = A_H * A_DV
    z3 = z.reshape(B, T, z.shape[1])

    def chunk(d, c):
        return c + d * (n - 1 - 2 * c)

    st_spec = pl.BlockSpec((G, A_H, A_DK, A_DV), lambda b, d, c: (b, 0, 0, 0))
    st_shape = jax.ShapeDtypeStruct((B, A_H, A_DK, A_DV), F32)
    o, sf, sb = pl.pallas_call(
        functools.partial(_ret_kernel, n=n, C=C, G=G),
        grid=(B // G, 2, n),
        in_specs=[pl.BlockSpec(memory_space=pltpu.SMEM),
                  pl.BlockSpec((G, C, HK), lambda b, d, c: (b, chunk(d, c), 0)),
                  pl.BlockSpec((G, C, HK), lambda b, d, c: (b, chunk(d, c), 1)),
                  pl.BlockSpec((G, C, HV), lambda b, d, c: (b, chunk(d, c), 1)),
                  pl.BlockSpec((G, C, HV), lambda b, d, c: (b, chunk(d, c), 2)),
                  st_spec, st_spec,
                  pl.BlockSpec((1, HV), lambda b, d, c: (0, 0))],
        out_specs=[pl.BlockSpec((G, C, HV), lambda b, d, c: (b, (n - 1) - d * c, 0)),
                   st_spec, st_spec],
        out_shape=[jax.ShapeDtypeStruct((B, T, HV), BF16), st_shape, st_shape],
        scratch_shapes=[pltpu.VMEM((G, A_H, A_DK, A_DV), F32), pltpu.VMEM((G, n, C, HV), F32)],
        compiler_params=_params("arbitrary", "arbitrary", "arbitrary"),
        name="retention",
    )(log_gamma * LOG2E, z3, z3, z3, z3, s0f, s0b, norm_g.reshape(1, HV))
    return o.reshape(B * T, HV), sf, sb


def _group_sum_matrix(width, group):
    i = np.arange(width)
    return jnp.asarray((i[:, None] // group == i[None, :] // group).astype(np.float32), dtype=BF16)


def _q_pad_matrix():
    m = np.zeros((B_H * B_HD, B_H * LANES), np.float32)
    g = B_H // B_HKV
    for h in range(B_H):
        for t in range(B_HD):
            m[h * B_HD + t, h * LANES + (h // g) * B_HD + t] = 1.0
    return jnp.asarray(m, dtype=BF16)


def _rope_tables(T):
    rows = T // GRID_W
    row = np.repeat(np.arange(rows, dtype=np.float64), GRID_W)
    col = np.tile(np.arange(GRID_W, dtype=np.float64), rows)
    nq = B_HD // 4
    inv = ROPE_THETA ** (-np.arange(nq, dtype=np.float64) / nq)
    ang = np.concatenate([row[:, None] * inv, col[:, None] * inv], axis=-1)
    cos = np.repeat(np.cos(ang), 2, axis=-1)
    sin = np.repeat(np.sin(ang), 2, axis=-1)
    sign = np.tile(np.array([-1.0, 1.0]), B_HD // 2)
    reps = LANES // B_HD
    return (jnp.asarray(np.tile(cos, (1, reps)), dtype=F32),
            jnp.asarray(np.tile(sin * sign, (1, reps)), dtype=F32))


def _group_rmsnorm(x, gsum, g):
    x2 = x * x
    hi = x2.astype(BF16)
    lo = (x2 - hi.astype(F32)).astype(BF16)
    ss = _dot(hi, gsum) + _dot(lo, gsum)
    return x * lax.rsqrt(ss * (1.0 / B_HD) + EPS) * g


def _rotate_pairs(x, cos, sin_signed):
    n = x.shape[1]
    lane = lax.broadcasted_iota(jnp.int32, x.shape, 1)
    partner = jnp.where(lane % 2 == 0, pltpu.roll(x, n - 1, 1), pltpu.roll(x, 1, 1))
    reps = n // LANES
    if reps > 1:
        cos = jnp.concatenate([cos] * reps, axis=1)
        sin_signed = jnp.concatenate([sin_signed] * reps, axis=1)
    return x * cos + partner * sin_signed


def _bprep_kernel(z_ref, qg_ref, kg_ref, cos_ref, sin_ref, gq_ref, gk_ref, pad_ref,
                  qpad_ref, kn_ref, kr_ref, vb_ref, *, rope):
    nq = B_H * B_HD
    nk = B_HKV * B_HD
    qn = _group_rmsnorm(z_ref[:, 0:nq], gq_ref[...], qg_ref[...])
    kn = _group_rmsnorm(z_ref[:, nq:nq + nk], gk_ref[...], kg_ref[...])
    kn_ref[...] = kn
    if rope:
        qn = _rotate_pairs(qn, cos_ref[...], sin_ref[...])
        kn = _rotate_pairs(kn, cos_ref[...], sin_ref[...])
    kr_ref[...] = kn.astype(BF16)
    vb_ref[...] = z_ref[:, nq + nk:nq + 2 * nk].astype(BF16)
    qs = (qn * (B_HD ** -0.5 * LOG2E)).astype(BF16)
    qpad_ref[...] = _dot(qs, pad_ref[...]).astype(BF16)


def _bprep(z, T, q_g, k_g, rope):
    N = z.shape[0]
    TM = min(512, T)
    nq = B_H * B_HD
    nk = B_HKV * B_HD
    width = nq + 2 * nk
    assert z.shape[1] == width
    cos, sin = _rope_tables(T if rope else TM)
    nt = T // TM if rope else 1
    const = lambda i: (0, 0)
    return pl.pallas_call(
        functools.partial(_bprep_kernel, rope=rope),
        grid=(N // TM,),
        in_specs=[pl.BlockSpec((TM, width), lambda i: (i, 0)),
                  pl.BlockSpec((1, nq), const),
                  pl.BlockSpec((1, nk), const),
                  pl.BlockSpec((TM, LANES), lambda i: (i % nt, 0)),
                  pl.BlockSpec((TM, LANES), lambda i: (i % nt, 0)),
                  pl.BlockSpec((nq, nq), const),
                  pl.BlockSpec((nk, nk), const),
                  pl.BlockSpec((nq, B_H * LANES), const)],
        out_specs=[pl.BlockSpec((TM, B_H * LANES), lambda i: (i, 0)),
                   pl.BlockSpec((TM, nk), lambda i: (i, 0)),
                   pl.BlockSpec((TM, nk), lambda i: (i, 0)),
                   pl.BlockSpec((TM, nk), lambda i: (i, 0))],
        out_shape=[jax.ShapeDtypeStruct((N, B_H * LANES), BF16),
                   jax.ShapeDtypeStruct((N, nk), F32),
                   jax.ShapeDtypeStruct((N, nk), BF16),
                   jax.ShapeDtypeStruct((N, nk), BF16)],
        compiler_params=_params("arbitrary"),
        name="attn_prep",
    )(z, jnp.tile(q_g, B_H).reshape(1, nq), jnp.tile(k_g, B_HKV).reshape(1, nk), cos, sin,
      _group_sum_matrix(nq, B_HD), _group_sum_matrix(nk, B_HD), _q_pad_matrix())


def _lane_fold(x, op):
    acc = x[:, 0:LANES]
    for j in range(1, x.shape[1] // LANES):
        acc = op(acc, x[:, j * LANES:(j + 1) * LANES])
    return acc


def _attn_kernel(*refs, has_cache, kc, nq):
    if has_cache:
        q_ref, k_ref, v_ref, ck_ref, cv_ref, o_ref, s_scr, m_scr, mprev_scr, l_scr, acc_scr = refs
        kcc = min(kc, ck_ref.shape[0])
        ncache = ck_ref.shape[0] // kcc
    else:
        q_ref, k_ref, v_ref, o_ref, s_scr, m_scr, mprev_scr, l_scr, acc_scr = refs
        kcc, ncache = kc, 0
    i = pl.program_id(1)
    tq = q_ref.shape[0]
    nlat = k_ref.shape[0] // kc

    def score(c, kblk):
        q = jnp.concatenate([q_ref[:, h * LANES:(h + 1) * LANES] for h in range(B_H)], axis=0)
        s = _dot_nt(q, kblk)
        s_scr[c, :, 0:kblk.shape[0]] = s
        m_scr[...] = jnp.maximum(m_scr[...], _lane_fold(s, jnp.maximum))

    def weight(c, vblk):
        s = s_scr[c, :, 0:vblk.shape[0]]
        mp = mprev_scr[...]
        ps = [jnp.exp2(s[:, j * LANES:(j + 1) * LANES] - mp) for j in range(vblk.shape[0] // LANES)]
        tot = ps[0]
        for pj in ps[1:]:
            tot = tot + pj
        l_scr[...] += tot
        acc_scr[...] += _dot(jnp.concatenate(ps, axis=1).astype(BF16), vblk)

    def run(do_weight, do_score):
        def unit(c, kblk, vblk):
            if do_weight:
                weight(c, vblk())
            if do_score:
                score(c, kblk())

        for c in range(ncache):
            unit(c, lambda: ck_ref[c * kcc:(c + 1) * kcc, :].astype(BF16),
                 lambda: cv_ref[c * kcc:(c + 1) * kcc, :].astype(BF16))

        def body(c, carry):
            rows = pl.ds(pl.multiple_of(c * kc, kc), kc)
            unit(ncache + c, lambda: k_ref[rows, :], lambda: v_ref[rows, :])
            return carry
        lax.fori_loop(0, nlat, body, 0)

    @pl.when(i < nq)
    def _():
        m_scr[...] = jnp.full(m_scr.shape, -jnp.inf, F32)

    @pl.when(i > 0)
    def _():
        l_scr[...] = jnp.zeros_like(l_scr)
        acc_scr[...] = jnp.zeros_like(acc_scr)

    @pl.when(i == 0)
    def _():
        run(False, True)

    @pl.when(jnp.logical_and(i > 0, i < nq))
    def _():
        run(True, True)

    @pl.when(i == nq)
    def _():
        run(True, False)

    @pl.when(i > 0)
    def _():
        r_all = acc_scr[...] / jnp.sum(l_scr[...], axis=-1, keepdims=True)
        g = B_H // B_HKV
        lane = lax.broadcasted_iota(jnp.int32, (tq, LANES), 1)
        outs = []
        for j in range(B_H // 2):
            pair = []
            for half in range(2):
                h = 2 * j + half
                r = r_all[h * tq:(h + 1) * tq, :]
                if h // g != half:
                    r = pltpu.roll(r, B_HD, 1)
                pair.append(r)
            outs.append(jnp.where(lane < B_HD, pair[0], pair[1]))
        o_ref[...] = jnp.concatenate(outs, axis=-1).astype(o_ref.dtype)

    @pl.when(i < nq)
    def _():
        mprev_scr[...] = jnp.broadcast_to(jnp.max(m_scr[...], axis=-1, keepdims=True), mprev_scr.shape)


def _attention(qpad, kr, vb, B, T, cache_k, cache_v):
    has_cache = cache_k is not None
    TQ = Q_TILE
    nq = T // TQ
    nk = B_HKV * B_HD
    in_specs = [pl.BlockSpec((TQ, B_H * LANES), lambda b, i: (b * nq + jnp.minimum(i, nq - 1), 0)),
                pl.BlockSpec((T, nk), lambda b, i: (b, 0)),
                pl.BlockSpec((T, nk), lambda b, i: (b, 0))]
    args = [qpad, kr, vb]
    kc = min(1024, T)
    nchunks = T // kc
    if has_cache:
        P = cache_k.shape[1]
        assert P % min(kc, P) == 0
        nchunks += P // min(kc, P)
        in_specs += [pl.BlockSpec((None, P, nk), lambda b, i: (b, 0, 0))] * 2
        args += [cache_k, cache_v]
    R = B_H * TQ
    return pl.pallas_call(
        functools.partial(_attn_kernel, has_cache=has_cache, kc=kc, nq=nq),
        grid=(B, nq + 1),
        in_specs=in_specs,
        out_specs=pl.BlockSpec((TQ, B_H * B_HD), lambda b, i: (b * nq + jnp.maximum(i - 1, 0), 0)),
        out_shape=jax.ShapeDtypeStruct((B * T, B_H * B_HD), BF16),
        scratch_shapes=[pltpu.VMEM((nchunks, R, kc), F32)] + [pltpu.VMEM((R, LANES), F32)] * 4,
        compiler_params=_params("arbitrary", "arbitrary"),
        name="attention",
    )(*args)


def _proj_res_kernel(*refs, n_in):
    x_ref, gate_ref = refs[0], refs[1]
    o_refs = refs[2:2 + n_in]
    w_refs = refs[2 + n_in:2 + 2 * n_in]
    out_ref = refs[2 + 2 * n_in]
    wbf_refs = refs[3 + 2 * n_in:]

    @pl.when(pl.program_id(0) == 0)
    def _():
        for w_ref, wbf_ref in zip(w_refs, wbf_refs):
            wbf_ref[...] = w_ref[...].astype(BF16)

    acc = _dot(o_refs[0][...], wbf_refs[0][...])
    for o_ref, wbf_ref in zip(o_refs[1:], wbf_refs[1:]):
        acc = acc + _dot(o_ref[...], wbf_ref[...])
    out_ref[...] = x_ref[...] + gate_ref[...] * acc


def _proj_res(x, mod, part, acts, w, rows_per_group):
    N, D = x.shape
    TM = min(1024, rows_per_group)
    n_in = len(acts)
    widths = [a.shape[1] for a in acts]
    offs = np.cumsum([0] + widths[:-1]).tolist()
    in_specs = [pl.BlockSpec((TM, D), lambda i: (i, 0)),
                _mod_spec(part, D, TM, rows_per_group, 0)]
    in_specs += [pl.BlockSpec((TM, wd), lambda i: (i, 0)) for wd in widths]
    in_specs += [pl.BlockSpec((wd, D), functools.partial(lambda i, blk: (blk, 0), blk=off // wd))
                 for wd, off in zip(widths, offs)]
    return pl.pallas_call(
        functools.partial(_proj_res_kernel, n_in=n_in),
        grid=(N // TM,),
        in_specs=in_specs,
        out_specs=pl.BlockSpec((TM, D), lambda i: (i, 0)),
        out_shape=jax.ShapeDtypeStruct((N, D), F32),
        scratch_shapes=[pltpu.VMEM((wd, D), BF16) for wd in widths],
        compiler_params=_params("arbitrary"),
        name="proj_residual",
    )(x, mod, *acts, *([w] * n_in))


def _ffn_kernel(x_ref, g_ref, sh_ref, sc_ref, gate_ref, w1_ref, w3_ref, w2_ref, out_ref, h_scr, acc_scr, *, nf):
    f = pl.program_id(1)

    @pl.when(f == 0)
    def _():
        h_scr[...] = _norm_mod(x_ref[...], g_ref[...], sh_ref[...], sc_ref[...]).astype(BF16)
        acc_scr[...] = jnp.zeros_like(acc_scr)

    h = h_scr[...]
    a = _dot(h, w1_ref[...].astype(BF16))
    b = _dot(h, w3_ref[...].astype(BF16))
    acc_scr[...] += _dot((_silu(a) * b).astype(BF16), w2_ref[...].astype(BF16))

    @pl.when(f == nf - 1)
    def _():
        out_ref[...] = x_ref[...] + gate_ref[...] * acc_scr[...]


def _ffn(x, g, mod, w1, w3, w2, rows_per_group):
    N, D = x.shape
    FF = w1.shape[1]
    TM, TF = min(1024, rows_per_group), 256
    nf = FF // TF
    return pl.pallas_call(
        functools.partial(_ffn_kernel, nf=nf),
        grid=(N // TM, nf),
        in_specs=[pl.BlockSpec((TM, D), lambda i, f: (i, 0)),
                  pl.BlockSpec((1, D), lambda i, f: (0, 0)),
                  _mod_spec(3, D, TM, rows_per_group, 0),
                  _mod_spec(4, D, TM, rows_per_group, 0),
                  _mod_spec(5, D, TM, rows_per_group, 0),
                  pl.BlockSpec((D, TF), lambda i, f: (0, f)),
                  pl.BlockSpec((D, TF), lambda i, f: (0, f)),
                  pl.BlockSpec((TF, D), lambda i, f: (f, 0))],
        out_specs=pl.BlockSpec((TM, D), lambda i, f: (i, 0)),
        out_shape=jax.ShapeDtypeStruct((N, D), F32),
        scratch_shapes=[pltpu.VMEM((TM, D), BF16), pltpu.VMEM((TM, D), F32)],
        compiler_params=_params("arbitrary", "arbitrary"),
        name="ffn",
    )(x, g.reshape(1, D), mod, mod, mod, w1, w3, w2)


MOE_SB = 1024
MOE_SBG = 512
MOE_GG = 4
MOE_TRG = 256
MOE_TR = 2048
MOE_CG = 4


def _two_stream_specs(shape, ntp, ax=0):
    def idx_p(*g):
        return (jnp.minimum(g[ax], ntp - 1), 0)

    def idx_s(*g):
        return (jnp.maximum(g[ax] - ntp, 0), 0)
    return pl.BlockSpec(shape, idx_p), pl.BlockSpec(shape, idx_s)


def _pool_mod_spec(part, D, TM, ntp, rows_per_group):
    def idx(i, *_):
        return (jnp.where(i < ntp, 0, 1 + ((i - ntp) * TM) // rows_per_group), part, 0, 0)
    return pl.BlockSpec((None, None, 1, D), idx)


def _route_kernel(xp_ref, xs_ref, g_ref, sh_ref, sc_ref, rw_ref, tri_ref, h_ref, info_ref, infot_ref, cum_ref,
                  carry_scr, *, ntp):
    i = pl.program_id(0)

    @pl.when(i == 0)
    def _():
        carry_scr[...] = jnp.zeros_like(carry_scr)

    x = jnp.where(i < ntp, xp_ref[...], xs_ref[...])
    h = _norm_mod(x, g_ref[...], sh_ref[...], sc_ref[...])
    h_ref[...] = h.astype(BF16)
    lane = lax.broadcasted_iota(jnp.int32, (x.shape[0], LANES), 1).astype(F32)
    rw = rw_ref[...]
    h_hi = h.astype(BF16)
    h_lo = (h - h_hi.astype(F32)).astype(BF16)
    rw_hi = rw.astype(BF16)
    rw_lo = (rw - rw_hi.astype(F32)).astype(BF16)
    logits = _dot(h_hi, rw_hi) + (_dot(h_hi, rw_lo) + _dot(h_lo, rw_hi))
    logits = jnp.where(lane < N_EXPERTS, logits, -jnp.inf)
    m1 = jnp.max(logits, axis=-1, keepdims=True)
    i1 = jnp.min(jnp.where(logits == m1, lane, float(LANES)), axis=-1, keepdims=True)
    rest = jnp.where(lane == i1, -jnp.inf, logits)
    m2 = jnp.max(rest, axis=-1, keepdims=True)
    i2 = jnp.min(jnp.where(rest == m2, lane, float(LANES)), axis=-1, keepdims=True)
    e2 = jnp.exp(m2 - m1)
    w1 = 1.0 / (1.0 + e2)
    w2 = e2 / (1.0 + e2)
    ind = jnp.where(jnp.logical_or(lane == i1, lane == i2), 1.0, 0.0)
    before = _dot(tri_ref[...], ind.astype(BF16)) + carry_scr[...]
    r1 = jnp.sum(jnp.where(lane == i1, before, 0.0), axis=-1, keepdims=True)
    r2 = jnp.sum(jnp.where(lane == i2, before, 0.0), axis=-1, keepdims=True)
    total = carry_scr[...] + jnp.sum(ind, axis=0, keepdims=True)
    carry_scr[...] = total
    for part in range(1, MOE_SB // MOE_SBG):
        cum_ref[part - 1] = before[part * MOE_SBG:part * MOE_SBG + 1, :]
    cum_ref[MOE_SB // MOE_SBG - 1] = total
    info = jnp.where(lane == 0.0, i1, jnp.where(lane == 1.0, i2, jnp.where(lane == 2.0, w1, jnp.where(
        lane == 3.0, w2, jnp.where(lane == 4.0, r1, jnp.where(lane == 5.0, r2, 0.0))))))
    info_ref[...] = info[:, 0:SUB]
    info_t = jnp.concatenate([info[r:r + LANES, :].T for r in range(0, info.shape[0], LANES)], axis=1)
    infot_ref[...] = info_t[0:SUB, :]


def _moe_route(xp, xs, g, mod, router_w, rows_per_group):
    Np, D = xp.shape
    N = Np + xs.shape[0]
    TM = MOE_SB
    ntp = Np // TM
    nt = N // TM
    rw = jnp.pad(router_w, ((0, 0), (0, LANES - router_w.shape[1])))
    tri = jnp.asarray(np.tril(np.ones((TM, TM), np.float32), -1), dtype=BF16)
    xp_spec, xs_spec = _two_stream_specs((TM, D), ntp)
    return pl.pallas_call(
        functools.partial(_route_kernel, ntp=ntp),
        grid=(nt,),
        in_specs=[xp_spec, xs_spec,
                  pl.BlockSpec((1, D), lambda i: (0, 0)),
                  _pool_mod_spec(3, D, TM, ntp, rows_per_group),
                  _pool_mod_spec(4, D, TM, ntp, rows_per_group),
                  pl.BlockSpec((D, LANES), lambda i: (0, 0)),
                  pl.BlockSpec((TM, TM), lambda i: (0, 0))],
        out_specs=[pl.BlockSpec((TM, D), lambda i: (i, 0)),
                   pl.BlockSpec((TM, SUB), lambda i: (i, 0)),
                   pl.BlockSpec((SUB, TM), lambda i: (0, i)),
                   pl.BlockSpec((MOE_SB // MOE_SBG, 1, LANES), lambda i: (i, 0, 0))],
        out_shape=[jax.ShapeDtypeStruct((N, D), BF16),
                   jax.ShapeDtypeStruct((N, SUB), F32),
                   jax.ShapeDtypeStruct((SUB, N), F32),
                   jax.ShapeDtypeStruct((nt * (MOE_SB // MOE_SBG), 1, LANES), F32)],
        scratch_shapes=[pltpu.VMEM((1, LANES), F32)],
        compiler_params=_params("arbitrary"),
        name="moe_route",
    )(xp, xs, g.reshape(1, D), mod, mod, rw, tri)


def _hold_unused(idx, read):
    steps, slots = idx.shape
    step = jnp.arange(steps, dtype=jnp.int32)[:, None]
    last = lax.cummax(jnp.where(read, step, -1), axis=0)
    held = jnp.take_along_axis(idx, jnp.maximum(last, 0), axis=0)
    return jnp.where(last >= 0, held, 0)


def _moe_plan(info, info_t, cum, N):
    E, SB, TRG, TR = N_EXPERTS, MOE_SB, MOE_TRG, MOE_TR
    NB = N // SB
    rmax = 2 * N + E * TR
    RG, RT = rmax // TRG, rmax // TR
    PMAX = RG + E * NB
    i32 = jnp.int32
    parts = SB // MOE_SBG
    cum_g = cum[:, 0, :E].astype(i32).T
    cum_e = cum_g[:, parts - 1::parts]
    cnt = cum_e[:, -1]
    tiles = (cnt + TR - 1) // TR
    start = TR * (jnp.cumsum(tiles) - tiles)

    startf = start.astype(F32)

    def region_start(e):
        out = jnp.zeros_like(e)
        for k in range(E):
            out = jnp.where(e == float(k), startf[k], out)
        return out

    pos_cols = jnp.concatenate([region_start(info[:, 0:2]) + info[:, 4:6], info[:, 2:4],
                                jnp.zeros((N, 4), F32)], axis=1)
    pos_rows = jnp.concatenate([region_start(info_t[0:2]) + info_t[4:6], jnp.zeros((6, N), F32)],
                               axis=0)

    def region(row0):
        e = jnp.clip(jnp.sum(row0[:, None] >= start[None, :], axis=1) - 1, 0, E - 1)
        return e, row0 - start[e]

    eq, lo = region(jnp.arange(RG, dtype=i32) * TRG)
    hi = jnp.minimum(lo + TRG, cnt[eq])
    first = jnp.sum(cum_e[eq] <= lo[:, None], axis=1)
    last = jnp.sum(cum_e[eq] < hi[:, None], axis=1)
    nblk = jnp.where(hi > lo, last - first + 1, 0)
    pend = jnp.cumsum(nblk)
    npairs = pend[-1]
    p = jnp.arange(PMAX, dtype=i32)
    valid = p < npairs
    pc = jnp.minimum(p, npairs - 1)
    q_of = jnp.minimum(jnp.sum(pend[None, :] <= pc[:, None], axis=1), RG - 1).astype(i32)
    pstart = pend - nblk
    s_of = (first[q_of] + pc - pstart[q_of]).astype(i32)

    GG = MOE_GG
    first_g = jnp.sum(cum_g[eq] <= lo[:, None], axis=1)
    last_g = jnp.sum(cum_g[eq] < hi[:, None], axis=1)
    nblk_g = jnp.where(hi > lo, last_g - first_g + 1, 0)
    nst = (nblk_g + GG - 1) // GG
    gst_end = jnp.cumsum(nst)
    gtotal = gst_end[-1]
    smax_g = (RG + E * NB * parts + (GG - 1) * RG) // GG + 1
    jg = jnp.arange(smax_g, dtype=i32)
    g_ok = jg < gtotal
    jgc = jnp.minimum(jg, gtotal - 1)
    tq = jnp.minimum(jnp.sum(gst_end[None, :] <= jgc[:, None], axis=1), RG - 1).astype(i32)
    gg = jgc - (gst_end - nst)[tq]
    g_slots = jnp.where(g_ok, jnp.clip(nblk_g[tq] - GG * gg, 0, GG), 0)
    g_base = first_g[tq] + GG * gg
    g_off = (jnp.arange(GG, dtype=i32)[None, :] - g_base[:, None]) % GG
    g_parts = _hold_unused(g_base[:, None] + g_off, g_off < g_slots[:, None])
    g_case = jnp.where(g_slots > 0, (g_slots - 1) * GG + g_base % GG, -1)
    g_first = jnp.logical_and(g_ok, gg == 0)
    gather_plan = (tq, g_parts.reshape(-1).astype(i32), g_case.astype(i32), g_first.astype(i32))

    order = jnp.argsort(jnp.where(valid, s_of * RG + q_of, jnp.iinfo(jnp.int32).max))
    s2, q2 = s_of[order], q_of[order]
    CG = MOE_CG
    blocks = jnp.arange(NB, dtype=i32)
    per_blk = jnp.sum(jnp.logical_and(valid[None, :], s2[None, :] == blocks[:, None]), axis=1)
    pb_end = jnp.cumsum(per_blk)
    pb_start = pb_end - per_blk
    nsteps = (per_blk + CG - 1) // CG
    st_end = jnp.cumsum(nsteps)
    total = st_end[-1]
    SMAX = (PMAX + CG - 1) // CG + NB
    j = jnp.arange(SMAX, dtype=i32)
    step_ok = j < total
    jc = jnp.minimum(j, total - 1)
    blk = jnp.minimum(jnp.sum(st_end[None, :] <= jc[:, None], axis=1), NB - 1).astype(i32)
    grp = jc - (st_end - nsteps)[blk]
    slot_p = pb_start[blk][:, None] + CG * grp[:, None] + jnp.arange(CG, dtype=i32)[None, :]
    slot_ok = jnp.logical_and(slot_p < pb_end[blk][:, None], step_ok[:, None])
    c_slots = jnp.sum(slot_ok, axis=1).astype(i32)
    slot_q = _hold_unused(q2[jnp.minimum(slot_p, npairs - 1)], slot_ok)
    c_first = jnp.logical_and(step_ok, grp == 0).astype(i32)
    c_last = jnp.logical_and(step_ok, grp == nsteps[blk] - 1).astype(i32)
    combine_plan = (blk, slot_q.reshape(-1).astype(i32), c_slots, c_first, c_last)

    te, tlo = region(jnp.arange(RT, dtype=i32) * TR)
    tvalid = jnp.clip(cnt[te] - tlo, 0, TR)
    last_t = jnp.sum(tiles) - 1
    t_idx = jnp.where(tvalid > 0, jnp.arange(RT, dtype=i32), last_t).astype(i32)
    ffn_plan = (t_idx, te[t_idx].astype(i32), tvalid.astype(i32))
    return pos_cols, pos_rows, gather_plan, combine_plan, ffn_plan, rmax


def _moe_gather_kernel(q_ref, s_ref, case_ref, first_ref, *refs):
    pos_refs, h_refs, out_ref = refs[:MOE_GG], refs[MOE_GG:2 * MOE_GG], refs[2 * MOE_GG]
    p = pl.program_id(0)
    rows = out_ref.shape[0]

    @pl.when(first_ref[p] == 1)
    def _():
        out_ref[...] = jnp.zeros_like(out_ref)

    for ns in range(1, MOE_GG + 1):
        for rot in range(MOE_GG):
            @pl.when(case_ref[p] == (ns - 1) * MOE_GG + rot)
            def _(ns=ns, rot=rot):
                slots = [(rot + i) % MOE_GG for i in range(ns)]
                row = (lax.broadcasted_iota(jnp.int32, (rows, 1), 0) + q_ref[p] * rows).astype(F32)
                sels = []
                for k in slots:
                    hit = jnp.logical_or(pos_refs[k][0:1, :] == row, pos_refs[k][1:2, :] == row)
                    sels.append(jnp.where(hit, 1.0, 0.0).astype(BF16))
                sel = sels[0] if ns == 1 else jnp.concatenate(sels, axis=1)
                hs = h_refs[slots[0]][...] if ns == 1 else jnp.concatenate([h_refs[k][...] for k in slots], axis=0)
                out_ref[...] = out_ref[...] + _dot(sel, hs).astype(BF16)


def _moe_gather(h, pos_rows, plan, rmax):
    N, D = h.shape
    nsteps = plan[0].shape[0]

    def pos_spec(k):
        return pl.BlockSpec((SUB, MOE_SBG), lambda p, q, s, *_: (0, s[MOE_GG * p + k]))

    def tok_spec(k):
        return pl.BlockSpec((MOE_SBG, D), lambda p, q, s, *_: (s[MOE_GG * p + k], 0))

    return pl.pallas_call(
        _moe_gather_kernel,
        grid_spec=pltpu.PrefetchScalarGridSpec(
            num_scalar_prefetch=4, grid=(nsteps,),
            in_specs=[pos_spec(k) for k in range(MOE_GG)] + [tok_spec(k) for k in range(MOE_GG)],
            out_specs=pl.BlockSpec((MOE_TRG, D), lambda p, q, *_: (q[p], 0))),
        out_shape=jax.ShapeDtypeStruct((rmax, D), BF16),
        compiler_params=_params("arbitrary"),
        name="moe_gather",
    )(*plan, *([pos_rows] * MOE_GG), *([h] * MOE_GG))


def _moe_ffn_kernel(t_ref, e_ref, nv_ref, x_ref, w1_ref, w3_ref, w2_ref, out_ref, acc_scr, *, nf):
    t = pl.program_id(0)
    f = pl.program_id(1)
    nv = nv_ref[t]

    def block(start, size):
        rows = pl.ds(start, size)

        @pl.when(f == 0)
        def _():
            acc_scr[rows, :] = jnp.zeros((size, acc_scr.shape[1]), F32)

        x = x_ref[rows, :]
        a = _dot(x, w1_ref[...].astype(BF16))
        b = _dot(x, w3_ref[...].astype(BF16))
        acc_scr[rows, :] += _dot((_silu(a) * b).astype(BF16), w2_ref[...].astype(BF16))

        @pl.when(f == nf - 1)
        def _():
            out_ref[rows, :] = acc_scr[rows, :].astype(out_ref.dtype)

    nsub = MOE_TR // MOE_TRG
    used = (nv + MOE_TRG - 1) // MOE_TRG

    @pl.when(used == nsub)
    def _():
        block(0, MOE_TR)

    @pl.when(jnp.logical_and(used > 0, used < nsub))
    def _():
        start = jnp.int32(0)
        size = MOE_TR // 2
        while size >= MOE_TRG:
            has = (used & (size // MOE_TRG)) != 0

            @pl.when(has)
            def _(start=start, size=size):
                block(pl.multiple_of(start, MOE_TRG), size)

            start = start + jnp.where(has, size, 0)
            size //= 2


def _moe_ffn(xs, plan, w1, w3, w2):
    rmax, D = xs.shape
    FF = w1.shape[2]
    TF = 256
    nf = FF // TF
    RT = rmax // MOE_TR

    def fidx(t, f, nv):
        return jnp.where(nv[t] > 0, f, nf - 1)

    return pl.pallas_call(
        functools.partial(_moe_ffn_kernel, nf=nf),
        grid_spec=pltpu.PrefetchScalarGridSpec(
            num_scalar_prefetch=3, grid=(RT, nf),
            in_specs=[pl.BlockSpec((MOE_TR, D), lambda t, f, ti, e, nv: (ti[t], 0)),
                      pl.BlockSpec((None, D, TF), lambda t, f, ti, e, nv: (e[t], 0, fidx(t, f, nv))),
                      pl.BlockSpec((None, D, TF), lambda t, f, ti, e, nv: (e[t], 0, fidx(t, f, nv))),
                      pl.BlockSpec((None, TF, D), lambda t, f, ti, e, nv: (e[t], fidx(t, f, nv), 0))],
            out_specs=pl.BlockSpec((MOE_TR, D), lambda t, f, ti, e, nv: (ti[t], 0)),
            scratch_shapes=[pltpu.VMEM((MOE_TR, D), F32)]),
        out_shape=jax.ShapeDtypeStruct((rmax, D), BF16),
        compiler_params=_params("arbitrary", "arbitrary"),
        name="moe_ffn",
    )(*plan, xs, w1, w3, w2)


def _moe_combine_kernel(s_ref, q_ref, slots_ref, first_ref, last_ref, pos_ref, *refs, ntp):
    ys_refs = refs[:MOE_CG]
    xp_ref, xs_ref, gate_ref, fg_ref, op_ref, os_ref, acc_scr = refs[MOE_CG:]
    p = pl.program_id(0)
    rows = ys_refs[0].shape[0]

    @pl.when(first_ref[p] == 1)
    def _():
        acc_scr[...] = jnp.zeros_like(acc_scr)

    for ns in range(1, MOE_CG + 1):
        @pl.when(slots_ref[p] == ns)
        def _(ns=ns):
            sels = []
            for k in range(ns):
                col = (lax.broadcasted_iota(jnp.int32, (1, rows), 1) + q_ref[MOE_CG * p + k] * rows).astype(F32)
                sels.append((jnp.where(pos_ref[:, 0:1] == col, pos_ref[:, 2:3], 0.0)
                             + jnp.where(pos_ref[:, 1:2] == col, pos_ref[:, 3:4], 0.0)).astype(BF16))
            sel = sels[0] if ns == 1 else jnp.concatenate(sels, axis=1)
            ys = ys_refs[0][...] if ns == 1 else jnp.concatenate([ys_refs[k][...] for k in range(ns)], axis=0)
            acc_scr[...] += _dot(sel, ys)

    @pl.when(last_ref[p] == 1)
    def _():
        s = s_ref[p]
        x = jnp.where(s < ntp, xp_ref[...], xs_ref[...])
        y = x + gate_ref[...] * acc_scr[...]
        out = y * lax.rsqrt(jnp.mean(y * y, axis=-1, keepdims=True) + EPS) * fg_ref[...]

        @pl.when(s < ntp)
        def _():
            op_ref[...] = out

        @pl.when(s >= ntp)
        def _():
            os_ref[...] = out


def _moe_combine(ys, pos_cols, plan, xp, xs, mod, final_g, rows_per_group):
    Np, D = xp.shape
    Ns = xs.shape[0]
    SB = MOE_SB
    ntp = Np // SB
    nsteps = plan[0].shape[0]

    def tile_spec(k):
        return pl.BlockSpec((MOE_TRG, D), lambda p, s, q, *_: (q[MOE_CG * p + k], 0))

    def tok_p(p, s, *_):
        return (jnp.minimum(s[p], ntp - 1), 0)

    def tok_s(p, s, *_):
        return (jnp.maximum(s[p] - ntp, 0), 0)

    def gate_idx(p, s, *_):
        return (jnp.where(s[p] < ntp, 0, 1 + ((s[p] - ntp) * SB) // rows_per_group), 5, 0, 0)

    return pl.pallas_call(
        functools.partial(_moe_combine_kernel, ntp=ntp),
        grid_spec=pltpu.PrefetchScalarGridSpec(
            num_scalar_prefetch=5, grid=(nsteps,),
            in_specs=[pl.BlockSpec((SB, SUB), lambda p, s, q, *_: (s[p], 0))]
            + [tile_spec(k) for k in range(MOE_CG)]
            + [pl.BlockSpec((SB, D), tok_p),
                      pl.BlockSpec((SB, D), tok_s),
                      pl.BlockSpec((None, None, 1, D), gate_idx),
                      pl.BlockSpec((1, D), lambda p, *_: (0, 0))],
            out_specs=[pl.BlockSpec((SB, D), tok_p), pl.BlockSpec((SB, D), tok_s)],
            scratch_shapes=[pltpu.VMEM((SB, D), F32)]),
        out_shape=[jax.ShapeDtypeStruct((Np, D), F32), jax.ShapeDtypeStruct((Ns, D), F32)],
        compiler_params=_params("arbitrary"),
        name="moe_combine",
    )(*plan, pos_cols, *([ys] * MOE_CG), xp, xs, mod, final_g.reshape(1, D))


def _moe(xp, xs, g, mod, router_w, w1, w3, w2, final_g, rows_per_group):
    N = xp.shape[0] + xs.shape[0]
    h, info, info_t, cum = _moe_route(xp, xs, g, mod, router_w, rows_per_group)
    pos_cols, pos_rows, gather_plan, combine_plan, ffn_plan, rmax = _moe_plan(info, info_t, cum, N)
    x_sorted = _moe_gather(h, pos_rows, gather_plan, rmax)
    y_sorted = _moe_ffn(x_sorted, ffn_plan, w1, w3, w2)
    return _moe_combine(y_sorted, pos_cols, combine_plan, xp, xs, mod, final_g, rows_per_group)


def _gla_levels(C):
    lv, c = [], C // 2
    while c >= SUB:
        lv.append(c)
        c //= 2
    return lv


def _gla_tables(C):
    levels = _gla_levels(C)
    nr = 2 + 2 * len(levels)
    mat = np.zeros((2, nr * C, C), np.float32)
    code = np.zeros((2, C, C), np.int32)
    for d in range(2):
        p = np.arange(C) if d == 0 else C - 1 - np.arange(C)
        pi, pj = p[:, None], p[None, :]
        mat[d, 0:C] = pj <= pi
        mat[d, C:2 * C] = pj > pi
        code[d] = np.where((pj <= pi) & (pi // SUB == pj // SUB), 1, 0)
        for lv, c in enumerate(levels):
            blk = pi // c
            later = blk % 2 == 1
            mat[d, (2 + 2 * lv) * C:(3 + 2 * lv) * C] = later & (pj > blk * c - 1) & (pj <= pi)
            mat[d, (3 + 2 * lv) * C:(4 + 2 * lv) * C] = (~later) & (pj > pi) & (pj <= (blk + 1) * c - 1)
            pair = (pi // (2 * c) == pj // (2 * c)) & (pi // c != pj // c) & (pj <= pi)
            code[d] = np.where(pair, 2 + lv, code[d])
    ones = np.zeros((SUB * LANES, C), np.float32)
    for jj in range(SUB):
        ones[jj * LANES:(jj + 1) * LANES, jj::SUB] = 1.0
    return jnp.asarray(mat, dtype=BF16), jnp.asarray(code), jnp.asarray(ones, dtype=BF16)


def _bcast_sublane(x, jj):
    r, w = x.shape
    x3 = x.reshape(r // SUB, SUB, w)
    return jnp.broadcast_to(x3[:, jj:jj + 1, :], x3.shape).reshape(r, w)


def _t128(x):
    r, w = x.shape
    if w > LANES:
        return jnp.concatenate([x[:, i:i + LANES].T for i in range(0, w, LANES)], axis=0)
    return jnp.concatenate([x[i:i + LANES, :].T for i in range(0, r, LANES)], axis=1)


def _gla_kernel(q_ref, k_ref, v_ref, g_ref, lr_ref, wg_ref, ba_ref, mat_ref, code_ref, ones_ref,
                s0f_ref, s0b_ref, ng_ref, o_ref, sf_ref, sb_ref, st_scr, of_scr, *, n, C, G):
    d = pl.program_id(1)
    c = pl.program_id(2)
    levels = _gla_levels(C)

    @pl.when(jnp.logical_and(c == 0, d == 0))
    def _():
        for bb in range(G):
            for h in range(C_H):
                st_scr[bb, h] = _t128(s0f_ref[bb, h])

    @pl.when(jnp.logical_and(c == 0, d == 1))
    def _():
        for bb in range(G):
            for h in range(C_H):
                st_scr[bb, h] = _t128(s0b_ref[bb, h])

    mat = mat_ref[...]
    code = code_ref[...]
    ones = ones_ref[...]
    cums = []
    for bb in range(G):
        xg = jnp.dot(lr_ref[bb], wg_ref[...], precision=HIGHEST, preferred_element_type=F32) + ba_ref[...]
        la = (jnp.minimum(xg, 0.0) - jnp.log1p(jnp.exp(-jnp.abs(xg)))) * (LOG2E / C_TAU)
        hi = la.astype(BF16)
        lo = (la - hi.astype(F32)).astype(BF16)
        cums.append(_dot(mat, hi) + _dot(mat, lo))

    def prepare(bb, h):
        cum = cums[bb]
        ks = slice(h * C_DK, (h + 1) * C_DK)
        qh = q_ref[bb, :, ks].astype(F32) * (C_DK ** -0.5)
        kh = k_ref[bb, :, ks].astype(F32)
        b = cum[0:C, ks]
        b_rest = cum[C:2 * C, ks]
        ps = []
        for jj in range(SUB):
            dec = jnp.exp2(jnp.minimum(b - _bcast_sublane(b, jj), 0.0))
            ps.append((qh * _bcast_sublane(kh, jj) * dec).astype(BF16))
        lv_ops = []
        for lv in range(len(levels)):
            eq = cum[(2 + 2 * lv) * C:(3 + 2 * lv) * C, ks]
            ek = cum[(3 + 2 * lv) * C:(4 + 2 * lv) * C, ks]
            lv_ops.append(((qh * jnp.exp2(eq)).astype(BF16), (kh * jnp.exp2(ek)).astype(BF16)))
        qe = (qh * jnp.exp2(b)).astype(BF16)
        ke = (kh * jnp.exp2(b_rest)).astype(BF16)
        e_end = jnp.exp2(b[0:1, :] + b_rest[0:1, :])
        return jnp.concatenate(ps, axis=1), lv_ops, qe, ke, e_end

    def contract(bb, h, prep):
        pcat, lv_ops, qe, ke, e_end = prep
        vh = v_ref[bb, :, h * C_DV:(h + 1) * C_DV].astype(F32)
        att = jnp.where(code == 1, _dot(pcat, ones), 0.0)
        for lv, (qs, ks_) in enumerate(lv_ops):
            att = jnp.where(code == 2 + lv, _dot_nt(qs, ks_), att)
        st = st_scr[bb, h]
        o = _dot(att.astype(BF16), vh.astype(BF16)) + _dot_nt(qe, st.astype(BF16))
        st_scr[bb, h] = e_end * st + _dot(_t128(vh).astype(BF16), ke)
        return o

    units = [(bb, h) for h in range(C_H) for bb in range(G)]
    outs = {}
    prep = prepare(*units[0])
    for idx, (bb, h) in enumerate(units):
        nxt = prepare(*units[idx + 1]) if idx + 1 < len(units) else None
        outs[(bb, h)] = contract(bb, h, prep)
        prep = nxt
    o_all = [jnp.concatenate([outs[(bb, h)] for h in range(C_H)], axis=-1) for bb in range(G)]

    @pl.when(d == 0)
    def _():
        for bb in range(G):
            of_scr[bb, c] = o_all[bb]

    @pl.when(d == 1)
    def _():
        for bb in range(G):
            tot = o_all[bb] + of_scr[bb, n - 1 - c]
            res = []
            for h in range(C_H):
                sl = slice(h * C_DV, (h + 1) * C_DV)
                t = tot[:, sl]
                y = t * lax.rsqrt(jnp.mean(t * t, axis=-1, keepdims=True) + EPS) * ng_ref[:, sl]
                res.append(y * _silu(g_ref[bb, :, sl].astype(F32)))
            o_ref[bb] = jnp.concatenate(res, axis=-1).astype(o_ref.dtype)

    @pl.when(jnp.logical_and(c == n - 1, d == 0))
    def _():
        for bb in range(G):
            for h in range(C_H):
                sf_ref[bb, h] = _t128(st_scr[bb, h])

    @pl.when(jnp.logical_and(c == n - 1, d == 1))
    def _():
        for bb in range(G):
            for h in range(C_H):
                sb_ref[bb, h] = _t128(st_scr[bb, h])


def _gla(z, zg, B, T, w_a2, b_a, s0f, s0b, norm_g):
    C = GLA_CHUNK
    G = _rows_per_step(B, T, C_H * C_DV)
    assert B % G == 0 and T % C == 0
    n = T // C
    HK = C_H * C_DK
    HV = C_H * C_DV
    mat, code, ones = _gla_tables(C)
    nr = mat.shape[1] // C
    wg = jnp.zeros((2, LANES, HK), F32)
    for dr in range(2):
        wg = wg.at[dr, dr * C_RANK:(dr + 1) * C_RANK, :].set(w_a2[dr])
    z3 = z.reshape(B, T, z.shape[1])
    zg3 = zg.reshape(B, T, zg.shape[1])

    def chunk(d, c):
        return c + d * (n - 1 - 2 * c)

    st_spec = pl.BlockSpec((G, C_H, C_DK, C_DV), lambda b, d, c: (b, 0, 0, 0))
    st_shape = jax.ShapeDtypeStruct((B, C_H, C_DK, C_DV), F32)
    o, sf, sb = pl.pallas_call(
        functools.partial(_gla_kernel, n=n, C=C, G=G),
        grid=(B // G, 2, n),
        in_specs=[pl.BlockSpec((G, C, HK), lambda b, d, c: (b, chunk(d, c), 0)),
                  pl.BlockSpec((G, C, HK), lambda b, d, c: (b, chunk(d, c), 1)),
                  pl.BlockSpec((G, C, HV), lambda b, d, c: (b, chunk(d, c), 1)),
                  pl.BlockSpec((G, C, HV), lambda b, d, c: (b, chunk(d, c), 2)),
                  pl.BlockSpec((G, C, LANES), lambda b, d, c: (b, chunk(d, c), 0)),
                  pl.BlockSpec((None, LANES, HK), lambda b, d, c: (d, 0, 0)),
                  pl.BlockSpec((None, 1, HK), lambda b, d, c: (d, 0, 0)),
                  pl.BlockSpec((None, nr * C, C), lambda b, d, c: (d, 0, 0)),
                  pl.BlockSpec((None, C, C), lambda b, d, c: (d, 0, 0)),
                  pl.BlockSpec((SUB * LANES, C), lambda b, d, c: (0, 0)),
                  st_spec, st_spec,
                  pl.BlockSpec((1, HV), lambda b, d, c: (0, 0))],
        out_specs=[pl.BlockSpec((G, C, HV), lambda b, d, c: (b, (n - 1) - d * c, 0)),
                   st_spec, st_spec],
        out_shape=[jax.ShapeDtypeStruct((B, T, HV), BF16), st_shape, st_shape],
        scratch_shapes=[pltpu.VMEM((G, C_H, C_DV, C_DK), F32), pltpu.VMEM((G, n, C, HV), F32)],
        compiler_params=_params("arbitrary", "arbitrary", "arbitrary"),
        name="gla",
    )(z3, z3, z3, z3, zg3, wg, b_a.reshape(2, 1, HK), mat, code, ones, s0f, s0b, norm_g.reshape(1, HV))
    return o.reshape(B * T, HV), sf, sb


def _run_stream(x, B, T, mods, ctx, p):
    N, D = x.shape
    rpg = N // mods[0].shape[0]
    TM = min(2048, rpg)
    nb = (B_H + 2 * B_HKV) * B_HD

    w_in = p['even_w_in'][0]
    z, zb = _norm_mm(x, p['norm1_g'][0], mods[0], (0, 1), w_in, w_in, (nb, MIX_MAIN // nb), TM, rpg)
    if ctx is None:
        s0 = jnp.zeros((B, A_H, A_DK, A_DV), F32)
        a_f0, a_b0, cache_k, cache_v = s0, s0, None, None
    else:
        cache_k, cache_v, a_f0, a_b0 = ctx[0], ctx[1], ctx[2], ctx[3]
    o_a, a_sf, a_sb = _retention(z, B, T, p['a_log_gamma'][0], a_f0, a_b0, p['a_norm_g'][0])
    qpad, k_norm, k_rot, v_bf = _bprep(zb, T, p['b_q_g'][0], p['b_k_g'][0], rope=ctx is not None)
    o_b = _attention(qpad, k_rot, v_bf, B, T, cache_k, cache_v)
    x = _proj_res(x, mods[0], 2, [o_a, o_b], p['even_w_out'][0], rpg)
    x = _ffn(x, p['norm2_g'][0], mods[0], p['ff_w1'][0], p['ff_w3'][0], p['ff_w2'][0], rpg)

    w_in = p['odd_w_in'][0]
    w_gate = jnp.pad(w_in[:, MIX_MAIN:], ((0, 0), (0, LANES - 2 * C_RANK)))
    z1, z1g = _norm_mm(x, p['norm1_g'][1], mods[1], (0, 1), w_in, w_gate, (LANES, 0), TM, rpg)
    if ctx is None:
        s0 = jnp.zeros((B, C_H, C_DK, C_DV), F32)
        c_f0, c_b0 = s0, s0
    else:
        c_f0, c_b0 = ctx[4], ctx[5]
    o_c, c_sf, c_sb = _gla(z1, z1g, B, T, p['c_w_a2'][0], p['c_b_a'][0], c_f0, c_b0, p['c_norm_g'][0])
    x = _proj_res(x, mods[1], 2, [o_c], p['odd_w_out'][0], rpg)
    v_raw = zb[:, (B_H + B_HKV) * B_HD:]
    return x, (k_norm, v_raw, a_sf, a_sb, c_sf, c_sb)


def kernel(x_prompt, x_sample, c, cache_b_k, cache_b_v, state_a_fwd, state_a_bwd, state_c_fwd, state_c_bwd,
           c_ctx, w_mod, b_mod, norm1_g, norm2_g, final_g, even_w_in, even_w_out, a_log_gamma, a_norm_g,
           b_q_g, b_k_g, odd_w_in, c_w_a2, c_b_a, c_norm_g, odd_w_out, ff_w1, ff_w3, ff_w2,
           router_w, moe_w1, moe_w3, moe_w2):
    Bp, Tp, D = x_prompt.shape
    Bs, Ts, _ = x_sample.shape
    L = w_mod.shape[0]
    assert L == 2 and even_w_in.shape[0] == 1 and odd_w_in.shape[0] == 1
    p = dict(norm1_g=norm1_g, norm2_g=norm2_g, final_g=final_g, even_w_in=even_w_in, even_w_out=even_w_out,
             a_log_gamma=a_log_gamma, a_norm_g=a_norm_g, b_q_g=b_q_g, b_k_g=b_k_g, odd_w_in=odd_w_in,
             c_w_a2=c_w_a2, c_b_a=c_b_a, c_norm_g=c_norm_g, odd_w_out=odd_w_out, ff_w1=ff_w1, ff_w3=ff_w3,
             ff_w2=ff_w2, router_w=router_w, moe_w1=moe_w1, moe_w3=moe_w3, moe_w2=moe_w2)

    rows = 8
    conds = jnp.concatenate([c_ctx[None, :], c, jnp.zeros((rows - 1 - Bs, D), F32)], axis=0)
    mod = _modulation(conds, w_mod, b_mod).reshape(L, rows, 6, 1, D)
    mods_p = [mod[l, 0:1] for l in range(L)]
    mods_s = [mod[l, 1:1 + Bs] for l in range(L)]

    x_p, kept = _run_stream(x_prompt.reshape(Bp * Tp, D), Bp, Tp, mods_p, None, p)
    nk = B_HKV * B_HD
    ctx = (cache_b_k[:, 0].reshape(Bs, -1, nk), cache_b_v[:, 0].reshape(Bs, -1, nk),
           state_a_fwd[:, 0], state_a_bwd[:, 0], state_c_fwd[:, 0], state_c_bwd[:, 0])
    x_s, _ = _run_stream(x_sample.reshape(Bs * Ts, D), Bs, Ts, mods_s, ctx, p)
    y_p, y_s = _moe(x_p, x_s, norm2_g[1], mod[1, 0:1 + Bs], router_w[0], moe_w1[0], moe_w3[0], moe_w2[0],
                    final_g, Ts)

    k_norm, v_raw, a_sf, a_sb, c_sf, c_sb = kept
    return (y_p.reshape(Bp, Tp, D), y_s.reshape(Bs, Ts, D),
            k_norm.reshape(Bp, 1, Tp, B_HKV, B_HD), v_raw.reshape(Bp, 1, Tp, B_HKV, B_HD),
            a_sf[:, None], a_sb[:, None], c_sf[:, None], c_sb[:, None])
```

```python
import functools

import numpy as np
import jax
import jax.numpy as jnp
from jax import lax
from jax.experimental import pallas as pl
from jax.experimental.pallas import tpu as pltpu

F32 = jnp.float32
BF16 = jnp.bfloat16
EPS = 1e-6
HIGHEST = lax.Precision.HIGHEST
LOG2E = 1.4426950408889634

VMEM_LIMIT_BYTES = 56 * 1024 * 1024

A_H, A_DK, A_DV = 4, 128, 256
B_H, B_HKV, B_HD = 8, 2, 64
C_H, C_DK, C_DV, C_RANK = 4, 128, 256, 16
C_TAU = 16.0
GRID_W = 64
ROPE_THETA = 10000.0
N_EXPERTS = 8
LANES = 128
SUB = 8
RET_CHUNK = 128
GLA_CHUNK = 128
Q_TILE = 128
SCAN_ROWS_MAX = 4
SCAN_FWD_BYTES = 32 * 1024 * 1024


def _rows_per_step(B, T, width):
    g = SCAN_ROWS_MAX
    while g > 1 and (B % g or g * T * width * 4 > SCAN_FWD_BYTES):
        g //= 2
    return g


def _params(*sem):
    return pltpu.CompilerParams(dimension_semantics=sem, vmem_limit_bytes=VMEM_LIMIT_BYTES)


def _dot(a, b):
    return jnp.dot(a, b, preferred_element_type=F32)


def _dot_nt(a, b):
    return lax.dot_general(a, b, (((1,), (1,)), ((), ())), preferred_element_type=F32)


def _silu(x):
    return x * jax.nn.sigmoid(x)


def _norm_mod(x, g, sh, sc):
    r = lax.rsqrt(jnp.mean(x * x, axis=-1, keepdims=True) + EPS)
    return (x * r * g) * (1.0 + sc) + sh


def _mod_kernel(c_ref, w_ref, b_ref, o_ref):
    c = c_ref[...]
    o_ref[...] = jnp.dot(_silu(c), w_ref[...], precision=HIGHEST, preferred_element_type=F32) + b_ref[...]


def _modulation(conds, w_mod, b_mod):
    L, D, D6 = w_mod.shape
    R = conds.shape[0]
    TN = 1024
    return pl.pallas_call(
        _mod_kernel,
        grid=(L, D6 // TN),
        in_specs=[pl.BlockSpec((R, D), lambda l, j: (0, 0)),
                  pl.BlockSpec((None, D, TN), lambda l, j: (l, 0, j)),
                  pl.BlockSpec((None, 1, TN), lambda l, j: (l, 0, j))],
        out_specs=pl.BlockSpec((None, R, TN), lambda l, j: (l, 0, j)),
        out_shape=jax.ShapeDtypeStruct((L, R, D6), F32),
        compiler_params=_params("arbitrary", "arbitrary"),
        name="modulation",
    )(conds, w_mod, b_mod.reshape(L, 1, D6))


def _mod_spec(part, D, TM, rows_per_group, axis):
    def idx(*g):
        return ((g[axis] * TM) // rows_per_group, part, 0, 0)
    return pl.BlockSpec((None, None, 1, D), idx)


MIX_MAIN = A_H * (2 * A_DK + 2 * A_DV)
MIX_TN = 768


def _norm_mm_kernel(x_ref, g_ref, sh_ref, sc_ref, w_ref, we_ref, o_ref, oe_ref, h_scr, *, nmain):
    j = pl.program_id(1)

    @pl.when(j == 0)
    def _():
        h_scr[...] = _norm_mod(x_ref[...], g_ref[...], sh_ref[...], sc_ref[...]).astype(BF16)

    @pl.when(j < nmain)
    def _():
        o_ref[...] = _dot(h_scr[...], w_ref[...].astype(BF16)).astype(o_ref.dtype)

    @pl.when(j == nmain)
    def _():
        oe_ref[...] = _dot(h_scr[...], we_ref[...].astype(BF16))


def _norm_mm(x, g, mod, parts, w, w_extra, extra_block, TM, rows_per_group):
    N, D = x.shape
    nmain = MIX_MAIN // MIX_TN
    WE = extra_block[0]
    return pl.pallas_call(
        functools.partial(_norm_mm_kernel, nmain=nmain),
        grid=(N // TM, nmain + 1),
        in_specs=[pl.BlockSpec((TM, D), lambda i, j: (i, 0)),
                  pl.BlockSpec((1, D), lambda i, j: (0, 0)),
                  _mod_spec(parts[0], D, TM, rows_per_group, 0),
                  _mod_spec(parts[1], D, TM, rows_per_group, 0),
                  pl.BlockSpec((D, MIX_TN), lambda i, j: (0, jnp.minimum(j, nmain - 1))),
                  pl.BlockSpec((D, WE), lambda i, j: (0, extra_block[1]))],
        out_specs=[pl.BlockSpec((TM, MIX_TN), lambda i, j: (i, jnp.minimum(j, nmain - 1))),
                   pl.BlockSpec((TM, WE), lambda i, j: (i, 0))],
        out_shape=[jax.ShapeDtypeStruct((N, MIX_MAIN), BF16), jax.ShapeDtypeStruct((N, WE), F32)],
        scratch_shapes=[pltpu.VMEM((TM, D), BF16)],
        compiler_params=_params("arbitrary", "arbitrary"),
        name="norm_mm",
    )(x, g.reshape(1, D), mod, mod, w, w_extra)


def _ret_kernel(lg_ref, q_ref, k_ref, v_ref, ag_ref, s0f_ref, s0b_ref, ng_ref,
                o_ref, sf_ref, sb_ref, s_scr, of_scr, *, n, C, G):
    d = pl.program_id(1)
    c = pl.program_id(2)

    @pl.when(jnp.logical_and(c == 0, d == 0))
    def _():
        s_scr[...] = s0f_ref[...]

    @pl.when(jnp.logical_and(c == 0, d == 1))
    def _():
        s_scr[...] = s0b_ref[...]

    df = d.astype(F32)
    sgn = 1.0 - 2.0 * df
    ii = lax.broadcasted_iota(jnp.int32, (C, C), 0).astype(F32)
    jj = lax.broadcasted_iota(jnp.int32, (C, C), 1).astype(F32)
    dd = (ii - jj) * sgn
    feeds = dd >= 0.0
    ddc = jnp.maximum(dd, 0.0)
    ri = lax.broadcasted_iota(jnp.int32, (C, 1), 0).astype(F32)
    pos_q = (ri + 1.0) + df * (C - 2.0 * ri - 1.0)
    pos_k = (C - 1.0 - ri) + df * (2.0 * ri - C + 1.0)
    chunk_len = jnp.full((1, A_DV), float(C), F32)

    outs = [[] for _ in range(G)]
    for h in range(A_H):
        lg = lg_ref[d, h]
        dmask = jnp.where(feeds, jnp.exp2(lg * ddc), 0.0)
        q_dec = jnp.exp2(lg * pos_q)
        k_dec = jnp.exp2(lg * pos_k)
        c_dec = jnp.exp2(lg * chunk_len)
        for bb in range(G):
            qh = q_ref[bb, :, h * A_DK:(h + 1) * A_DK].astype(F32) * (A_DK ** -0.5)
            kh = k_ref[bb, :, h * A_DK:(h + 1) * A_DK].astype(F32)
            vh = v_ref[bb, :, h * A_DV:(h + 1) * A_DV].astype(BF16)
            s = s_scr[bb, h]
            att = _dot_nt(qh.astype(BF16), kh.astype(BF16)) * dmask
            o = _dot(att.astype(BF16), vh) + _dot((qh * q_dec).astype(BF16), s.astype(BF16))
            kd = kh * k_dec
            s_scr[bb, h] = c_dec * s + _dot(kd.T.astype(BF16), vh)
            outs[bb].append(o)
    o_all = [jnp.concatenate(o, axis=-1) for o in outs]

    @pl.when(d == 0)
    def _():
        for bb in range(G):
            of_scr[bb, c] = o_all[bb]

    @pl.when(d == 1)
    def _():
        for bb in range(G):
            tot = o_all[bb] + of_scr[bb, n - 1 - c]
            res = []
            for h in range(A_H):
                sl = slice(h * A_DV, (h + 1) * A_DV)
                t = tot[:, sl]
                dev = t - jnp.mean(t, axis=-1, keepdims=True)
                y = dev * lax.rsqrt(jnp.mean(dev * dev, axis=-1, keepdims=True) + EPS) * ng_ref[:, sl]
                res.append(y * _silu(ag_ref[bb, :, sl].astype(F32)))
            o_ref[bb] = jnp.concatenate(res, axis=-1).astype(o_ref.dtype)

    @pl.when(jnp.logical_and(c == n - 1, d == 0))
    def _():
        sf_ref[...] = s_scr[...]

    @pl.when(jnp.logical_and(c == n - 1, d == 1))
    def _():
        sb_ref[...] = s_scr[...]


def _retention(z, B, T, log_gamma, s0f, s0b, norm_g):
    C = RET_CHUNK
    G = _rows_per_step(B, T, A_H * A_DV)
    assert B % G == 0 and T % C == 0
    n = T // C
    HK = A_H * A_DK
    HV = A_H * A_DV
    z3 = z.reshape(B, T, z.shape[1])

    def chunk(d, c):
        return c + d * (n - 1 - 2 * c)

    st_spec = pl.BlockSpec((G, A_H, A_DK, A_DV), lambda b, d, c: (b, 0, 0, 0))
    st_shape = jax.ShapeDtypeStruct((B, A_H, A_DK, A_DV), F32)
    o, sf, sb = pl.pallas_call(
        functools.partial(_ret_kernel, n=n, C=C, G=G),
        grid=(B // G, 2, n),
        in_specs=[pl.BlockSpec(memory_space=pltpu.SMEM),
                  pl.BlockSpec((G, C, HK), lambda b, d, c: (b, chunk(d, c), 0)),
                  pl.BlockSpec((G, C, HK), lambda b, d, c: (b, chunk(d, c), 1)),
                  pl.BlockSpec((G, C, HV), lambda b, d, c: (b, chunk(d, c), 1)),
                  pl.BlockSpec((G, C, HV), lambda b, d, c: (b, chunk(d, c), 2)),
                  st_spec, st_spec,
                  pl.BlockSpec((1, HV), lambda b, d, c: (0, 0))],
        out_specs=[pl.BlockSpec((G, C, HV), lambda b, d, c: (b, (n - 1) - d * c, 0)),
                   st_spec, st_spec],
        out_shape=[jax.ShapeDtypeStruct((B, T, HV), BF16), st_shape, st_shape],
        scratch_shapes=[pltpu.VMEM((G, A_H, A_DK, A_DV), F32), pltpu.VMEM((G, n, C, HV), F32)],
        compiler_params=_params("arbitrary", "arbitrary", "arbitrary"),
        name="retention",
    )(log_gamma * LOG2E, z3, z3, z3, z3, s0f, s0b, norm_g.reshape(1, HV))
    return o.reshape(B * T, HV), sf, sb


def _group_sum_matrix(width, group):
    i = np.arange(width)
    return jnp.asarray((i[:, None] // group == i[None, :] // group).astype(np.float32), dtype=BF16)


def _q_pad_matrix():
    m = np.zeros((B_H * B_HD, B_H * LANES), np.float32)
    g = B_H // B_HKV
    for h in range(B_H):
        for t in range(B_HD):
            m[h * B_HD + t, h * LANES + (h // g) * B_HD + t] = 1.0
    return jnp.asarray(m, dtype=BF16)


def _rope_tables(T):
    rows = T // GRID_W
    row = np.repeat(np.arange(rows, dtype=np.float64), GRID_W)
    col = np.tile(np.arange(GRID_W, dtype=np.float64), rows)
    nq = B_HD // 4
    inv = ROPE_THETA ** (-np.arange(nq, dtype=np.float64) / nq)
    ang = np.concatenate([row[:, None] * inv, col[:, None] * inv], axis=-1)
    cos = np.repeat(np.cos(ang), 2, axis=-1)
    sin = np.repeat(np.sin(ang), 2, axis=-1)
    sign = np.tile(np.array([-1.0, 1.0]), B_HD // 2)
    reps = LANES // B_HD
    return (jnp.asarray(np.tile(cos, (1, reps)), dtype=F32),
            jnp.asarray(np.tile(sin * sign, (1, reps)), dtype=F32))


def _group_rmsnorm(x, gsum, g):
    x2 = x * x
    hi = x2.astype(BF16)
    lo = (x2 - hi.astype(F32)).astype(BF16)
    ss = _dot(hi, gsum) + _dot(lo, gsum)
    return x * lax.rsqrt(ss * (1.0 / B_HD) + EPS) * g


def _rotate_pairs(x, cos, sin_signed):
    n = x.shape[1]
    lane = lax.broadcasted_iota(jnp.int32, x.shape, 1)
    partner = jnp.where(lane % 2 == 0, pltpu.roll(x, n - 1, 1), pltpu.roll(x, 1, 1))
    reps = n // LANES
    if reps > 1:
        cos = jnp.concatenate([cos] * reps, axis=1)
        sin_signed = jnp.concatenate([sin_signed] * reps, axis=1)
    return x * cos + partner * sin_signed


def _bprep_kernel(z_ref, qg_ref, kg_ref, cos_ref, sin_ref, gq_ref, gk_ref, pad_ref,
                  qpad_ref, kn_ref, kr_ref, vb_ref, *, rope):
    nq = B_H * B_HD
    nk = B_HKV * B_HD
    qn = _group_rmsnorm(z_ref[:, 0:nq], gq_ref[...], qg_ref[...])
    kn = _group_rmsnorm(z_ref[:, nq:nq + nk], gk_ref[...], kg_ref[...])
    kn_ref[...] = kn
    if rope:
        qn = _rotate_pairs(qn, cos_ref[...], sin_ref[...])
        kn = _rotate_pairs(kn, cos_ref[...], sin_ref[...])
    kr_ref[...] = kn.astype(BF16)
    vb_ref[...] = z_ref[:, nq + nk:nq + 2 * nk].astype(BF16)
    qs = (qn * (B_HD ** -0.5 * LOG2E)).astype(BF16)
    qpad_ref[...] = _dot(qs, pad_ref[...]).astype(BF16)


def _bprep(z, T, q_g, k_g, rope):
    N = z.shape[0]
    TM = min(512, T)
    nq = B_H * B_HD
    nk = B_HKV * B_HD
    width = nq + 2 * nk
    assert z.shape[1] == width
    cos, sin = _rope_tables(T if rope else TM)
    nt = T // TM if rope else 1
    const = lambda i: (0, 0)
    return pl.pallas_call(
        functools.partial(_bprep_kernel, rope=rope),
        grid=(N // TM,),
        in_specs=[pl.BlockSpec((TM, width), lambda i: (i, 0)),
                  pl.BlockSpec((1, nq), const),
                  pl.BlockSpec((1, nk), const),
                  pl.BlockSpec((TM, LANES), lambda i: (i % nt, 0)),
                  pl.BlockSpec((TM, LANES), lambda i: (i % nt, 0)),
                  pl.BlockSpec((nq, nq), const),
                  pl.BlockSpec((nk, nk), const),
                  pl.BlockSpec((nq, B_H * LANES), const)],
        out_specs=[pl.BlockSpec((TM, B_H * LANES), lambda i: (i, 0)),
                   pl.BlockSpec((TM, nk), lambda i: (i, 0)),
                   pl.BlockSpec((TM, nk), lambda i: (i, 0)),
                   pl.BlockSpec((TM, nk), lambda i: (i, 0))],
        out_shape=[jax.ShapeDtypeStruct((N, B_H * LANES), BF16),
                   jax.ShapeDtypeStruct((N, nk), F32),
                   jax.ShapeDtypeStruct((N, nk), BF16),
                   jax.ShapeDtypeStruct((N, nk), BF16)],
        compiler_params=_params("arbitrary"),
        name="attn_prep",
    )(z, jnp.tile(q_g, B_H).reshape(1, nq), jnp.tile(k_g, B_HKV).reshape(1, nk), cos, sin,
      _group_sum_matrix(nq, B_HD), _group_sum_matrix(nk, B_HD), _q_pad_matrix())


def _lane_fold(x, op):
    acc = x[:, 0:LANES]
    for j in range(1, x.shape[1] // LANES):
        acc = op(acc, x[:, j * LANES:(j + 1) * LANES])
    return acc


def _attn_kernel(*refs, has_cache, kc, nq):
    if has_cache:
        q_ref, k_ref, v_ref, ck_ref, cv_ref, o_ref, s_scr, m_scr, mprev_scr, l_scr, acc_scr = refs
        kcc = min(kc, ck_ref.shape[0])
        ncache = ck_ref.shape[0] // kcc
    else:
        q_ref, k_ref, v_ref, o_ref, s_scr, m_scr, mprev_scr, l_scr, acc_scr = refs
        kcc, ncache = kc, 0
    i = pl.program_id(1)
    tq = q_ref.shape[0]
    nlat = k_ref.shape[0] // kc

    def score(c, kblk):
        q = jnp.concatenate([q_ref[:, h * LANES:(h + 1) * LANES] for h in range(B_H)], axis=0)
        s = _dot_nt(q, kblk)
        s_scr[c, :, 0:kblk.shape[0]] = s
        m_scr[...] = jnp.maximum(m_scr[...], _lane_fold(s, jnp.maximum))

    def weight(c, vblk):
        s = s_scr[c, :, 0:vblk.shape[0]]
        mp = mprev_scr[...]
        ps = [jnp.exp2(s[:, j * LANES:(j + 1) * LANES] - mp) for j in range(vblk.shape[0] // LANES)]
        tot = ps[0]
        for pj in ps[1:]:
            tot = tot + pj
        l_scr[...] += tot
        acc_scr[...] += _dot(jnp.concatenate(ps, axis=1).astype(BF16), vblk)

    def run(do_weight, do_score):
        def unit(c, kblk, vblk):
            if do_weight:
                weight(c, vblk())
            if do_score:
                score(c, kblk())

        for c in range(ncache):
            unit(c, lambda: ck_ref[c * kcc:(c + 1) * kcc, :].astype(BF16),
                 lambda: cv_ref[c * kcc:(c + 1) * kcc, :].astype(BF16))

        def body(c, carry):
            rows = pl.ds(pl.multiple_of(c * kc, kc), kc)
            unit(ncache + c, lambda: k_ref[rows, :], lambda: v_ref[rows, :])
            return carry
        lax.fori_loop(0, nlat, body, 0)

    @pl.when(i < nq)
    def _():
        m_scr[...] = jnp.full(m_scr.shape, -jnp.inf, F32)

    @pl.when(i > 0)
    def _():
        l_scr[...] = jnp.zeros_like(l_scr)
        acc_scr[...] = jnp.zeros_like(acc_scr)

    @pl.when(i == 0)
    def _():
        run(False, True)

    @pl.when(jnp.logical_and(i > 0, i < nq))
    def _():
        run(True, True)

    @pl.when(i == nq)
    def _():
        run(True, False)

    @pl.when(i > 0)
    def _():
        r_all = acc_scr[...] / jnp.sum(l_scr[...], axis=-1, keepdims=True)
        g = B_H // B_HKV
        lane = lax.broadcasted_iota(jnp.int32, (tq, LANES), 1)
        outs = []
        for j in range(B_H // 2):
            pair = []
            for half in range(2):
                h = 2 * j + half
                r = r_all[h * tq:(h + 1) * tq, :]
                if h // g != half:
                    r = pltpu.roll(r, B_HD, 1)
                pair.append(r)
            outs.append(jnp.where(lane < B_HD, pair[0], pair[1]))
        o_ref[...] = jnp.concatenate(outs, axis=-1).astype(o_ref.dtype)

    @pl.when(i < nq)
    def _():
        mprev_scr[...] = jnp.broadcast_to(jnp.max(m_scr[...], axis=-1, keepdims=True), mprev_scr.shape)


def _attention(qpad, kr, vb, B, T, cache_k, cache_v):
    has_cache = cache_k is not None
    TQ = Q_TILE
    nq = T // TQ
    nk = B_HKV * B_HD
    in_specs = [pl.BlockSpec((TQ, B_H * LANES), lambda b, i: (b * nq + jnp.minimum(i, nq - 1), 0)),
                pl.BlockSpec((T, nk), lambda b, i: (b, 0)),
                pl.BlockSpec((T, nk), lambda b, i: (b, 0))]
    args = [qpad, kr, vb]
    kc = min(1024, T)
    nchunks = T // kc
    if has_cache:
        P = cache_k.shape[1]
        assert P % min(kc, P) == 0
        nchunks += P // min(kc, P)
        in_specs += [pl.BlockSpec((None, P, nk), lambda b, i: (b, 0, 0))] * 2
        args += [cache_k, cache_v]
    R = B_H * TQ
    return pl.pallas_call(
        functools.partial(_attn_kernel, has_cache=has_cache, kc=kc, nq=nq),
        grid=(B, nq + 1),
        in_specs=in_specs,
        out_specs=pl.BlockSpec((TQ, B_H * B_HD), lambda b, i: (b * nq + jnp.maximum(i - 1, 0), 0)),
        out_shape=jax.ShapeDtypeStruct((B * T, B_H * B_HD), BF16),
        scratch_shapes=[pltpu.VMEM((nchunks, R, kc), F32)] + [pltpu.VMEM((R, LANES), F32)] * 4,
        compiler_params=_params("arbitrary", "arbitrary"),
        name="attention",
    )(*args)


def _proj_res_kernel(*refs, n_in):
    x_ref, gate_ref = refs[0], refs[1]
    o_refs = refs[2:2 + n_in]
    w_refs = refs[2 + n_in:2 + 2 * n_in]
    out_ref = refs[2 + 2 * n_in]
    wbf_refs = refs[3 + 2 * n_in:]

    @pl.when(pl.program_id(0) == 0)
    def _():
        for w_ref, wbf_ref in zip(w_refs, wbf_refs):
            wbf_ref[...] = w_ref[...].astype(BF16)

    acc = _dot(o_refs[0][...], wbf_refs[0][...])
    for o_ref, wbf_ref in zip(o_refs[1:], wbf_refs[1:]):
        acc = acc + _dot(o_ref[...], wbf_ref[...])
    out_ref[...] = x_ref[...] + gate_ref[...] * acc


def _proj_res(x, mod, part, acts, w, rows_per_group):
    N, D = x.shape
    TM = min(1024, rows_per_group)
    n_in = len(acts)
    widths = [a.shape[1] for a in acts]
    offs = np.cumsum([0] + widths[:-1]).tolist()
    in_specs = [pl.BlockSpec((TM, D), lambda i: (i, 0)),
                _mod_spec(part, D, TM, rows_per_group, 0)]
    in_specs += [pl.BlockSpec((TM, wd), lambda i: (i, 0)) for wd in widths]
    in_specs += [pl.BlockSpec((wd, D), functools.partial(lambda i, blk: (blk, 0), blk=off // wd))
                 for wd, off in zip(widths, offs)]
    return pl.pallas_call(
        functools.partial(_proj_res_kernel, n_in=n_in),
        grid=(N // TM,),
        in_specs=in_specs,
        out_specs=pl.BlockSpec((TM, D), lambda i: (i, 0)),
        out_shape=jax.ShapeDtypeStruct((N, D), F32),
        scratch_shapes=[pltpu.VMEM((wd, D), BF16) for wd in widths],
        compiler_params=_params("arbitrary"),
        name="proj_residual",
    )(x, mod, *acts, *([w] * n_in))


def _ffn_kernel(x_ref, oa_ref, ob_ref, wo_ref, g1_ref, g_ref, sh_ref, sc_ref, gate_ref, w1_ref, w3_ref, w2_ref,
                out_ref, wo_scr, x1_scr, h_scr, acc_scr, *, nf):
    f = pl.program_id(1)

    @pl.when(jnp.logical_and(pl.program_id(0) == 0, f == 0))
    def _():
        wo_scr[...] = wo_ref[...].astype(BF16)

    @pl.when(f == 0)
    def _():
        na = oa_ref.shape[1]
        y = _dot(oa_ref[...], wo_scr[0:na, :]) + _dot(ob_ref[...], wo_scr[na:, :])
        x1 = x_ref[...] + g1_ref[...] * y
        x1_scr[...] = x1
        h_scr[...] = _norm_mod(x1, g_ref[...], sh_ref[...], sc_ref[...]).astype(BF16)
        acc_scr[...] = jnp.zeros_like(acc_scr)

    h = h_scr[...]
    a = _dot(h, w1_ref[...].astype(BF16))
    b = _dot(h, w3_ref[...].astype(BF16))
    acc_scr[...] += _dot((_silu(a) * b).astype(BF16), w2_ref[...].astype(BF16))

    @pl.when(f == nf - 1)
    def _():
        out_ref[...] = x1_scr[...] + gate_ref[...] * acc_scr[...]


def _ffn(x, o_a, o_b, w_out, g, mod, w1, w3, w2, rows_per_group):
    N, D = x.shape
    FF = w1.shape[1]
    TM, TF = min(1024, rows_per_group), 256
    nf = FF // TF
    na, nb = o_a.shape[1], o_b.shape[1]
    return pl.pallas_call(
        functools.partial(_ffn_kernel, nf=nf),
        grid=(N // TM, nf),
        in_specs=[pl.BlockSpec((TM, D), lambda i, f: (i, 0)),
                  pl.BlockSpec((TM, na), lambda i, f: (i, 0)),
                  pl.BlockSpec((TM, nb), lambda i, f: (i, 0)),
                  pl.BlockSpec((na + nb, D), lambda i, f: (0, 0), pipeline_mode=pl.Buffered(1)),
                  _mod_spec(2, D, TM, rows_per_group, 0),
                  pl.BlockSpec((1, D), lambda i, f: (0, 0)),
                  _mod_spec(3, D, TM, rows_per_group, 0),
                  _mod_spec(4, D, TM, rows_per_group, 0),
                  _mod_spec(5, D, TM, rows_per_group, 0),
                  pl.BlockSpec((D, TF), lambda i, f: (0, f)),
                  pl.BlockSpec((D, TF), lambda i, f: (0, f)),
                  pl.BlockSpec((TF, D), lambda i, f: (f, 0))],
        out_specs=pl.BlockSpec((TM, D), lambda i, f: (i, 0)),
        out_shape=jax.ShapeDtypeStruct((N, D), F32),
        scratch_shapes=[pltpu.VMEM((na + nb, D), BF16), pltpu.VMEM((TM, D), F32), pltpu.VMEM((TM, D), BF16),
                        pltpu.VMEM((TM, D), F32)],
        compiler_params=_params("arbitrary", "arbitrary"),
        name="ffn",
    )(x, o_a, o_b, w_out, mod, g.reshape(1, D), mod, mod, mod, w1, w3, w2)


MOE_SB = 1024
MOE_SBG = 512
MOE_GG = 4
MOE_TRG = 256
MOE_TR = 2048
MOE_CG = 4


def _two_stream_specs(shape, ntp, ax=0):
    def idx_p(*g):
        return (jnp.minimum(g[ax], ntp - 1), 0)

    def idx_s(*g):
        return (jnp.maximum(g[ax] - ntp, 0), 0)
    return pl.BlockSpec(shape, idx_p), pl.BlockSpec(shape, idx_s)


def _pool_mod_spec(part, D, TM, ntp, rows_per_group):
    def idx(i, *_):
        return (jnp.where(i < ntp, 0, 1 + ((i - ntp) * TM) // rows_per_group), part, 0, 0)
    return pl.BlockSpec((None, None, 1, D), idx)


def _route_kernel(xp_ref, xs_ref, g_ref, sh_ref, sc_ref, rw_ref, tri_ref, h_ref, info_ref, infot_ref, cum_ref,
                  carry_scr, *, ntp):
    i = pl.program_id(0)

    @pl.when(i == 0)
    def _():
        carry_scr[...] = jnp.zeros_like(carry_scr)

    x = jnp.where(i < ntp, xp_ref[...], xs_ref[...])
    h = _norm_mod(x, g_ref[...], sh_ref[...], sc_ref[...])
    h_ref[...] = h.astype(BF16)
    lane = lax.broadcasted_iota(jnp.int32, (x.shape[0], LANES), 1).astype(F32)
    rw = rw_ref[...]
    h_hi = h.astype(BF16)
    h_lo = (h - h_hi.astype(F32)).astype(BF16)
    rw_hi = rw.astype(BF16)
    rw_lo = (rw - rw_hi.astype(F32)).astype(BF16)
    logits = _dot(h_hi, rw_hi) + (_dot(h_hi, rw_lo) + _dot(h_lo, rw_hi))
    logits = jnp.where(lane < N_EXPERTS, logits, -jnp.inf)
    m1 = jnp.max(logits, axis=-1, keepdims=True)
    i1 = jnp.min(jnp.where(logits == m1, lane, float(LANES)), axis=-1, keepdims=True)
    rest = jnp.where(lane == i1, -jnp.inf, logits)
    m2 = jnp.max(rest, axis=-1, keepdims=True)
    i2 = jnp.min(jnp.where(rest == m2, lane, float(LANES)), axis=-1, keepdims=True)
    e2 = jnp.exp(m2 - m1)
    w1 = 1.0 / (1.0 + e2)
    w2 = e2 / (1.0 + e2)
    ind = jnp.where(jnp.logical_or(lane == i1, lane == i2), 1.0, 0.0)
    before = _dot(tri_ref[...], ind.astype(BF16)) + carry_scr[...]
    r1 = jnp.sum(jnp.where(lane == i1, before, 0.0), axis=-1, keepdims=True)
    r2 = jnp.sum(jnp.where(lane == i2, before, 0.0), axis=-1, keepdims=True)
    total = carry_scr[...] + jnp.sum(ind, axis=0, keepdims=True)
    carry_scr[...] = total
    for part in range(1, MOE_SB // MOE_SBG):
        cum_ref[part - 1] = before[part * MOE_SBG:part * MOE_SBG + 1, :]
    cum_ref[MOE_SB // MOE_SBG - 1] = total
    info = jnp.where(lane == 0.0, i1, jnp.where(lane == 1.0, i2, jnp.where(lane == 2.0, w1, jnp.where(
        lane == 3.0, w2, jnp.where(lane == 4.0, r1, jnp.where(lane == 5.0, r2, 0.0))))))
    info_ref[...] = info[:, 0:SUB]
    info_t = jnp.concatenate([info[r:r + LANES, :].T for r in range(0, info.shape[0], LANES)], axis=1)
    infot_ref[...] = info_t[0:SUB, :]


def _moe_route(xp, xs, g, mod, router_w, rows_per_group):
    Np, D = xp.shape
    N = Np + xs.shape[0]
    TM = MOE_SB
    ntp = Np // TM
    nt = N // TM
    rw = jnp.pad(router_w, ((0, 0), (0, LANES - router_w.shape[1])))
    tri = jnp.asarray(np.tril(np.ones((TM, TM), np.float32), -1), dtype=BF16)
    xp_spec, xs_spec = _two_stream_specs((TM, D), ntp)
    return pl.pallas_call(
        functools.partial(_route_kernel, ntp=ntp),
        grid=(nt,),
        in_specs=[xp_spec, xs_spec,
                  pl.BlockSpec((1, D), lambda i: (0, 0)),
                  _pool_mod_spec(3, D, TM, ntp, rows_per_group),
                  _pool_mod_spec(4, D, TM, ntp, rows_per_group),
                  pl.BlockSpec((D, LANES), lambda i: (0, 0)),
                  pl.BlockSpec((TM, TM), lambda i: (0, 0))],
        out_specs=[pl.BlockSpec((TM, D), lambda i: (i, 0)),
                   pl.BlockSpec((TM, SUB), lambda i: (i, 0)),
                   pl.BlockSpec((SUB, TM), lambda i: (0, i)),
                   pl.BlockSpec((MOE_SB // MOE_SBG, 1, LANES), lambda i: (i, 0, 0))],
        out_shape=[jax.ShapeDtypeStruct((N, D), BF16),
                   jax.ShapeDtypeStruct((N, SUB), F32),
                   jax.ShapeDtypeStruct((SUB, N), F32),
                   jax.ShapeDtypeStruct((nt * (MOE_SB // MOE_SBG), 1, LANES), F32)],
        scratch_shapes=[pltpu.VMEM((1, LANES), F32)],
        compiler_params=_params("arbitrary"),
        name="moe_route",
    )(xp, xs, g.reshape(1, D), mod, mod, rw, tri)


def _hold_unused(idx, read):
    steps, slots = idx.shape
    step = jnp.arange(steps, dtype=jnp.int32)[:, None]
    last = lax.cummax(jnp.where(read, step, -1), axis=0)
    held = jnp.take_along_axis(idx, jnp.maximum(last, 0), axis=0)
    return jnp.where(last >= 0, held, 0)


def _moe_plan(info, info_t, cum, N):
    E, SB, TRG, TR = N_EXPERTS, MOE_SB, MOE_TRG, MOE_TR
    NB = N // SB
    rmax = 2 * N + E * TR
    RG, RT = rmax // TRG, rmax // TR
    PMAX = RG + E * NB
    i32 = jnp.int32
    parts = SB // MOE_SBG
    cum_g = cum[:, 0, :E].astype(i32).T
    cum_e = cum_g[:, parts - 1::parts]
    cnt = cum_e[:, -1]
    tiles = (cnt + TR - 1) // TR
    start = TR * (jnp.cumsum(tiles) - tiles)

    startf = start.astype(F32)

    def region_start(e):
        out = jnp.zeros_like(e)
        for k in range(E):
            out = jnp.where(e == float(k), startf[k], out)
        return out

    pos_cols = jnp.concatenate([region_start(info[:, 0:2]) + info[:, 4:6], info[:, 2:4],
                                jnp.zeros((N, 4), F32)], axis=1)
    pos_rows = jnp.concatenate([region_start(info_t[0:2]) + info_t[4:6], jnp.zeros((6, N), F32)],
                               axis=0)

    def region(row0):
        e = jnp.clip(jnp.sum(row0[:, None] >= start[None, :], axis=1) - 1, 0, E - 1)
        return e, row0 - start[e]

    eq, lo = region(jnp.arange(RG, dtype=i32) * TRG)
    hi = jnp.minimum(lo + TRG, cnt[eq])
    first = jnp.sum(cum_e[eq] <= lo[:, None], axis=1)
    last = jnp.sum(cum_e[eq] < hi[:, None], axis=1)
    nblk = jnp.where(hi > lo, last - first + 1, 0)
    pend = jnp.cumsum(nblk)
    npairs = pend[-1]
    p = jnp.arange(PMAX, dtype=i32)
    valid = p < npairs
    pc = jnp.minimum(p, npairs - 1)
    q_of = jnp.minimum(jnp.sum(pend[None, :] <= pc[:, None], axis=1), RG - 1).astype(i32)
    pstart = pend - nblk
    s_of = (first[q_of] + pc - pstart[q_of]).astype(i32)

    GG = MOE_GG
    first_g = jnp.sum(cum_g[eq] <= lo[:, None], axis=1)
    last_g = jnp.sum(cum_g[eq] < hi[:, None], axis=1)
    nblk_g = jnp.where(hi > lo, last_g - first_g + 1, 0)
    nst = (nblk_g + GG - 1) // GG
    gst_end = jnp.cumsum(nst)
    gtotal = gst_end[-1]
    smax_g = (RG + E * NB * parts + (GG - 1) * RG) // GG + 1
    jg = jnp.arange(smax_g, dtype=i32)
    g_ok = jg < gtotal
    jgc = jnp.minimum(jg, gtotal - 1)
    tq = jnp.minimum(jnp.sum(gst_end[None, :] <= jgc[:, None], axis=1), RG - 1).astype(i32)
    gg = jgc - (gst_end - nst)[tq]
    g_slots = jnp.where(g_ok, jnp.clip(nblk_g[tq] - GG * gg, 0, GG), 0)
    g_slot_ids = jnp.arange(GG, dtype=i32)[None, :]
    g_parts = _hold_unused((first_g[tq] + GG * gg)[:, None] + g_slot_ids, g_slot_ids < g_slots[:, None])
    g_first = jnp.logical_and(g_ok, gg == 0)
    gather_plan = (tq, g_parts.reshape(-1).astype(i32), g_slots.astype(i32), g_first.astype(i32))

    order = jnp.argsort(jnp.where(valid, s_of * RG + q_of, jnp.iinfo(jnp.int32).max))
    s2, q2 = s_of[order], q_of[order]
    CG = MOE_CG
    blocks = jnp.arange(NB, dtype=i32)
    per_blk = jnp.sum(jnp.logical_and(valid[None, :], s2[None, :] == blocks[:, None]), axis=1)
    pb_end = jnp.cumsum(per_blk)
    pb_start = pb_end - per_blk
    nsteps = (per_blk + CG - 1) // CG
    st_end = jnp.cumsum(nsteps)
    total = st_end[-1]
    SMAX = (PMAX + CG - 1) // CG + NB
    j = jnp.arange(SMAX, dtype=i32)
    step_ok = j < total
    jc = jnp.minimum(j, total - 1)
    blk = jnp.minimum(jnp.sum(st_end[None, :] <= jc[:, None], axis=1), NB - 1).astype(i32)
    grp = jc - (st_end - nsteps)[blk]
    slot_p = pb_start[blk][:, None] + CG * grp[:, None] + jnp.arange(CG, dtype=i32)[None, :]
    slot_ok = jnp.logical_and(slot_p < pb_end[blk][:, None], step_ok[:, None])
    c_slots = jnp.sum(slot_ok, axis=1).astype(i32)
    slot_q = _hold_unused(q2[jnp.minimum(slot_p, npairs - 1)], slot_ok)
    c_first = jnp.logical_and(step_ok, grp == 0).astype(i32)
    c_last = jnp.logical_and(step_ok, grp == nsteps[blk] - 1).astype(i32)
    combine_plan = (blk, slot_q.reshape(-1).astype(i32), c_slots, c_first, c_last)

    te, tlo = region(jnp.arange(RT, dtype=i32) * TR)
    tvalid = jnp.clip(cnt[te] - tlo, 0, TR)
    last_t = jnp.sum(tiles) - 1
    t_idx = jnp.where(tvalid > 0, jnp.arange(RT, dtype=i32), last_t).astype(i32)
    ffn_plan = (t_idx, te[t_idx].astype(i32), tvalid.astype(i32))
    return pos_cols, pos_rows, gather_plan, combine_plan, ffn_plan, rmax


def _moe_gather_kernel(q_ref, s_ref, slots_ref, first_ref, *refs):
    pos_refs, h_refs, out_ref = refs[:MOE_GG], refs[MOE_GG:2 * MOE_GG], refs[2 * MOE_GG]
    p = pl.program_id(0)
    rows = out_ref.shape[0]

    @pl.when(first_ref[p] == 1)
    def _():
        out_ref[...] = jnp.zeros_like(out_ref)

    for ns in range(1, MOE_GG + 1):
        @pl.when(slots_ref[p] == ns)
        def _(ns=ns):
            row = (lax.broadcasted_iota(jnp.int32, (rows, 1), 0) + q_ref[p] * rows).astype(F32)
            sels = []
            for k in range(ns):
                hit = jnp.logical_or(pos_refs[k][0:1, :] == row, pos_refs[k][1:2, :] == row)
                sels.append(jnp.where(hit, 1.0, 0.0).astype(BF16))
            sel = sels[0] if ns == 1 else jnp.concatenate(sels, axis=1)
            hs = h_refs[0][...] if ns == 1 else jnp.concatenate([h_refs[k][...] for k in range(ns)], axis=0)
            out_ref[...] = out_ref[...] + _dot(sel, hs).astype(BF16)


def _moe_gather(h, pos_rows, plan, rmax):
    N, D = h.shape
    nsteps = plan[0].shape[0]

    def pos_spec(k):
        return pl.BlockSpec((SUB, MOE_SBG), lambda p, q, s, *_: (0, s[MOE_GG * p + k]))

    def tok_spec(k):
        return pl.BlockSpec((MOE_SBG, D), lambda p, q, s, *_: (s[MOE_GG * p + k], 0))

    return pl.pallas_call(
        _moe_gather_kernel,
        grid_spec=pltpu.PrefetchScalarGridSpec(
            num_scalar_prefetch=4, grid=(nsteps,),
            in_specs=[pos_spec(k) for k in range(MOE_GG)] + [tok_spec(k) for k in range(MOE_GG)],
            out_specs=pl.BlockSpec((MOE_TRG, D), lambda p, q, *_: (q[p], 0))),
        out_shape=jax.ShapeDtypeStruct((rmax, D), BF16),
        compiler_params=_params("arbitrary"),
        name="moe_gather",
    )(*plan, *([pos_rows] * MOE_GG), *([h] * MOE_GG))


def _moe_ffn_kernel(t_ref, e_ref, nv_ref, x_ref, w1_ref, w3_ref, w2_ref, out_ref, acc_scr, *, nf):
    t = pl.program_id(0)
    f = pl.program_id(1)
    nv = nv_ref[t]

    def block(start, size):
        rows = pl.ds(start, size)

        @pl.when(f == 0)
        def _():
            acc_scr[rows, :] = jnp.zeros((size, acc_scr.shape[1]), F32)

        x = x_ref[rows, :]
        a = _dot(x, w1_ref[...].astype(BF16))
        b = _dot(x, w3_ref[...].astype(BF16))
        acc_scr[rows, :] += _dot((_silu(a) * b).astype(BF16), w2_ref[...].astype(BF16))

        @pl.when(f == nf - 1)
        def _():
            out_ref[rows, :] = acc_scr[rows, :].astype(out_ref.dtype)

    nsub = MOE_TR // MOE_TRG
    used = (nv + MOE_TRG - 1) // MOE_TRG

    @pl.when(used == nsub)
    def _():
        block(0, MOE_TR)

    @pl.when(jnp.logical_and(used > 0, used < nsub))
    def _():
        start = jnp.int32(0)
        size = MOE_TR // 2
        while size >= MOE_TRG:
            has = (used & (size // MOE_TRG)) != 0

            @pl.when(has)
            def _(start=start, size=size):
                block(pl.multiple_of(start, MOE_TRG), size)

            start = start + jnp.where(has, size, 0)
            size //= 2


def _moe_ffn(xs, plan, w1, w3, w2):
    rmax, D = xs.shape
    FF = w1.shape[2]
    TF = 256
    nf = FF // TF
    RT = rmax // MOE_TR

    def fidx(t, f, nv):
        return jnp.where(nv[t] > 0, f, nf - 1)

    return pl.pallas_call(
        functools.partial(_moe_ffn_kernel, nf=nf),
        grid_spec=pltpu.PrefetchScalarGridSpec(
            num_scalar_prefetch=3, grid=(RT, nf),
            in_specs=[pl.BlockSpec((MOE_TR, D), lambda t, f, ti, e, nv: (ti[t], 0)),
                      pl.BlockSpec((None, D, TF), lambda t, f, ti, e, nv: (e[t], 0, fidx(t, f, nv))),
                      pl.BlockSpec((None, D, TF), lambda t, f, ti, e, nv: (e[t], 0, fidx(t, f, nv))),
                      pl.BlockSpec((None, TF, D), lambda t, f, ti, e, nv: (e[t], fidx(t, f, nv), 0))],
            out_specs=pl.BlockSpec((MOE_TR, D), lambda t, f, ti, e, nv: (ti[t], 0)),
            scratch_shapes=[pltpu.VMEM((MOE_TR, D), F32)]),
        out_shape=jax.ShapeDtypeStruct((rmax, D), BF16),
        compiler_params=_params("arbitrary", "arbitrary"),
        name="moe_ffn",
    )(*plan, xs, w1, w3, w2)


def _moe_combine_kernel(s_ref, q_ref, slots_ref, first_ref, last_ref, pos_ref, *refs, ntp):
    ys_refs = refs[:MOE_CG]
    xp_ref, xs_ref, gate_ref, fg_ref, op_ref, os_ref, acc_scr = refs[MOE_CG:]
    p = pl.program_id(0)
    rows = ys_refs[0].shape[0]

    @pl.when(first_ref[p] == 1)
    def _():
        acc_scr[...] = jnp.zeros_like(acc_scr)

    for ns in range(1, MOE_CG + 1):
        @pl.when(slots_ref[p] == ns)
        def _(ns=ns):
            sels = []
            for k in range(ns):
                col = (lax.broadcasted_iota(jnp.int32, (1, rows), 1) + q_ref[MOE_CG * p + k] * rows).astype(F32)
                sels.append((jnp.where(pos_ref[:, 0:1] == col, pos_ref[:, 2:3], 0.0)
                             + jnp.where(pos_ref[:, 1:2] == col, pos_ref[:, 3:4], 0.0)).astype(BF16))
            sel = sels[0] if ns == 1 else jnp.concatenate(sels, axis=1)
            ys = ys_refs[0][...] if ns == 1 else jnp.concatenate([ys_refs[k][...] for k in range(ns)], axis=0)
            acc_scr[...] += _dot(sel, ys)

    @pl.when(last_ref[p] == 1)
    def _():
        s = s_ref[p]
        x = jnp.where(s < ntp, xp_ref[...], xs_ref[...])
        y = x + gate_ref[...] * acc_scr[...]
        out = y * lax.rsqrt(jnp.mean(y * y, axis=-1, keepdims=True) + EPS) * fg_ref[...]

        @pl.when(s < ntp)
        def _():
            op_ref[...] = out

        @pl.when(s >= ntp)
        def _():
            os_ref[...] = out


def _moe_combine(ys, pos_cols, plan, xp, xs, mod, final_g, rows_per_group):
    Np, D = xp.shape
    Ns = xs.shape[0]
    SB = MOE_SB
    ntp = Np // SB
    nsteps = plan[0].shape[0]

    def tile_spec(k):
        return pl.BlockSpec((MOE_TRG, D), lambda p, s, q, *_: (q[MOE_CG * p + k], 0))

    def tok_p(p, s, *_):
        return (jnp.minimum(s[p], ntp - 1), 0)

    def tok_s(p, s, *_):
        return (jnp.maximum(s[p] - ntp, 0), 0)

    def gate_idx(p, s, *_):
        return (jnp.where(s[p] < ntp, 0, 1 + ((s[p] - ntp) * SB) // rows_per_group), 5, 0, 0)

    return pl.pallas_call(
        functools.partial(_moe_combine_kernel, ntp=ntp),
        grid_spec=pltpu.PrefetchScalarGridSpec(
            num_scalar_prefetch=5, grid=(nsteps,),
            in_specs=[pl.BlockSpec((SB, SUB), lambda p, s, q, *_: (s[p], 0))]
            + [tile_spec(k) for k in range(MOE_CG)]
            + [pl.BlockSpec((SB, D), tok_p),
                      pl.BlockSpec((SB, D), tok_s),
                      pl.BlockSpec((None, None, 1, D), gate_idx),
                      pl.BlockSpec((1, D), lambda p, *_: (0, 0))],
            out_specs=[pl.BlockSpec((SB, D), tok_p), pl.BlockSpec((SB, D), tok_s)],
            scratch_shapes=[pltpu.VMEM((SB, D), F32)]),
        out_shape=[jax.ShapeDtypeStruct((Np, D), F32), jax.ShapeDtypeStruct((Ns, D), F32)],
        compiler_params=_params("arbitrary"),
        name="moe_combine",
    )(*plan, pos_cols, *([ys] * MOE_CG), xp, xs, mod, final_g.reshape(1, D))


def _moe(xp, xs, g, mod, router_w, w1, w3, w2, final_g, rows_per_group):
    N = xp.shape[0] + xs.shape[0]
    h, info, info_t, cum = _moe_route(xp, xs, g, mod, router_w, rows_per_group)
    pos_cols, pos_rows, gather_plan, combine_plan, ffn_plan, rmax = _moe_plan(info, info_t, cum, N)
    x_sorted = _moe_gather(h, pos_rows, gather_plan, rmax)
    y_sorted = _moe_ffn(x_sorted, ffn_plan, w1, w3, w2)
    return _moe_combine(y_sorted, pos_cols, combine_plan, xp, xs, mod, final_g, rows_per_group)


def _gla_levels(C):
    lv, c = [], C // 2
    while c >= SUB:
        lv.append(c)
        c //= 2
    return lv


def _gla_tables(C):
    levels = _gla_levels(C)
    nr = 2 + 2 * len(levels)
    mat = np.zeros((2, nr * C, C), np.float32)
    code = np.zeros((2, C, C), np.int32)
    for d in range(2):
        p = np.arange(C) if d == 0 else C - 1 - np.arange(C)
        pi, pj = p[:, None], p[None, :]
        mat[d, 0:C] = pj <= pi
        mat[d, C:2 * C] = pj > pi
        code[d] = np.where((pj <= pi) & (pi // SUB == pj // SUB), 1, 0)
        for lv, c in enumerate(levels):
            blk = pi // c
            later = blk % 2 == 1
            mat[d, (2 + 2 * lv) * C:(3 + 2 * lv) * C] = later & (pj > blk * c - 1) & (pj <= pi)
            mat[d, (3 + 2 * lv) * C:(4 + 2 * lv) * C] = (~later) & (pj > pi) & (pj <= (blk + 1) * c - 1)
            pair = (pi // (2 * c) == pj // (2 * c)) & (pi // c != pj // c) & (pj <= pi)
            code[d] = np.where(pair, 2 + lv, code[d])
    ones = np.zeros((SUB * LANES, C), np.float32)
    for jj in range(SUB):
        ones[jj * LANES:(jj + 1) * LANES, jj::SUB] = 1.0
    return jnp.asarray(mat, dtype=BF16), jnp.asarray(code), jnp.asarray(ones, dtype=BF16)


def _bcast_sublane(x, jj):
    r, w = x.shape
    x3 = x.reshape(r // SUB, SUB, w)
    return jnp.broadcast_to(x3[:, jj:jj + 1, :], x3.shape).reshape(r, w)


def _t128(x):
    r, w = x.shape
    if w > LANES:
        return jnp.concatenate([x[:, i:i + LANES].T for i in range(0, w, LANES)], axis=0)
    return jnp.concatenate([x[i:i + LANES, :].T for i in range(0, r, LANES)], axis=1)


def _gla_kernel(q_ref, k_ref, v_ref, g_ref, lr_ref, wg_ref, ba_ref, mat_ref, code_ref, ones_ref,
                s0f_ref, s0b_ref, ng_ref, o_ref, sf_ref, sb_ref, st_scr, of_scr, *, n, C, G):
    d = pl.program_id(1)
    c = pl.program_id(2)
    levels = _gla_levels(C)

    @pl.when(jnp.logical_and(c == 0, d == 0))
    def _():
        for bb in range(G):
            for h in range(C_H):
                st_scr[bb, h] = _t128(s0f_ref[bb, h])

    @pl.when(jnp.logical_and(c == 0, d == 1))
    def _():
        for bb in range(G):
            for h in range(C_H):
                st_scr[bb, h] = _t128(s0b_ref[bb, h])

    mat = mat_ref[...]
    code = code_ref[...]
    ones = ones_ref[...]
    cums = []
    for bb in range(G):
        xg = jnp.dot(lr_ref[bb], wg_ref[...], precision=HIGHEST, preferred_element_type=F32) + ba_ref[...]
        la = (jnp.minimum(xg, 0.0) - jnp.log1p(jnp.exp(-jnp.abs(xg)))) * (LOG2E / C_TAU)
        hi = la.astype(BF16)
        lo = (la - hi.astype(F32)).astype(BF16)
        cums.append(_dot(mat, hi) + _dot(mat, lo))

    def prepare(bb, h):
        cum = cums[bb]
        ks = slice(h * C_DK, (h + 1) * C_DK)
        qh = q_ref[bb, :, ks].astype(F32) * (C_DK ** -0.5)
        kh = k_ref[bb, :, ks].astype(F32)
        b = cum[0:C, ks]
        b_rest = cum[C:2 * C, ks]
        ps = []
        for jj in range(SUB):
            dec = jnp.exp2(jnp.minimum(b - _bcast_sublane(b, jj), 0.0))
            ps.append((qh * _bcast_sublane(kh, jj) * dec).astype(BF16))
        lv_ops = []
        for lv in range(len(levels)):
            eq = cum[(2 + 2 * lv) * C:(3 + 2 * lv) * C, ks]
            ek = cum[(3 + 2 * lv) * C:(4 + 2 * lv) * C, ks]
            lv_ops.append(((qh * jnp.exp2(eq)).astype(BF16), (kh * jnp.exp2(ek)).astype(BF16)))
        qe = (qh * jnp.exp2(b)).astype(BF16)
        ke = (kh * jnp.exp2(b_rest)).astype(BF16)
        e_end = jnp.exp2(b[0:1, :] + b_rest[0:1, :])
        return jnp.concatenate(ps, axis=1), lv_ops, qe, ke, e_end

    def contract(bb, h, prep):
        pcat, lv_ops, qe, ke, e_end = prep
        vh = v_ref[bb, :, h * C_DV:(h + 1) * C_DV].astype(F32)
        att = jnp.where(code == 1, _dot(pcat, ones), 0.0)
        for lv, (qs, ks_) in enumerate(lv_ops):
            att = jnp.where(code == 2 + lv, _dot_nt(qs, ks_), att)
        st = st_scr[bb, h]
        o = _dot(att.astype(BF16), vh.astype(BF16)) + _dot_nt(qe, st.astype(BF16))
        st_scr[bb, h] = e_end * st + _dot(_t128(vh).astype(BF16), ke)
        return o

    units = [(bb, h) for h in range(C_H) for bb in range(G)]
    outs = {}
    prep = prepare(*units[0])
    for idx, (bb, h) in enumerate(units):
        nxt = prepare(*units[idx + 1]) if idx + 1 < len(units) else None
        outs[(bb, h)] = contract(bb, h, prep)
        prep = nxt
    o_all = [jnp.concatenate([outs[(bb, h)] for h in range(C_H)], axis=-1) for bb in range(G)]

    @pl.when(d == 0)
    def _():
        for bb in range(G):
            of_scr[bb, c] = o_all[bb]

    @pl.when(d == 1)
    def _():
        for bb in range(G):
            tot = o_all[bb] + of_scr[bb, n - 1 - c]
            res = []
            for h in range(C_H):
                sl = slice(h * C_DV, (h + 1) * C_DV)
                t = tot[:, sl]
                y = t * lax.rsqrt(jnp.mean(t * t, axis=-1, keepdims=True) + EPS) * ng_ref[:, sl]
                res.append(y * _silu(g_ref[bb, :, sl].astype(F32)))
            o_ref[bb] = jnp.concatenate(res, axis=-1).astype(o_ref.dtype)

    @pl.when(jnp.logical_and(c == n - 1, d == 0))
    def _():
        for bb in range(G):
            for h in range(C_H):
                sf_ref[bb, h] = _t128(st_scr[bb, h])

    @pl.when(jnp.logical_and(c == n - 1, d == 1))
    def _():
        for bb in range(G):
            for h in range(C_H):
                sb_ref[bb, h] = _t128(st_scr[bb, h])


def _gla(z, zg, B, T, w_a2, b_a, s0f, s0b, norm_g):
    C = GLA_CHUNK
    G = _rows_per_step(B, T, C_H * C_DV)
    assert B % G == 0 and T % C == 0
    n = T // C
    HK = C_H * C_DK
    HV = C_H * C_DV
    mat, code, ones = _gla_tables(C)
    nr = mat.shape[1] // C
    wg = jnp.zeros((2, LANES, HK), F32)
    for dr in range(2):
        wg = wg.at[dr, dr * C_RANK:(dr + 1) * C_RANK, :].set(w_a2[dr])
    z3 = z.reshape(B, T, z.shape[1])
    zg3 = zg.reshape(B, T, zg.shape[1])

    def chunk(d, c):
        return c + d * (n - 1 - 2 * c)

    st_spec = pl.BlockSpec((G, C_H, C_DK, C_DV), lambda b, d, c: (b, 0, 0, 0))
    st_shape = jax.ShapeDtypeStruct((B, C_H, C_DK, C_DV), F32)
    o, sf, sb = pl.pallas_call(
        functools.partial(_gla_kernel, n=n, C=C, G=G),
        grid=(B // G, 2, n),
        in_specs=[pl.BlockSpec((G, C, HK), lambda b, d, c: (b, chunk(d, c), 0)),
                  pl.BlockSpec((G, C, HK), lambda b, d, c: (b, chunk(d, c), 1)),
                  pl.BlockSpec((G, C, HV), lambda b, d, c: (b, chunk(d, c), 1)),
                  pl.BlockSpec((G, C, HV), lambda b, d, c: (b, chunk(d, c), 2)),
                  pl.BlockSpec((G, C, LANES), lambda b, d, c: (b, chunk(d, c), 0)),
                  pl.BlockSpec((None, LANES, HK), lambda b, d, c: (d, 0, 0)),
                  pl.BlockSpec((None, 1, HK), lambda b, d, c: (d, 0, 0)),
                  pl.BlockSpec((None, nr * C, C), lambda b, d, c: (d, 0, 0)),
                  pl.BlockSpec((None, C, C), lambda b, d, c: (d, 0, 0)),
                  pl.BlockSpec((SUB * LANES, C), lambda b, d, c: (0, 0)),
                  st_spec, st_spec,
                  pl.BlockSpec((1, HV), lambda b, d, c: (0, 0))],
        out_specs=[pl.BlockSpec((G, C, HV), lambda b, d, c: (b, (n - 1) - d * c, 0)),
                   st_spec, st_spec],
        out_shape=[jax.ShapeDtypeStruct((B, T, HV), BF16), st_shape, st_shape],
        scratch_shapes=[pltpu.VMEM((G, C_H, C_DV, C_DK), F32), pltpu.VMEM((G, n, C, HV), F32)],
        compiler_params=_params("arbitrary", "arbitrary", "arbitrary"),
        name="gla",
    )(z3, z3, z3, z3, zg3, wg, b_a.reshape(2, 1, HK), mat, code, ones, s0f, s0b, norm_g.reshape(1, HV))
    return o.reshape(B * T, HV), sf, sb


def _run_stream(x, B, T, mods, ctx, p):
    N, D = x.shape
    rpg = N // mods[0].shape[0]
    TM = min(2048, rpg)
    nb = (B_H + 2 * B_HKV) * B_HD

    w_in = p['even_w_in'][0]
    z, zb = _norm_mm(x, p['norm1_g'][0], mods[0], (0, 1), w_in, w_in, (nb, MIX_MAIN // nb), TM, rpg)
    if ctx is None:
        s0 = jnp.zeros((B, A_H, A_DK, A_DV), F32)
        a_f0, a_b0, cache_k, cache_v = s0, s0, None, None
    else:
        cache_k, cache_v, a_f0, a_b0 = ctx[0], ctx[1], ctx[2], ctx[3]
    o_a, a_sf, a_sb = _retention(z, B, T, p['a_log_gamma'][0], a_f0, a_b0, p['a_norm_g'][0])
    qpad, k_norm, k_rot, v_bf = _bprep(zb, T, p['b_q_g'][0], p['b_k_g'][0], rope=ctx is not None)
    o_b = _attention(qpad, k_rot, v_bf, B, T, cache_k, cache_v)
    x = _ffn(x, o_a, o_b, p['even_w_out'][0], p['norm2_g'][0], mods[0], p['ff_w1'][0], p['ff_w3'][0],
             p['ff_w2'][0], rpg)

    w_in = p['odd_w_in'][0]
    w_gate = jnp.pad(w_in[:, MIX_MAIN:], ((0, 0), (0, LANES - 2 * C_RANK)))
    z1, z1g = _norm_mm(x, p['norm1_g'][1], mods[1], (0, 1), w_in, w_gate, (LANES, 0), TM, rpg)
    if ctx is None:
        s0 = jnp.zeros((B, C_H, C_DK, C_DV), F32)
        c_f0, c_b0 = s0, s0
    else:
        c_f0, c_b0 = ctx[4], ctx[5]
    o_c, c_sf, c_sb = _gla(z1, z1g, B, T, p['c_w_a2'][0], p['c_b_a'][0], c_f0, c_b0, p['c_norm_g'][0])
    x = _proj_res(x, mods[1], 2, [o_c], p['odd_w_out'][0], rpg)
    v_raw = zb[:, (B_H + B_HKV) * B_HD:]
    return x, (k_norm, v_raw, a_sf, a_sb, c_sf, c_sb)


def kernel(x_prompt, x_sample, c, cache_b_k, cache_b_v, state_a_fwd, state_a_bwd, state_c_fwd, state_c_bwd,
           c_ctx, w_mod, b_mod, norm1_g, norm2_g, final_g, even_w_in, even_w_out, a_log_gamma, a_norm_g,
           b_q_g, b_k_g, odd_w_in, c_w_a2, c_b_a, c_norm_g, odd_w_out, ff_w1, ff_w3, ff_w2,
           router_w, moe_w1, moe_w3, moe_w2):
    Bp, Tp, D = x_prompt.shape
    Bs, Ts, _ = x_sample.shape
    L = w_mod.shape[0]
    assert L == 2 and even_w_in.shape[0] == 1 and odd_w_in.shape[0] == 1
    p = dict(norm1_g=norm1_g, norm2_g=norm2_g, final_g=final_g, even_w_in=even_w_in, even_w_out=even_w_out,
             a_log_gamma=a_log_gamma, a_norm_g=a_norm_g, b_q_g=b_q_g, b_k_g=b_k_g, odd_w_in=odd_w_in,
             c_w_a2=c_w_a2, c_b_a=c_b_a, c_norm_g=c_norm_g, odd_w_out=odd_w_out, ff_w1=ff_w1, ff_w3=ff_w3,
             ff_w2=ff_w2, router_w=router_w, moe_w1=moe_w1, moe_w3=moe_w3, moe_w2=moe_w2)

    rows = 8
    conds = jnp.concatenate([c_ctx[None, :], c, jnp.zeros((rows - 1 - Bs, D), F32)], axis=0)
    mod = _modulation(conds, w_mod, b_mod).reshape(L, rows, 6, 1, D)
    mods_p = [mod[l, 0:1] for l in range(L)]
    mods_s = [mod[l, 1:1 + Bs] for l in range(L)]

    x_p, kept = _run_stream(x_prompt.reshape(Bp * Tp, D), Bp, Tp, mods_p, None, p)
    nk = B_HKV * B_HD
    ctx = (cache_b_k[:, 0].reshape(Bs, -1, nk), cache_b_v[:, 0].reshape(Bs, -1, nk),
           state_a_fwd[:, 0], state_a_bwd[:, 0], state_c_fwd[:, 0], state_c_bwd[:, 0])
    x_s, _ = _run_stream(x_sample.reshape(Bs * Ts, D), Bs, Ts, mods_s, ctx, p)
    y_p, y_s = _moe(x_p, x_s, norm2_g[1], mod[1, 0:1 + Bs], router_w[0], moe_w1[0], moe_w3[0], moe_w2[0],
                    final_g, Ts)

    k_norm, v_raw, a_sf, a_sb, c_sf, c_sb = kept
    return (y_p.reshape(Bp, Tp, D), y_s.reshape(Bs, Ts, D),
            k_norm.reshape(Bp, 1, Tp, B_HKV, B_HD), v_raw.reshape(Bp, 1, Tp, B_HKV, B_HD),
            a_sf[:, None], a_sb[:, None], c_sf[:, None], c_sb[:, None])
```

```python
import functools

import numpy as np
import jax
import jax.numpy as jnp
from jax import lax
from jax.experimental import pallas as pl
from jax.experimental.pallas import tpu as pltpu

F32 = jnp.float32
BF16 = jnp.bfloat16
EPS = 1e-6
LOG2E = 1.4426950408889634

VMEM_LIMIT_BYTES = 56 * 1024 * 1024

A_H, A_DK, A_DV = 4, 128, 256
B_H, B_HKV, B_HD = 8, 2, 64
C_H, C_DK, C_DV, C_RANK = 4, 128, 256, 16
C_TAU = 16.0
GRID_W = 64
ROPE_THETA = 10000.0
N_EXPERTS = 8
LANES = 128
SUB = 8
RET_CHUNK = 128
GLA_CHUNK = 128
Q_TILE = 128
SCAN_ROWS_MAX = 4
SCAN_FWD_BYTES = 32 * 1024 * 1024


def _rows_per_step(B, T, width):
    g = SCAN_ROWS_MAX
    while g > 1 and (B % g or g * T * width * 4 > SCAN_FWD_BYTES):
        g //= 2
    return g


def _params(*sem):
    return pltpu.CompilerParams(dimension_semantics=sem, vmem_limit_bytes=VMEM_LIMIT_BYTES)


def _dot(a, b):
    return jnp.dot(a, b, preferred_element_type=F32)


def _dot_nt(a, b):
    return lax.dot_general(a, b, (((1,), (1,)), ((), ())), preferred_element_type=F32)


def _split(x):
    hi = x.astype(BF16)
    return hi, (x - hi.astype(F32)).astype(BF16)


def _dot3(a, b):
    return _dot(a[0], b[0]) + (_dot(a[0], b[1]) + _dot(a[1], b[0]))


def _silu(x):
    return x * jax.nn.sigmoid(x)


def _norm_mod(x, g, sh, sc):
    r = lax.rsqrt(jnp.mean(x * x, axis=-1, keepdims=True) + EPS)
    return (x * r * g) * (1.0 + sc) + sh


def _mod_kernel(c_ref, w_ref, b_ref, o_ref):
    o_ref[...] = _dot3(_split(_silu(c_ref[...])), _split(w_ref[...])) + b_ref[...]


def _modulation(conds, w_mod, b_mod):
    L, D, D6 = w_mod.shape
    R = conds.shape[0]
    TN = 1024
    return pl.pallas_call(
        _mod_kernel,
        grid=(L, D6 // TN),
        in_specs=[pl.BlockSpec((R, D), lambda l, j: (0, 0)),
                  pl.BlockSpec((None, D, TN), lambda l, j: (l, 0, j)),
                  pl.BlockSpec((None, 1, TN), lambda l, j: (l, 0, j))],
        out_specs=pl.BlockSpec((None, R, TN), lambda l, j: (l, 0, j)),
        out_shape=jax.ShapeDtypeStruct((L, R, D6), F32),
        compiler_params=_params("arbitrary", "arbitrary"),
        name="modulation",
    )(conds, w_mod, b_mod.reshape(L, 1, D6))


def _mod_spec(part, D, TM, rows_per_group, axis):
    def idx(*g):
        return ((g[axis] * TM) // rows_per_group, part, 0, 0)
    return pl.BlockSpec((None, None, 1, D), idx)


MIX_MAIN = A_H * (2 * A_DK + 2 * A_DV)
MIX_TN = 768


def _norm_mm_kernel(x_ref, g_ref, sh_ref, sc_ref, w_ref, we_ref, o_ref, oe_ref, h_scr, *, nmain):
    j = pl.program_id(1)

    @pl.when(j == 0)
    def _():
        h_scr[...] = _norm_mod(x_ref[...], g_ref[...], sh_ref[...], sc_ref[...]).astype(BF16)

    @pl.when(j < nmain)
    def _():
        o_ref[...] = _dot(h_scr[...], w_ref[...].astype(BF16)).astype(o_ref.dtype)

    @pl.when(j == nmain)
    def _():
        oe_ref[...] = _dot(h_scr[...], we_ref[...].astype(BF16))


def _norm_mm(x, g, mod, parts, w, w_extra, extra_block, TM, rows_per_group):
    N, D = x.shape
    nmain = MIX_MAIN // MIX_TN
    WE = extra_block[0]
    return pl.pallas_call(
        functools.partial(_norm_mm_kernel, nmain=nmain),
        grid=(N // TM, nmain + 1),
        in_specs=[pl.BlockSpec((TM, D), lambda i, j: (i, 0)),
                  pl.BlockSpec((1, D), lambda i, j: (0, 0)),
                  _mod_spec(parts[0], D, TM, rows_per_group, 0),
                  _mod_spec(parts[1], D, TM, rows_per_group, 0),
                  pl.BlockSpec((D, MIX_TN), lambda i, j: (0, jnp.minimum(j, nmain - 1))),
                  pl.BlockSpec((D, WE), lambda i, j: (0, extra_block[1]))],
        out_specs=[pl.BlockSpec((TM, MIX_TN), lambda i, j: (i, jnp.minimum(j, nmain - 1))),
                   pl.BlockSpec((TM, WE), lambda i, j: (i, 0))],
        out_shape=[jax.ShapeDtypeStruct((N, MIX_MAIN), BF16), jax.ShapeDtypeStruct((N, WE), F32)],
        scratch_shapes=[pltpu.VMEM((TM, D), BF16)],
        compiler_params=_params("arbitrary", "arbitrary"),
        name="norm_mm",
    )(x, g.reshape(1, D), mod, mod, w, w_extra)


def _ret_kernel(lg_ref, q_ref, k_ref, v_ref, ag_ref, s0f_ref, s0b_ref, ng_ref,
                o_ref, sf_ref, sb_ref, s_scr, of_scr, *, n, C, G):
    d = pl.program_id(1)
    c = pl.program_id(2)

    @pl.when(jnp.logical_and(c == 0, d == 0))
    def _():
        s_scr[...] = s0f_ref[...]

    @pl.when(jnp.logical_and(c == 0, d == 1))
    def _():
        s_scr[...] = s0b_ref[...]

    df = d.astype(F32)
    sgn = 1.0 - 2.0 * df
    ii = lax.broadcasted_iota(jnp.int32, (C, C), 0).astype(F32)
    jj = lax.broadcasted_iota(jnp.int32, (C, C), 1).astype(F32)
    dd = (ii - jj) * sgn
    feeds = dd >= 0.0
    ddc = jnp.maximum(dd, 0.0)
    ri = lax.broadcasted_iota(jnp.int32, (C, 1), 0).astype(F32)
    pos_q = (ri + 1.0) + df * (C - 2.0 * ri - 1.0)
    pos_k = (C - 1.0 - ri) + df * (2.0 * ri - C + 1.0)
    chunk_len = jnp.full((1, A_DV), float(C), F32)

    outs = [[] for _ in range(G)]
    for h in range(A_H):
        lg = lg_ref[d, h]
        dmask = jnp.where(feeds, jnp.exp2(lg * ddc), 0.0)
        q_dec = jnp.exp2(lg * pos_q)
        k_dec = jnp.exp2(lg * pos_k)
        c_dec = jnp.exp2(lg * chunk_len)
        for bb in range(G):
            qh = q_ref[bb, :, h * A_DK:(h + 1) * A_DK].astype(F32) * (A_DK ** -0.5)
            kh = k_ref[bb, :, h * A_DK:(h + 1) * A_DK].astype(F32)
            vh = v_ref[bb, :, h * A_DV:(h + 1) * A_DV].astype(BF16)
            s = s_scr[bb, h]
            att = _dot_nt(qh.astype(BF16), kh.astype(BF16)) * dmask
            o = _dot(att.astype(BF16), vh) + _dot((qh * q_dec).astype(BF16), s.astype(BF16))
            kd = kh * k_dec
            s_scr[bb, h] = c_dec * s + _dot(kd.T.astype(BF16), vh)
            outs[bb].append(o)
    o_all = [jnp.concatenate(o, axis=-1) for o in outs]

    @pl.when(d == 0)
    def _():
        for bb in range(G):
            of_scr[bb, c] = o_all[bb]

    @pl.when(d == 1)
    def _():
        for bb in range(G):
            tot = o_all[bb] + of_scr[bb, n - 1 - c]
            res = []
            for h in range(A_H):
                sl = slice(h * A_DV, (h + 1) * A_DV)
                t = tot[:, sl]
                dev = t - jnp.mean(t, axis=-1, keepdims=True)
                y = dev * lax.rsqrt(jnp.mean(dev * dev, axis=-1, keepdims=True) + EPS) * ng_ref[:, sl]
                res.append(y * _silu(ag_ref[bb, :, sl].astype(F32)))
            o_ref[bb] = jnp.concatenate(res, axis=-1).astype(o_ref.dtype)

    @pl.when(jnp.logical_and(c == n - 1, d == 0))
    def _():
        sf_ref[...] = s_scr[...]

    @pl.when(jnp.logical_and(c == n - 1, d == 1))
    def _():
        sb_ref[...] = s_scr[...]


def _retention(z, B, T, log_gamma, s0f, s0b, norm_g):
    C = RET_CHUNK
    G = _rows_per_step(B, T, A_H * A_DV)
    assert B % G == 0 and T % C == 0
    n = T // C
    HK = A_H * A_DK
    HV = A_H * A_DV
    z3 = z.reshape(B, T, z.shape[1])

    def chunk(d, c):
        return c + d * (n - 1 - 2 * c)

    st_spec = pl.BlockSpec((G, A_H, A_DK, A_DV), lambda b, d, c: (b, 0, 0, 0))
    st_shape = jax.ShapeDtypeStruct((B, A_H, A_DK, A_DV), F32)
    o, sf, sb = pl.pallas_call(
        functools.partial(_ret_kernel, n=n, C=C, G=G),
        grid=(B // G, 2, n),
        in_specs=[pl.BlockSpec(memory_space=pltpu.SMEM),
                  pl.BlockSpec((G, C, HK), lambda b, d, c: (b, chunk(d, c), 0)),
                  pl.BlockSpec((G, C, HK), lambda b, d, c: (b, chunk(d, c), 1)),
                  pl.BlockSpec((G, C, HV), lambda b, d, c: (b, chunk(d, c), 1)),
                  pl.BlockSpec((G, C, HV), lambda b, d, c: (b, chunk(d, c), 2)),
                  st_spec, st_spec,
                  pl.BlockSpec((1, HV), lambda b, d, c: (0, 0))],
        out_specs=[pl.BlockSpec((G, C, HV), lambda b, d, c: (b, (n - 1) - d * c, 0)),
                   st_spec, st_spec],
        out_shape=[jax.ShapeDtypeStruct((B, T, HV), BF16), st_shape, st_shape],
        scratch_shapes=[pltpu.VMEM((G, A_H, A_DK, A_DV), F32), pltpu.VMEM((G, n, C, HV), F32)],
        compiler_params=_params("arbitrary", "arbitrary", "arbitrary"),
        name="retention",
    )(log_gamma * LOG2E, z3, z3, z3, z3, s0f, s0b, norm_g.reshape(1, HV))
    return o.reshape(B * T, HV), sf, sb


def _group_sum_matrix(width, group):
    i = np.arange(width)
    return jnp.asarray((i[:, None] // group == i[None, :] // group).astype(np.float32), dtype=BF16)


def _q_pad_matrix():
    m = np.zeros((B_H * B_HD, B_H * LANES), np.float32)
    g = B_H // B_HKV
    for h in range(B_H):
        for t in range(B_HD):
            m[h * B_HD + t, h * LANES + (h // g) * B_HD + t] = 1.0
    return jnp.asarray(m, dtype=BF16)


def _rope_tables(T):
    rows = T // GRID_W
    row = np.repeat(np.arange(rows, dtype=np.float64), GRID_W)
    col = np.tile(np.arange(GRID_W, dtype=np.float64), rows)
    nq = B_HD // 4
    inv = ROPE_THETA ** (-np.arange(nq, dtype=np.float64) / nq)
    ang = np.concatenate([row[:, None] * inv, col[:, None] * inv], axis=-1)
    cos = np.repeat(np.cos(ang), 2, axis=-1)
    sin = np.repeat(np.sin(ang), 2, axis=-1)
    sign = np.tile(np.array([-1.0, 1.0]), B_HD // 2)
    reps = LANES // B_HD
    return (jnp.asarray(np.tile(cos, (1, reps)), dtype=F32),
            jnp.asarray(np.tile(sin * sign, (1, reps)), dtype=F32))


def _group_rmsnorm(x, gsum, g):
    hi, lo = _split(x * x)
    ss = _dot(hi, gsum) + _dot(lo, gsum)
    return x * lax.rsqrt(ss * (1.0 / B_HD) + EPS) * g


def _rotate_pairs(x, cos, sin_signed):
    n = x.shape[1]
    lane = lax.broadcasted_iota(jnp.int32, x.shape, 1)
    partner = jnp.where(lane % 2 == 0, pltpu.roll(x, n - 1, 1), pltpu.roll(x, 1, 1))
    reps = n // LANES
    if reps > 1:
        cos = jnp.concatenate([cos] * reps, axis=1)
        sin_signed = jnp.concatenate([sin_signed] * reps, axis=1)
    return x * cos + partner * sin_signed


def _bprep_kernel(z_ref, qg_ref, kg_ref, cos_ref, sin_ref, gq_ref, gk_ref, pad_ref,
                  qpad_ref, kn_ref, kr_ref, vb_ref, *, rope):
    nq = B_H * B_HD
    nk = B_HKV * B_HD
    qn = _group_rmsnorm(z_ref[:, 0:nq], gq_ref[...], qg_ref[...])
    kn = _group_rmsnorm(z_ref[:, nq:nq + nk], gk_ref[...], kg_ref[...])
    kn_ref[...] = kn
    if rope:
        qn = _rotate_pairs(qn, cos_ref[...], sin_ref[...])
        kn = _rotate_pairs(kn, cos_ref[...], sin_ref[...])
    kr_ref[...] = kn.astype(BF16)
    vb_ref[...] = z_ref[:, nq + nk:nq + 2 * nk].astype(BF16)
    qs = (qn * (B_HD ** -0.5 * LOG2E)).astype(BF16)
    qpad_ref[...] = _dot(qs, pad_ref[...]).astype(BF16)


def _bprep(z, T, q_g, k_g, rope):
    N = z.shape[0]
    TM = min(512, T)
    nq = B_H * B_HD
    nk = B_HKV * B_HD
    width = nq + 2 * nk
    assert z.shape[1] == width
    cos, sin = _rope_tables(T if rope else TM)
    nt = T // TM if rope else 1
    const = lambda i: (0, 0)
    return pl.pallas_call(
        functools.partial(_bprep_kernel, rope=rope),
        grid=(N // TM,),
        in_specs=[pl.BlockSpec((TM, width), lambda i: (i, 0)),
                  pl.BlockSpec((1, nq), const),
                  pl.BlockSpec((1, nk), const),
                  pl.BlockSpec((TM, LANES), lambda i: (i % nt, 0)),
                  pl.BlockSpec((TM, LANES), lambda i: (i % nt, 0)),
                  pl.BlockSpec((nq, nq), const),
                  pl.BlockSpec((nk, nk), const),
                  pl.BlockSpec((nq, B_H * LANES), const)],
        out_specs=[pl.BlockSpec((TM, B_H * LANES), lambda i: (i, 0)),
                   pl.BlockSpec((TM, nk), lambda i: (i, 0)),
                   pl.BlockSpec((TM, nk), lambda i: (i, 0)),
                   pl.BlockSpec((TM, nk), lambda i: (i, 0))],
        out_shape=[jax.ShapeDtypeStruct((N, B_H * LANES), BF16),
                   jax.ShapeDtypeStruct((N, nk), F32),
                   jax.ShapeDtypeStruct((N, nk), BF16),
                   jax.ShapeDtypeStruct((N, nk), BF16)],
        compiler_params=_params("arbitrary"),
        name="attn_prep",
    )(z, jnp.tile(q_g, B_H).reshape(1, nq), jnp.tile(k_g, B_HKV).reshape(1, nk), cos, sin,
      _group_sum_matrix(nq, B_HD), _group_sum_matrix(nk, B_HD), _q_pad_matrix())


def _lane_fold(x, op):
    acc = x[:, 0:LANES]
    for j in range(1, x.shape[1] // LANES):
        acc = op(acc, x[:, j * LANES:(j + 1) * LANES])
    return acc


def _attn_kernel(*refs, has_cache, kc, nq):
    if has_cache:
        q_ref, k_ref, v_ref, ck_ref, cv_ref, o_ref, s_scr, m_scr, mprev_scr, l_scr, acc_scr = refs
        kcc = min(kc, ck_ref.shape[0])
        ncache = ck_ref.shape[0] // kcc
    else:
        q_ref, k_ref, v_ref, o_ref, s_scr, m_scr, mprev_scr, l_scr, acc_scr = refs
        kcc, ncache = kc, 0
    i = pl.program_id(1)
    tq = q_ref.shape[0]
    nlat = k_ref.shape[0] // kc

    def score(c, kblk):
        q = jnp.concatenate([q_ref[:, h * LANES:(h + 1) * LANES] for h in range(B_H)], axis=0)
        s = _dot_nt(q, kblk)
        s_scr[c, :, 0:kblk.shape[0]] = s
        m_scr[...] = jnp.maximum(m_scr[...], _lane_fold(s, jnp.maximum))

    def weight(c, vblk):
        s = s_scr[c, :, 0:vblk.shape[0]]
        mp = mprev_scr[...]
        ps = [jnp.exp2(s[:, j * LANES:(j + 1) * LANES] - mp) for j in range(vblk.shape[0] // LANES)]
        tot = ps[0]
        for pj in ps[1:]:
            tot = tot + pj
        l_scr[...] += tot
        acc_scr[...] += _dot(jnp.concatenate(ps, axis=1).astype(BF16), vblk)

    def run(do_weight, do_score):
        def unit(c, kblk, vblk):
            if do_weight:
                weight(c, vblk())
            if do_score:
                score(c, kblk())

        for c in range(ncache):
            unit(c, lambda: ck_ref[c * kcc:(c + 1) * kcc, :].astype(BF16),
                 lambda: cv_ref[c * kcc:(c + 1) * kcc, :].astype(BF16))

        def body(c, carry):
            rows = pl.ds(pl.multiple_of(c * kc, kc), kc)
            unit(ncache + c, lambda: k_ref[rows, :], lambda: v_ref[rows, :])
            return carry
        lax.fori_loop(0, nlat, body, 0)

    @pl.when(i < nq)
    def _():
        m_scr[...] = jnp.full(m_scr.shape, -jnp.inf, F32)

    @pl.when(i > 0)
    def _():
        l_scr[...] = jnp.zeros_like(l_scr)
        acc_scr[...] = jnp.zeros_like(acc_scr)

    @pl.when(i == 0)
    def _():
        run(False, True)

    @pl.when(jnp.logical_and(i > 0, i < nq))
    def _():
        run(True, True)

    @pl.when(i == nq)
    def _():
        run(True, False)

    @pl.when(i > 0)
    def _():
        r_all = acc_scr[...] / jnp.sum(l_scr[...], axis=-1, keepdims=True)
        g = B_H // B_HKV
        lane = lax.broadcasted_iota(jnp.int32, (tq, LANES), 1)
        outs = []
        for j in range(B_H // 2):
            pair = []
            for half in range(2):
                h = 2 * j + half
                r = r_all[h * tq:(h + 1) * tq, :]
                if h // g != half:
                    r = pltpu.roll(r, B_HD, 1)
                pair.append(r)
            outs.append(jnp.where(lane < B_HD, pair[0], pair[1]))
        o_ref[...] = jnp.concatenate(outs, axis=-1).astype(o_ref.dtype)

    @pl.when(i < nq)
    def _():
        mprev_scr[...] = jnp.broadcast_to(jnp.max(m_scr[...], axis=-1, keepdims=True), mprev_scr.shape)


def _attention(qpad, kr, vb, B, T, cache_k, cache_v):
    has_cache = cache_k is not None
    TQ = Q_TILE
    nq = T // TQ
    nk = B_HKV * B_HD
    in_specs = [pl.BlockSpec((TQ, B_H * LANES), lambda b, i: (b * nq + jnp.minimum(i, nq - 1), 0)),
                pl.BlockSpec((T, nk), lambda b, i: (b, 0)),
                pl.BlockSpec((T, nk), lambda b, i: (b, 0))]
    args = [qpad, kr, vb]
    kc = min(1024, T)
    nchunks = T // kc
    if has_cache:
        P = cache_k.shape[1]
        assert P % min(kc, P) == 0
        nchunks += P // min(kc, P)
        in_specs += [pl.BlockSpec((None, P, nk), lambda b, i: (b, 0, 0))] * 2
        args += [cache_k, cache_v]
    R = B_H * TQ
    return pl.pallas_call(
        functools.partial(_attn_kernel, has_cache=has_cache, kc=kc, nq=nq),
        grid=(B, nq + 1),
        in_specs=in_specs,
        out_specs=pl.BlockSpec((TQ, B_H * B_HD), lambda b, i: (b * nq + jnp.maximum(i - 1, 0), 0)),
        out_shape=jax.ShapeDtypeStruct((B * T, B_H * B_HD), BF16),
        scratch_shapes=[pltpu.VMEM((nchunks, R, kc), F32)] + [pltpu.VMEM((R, LANES), F32)] * 4,
        compiler_params=_params("arbitrary", "arbitrary"),
        name="attention",
    )(*args)


def _proj_res_kernel(*refs, n_in):
    x_ref, gate_ref = refs[0], refs[1]
    o_refs = refs[2:2 + n_in]
    w_refs = refs[2 + n_in:2 + 2 * n_in]
    out_ref = refs[2 + 2 * n_in]
    wbf_refs = refs[3 + 2 * n_in:]

    @pl.when(pl.program_id(0) == 0)
    def _():
        for w_ref, wbf_ref in zip(w_refs, wbf_refs):
            wbf_ref[...] = w_ref[...].astype(BF16)

    acc = _dot(o_refs[0][...], wbf_refs[0][...])
    for o_ref, wbf_ref in zip(o_refs[1:], wbf_refs[1:]):
        acc = acc + _dot(o_ref[...], wbf_ref[...])
    out_ref[...] = x_ref[...] + gate_ref[...] * acc


def _proj_res(x, mod, part, acts, w, rows_per_group):
    N, D = x.shape
    TM = min(1024, rows_per_group)
    n_in = len(acts)
    widths = [a.shape[1] for a in acts]
    offs = np.cumsum([0] + widths[:-1]).tolist()
    in_specs = [pl.BlockSpec((TM, D), lambda i: (i, 0)),
                _mod_spec(part, D, TM, rows_per_group, 0)]
    in_specs += [pl.BlockSpec((TM, wd), lambda i: (i, 0)) for wd in widths]
    in_specs += [pl.BlockSpec((wd, D), functools.partial(lambda i, blk: (blk, 0), blk=off // wd))
                 for wd, off in zip(widths, offs)]
    return pl.pallas_call(
        functools.partial(_proj_res_kernel, n_in=n_in),
        grid=(N // TM,),
        in_specs=in_specs,
        out_specs=pl.BlockSpec((TM, D), lambda i: (i, 0)),
        out_shape=jax.ShapeDtypeStruct((N, D), F32),
        scratch_shapes=[pltpu.VMEM((wd, D), BF16) for wd in widths],
        compiler_params=_params("arbitrary"),
        name="proj_residual",
    )(x, mod, *acts, *([w] * n_in))


def _ffn_kernel(x_ref, g_ref, sh_ref, sc_ref, gate_ref, w1_ref, w3_ref, w2_ref, out_ref, h_scr, acc_scr, *, nf):
    f = pl.program_id(1)

    @pl.when(f == 0)
    def _():
        h_scr[...] = _norm_mod(x_ref[...], g_ref[...], sh_ref[...], sc_ref[...]).astype(BF16)
        acc_scr[...] = jnp.zeros_like(acc_scr)

    h = h_scr[...]
    a = _dot(h, w1_ref[...].astype(BF16))
    b = _dot(h, w3_ref[...].astype(BF16))
    acc_scr[...] += _dot((_silu(a) * b).astype(BF16), w2_ref[...].astype(BF16))

    @pl.when(f == nf - 1)
    def _():
        out_ref[...] = x_ref[...] + gate_ref[...] * acc_scr[...]


def _ffn(x, g, mod, w1, w3, w2, rows_per_group):
    N, D = x.shape
    FF = w1.shape[1]
    TM, TF = min(1024, rows_per_group), 256
    nf = FF // TF
    return pl.pallas_call(
        functools.partial(_ffn_kernel, nf=nf),
        grid=(N // TM, nf),
        in_specs=[pl.BlockSpec((TM, D), lambda i, f: (i, 0)),
                  pl.BlockSpec((1, D), lambda i, f: (0, 0)),
                  _mod_spec(3, D, TM, rows_per_group, 0),
                  _mod_spec(4, D, TM, rows_per_group, 0),
                  _mod_spec(5, D, TM, rows_per_group, 0),
                  pl.BlockSpec((D, TF), lambda i, f: (0, f)),
                  pl.BlockSpec((D, TF), lambda i, f: (0, f)),
                  pl.BlockSpec((TF, D), lambda i, f: (f, 0))],
        out_specs=pl.BlockSpec((TM, D), lambda i, f: (i, 0)),
        out_shape=jax.ShapeDtypeStruct((N, D), F32),
        scratch_shapes=[pltpu.VMEM((TM, D), BF16), pltpu.VMEM((TM, D), F32)],
        compiler_params=_params("arbitrary", "arbitrary"),
        name="ffn",
    )(x, g.reshape(1, D), mod, mod, mod, w1, w3, w2)


MOE_SB = 1024
MOE_SBG = 512
MOE_GG = 4
MOE_TRG = 256
MOE_TR = 2048
MOE_CG = 4


def _two_stream_specs(shape, ntp, ax=0):
    def idx_p(*g):
        return (jnp.minimum(g[ax], ntp - 1), 0)

    def idx_s(*g):
        return (jnp.maximum(g[ax] - ntp, 0), 0)
    return pl.BlockSpec(shape, idx_p), pl.BlockSpec(shape, idx_s)


def _pool_mod_spec(part, D, TM, ntp, rows_per_group):
    def idx(i, *_):
        return (jnp.where(i < ntp, 0, 1 + ((i - ntp) * TM) // rows_per_group), part, 0, 0)
    return pl.BlockSpec((None, None, 1, D), idx)


def _route_kernel(xp_ref, xs_ref, g_ref, sh_ref, sc_ref, rw_ref, tri_ref, h_ref, info_ref, infot_ref, cum_ref,
                  carry_scr, *, ntp):
    i = pl.program_id(0)

    @pl.when(i == 0)
    def _():
        carry_scr[...] = jnp.zeros_like(carry_scr)

    x = jnp.where(i < ntp, xp_ref[...], xs_ref[...])
    h = _norm_mod(x, g_ref[...], sh_ref[...], sc_ref[...])
    h_ref[...] = h.astype(BF16)
    lane = lax.broadcasted_iota(jnp.int32, (x.shape[0], LANES), 1).astype(F32)
    logits = _dot3(_split(h), _split(rw_ref[...]))
    logits = jnp.where(lane < N_EXPERTS, logits, -jnp.inf)
    m1 = jnp.max(logits, axis=-1, keepdims=True)
    i1 = jnp.min(jnp.where(logits == m1, lane, float(LANES)), axis=-1, keepdims=True)
    rest = jnp.where(lane == i1, -jnp.inf, logits)
    m2 = jnp.max(rest, axis=-1, keepdims=True)
    i2 = jnp.min(jnp.where(rest == m2, lane, float(LANES)), axis=-1, keepdims=True)
    e2 = jnp.exp(m2 - m1)
    w1 = 1.0 / (1.0 + e2)
    w2 = e2 / (1.0 + e2)
    ind = jnp.where(jnp.logical_or(lane == i1, lane == i2), 1.0, 0.0)
    before = _dot(tri_ref[...], ind.astype(BF16)) + carry_scr[...]
    r1 = jnp.sum(jnp.where(lane == i1, before, 0.0), axis=-1, keepdims=True)
    r2 = jnp.sum(jnp.where(lane == i2, before, 0.0), axis=-1, keepdims=True)
    total = carry_scr[...] + jnp.sum(ind, axis=0, keepdims=True)
    carry_scr[...] = total
    for part in range(1, MOE_SB // MOE_SBG):
        cum_ref[part - 1] = before[part * MOE_SBG:part * MOE_SBG + 1, :]
    cum_ref[MOE_SB // MOE_SBG - 1] = total
    info = jnp.where(lane == 0.0, i1, jnp.where(lane == 1.0, i2, jnp.where(lane == 2.0, w1, jnp.where(
        lane == 3.0, w2, jnp.where(lane == 4.0, r1, jnp.where(lane == 5.0, r2, 0.0))))))
    info_ref[...] = info[:, 0:SUB]
    info_t = jnp.concatenate([info[r:r + LANES, :].T for r in range(0, info.shape[0], LANES)], axis=1)
    infot_ref[...] = info_t[0:SUB, :]


def _moe_route(xp, xs, g, mod, router_w, rows_per_group):
    Np, D = xp.shape
    N = Np + xs.shape[0]
    TM = MOE_SB
    ntp = Np // TM
    nt = N // TM
    rw = jnp.pad(router_w, ((0, 0), (0, LANES - router_w.shape[1])))
    tri = jnp.asarray(np.tril(np.ones((TM, TM), np.float32), -1), dtype=BF16)
    xp_spec, xs_spec = _two_stream_specs((TM, D), ntp)
    return pl.pallas_call(
        functools.partial(_route_kernel, ntp=ntp),
        grid=(nt,),
        in_specs=[xp_spec, xs_spec,
                  pl.BlockSpec((1, D), lambda i: (0, 0)),
                  _pool_mod_spec(3, D, TM, ntp, rows_per_group),
                  _pool_mod_spec(4, D, TM, ntp, rows_per_group),
                  pl.BlockSpec((D, LANES), lambda i: (0, 0)),
                  pl.BlockSpec((TM, TM), lambda i: (0, 0))],
        out_specs=[pl.BlockSpec((TM, D), lambda i: (i, 0)),
                   pl.BlockSpec((TM, SUB), lambda i: (i, 0)),
                   pl.BlockSpec((SUB, TM), lambda i: (0, i)),
                   pl.BlockSpec((MOE_SB // MOE_SBG, 1, LANES), lambda i: (i, 0, 0))],
        out_shape=[jax.ShapeDtypeStruct((N, D), BF16),
                   jax.ShapeDtypeStruct((N, SUB), F32),
                   jax.ShapeDtypeStruct((SUB, N), F32),
                   jax.ShapeDtypeStruct((nt * (MOE_SB // MOE_SBG), 1, LANES), F32)],
        scratch_shapes=[pltpu.VMEM((1, LANES), F32)],
        compiler_params=_params("arbitrary"),
        name="moe_route",
    )(xp, xs, g.reshape(1, D), mod, mod, rw, tri)


def _hold_unused(idx, used):
    steps, slots = idx.shape
    read = jnp.arange(slots, dtype=jnp.int32)[None, :] < used[:, None]
    step = jnp.arange(steps, dtype=jnp.int32)[:, None]
    last = lax.cummax(jnp.where(read, step, -1), axis=0)
    held = jnp.take_along_axis(idx, jnp.maximum(last, 0), axis=0)
    return jnp.where(last >= 0, held, 0)


def _moe_plan(info, info_t, cum, N):
    E, SB, TRG, TR = N_EXPERTS, MOE_SB, MOE_TRG, MOE_TR
    NB = N // SB
    rmax = 2 * N + E * TR
    RG, RT = rmax // TRG, rmax // TR
    PMAX = RG + E * NB
    i32 = jnp.int32
    parts = SB // MOE_SBG
    cum_g = cum[:, 0, :E].astype(i32).T
    cum_e = cum_g[:, parts - 1::parts]
    cnt = cum_e[:, -1]
    tiles = (cnt + TR - 1) // TR
    start = TR * (jnp.cumsum(tiles) - tiles)

    startf = start.astype(F32)

    def region_start(e):
        out = jnp.zeros_like(e)
        for k in range(E):
            out = jnp.where(e == float(k), startf[k], out)
        return out

    pos_cols = jnp.concatenate([region_start(info[:, 0:2]) + info[:, 4:6], info[:, 2:4],
                                jnp.zeros((N, 4), F32)], axis=1)
    pos_rows = jnp.concatenate([region_start(info_t[0:2]) + info_t[4:6], jnp.zeros((6, N), F32)],
                               axis=0)

    def region(row0):
        e = jnp.clip(jnp.sum(row0[:, None] >= start[None, :], axis=1) - 1, 0, E - 1)
        return e, row0 - start[e]

    eq, lo = region(jnp.arange(RG, dtype=i32) * TRG)
    hi = jnp.minimum(lo + TRG, cnt[eq])
    first = jnp.sum(cum_e[eq] <= lo[:, None], axis=1)
    last = jnp.sum(cum_e[eq] < hi[:, None], axis=1)
    nblk = jnp.where(hi > lo, last - first + 1, 0)
    pend = jnp.cumsum(nblk)
    npairs = pend[-1]
    p = jnp.arange(PMAX, dtype=i32)
    valid = p < npairs
    pc = jnp.minimum(p, npairs - 1)
    q_of = jnp.minimum(jnp.sum(pend[None, :] <= pc[:, None], axis=1), RG - 1).astype(i32)
    pstart = pend - nblk
    s_of = (first[q_of] + pc - pstart[q_of]).astype(i32)

    GG = MOE_GG
    first_g = jnp.sum(cum_g[eq] <= lo[:, None], axis=1)
    last_g = jnp.sum(cum_g[eq] < hi[:, None], axis=1)
    nblk_g = jnp.where(hi > lo, last_g - first_g + 1, 0)
    nst = (nblk_g + GG - 1) // GG
    gst_end = jnp.cumsum(nst)
    gtotal = gst_end[-1]
    smax_g = (RG + E * NB * parts + (GG - 1) * RG) // GG + 1
    jg = jnp.arange(smax_g, dtype=i32)
    g_ok = jg < gtotal
    jgc = jnp.minimum(jg, gtotal - 1)
    tq = jnp.minimum(jnp.sum(gst_end[None, :] <= jgc[:, None], axis=1), RG - 1).astype(i32)
    gg = jgc - (gst_end - nst)[tq]
    g_slots = jnp.where(g_ok, jnp.clip(nblk_g[tq] - GG * gg, 0, GG), 0)
    g_parts = _hold_unused((first_g[tq] + GG * gg)[:, None] + jnp.arange(GG, dtype=i32)[None, :], g_slots)
    g_first = jnp.logical_and(g_ok, gg == 0)
    gather_plan = (tq, g_parts.reshape(-1).astype(i32), g_slots.astype(i32), g_first.astype(i32))

    order = jnp.argsort(jnp.where(valid, s_of * RG + q_of, jnp.iinfo(jnp.int32).max))
    s2, q2 = s_of[order], q_of[order]
    CG = MOE_CG
    blocks = jnp.arange(NB, dtype=i32)
    per_blk = jnp.sum(jnp.logical_and(valid[None, :], s2[None, :] == blocks[:, None]), axis=1)
    pb_end = jnp.cumsum(per_blk)
    pb_start = pb_end - per_blk
    nsteps = (per_blk + CG - 1) // CG
    st_end = jnp.cumsum(nsteps)
    total = st_end[-1]
    SMAX = (PMAX + CG - 1) // CG + NB
    j = jnp.arange(SMAX, dtype=i32)
    step_ok = j < total
    jc = jnp.minimum(j, total - 1)
    blk = jnp.minimum(jnp.sum(st_end[None, :] <= jc[:, None], axis=1), NB - 1).astype(i32)
    grp = jc - (st_end - nsteps)[blk]
    slot_p = pb_start[blk][:, None] + CG * grp[:, None] + jnp.arange(CG, dtype=i32)[None, :]
    slot_ok = jnp.logical_and(slot_p < pb_end[blk][:, None], step_ok[:, None])
    c_slots = jnp.sum(slot_ok, axis=1).astype(i32)
    slot_q = _hold_unused(q2[jnp.minimum(slot_p, npairs - 1)], c_slots)
    c_first = jnp.logical_and(step_ok, grp == 0).astype(i32)
    c_last = jnp.logical_and(step_ok, grp == nsteps[blk] - 1).astype(i32)
    combine_plan = (blk, slot_q.reshape(-1).astype(i32), c_slots, c_first, c_last)

    te, tlo = region(jnp.arange(RT, dtype=i32) * TR)
    tvalid = jnp.clip(cnt[te] - tlo, 0, TR)
    last_t = jnp.sum(tiles) - 1
    t_idx = jnp.where(tvalid > 0, jnp.arange(RT, dtype=i32), last_t).astype(i32)
    ffn_plan = (t_idx, te[t_idx].astype(i32), tvalid.astype(i32))
    return pos_cols, pos_rows, gather_plan, combine_plan, ffn_plan, rmax


def _moe_gather_kernel(q_ref, s_ref, slots_ref, first_ref, *refs):
    pos_refs, h_refs, out_ref = refs[:MOE_GG], refs[MOE_GG:2 * MOE_GG], refs[2 * MOE_GG]
    p = pl.program_id(0)
    rows = out_ref.shape[0]

    @pl.when(first_ref[p] == 1)
    def _():
        out_ref[...] = jnp.zeros_like(out_ref)

    for ns in range(1, MOE_GG + 1):
        @pl.when(slots_ref[p] == ns)
        def _(ns=ns):
            row = (lax.broadcasted_iota(jnp.int32, (rows, 1), 0) + q_ref[p] * rows).astype(F32)
            sels = []
            for k in range(ns):
                hit = jnp.logical_or(pos_refs[k][0:1, :] == row, pos_refs[k][1:2, :] == row)
                sels.append(jnp.where(hit, 1.0, 0.0).astype(BF16))
            sel = sels[0] if ns == 1 else jnp.concatenate(sels, axis=1)
            hs = h_refs[0][...] if ns == 1 else jnp.concatenate([h_refs[k][...] for k in range(ns)], axis=0)
            out_ref[...] = out_ref[...] + _dot(sel, hs).astype(BF16)


def _moe_gather(h, pos_rows, plan, rmax):
    N, D = h.shape
    nsteps = plan[0].shape[0]

    def pos_spec(k):
        return pl.BlockSpec((SUB, MOE_SBG), lambda p, q, s, *_: (0, s[MOE_GG * p + k]))

    def tok_spec(k):
        return pl.BlockSpec((MOE_SBG, D), lambda p, q, s, *_: (s[MOE_GG * p + k], 0))

    return pl.pallas_call(
        _moe_gather_kernel,
        grid_spec=pltpu.PrefetchScalarGridSpec(
            num_scalar_prefetch=4, grid=(nsteps,),
            in_specs=[pos_spec(k) for k in range(MOE_GG)] + [tok_spec(k) for k in range(MOE_GG)],
            out_specs=pl.BlockSpec((MOE_TRG, D), lambda p, q, *_: (q[p], 0))),
        out_shape=jax.ShapeDtypeStruct((rmax, D), BF16),
        compiler_params=_params("arbitrary"),
        name="moe_gather",
    )(*plan, *([pos_rows] * MOE_GG), *([h] * MOE_GG))


def _moe_ffn_kernel(t_ref, e_ref, nv_ref, x_ref, w1_ref, w3_ref, w2_ref, out_ref, acc_scr, *, nf):
    t = pl.program_id(0)
    f = pl.program_id(1)
    nv = nv_ref[t]

    def block(start, size):
        rows = pl.ds(start, size)

        @pl.when(f == 0)
        def _():
            acc_scr[rows, :] = jnp.zeros((size, acc_scr.shape[1]), F32)

        x = x_ref[rows, :]
        a = _dot(x, w1_ref[...].astype(BF16))
        b = _dot(x, w3_ref[...].astype(BF16))
        acc_scr[rows, :] += _dot((_silu(a) * b).astype(BF16), w2_ref[...].astype(BF16))

        @pl.when(f == nf - 1)
        def _():
            out_ref[rows, :] = acc_scr[rows, :].astype(out_ref.dtype)

    nsub = MOE_TR // MOE_TRG
    used = (nv + MOE_TRG - 1) // MOE_TRG

    @pl.when(used == nsub)
    def _():
        block(0, MOE_TR)

    @pl.when(jnp.logical_and(used > 0, used < nsub))
    def _():
        start = jnp.int32(0)
        size = MOE_TR // 2
        while size >= MOE_TRG:
            has = (used & (size // MOE_TRG)) != 0

            @pl.when(has)
            def _(start=start, size=size):
                block(pl.multiple_of(start, MOE_TRG), size)

            start = start + jnp.where(has, size, 0)
            size //= 2


def _moe_ffn(xs, plan, w1, w3, w2):
    rmax, D = xs.shape
    FF = w1.shape[2]
    TF = 256
    nf = FF // TF
    RT = rmax // MOE_TR

    def fidx(t, f, nv):
        return jnp.where(nv[t] > 0, f, nf - 1)

    return pl.pallas_call(
        functools.partial(_moe_ffn_kernel, nf=nf),
        grid_spec=pltpu.PrefetchScalarGridSpec(
            num_scalar_prefetch=3, grid=(RT, nf),
            in_specs=[pl.BlockSpec((MOE_TR, D), lambda t, f, ti, e, nv: (ti[t], 0)),
                      pl.BlockSpec((None, D, TF), lambda t, f, ti, e, nv: (e[t], 0, fidx(t, f, nv))),
                      pl.BlockSpec((None, D, TF), lambda t, f, ti, e, nv: (e[t], 0, fidx(t, f, nv))),
                      pl.BlockSpec((None, TF, D), lambda t, f, ti, e, nv: (e[t], fidx(t, f, nv), 0))],
            out_specs=pl.BlockSpec((MOE_TR, D), lambda t, f, ti, e, nv: (ti[t], 0)),
            scratch_shapes=[pltpu.VMEM((MOE_TR, D), F32)]),
        out_shape=jax.ShapeDtypeStruct((rmax, D), BF16),
        compiler_params=_params("arbitrary", "arbitrary"),
        name="moe_ffn",
    )(*plan, xs, w1, w3, w2)


def _moe_combine_kernel(s_ref, q_ref, slots_ref, first_ref, last_ref, pos_ref, *refs, ntp):
    ys_refs = refs[:MOE_CG]
    xp_ref, xs_ref, gate_ref, fg_ref, op_ref, os_ref, acc_scr = refs[MOE_CG:]
    p = pl.program_id(0)
    rows = ys_refs[0].shape[0]

    @pl.when(first_ref[p] == 1)
    def _():
        acc_scr[...] = jnp.zeros_like(acc_scr)

    for ns in range(1, MOE_CG + 1):
        @pl.when(slots_ref[p] == ns)
        def _(ns=ns):
            sels = []
            for k in range(ns):
                col = (lax.broadcasted_iota(jnp.int32, (1, rows), 1) + q_ref[MOE_CG * p + k] * rows).astype(F32)
                sels.append((jnp.where(pos_ref[:, 0:1] == col, pos_ref[:, 2:3], 0.0)
                             + jnp.where(pos_ref[:, 1:2] == col, pos_ref[:, 3:4], 0.0)).astype(BF16))
            sel = sels[0] if ns == 1 else jnp.concatenate(sels, axis=1)
            ys = ys_refs[0][...] if ns == 1 else jnp.concatenate([ys_refs[k][...] for k in range(ns)], axis=0)
            acc_scr[...] += _dot(sel, ys)

    @pl.when(last_ref[p] == 1)
    def _():
        s = s_ref[p]
        x = jnp.where(s < ntp, xp_ref[...], xs_ref[...])
        y = x + gate_ref[...] * acc_scr[...]
        out = y * lax.rsqrt(jnp.mean(y * y, axis=-1, keepdims=True) + EPS) * fg_ref[...]

        @pl.when(s < ntp)
        def _():
            op_ref[...] = out

        @pl.when(s >= ntp)
        def _():
            os_ref[...] = out


def _moe_combine(ys, pos_cols, plan, xp, xs, mod, final_g, rows_per_group):
    Np, D = xp.shape
    Ns = xs.shape[0]
    SB = MOE_SB
    ntp = Np // SB
    nsteps = plan[0].shape[0]

    def tile_spec(k):
        return pl.BlockSpec((MOE_TRG, D), lambda p, s, q, *_: (q[MOE_CG * p + k], 0))

    def tok_p(p, s, *_):
        return (jnp.minimum(s[p], ntp - 1), 0)

    def tok_s(p, s, *_):
        return (jnp.maximum(s[p] - ntp, 0), 0)

    def gate_idx(p, s, *_):
        return (jnp.where(s[p] < ntp, 0, 1 + ((s[p] - ntp) * SB) // rows_per_group), 5, 0, 0)

    return pl.pallas_call(
        functools.partial(_moe_combine_kernel, ntp=ntp),
        grid_spec=pltpu.PrefetchScalarGridSpec(
            num_scalar_prefetch=5, grid=(nsteps,),
            in_specs=[pl.BlockSpec((SB, SUB), lambda p, s, q, *_: (s[p], 0))]
            + [tile_spec(k) for k in range(MOE_CG)]
            + [pl.BlockSpec((SB, D), tok_p),
                      pl.BlockSpec((SB, D), tok_s),
                      pl.BlockSpec((None, None, 1, D), gate_idx),
                      pl.BlockSpec((1, D), lambda p, *_: (0, 0))],
            out_specs=[pl.BlockSpec((SB, D), tok_p), pl.BlockSpec((SB, D), tok_s)],
            scratch_shapes=[pltpu.VMEM((SB, D), F32)]),
        out_shape=[jax.ShapeDtypeStruct((Np, D), F32), jax.ShapeDtypeStruct((Ns, D), F32)],
        compiler_params=_params("arbitrary"),
        name="moe_combine",
    )(*plan, pos_cols, *([ys] * MOE_CG), xp, xs, mod, final_g.reshape(1, D))


def _moe(xp, xs, g, mod, router_w, w1, w3, w2, final_g, rows_per_group):
    N = xp.shape[0] + xs.shape[0]
    h, info, info_t, cum = _moe_route(xp, xs, g, mod, router_w, rows_per_group)
    pos_cols, pos_rows, gather_plan, combine_plan, ffn_plan, rmax = _moe_plan(info, info_t, cum, N)
    x_sorted = _moe_gather(h, pos_rows, gather_plan, rmax)
    y_sorted = _moe_ffn(x_sorted, ffn_plan, w1, w3, w2)
    return _moe_combine(y_sorted, pos_cols, combine_plan, xp, xs, mod, final_g, rows_per_group)


def _gla_levels(C):
    lv, c = [], C // 2
    while c >= SUB:
        lv.append(c)
        c //= 2
    return lv


def _gla_tables(C):
    levels = _gla_levels(C)
    nr = 2 + 2 * len(levels)
    mat = np.zeros((2, nr * C, C), np.float32)
    code = np.zeros((2, C, C), np.int32)
    for d in range(2):
        p = np.arange(C) if d == 0 else C - 1 - np.arange(C)
        pi, pj = p[:, None], p[None, :]
        mat[d, 0:C] = pj <= pi
        mat[d, C:2 * C] = pj > pi
        code[d] = np.where((pj <= pi) & (pi // SUB == pj // SUB), 1, 0)
        for lv, c in enumerate(levels):
            blk = pi // c
            later = blk % 2 == 1
            mat[d, (2 + 2 * lv) * C:(3 + 2 * lv) * C] = later & (pj > blk * c - 1) & (pj <= pi)
            mat[d, (3 + 2 * lv) * C:(4 + 2 * lv) * C] = (~later) & (pj > pi) & (pj <= (blk + 1) * c - 1)
            pair = (pi // (2 * c) == pj // (2 * c)) & (pi // c != pj // c) & (pj <= pi)
            code[d] = np.where(pair, 2 + lv, code[d])
    ones = np.zeros((SUB * LANES, C), np.float32)
    for jj in range(SUB):
        ones[jj * LANES:(jj + 1) * LANES, jj::SUB] = 1.0
    return jnp.asarray(mat, dtype=BF16), jnp.asarray(code), jnp.asarray(ones, dtype=BF16)


def _bcast_sublane(x, jj):
    r, w = x.shape
    x3 = x.reshape(r // SUB, SUB, w)
    return jnp.broadcast_to(x3[:, jj:jj + 1, :], x3.shape).reshape(r, w)


def _t128(x):
    r, w = x.shape
    if w > LANES:
        return jnp.concatenate([x[:, i:i + LANES].T for i in range(0, w, LANES)], axis=0)
    return jnp.concatenate([x[i:i + LANES, :].T for i in range(0, r, LANES)], axis=1)


def _gla_kernel(q_ref, k_ref, v_ref, g_ref, lr_ref, wg_ref, ba_ref, mat_ref, code_ref, ones_ref,
                s0f_ref, s0b_ref, ng_ref, o_ref, sf_ref, sb_ref, st_scr, of_scr, *, n, C, G):
    d = pl.program_id(1)
    c = pl.program_id(2)
    levels = _gla_levels(C)

    @pl.when(jnp.logical_and(c == 0, d == 0))
    def _():
        for bb in range(G):
            for h in range(C_H):
                st_scr[bb, h] = _t128(s0f_ref[bb, h])

    @pl.when(jnp.logical_and(c == 0, d == 1))
    def _():
        for bb in range(G):
            for h in range(C_H):
                st_scr[bb, h] = _t128(s0b_ref[bb, h])

    mat = mat_ref[...]
    code = code_ref[...]
    ones = ones_ref[...]
    wg = _split(wg_ref[...])
    cums = []
    for bb in range(G):
        xg = _dot3(_split(lr_ref[bb]), wg) + ba_ref[...]
        la = (jnp.minimum(xg, 0.0) - jnp.log1p(jnp.exp(-jnp.abs(xg)))) * (LOG2E / C_TAU)
        hi, lo = _split(la)
        cums.append(_dot(mat, hi) + _dot(mat, lo))

    def prepare(bb, h):
        cum = cums[bb]
        ks = slice(h * C_DK, (h + 1) * C_DK)
        qh = q_ref[bb, :, ks].astype(F32) * (C_DK ** -0.5)
        kh = k_ref[bb, :, ks].astype(F32)
        b = cum[0:C, ks]
        b_rest = cum[C:2 * C, ks]
        ps = []
        for jj in range(SUB):
            dec = jnp.exp2(jnp.minimum(b - _bcast_sublane(b, jj), 0.0))
            ps.append((qh * _bcast_sublane(kh, jj) * dec).astype(BF16))
        lv_ops = []
        for lv in range(len(levels)):
            eq = cum[(2 + 2 * lv) * C:(3 + 2 * lv) * C, ks]
            ek = cum[(3 + 2 * lv) * C:(4 + 2 * lv) * C, ks]
            lv_ops.append(((qh * jnp.exp2(eq)).astype(BF16), (kh * jnp.exp2(ek)).astype(BF16)))
        qe = (qh * jnp.exp2(b)).astype(BF16)
        ke = (kh * jnp.exp2(b_rest)).astype(BF16)
        e_end = jnp.exp2(b[0:1, :] + b_rest[0:1, :])
        return jnp.concatenate(ps, axis=1), lv_ops, qe, ke, e_end

    def contract(bb, h, prep):
        pcat, lv_ops, qe, ke, e_end = prep
        vh = v_ref[bb, :, h * C_DV:(h + 1) * C_DV].astype(F32)
        att = jnp.where(code == 1, _dot(pcat, ones), 0.0)
        for lv, (qs, ks_) in enumerate(lv_ops):
            att = jnp.where(code == 2 + lv, _dot_nt(qs, ks_), att)
        st = st_scr[bb, h]
        o = _dot(att.astype(BF16), vh.astype(BF16)) + _dot_nt(qe, st.astype(BF16))
        st_scr[bb, h] = e_end * st + _dot(_t128(vh).astype(BF16), ke)
        return o

    units = [(bb, h) for h in range(C_H) for bb in range(G)]
    outs = {}
    prep = prepare(*units[0])
    for idx, (bb, h) in enumerate(units):
        nxt = prepare(*units[idx + 1]) if idx + 1 < len(units) else None
        outs[(bb, h)] = contract(bb, h, prep)
        prep = nxt
    o_all = [jnp.concatenate([outs[(bb, h)] for h in range(C_H)], axis=-1) for bb in range(G)]

    @pl.when(d == 0)
    def _():
        for bb in range(G):
            of_scr[bb, c] = o_all[bb]

    @pl.when(d == 1)
    def _():
        for bb in range(G):
            tot = o_all[bb] + of_scr[bb, n - 1 - c]
            res = []
            for h in range(C_H):
                sl = slice(h * C_DV, (h + 1) * C_DV)
                t = tot[:, sl]
                y = t * lax.rsqrt(jnp.mean(t * t, axis=-1, keepdims=True) + EPS) * ng_ref[:, sl]
                res.append(y * _silu(g_ref[bb, :, sl].astype(F32)))
            o_ref[bb] = jnp.concatenate(res, axis=-1).astype(o_ref.dtype)

    @pl.when(jnp.logical_and(c == n - 1, d == 0))
    def _():
        for bb in range(G):
            for h in range(C_H):
                sf_ref[bb, h] = _t128(st_scr[bb, h])

    @pl.when(jnp.logical_and(c == n - 1, d == 1))
    def _():
        for bb in range(G):
            for h in range(C_H):
                sb_ref[bb, h] = _t128(st_scr[bb, h])


def _gla(z, zg, B, T, w_a2, b_a, s0f, s0b, norm_g):
    C = GLA_CHUNK
    G = _rows_per_step(B, T, C_H * C_DV)
    assert B % G == 0 and T % C == 0
    n = T // C
    HK = C_H * C_DK
    HV = C_H * C_DV
    mat, code, ones = _gla_tables(C)
    nr = mat.shape[1] // C
    wg = jnp.zeros((2, LANES, HK), F32)
    for dr in range(2):
        wg = wg.at[dr, dr * C_RANK:(dr + 1) * C_RANK, :].set(w_a2[dr])
    z3 = z.reshape(B, T, z.shape[1])
    zg3 = zg.reshape(B, T, zg.shape[1])

    def chunk(d, c):
        return c + d * (n - 1 - 2 * c)

    st_spec = pl.BlockSpec((G, C_H, C_DK, C_DV), lambda b, d, c: (b, 0, 0, 0))
    st_shape = jax.ShapeDtypeStruct((B, C_H, C_DK, C_DV), F32)
    o, sf, sb = pl.pallas_call(
        functools.partial(_gla_kernel, n=n, C=C, G=G),
        grid=(B // G, 2, n),
        in_specs=[pl.BlockSpec((G, C, HK), lambda b, d, c: (b, chunk(d, c), 0)),
                  pl.BlockSpec((G, C, HK), lambda b, d, c: (b, chunk(d, c), 1)),
                  pl.BlockSpec((G, C, HV), lambda b, d, c: (b, chunk(d, c), 1)),
                  pl.BlockSpec((G, C, HV), lambda b, d, c: (b, chunk(d, c), 2)),
                  pl.BlockSpec((G, C, LANES), lambda b, d, c: (b, chunk(d, c), 0)),
                  pl.BlockSpec((None, LANES, HK), lambda b, d, c: (d, 0, 0)),
                  pl.BlockSpec((None, 1, HK), lambda b, d, c: (d, 0, 0)),
                  pl.BlockSpec((None, nr * C, C), lambda b, d, c: (d, 0, 0)),
                  pl.BlockSpec((None, C, C), lambda b, d, c: (d, 0, 0)),
                  pl.BlockSpec((SUB * LANES, C), lambda b, d, c: (0, 0)),
                  st_spec, st_spec,
                  pl.BlockSpec((1, HV), lambda b, d, c: (0, 0))],
        out_specs=[pl.BlockSpec((G, C, HV), lambda b, d, c: (b, (n - 1) - d * c, 0)),
                   st_spec, st_spec],
        out_shape=[jax.ShapeDtypeStruct((B, T, HV), BF16), st_shape, st_shape],
        scratch_shapes=[pltpu.VMEM((G, C_H, C_DV, C_DK), F32), pltpu.VMEM((G, n, C, HV), F32)],
        compiler_params=_params("arbitrary", "arbitrary", "arbitrary"),
        name="gla",
    )(z3, z3, z3, z3, zg3, wg, b_a.reshape(2, 1, HK), mat, code, ones, s0f, s0b, norm_g.reshape(1, HV))
    return o.reshape(B * T, HV), sf, sb


def _run_stream(x, B, T, mods, ctx, p):
    N, D = x.shape
    rpg = N // mods[0].shape[0]
    TM = min(2048, rpg)
    nb = (B_H + 2 * B_HKV) * B_HD

    w_in = p['even_w_in'][0]
    z, zb = _norm_mm(x, p['norm1_g'][0], mods[0], (0, 1), w_in, w_in, (nb, MIX_MAIN // nb), TM, rpg)
    if ctx is None:
        s0 = jnp.zeros((B, A_H, A_DK, A_DV), F32)
        a_f0, a_b0, cache_k, cache_v = s0, s0, None, None
    else:
        cache_k, cache_v, a_f0, a_b0 = ctx[0], ctx[1], ctx[2], ctx[3]
    o_a, a_sf, a_sb = _retention(z, B, T, p['a_log_gamma'][0], a_f0, a_b0, p['a_norm_g'][0])
    qpad, k_norm, k_rot, v_bf = _bprep(zb, T, p['b_q_g'][0], p['b_k_g'][0], rope=ctx is not None)
    o_b = _attention(qpad, k_rot, v_bf, B, T, cache_k, cache_v)
    x = _proj_res(x, mods[0], 2, [o_a, o_b], p['even_w_out'][0], rpg)
    x = _ffn(x, p['norm2_g'][0], mods[0], p['ff_w1'][0], p['ff_w3'][0], p['ff_w2'][0], rpg)

    w_in = p['odd_w_in'][0]
    w_gate = jnp.pad(w_in[:, MIX_MAIN:], ((0, 0), (0, LANES - 2 * C_RANK)))
    z1, z1g = _norm_mm(x, p['norm1_g'][1], mods[1], (0, 1), w_in, w_gate, (LANES, 0), TM, rpg)
    if ctx is None:
        s0 = jnp.zeros((B, C_H, C_DK, C_DV), F32)
        c_f0, c_b0 = s0, s0
    else:
        c_f0, c_b0 = ctx[4], ctx[5]
    o_c, c_sf, c_sb = _gla(z1, z1g, B, T, p['c_w_a2'][0], p['c_b_a'][0], c_f0, c_b0, p['c_norm_g'][0])
    x = _proj_res(x, mods[1], 2, [o_c], p['odd_w_out'][0], rpg)
    v_raw = zb[:, (B_H + B_HKV) * B_HD:]
    return x, (k_norm, v_raw, a_sf, a_sb, c_sf, c_sb)


def kernel(x_prompt, x_sample, c, cache_b_k, cache_b_v, state_a_fwd, state_a_bwd, state_c_fwd, state_c_bwd,
           c_ctx, w_mod, b_mod, norm1_g, norm2_g, final_g, even_w_in, even_w_out, a_log_gamma, a_norm_g,
           b_q_g, b_k_g, odd_w_in, c_w_a2, c_b_a, c_norm_g, odd_w_out, ff_w1, ff_w3, ff_w2,
           router_w, moe_w1, moe_w3, moe_w2):
    Bp, Tp, D = x_prompt.shape
    Bs, Ts, _ = x_sample.shape
    L = w_mod.shape[0]
    assert L == 2 and even_w_in.shape[0] == 1 and odd_w_in.shape[0] == 1
    p = dict(norm1_g=norm1_g, norm2_g=norm2_g, final_g=final_g, even_w_in=even_w_in, even_w_out=even_w_out,
             a_log_gamma=a_log_gamma, a_norm_g=a_norm_g, b_q_g=b_q_g, b_k_g=b_k_g, odd_w_in=odd_w_in,
             c_w_a2=c_w_a2, c_b_a=c_b_a, c_norm_g=c_norm_g, odd_w_out=odd_w_out, ff_w1=ff_w1, ff_w3=ff_w3,
             ff_w2=ff_w2, router_w=router_w, moe_w1=moe_w1, moe_w3=moe_w3, moe_w2=moe_w2)

    rows = 8
    conds = jnp.concatenate([c_ctx[None, :], c, jnp.zeros((rows - 1 - Bs, D), F32)], axis=0)
    mod = _modulation(conds, w_mod, b_mod).reshape(L, rows, 6, 1, D)
    mods_p = [mod[l, 0:1] for l in range(L)]
    mods_s = [mod[l, 1:1 + Bs] for l in range(L)]

    x_p, kept = _run_stream(x_prompt.reshape(Bp * Tp, D), Bp, Tp, mods_p, None, p)
    nk = B_HKV * B_HD
    ctx = (cache_b_k[:, 0].reshape(Bs, -1, nk), cache_b_v[:, 0].reshape(Bs, -1, nk),
           state_a_fwd[:, 0], state_a_bwd[:, 0], state_c_fwd[:, 0], state_c_bwd[:, 0])
    x_s, _ = _run_stream(x_sample.reshape(Bs * Ts, D), Bs, Ts, mods_s, ctx, p)
    y_p, y_s = _moe(x_p, x_s, norm2_g[1], mod[1, 0:1 + Bs], router_w[0], moe_w1[0], moe_w3[0], moe_w2[0],
                    final_g, Ts)

    k_norm, v_raw, a_sf, a_sb, c_sf, c_sb = kept
    return (y_p.reshape(Bp, Tp, D), y_s.reshape(Bs, Ts, D),
            k_norm.reshape(Bp, 1, Tp, B_HKV, B_HD), v_raw.reshape(Bp, 1, Tp, B_HKV, B_HD),
            a_sf[:, None], a_sb[:, None], c_sf[:, None], c_sb[:, None])
```

```python
import functools

import numpy as np
import jax
import jax.numpy as jnp
from jax import lax
from jax.experimental import pallas as pl
from jax.experimental.pallas import tpu as pltpu

F32 = jnp.float32
BF16 = jnp.bfloat16
EPS = 1e-6
LOG2E = 1.4426950408889634

VMEM_LIMIT_BYTES = 56 * 1024 * 1024

A_H, A_DK, A_DV = 4, 128, 256
B_H, B_HKV, B_HD = 8, 2, 64
C_H, C_DK, C_DV, C_RANK = 4, 128, 256, 16
C_TAU = 16.0
GRID_W = 64
ROPE_THETA = 10000.0
N_EXPERTS = 8
LANES = 128
SUB = 8
RET_CHUNK = 128
GLA_CHUNK = 128
Q_TILE = 128
SCAN_ROWS_MAX = 4
SCAN_FWD_BYTES = 32 * 1024 * 1024


def _rows_per_step(B, T, width):
    g = SCAN_ROWS_MAX
    while g > 1 and (B % g or g * T * width * 4 > SCAN_FWD_BYTES):
        g //= 2
    return g


def _params(*sem):
    return pltpu.CompilerParams(dimension_semantics=sem, vmem_limit_bytes=VMEM_LIMIT_BYTES)


def _dot(a, b):
    return jnp.dot(a, b, preferred_element_type=F32)


def _dot_nt(a, b):
    return lax.dot_general(a, b, (((1,), (1,)), ((), ())), preferred_element_type=F32)


def _split(x):
    hi = x.astype(BF16)
    return hi, (x - hi.astype(F32)).astype(BF16)


def _dot3(a, b):
    return _dot(a[0], b[0]) + (_dot(a[0], b[1]) + _dot(a[1], b[0]))


def _silu(x):
    return x * jax.nn.sigmoid(x)


def _norm_mod(x, g, sh, sc):
    r = lax.rsqrt(jnp.mean(x * x, axis=-1, keepdims=True) + EPS)
    return (x * r * g) * (1.0 + sc) + sh


def _mod_kernel(c_ref, w_ref, b_ref, o_ref):
    o_ref[...] = _dot3(_split(_silu(c_ref[...])), _split(w_ref[...])) + b_ref[...]


def _modulation(conds, w_mod, b_mod):
    L, D, D6 = w_mod.shape
    R = conds.shape[0]
    TN = 1024
    return pl.pallas_call(
        _mod_kernel,
        grid=(L, D6 // TN),
        in_specs=[pl.BlockSpec((R, D), lambda l, j: (0, 0)),
                  pl.BlockSpec((None, D, TN), lambda l, j: (l, 0, j)),
                  pl.BlockSpec((None, 1, TN), lambda l, j: (l, 0, j))],
        out_specs=pl.BlockSpec((None, R, TN), lambda l, j: (l, 0, j)),
        out_shape=jax.ShapeDtypeStruct((L, R, D6), F32),
        compiler_params=_params("arbitrary", "arbitrary"),
        name="modulation",
    )(conds, w_mod, b_mod.reshape(L, 1, D6))


def _mod_spec(part, D, TM, rows_per_group, axis):
    def idx(*g):
        return ((g[axis] * TM) // rows_per_group, part, 0, 0)
    return pl.BlockSpec((None, None, 1, D), idx)


MIX_MAIN = A_H * (2 * A_DK + 2 * A_DV)
MIX_TN = 768


def _norm_mm_kernel(x_ref, g_ref, sh_ref, sc_ref, w_ref, we_ref, o_ref, oe_ref, h_scr, *, nmain):
    j = pl.program_id(1)

    @pl.when(j == 0)
    def _():
        h_scr[...] = _norm_mod(x_ref[...], g_ref[...], sh_ref[...], sc_ref[...]).astype(BF16)

    @pl.when(j < nmain)
    def _():
        o_ref[...] = _dot(h_scr[...], w_ref[...].astype(BF16)).astype(o_ref.dtype)

    @pl.when(j == nmain)
    def _():
        oe_ref[...] = _dot(h_scr[...], we_ref[...].astype(BF16))


def _norm_mm(x, g, mod, parts, w, w_extra, extra_block, TM, rows_per_group):
    N, D = x.shape
    nmain = MIX_MAIN // MIX_TN
    WE = extra_block[0]
    return pl.pallas_call(
        functools.partial(_norm_mm_kernel, nmain=nmain),
        grid=(N // TM, nmain + 1),
        in_specs=[pl.BlockSpec((TM, D), lambda i, j: (i, 0)),
                  pl.BlockSpec((1, D), lambda i, j: (0, 0)),
                  _mod_spec(parts[0], D, TM, rows_per_group, 0),
                  _mod_spec(parts[1], D, TM, rows_per_group, 0),
                  pl.BlockSpec((D, MIX_TN), lambda i, j: (0, jnp.minimum(j, nmain - 1))),
                  pl.BlockSpec((D, WE), lambda i, j: (0, extra_block[1]))],
        out_specs=[pl.BlockSpec((TM, MIX_TN), lambda i, j: (i, jnp.minimum(j, nmain - 1))),
                   pl.BlockSpec((TM, WE), lambda i, j: (i, 0))],
        out_shape=[jax.ShapeDtypeStruct((N, MIX_MAIN), BF16), jax.ShapeDtypeStruct((N, WE), F32)],
        scratch_shapes=[pltpu.VMEM((TM, D), BF16)],
        compiler_params=_params("arbitrary", "arbitrary"),
        name="norm_mm",
    )(x, g.reshape(1, D), mod, mod, w, w_extra)


def _ret_kernel(lg_ref, q_ref, k_ref, v_ref, ag_ref, s0f_ref, s0b_ref, ng_ref,
                o_ref, sf_ref, sb_ref, s_scr, of_scr, *, n, C, G):
    d = pl.program_id(1)
    c = pl.program_id(2)

    @pl.when(jnp.logical_and(c == 0, d == 0))
    def _():
        s_scr[...] = s0f_ref[...]

    @pl.when(jnp.logical_and(c == 0, d == 1))
    def _():
        s_scr[...] = s0b_ref[...]

    df = d.astype(F32)
    sgn = 1.0 - 2.0 * df
    ii = lax.broadcasted_iota(jnp.int32, (C, C), 0).astype(F32)
    jj = lax.broadcasted_iota(jnp.int32, (C, C), 1).astype(F32)
    dd = (ii - jj) * sgn
    feeds = dd >= 0.0
    ddc = jnp.maximum(dd, 0.0)
    ri = lax.broadcasted_iota(jnp.int32, (C, 1), 0).astype(F32)
    pos_q = (ri + 1.0) + df * (C - 2.0 * ri - 1.0)
    pos_k = (C - 1.0 - ri) + df * (2.0 * ri - C + 1.0)
    chunk_len = jnp.full((1, A_DV), float(C), F32)

    outs = [[] for _ in range(G)]
    for h in range(A_H):
        lg = lg_ref[d, h]
        dmask = jnp.where(feeds, jnp.exp2(lg * ddc), 0.0)
        q_dec = jnp.exp2(lg * pos_q)
        k_dec = jnp.exp2(lg * pos_k)
        c_dec = jnp.exp2(lg * chunk_len)
        for bb in range(G):
            qh = q_ref[bb, :, h * A_DK:(h + 1) * A_DK].astype(F32) * (A_DK ** -0.5)
            kh = k_ref[bb, :, h * A_DK:(h + 1) * A_DK].astype(F32)
            vh = v_ref[bb, :, h * A_DV:(h + 1) * A_DV].astype(BF16)
            s = s_scr[bb, h]
            att = _dot_nt(qh.astype(BF16), kh.astype(BF16)) * dmask
            o = _dot(att.astype(BF16), vh) + _dot((qh * q_dec).astype(BF16), s.astype(BF16))
            kd = kh * k_dec
            s_scr[bb, h] = c_dec * s + _dot(kd.T.astype(BF16), vh)
            outs[bb].append(o)
    o_all = [jnp.concatenate(o, axis=-1) for o in outs]

    @pl.when(d == 0)
    def _():
        for bb in range(G):
            of_scr[bb, c] = o_all[bb]

    @pl.when(d == 1)
    def _():
        for bb in range(G):
            tot = o_all[bb] + of_scr[bb, n - 1 - c]
            res = []
            for h in range(A_H):
                sl = slice(h * A_DV, (h + 1) * A_DV)
                t = tot[:, sl]
                dev = t - jnp.mean(t, axis=-1, keepdims=True)
                y = dev * lax.rsqrt(jnp.mean(dev * dev, axis=-1, keepdims=True) + EPS) * ng_ref[:, sl]
                res.append(y * _silu(ag_ref[bb, :, sl].astype(F32)))
            o_ref[bb] = jnp.concatenate(res, axis=-1).astype(o_ref.dtype)

    @pl.when(jnp.logical_and(c == n - 1, d == 0))
    def _():
        sf_ref[...] = s_scr[...]

    @pl.when(jnp.logical_and(c == n - 1, d == 1))
    def _():
        sb_ref[...] = s_scr[...]


def _retention(z, B, T, log_gamma, s0f, s0b, norm_g):
    C = RET_CHUNK
    G = _rows_per_step(B, T, A_H * A_DV)
    assert B % G == 0 and T % C == 0
    n = T // C
    HK = A_H * A_DK
    HV = A_H * A_DV
    z3 = z.reshape(B, T, z.shape[1])

    def chunk(d, c):
        return c + d * (n - 1 - 2 * c)

    st_spec = pl.BlockSpec((G, A_H, A_DK, A_DV), lambda b, d, c: (b, 0, 0, 0))
    st_shape = jax.ShapeDtypeStruct((B, A_H, A_DK, A_DV), F32)
    o, sf, sb = pl.pallas_call(
        functools.partial(_ret_kernel, n=n, C=C, G=G),
        grid=(B // G, 2, n),
        in_specs=[pl.BlockSpec(memory_space=pltpu.SMEM),
                  pl.BlockSpec((G, C, HK), lambda b, d, c: (b, chunk(d, c), 0)),
                  pl.BlockSpec((G, C, HK), lambda b, d, c: (b, chunk(d, c), 1)),
                  pl.BlockSpec((G, C, HV), lambda b, d, c: (b, chunk(d, c), 1)),
                  pl.BlockSpec((G, C, HV), lambda b, d, c: (b, chunk(d, c), 2)),
                  st_spec, st_spec,
                  pl.BlockSpec((1, HV), lambda b, d, c: (0, 0))],
        out_specs=[pl.BlockSpec((G, C, HV), lambda b, d, c: (b, (n - 1) - d * c, 0)),
                   st_spec, st_spec],
        out_shape=[jax.ShapeDtypeStruct((B, T, HV), BF16), st_shape, st_shape],
        scratch_shapes=[pltpu.VMEM((G, A_H, A_DK, A_DV), F32), pltpu.VMEM((G, n, C, HV), F32)],
        compiler_params=_params("arbitrary", "arbitrary", "arbitrary"),
        name="retention",
    )(log_gamma * LOG2E, z3, z3, z3, z3, s0f, s0b, norm_g.reshape(1, HV))
    return o.reshape(B * T, HV), sf, sb


def _group_sum_matrix(width, group):
    i = np.arange(width)
    return jnp.asarray((i[:, None] // group == i[None, :] // group).astype(np.float32), dtype=BF16)


def _q_pad_matrix():
    m = np.zeros((B_H * B_HD, B_H * LANES), np.float32)
    g = B_H // B_HKV
    for h in range(B_H):
        for t in range(B_HD):
            m[h * B_HD + t, h * LANES + (h // g) * B_HD + t] = 1.0
    return jnp.asarray(m, dtype=BF16)


def _rope_tables(T):
    rows = T // GRID_W
    row = np.repeat(np.arange(rows, dtype=np.float64), GRID_W)
    col = np.tile(np.arange(GRID_W, dtype=np.float64), rows)
    nq = B_HD // 4
    inv = ROPE_THETA ** (-np.arange(nq, dtype=np.float64) / nq)
    ang = np.concatenate([row[:, None] * inv, col[:, None] * inv], axis=-1)
    cos = np.repeat(np.cos(ang), 2, axis=-1)
    sin = np.repeat(np.sin(ang), 2, axis=-1)
    sign = np.tile(np.array([-1.0, 1.0]), B_HD // 2)
    reps = LANES // B_HD
    return (jnp.asarray(np.tile(cos, (1, reps)), dtype=F32),
            jnp.asarray(np.tile(sin * sign, (1, reps)), dtype=F32))


def _group_rmsnorm(x, gsum, g):
    hi, lo = _split(x * x)
    ss = _dot(hi, gsum) + _dot(lo, gsum)
    return x * lax.rsqrt(ss * (1.0 / B_HD) + EPS) * g


def _rotate_pairs(x, cos, sin_signed):
    n = x.shape[1]
    lane = lax.broadcasted_iota(jnp.int32, x.shape, 1)
    partner = jnp.where(lane % 2 == 0, pltpu.roll(x, n - 1, 1), pltpu.roll(x, 1, 1))
    reps = n // LANES
    if reps > 1:
        cos = jnp.concatenate([cos] * reps, axis=1)
        sin_signed = jnp.concatenate([sin_signed] * reps, axis=1)
    return x * cos + partner * sin_signed


def _bprep_kernel(z_ref, qg_ref, kg_ref, cos_ref, sin_ref, gq_ref, gk_ref, pad_ref,
                  qpad_ref, kn_ref, kr_ref, vb_ref, *, rope):
    nq = B_H * B_HD
    nk = B_HKV * B_HD
    qn = _group_rmsnorm(z_ref[:, 0:nq], gq_ref[...], qg_ref[...])
    kn = _group_rmsnorm(z_ref[:, nq:nq + nk], gk_ref[...], kg_ref[...])
    kn_ref[...] = kn
    if rope:
        qn = _rotate_pairs(qn, cos_ref[...], sin_ref[...])
        kn = _rotate_pairs(kn, cos_ref[...], sin_ref[...])
    kr_ref[...] = kn.astype(BF16)
    vb_ref[...] = z_ref[:, nq + nk:nq + 2 * nk].astype(BF16)
    qs = (qn * (B_HD ** -0.5 * LOG2E)).astype(BF16)
    qpad_ref[...] = _dot(qs, pad_ref[...]).astype(BF16)


def _bprep(z, T, q_g, k_g, rope):
    N = z.shape[0]
    TM = min(512, T)
    nq = B_H * B_HD
    nk = B_HKV * B_HD
    width = nq + 2 * nk
    assert z.shape[1] == width
    cos, sin = _rope_tables(T if rope else TM)
    nt = T // TM if rope else 1
    const = lambda i: (0, 0)
    return pl.pallas_call(
        functools.partial(_bprep_kernel, rope=rope),
        grid=(N // TM,),
        in_specs=[pl.BlockSpec((TM, width), lambda i: (i, 0)),
                  pl.BlockSpec((1, nq), const),
                  pl.BlockSpec((1, nk), const),
                  pl.BlockSpec((TM, LANES), lambda i: (i % nt, 0)),
                  pl.BlockSpec((TM, LANES), lambda i: (i % nt, 0)),
                  pl.BlockSpec((nq, nq), const),
                  pl.BlockSpec((nk, nk), const),
                  pl.BlockSpec((nq, B_H * LANES), const)],
        out_specs=[pl.BlockSpec((TM, B_H * LANES), lambda i: (i, 0)),
                   pl.BlockSpec((TM, nk), lambda i: (i, 0)),
                   pl.BlockSpec((TM, nk), lambda i: (i, 0)),
                   pl.BlockSpec((TM, nk), lambda i: (i, 0))],
        out_shape=[jax.ShapeDtypeStruct((N, B_H * LANES), BF16),
                   jax.ShapeDtypeStruct((N, nk), F32),
                   jax.ShapeDtypeStruct((N, nk), BF16),
                   jax.ShapeDtypeStruct((N, nk), BF16)],
        compiler_params=_params("arbitrary"),
        name="attn_prep",
    )(z, jnp.tile(q_g, B_H).reshape(1, nq), jnp.tile(k_g, B_HKV).reshape(1, nk), cos, sin,
      _group_sum_matrix(nq, B_HD), _group_sum_matrix(nk, B_HD), _q_pad_matrix())


def _lane_fold(x, op):
    acc = x[:, 0:LANES]
    for j in range(1, x.shape[1] // LANES):
        acc = op(acc, x[:, j * LANES:(j + 1) * LANES])
    return acc


def _attn_kernel(*refs, has_cache, kc, nq):
    if has_cache:
        q_ref, k_ref, v_ref, ck_ref, cv_ref, o_ref, s_scr, m_scr, mprev_scr, l_scr, acc_scr = refs
        kcc = min(kc, ck_ref.shape[0])
        ncache = ck_ref.shape[0] // kcc
    else:
        q_ref, k_ref, v_ref, o_ref, s_scr, m_scr, mprev_scr, l_scr, acc_scr = refs
        kcc, ncache = kc, 0
    i = pl.program_id(1)
    tq = q_ref.shape[0]
    nlat = k_ref.shape[0] // kc

    def score(c, kblk):
        q = jnp.concatenate([q_ref[:, h * LANES:(h + 1) * LANES] for h in range(B_H)], axis=0)
        s = _dot_nt(q, kblk)
        s_scr[c, :, 0:kblk.shape[0]] = s
        m_scr[...] = jnp.maximum(m_scr[...], _lane_fold(s, jnp.maximum))

    def weight(c, vblk):
        s = s_scr[c, :, 0:vblk.shape[0]]
        mp = mprev_scr[...]
        ps = [jnp.exp2(s[:, j * LANES:(j + 1) * LANES] - mp) for j in range(vblk.shape[0] // LANES)]
        tot = ps[0]
        for pj in ps[1:]:
            tot = tot + pj
        l_scr[...] += tot
        acc_scr[...] += _dot(jnp.concatenate(ps, axis=1).astype(BF16), vblk)

    def run(do_weight, do_score):
        def unit(c, kblk, vblk):
            if do_weight:
                weight(c, vblk())
            if do_score:
                score(c, kblk())

        for c in range(ncache):
            unit(c, lambda: ck_ref[c * kcc:(c + 1) * kcc, :].astype(BF16),
                 lambda: cv_ref[c * kcc:(c + 1) * kcc, :].astype(BF16))

        def body(c, carry):
            rows = pl.ds(pl.multiple_of(c * kc, kc), kc)
            unit(ncache + c, lambda: k_ref[rows, :], lambda: v_ref[rows, :])
            return carry
        lax.fori_loop(0, nlat, body, 0)

    @pl.when(i < nq)
    def _():
        m_scr[...] = jnp.full(m_scr.shape, -jnp.inf, F32)

    @pl.when(i > 0)
    def _():
        l_scr[...] = jnp.zeros_like(l_scr)
        acc_scr[...] = jnp.zeros_like(acc_scr)

    @pl.when(i == 0)
    def _():
        run(False, True)

    @pl.when(jnp.logical_and(i > 0, i < nq))
    def _():
        run(True, True)

    @pl.when(i == nq)
    def _():
        run(True, False)

    @pl.when(i > 0)
    def _():
        r_all = acc_scr[...] / jnp.sum(l_scr[...], axis=-1, keepdims=True)
        g = B_H // B_HKV
        lane = lax.broadcasted_iota(jnp.int32, (tq, LANES), 1)
        outs = []
        for j in range(B_H // 2):
            pair = []
            for half in range(2):
                h = 2 * j + half
                r = r_all[h * tq:(h + 1) * tq, :]
                if h // g != half:
                    r = pltpu.roll(r, B_HD, 1)
                pair.append(r)
            outs.append(jnp.where(lane < B_HD, pair[0], pair[1]))
        o_ref[...] = jnp.concatenate(outs, axis=-1).astype(o_ref.dtype)

    @pl.when(i < nq)
    def _():
        mprev_scr[...] = jnp.broadcast_to(jnp.max(m_scr[...], axis=-1, keepdims=True), mprev_scr.shape)


def _attention(qpad, kr, vb, B, T, cache_k, cache_v):
    has_cache = cache_k is not None
    TQ = Q_TILE
    nq = T // TQ
    nk = B_HKV * B_HD
    in_specs = [pl.BlockSpec((TQ, B_H * LANES), lambda b, i: (b * nq + jnp.minimum(i, nq - 1), 0)),
                pl.BlockSpec((T, nk), lambda b, i: (b, 0)),
                pl.BlockSpec((T, nk), lambda b, i: (b, 0))]
    args = [qpad, kr, vb]
    kc = min(1024, T)
    nchunks = T // kc
    if has_cache:
        P = cache_k.shape[1]
        assert P % min(kc, P) == 0
        nchunks += P // min(kc, P)
        in_specs += [pl.BlockSpec((None, P, nk), lambda b, i: (b, 0, 0))] * 2
        args += [cache_k, cache_v]
    R = B_H * TQ
    return pl.pallas_call(
        functools.partial(_attn_kernel, has_cache=has_cache, kc=kc, nq=nq),
        grid=(B, nq + 1),
        in_specs=in_specs,
        out_specs=pl.BlockSpec((TQ, B_H * B_HD), lambda b, i: (b * nq + jnp.maximum(i - 1, 0), 0)),
        out_shape=jax.ShapeDtypeStruct((B * T, B_H * B_HD), BF16),
        scratch_shapes=[pltpu.VMEM((nchunks, R, kc), F32)] + [pltpu.VMEM((R, LANES), F32)] * 4,
        compiler_params=_params("arbitrary", "arbitrary"),
        name="attention",
    )(*args)


def _proj_res_kernel(*refs, n_in):
    x_ref, gate_ref = refs[0], refs[1]
    o_refs = refs[2:2 + n_in]
    w_refs = refs[2 + n_in:2 + 2 * n_in]
    out_ref = refs[2 + 2 * n_in]
    wbf_refs = refs[3 + 2 * n_in:]

    @pl.when(pl.program_id(0) == 0)
    def _():
        for w_ref, wbf_ref in zip(w_refs, wbf_refs):
            wbf_ref[...] = w_ref[...].astype(BF16)

    acc = _dot(o_refs[0][...], wbf_refs[0][...])
    for o_ref, wbf_ref in zip(o_refs[1:], wbf_refs[1:]):
        acc = acc + _dot(o_ref[...], wbf_ref[...])
    out_ref[...] = x_ref[...] + gate_ref[...] * acc


def _proj_res(x, mod, part, acts, w, rows_per_group):
    N, D = x.shape
    TM = min(1024, rows_per_group)
    n_in = len(acts)
    widths = [a.shape[1] for a in acts]
    offs = np.cumsum([0] + widths[:-1]).tolist()
    in_specs = [pl.BlockSpec((TM, D), lambda i: (i, 0)),
                _mod_spec(part, D, TM, rows_per_group, 0)]
    in_specs += [pl.BlockSpec((TM, wd), lambda i: (i, 0)) for wd in widths]
    in_specs += [pl.BlockSpec((wd, D), functools.partial(lambda i, blk: (blk, 0), blk=off // wd))
                 for wd, off in zip(widths, offs)]
    return pl.pallas_call(
        functools.partial(_proj_res_kernel, n_in=n_in),
        grid=(N // TM,),
        in_specs=in_specs,
        out_specs=pl.BlockSpec((TM, D), lambda i: (i, 0)),
        out_shape=jax.ShapeDtypeStruct((N, D), F32),
        scratch_shapes=[pltpu.VMEM((wd, D), BF16) for wd in widths],
        compiler_params=_params("arbitrary"),
        name="proj_residual",
    )(x, mod, *acts, *([w] * n_in))


def _ffn_kernel(x_ref, g_ref, sh_ref, sc_ref, gate_ref, w1_ref, w3_ref, w2_ref, out_ref, h_scr, acc_scr, *, nf):
    f = pl.program_id(1)

    @pl.when(f == 0)
    def _():
        h_scr[...] = _norm_mod(x_ref[...], g_ref[...], sh_ref[...], sc_ref[...]).astype(BF16)
        acc_scr[...] = jnp.zeros_like(acc_scr)

    h = h_scr[...]
    a = _dot(h, w1_ref[...].astype(BF16))
    b = _dot(h, w3_ref[...].astype(BF16))
    acc_scr[...] += _dot((_silu(a) * b).astype(BF16), w2_ref[...].astype(BF16))

    @pl.when(f == nf - 1)
    def _():
        out_ref[...] = x_ref[...] + gate_ref[...] * acc_scr[...]


def _ffn(x, g, mod, w1, w3, w2, rows_per_group):
    N, D = x.shape
    FF = w1.shape[1]
    TM, TF = min(1024, rows_per_group), 256
    nf = FF // TF
    return pl.pallas_call(
        functools.partial(_ffn_kernel, nf=nf),
        grid=(N // TM, nf),
        in_specs=[pl.BlockSpec((TM, D), lambda i, f: (i, 0)),
                  pl.BlockSpec((1, D), lambda i, f: (0, 0)),
                  _mod_spec(3, D, TM, rows_per_group, 0),
                  _mod_spec(4, D, TM, rows_per_group, 0),
                  _mod_spec(5, D, TM, rows_per_group, 0),
                  pl.BlockSpec((D, TF), lambda i, f: (0, f)),
                  pl.BlockSpec((D, TF), lambda i, f: (0, f)),
                  pl.BlockSpec((TF, D), lambda i, f: (f, 0))],
        out_specs=pl.BlockSpec((TM, D), lambda i, f: (i, 0)),
        out_shape=jax.ShapeDtypeStruct((N, D), F32),
        scratch_shapes=[pltpu.VMEM((TM, D), BF16), pltpu.VMEM((TM, D), F32)],
        compiler_params=_params("arbitrary", "arbitrary"),
        name="ffn",
    )(x, g.reshape(1, D), mod, mod, mod, w1, w3, w2)


MOE_SB = 1024
MOE_SBG = 512
MOE_GG = 4
MOE_TRG = 256
MOE_TR = 2048
MOE_CG = 4


def _two_stream_specs(shape, ntp, ax=0):
    def idx_p(*g):
        return (jnp.minimum(g[ax], ntp - 1), 0)

    def idx_s(*g):
        return (jnp.maximum(g[ax] - ntp, 0), 0)
    return pl.BlockSpec(shape, idx_p), pl.BlockSpec(shape, idx_s)


def _pool_mod_spec(part, D, TM, ntp, rows_per_group):
    def idx(i, *_):
        return (jnp.where(i < ntp, 0, 1 + ((i - ntp) * TM) // rows_per_group), part, 0, 0)
    return pl.BlockSpec((None, None, 1, D), idx)


def _route_kernel(xp_ref, xs_ref, g_ref, sh_ref, sc_ref, rw_ref, tri_ref, h_ref, info_ref, infot_ref, cum_ref,
                  carry_scr, *, ntp):
    i = pl.program_id(0)

    @pl.when(i == 0)
    def _():
        carry_scr[...] = jnp.zeros_like(carry_scr)

    x = jnp.where(i < ntp, xp_ref[...], xs_ref[...])
    h = _norm_mod(x, g_ref[...], sh_ref[...], sc_ref[...])
    h_ref[...] = h.astype(BF16)
    lane = lax.broadcasted_iota(jnp.int32, (x.shape[0], LANES), 1).astype(F32)
    logits = _dot3(_split(h), _split(rw_ref[...]))
    logits = jnp.where(lane < N_EXPERTS, logits, -jnp.inf)
    m1 = jnp.max(logits, axis=-1, keepdims=True)
    i1 = jnp.min(jnp.where(logits == m1, lane, float(LANES)), axis=-1, keepdims=True)
    rest = jnp.where(lane == i1, -jnp.inf, logits)
    m2 = jnp.max(rest, axis=-1, keepdims=True)
    i2 = jnp.min(jnp.where(rest == m2, lane, float(LANES)), axis=-1, keepdims=True)
    e2 = jnp.exp(m2 - m1)
    w1 = 1.0 / (1.0 + e2)
    w2 = e2 / (1.0 + e2)
    ind = jnp.where(jnp.logical_or(lane == i1, lane == i2), 1.0, 0.0)
    before = _dot(tri_ref[...], ind.astype(BF16)) + carry_scr[...]
    r1 = jnp.sum(jnp.where(lane == i1, before, 0.0), axis=-1, keepdims=True)
    r2 = jnp.sum(jnp.where(lane == i2, before, 0.0), axis=-1, keepdims=True)
    total = carry_scr[...] + jnp.sum(ind, axis=0, keepdims=True)
    carry_scr[...] = total
    for part in range(1, MOE_SB // MOE_SBG):
        cum_ref[part - 1] = before[part * MOE_SBG:part * MOE_SBG + 1, :]
    cum_ref[MOE_SB // MOE_SBG - 1] = total
    info = jnp.where(lane == 0.0, i1, jnp.where(lane == 1.0, i2, jnp.where(lane == 2.0, w1, jnp.where(
        lane == 3.0, w2, jnp.where(lane == 4.0, r1, jnp.where(lane == 5.0, r2, 0.0))))))
    info_ref[...] = info[:, 0:SUB]
    info_t = jnp.concatenate([info[r:r + LANES, :].T for r in range(0, info.shape[0], LANES)], axis=1)
    infot_ref[...] = info_t[0:SUB, :]


def _moe_route(xp, xs, g, mod, router_w, rows_per_group):
    Np, D = xp.shape
    N = Np + xs.shape[0]
    TM = MOE_SB
    ntp = Np // TM
    nt = N // TM
    rw = jnp.pad(router_w, ((0, 0), (0, LANES - router_w.shape[1])))
    tri = jnp.asarray(np.tril(np.ones((TM, TM), np.float32), -1), dtype=BF16)
    xp_spec, xs_spec = _two_stream_specs((TM, D), ntp)
    return pl.pallas_call(
        functools.partial(_route_kernel, ntp=ntp),
        grid=(nt,),
        in_specs=[xp_spec, xs_spec,
                  pl.BlockSpec((1, D), lambda i: (0, 0)),
                  _pool_mod_spec(3, D, TM, ntp, rows_per_group),
                  _pool_mod_spec(4, D, TM, ntp, rows_per_group),
                  pl.BlockSpec((D, LANES), lambda i: (0, 0)),
                  pl.BlockSpec((TM, TM), lambda i: (0, 0))],
        out_specs=[pl.BlockSpec((TM, D), lambda i: (i, 0)),
                   pl.BlockSpec((TM, SUB), lambda i: (i, 0)),
                   pl.BlockSpec((SUB, TM), lambda i: (0, i)),
                   pl.BlockSpec((MOE_SB // MOE_SBG, 1, LANES), lambda i: (i, 0, 0))],
        out_shape=[jax.ShapeDtypeStruct((N, D), BF16),
                   jax.ShapeDtypeStruct((N, SUB), F32),
                   jax.ShapeDtypeStruct((SUB, N), F32),
                   jax.ShapeDtypeStruct((nt * (MOE_SB // MOE_SBG), 1, LANES), F32)],
        scratch_shapes=[pltpu.VMEM((1, LANES), F32)],
        compiler_params=_params("arbitrary"),
        name="moe_route",
    )(xp, xs, g.reshape(1, D), mod, mod, rw, tri)


def _hold_unused(idx, used):
    steps, slots = idx.shape
    read = jnp.arange(slots, dtype=jnp.int32)[None, :] < used[:, None]
    step = jnp.arange(steps, dtype=jnp.int32)[:, None]
    last = lax.cummax(jnp.where(read, step, -1), axis=0)
    held = jnp.take_along_axis(idx, jnp.maximum(last, 0), axis=0)
    return jnp.where(last >= 0, held, 0)


def _moe_plan(info, info_t, cum, N):
    E, SB, TRG, TR = N_EXPERTS, MOE_SB, MOE_TRG, MOE_TR
    NB = N // SB
    rmax = 2 * N + E * TR
    RG, RT = rmax // TRG, rmax // TR
    PMAX = RG + E * NB
    i32 = jnp.int32
    parts = SB // MOE_SBG
    cum_g = cum[:, 0, :E].astype(i32).T
    cum_e = cum_g[:, parts - 1::parts]
    cnt = cum_e[:, -1]
    tiles = (cnt + TR - 1) // TR
    start = TR * (jnp.cumsum(tiles) - tiles)

    startf = start.astype(F32)

    def region_start(e):
        out = jnp.zeros_like(e)
        for k in range(E):
            out = jnp.where(e == float(k), startf[k], out)
        return out

    pos_cols = jnp.concatenate([region_start(info[:, 0:2]) + info[:, 4:6], info[:, 2:4],
                                jnp.zeros((N, 4), F32)], axis=1)
    pos_rows = jnp.concatenate([region_start(info_t[0:2]) + info_t[4:6], jnp.zeros((6, N), F32)],
                               axis=0)

    def region(row0):
        e = jnp.clip(jnp.sum(row0[:, None] >= start[None, :], axis=1) - 1, 0, E - 1)
        return e, row0 - start[e]

    eq, lo = region(jnp.arange(RG, dtype=i32) * TRG)
    hi = jnp.minimum(lo + TRG, cnt[eq])
    first = jnp.sum(cum_e[eq] <= lo[:, None], axis=1)
    last = jnp.sum(cum_e[eq] < hi[:, None], axis=1)
    nblk = jnp.where(hi > lo, last - first + 1, 0)
    pend = jnp.cumsum(nblk)
    npairs = pend[-1]
    p = jnp.arange(PMAX, dtype=i32)
    valid = p < npairs
    pc = jnp.minimum(p, npairs - 1)
    q_of = jnp.minimum(jnp.sum(pend[None, :] <= pc[:, None], axis=1), RG - 1).astype(i32)
    pstart = pend - nblk
    s_of = (first[q_of] + pc - pstart[q_of]).astype(i32)

    GG = MOE_GG
    first_g = jnp.sum(cum_g[eq] <= lo[:, None], axis=1)
    last_g = jnp.sum(cum_g[eq] < hi[:, None], axis=1)
    nblk_g = jnp.where(hi > lo, last_g - first_g + 1, 0)
    nst = (nblk_g + GG - 1) // GG
    gst_end = jnp.cumsum(nst)
    gtotal = gst_end[-1]
    smax_g = (RG + E * NB * parts + (GG - 1) * RG) // GG + 1
    jg = jnp.arange(smax_g, dtype=i32)
    g_ok = jg < gtotal
    jgc = jnp.minimum(jg, gtotal - 1)
    tq = jnp.minimum(jnp.sum(gst_end[None, :] <= jgc[:, None], axis=1), RG - 1).astype(i32)
    gg = jgc - (gst_end - nst)[tq]
    g_slots = jnp.where(g_ok, jnp.clip(nblk_g[tq] - GG * gg, 0, GG), 0)
    g_parts = _hold_unused((first_g[tq] + GG * gg)[:, None] + jnp.arange(GG, dtype=i32)[None, :], g_slots)
    g_first = jnp.logical_and(g_ok, gg == 0)
    gather_plan = (tq, g_parts.reshape(-1).astype(i32), g_slots.astype(i32), g_first.astype(i32))

    order = jnp.argsort(jnp.where(valid, s_of * RG + q_of, jnp.iinfo(jnp.int32).max))
    s2, q2 = s_of[order], q_of[order]
    CG = MOE_CG
    blocks = jnp.arange(NB, dtype=i32)
    per_blk = jnp.sum(jnp.logical_and(valid[None, :], s2[None, :] == blocks[:, None]), axis=1)
    pb_end = jnp.cumsum(per_blk)
    pb_start = pb_end - per_blk
    nsteps = (per_blk + CG - 1) // CG
    st_end = jnp.cumsum(nsteps)
    total = st_end[-1]
    SMAX = (PMAX + CG - 1) // CG + NB
    j = jnp.arange(SMAX, dtype=i32)
    step_ok = j < total
    jc = jnp.minimum(j, total - 1)
    blk = jnp.minimum(jnp.sum(st_end[None, :] <= jc[:, None], axis=1), NB - 1).astype(i32)
    grp = jc - (st_end - nsteps)[blk]
    slot_p = pb_start[blk][:, None] + CG * grp[:, None] + jnp.arange(CG, dtype=i32)[None, :]
    slot_ok = jnp.logical_and(slot_p < pb_end[blk][:, None], step_ok[:, None])
    c_slots = jnp.sum(slot_ok, axis=1).astype(i32)
    slot_q = _hold_unused(q2[jnp.minimum(slot_p, npairs - 1)], c_slots)
    c_first = jnp.logical_and(step_ok, grp == 0).astype(i32)
    c_last = jnp.logical_and(step_ok, grp == nsteps[blk] - 1).astype(i32)
    combine_plan = (blk, slot_q.reshape(-1).astype(i32), c_slots, c_first, c_last)

    te, tlo = region(jnp.arange(RT, dtype=i32) * TR)
    tvalid = jnp.clip(cnt[te] - tlo, 0, TR)
    last_t = jnp.sum(tiles) - 1
    t_idx = jnp.where(tvalid > 0, jnp.arange(RT, dtype=i32), last_t).astype(i32)
    ffn_plan = (t_idx, te[t_idx].astype(i32), tvalid.astype(i32))
    return pos_cols, pos_rows, gather_plan, combine_plan, ffn_plan, rmax


def _moe_gather_kernel(q_ref, s_ref, slots_ref, first_ref, *refs):
    pos_refs, h_refs, out_ref = refs[:MOE_GG], refs[MOE_GG:2 * MOE_GG], refs[2 * MOE_GG]
    p = pl.program_id(0)
    rows = out_ref.shape[0]

    @pl.when(first_ref[p] == 1)
    def _():
        out_ref[...] = jnp.zeros_like(out_ref)

    for ns in range(1, MOE_GG + 1):
        @pl.when(slots_ref[p] == ns)
        def _(ns=ns):
            row = (lax.broadcasted_iota(jnp.int32, (rows, 1), 0) + q_ref[p] * rows).astype(F32)
            sels = []
            for k in range(ns):
                hit = jnp.logical_or(pos_refs[k][0:1, :] == row, pos_refs[k][1:2, :] == row)
                sels.append(jnp.where(hit, 1.0, 0.0).astype(BF16))
            sel = sels[0] if ns == 1 else jnp.concatenate(sels, axis=1)
            hs = h_refs[0][...] if ns == 1 else jnp.concatenate([h_refs[k][...] for k in range(ns)], axis=0)
            out_ref[...] = out_ref[...] + _dot(sel, hs).astype(BF16)


def _moe_gather(h, pos_rows, plan, rmax):
    N, D = h.shape
    nsteps = plan[0].shape[0]

    def pos_spec(k):
        return pl.BlockSpec((SUB, MOE_SBG), lambda p, q, s, *_: (0, s[MOE_GG * p + k]))

    def tok_spec(k):
        return pl.BlockSpec((MOE_SBG, D), lambda p, q, s, *_: (s[MOE_GG * p + k], 0))

    return pl.pallas_call(
        _moe_gather_kernel,
        grid_spec=pltpu.PrefetchScalarGridSpec(
            num_scalar_prefetch=4, grid=(nsteps,),
            in_specs=[pos_spec(k) for k in range(MOE_GG)] + [tok_spec(k) for k in range(MOE_GG)],
            out_specs=pl.BlockSpec((MOE_TRG, D), lambda p, q, *_: (q[p], 0))),
        out_shape=jax.ShapeDtypeStruct((rmax, D), BF16),
        compiler_params=_params("arbitrary"),
        name="moe_gather",
    )(*plan, *([pos_rows] * MOE_GG), *([h] * MOE_GG))


def _moe_ffn_kernel(t_ref, e_ref, nv_ref, x_ref, w1_ref, w3_ref, w2_ref, out_ref, acc_scr, *, nf):
    t = pl.program_id(0)
    f = pl.program_id(1)
    nv = nv_ref[t]

    def block(start, size):
        rows = pl.ds(start, size)

        @pl.when(f == 0)
        def _():
            acc_scr[rows, :] = jnp.zeros((size, acc_scr.shape[1]), F32)

        x = x_ref[rows, :]
        a = _dot(x, w1_ref[...].astype(BF16))
        b = _dot(x, w3_ref[...].astype(BF16))
        acc_scr[rows, :] += _dot((_silu(a) * b).astype(BF16), w2_ref[...].astype(BF16))

        @pl.when(f == nf - 1)
        def _():
            out_ref[rows, :] = acc_scr[rows, :].astype(out_ref.dtype)

    nsub = MOE_TR // MOE_TRG
    used = (nv + MOE_TRG - 1) // MOE_TRG

    @pl.when(used == nsub)
    def _():
        block(0, MOE_TR)

    @pl.when(jnp.logical_and(used > 0, used < nsub))
    def _():
        start = jnp.int32(0)
        size = MOE_TR // 2
        while size >= MOE_TRG:
            has = (used & (size // MOE_TRG)) != 0

            @pl.when(has)
            def _(start=start, size=size):
                block(pl.multiple_of(start, MOE_TRG), size)

            start = start + jnp.where(has, size, 0)
            size //= 2


def _moe_ffn(xs, plan, w1, w3, w2):
    rmax, D = xs.shape
    FF = w1.shape[2]
    TF = 256
    nf = FF // TF
    RT = rmax // MOE_TR

    def fidx(t, f, nv):
        return jnp.where(nv[t] > 0, f, nf - 1)

    return pl.pallas_call(
        functools.partial(_moe_ffn_kernel, nf=nf),
        grid_spec=pltpu.PrefetchScalarGridSpec(
            num_scalar_prefetch=3, grid=(RT, nf),
            in_specs=[pl.BlockSpec((MOE_TR, D), lambda t, f, ti, e, nv: (ti[t], 0)),
                      pl.BlockSpec((None, D, TF), lambda t, f, ti, e, nv: (e[t], 0, fidx(t, f, nv))),
                      pl.BlockSpec((None, D, TF), lambda t, f, ti, e, nv: (e[t], 0, fidx(t, f, nv))),
                      pl.BlockSpec((None, TF, D), lambda t, f, ti, e, nv: (e[t], fidx(t, f, nv), 0))],
            out_specs=pl.BlockSpec((MOE_TR, D), lambda t, f, ti, e, nv: (ti[t], 0)),
            scratch_shapes=[pltpu.VMEM((MOE_TR, D), F32)]),
        out_shape=jax.ShapeDtypeStruct((rmax, D), BF16),
        compiler_params=_params("arbitrary", "arbitrary"),
        name="moe_ffn",
    )(*plan, xs, w1, w3, w2)


def _moe_combine_kernel(s_ref, q_ref, slots_ref, first_ref, last_ref, pos_ref, *refs, ntp):
    ys_refs = refs[:MOE_CG]
    xp_ref, xs_ref, gate_ref, fg_ref, op_ref, os_ref, acc_scr = refs[MOE_CG:]
    p = pl.program_id(0)
    rows = ys_refs[0].shape[0]

    @pl.when(first_ref[p] == 1)
    def _():
        acc_scr[...] = jnp.zeros_like(acc_scr)

    for ns in range(1, MOE_CG + 1):
        @pl.when(slots_ref[p] == ns)
        def _(ns=ns):
            sels = []
            for k in range(ns):
                col = (lax.broadcasted_iota(jnp.int32, (1, rows), 1) + q_ref[MOE_CG * p + k] * rows).astype(F32)
                sels.append((jnp.where(pos_ref[:, 0:1] == col, pos_ref[:, 2:3], 0.0)
                             + jnp.where(pos_ref[:, 1:2] == col, pos_ref[:, 3:4], 0.0)).astype(BF16))
            sel = sels[0] if ns == 1 else jnp.concatenate(sels, axis=1)
            ys = ys_refs[0][...] if ns == 1 else jnp.concatenate([ys_refs[k][...] for k in range(ns)], axis=0)
            acc_scr[...] += _dot(sel, ys)

    @pl.when(last_ref[p] == 1)
    def _():
        s = s_ref[p]
        x = jnp.where(s < ntp, xp_ref[...], xs_ref[...])
        y = x + gate_ref[...] * acc_scr[...]
        out = y * lax.rsqrt(jnp.mean(y * y, axis=-1, keepdims=True) + EPS) * fg_ref[...]

        @pl.when(s < ntp)
        def _():
            op_ref[...] = out

        @pl.when(s >= ntp)
        def _():
            os_ref[...] = out


def _moe_combine(ys, pos_cols, plan, xp, xs, mod, final_g, rows_per_group):
    Np, D = xp.shape
    Ns = xs.shape[0]
    SB = MOE_SB
    ntp = Np // SB
    nsteps = plan[0].shape[0]

    def tile_spec(k):
        return pl.BlockSpec((MOE_TRG, D), lambda p, s, q, *_: (q[MOE_CG * p + k], 0))

    def tok_p(p, s, *_):
        return (jnp.minimum(s[p], ntp - 1), 0)

    def tok_s(p, s, *_):
        return (jnp.maximum(s[p] - ntp, 0), 0)

    def gate_idx(p, s, *_):
        return (jnp.where(s[p] < ntp, 0, 1 + ((s[p] - ntp) * SB) // rows_per_group), 5, 0, 0)

    return pl.pallas_call(
        functools.partial(_moe_combine_kernel, ntp=ntp),
        grid_spec=pltpu.PrefetchScalarGridSpec(
            num_scalar_prefetch=5, grid=(nsteps,),
            in_specs=[pl.BlockSpec((SB, SUB), lambda p, s, q, *_: (s[p], 0))]
            + [tile_spec(k) for k in range(MOE_CG)]
            + [pl.BlockSpec((SB, D), tok_p),
                      pl.BlockSpec((SB, D), tok_s),
                      pl.BlockSpec((None, None, 1, D), gate_idx),
                      pl.BlockSpec((1, D), lambda p, *_: (0, 0))],
            out_specs=[pl.BlockSpec((SB, D), tok_p), pl.BlockSpec((SB, D), tok_s)],
            scratch_shapes=[pltpu.VMEM((SB, D), F32)]),
        out_shape=[jax.ShapeDtypeStruct((Np, D), F32), jax.ShapeDtypeStruct((Ns, D), F32)],
        compiler_params=_params("arbitrary"),
        name="moe_combine",
    )(*plan, pos_cols, *([ys] * MOE_CG), xp, xs, mod, final_g.reshape(1, D))


def _moe(xp, xs, g, mod, router_w, w1, w3, w2, final_g, rows_per_group):
    N = xp.shape[0] + xs.shape[0]
    h, info, info_t, cum = _moe_route(xp, xs, g, mod, router_w, rows_per_group)
    pos_cols, pos_rows, gather_plan, combine_plan, ffn_plan, rmax = _moe_plan(info, info_t, cum, N)
    x_sorted = _moe_gather(h, pos_rows, gather_plan, rmax)
    y_sorted = _moe_ffn(x_sorted, ffn_plan, w1, w3, w2)
    return _moe_combine(y_sorted, pos_cols, combine_plan, xp, xs, mod, final_g, rows_per_group)


def _gla_levels(C):
    lv, c = [], C // 2
    while c >= SUB:
        lv.append(c)
        c //= 2
    return lv


def _gla_tables(C):
    levels = _gla_levels(C)
    nr = 2 + len(levels)
    mat = np.zeros((2, nr * C, C), np.float32)
    code = np.zeros((2, C, C), np.int32)
    for d in range(2):
        p = np.arange(C) if d == 0 else C - 1 - np.arange(C)
        pi, pj = p[:, None], p[None, :]
        mat[d, 0:C] = pj <= pi
        mat[d, C:2 * C] = pj > pi
        code[d] = np.where((pj <= pi) & (pi // SUB == pj // SUB), 1, 0)
        for lv, c in enumerate(levels):
            blk = pi // c
            later = blk % 2 == 1
            mat[d, (2 + lv) * C:(3 + lv) * C] = ((later & (pj > blk * c - 1) & (pj <= pi))
                                                 | (~later & (pj > pi) & (pj <= (blk + 1) * c - 1)))
            pair = (pi // (2 * c) == pj // (2 * c)) & (pi // c != pj // c) & (pj <= pi)
            code[d] = np.where(pair, 2 + lv, code[d])
    ones = np.zeros((SUB * LANES, C), np.float32)
    for jj in range(SUB):
        ones[jj * LANES:(jj + 1) * LANES, jj::SUB] = 1.0
    return jnp.asarray(mat, dtype=BF16), jnp.asarray(code), jnp.asarray(ones, dtype=BF16)


def _bcast_sublane(x, jj):
    r, w = x.shape
    x3 = x.reshape(r // SUB, SUB, w)
    return jnp.broadcast_to(x3[:, jj:jj + 1, :], x3.shape).reshape(r, w)


def _t128(x):
    r, w = x.shape
    if w > LANES:
        return jnp.concatenate([x[:, i:i + LANES].T for i in range(0, w, LANES)], axis=0)
    return jnp.concatenate([x[i:i + LANES, :].T for i in range(0, r, LANES)], axis=1)


def _gla_kernel(q_ref, k_ref, v_ref, g_ref, lr_ref, wg_ref, ba_ref, mat_ref, code_ref, ones_ref,
                s0f_ref, s0b_ref, ng_ref, o_ref, sf_ref, sb_ref, st_scr, of_scr, *, n, C, G):
    d = pl.program_id(1)
    c = pl.program_id(2)
    levels = _gla_levels(C)

    @pl.when(jnp.logical_and(c == 0, d == 0))
    def _():
        for bb in range(G):
            for h in range(C_H):
                st_scr[bb, h] = _t128(s0f_ref[bb, h])

    @pl.when(jnp.logical_and(c == 0, d == 1))
    def _():
        for bb in range(G):
            for h in range(C_H):
                st_scr[bb, h] = _t128(s0b_ref[bb, h])

    mat = mat_ref[...]
    code = code_ref[...]
    ones = ones_ref[...]
    wg = _split(wg_ref[...])
    cums = []
    for bb in range(G):
        xg = _dot3(_split(lr_ref[bb]), wg) + ba_ref[...]
        la = (jnp.minimum(xg, 0.0) - jnp.log1p(jnp.exp(-jnp.abs(xg)))) * (LOG2E / C_TAU)
        hi, lo = _split(la)
        cums.append(_dot(mat, hi) + _dot(mat, lo))

    def prepare(bb, h):
        cum = cums[bb]
        ks = slice(h * C_DK, (h + 1) * C_DK)
        qh = q_ref[bb, :, ks].astype(F32) * (C_DK ** -0.5)
        kh = k_ref[bb, :, ks].astype(F32)
        b = cum[0:C, ks]
        b_rest = cum[C:2 * C, ks]
        ps = []
        for jj in range(SUB):
            dec = jnp.exp2(jnp.minimum(b - _bcast_sublane(b, jj), 0.0))
            ps.append((qh * _bcast_sublane(kh, jj) * dec).astype(BF16))
        lv_ops = []
        for lv in range(len(levels)):
            fac = jnp.exp2(cum[(2 + lv) * C:(3 + lv) * C, ks])
            lv_ops.append(((qh * fac).astype(BF16), (kh * fac).astype(BF16)))
        qe = (qh * jnp.exp2(b)).astype(BF16)
        ke = (kh * jnp.exp2(b_rest)).astype(BF16)
        e_end = jnp.exp2(b[0:1, :] + b_rest[0:1, :])
        return jnp.concatenate(ps, axis=1), lv_ops, qe, ke, e_end

    def contract(bb, h, prep):
        pcat, lv_ops, qe, ke, e_end = prep
        vh = v_ref[bb, :, h * C_DV:(h + 1) * C_DV].astype(F32)
        att = jnp.where(code == 1, _dot(pcat, ones), 0.0)
        for lv, (qs, ks_) in enumerate(lv_ops):
            att = jnp.where(code == 2 + lv, _dot_nt(qs, ks_), att)
        st = st_scr[bb, h]
        o = _dot(att.astype(BF16), vh.astype(BF16)) + _dot_nt(qe, st.astype(BF16))
        st_scr[bb, h] = e_end * st + _dot(_t128(vh).astype(BF16), ke)
        return o

    units = [(bb, h) for h in range(C_H) for bb in range(G)]
    outs = {}
    prep = prepare(*units[0])
    for idx, (bb, h) in enumerate(units):
        nxt = prepare(*units[idx + 1]) if idx + 1 < len(units) else None
        outs[(bb, h)] = contract(bb, h, prep)
        prep = nxt
    o_all = [jnp.concatenate([outs[(bb, h)] for h in range(C_H)], axis=-1) for bb in range(G)]

    @pl.when(d == 0)
    def _():
        for bb in range(G):
            of_scr[bb, c] = o_all[bb]

    @pl.when(d == 1)
    def _():
        for bb in range(G):
            tot = o_all[bb] + of_scr[bb, n - 1 - c]
            res = []
            for h in range(C_H):
                sl = slice(h * C_DV, (h + 1) * C_DV)
                t = tot[:, sl]
                y = t * lax.rsqrt(jnp.mean(t * t, axis=-1, keepdims=True) + EPS) * ng_ref[:, sl]
                res.append(y * _silu(g_ref[bb, :, sl].astype(F32)))
            o_ref[bb] = jnp.concatenate(res, axis=-1).astype(o_ref.dtype)

    @pl.when(jnp.logical_and(c == n - 1, d == 0))
    def _():
        for bb in range(G):
            for h in range(C_H):
                sf_ref[bb, h] = _t128(st_scr[bb, h])

    @pl.when(jnp.logical_and(c == n - 1, d == 1))
    def _():
        for bb in range(G):
            for h in range(C_H):
                sb_ref[bb, h] = _t128(st_scr[bb, h])


def _gla(z, zg, B, T, w_a2, b_a, s0f, s0b, norm_g):
    C = GLA_CHUNK
    G = _rows_per_step(B, T, C_H * C_DV)
    assert B % G == 0 and T % C == 0
    n = T // C
    HK = C_H * C_DK
    HV = C_H * C_DV
    mat, code, ones = _gla_tables(C)
    nr = mat.shape[1] // C
    wg = jnp.zeros((2, LANES, HK), F32)
    for dr in range(2):
        wg = wg.at[dr, dr * C_RANK:(dr + 1) * C_RANK, :].set(w_a2[dr])
    z3 = z.reshape(B, T, z.shape[1])
    zg3 = zg.reshape(B, T, zg.shape[1])

    def chunk(d, c):
        return c + d * (n - 1 - 2 * c)

    st_spec = pl.BlockSpec((G, C_H, C_DK, C_DV), lambda b, d, c: (b, 0, 0, 0))
    st_shape = jax.ShapeDtypeStruct((B, C_H, C_DK, C_DV), F32)
    o, sf, sb = pl.pallas_call(
        functools.partial(_gla_kernel, n=n, C=C, G=G),
        grid=(B // G, 2, n),
        in_specs=[pl.BlockSpec((G, C, HK), lambda b, d, c: (b, chunk(d, c), 0)),
                  pl.BlockSpec((G, C, HK), lambda b, d, c: (b, chunk(d, c), 1)),
                  pl.BlockSpec((G, C, HV), lambda b, d, c: (b, chunk(d, c), 1)),
                  pl.BlockSpec((G, C, HV), lambda b, d, c: (b, chunk(d, c), 2)),
                  pl.BlockSpec((G, C, LANES), lambda b, d, c: (b, chunk(d, c), 0)),
                  pl.BlockSpec((None, LANES, HK), lambda b, d, c: (d, 0, 0)),
                  pl.BlockSpec((None, 1, HK), lambda b, d, c: (d, 0, 0)),
                  pl.BlockSpec((None, nr * C, C), lambda b, d, c: (d, 0, 0)),
                  pl.BlockSpec((None, C, C), lambda b, d, c: (d, 0, 0)),
                  pl.BlockSpec((SUB * LANES, C), lambda b, d, c: (0, 0)),
                  st_spec, st_spec,
                  pl.BlockSpec((1, HV), lambda b, d, c: (0, 0))],
        out_specs=[pl.BlockSpec((G, C, HV), lambda b, d, c: (b, (n - 1) - d * c, 0)),
                   st_spec, st_spec],
        out_shape=[jax.ShapeDtypeStruct((B, T, HV), BF16), st_shape, st_shape],
        scratch_shapes=[pltpu.VMEM((G, C_H, C_DV, C_DK), F32), pltpu.VMEM((G, n, C, HV), F32)],
        compiler_params=_params("arbitrary", "arbitrary", "arbitrary"),
        name="gla",
    )(z3, z3, z3, z3, zg3, wg, b_a.reshape(2, 1, HK), mat, code, ones, s0f, s0b, norm_g.reshape(1, HV))
    return o.reshape(B * T, HV), sf, sb


def _run_stream(x, B, T, mods, ctx, p):
    N, D = x.shape
    rpg = N // mods[0].shape[0]
    TM = min(2048, rpg)
    nb = (B_H + 2 * B_HKV) * B_HD

    w_in = p['even_w_in'][0]
    z, zb = _norm_mm(x, p['norm1_g'][0], mods[0], (0, 1), w_in, w_in, (nb, MIX_MAIN // nb), TM, rpg)
    if ctx is None:
        s0 = jnp.zeros((B, A_H, A_DK, A_DV), F32)
        a_f0, a_b0, cache_k, cache_v = s0, s0, None, None
    else:
        cache_k, cache_v, a_f0, a_b0 = ctx[0], ctx[1], ctx[2], ctx[3]
    o_a, a_sf, a_sb = _retention(z, B, T, p['a_log_gamma'][0], a_f0, a_b0, p['a_norm_g'][0])
    qpad, k_norm, k_rot, v_bf = _bprep(zb, T, p['b_q_g'][0], p['b_k_g'][0], rope=ctx is not None)
    o_b = _attention(qpad, k_rot, v_bf, B, T, cache_k, cache_v)
    x = _proj_res(x, mods[0], 2, [o_a, o_b], p['even_w_out'][0], rpg)
    x = _ffn(x, p['norm2_g'][0], mods[0], p['ff_w1'][0], p['ff_w3'][0], p['ff_w2'][0], rpg)

    w_in = p['odd_w_in'][0]
    w_gate = jnp.pad(w_in[:, MIX_MAIN:], ((0, 0), (0, LANES - 2 * C_RANK)))
    z1, z1g = _norm_mm(x, p['norm1_g'][1], mods[1], (0, 1), w_in, w_gate, (LANES, 0), TM, rpg)
    if ctx is None:
        s0 = jnp.zeros((B, C_H, C_DK, C_DV), F32)
        c_f0, c_b0 = s0, s0
    else:
        c_f0, c_b0 = ctx[4], ctx[5]
    o_c, c_sf, c_sb = _gla(z1, z1g, B, T, p['c_w_a2'][0], p['c_b_a'][0], c_f0, c_b0, p['c_norm_g'][0])
    x = _proj_res(x, mods[1], 2, [o_c], p['odd_w_out'][0], rpg)
    v_raw = zb[:, (B_H + B_HKV) * B_HD:]
    return x, (k_norm, v_raw, a_sf, a_sb, c_sf, c_sb)


def kernel(x_prompt, x_sample, c, cache_b_k, cache_b_v, state_a_fwd, state_a_bwd, state_c_fwd, state_c_bwd,
           c_ctx, w_mod, b_mod, norm1_g, norm2_g, final_g, even_w_in, even_w_out, a_log_gamma, a_norm_g,
           b_q_g, b_k_g, odd_w_in, c_w_a2, c_b_a, c_norm_g, odd_w_out, ff_w1, ff_w3, ff_w2,
           router_w, moe_w1, moe_w3, moe_w2):
    Bp, Tp, D = x_prompt.shape
    Bs, Ts, _ = x_sample.shape
    L = w_mod.shape[0]
    assert L == 2 and even_w_in.shape[0] == 1 and odd_w_in.shape[0] == 1
    p = dict(norm1_g=norm1_g, norm2_g=norm2_g, final_g=final_g, even_w_in=even_w_in, even_w_out=even_w_out,
             a_log_gamma=a_log_gamma, a_norm_g=a_norm_g, b_q_g=b_q_g, b_k_g=b_k_g, odd_w_in=odd_w_in,
             c_w_a2=c_w_a2, c_b_a=c_b_a, c_norm_g=c_norm_g, odd_w_out=odd_w_out, ff_w1=ff_w1, ff_w3=ff_w3,
             ff_w2=ff_w2, router_w=router_w, moe_w1=moe_w1, moe_w3=moe_w3, moe_w2=moe_w2)

    rows = 8
    conds = jnp.concatenate([c_ctx[None, :], c, jnp.zeros((rows - 1 - Bs, D), F32)], axis=0)
    mod = _modulation(conds, w_mod, b_mod).reshape(L, rows, 6, 1, D)
    mods_p = [mod[l, 0:1] for l in range(L)]
    mods_s = [mod[l, 1:1 + Bs] for l in range(L)]

    x_p, kept = _run_stream(x_prompt.reshape(Bp * Tp, D), Bp, Tp, mods_p, None, p)
    nk = B_HKV * B_HD
    ctx = (cache_b_k[:, 0].reshape(Bs, -1, nk), cache_b_v[:, 0].reshape(Bs, -1, nk),
           state_a_fwd[:, 0], state_a_bwd[:, 0], state_c_fwd[:, 0], state_c_bwd[:, 0])
    x_s, _ = _run_stream(x_sample.reshape(Bs * Ts, D), Bs, Ts, mods_s, ctx, p)
    y_p, y_s = _moe(x_p, x_s, norm2_g[1], mod[1, 0:1 + Bs], router_w[0], moe_w1[0], moe_w3[0], moe_w2[0],
                    final_g, Ts)

    k_norm, v_raw, a_sf, a_sb, c_sf, c_sb = kept
    return (y_p.reshape(Bp, Tp, D), y_s.reshape(Bs, Ts, D),
            k_norm.reshape(Bp, 1, Tp, B_HKV, B_HD), v_raw.reshape(Bp, 1, Tp, B_HKV, B_HD),
            a_sf[:, None], a_sb[:, None], c_sf[:, None], c_sb[:, None])
```

```python
import functools

import numpy as np
import jax
import jax.numpy as jnp
from jax import lax
from jax.experimental import pallas as pl
from jax.experimental.pallas import tpu as pltpu

F32 = jnp.float32
BF16 = jnp.bfloat16
EPS = 1e-6
LOG2E = 1.4426950408889634

VMEM_LIMIT_BYTES = 56 * 1024 * 1024

A_H, A_DK, A_DV = 4, 128, 256
B_H, B_HKV, B_HD = 8, 2, 64
C_H, C_DK, C_DV, C_RANK = 4, 128, 256, 16
C_TAU = 16.0
GRID_W = 64
ROPE_THETA = 10000.0
N_EXPERTS = 8
LANES = 128
SUB = 8
RET_CHUNK = 128
GLA_CHUNK = 128
Q_TILE = 128
SCAN_ROWS_MAX = 4
SCAN_FWD_BYTES = 32 * 1024 * 1024


def _rows_per_step(B, T, width):
    g = SCAN_ROWS_MAX
    while g > 1 and (B % g or g * T * width * 4 > SCAN_FWD_BYTES):
        g //= 2
    return g


def _params(*sem):
    return pltpu.CompilerParams(dimension_semantics=sem, vmem_limit_bytes=VMEM_LIMIT_BYTES)


def _dot(a, b):
    return jnp.dot(a, b, preferred_element_type=F32)


def _dot_nt(a, b):
    return lax.dot_general(a, b, (((1,), (1,)), ((), ())), preferred_element_type=F32)


def _split(x):
    hi = x.astype(BF16)
    return hi, (x - hi.astype(F32)).astype(BF16)


def _dot3(a, b):
    return _dot(a[0], b[0]) + (_dot(a[0], b[1]) + _dot(a[1], b[0]))


def _silu(x):
    return x * jax.nn.sigmoid(x)


def _norm_mod(x, g, sh, sc):
    r = lax.rsqrt(jnp.mean(x * x, axis=-1, keepdims=True) + EPS)
    return (x * r * g) * (1.0 + sc) + sh


def _mod_kernel(c_ref, w_ref, b_ref, o_ref):
    o_ref[...] = _dot3(_split(_silu(c_ref[...])), _split(w_ref[...])) + b_ref[...]


def _modulation(conds, w_mod, b_mod):
    L, D, D6 = w_mod.shape
    R = conds.shape[0]
    TN = 1024
    return pl.pallas_call(
        _mod_kernel,
        grid=(L, D6 // TN),
        in_specs=[pl.BlockSpec((R, D), lambda l, j: (0, 0)),
                  pl.BlockSpec((None, D, TN), lambda l, j: (l, 0, j)),
                  pl.BlockSpec((None, 1, TN), lambda l, j: (l, 0, j))],
        out_specs=pl.BlockSpec((None, R, TN), lambda l, j: (l, 0, j)),
        out_shape=jax.ShapeDtypeStruct((L, R, D6), F32),
        compiler_params=_params("arbitrary", "arbitrary"),
        name="modulation",
    )(conds, w_mod, b_mod.reshape(L, 1, D6))


def _mod_spec(part, D, TM, rows_per_group, axis):
    def idx(*g):
        return ((g[axis] * TM) // rows_per_group, part, 0, 0)
    return pl.BlockSpec((None, None, 1, D), idx)


MIX_MAIN = A_H * (2 * A_DK + 2 * A_DV)
MIX_TN = 768


def _norm_mm_kernel(x_ref, g_ref, sh_ref, sc_ref, w_ref, we_ref, o_ref, oe_ref, h_scr, *, nmain):
    j = pl.program_id(1)

    @pl.when(j == 0)
    def _():
        h_scr[...] = _norm_mod(x_ref[...], g_ref[...], sh_ref[...], sc_ref[...]).astype(BF16)

    @pl.when(j < nmain)
    def _():
        o_ref[...] = _dot(h_scr[...], w_ref[...].astype(BF16)).astype(o_ref.dtype)

    @pl.when(j == nmain)
    def _():
        oe_ref[...] = _dot(h_scr[...], we_ref[...].astype(BF16))


def _norm_mm(x, g, mod, parts, w, w_extra, extra_block, TM, rows_per_group):
    N, D = x.shape
    nmain = MIX_MAIN // MIX_TN
    WE = extra_block[0]
    return pl.pallas_call(
        functools.partial(_norm_mm_kernel, nmain=nmain),
        grid=(N // TM, nmain + 1),
        in_specs=[pl.BlockSpec((TM, D), lambda i, j: (i, 0)),
                  pl.BlockSpec((1, D), lambda i, j: (0, 0)),
                  _mod_spec(parts[0], D, TM, rows_per_group, 0),
                  _mod_spec(parts[1], D, TM, rows_per_group, 0),
                  pl.BlockSpec((D, MIX_TN), lambda i, j: (0, jnp.minimum(j, nmain - 1))),
                  pl.BlockSpec((D, WE), lambda i, j: (0, extra_block[1]))],
        out_specs=[pl.BlockSpec((TM, MIX_TN), lambda i, j: (i, jnp.minimum(j, nmain - 1))),
                   pl.BlockSpec((TM, WE), lambda i, j: (i, 0))],
        out_shape=[jax.ShapeDtypeStruct((N, MIX_MAIN), BF16), jax.ShapeDtypeStruct((N, WE), F32)],
        scratch_shapes=[pltpu.VMEM((TM, D), BF16)],
        compiler_params=_params("arbitrary", "arbitrary"),
        name="norm_mm",
    )(x, g.reshape(1, D), mod, mod, w, w_extra)


def _ret_kernel(lg_ref, q_ref, k_ref, v_ref, ag_ref, s0f_ref, s0b_ref, ng_ref,
                o_ref, sf_ref, sb_ref, s_scr, of_scr, *, n, C, G):
    d = pl.program_id(1)
    c = pl.program_id(2)

    @pl.when(jnp.logical_and(c == 0, d == 0))
    def _():
        s_scr[...] = s0f_ref[...]

    @pl.when(jnp.logical_and(c == 0, d == 1))
    def _():
        s_scr[...] = s0b_ref[...]

    df = d.astype(F32)
    sgn = 1.0 - 2.0 * df
    ii = lax.broadcasted_iota(jnp.int32, (C, C), 0).astype(F32)
    jj = lax.broadcasted_iota(jnp.int32, (C, C), 1).astype(F32)
    dd = (ii - jj) * sgn
    feeds = dd >= 0.0
    ddc = jnp.maximum(dd, 0.0)
    ri = lax.broadcasted_iota(jnp.int32, (C, 1), 0).astype(F32)
    pos_q = (ri + 1.0) + df * (C - 2.0 * ri - 1.0)
    pos_k = (C - 1.0 - ri) + df * (2.0 * ri - C + 1.0)
    chunk_len = jnp.full((1, A_DV), float(C), F32)

    outs = [[] for _ in range(G)]
    for h in range(A_H):
        lg = lg_ref[d, h]
        dmask = jnp.where(feeds, jnp.exp2(lg * ddc), 0.0)
        q_dec = jnp.exp2(lg * pos_q)
        k_dec = jnp.exp2(lg * pos_k)
        c_dec = jnp.exp2(lg * chunk_len)
        for bb in range(G):
            qh = q_ref[bb, :, h * A_DK:(h + 1) * A_DK].astype(F32) * (A_DK ** -0.5)
            kh = k_ref[bb, :, h * A_DK:(h + 1) * A_DK].astype(F32)
            vh = v_ref[bb, :, h * A_DV:(h + 1) * A_DV].astype(BF16)
            s = s_scr[bb, h]
            att = _dot_nt(qh.astype(BF16), kh.astype(BF16)) * dmask
            o = _dot(att.astype(BF16), vh) + _dot((qh * q_dec).astype(BF16), s.astype(BF16))
            kd = kh * k_dec
            s_scr[bb, h] = c_dec * s + _dot(kd.T.astype(BF16), vh)
            outs[bb].append(o)
    o_all = [jnp.concatenate(o, axis=-1) for o in outs]

    @pl.when(d == 0)
    def _():
        for bb in range(G):
            of_scr[bb, c] = o_all[bb]

    @pl.when(d == 1)
    def _():
        for bb in range(G):
            tot = o_all[bb] + of_scr[bb, n - 1 - c]
            res = []
            for h in range(A_H):
                sl = slice(h * A_DV, (h + 1) * A_DV)
                t = tot[:, sl]
                dev = t - jnp.mean(t, axis=-1, keepdims=True)
                y = dev * lax.rsqrt(jnp.mean(dev * dev, axis=-1, keepdims=True) + EPS) * ng_ref[:, sl]
                res.append(y * _silu(ag_ref[bb, :, sl].astype(F32)))
            o_ref[bb] = jnp.concatenate(res, axis=-1).astype(o_ref.dtype)

    @pl.when(jnp.logical_and(c == n - 1, d == 0))
    def _():
        sf_ref[...] = s_scr[...]

    @pl.when(jnp.logical_and(c == n - 1, d == 1))
    def _():
        sb_ref[...] = s_scr[...]


def _retention(z, B, T, log_gamma, s0f, s0b, norm_g):
    C = RET_CHUNK
    G = _rows_per_step(B, T, A_H * A_DV)
    assert B % G == 0 and T % C == 0
    n = T // C
    HK = A_H * A_DK
    HV = A_H * A_DV
    z3 = z.reshape(B, T, z.shape[1])

    def chunk(d, c):
        return c + d * (n - 1 - 2 * c)

    st_spec = pl.BlockSpec((G, A_H, A_DK, A_DV), lambda b, d, c: (b, 0, 0, 0))
    st_shape = jax.ShapeDtypeStruct((B, A_H, A_DK, A_DV), F32)
    o, sf, sb = pl.pallas_call(
        functools.partial(_ret_kernel, n=n, C=C, G=G),
        grid=(B // G, 2, n),
        in_specs=[pl.BlockSpec(memory_space=pltpu.SMEM),
                  pl.BlockSpec((G, C, HK), lambda b, d, c: (b, chunk(d, c), 0)),
                  pl.BlockSpec((G, C, HK), lambda b, d, c: (b, chunk(d, c), 1)),
                  pl.BlockSpec((G, C, HV), lambda b, d, c: (b, chunk(d, c), 1)),
                  pl.BlockSpec((G, C, HV), lambda b, d, c: (b, chunk(d, c), 2)),
                  st_spec, st_spec,
                  pl.BlockSpec((1, HV), lambda b, d, c: (0, 0))],
        out_specs=[pl.BlockSpec((G, C, HV), lambda b, d, c: (b, (n - 1) - d * c, 0)),
                   st_spec, st_spec],
        out_shape=[jax.ShapeDtypeStruct((B, T, HV), BF16), st_shape, st_shape],
        scratch_shapes=[pltpu.VMEM((G, A_H, A_DK, A_DV), F32), pltpu.VMEM((G, n, C, HV), F32)],
        compiler_params=_params("arbitrary", "arbitrary", "arbitrary"),
        name="retention",
    )(log_gamma * LOG2E, z3, z3, z3, z3, s0f, s0b, norm_g.reshape(1, HV))
    return o.reshape(B * T, HV), sf, sb


def _group_sum_matrix(width, group):
    i = np.arange(width)
    return jnp.asarray((i[:, None] // group == i[None, :] // group).astype(np.float32), dtype=BF16)


def _q_pad_matrix():
    m = np.zeros((B_H * B_HD, B_H * LANES), np.float32)
    g = B_H // B_HKV
    for h in range(B_H):
        for t in range(B_HD):
            m[h * B_HD + t, h * LANES + (h // g) * B_HD + t] = 1.0
    return jnp.asarray(m, dtype=BF16)


def _rope_tables(T):
    rows = T // GRID_W
    row = np.repeat(np.arange(rows, dtype=np.float64), GRID_W)
    col = np.tile(np.arange(GRID_W, dtype=np.float64), rows)
    nq = B_HD // 4
    inv = ROPE_THETA ** (-np.arange(nq, dtype=np.float64) / nq)
    ang = np.concatenate([row[:, None] * inv, col[:, None] * inv], axis=-1)
    cos = np.repeat(np.cos(ang), 2, axis=-1)
    sin = np.repeat(np.sin(ang), 2, axis=-1)
    sign = np.tile(np.array([-1.0, 1.0]), B_HD // 2)
    reps = LANES // B_HD
    return (jnp.asarray(np.tile(cos, (1, reps)), dtype=F32),
            jnp.asarray(np.tile(sin * sign, (1, reps)), dtype=F32))


def _group_rmsnorm(x, gsum, g):
    hi, lo = _split(x * x)
    ss = _dot(hi, gsum) + _dot(lo, gsum)
    return x * lax.rsqrt(ss * (1.0 / B_HD) + EPS) * g


def _rotate_pairs(x, cos, sin_signed):
    n = x.shape[1]
    lane = lax.broadcasted_iota(jnp.int32, x.shape, 1)
    partner = jnp.where(lane % 2 == 0, pltpu.roll(x, n - 1, 1), pltpu.roll(x, 1, 1))
    reps = n // LANES
    if reps > 1:
        cos = jnp.concatenate([cos] * reps, axis=1)
        sin_signed = jnp.concatenate([sin_signed] * reps, axis=1)
    return x * cos + partner * sin_signed


def _bprep_kernel(z_ref, qg_ref, kg_ref, cos_ref, sin_ref, gq_ref, gk_ref, pad_ref,
                  qpad_ref, kn_ref, kr_ref, vb_ref, *, rope):
    nq = B_H * B_HD
    nk = B_HKV * B_HD
    qn = _group_rmsnorm(z_ref[:, 0:nq], gq_ref[...], qg_ref[...])
    kn = _group_rmsnorm(z_ref[:, nq:nq + nk], gk_ref[...], kg_ref[...])
    kn_ref[...] = kn
    if rope:
        qn = _rotate_pairs(qn, cos_ref[...], sin_ref[...])
        kn = _rotate_pairs(kn, cos_ref[...], sin_ref[...])
    kr_ref[...] = kn.astype(BF16)
    vb_ref[...] = z_ref[:, nq + nk:nq + 2 * nk].astype(BF16)
    qs = (qn * (B_HD ** -0.5 * LOG2E)).astype(BF16)
    qpad_ref[...] = _dot(qs, pad_ref[...]).astype(BF16)


def _bprep(z, T, q_g, k_g, rope):
    N = z.shape[0]
    TM = min(512, T)
    nq = B_H * B_HD
    nk = B_HKV * B_HD
    width = nq + 2 * nk
    assert z.shape[1] == width
    cos, sin = _rope_tables(T if rope else TM)
    nt = T // TM if rope else 1
    const = lambda i: (0, 0)
    return pl.pallas_call(
        functools.partial(_bprep_kernel, rope=rope),
        grid=(N // TM,),
        in_specs=[pl.BlockSpec((TM, width), lambda i: (i, 0)),
                  pl.BlockSpec((1, nq), const),
                  pl.BlockSpec((1, nk), const),
                  pl.BlockSpec((TM, LANES), lambda i: (i % nt, 0)),
                  pl.BlockSpec((TM, LANES), lambda i: (i % nt, 0)),
                  pl.BlockSpec((nq, nq), const),
                  pl.BlockSpec((nk, nk), const),
                  pl.BlockSpec((nq, B_H * LANES), const)],
        out_specs=[pl.BlockSpec((TM, B_H * LANES), lambda i: (i, 0)),
                   pl.BlockSpec((TM, nk), lambda i: (i, 0)),
                   pl.BlockSpec((TM, nk), lambda i: (i, 0)),
                   pl.BlockSpec((TM, nk), lambda i: (i, 0))],
        out_shape=[jax.ShapeDtypeStruct((N, B_H * LANES), BF16),
                   jax.ShapeDtypeStruct((N, nk), F32),
                   jax.ShapeDtypeStruct((N, nk), BF16),
                   jax.ShapeDtypeStruct((N, nk), BF16)],
        compiler_params=_params("arbitrary"),
        name="attn_prep",
    )(z, jnp.tile(q_g, B_H).reshape(1, nq), jnp.tile(k_g, B_HKV).reshape(1, nk), cos, sin,
      _group_sum_matrix(nq, B_HD), _group_sum_matrix(nk, B_HD), _q_pad_matrix())


def _lane_fold(x, op):
    acc = x[:, 0:LANES]
    for j in range(1, x.shape[1] // LANES):
        acc = op(acc, x[:, j * LANES:(j + 1) * LANES])
    return acc


def _attn_kernel(*refs, has_cache, kc, nq):
    if has_cache:
        q_ref, k_ref, v_ref, ck_ref, cv_ref, o_ref, s_scr, m_scr, mprev_scr, l_scr, acc_scr = refs
        kcc = min(kc, ck_ref.shape[0])
        ncache = ck_ref.shape[0] // kcc
    else:
        q_ref, k_ref, v_ref, o_ref, s_scr, m_scr, mprev_scr, l_scr, acc_scr = refs
        kcc, ncache = kc, 0
    i = pl.program_id(1)
    tq = q_ref.shape[0]
    nlat = k_ref.shape[0] // kc

    def score(c, kblk):
        q = jnp.concatenate([q_ref[:, h * LANES:(h + 1) * LANES] for h in range(B_H)], axis=0)
        s = _dot_nt(q, kblk)
        s_scr[c, :, 0:kblk.shape[0]] = s
        m_scr[...] = jnp.maximum(m_scr[...], _lane_fold(s, jnp.maximum))

    def weight(c, vblk):
        s = s_scr[c, :, 0:vblk.shape[0]]
        mp = mprev_scr[...]
        ps = [jnp.exp2(s[:, j * LANES:(j + 1) * LANES] - mp) for j in range(vblk.shape[0] // LANES)]
        tot = ps[0]
        for pj in ps[1:]:
            tot = tot + pj
        l_scr[...] += tot
        acc_scr[...] += _dot(jnp.concatenate(ps, axis=1).astype(BF16), vblk)

    def run(do_weight, do_score):
        def unit(c, kblk, vblk):
            if do_weight:
                weight(c, vblk())
            if do_score:
                score(c, kblk())

        for c in range(ncache):
            unit(c, lambda: ck_ref[c * kcc:(c + 1) * kcc, :].astype(BF16),
                 lambda: cv_ref[c * kcc:(c + 1) * kcc, :].astype(BF16))

        def body(c, carry):
            rows = pl.ds(pl.multiple_of(c * kc, kc), kc)
            unit(ncache + c, lambda: k_ref[rows, :], lambda: v_ref[rows, :])
            return carry
        lax.fori_loop(0, nlat, body, 0)

    @pl.when(i < nq)
    def _():
        m_scr[...] = jnp.full(m_scr.shape, -jnp.inf, F32)

    @pl.when(i > 0)
    def _():
        l_scr[...] = jnp.zeros_like(l_scr)
        acc_scr[...] = jnp.zeros_like(acc_scr)

    @pl.when(i == 0)
    def _():
        run(False, True)

    @pl.when(jnp.logical_and(i > 0, i < nq))
    def _():
        run(True, True)

    @pl.when(i == nq)
    def _():
        run(True, False)

    @pl.when(i > 0)
    def _():
        r_all = acc_scr[...] / jnp.sum(l_scr[...], axis=-1, keepdims=True)
        g = B_H // B_HKV
        lane = lax.broadcasted_iota(jnp.int32, (tq, LANES), 1)
        outs = []
        for j in range(B_H // 2):
            pair = []
            for half in range(2):
                h = 2 * j + half
                r = r_all[h * tq:(h + 1) * tq, :]
                if h // g != half:
                    r = pltpu.roll(r, B_HD, 1)
                pair.append(r)
            outs.append(jnp.where(lane < B_HD, pair[0], pair[1]))
        o_ref[...] = jnp.concatenate(outs, axis=-1).astype(o_ref.dtype)

    @pl.when(i < nq)
    def _():
        mprev_scr[...] = jnp.broadcast_to(jnp.max(m_scr[...], axis=-1, keepdims=True), mprev_scr.shape)


def _attention(qpad, kr, vb, B, T, cache_k, cache_v):
    has_cache = cache_k is not None
    TQ = Q_TILE
    nq = T // TQ
    nk = B_HKV * B_HD
    in_specs = [pl.BlockSpec((TQ, B_H * LANES), lambda b, i: (b * nq + jnp.minimum(i, nq - 1), 0)),
                pl.BlockSpec((T, nk), lambda b, i: (b, 0)),
                pl.BlockSpec((T, nk), lambda b, i: (b, 0))]
    args = [qpad, kr, vb]
    kc = min(1024, T)
    nchunks = T // kc
    if has_cache:
        P = cache_k.shape[1]
        assert P % min(kc, P) == 0
        nchunks += P // min(kc, P)
        in_specs += [pl.BlockSpec((None, P, nk), lambda b, i: (b, 0, 0))] * 2
        args += [cache_k, cache_v]
    R = B_H * TQ
    return pl.pallas_call(
        functools.partial(_attn_kernel, has_cache=has_cache, kc=kc, nq=nq),
        grid=(B, nq + 1),
        in_specs=in_specs,
        out_specs=pl.BlockSpec((TQ, B_H * B_HD), lambda b, i: (b * nq + jnp.maximum(i - 1, 0), 0)),
        out_shape=jax.ShapeDtypeStruct((B * T, B_H * B_HD), BF16),
        scratch_shapes=[pltpu.VMEM((nchunks, R, kc), F32)] + [pltpu.VMEM((R, LANES), F32)] * 4,
        compiler_params=_params("arbitrary", "arbitrary"),
        name="attention",
    )(*args)


def _proj_res_kernel(*refs, n_in):
    x_ref, gate_ref = refs[0], refs[1]
    o_refs = refs[2:2 + n_in]
    w_refs = refs[2 + n_in:2 + 2 * n_in]
    out_ref = refs[2 + 2 * n_in]
    wbf_refs = refs[3 + 2 * n_in:]

    @pl.when(pl.program_id(0) == 0)
    def _():
        for w_ref, wbf_ref in zip(w_refs, wbf_refs):
            wbf_ref[...] = w_ref[...].astype(BF16)

    acc = _dot(o_refs[0][...], wbf_refs[0][...])
    for o_ref, wbf_ref in zip(o_refs[1:], wbf_refs[1:]):
        acc = acc + _dot(o_ref[...], wbf_ref[...])
    out_ref[...] = x_ref[...] + gate_ref[...] * acc


def _proj_res(x, mod, part, acts, w, rows_per_group):
    N, D = x.shape
    TM = min(1024, rows_per_group)
    n_in = len(acts)
    widths = [a.shape[1] for a in acts]
    offs = np.cumsum([0] + widths[:-1]).tolist()
    in_specs = [pl.BlockSpec((TM, D), lambda i: (i, 0)),
                _mod_spec(part, D, TM, rows_per_group, 0)]
    in_specs += [pl.BlockSpec((TM, wd), lambda i: (i, 0)) for wd in widths]
    in_specs += [pl.BlockSpec((wd, D), functools.partial(lambda i, blk: (blk, 0), blk=off // wd))
                 for wd, off in zip(widths, offs)]
    return pl.pallas_call(
        functools.partial(_proj_res_kernel, n_in=n_in),
        grid=(N // TM,),
        in_specs=in_specs,
        out_specs=pl.BlockSpec((TM, D), lambda i: (i, 0)),
        out_shape=jax.ShapeDtypeStruct((N, D), F32),
        scratch_shapes=[pltpu.VMEM((wd, D), BF16) for wd in widths],
        compiler_params=_params("arbitrary"),
        name="proj_residual",
    )(x, mod, *acts, *([w] * n_in))


def _ffn_kernel(x_ref, g_ref, sh_ref, sc_ref, gate_ref, w1_ref, w3_ref, w2_ref, out_ref, h_scr, acc_scr, *, nf):
    f = pl.program_id(1)

    @pl.when(f == 0)
    def _():
        h_scr[...] = _norm_mod(x_ref[...], g_ref[...], sh_ref[...], sc_ref[...]).astype(BF16)
        acc_scr[...] = jnp.zeros_like(acc_scr)

    h = h_scr[...]
    a = _dot(h, w1_ref[...].astype(BF16))
    b = _dot(h, w3_ref[...].astype(BF16))
    acc_scr[...] += _dot((_silu(a) * b).astype(BF16), w2_ref[...].astype(BF16))

    @pl.when(f == nf - 1)
    def _():
        out_ref[...] = x_ref[...] + gate_ref[...] * acc_scr[...]


def _ffn(x, g, mod, w1, w3, w2, rows_per_group):
    N, D = x.shape
    FF = w1.shape[1]
    TM, TF = min(1024, rows_per_group), 256
    nf = FF // TF
    return pl.pallas_call(
        functools.partial(_ffn_kernel, nf=nf),
        grid=(N // TM, nf),
        in_specs=[pl.BlockSpec((TM, D), lambda i, f: (i, 0)),
                  pl.BlockSpec((1, D), lambda i, f: (0, 0)),
                  _mod_spec(3, D, TM, rows_per_group, 0),
                  _mod_spec(4, D, TM, rows_per_group, 0),
                  _mod_spec(5, D, TM, rows_per_group, 0),
                  pl.BlockSpec((D, TF), lambda i, f: (0, f)),
                  pl.BlockSpec((D, TF), lambda i, f: (0, f)),
                  pl.BlockSpec((TF, D), lambda i, f: (f, 0))],
        out_specs=pl.BlockSpec((TM, D), lambda i, f: (i, 0)),
        out_shape=jax.ShapeDtypeStruct((N, D), F32),
        scratch_shapes=[pltpu.VMEM((TM, D), BF16), pltpu.VMEM((TM, D), F32)],
        compiler_params=_params("arbitrary", "arbitrary"),
        name="ffn",
    )(x, g.reshape(1, D), mod, mod, mod, w1, w3, w2)


MOE_SB = 1024
MOE_SBG = 512
MOE_GG = 4
MOE_TRG = 256
MOE_TR = 2048
MOE_CG = 4


def _two_stream_specs(shape, ntp, ax=0):
    def idx_p(*g):
        return (jnp.minimum(g[ax], ntp - 1), 0)

    def idx_s(*g):
        return (jnp.maximum(g[ax] - ntp, 0), 0)
    return pl.BlockSpec(shape, idx_p), pl.BlockSpec(shape, idx_s)


def _pool_mod_spec(part, D, TM, ntp, rows_per_group):
    def idx(i, *_):
        return (jnp.where(i < ntp, 0, 1 + ((i - ntp) * TM) // rows_per_group), part, 0, 0)
    return pl.BlockSpec((None, None, 1, D), idx)


def _route_kernel(xp_ref, xs_ref, g_ref, sh_ref, sc_ref, rw_ref, tri_ref, h_ref, info_ref, infot_ref, cum_ref,
                  carry_scr, *, ntp):
    i = pl.program_id(0)

    @pl.when(i == 0)
    def _():
        carry_scr[...] = jnp.zeros_like(carry_scr)

    x = jnp.where(i < ntp, xp_ref[...], xs_ref[...])
    h = _norm_mod(x, g_ref[...], sh_ref[...], sc_ref[...])
    h_ref[...] = h.astype(BF16)
    lane = lax.broadcasted_iota(jnp.int32, (x.shape[0], LANES), 1).astype(F32)
    logits = _dot3(_split(h), _split(rw_ref[...]))
    logits = jnp.where(lane < N_EXPERTS, logits, -jnp.inf)
    m1 = jnp.max(logits, axis=-1, keepdims=True)
    i1 = jnp.min(jnp.where(logits == m1, lane, float(LANES)), axis=-1, keepdims=True)
    rest = jnp.where(lane == i1, -jnp.inf, logits)
    m2 = jnp.max(rest, axis=-1, keepdims=True)
    i2 = jnp.min(jnp.where(rest == m2, lane, float(LANES)), axis=-1, keepdims=True)
    e2 = jnp.exp(m2 - m1)
    w1 = 1.0 / (1.0 + e2)
    w2 = e2 / (1.0 + e2)
    ind = jnp.where(jnp.logical_or(lane == i1, lane == i2), 1.0, 0.0)
    before = _dot(tri_ref[...], ind.astype(BF16)) + carry_scr[...]
    r1 = jnp.sum(jnp.where(lane == i1, before, 0.0), axis=-1, keepdims=True)
    r2 = jnp.sum(jnp.where(lane == i2, before, 0.0), axis=-1, keepdims=True)
    total = carry_scr[...] + jnp.sum(ind, axis=0, keepdims=True)
    carry_scr[...] = total
    for part in range(1, MOE_SB // MOE_SBG):
        cum_ref[part - 1] = before[part * MOE_SBG:part * MOE_SBG + 1, :]
    cum_ref[MOE_SB // MOE_SBG - 1] = total
    info = jnp.where(lane == 0.0, i1, jnp.where(lane == 1.0, i2, jnp.where(lane == 2.0, w1, jnp.where(
        lane == 3.0, w2, jnp.where(lane == 4.0, r1, jnp.where(lane == 5.0, r2, 0.0))))))
    info_ref[...] = info[:, 0:SUB]
    info_t = jnp.concatenate([info[r:r + LANES, :].T for r in range(0, info.shape[0], LANES)], axis=1)
    infot_ref[...] = info_t[0:SUB, :]


def _moe_route(xp, xs, g, mod, router_w, rows_per_group):
    Np, D = xp.shape
    N = Np + xs.shape[0]
    TM = MOE_SB
    ntp = Np // TM
    nt = N // TM
    rw = jnp.pad(router_w, ((0, 0), (0, LANES - router_w.shape[1])))
    tri = jnp.asarray(np.tril(np.ones((TM, TM), np.float32), -1), dtype=BF16)
    xp_spec, xs_spec = _two_stream_specs((TM, D), ntp)
    return pl.pallas_call(
        functools.partial(_route_kernel, ntp=ntp),
        grid=(nt,),
        in_specs=[xp_spec, xs_spec,
                  pl.BlockSpec((1, D), lambda i: (0, 0)),
                  _pool_mod_spec(3, D, TM, ntp, rows_per_group),
                  _pool_mod_spec(4, D, TM, ntp, rows_per_group),
                  pl.BlockSpec((D, LANES), lambda i: (0, 0)),
                  pl.BlockSpec((TM, TM), lambda i: (0, 0))],
        out_specs=[pl.BlockSpec((TM, D), lambda i: (i, 0)),
                   pl.BlockSpec((TM, SUB), lambda i: (i, 0)),
                   pl.BlockSpec((SUB, TM), lambda i: (0, i)),
                   pl.BlockSpec((MOE_SB // MOE_SBG, 1, LANES), lambda i: (i, 0, 0))],
        out_shape=[jax.ShapeDtypeStruct((N, D), BF16),
                   jax.ShapeDtypeStruct((N, SUB), F32),
                   jax.ShapeDtypeStruct((SUB, N), F32),
                   jax.ShapeDtypeStruct((nt * (MOE_SB // MOE_SBG), 1, LANES), F32)],
        scratch_shapes=[pltpu.VMEM((1, LANES), F32)],
        compiler_params=_params("arbitrary"),
        name="moe_route",
    )(xp, xs, g.reshape(1, D), mod, mod, rw, tri)


def _hold_unused(idx, used):
    steps, slots = idx.shape
    read = jnp.arange(slots, dtype=jnp.int32)[None, :] < used[:, None]
    step = jnp.arange(steps, dtype=jnp.int32)[:, None]
    last = lax.cummax(jnp.where(read, step, -1), axis=0)
    held = jnp.take_along_axis(idx, jnp.maximum(last, 0), axis=0)
    return jnp.where(last >= 0, held, 0)


def _moe_plan(info, info_t, cum, N):
    E, SB, TRG, TR = N_EXPERTS, MOE_SB, MOE_TRG, MOE_TR
    NB = N // SB
    rmax = 2 * N + E * TR
    RG, RT = rmax // TRG, rmax // TR
    PMAX = RG + E * NB
    i32 = jnp.int32
    parts = SB // MOE_SBG
    cum_g = cum[:, 0, :E].astype(i32).T
    cum_e = cum_g[:, parts - 1::parts]
    cnt = cum_e[:, -1]
    tiles = (cnt + TR - 1) // TR
    start = TR * (jnp.cumsum(tiles) - tiles)

    startf = start.astype(F32)

    def region_start(e):
        out = jnp.zeros_like(e)
        for k in range(E):
            out = jnp.where(e == float(k), startf[k], out)
        return out

    pos_cols = jnp.concatenate([region_start(info[:, 0:2]) + info[:, 4:6], info[:, 2:4],
                                jnp.zeros((N, 4), F32)], axis=1)
    pos_rows = jnp.concatenate([region_start(info_t[0:2]) + info_t[4:6], jnp.zeros((6, N), F32)],
                               axis=0)

    def region(row0):
        e = jnp.clip(jnp.sum(row0[:, None] >= start[None, :], axis=1) - 1, 0, E - 1)
        return e, row0 - start[e]

    eq, lo = region(jnp.arange(RG, dtype=i32) * TRG)
    hi = jnp.minimum(lo + TRG, cnt[eq])
    first = jnp.sum(cum_e[eq] <= lo[:, None], axis=1)
    last = jnp.sum(cum_e[eq] < hi[:, None], axis=1)
    nblk = jnp.where(hi > lo, last - first + 1, 0)
    pend = jnp.cumsum(nblk)
    npairs = pend[-1]
    p = jnp.arange(PMAX, dtype=i32)
    valid = p < npairs
    pc = jnp.minimum(p, npairs - 1)
    q_of = jnp.minimum(jnp.sum(pend[None, :] <= pc[:, None], axis=1), RG - 1).astype(i32)
    pstart = pend - nblk
    s_of = (first[q_of] + pc - pstart[q_of]).astype(i32)

    GG = MOE_GG
    first_g = jnp.sum(cum_g[eq] <= lo[:, None], axis=1)
    last_g = jnp.sum(cum_g[eq] < hi[:, None], axis=1)
    nblk_g = jnp.where(hi > lo, last_g - first_g + 1, 0)
    nst = (nblk_g + GG - 1) // GG
    gst_end = jnp.cumsum(nst)
    gtotal = gst_end[-1]
    smax_g = (RG + E * NB * parts + (GG - 1) * RG) // GG + 1
    jg = jnp.arange(smax_g, dtype=i32)
    g_ok = jg < gtotal
    jgc = jnp.minimum(jg, gtotal - 1)
    tq = jnp.minimum(jnp.sum(gst_end[None, :] <= jgc[:, None], axis=1), RG - 1).astype(i32)
    gg = jgc - (gst_end - nst)[tq]
    g_slots = jnp.where(g_ok, jnp.clip(nblk_g[tq] - GG * gg, 0, GG), 0)
    g_parts = _hold_unused((first_g[tq] + GG * gg)[:, None] + jnp.arange(GG, dtype=i32)[None, :], g_slots)
    g_first = jnp.logical_and(g_ok, gg == 0)
    gather_plan = (tq, g_parts.reshape(-1).astype(i32), g_slots.astype(i32), g_first.astype(i32))

    order = jnp.argsort(jnp.where(valid, s_of * RG + q_of, jnp.iinfo(jnp.int32).max))
    s2, q2 = s_of[order], q_of[order]
    CG = MOE_CG
    blocks = jnp.arange(NB, dtype=i32)
    per_blk = jnp.sum(jnp.logical_and(valid[None, :], s2[None, :] == blocks[:, None]), axis=1)
    pb_end = jnp.cumsum(per_blk)
    pb_start = pb_end - per_blk
    nsteps = (per_blk + CG - 1) // CG
    st_end = jnp.cumsum(nsteps)
    total = st_end[-1]
    SMAX = (PMAX + CG - 1) // CG + NB
    j = jnp.arange(SMAX, dtype=i32)
    step_ok = j < total
    jc = jnp.minimum(j, total - 1)
    blk = jnp.minimum(jnp.sum(st_end[None, :] <= jc[:, None], axis=1), NB - 1).astype(i32)
    grp = jc - (st_end - nsteps)[blk]
    slot_p = pb_start[blk][:, None] + CG * grp[:, None] + jnp.arange(CG, dtype=i32)[None, :]
    slot_ok = jnp.logical_and(slot_p < pb_end[blk][:, None], step_ok[:, None])
    c_slots = jnp.sum(slot_ok, axis=1).astype(i32)
    slot_q = _hold_unused(q2[jnp.minimum(slot_p, npairs - 1)], c_slots)
    c_first = jnp.logical_and(step_ok, grp == 0).astype(i32)
    c_last = jnp.logical_and(step_ok, grp == nsteps[blk] - 1).astype(i32)
    combine_plan = (blk, slot_q.reshape(-1).astype(i32), c_slots, c_first, c_last)

    te, tlo = region(jnp.arange(RT, dtype=i32) * TR)
    tvalid = jnp.clip(cnt[te] - tlo, 0, TR)
    last_t = jnp.sum(tiles) - 1
    t_idx = jnp.where(tvalid > 0, jnp.arange(RT, dtype=i32), last_t).astype(i32)
    ffn_plan = (t_idx, te[t_idx].astype(i32), tvalid.astype(i32))
    return pos_cols, pos_rows, gather_plan, combine_plan, ffn_plan, rmax


def _moe_gather_kernel(q_ref, s_ref, slots_ref, first_ref, *refs):
    pos_refs, h_refs, out_ref = refs[:MOE_GG], refs[MOE_GG:2 * MOE_GG], refs[2 * MOE_GG]
    p = pl.program_id(0)
    rows = out_ref.shape[0]

    @pl.when(first_ref[p] == 1)
    def _():
        out_ref[...] = jnp.zeros_like(out_ref)

    for ns in range(1, MOE_GG + 1):
        @pl.when(slots_ref[p] == ns)
        def _(ns=ns):
            row = (lax.broadcasted_iota(jnp.int32, (rows, 1), 0) + q_ref[p] * rows).astype(F32)
            sels = []
            for k in range(ns):
                hit = jnp.logical_or(pos_refs[k][0:1, :] == row, pos_refs[k][1:2, :] == row)
                sels.append(jnp.where(hit, 1.0, 0.0).astype(BF16))
            sel = sels[0] if ns == 1 else jnp.concatenate(sels, axis=1)
            hs = h_refs[0][...] if ns == 1 else jnp.concatenate([h_refs[k][...] for k in range(ns)], axis=0)
            out_ref[...] = out_ref[...] + _dot(sel, hs).astype(BF16)


def _moe_gather(h, pos_rows, plan, rmax):
    N, D = h.shape
    nsteps = plan[0].shape[0]

    def pos_spec(k):
        return pl.BlockSpec((SUB, MOE_SBG), lambda p, q, s, *_: (0, s[MOE_GG * p + k]))

    def tok_spec(k):
        return pl.BlockSpec((MOE_SBG, D), lambda p, q, s, *_: (s[MOE_GG * p + k], 0))

    return pl.pallas_call(
        _moe_gather_kernel,
        grid_spec=pltpu.PrefetchScalarGridSpec(
            num_scalar_prefetch=4, grid=(nsteps,),
            in_specs=[pos_spec(k) for k in range(MOE_GG)] + [tok_spec(k) for k in range(MOE_GG)],
            out_specs=pl.BlockSpec((MOE_TRG, D), lambda p, q, *_: (q[p], 0))),
        out_shape=jax.ShapeDtypeStruct((rmax, D), BF16),
        compiler_params=_params("arbitrary"),
        name="moe_gather",
    )(*plan, *([pos_rows] * MOE_GG), *([h] * MOE_GG))


def _moe_ffn_kernel(t_ref, e_ref, nv_ref, x_ref, w1_ref, w3_ref, w2_ref, out_ref, acc_scr, *, nf):
    t = pl.program_id(0)
    f = pl.program_id(1)
    nv = nv_ref[t]

    def block(start, size):
        rows = pl.ds(start, size)

        @pl.when(f == 0)
        def _():
            acc_scr[rows, :] = jnp.zeros((size, acc_scr.shape[1]), F32)

        x = x_ref[rows, :]
        a = _dot(x, w1_ref[...].astype(BF16))
        b = _dot(x, w3_ref[...].astype(BF16))
        acc_scr[rows, :] += _dot((_silu(a) * b).astype(BF16), w2_ref[...].astype(BF16))

        @pl.when(f == nf - 1)
        def _():
            out_ref[rows, :] = acc_scr[rows, :].astype(out_ref.dtype)

    nsub = MOE_TR // MOE_TRG
    used = (nv + MOE_TRG - 1) // MOE_TRG

    @pl.when(used == nsub)
    def _():
        block(0, MOE_TR)

    @pl.when(jnp.logical_and(used > 0, used < nsub))
    def _():
        start = jnp.int32(0)
        size = MOE_TR // 2
        while size >= MOE_TRG:
            has = (used & (size // MOE_TRG)) != 0

            @pl.when(has)
            def _(start=start, size=size):
                block(pl.multiple_of(start, MOE_TRG), size)

            start = start + jnp.where(has, size, 0)
            size //= 2


def _moe_ffn(xs, plan, w1, w3, w2):
    rmax, D = xs.shape
    FF = w1.shape[2]
    TF = 256
    nf = FF // TF
    RT = rmax // MOE_TR

    def fidx(t, f, nv):
        return jnp.where(nv[t] > 0, f, nf - 1)

    return pl.pallas_call(
        functools.partial(_moe_ffn_kernel, nf=nf),
        grid_spec=pltpu.PrefetchScalarGridSpec(
            num_scalar_prefetch=3, grid=(RT, nf),
            in_specs=[pl.BlockSpec((MOE_TR, D), lambda t, f, ti, e, nv: (ti[t], 0)),
                      pl.BlockSpec((None, D, TF), lambda t, f, ti, e, nv: (e[t], 0, fidx(t, f, nv))),
                      pl.BlockSpec((None, D, TF), lambda t, f, ti, e, nv: (e[t], 0, fidx(t, f, nv))),
                      pl.BlockSpec((None, TF, D), lambda t, f, ti, e, nv: (e[t], fidx(t, f, nv), 0))],
            out_specs=pl.BlockSpec((MOE_TR, D), lambda t, f, ti, e, nv: (ti[t], 0)),
            scratch_shapes=[pltpu.VMEM((MOE_TR, D), F32)]),
        out_shape=jax.ShapeDtypeStruct((rmax, D), BF16),
        compiler_params=_params("arbitrary", "arbitrary"),
        name="moe_ffn",
    )(*plan, xs, w1, w3, w2)


def _moe_combine_kernel(s_ref, q_ref, slots_ref, first_ref, last_ref, pos_ref, *refs, ntp):
    ys_refs = refs[:MOE_CG]
    xp_ref, xs_ref, gate_ref, fg_ref, op_ref, os_ref, acc_scr = refs[MOE_CG:]
    p = pl.program_id(0)
    rows = ys_refs[0].shape[0]

    @pl.when(first_ref[p] == 1)
    def _():
        acc_scr[...] = jnp.zeros_like(acc_scr)

    for ns in range(1, MOE_CG + 1):
        @pl.when(slots_ref[p] == ns)
        def _(ns=ns):
            sels = []
            for k in range(ns):
                col = (lax.broadcasted_iota(jnp.int32, (1, rows), 1) + q_ref[MOE_CG * p + k] * rows).astype(F32)
                sels.append((jnp.where(pos_ref[:, 0:1] == col, pos_ref[:, 2:3], 0.0)
                             + jnp.where(pos_ref[:, 1:2] == col, pos_ref[:, 3:4], 0.0)).astype(BF16))
            sel = sels[0] if ns == 1 else jnp.concatenate(sels, axis=1)
            ys = ys_refs[0][...] if ns == 1 else jnp.concatenate([ys_refs[k][...] for k in range(ns)], axis=0)
            acc_scr[...] += _dot(sel, ys)

    @pl.when(last_ref[p] == 1)
    def _():
        s = s_ref[p]
        x = jnp.where(s < ntp, xp_ref[...], xs_ref[...])
        y = x + gate_ref[...] * acc_scr[...]
        out = y * lax.rsqrt(jnp.mean(y * y, axis=-1, keepdims=True) + EPS) * fg_ref[...]

        @pl.when(s < ntp)
        def _():
            op_ref[...] = out

        @pl.when(s >= ntp)
        def _():
            os_ref[...] = out


def _moe_combine(ys, pos_cols, plan, xp, xs, mod, final_g, rows_per_group):
    Np, D = xp.shape
    Ns = xs.shape[0]
    SB = MOE_SB
    ntp = Np // SB
    nsteps = plan[0].shape[0]

    def tile_spec(k):
        return pl.BlockSpec((MOE_TRG, D), lambda p, s, q, *_: (q[MOE_CG * p + k], 0))

    def tok_p(p, s, *_):
        return (jnp.minimum(s[p], ntp - 1), 0)

    def tok_s(p, s, *_):
        return (jnp.maximum(s[p] - ntp, 0), 0)

    def gate_idx(p, s, *_):
        return (jnp.where(s[p] < ntp, 0, 1 + ((s[p] - ntp) * SB) // rows_per_group), 5, 0, 0)

    return pl.pallas_call(
        functools.partial(_moe_combine_kernel, ntp=ntp),
        grid_spec=pltpu.PrefetchScalarGridSpec(
            num_scalar_prefetch=5, grid=(nsteps,),
            in_specs=[pl.BlockSpec((SB, SUB), lambda p, s, q, *_: (s[p], 0))]
            + [tile_spec(k) for k in range(MOE_CG)]
            + [pl.BlockSpec((SB, D), tok_p),
                      pl.BlockSpec((SB, D), tok_s),
                      pl.BlockSpec((None, None, 1, D), gate_idx),
                      pl.BlockSpec((1, D), lambda p, *_: (0, 0))],
            out_specs=[pl.BlockSpec((SB, D), tok_p), pl.BlockSpec((SB, D), tok_s)],
            scratch_shapes=[pltpu.VMEM((SB, D), F32)]),
        out_shape=[jax.ShapeDtypeStruct((Np, D), F32), jax.ShapeDtypeStruct((Ns, D), F32)],
        compiler_params=_params("arbitrary"),
        name="moe_combine",
    )(*plan, pos_cols, *([ys] * MOE_CG), xp, xs, mod, final_g.reshape(1, D))


def _moe(xp, xs, g, mod, router_w, w1, w3, w2, final_g, rows_per_group):
    N = xp.shape[0] + xs.shape[0]
    h, info, info_t, cum = _moe_route(xp, xs, g, mod, router_w, rows_per_group)
    pos_cols, pos_rows, gather_plan, combine_plan, ffn_plan, rmax = _moe_plan(info, info_t, cum, N)
    x_sorted = _moe_gather(h, pos_rows, gather_plan, rmax)
    y_sorted = _moe_ffn(x_sorted, ffn_plan, w1, w3, w2)
    return _moe_combine(y_sorted, pos_cols, combine_plan, xp, xs, mod, final_g, rows_per_group)


def _gla_levels(C):
    lv, c = [], C // 2
    while c >= SUB:
        lv.append(c)
        c //= 2
    return lv


def _gla_tables(C):
    levels = _gla_levels(C)
    nr = 1 + len(levels)
    mat = np.zeros((2, nr * C, C), np.float32)
    code = np.zeros((2, C, C), np.int32)
    for d in range(2):
        p = np.arange(C) if d == 0 else C - 1 - np.arange(C)
        pi, pj = p[:, None], p[None, :]
        mat[d, 0:C] = pj <= pi
        code[d] = np.where((pj <= pi) & (pi // SUB == pj // SUB), 1, 0)
        for lv, c in enumerate(levels):
            blk = pi // c
            later = blk % 2 == 1
            mat[d, (1 + lv) * C:(2 + lv) * C] = ((later & (pj > blk * c - 1) & (pj <= pi))
                                                 | (~later & (pj > pi) & (pj <= (blk + 1) * c - 1)))
            pair = (pi // (2 * c) == pj // (2 * c)) & (pi // c != pj // c) & (pj <= pi)
            code[d] = np.where(pair, 2 + lv, code[d])
    ones = np.zeros((SUB * LANES, C), np.float32)
    for jj in range(SUB):
        ones[jj * LANES:(jj + 1) * LANES, jj::SUB] = 1.0
    return jnp.asarray(mat, dtype=BF16), jnp.asarray(code), jnp.asarray(ones, dtype=BF16)


def _bcast_sublane(x, jj):
    r, w = x.shape
    x3 = x.reshape(r // SUB, SUB, w)
    return jnp.broadcast_to(x3[:, jj:jj + 1, :], x3.shape).reshape(r, w)


def _t128(x):
    r, w = x.shape
    if w > LANES:
        return jnp.concatenate([x[:, i:i + LANES].T for i in range(0, w, LANES)], axis=0)
    return jnp.concatenate([x[i:i + LANES, :].T for i in range(0, r, LANES)], axis=1)


def _gla_kernel(q_ref, k_ref, v_ref, g_ref, lr_ref, wg_ref, ba_ref, mat_ref, code_ref, ones_ref,
                s0f_ref, s0b_ref, ng_ref, o_ref, sf_ref, sb_ref, st_scr, of_scr, *, n, C, G):
    d = pl.program_id(1)
    c = pl.program_id(2)
    levels = _gla_levels(C)

    @pl.when(jnp.logical_and(c == 0, d == 0))
    def _():
        for bb in range(G):
            for h in range(C_H):
                st_scr[bb, h] = _t128(s0f_ref[bb, h])

    @pl.when(jnp.logical_and(c == 0, d == 1))
    def _():
        for bb in range(G):
            for h in range(C_H):
                st_scr[bb, h] = _t128(s0b_ref[bb, h])

    mat = mat_ref[...]
    code = code_ref[...]
    ones = ones_ref[...]
    wg = _split(wg_ref[...])
    cums = []
    for bb in range(G):
        xg = _dot3(_split(lr_ref[bb]), wg) + ba_ref[...]
        la = (jnp.minimum(xg, 0.0) - jnp.log1p(jnp.exp(-jnp.abs(xg)))) * (LOG2E / C_TAU)
        hi, lo = _split(la)
        cums.append(_dot(mat, hi) + _dot(mat, lo))

    def prepare(bb, h):
        cum = cums[bb]
        ks = slice(h * C_DK, (h + 1) * C_DK)
        qh = q_ref[bb, :, ks].astype(F32) * (C_DK ** -0.5)
        kh = k_ref[bb, :, ks].astype(F32)
        b = cum[0:C, ks]
        b_end = jnp.min(b, axis=0, keepdims=True)
        b_rest = b_end - b
        ps = []
        for jj in range(SUB):
            dec = jnp.exp2(jnp.minimum(b - _bcast_sublane(b, jj), 0.0))
            ps.append((qh * _bcast_sublane(kh, jj) * dec).astype(BF16))
        lv_ops = []
        for lv in range(len(levels)):
            fac = jnp.exp2(cum[(1 + lv) * C:(2 + lv) * C, ks])
            lv_ops.append(((qh * fac).astype(BF16), (kh * fac).astype(BF16)))
        qe = (qh * jnp.exp2(b)).astype(BF16)
        ke = (kh * jnp.exp2(b_rest)).astype(BF16)
        e_end = jnp.exp2(b_end)
        return jnp.concatenate(ps, axis=1), lv_ops, qe, ke, e_end

    def contract(bb, h, prep):
        pcat, lv_ops, qe, ke, e_end = prep
        vh = v_ref[bb, :, h * C_DV:(h + 1) * C_DV].astype(F32)
        att = jnp.where(code == 1, _dot(pcat, ones), 0.0)
        for lv, (qs, ks_) in enumerate(lv_ops):
            att = jnp.where(code == 2 + lv, _dot_nt(qs, ks_), att)
        st = st_scr[bb, h]
        o = _dot(att.astype(BF16), vh.astype(BF16)) + _dot_nt(qe, st.astype(BF16))
        st_scr[bb, h] = e_end * st + _dot(_t128(vh).astype(BF16), ke)
        return o

    units = [(bb, h) for h in range(C_H) for bb in range(G)]
    outs = {}
    prep = prepare(*units[0])
    for idx, (bb, h) in enumerate(units):
        nxt = prepare(*units[idx + 1]) if idx + 1 < len(units) else None
        outs[(bb, h)] = contract(bb, h, prep)
        prep = nxt
    o_all = [jnp.concatenate([outs[(bb, h)] for h in range(C_H)], axis=-1) for bb in range(G)]

    @pl.when(d == 0)
    def _():
        for bb in range(G):
            of_scr[bb, c] = o_all[bb]

    @pl.when(d == 1)
    def _():
        for bb in range(G):
            tot = o_all[bb] + of_scr[bb, n - 1 - c]
            res = []
            for h in range(C_H):
                sl = slice(h * C_DV, (h + 1) * C_DV)
                t = tot[:, sl]
                y = t * lax.rsqrt(jnp.mean(t * t, axis=-1, keepdims=True) + EPS) * ng_ref[:, sl]
                res.append(y * _silu(g_ref[bb, :, sl].astype(F32)))
            o_ref[bb] = jnp.concatenate(res, axis=-1).astype(o_ref.dtype)

    @pl.when(jnp.logical_and(c == n - 1, d == 0))
    def _():
        for bb in range(G):
            for h in range(C_H):
                sf_ref[bb, h] = _t128(st_scr[bb, h])

    @pl.when(jnp.logical_and(c == n - 1, d == 1))
    def _():
        for bb in range(G):
            for h in range(C_H):
                sb_ref[bb, h] = _t128(st_scr[bb, h])


def _gla(z, zg, B, T, w_a2, b_a, s0f, s0b, norm_g):
    C = GLA_CHUNK
    G = _rows_per_step(B, T, C_H * C_DV)
    assert B % G == 0 and T % C == 0
    n = T // C
    HK = C_H * C_DK
    HV = C_H * C_DV
    mat, code, ones = _gla_tables(C)
    nr = mat.shape[1] // C
    wg = jnp.zeros((2, LANES, HK), F32)
    for dr in range(2):
        wg = wg.at[dr, dr * C_RANK:(dr + 1) * C_RANK, :].set(w_a2[dr])
    z3 = z.reshape(B, T, z.shape[1])
    zg3 = zg.reshape(B, T, zg.shape[1])

    def chunk(d, c):
        return c + d * (n - 1 - 2 * c)

    st_spec = pl.BlockSpec((G, C_H, C_DK, C_DV), lambda b, d, c: (b, 0, 0, 0))
    st_shape = jax.ShapeDtypeStruct((B, C_H, C_DK, C_DV), F32)
    o, sf, sb = pl.pallas_call(
        functools.partial(_gla_kernel, n=n, C=C, G=G),
        grid=(B // G, 2, n),
        in_specs=[pl.BlockSpec((G, C, HK), lambda b, d, c: (b, chunk(d, c), 0)),
                  pl.BlockSpec((G, C, HK), lambda b, d, c: (b, chunk(d, c), 1)),
                  pl.BlockSpec((G, C, HV), lambda b, d, c: (b, chunk(d, c), 1)),
                  pl.BlockSpec((G, C, HV), lambda b, d, c: (b, chunk(d, c), 2)),
                  pl.BlockSpec((G, C, LANES), lambda b, d, c: (b, chunk(d, c), 0)),
                  pl.BlockSpec((None, LANES, HK), lambda b, d, c: (d, 0, 0)),
                  pl.BlockSpec((None, 1, HK), lambda b, d, c: (d, 0, 0)),
                  pl.BlockSpec((None, nr * C, C), lambda b, d, c: (d, 0, 0)),
                  pl.BlockSpec((None, C, C), lambda b, d, c: (d, 0, 0)),
                  pl.BlockSpec((SUB * LANES, C), lambda b, d, c: (0, 0)),
                  st_spec, st_spec,
                  pl.BlockSpec((1, HV), lambda b, d, c: (0, 0))],
        out_specs=[pl.BlockSpec((G, C, HV), lambda b, d, c: (b, (n - 1) - d * c, 0)),
                   st_spec, st_spec],
        out_shape=[jax.ShapeDtypeStruct((B, T, HV), BF16), st_shape, st_shape],
        scratch_shapes=[pltpu.VMEM((G, C_H, C_DV, C_DK), F32), pltpu.VMEM((G, n, C, HV), F32)],
        compiler_params=_params("arbitrary", "arbitrary", "arbitrary"),
        name="gla",
    )(z3, z3, z3, z3, zg3, wg, b_a.reshape(2, 1, HK), mat, code, ones, s0f, s0b, norm_g.reshape(1, HV))
    return o.reshape(B * T, HV), sf, sb


def _run_stream(x, B, T, mods, ctx, p):
    N, D = x.shape
    rpg = N // mods[0].shape[0]
    TM = min(2048, rpg)
    nb = (B_H + 2 * B_HKV) * B_HD

    w_in = p['even_w_in'][0]
    z, zb = _norm_mm(x, p['norm1_g'][0], mods[0], (0, 1), w_in, w_in, (nb, MIX_MAIN // nb), TM, rpg)
    if ctx is None:
        s0 = jnp.zeros((B, A_H, A_DK, A_DV), F32)
        a_f0, a_b0, cache_k, cache_v = s0, s0, None, None
    else:
        cache_k, cache_v, a_f0, a_b0 = ctx[0], ctx[1], ctx[2], ctx[3]
    o_a, a_sf, a_sb = _retention(z, B, T, p['a_log_gamma'][0], a_f0, a_b0, p['a_norm_g'][0])
    qpad, k_norm, k_rot, v_bf = _bprep(zb, T, p['b_q_g'][0], p['b_k_g'][0], rope=ctx is not None)
    o_b = _attention(qpad, k_rot, v_bf, B, T, cache_k, cache_v)
    x = _proj_res(x, mods[0], 2, [o_a, o_b], p['even_w_out'][0], rpg)
    x = _ffn(x, p['norm2_g'][0], mods[0], p['ff_w1'][0], p['ff_w3'][0], p['ff_w2'][0], rpg)

    w_in = p['odd_w_in'][0]
    w_gate = jnp.pad(w_in[:, MIX_MAIN:], ((0, 0), (0, LANES - 2 * C_RANK)))
    z1, z1g = _norm_mm(x, p['norm1_g'][1], mods[1], (0, 1), w_in, w_gate, (LANES, 0), TM, rpg)
    if ctx is None:
        s0 = jnp.zeros((B, C_H, C_DK, C_DV), F32)
        c_f0, c_b0 = s0, s0
    else:
        c_f0, c_b0 = ctx[4], ctx[5]
    o_c, c_sf, c_sb = _gla(z1, z1g, B, T, p['c_w_a2'][0], p['c_b_a'][0], c_f0, c_b0, p['c_norm_g'][0])
    x = _proj_res(x, mods[1], 2, [o_c], p['odd_w_out'][0], rpg)
    v_raw = zb[:, (B_H + B_HKV) * B_HD:]
    return x, (k_norm, v_raw, a_sf, a_sb, c_sf, c_sb)


def kernel(x_prompt, x_sample, c, cache_b_k, cache_b_v, state_a_fwd, state_a_bwd, state_c_fwd, state_c_bwd,
           c_ctx, w_mod, b_mod, norm1_g, norm2_g, final_g, even_w_in, even_w_out, a_log_gamma, a_norm_g,
           b_q_g, b_k_g, odd_w_in, c_w_a2, c_b_a, c_norm_g, odd_w_out, ff_w1, ff_w3, ff_w2,
           router_w, moe_w1, moe_w3, moe_w2):
    Bp, Tp, D = x_prompt.shape
    Bs, Ts, _ = x_sample.shape
    L = w_mod.shape[0]
    assert L == 2 and even_w_in.shape[0] == 1 and odd_w_in.shape[0] == 1
    p = dict(norm1_g=norm1_g, norm2_g=norm2_g, final_g=final_g, even_w_in=even_w_in, even_w_out=even_w_out,
             a_log_gamma=a_log_gamma, a_norm_g=a_norm_g, b_q_g=b_q_g, b_k_g=b_k_g, odd_w_in=odd_w_in,
             c_w_a2=c_w_a2, c_b_a=c_b_a, c_norm_g=c_norm_g, odd_w_out=odd_w_out, ff_w1=ff_w1, ff_w3=ff_w3,
             ff_w2=ff_w2, router_w=router_w, moe_w1=moe_w1, moe_w3=moe_w3, moe_w2=moe_w2)

    rows = 8
    conds = jnp.concatenate([c_ctx[None, :], c, jnp.zeros((rows - 1 - Bs, D), F32)], axis=0)
    mod = _modulation(conds, w_mod, b_mod).reshape(L, rows, 6, 1, D)
    mods_p = [mod[l, 0:1] for l in range(L)]
    mods_s = [mod[l, 1:1 + Bs] for l in range(L)]

    x_p, kept = _run_stream(x_prompt.reshape(Bp * Tp, D), Bp, Tp, mods_p, None, p)
    nk = B_HKV * B_HD
    ctx = (cache_b_k[:, 0].reshape(Bs, -1, nk), cache_b_v[:, 0].reshape(Bs, -1, nk),
           state_a_fwd[:, 0], state_a_bwd[:, 0], state_c_fwd[:, 0], state_c_bwd[:, 0])
    x_s, _ = _run_stream(x_sample.reshape(Bs * Ts, D), Bs, Ts, mods_s, ctx, p)
    y_p, y_s = _moe(x_p, x_s, norm2_g[1], mod[1, 0:1 + Bs], router_w[0], moe_w1[0], moe_w3[0], moe_w2[0],
                    final_g, Ts)

    k_norm, v_raw, a_sf, a_sb, c_sf, c_sb = kept
    return (y_p.reshape(Bp, Tp, D), y_s.reshape(Bs, Ts, D),
            k_norm.reshape(Bp, 1, Tp, B_HKV, B_HD), v_raw.reshape(Bp, 1, Tp, B_HKV, B_HD),
            a_sf[:, None], a_sb[:, None], c_sf[:, None], c_sb[:, None])
```

```python
import functools

import numpy as np
import jax
import jax.numpy as jnp
from jax import lax
from jax.experimental import pallas as pl
from jax.experimental.pallas import tpu as pltpu

F32 = jnp.float32
BF16 = jnp.bfloat16
EPS = 1e-6
LOG2E = 1.4426950408889634

VMEM_LIMIT_BYTES = 56 * 1024 * 1024

A_H, A_DK, A_DV = 4, 128, 256
B_H, B_HKV, B_HD = 8, 2, 64
C_H, C_DK, C_DV, C_RANK = 4, 128, 256, 16
C_TAU = 16.0
GRID_W = 64
ROPE_THETA = 10000.0
N_EXPERTS = 8
LANES = 128
SUB = 8
RET_CHUNK = 128
GLA_CHUNK = 128
Q_TILE = 128
SCAN_ROWS_MAX = 4
SCAN_FWD_BYTES = 32 * 1024 * 1024


def _rows_per_step(B, T, width):
    g = SCAN_ROWS_MAX
    while g > 1 and (B % g or g * T * width * 4 > SCAN_FWD_BYTES):
        g //= 2
    return g


def _params(*sem):
    return pltpu.CompilerParams(dimension_semantics=sem, vmem_limit_bytes=VMEM_LIMIT_BYTES)


def _dot(a, b):
    return jnp.dot(a, b, preferred_element_type=F32)


def _dot_nt(a, b):
    return lax.dot_general(a, b, (((1,), (1,)), ((), ())), preferred_element_type=F32)


def _split(x):
    hi = x.astype(BF16)
    return hi, (x - hi.astype(F32)).astype(BF16)


def _dot3(a, b):
    return _dot(a[0], b[0]) + (_dot(a[0], b[1]) + _dot(a[1], b[0]))


def _silu(x):
    return x * jax.nn.sigmoid(x)


def _norm_mod(x, g, sh, sc):
    r = lax.rsqrt(jnp.mean(x * x, axis=-1, keepdims=True) + EPS)
    return (x * r * g) * (1.0 + sc) + sh


def _mod_kernel(c_ref, w_ref, b_ref, o_ref):
    o_ref[...] = _dot3(_split(_silu(c_ref[...])), _split(w_ref[...])) + b_ref[...]


def _modulation(conds, w_mod, b_mod):
    L, D, D6 = w_mod.shape
    R = conds.shape[0]
    TN = 1024
    return pl.pallas_call(
        _mod_kernel,
        grid=(L, D6 // TN),
        in_specs=[pl.BlockSpec((R, D), lambda l, j: (0, 0)),
                  pl.BlockSpec((None, D, TN), lambda l, j: (l, 0, j)),
                  pl.BlockSpec((None, 1, TN), lambda l, j: (l, 0, j))],
        out_specs=pl.BlockSpec((None, R, TN), lambda l, j: (l, 0, j)),
        out_shape=jax.ShapeDtypeStruct((L, R, D6), F32),
        compiler_params=_params("arbitrary", "arbitrary"),
        name="modulation",
    )(conds, w_mod, b_mod.reshape(L, 1, D6))


def _mod_spec(part, D, TM, rows_per_group, axis):
    def idx(*g):
        return ((g[axis] * TM) // rows_per_group, part, 0, 0)
    return pl.BlockSpec((None, None, 1, D), idx)


MIX_MAIN = A_H * (2 * A_DK + 2 * A_DV)
MIX_TN = 768


def _norm_mm_kernel(x_ref, g_ref, sh_ref, sc_ref, w_ref, we_ref, o_ref, oe_ref, h_scr, *, nmain):
    j = pl.program_id(1)

    @pl.when(j == 0)
    def _():
        h_scr[...] = _norm_mod(x_ref[...], g_ref[...], sh_ref[...], sc_ref[...]).astype(BF16)

    @pl.when(j < nmain)
    def _():
        o_ref[...] = _dot(h_scr[...], w_ref[...].astype(BF16)).astype(o_ref.dtype)

    @pl.when(j == nmain)
    def _():
        oe_ref[...] = _dot(h_scr[...], we_ref[...].astype(BF16))


def _norm_mm(x, g, mod, parts, w, w_extra, extra_block, TM, rows_per_group):
    N, D = x.shape
    nmain = MIX_MAIN // MIX_TN
    WE = extra_block[0]
    return pl.pallas_call(
        functools.partial(_norm_mm_kernel, nmain=nmain),
        grid=(N // TM, nmain + 1),
        in_specs=[pl.BlockSpec((TM, D), lambda i, j: (i, 0)),
                  pl.BlockSpec((1, D), lambda i, j: (0, 0)),
                  _mod_spec(parts[0], D, TM, rows_per_group, 0),
                  _mod_spec(parts[1], D, TM, rows_per_group, 0),
                  pl.BlockSpec((D, MIX_TN), lambda i, j: (0, jnp.minimum(j, nmain - 1))),
                  pl.BlockSpec((D, WE), lambda i, j: (0, extra_block[1]))],
        out_specs=[pl.BlockSpec((TM, MIX_TN), lambda i, j: (i, jnp.minimum(j, nmain - 1))),
                   pl.BlockSpec((TM, WE), lambda i, j: (i, 0))],
        out_shape=[jax.ShapeDtypeStruct((N, MIX_MAIN), BF16), jax.ShapeDtypeStruct((N, WE), F32)],
        scratch_shapes=[pltpu.VMEM((TM, D), BF16)],
        compiler_params=_params("arbitrary", "arbitrary"),
        name="norm_mm",
    )(x, g.reshape(1, D), mod, mod, w, w_extra)


def _ret_kernel(lg_ref, q_ref, k_ref, v_ref, ag_ref, s0f_ref, s0b_ref, ng_ref,
                o_ref, sf_ref, sb_ref, s_scr, of_scr, *, n, C, G):
    d = pl.program_id(1)
    c = pl.program_id(2)

    @pl.when(jnp.logical_and(c == 0, d == 0))
    def _():
        s_scr[...] = s0f_ref[...]

    @pl.when(jnp.logical_and(c == 0, d == 1))
    def _():
        s_scr[...] = s0b_ref[...]

    df = d.astype(F32)
    sgn = 1.0 - 2.0 * df
    ii = lax.broadcasted_iota(jnp.int32, (C, C), 0).astype(F32)
    jj = lax.broadcasted_iota(jnp.int32, (C, C), 1).astype(F32)
    dd = (ii - jj) * sgn
    feeds = dd >= 0.0
    ddc = jnp.maximum(dd, 0.0)
    ri = lax.broadcasted_iota(jnp.int32, (C, 1), 0).astype(F32)
    pos_q = (ri + 1.0) + df * (C - 2.0 * ri - 1.0)
    pos_k = (C - 1.0 - ri) + df * (2.0 * ri - C + 1.0)
    chunk_len = jnp.full((1, A_DV), float(C), F32)

    outs = [[] for _ in range(G)]
    for h in range(A_H):
        lg = lg_ref[d, h]
        dmask = jnp.where(feeds, jnp.exp2(lg * ddc), 0.0)
        q_dec = jnp.exp2(lg * pos_q)
        k_dec = jnp.exp2(lg * pos_k)
        c_dec = jnp.exp2(lg * chunk_len)
        for bb in range(G):
            qh = q_ref[bb, :, h * A_DK:(h + 1) * A_DK].astype(F32) * (A_DK ** -0.5)
            kh = k_ref[bb, :, h * A_DK:(h + 1) * A_DK].astype(F32)
            vh = v_ref[bb, :, h * A_DV:(h + 1) * A_DV].astype(BF16)
            s = s_scr[bb, h]
            att = _dot_nt(qh.astype(BF16), kh.astype(BF16)) * dmask
            o = _dot(att.astype(BF16), vh) + _dot((qh * q_dec).astype(BF16), s.astype(BF16))
            kd = kh * k_dec
            s_scr[bb, h] = c_dec * s + _dot(kd.T.astype(BF16), vh)
            outs[bb].append(o)
    o_all = [jnp.concatenate(o, axis=-1) for o in outs]

    @pl.when(d == 0)
    def _():
        for bb in range(G):
            of_scr[bb, c] = o_all[bb]

    @pl.when(d == 1)
    def _():
        for bb in range(G):
            tot = o_all[bb] + of_scr[bb, n - 1 - c]
            res = []
            for h in range(A_H):
                sl = slice(h * A_DV, (h + 1) * A_DV)
                t = tot[:, sl]
                dev = t - jnp.mean(t, axis=-1, keepdims=True)
                y = dev * lax.rsqrt(jnp.mean(dev * dev, axis=-1, keepdims=True) + EPS) * ng_ref[:, sl]
                res.append(y * _silu(ag_ref[bb, :, sl].astype(F32)))
            o_ref[bb] = jnp.concatenate(res, axis=-1).astype(o_ref.dtype)

    @pl.when(jnp.logical_and(c == n - 1, d == 0))
    def _():
        sf_ref[...] = s_scr[...]

    @pl.when(jnp.logical_and(c == n - 1, d == 1))
    def _():
        sb_ref[...] = s_scr[...]


def _retention(z, B, T, log_gamma, s0f, s0b, norm_g):
    C = RET_CHUNK
    G = _rows_per_step(B, T, A_H * A_DV)
    assert B % G == 0 and T % C == 0
    n = T // C
    HK = A_H * A_DK
    HV = A_H * A_DV
    z3 = z.reshape(B, T, z.shape[1])

    def chunk(d, c):
        return c + d * (n - 1 - 2 * c)

    st_spec = pl.BlockSpec((G, A_H, A_DK, A_DV), lambda b, d, c: (b, 0, 0, 0))
    st_shape = jax.ShapeDtypeStruct((B, A_H, A_DK, A_DV), F32)
    o, sf, sb = pl.pallas_call(
        functools.partial(_ret_kernel, n=n, C=C, G=G),
        grid=(B // G, 2, n),
        in_specs=[pl.BlockSpec(memory_space=pltpu.SMEM),
                  pl.BlockSpec((G, C, HK), lambda b, d, c: (b, chunk(d, c), 0)),
                  pl.BlockSpec((G, C, HK), lambda b, d, c: (b, chunk(d, c), 1)),
                  pl.BlockSpec((G, C, HV), lambda b, d, c: (b, chunk(d, c), 1)),
                  pl.BlockSpec((G, C, HV), lambda b, d, c: (b, chunk(d, c), 2)),
                  st_spec, st_spec,
                  pl.BlockSpec((1, HV), lambda b, d, c: (0, 0))],
        out_specs=[pl.BlockSpec((G, C, HV), lambda b, d, c: (b, (n - 1) - d * c, 0)),
                   st_spec, st_spec],
        out_shape=[jax.ShapeDtypeStruct((B, T, HV), BF16), st_shape, st_shape],
        scratch_shapes=[pltpu.VMEM((G, A_H, A_DK, A_DV), F32), pltpu.VMEM((G, n, C, HV), F32)],
        compiler_params=_params("arbitrary", "arbitrary", "arbitrary"),
        name="retention",
    )(log_gamma * LOG2E, z3, z3, z3, z3, s0f, s0b, norm_g.reshape(1, HV))
    return o.reshape(B * T, HV), sf, sb


def _group_sum_matrix(width, group):
    i = np.arange(width)
    return jnp.asarray((i[:, None] // group == i[None, :] // group).astype(np.float32), dtype=BF16)


def _q_pad_matrix():
    m = np.zeros((B_H * B_HD, B_H * LANES), np.float32)
    g = B_H // B_HKV
    for h in range(B_H):
        for t in range(B_HD):
            m[h * B_HD + t, h * LANES + (h // g) * B_HD + t] = 1.0
    return jnp.asarray(m, dtype=BF16)


def _rope_tables(T):
    rows = T // GRID_W
    row = np.repeat(np.arange(rows, dtype=np.float64), GRID_W)
    col = np.tile(np.arange(GRID_W, dtype=np.float64), rows)
    nq = B_HD // 4
    inv = ROPE_THETA ** (-np.arange(nq, dtype=np.float64) / nq)
    ang = np.concatenate([row[:, None] * inv, col[:, None] * inv], axis=-1)
    cos = np.repeat(np.cos(ang), 2, axis=-1)
    sin = np.repeat(np.sin(ang), 2, axis=-1)
    sign = np.tile(np.array([-1.0, 1.0]), B_HD // 2)
    reps = LANES // B_HD
    return (jnp.asarray(np.tile(cos, (1, reps)), dtype=F32),
            jnp.asarray(np.tile(sin * sign, (1, reps)), dtype=F32))


def _group_rmsnorm(x, gsum, g):
    hi, lo = _split(x * x)
    ss = _dot(hi, gsum) + _dot(lo, gsum)
    return x * lax.rsqrt(ss * (1.0 / B_HD) + EPS) * g


def _rotate_pairs(x, cos, sin_signed):
    n = x.shape[1]
    lane = lax.broadcasted_iota(jnp.int32, x.shape, 1)
    partner = jnp.where(lane % 2 == 0, pltpu.roll(x, n - 1, 1), pltpu.roll(x, 1, 1))
    reps = n // LANES
    if reps > 1:
        cos = jnp.concatenate([cos] * reps, axis=1)
        sin_signed = jnp.concatenate([sin_signed] * reps, axis=1)
    return x * cos + partner * sin_signed


def _bprep_kernel(z_ref, qg_ref, kg_ref, cos_ref, sin_ref, gq_ref, gk_ref, pad_ref,
                  qpad_ref, kn_ref, kr_ref, vb_ref, *, rope):
    nq = B_H * B_HD
    nk = B_HKV * B_HD
    qn = _group_rmsnorm(z_ref[:, 0:nq], gq_ref[...], qg_ref[...])
    kn = _group_rmsnorm(z_ref[:, nq:nq + nk], gk_ref[...], kg_ref[...])
    kn_ref[...] = kn
    if rope:
        qn = _rotate_pairs(qn, cos_ref[...], sin_ref[...])
        kn = _rotate_pairs(kn, cos_ref[...], sin_ref[...])
    kr_ref[...] = kn.astype(BF16)
    vb_ref[...] = z_ref[:, nq + nk:nq + 2 * nk].astype(BF16)
    qs = (qn * (B_HD ** -0.5 * LOG2E)).astype(BF16)
    qpad_ref[...] = _dot(qs, pad_ref[...]).astype(BF16)


def _bprep(z, T, q_g, k_g, rope):
    N = z.shape[0]
    TM = min(512, T)
    nq = B_H * B_HD
    nk = B_HKV * B_HD
    width = nq + 2 * nk
    assert z.shape[1] == width
    cos, sin = _rope_tables(T if rope else TM)
    nt = T // TM if rope else 1
    const = lambda i: (0, 0)
    return pl.pallas_call(
        functools.partial(_bprep_kernel, rope=rope),
        grid=(N // TM,),
        in_specs=[pl.BlockSpec((TM, width), lambda i: (i, 0)),
                  pl.BlockSpec((1, nq), const),
                  pl.BlockSpec((1, nk), const),
                  pl.BlockSpec((TM, LANES), lambda i: (i % nt, 0)),
                  pl.BlockSpec((TM, LANES), lambda i: (i % nt, 0)),
                  pl.BlockSpec((nq, nq), const),
                  pl.BlockSpec((nk, nk), const),
                  pl.BlockSpec((nq, B_H * LANES), const)],
        out_specs=[pl.BlockSpec((TM, B_H * LANES), lambda i: (i, 0)),
                   pl.BlockSpec((TM, nk), lambda i: (i, 0)),
                   pl.BlockSpec((TM, nk), lambda i: (i, 0)),
                   pl.BlockSpec((TM, nk), lambda i: (i, 0))],
        out_shape=[jax.ShapeDtypeStruct((N, B_H * LANES), BF16),
                   jax.ShapeDtypeStruct((N, nk), F32),
                   jax.ShapeDtypeStruct((N, nk), BF16),
                   jax.ShapeDtypeStruct((N, nk), BF16)],
        compiler_params=_params("arbitrary"),
        name="attn_prep",
    )(z, jnp.tile(q_g, B_H).reshape(1, nq), jnp.tile(k_g, B_HKV).reshape(1, nk), cos, sin,
      _group_sum_matrix(nq, B_HD), _group_sum_matrix(nk, B_HD), _q_pad_matrix())


def _lane_fold(x, op):
    acc = x[:, 0:LANES]
    for j in range(1, x.shape[1] // LANES):
        acc = op(acc, x[:, j * LANES:(j + 1) * LANES])
    return acc


def _attn_kernel(*refs, has_cache, kc, nq):
    if has_cache:
        q_ref, k_ref, v_ref, ck_ref, cv_ref, o_ref, s_scr, m_scr, mprev_scr, l_scr, acc_scr = refs
        kcc = min(kc, ck_ref.shape[0])
        ncache = ck_ref.shape[0] // kcc
    else:
        q_ref, k_ref, v_ref, o_ref, s_scr, m_scr, mprev_scr, l_scr, acc_scr = refs
        kcc, ncache = kc, 0
    i = pl.program_id(1)
    tq = q_ref.shape[0]
    nlat = k_ref.shape[0] // kc

    def score(c, kblk):
        q = jnp.concatenate([q_ref[:, h * LANES:(h + 1) * LANES] for h in range(B_H)], axis=0)
        s = _dot_nt(q, kblk)
        s_scr[c, :, 0:kblk.shape[0]] = s
        m_scr[...] = jnp.maximum(m_scr[...], _lane_fold(s, jnp.maximum))

    def weight(c, vblk):
        s = s_scr[c, :, 0:vblk.shape[0]]
        mp = mprev_scr[...]
        ps = [jnp.exp2(s[:, j * LANES:(j + 1) * LANES] - mp) for j in range(vblk.shape[0] // LANES)]
        tot = ps[0]
        for pj in ps[1:]:
            tot = tot + pj
        l_scr[...] += tot
        acc_scr[...] += _dot(jnp.concatenate(ps, axis=1).astype(BF16), vblk)

    def run(do_weight, do_score):
        def unit(c, kblk, vblk):
            if do_weight:
                weight(c, vblk())
            if do_score:
                score(c, kblk())

        for c in range(ncache):
            unit(c, lambda: ck_ref[c * kcc:(c + 1) * kcc, :].astype(BF16),
                 lambda: cv_ref[c * kcc:(c + 1) * kcc, :].astype(BF16))

        def body(c, carry):
            rows = pl.ds(pl.multiple_of(c * kc, kc), kc)
            unit(ncache + c, lambda: k_ref[rows, :], lambda: v_ref[rows, :])
            return carry
        lax.fori_loop(0, nlat, body, 0)

    @pl.when(i < nq)
    def _():
        m_scr[...] = jnp.full(m_scr.shape, -jnp.inf, F32)

    @pl.when(i > 0)
    def _():
        l_scr[...] = jnp.zeros_like(l_scr)
        acc_scr[...] = jnp.zeros_like(acc_scr)

    @pl.when(i == 0)
    def _():
        run(False, True)

    @pl.when(jnp.logical_and(i > 0, i < nq))
    def _():
        run(True, True)

    @pl.when(i == nq)
    def _():
        run(True, False)

    @pl.when(i > 0)
    def _():
        r_all = acc_scr[...] / jnp.sum(l_scr[...], axis=-1, keepdims=True)
        g = B_H // B_HKV
        lane = lax.broadcasted_iota(jnp.int32, (tq, LANES), 1)
        outs = []
        for j in range(B_H // 2):
            pair = []
            for half in range(2):
                h = 2 * j + half
                r = r_all[h * tq:(h + 1) * tq, :]
                if h // g != half:
                    r = pltpu.roll(r, B_HD, 1)
                pair.append(r)
            outs.append(jnp.where(lane < B_HD, pair[0], pair[1]))
        o_ref[...] = jnp.concatenate(outs, axis=-1).astype(o_ref.dtype)

    @pl.when(i < nq)
    def _():
        mprev_scr[...] = jnp.broadcast_to(jnp.max(m_scr[...], axis=-1, keepdims=True), mprev_scr.shape)


def _attention(qpad, kr, vb, B, T, cache_k, cache_v):
    has_cache = cache_k is not None
    TQ = Q_TILE
    nq = T // TQ
    nk = B_HKV * B_HD
    in_specs = [pl.BlockSpec((TQ, B_H * LANES), lambda b, i: (b * nq + jnp.minimum(i, nq - 1), 0)),
                pl.BlockSpec((T, nk), lambda b, i: (b, 0)),
                pl.BlockSpec((T, nk), lambda b, i: (b, 0))]
    args = [qpad, kr, vb]
    kc = min(1024, T)
    nchunks = T // kc
    if has_cache:
        P = cache_k.shape[1]
        assert P % min(kc, P) == 0
        nchunks += P // min(kc, P)
        in_specs += [pl.BlockSpec((None, P, nk), lambda b, i: (b, 0, 0))] * 2
        args += [cache_k, cache_v]
    R = B_H * TQ
    return pl.pallas_call(
        functools.partial(_attn_kernel, has_cache=has_cache, kc=kc, nq=nq),
        grid=(B, nq + 1),
        in_specs=in_specs,
        out_specs=pl.BlockSpec((TQ, B_H * B_HD), lambda b, i: (b * nq + jnp.maximum(i - 1, 0), 0)),
        out_shape=jax.ShapeDtypeStruct((B * T, B_H * B_HD), BF16),
        scratch_shapes=[pltpu.VMEM((nchunks, R, kc), F32)] + [pltpu.VMEM((R, LANES), F32)] * 4,
        compiler_params=_params("arbitrary", "arbitrary"),
        name="attention",
    )(*args)


def _proj_res_kernel(*refs, n_in):
    x_ref, gate_ref = refs[0], refs[1]
    o_refs = refs[2:2 + n_in]
    w_refs = refs[2 + n_in:2 + 2 * n_in]
    out_ref = refs[2 + 2 * n_in]
    wbf_refs = refs[3 + 2 * n_in:]

    @pl.when(pl.program_id(0) == 0)
    def _():
        for w_ref, wbf_ref in zip(w_refs, wbf_refs):
            wbf_ref[...] = w_ref[...].astype(BF16)

    acc = _dot(o_refs[0][...], wbf_refs[0][...])
    for o_ref, wbf_ref in zip(o_refs[1:], wbf_refs[1:]):
        acc = acc + _dot(o_ref[...], wbf_ref[...])
    out_ref[...] = x_ref[...] + gate_ref[...] * acc


def _proj_res(x, mod, part, acts, w, rows_per_group):
    N, D = x.shape
    TM = min(1024, rows_per_group)
    n_in = len(acts)
    widths = [a.shape[1] for a in acts]
    offs = np.cumsum([0] + widths[:-1]).tolist()
    in_specs = [pl.BlockSpec((TM, D), lambda i: (i, 0)),
                _mod_spec(part, D, TM, rows_per_group, 0)]
    in_specs += [pl.BlockSpec((TM, wd), lambda i: (i, 0)) for wd in widths]
    in_specs += [pl.BlockSpec((wd, D), functools.partial(lambda i, blk: (blk, 0), blk=off // wd))
                 for wd, off in zip(widths, offs)]
    return pl.pallas_call(
        functools.partial(_proj_res_kernel, n_in=n_in),
        grid=(N // TM,),
        in_specs=in_specs,
        out_specs=pl.BlockSpec((TM, D), lambda i: (i, 0)),
        out_shape=jax.ShapeDtypeStruct((N, D), F32),
        scratch_shapes=[pltpu.VMEM((wd, D), BF16) for wd in widths],
        compiler_params=_params("arbitrary"),
        name="proj_residual",
    )(x, mod, *acts, *([w] * n_in))


def _ffn_kernel(x_ref, g_ref, sh_ref, sc_ref, gate_ref, w1_ref, w3_ref, w2_ref, out_ref, h_scr, acc_scr, *, nf):
    f = pl.program_id(1)

    @pl.when(f == 0)
    def _():
        h_scr[...] = _norm_mod(x_ref[...], g_ref[...], sh_ref[...], sc_ref[...]).astype(BF16)
        acc_scr[...] = jnp.zeros_like(acc_scr)

    h = h_scr[...]
    a = _dot(h, w1_ref[...].astype(BF16))
    b = _dot(h, w3_ref[...].astype(BF16))
    acc_scr[...] += _dot((_silu(a) * b).astype(BF16), w2_ref[...].astype(BF16))

    @pl.when(f == nf - 1)
    def _():
        out_ref[...] = x_ref[...] + gate_ref[...] * acc_scr[...]


def _ffn(x, g, mod, w1, w3, w2, rows_per_group):
    N, D = x.shape
    FF = w1.shape[1]
    TM, TF = min(1024, rows_per_group), 256
    nf = FF // TF
    return pl.pallas_call(
        functools.partial(_ffn_kernel, nf=nf),
        grid=(N // TM, nf),
        in_specs=[pl.BlockSpec((TM, D), lambda i, f: (i, 0)),
                  pl.BlockSpec((1, D), lambda i, f: (0, 0)),
                  _mod_spec(3, D, TM, rows_per_group, 0),
                  _mod_spec(4, D, TM, rows_per_group, 0),
                  _mod_spec(5, D, TM, rows_per_group, 0),
                  pl.BlockSpec((D, TF), lambda i, f: (0, f)),
                  pl.BlockSpec((D, TF), lambda i, f: (0, f)),
                  pl.BlockSpec((TF, D), lambda i, f: (f, 0))],
        out_specs=pl.BlockSpec((TM, D), lambda i, f: (i, 0)),
        out_shape=jax.ShapeDtypeStruct((N, D), F32),
        scratch_shapes=[pltpu.VMEM((TM, D), BF16), pltpu.VMEM((TM, D), F32)],
        compiler_params=_params("arbitrary", "arbitrary"),
        name="ffn",
    )(x, g.reshape(1, D), mod, mod, mod, w1, w3, w2)


MOE_SB = 1024
MOE_SBG = 512
MOE_GG = 4
MOE_TRG = 256
MOE_TR = 2048
MOE_CG = 4


def _two_stream_specs(shape, ntp, ax=0):
    def idx_p(*g):
        return (jnp.minimum(g[ax], ntp - 1), 0)

    def idx_s(*g):
        return (jnp.maximum(g[ax] - ntp, 0), 0)
    return pl.BlockSpec(shape, idx_p), pl.BlockSpec(shape, idx_s)


def _pool_mod_spec(part, D, TM, ntp, rows_per_group):
    def idx(i, *_):
        return (jnp.where(i < ntp, 0, 1 + ((i - ntp) * TM) // rows_per_group), part, 0, 0)
    return pl.BlockSpec((None, None, 1, D), idx)


def _route_kernel(xp_ref, xs_ref, g_ref, sh_ref, sc_ref, rw_ref, tri_ref, h_ref, info_ref, infot_ref, cum_ref,
                  carry_scr, *, ntp, cap):
    i = pl.program_id(0)

    @pl.when(i == 0)
    def _():
        carry_scr[...] = jnp.zeros_like(carry_scr)

    x = jnp.where(i < ntp, xp_ref[...], xs_ref[...])
    h = _norm_mod(x, g_ref[...], sh_ref[...], sc_ref[...])
    h_ref[...] = h.astype(BF16)
    lane = lax.broadcasted_iota(jnp.int32, (x.shape[0], LANES), 1).astype(F32)
    logits = _dot3(_split(h), _split(rw_ref[...]))
    logits = jnp.where(lane < N_EXPERTS, logits, -jnp.inf)
    m1 = jnp.max(logits, axis=-1, keepdims=True)
    i1 = jnp.min(jnp.where(logits == m1, lane, float(LANES)), axis=-1, keepdims=True)
    rest = jnp.where(lane == i1, -jnp.inf, logits)
    m2 = jnp.max(rest, axis=-1, keepdims=True)
    i2 = jnp.min(jnp.where(rest == m2, lane, float(LANES)), axis=-1, keepdims=True)
    e2 = jnp.exp(m2 - m1)
    w1 = 1.0 / (1.0 + e2)
    w2 = e2 / (1.0 + e2)
    ind = jnp.where(jnp.logical_or(lane == i1, lane == i2), 1.0, 0.0)
    before = _dot(tri_ref[...], ind.astype(BF16)) + carry_scr[...]
    r1 = jnp.sum(jnp.where(lane == i1, before, 0.0), axis=-1, keepdims=True) + i1 * float(cap)
    r2 = jnp.sum(jnp.where(lane == i2, before, 0.0), axis=-1, keepdims=True) + i2 * float(cap)
    total = carry_scr[...] + jnp.sum(ind, axis=0, keepdims=True)
    carry_scr[...] = total
    for part in range(1, MOE_SB // MOE_SBG):
        cum_ref[part - 1] = before[part * MOE_SBG:part * MOE_SBG + 1, :]
    cum_ref[MOE_SB // MOE_SBG - 1] = total
    info = jnp.where(lane == 0.0, i1, jnp.where(lane == 1.0, i2, jnp.where(lane == 2.0, w1, jnp.where(
        lane == 3.0, w2, jnp.where(lane == 4.0, r1, jnp.where(lane == 5.0, r2, 0.0))))))
    info_ref[...] = info[:, 0:SUB]
    info_t = jnp.concatenate([info[r:r + LANES, :].T for r in range(0, info.shape[0], LANES)], axis=1)
    infot_ref[...] = info_t[0:SUB, :]


def _moe_route(xp, xs, g, mod, router_w, rows_per_group):
    Np, D = xp.shape
    N = Np + xs.shape[0]
    TM = MOE_SB
    ntp = Np // TM
    nt = N // TM
    rw = jnp.pad(router_w, ((0, 0), (0, LANES - router_w.shape[1])))
    tri = jnp.asarray(np.tril(np.ones((TM, TM), np.float32), -1), dtype=BF16)
    xp_spec, xs_spec = _two_stream_specs((TM, D), ntp)
    return pl.pallas_call(
        functools.partial(_route_kernel, ntp=ntp, cap=_moe_cap(N)),
        grid=(nt,),
        in_specs=[xp_spec, xs_spec,
                  pl.BlockSpec((1, D), lambda i: (0, 0)),
                  _pool_mod_spec(3, D, TM, ntp, rows_per_group),
                  _pool_mod_spec(4, D, TM, ntp, rows_per_group),
                  pl.BlockSpec((D, LANES), lambda i: (0, 0)),
                  pl.BlockSpec((TM, TM), lambda i: (0, 0))],
        out_specs=[pl.BlockSpec((TM, D), lambda i: (i, 0)),
                   pl.BlockSpec((TM, SUB), lambda i: (i, 0)),
                   pl.BlockSpec((SUB, TM), lambda i: (0, i)),
                   pl.BlockSpec((MOE_SB // MOE_SBG, 1, LANES), lambda i: (i, 0, 0))],
        out_shape=[jax.ShapeDtypeStruct((N, D), BF16),
                   jax.ShapeDtypeStruct((N, SUB), F32),
                   jax.ShapeDtypeStruct((SUB, N), F32),
                   jax.ShapeDtypeStruct((nt * (MOE_SB // MOE_SBG), 1, LANES), F32)],
        scratch_shapes=[pltpu.VMEM((1, LANES), F32)],
        compiler_params=_params("arbitrary"),
        name="moe_route",
    )(xp, xs, g.reshape(1, D), mod, mod, rw, tri)


def _hold_unused(idx, used):
    steps, slots = idx.shape
    read = jnp.arange(slots, dtype=jnp.int32)[None, :] < used[:, None]
    step = jnp.arange(steps, dtype=jnp.int32)[:, None]
    last = lax.cummax(jnp.where(read, step, -1), axis=0)
    held = jnp.take_along_axis(idx, jnp.maximum(last, 0), axis=0)
    return jnp.where(last >= 0, held, 0)


def _moe_cap(N):
    return -(-N // MOE_TR) * MOE_TR


def _moe_plan(cum, N):
    E, SB, TRG, TR = N_EXPERTS, MOE_SB, MOE_TRG, MOE_TR
    NB = N // SB
    cap = _moe_cap(N)
    tpe_g, tpe = cap // TRG, cap // TR
    RG = E * tpe_g
    used_g, used_t = 2 * N // TRG + E, 2 * N // TR + E
    PMAX = used_g + E * NB
    i32 = jnp.int32
    parts = SB // MOE_SBG
    cum_g = cum[:, 0, :E].astype(i32).T
    cum_e = cum_g[:, parts - 1::parts]
    cnt = cum_e[:, -1]

    lo = jnp.asarray((np.arange(RG) % tpe_g) * TRG, dtype=i32)
    hi = jnp.minimum(lo + TRG, jnp.repeat(cnt, tpe_g))
    cum_e_t = jnp.repeat(cum_e, tpe_g, axis=0)
    first = jnp.sum(cum_e_t <= lo[:, None], axis=1)
    last = jnp.sum(cum_e_t < hi[:, None], axis=1)
    nblk = jnp.where(hi > lo, last - first + 1, 0)
    pend = jnp.cumsum(nblk)
    npairs = pend[-1]
    p = jnp.arange(PMAX, dtype=i32)
    valid = p < npairs
    pc = jnp.minimum(p, npairs - 1)
    q_of = jnp.minimum(jnp.sum(pend[None, :] <= pc[:, None], axis=1), RG - 1).astype(i32)
    pstart = pend - nblk
    s_of = (first[q_of] + pc - pstart[q_of]).astype(i32)

    GG = MOE_GG
    cum_g_t = jnp.repeat(cum_g, tpe_g, axis=0)
    first_g = jnp.sum(cum_g_t <= lo[:, None], axis=1)
    last_g = jnp.sum(cum_g_t < hi[:, None], axis=1)
    nblk_g = jnp.where(hi > lo, last_g - first_g + 1, 0)
    nst = (nblk_g + GG - 1) // GG
    gst_end = jnp.cumsum(nst)
    gtotal = gst_end[-1]
    smax_g = (used_g + E * NB * parts + (GG - 1) * used_g) // GG + 1
    jg = jnp.arange(smax_g, dtype=i32)
    g_ok = jg < gtotal
    jgc = jnp.minimum(jg, gtotal - 1)
    tq = jnp.minimum(jnp.sum(gst_end[None, :] <= jgc[:, None], axis=1), RG - 1).astype(i32)
    gg = jgc - (gst_end - nst)[tq]
    g_slots = jnp.where(g_ok, jnp.clip(nblk_g[tq] - GG * gg, 0, GG), 0)
    g_parts = _hold_unused((first_g[tq] + GG * gg)[:, None] + jnp.arange(GG, dtype=i32)[None, :], g_slots)
    g_first = jnp.logical_and(g_ok, gg == 0)
    gather_plan = (tq, g_parts.reshape(-1).astype(i32), g_slots.astype(i32), g_first.astype(i32))

    order = jnp.argsort(jnp.where(valid, s_of * RG + q_of, jnp.iinfo(jnp.int32).max))
    s2, q2 = s_of[order], q_of[order]
    CG = MOE_CG
    blocks = jnp.arange(NB, dtype=i32)
    per_blk = jnp.sum(jnp.logical_and(valid[None, :], s2[None, :] == blocks[:, None]), axis=1)
    pb_end = jnp.cumsum(per_blk)
    pb_start = pb_end - per_blk
    nsteps = (per_blk + CG - 1) // CG
    st_end = jnp.cumsum(nsteps)
    total = st_end[-1]
    SMAX = (PMAX + CG - 1) // CG + NB
    j = jnp.arange(SMAX, dtype=i32)
    step_ok = j < total
    jc = jnp.minimum(j, total - 1)
    blk = jnp.minimum(jnp.sum(st_end[None, :] <= jc[:, None], axis=1), NB - 1).astype(i32)
    grp = jc - (st_end - nsteps)[blk]
    slot_p = pb_start[blk][:, None] + CG * grp[:, None] + jnp.arange(CG, dtype=i32)[None, :]
    slot_ok = jnp.logical_and(slot_p < pb_end[blk][:, None], step_ok[:, None])
    c_slots = jnp.sum(slot_ok, axis=1).astype(i32)
    slot_q = _hold_unused(q2[jnp.minimum(slot_p, npairs - 1)], c_slots)
    c_first = jnp.logical_and(step_ok, grp == 0).astype(i32)
    c_last = jnp.logical_and(step_ok, grp == nsteps[blk] - 1).astype(i32)
    combine_plan = (blk, slot_q.reshape(-1).astype(i32), c_slots, c_first, c_last)

    tiles = (cnt + TR - 1) // TR
    t_end = jnp.cumsum(tiles)
    jt = jnp.arange(used_t, dtype=i32)
    t_ok = jt < t_end[-1]
    jtc = jnp.minimum(jt, t_end[-1] - 1)
    te = jnp.minimum(jnp.sum(t_end[None, :] <= jtc[:, None], axis=1), E - 1).astype(i32)
    tk = jtc - (t_end - tiles)[te]
    tvalid = jnp.where(t_ok, jnp.clip(cnt[te] - tk * TR, 0, TR), 0)
    ffn_plan = ((te * tpe + tk).astype(i32), te, tvalid.astype(i32))
    return gather_plan, combine_plan, ffn_plan, E * cap


def _moe_gather_kernel(q_ref, s_ref, slots_ref, first_ref, *refs):
    pos_refs, h_refs, out_ref = refs[:MOE_GG], refs[MOE_GG:2 * MOE_GG], refs[2 * MOE_GG]
    p = pl.program_id(0)
    rows = out_ref.shape[0]

    @pl.when(first_ref[p] == 1)
    def _():
        out_ref[...] = jnp.zeros_like(out_ref)

    for ns in range(1, MOE_GG + 1):
        @pl.when(slots_ref[p] == ns)
        def _(ns=ns):
            row = (lax.broadcasted_iota(jnp.int32, (rows, 1), 0) + q_ref[p] * rows).astype(F32)
            sels = []
            for k in range(ns):
                hit = jnp.logical_or(pos_refs[k][4:5, :] == row, pos_refs[k][5:6, :] == row)
                sels.append(jnp.where(hit, 1.0, 0.0).astype(BF16))
            sel = sels[0] if ns == 1 else jnp.concatenate(sels, axis=1)
            hs = h_refs[0][...] if ns == 1 else jnp.concatenate([h_refs[k][...] for k in range(ns)], axis=0)
            out_ref[...] = out_ref[...] + _dot(sel, hs).astype(BF16)


def _moe_gather(h, pos_rows, plan, rmax):
    N, D = h.shape
    nsteps = plan[0].shape[0]

    def pos_spec(k):
        return pl.BlockSpec((SUB, MOE_SBG), lambda p, q, s, *_: (0, s[MOE_GG * p + k]))

    def tok_spec(k):
        return pl.BlockSpec((MOE_SBG, D), lambda p, q, s, *_: (s[MOE_GG * p + k], 0))

    return pl.pallas_call(
        _moe_gather_kernel,
        grid_spec=pltpu.PrefetchScalarGridSpec(
            num_scalar_prefetch=4, grid=(nsteps,),
            in_specs=[pos_spec(k) for k in range(MOE_GG)] + [tok_spec(k) for k in range(MOE_GG)],
            out_specs=pl.BlockSpec((MOE_TRG, D), lambda p, q, *_: (q[p], 0))),
        out_shape=jax.ShapeDtypeStruct((rmax, D), BF16),
        compiler_params=_params("arbitrary"),
        name="moe_gather",
    )(*plan, *([pos_rows] * MOE_GG), *([h] * MOE_GG))


def _moe_ffn_kernel(t_ref, e_ref, nv_ref, x_ref, w1_ref, w3_ref, w2_ref, out_ref, acc_scr, *, nf):
    t = pl.program_id(0)
    f = pl.program_id(1)
    nv = nv_ref[t]

    def block(start, size):
        rows = pl.ds(start, size)

        @pl.when(f == 0)
        def _():
            acc_scr[rows, :] = jnp.zeros((size, acc_scr.shape[1]), F32)

        x = x_ref[rows, :]
        a = _dot(x, w1_ref[...].astype(BF16))
        b = _dot(x, w3_ref[...].astype(BF16))
        acc_scr[rows, :] += _dot((_silu(a) * b).astype(BF16), w2_ref[...].astype(BF16))

        @pl.when(f == nf - 1)
        def _():
            out_ref[rows, :] = acc_scr[rows, :].astype(out_ref.dtype)

    nsub = MOE_TR // MOE_TRG
    used = (nv + MOE_TRG - 1) // MOE_TRG

    @pl.when(used == nsub)
    def _():
        block(0, MOE_TR)

    @pl.when(jnp.logical_and(used > 0, used < nsub))
    def _():
        start = jnp.int32(0)
        size = MOE_TR // 2
        while size >= MOE_TRG:
            has = (used & (size // MOE_TRG)) != 0

            @pl.when(has)
            def _(start=start, size=size):
                block(pl.multiple_of(start, MOE_TRG), size)

            start = start + jnp.where(has, size, 0)
            size //= 2


def _moe_ffn(xs, plan, w1, w3, w2):
    rmax, D = xs.shape
    FF = w1.shape[2]
    TF = 256
    nf = FF // TF
    RT = plan[0].shape[0]

    def fidx(t, f, nv):
        return jnp.where(nv[t] > 0, f, nf - 1)

    return pl.pallas_call(
        functools.partial(_moe_ffn_kernel, nf=nf),
        grid_spec=pltpu.PrefetchScalarGridSpec(
            num_scalar_prefetch=3, grid=(RT, nf),
            in_specs=[pl.BlockSpec((MOE_TR, D), lambda t, f, ti, e, nv: (ti[t], 0)),
                      pl.BlockSpec((None, D, TF), lambda t, f, ti, e, nv: (e[t], 0, fidx(t, f, nv))),
                      pl.BlockSpec((None, D, TF), lambda t, f, ti, e, nv: (e[t], 0, fidx(t, f, nv))),
                      pl.BlockSpec((None, TF, D), lambda t, f, ti, e, nv: (e[t], fidx(t, f, nv), 0))],
            out_specs=pl.BlockSpec((MOE_TR, D), lambda t, f, ti, e, nv: (ti[t], 0)),
            scratch_shapes=[pltpu.VMEM((MOE_TR, D), F32)]),
        out_shape=jax.ShapeDtypeStruct((rmax, D), BF16),
        compiler_params=_params("arbitrary", "arbitrary"),
        name="moe_ffn",
    )(*plan, xs, w1, w3, w2)


def _moe_combine_kernel(s_ref, q_ref, slots_ref, first_ref, last_ref, pos_ref, *refs, ntp):
    ys_refs = refs[:MOE_CG]
    xp_ref, xs_ref, gate_ref, fg_ref, op_ref, os_ref, acc_scr = refs[MOE_CG:]
    p = pl.program_id(0)
    rows = ys_refs[0].shape[0]

    @pl.when(first_ref[p] == 1)
    def _():
        acc_scr[...] = jnp.zeros_like(acc_scr)

    for ns in range(1, MOE_CG + 1):
        @pl.when(slots_ref[p] == ns)
        def _(ns=ns):
            sels = []
            for k in range(ns):
                col = (lax.broadcasted_iota(jnp.int32, (1, rows), 1) + q_ref[MOE_CG * p + k] * rows).astype(F32)
                sels.append((jnp.where(pos_ref[:, 4:5] == col, pos_ref[:, 2:3], 0.0)
                             + jnp.where(pos_ref[:, 5:6] == col, pos_ref[:, 3:4], 0.0)).astype(BF16))
            sel = sels[0] if ns == 1 else jnp.concatenate(sels, axis=1)
            ys = ys_refs[0][...] if ns == 1 else jnp.concatenate([ys_refs[k][...] for k in range(ns)], axis=0)
            acc_scr[...] += _dot(sel, ys)

    @pl.when(last_ref[p] == 1)
    def _():
        s = s_ref[p]
        x = jnp.where(s < ntp, xp_ref[...], xs_ref[...])
        y = x + gate_ref[...] * acc_scr[...]
        out = y * lax.rsqrt(jnp.mean(y * y, axis=-1, keepdims=True) + EPS) * fg_ref[...]

        @pl.when(s < ntp)
        def _():
            op_ref[...] = out

        @pl.when(s >= ntp)
        def _():
            os_ref[...] = out


def _moe_combine(ys, pos_cols, plan, xp, xs, mod, final_g, rows_per_group):
    Np, D = xp.shape
    Ns = xs.shape[0]
    SB = MOE_SB
    ntp = Np // SB
    nsteps = plan[0].shape[0]

    def tile_spec(k):
        return pl.BlockSpec((MOE_TRG, D), lambda p, s, q, *_: (q[MOE_CG * p + k], 0))

    def tok_p(p, s, *_):
        return (jnp.minimum(s[p], ntp - 1), 0)

    def tok_s(p, s, *_):
        return (jnp.maximum(s[p] - ntp, 0), 0)

    def gate_idx(p, s, *_):
        return (jnp.where(s[p] < ntp, 0, 1 + ((s[p] - ntp) * SB) // rows_per_group), 5, 0, 0)

    return pl.pallas_call(
        functools.partial(_moe_combine_kernel, ntp=ntp),
        grid_spec=pltpu.PrefetchScalarGridSpec(
            num_scalar_prefetch=5, grid=(nsteps,),
            in_specs=[pl.BlockSpec((SB, SUB), lambda p, s, q, *_: (s[p], 0))]
            + [tile_spec(k) for k in range(MOE_CG)]
            + [pl.BlockSpec((SB, D), tok_p),
                      pl.BlockSpec((SB, D), tok_s),
                      pl.BlockSpec((None, None, 1, D), gate_idx),
                      pl.BlockSpec((1, D), lambda p, *_: (0, 0))],
            out_specs=[pl.BlockSpec((SB, D), tok_p), pl.BlockSpec((SB, D), tok_s)],
            scratch_shapes=[pltpu.VMEM((SB, D), F32)]),
        out_shape=[jax.ShapeDtypeStruct((Np, D), F32), jax.ShapeDtypeStruct((Ns, D), F32)],
        compiler_params=_params("arbitrary"),
        name="moe_combine",
    )(*plan, pos_cols, *([ys] * MOE_CG), xp, xs, mod, final_g.reshape(1, D))


def _moe(xp, xs, g, mod, router_w, w1, w3, w2, final_g, rows_per_group):
    N = xp.shape[0] + xs.shape[0]
    h, info, info_t, cum = _moe_route(xp, xs, g, mod, router_w, rows_per_group)
    gather_plan, combine_plan, ffn_plan, rmax = _moe_plan(cum, N)
    x_sorted = _moe_gather(h, info_t, gather_plan, rmax)
    y_sorted = _moe_ffn(x_sorted, ffn_plan, w1, w3, w2)
    return _moe_combine(y_sorted, info, combine_plan, xp, xs, mod, final_g, rows_per_group)


def _gla_levels(C):
    lv, c = [], C // 2
    while c >= SUB:
        lv.append(c)
        c //= 2
    return lv


def _gla_tables(C):
    levels = _gla_levels(C)
    nr = 1 + len(levels)
    mat = np.zeros((2, nr * C, C), np.float32)
    code = np.zeros((2, C, C), np.int32)
    for d in range(2):
        p = np.arange(C) if d == 0 else C - 1 - np.arange(C)
        pi, pj = p[:, None], p[None, :]
        mat[d, 0:C] = pj <= pi
        code[d] = np.where((pj <= pi) & (pi // SUB == pj // SUB), 1, 0)
        for lv, c in enumerate(levels):
            blk = pi // c
            later = blk % 2 == 1
            mat[d, (1 + lv) * C:(2 + lv) * C] = ((later & (pj > blk * c - 1) & (pj <= pi))
                                                 | (~later & (pj > pi) & (pj <= (blk + 1) * c - 1)))
            pair = (pi // (2 * c) == pj // (2 * c)) & (pi // c != pj // c) & (pj <= pi)
            code[d] = np.where(pair, 2 + lv, code[d])
    ones = np.zeros((SUB * LANES, C), np.float32)
    for jj in range(SUB):
        ones[jj * LANES:(jj + 1) * LANES, jj::SUB] = 1.0
    return jnp.asarray(mat, dtype=BF16), jnp.asarray(code), jnp.asarray(ones, dtype=BF16)


def _bcast_sublane(x, jj):
    r, w = x.shape
    x3 = x.reshape(r // SUB, SUB, w)
    return jnp.broadcast_to(x3[:, jj:jj + 1, :], x3.shape).reshape(r, w)


def _t128(x):
    r, w = x.shape
    if w > LANES:
        return jnp.concatenate([x[:, i:i + LANES].T for i in range(0, w, LANES)], axis=0)
    return jnp.concatenate([x[i:i + LANES, :].T for i in range(0, r, LANES)], axis=1)


def _gla_kernel(q_ref, k_ref, v_ref, g_ref, lr_ref, wg_ref, ba_ref, mat_ref, code_ref, ones_ref,
                s0f_ref, s0b_ref, ng_ref, o_ref, sf_ref, sb_ref, st_scr, of_scr, *, n, C, G):
    d = pl.program_id(1)
    c = pl.program_id(2)
    levels = _gla_levels(C)

    @pl.when(jnp.logical_and(c == 0, d == 0))
    def _():
        for bb in range(G):
            for h in range(C_H):
                st_scr[bb, h] = _t128(s0f_ref[bb, h])

    @pl.when(jnp.logical_and(c == 0, d == 1))
    def _():
        for bb in range(G):
            for h in range(C_H):
                st_scr[bb, h] = _t128(s0b_ref[bb, h])

    mat = mat_ref[...]
    code = code_ref[...]
    ones = ones_ref[...]
    wg = _split(wg_ref[...])
    cums = []
    for bb in range(G):
        xg = _dot3(_split(lr_ref[bb]), wg) + ba_ref[...]
        la = (jnp.minimum(xg, 0.0) - jnp.log1p(jnp.exp(-jnp.abs(xg)))) * (LOG2E / C_TAU)
        hi, lo = _split(la)
        cums.append(_dot(mat, hi) + _dot(mat, lo))

    def prepare(bb, h):
        cum = cums[bb]
        ks = slice(h * C_DK, (h + 1) * C_DK)
        qh = q_ref[bb, :, ks].astype(F32) * (C_DK ** -0.5)
        kh = k_ref[bb, :, ks].astype(F32)
        b = cum[0:C, ks]
        b_end = jnp.min(b, axis=0, keepdims=True)
        b_rest = b_end - b
        ps = []
        for jj in range(SUB):
            dec = jnp.exp2(jnp.minimum(b - _bcast_sublane(b, jj), 0.0))
            ps.append((qh * _bcast_sublane(kh, jj) * dec).astype(BF16))
        lv_ops = []
        for lv in range(len(levels)):
            fac = jnp.exp2(cum[(1 + lv) * C:(2 + lv) * C, ks])
            lv_ops.append(((qh * fac).astype(BF16), (kh * fac).astype(BF16)))
        qe = (qh * jnp.exp2(b)).astype(BF16)
        ke = (kh * jnp.exp2(b_rest)).astype(BF16)
        e_end = jnp.exp2(b_end)
        return jnp.concatenate(ps, axis=1), lv_ops, qe, ke, e_end

    def contract(bb, h, prep):
        pcat, lv_ops, qe, ke, e_end = prep
        vh = v_ref[bb, :, h * C_DV:(h + 1) * C_DV].astype(F32)
        att = jnp.where(code == 1, _dot(pcat, ones), 0.0)
        for lv, (qs, ks_) in enumerate(lv_ops):
            att = jnp.where(code == 2 + lv, _dot_nt(qs, ks_), att)
        st = st_scr[bb, h]
        o = _dot(att.astype(BF16), vh.astype(BF16)) + _dot_nt(qe, st.astype(BF16))
        st_scr[bb, h] = e_end * st + _dot(_t128(vh).astype(BF16), ke)
        return o

    units = [(bb, h) for h in range(C_H) for bb in range(G)]
    outs = {}
    prep = prepare(*units[0])
    for idx, (bb, h) in enumerate(units):
        nxt = prepare(*units[idx + 1]) if idx + 1 < len(units) else None
        outs[(bb, h)] = contract(bb, h, prep)
        prep = nxt
    o_all = [jnp.concatenate([outs[(bb, h)] for h in range(C_H)], axis=-1) for bb in range(G)]

    @pl.when(d == 0)
    def _():
        for bb in range(G):
            of_scr[bb, c] = o_all[bb]

    @pl.when(d == 1)
    def _():
        for bb in range(G):
            tot = o_all[bb] + of_scr[bb, n - 1 - c]
            res = []
            for h in range(C_H):
                sl = slice(h * C_DV, (h + 1) * C_DV)
                t = tot[:, sl]
                y = t * lax.rsqrt(jnp.mean(t * t, axis=-1, keepdims=True) + EPS) * ng_ref[:, sl]
                res.append(y * _silu(g_ref[bb, :, sl].astype(F32)))
            o_ref[bb] = jnp.concatenate(res, axis=-1).astype(o_ref.dtype)

    @pl.when(jnp.logical_and(c == n - 1, d == 0))
    def _():
        for bb in range(G):
            for h in range(C_H):
                sf_ref[bb, h] = _t128(st_scr[bb, h])

    @pl.when(jnp.logical_and(c == n - 1, d == 1))
    def _():
        for bb in range(G):
            for h in range(C_H):
                sb_ref[bb, h] = _t128(st_scr[bb, h])


def _gla(z, zg, B, T, w_a2, b_a, s0f, s0b, norm_g):
    C = GLA_CHUNK
    G = _rows_per_step(B, T, C_H * C_DV)
    assert B % G == 0 and T % C == 0
    n = T // C
    HK = C_H * C_DK
    HV = C_H * C_DV
    mat, code, ones = _gla_tables(C)
    nr = mat.shape[1] // C
    wg = jnp.zeros((2, LANES, HK), F32)
    for dr in range(2):
        wg = wg.at[dr, dr * C_RANK:(dr + 1) * C_RANK, :].set(w_a2[dr])
    z3 = z.reshape(B, T, z.shape[1])
    zg3 = zg.reshape(B, T, zg.shape[1])

    def chunk(d, c):
        return c + d * (n - 1 - 2 * c)

    st_spec = pl.BlockSpec((G, C_H, C_DK, C_DV), lambda b, d, c: (b, 0, 0, 0))
    st_shape = jax.ShapeDtypeStruct((B, C_H, C_DK, C_DV), F32)
    o, sf, sb = pl.pallas_call(
        functools.partial(_gla_kernel, n=n, C=C, G=G),
        grid=(B // G, 2, n),
        in_specs=[pl.BlockSpec((G, C, HK), lambda b, d, c: (b, chunk(d, c), 0)),
                  pl.BlockSpec((G, C, HK), lambda b, d, c: (b, chunk(d, c), 1)),
                  pl.BlockSpec((G, C, HV), lambda b, d, c: (b, chunk(d, c), 1)),
                  pl.BlockSpec((G, C, HV), lambda b, d, c: (b, chunk(d, c), 2)),
                  pl.BlockSpec((G, C, LANES), lambda b, d, c: (b, chunk(d, c), 0)),
                  pl.BlockSpec((None, LANES, HK), lambda b, d, c: (d, 0, 0)),
                  pl.BlockSpec((None, 1, HK), lambda b, d, c: (d, 0, 0)),
                  pl.BlockSpec((None, nr * C, C), lambda b, d, c: (d, 0, 0)),
                  pl.BlockSpec((None, C, C), lambda b, d, c: (d, 0, 0)),
                  pl.BlockSpec((SUB * LANES, C), lambda b, d, c: (0, 0)),
                  st_spec, st_spec,
                  pl.BlockSpec((1, HV), lambda b, d, c: (0, 0))],
        out_specs=[pl.BlockSpec((G, C, HV), lambda b, d, c: (b, (n - 1) - d * c, 0)),
                   st_spec, st_spec],
        out_shape=[jax.ShapeDtypeStruct((B, T, HV), BF16), st_shape, st_shape],
        scratch_shapes=[pltpu.VMEM((G, C_H, C_DV, C_DK), F32), pltpu.VMEM((G, n, C, HV), F32)],
        compiler_params=_params("arbitrary", "arbitrary", "arbitrary"),
        name="gla",
    )(z3, z3, z3, z3, zg3, wg, b_a.reshape(2, 1, HK), mat, code, ones, s0f, s0b, norm_g.reshape(1, HV))
    return o.reshape(B * T, HV), sf, sb


def _run_stream(x, B, T, mods, ctx, p):
    N, D = x.shape
    rpg = N // mods[0].shape[0]
    TM = min(2048, rpg)
    nb = (B_H + 2 * B_HKV) * B_HD

    w_in = p['even_w_in'][0]
    z, zb = _norm_mm(x, p['norm1_g'][0], mods[0], (0, 1), w_in, w_in, (nb, MIX_MAIN // nb), TM, rpg)
    if ctx is None:
        s0 = jnp.zeros((B, A_H, A_DK, A_DV), F32)
        a_f0, a_b0, cache_k, cache_v = s0, s0, None, None
    else:
        cache_k, cache_v, a_f0, a_b0 = ctx[0], ctx[1], ctx[2], ctx[3]
    o_a, a_sf, a_sb = _retention(z, B, T, p['a_log_gamma'][0], a_f0, a_b0, p['a_norm_g'][0])
    qpad, k_norm, k_rot, v_bf = _bprep(zb, T, p['b_q_g'][0], p['b_k_g'][0], rope=ctx is not None)
    o_b = _attention(qpad, k_rot, v_bf, B, T, cache_k, cache_v)
    x = _proj_res(x, mods[0], 2, [o_a, o_b], p['even_w_out'][0], rpg)
    x = _ffn(x, p['norm2_g'][0], mods[0], p['ff_w1'][0], p['ff_w3'][0], p['ff_w2'][0], rpg)

    w_in = p['odd_w_in'][0]
    w_gate = jnp.pad(w_in[:, MIX_MAIN:], ((0, 0), (0, LANES - 2 * C_RANK)))
    z1, z1g = _norm_mm(x, p['norm1_g'][1], mods[1], (0, 1), w_in, w_gate, (LANES, 0), TM, rpg)
    if ctx is None:
        s0 = jnp.zeros((B, C_H, C_DK, C_DV), F32)
        c_f0, c_b0 = s0, s0
    else:
        c_f0, c_b0 = ctx[4], ctx[5]
    o_c, c_sf, c_sb = _gla(z1, z1g, B, T, p['c_w_a2'][0], p['c_b_a'][0], c_f0, c_b0, p['c_norm_g'][0])
    x = _proj_res(x, mods[1], 2, [o_c], p['odd_w_out'][0], rpg)
    v_raw = zb[:, (B_H + B_HKV) * B_HD:]
    return x, (k_norm, v_raw, a_sf, a_sb, c_sf, c_sb)


def kernel(x_prompt, x_sample, c, cache_b_k, cache_b_v, state_a_fwd, state_a_bwd, state_c_fwd, state_c_bwd,
           c_ctx, w_mod, b_mod, norm1_g, norm2_g, final_g, even_w_in, even_w_out, a_log_gamma, a_norm_g,
           b_q_g, b_k_g, odd_w_in, c_w_a2, c_b_a, c_norm_g, odd_w_out, ff_w1, ff_w3, ff_w2,
           router_w, moe_w1, moe_w3, moe_w2):
    Bp, Tp, D = x_prompt.shape
    Bs, Ts, _ = x_sample.shape
    L = w_mod.shape[0]
    assert L == 2 and even_w_in.shape[0] == 1 and odd_w_in.shape[0] == 1
    p = dict(norm1_g=norm1_g, norm2_g=norm2_g, final_g=final_g, even_w_in=even_w_in, even_w_out=even_w_out,
             a_log_gamma=a_log_gamma, a_norm_g=a_norm_g, b_q_g=b_q_g, b_k_g=b_k_g, odd_w_in=odd_w_in,
             c_w_a2=c_w_a2, c_b_a=c_b_a, c_norm_g=c_norm_g, odd_w_out=odd_w_out, ff_w1=ff_w1, ff_w3=ff_w3,
             ff_w2=ff_w2, router_w=router_w, moe_w1=moe_w1, moe_w3=moe_w3, moe_w2=moe_w2)

    rows = 8
    conds = jnp.concatenate([c_ctx[None, :], c, jnp.zeros((rows - 1 - Bs, D), F32)], axis=0)
    mod = _modulation(conds, w_mod, b_mod).reshape(L, rows, 6, 1, D)
    mods_p = [mod[l, 0:1] for l in range(L)]
    mods_s = [mod[l, 1:1 + Bs] for l in range(L)]

    x_p, kept = _run_stream(x_prompt.reshape(Bp * Tp, D), Bp, Tp, mods_p, None, p)
    nk = B_HKV * B_HD
    ctx = (cache_b_k[:, 0].reshape(Bs, -1, nk), cache_b_v[:, 0].reshape(Bs, -1, nk),
           state_a_fwd[:, 0], state_a_bwd[:, 0], state_c_fwd[:, 0], state_c_bwd[:, 0])
    x_s, _ = _run_stream(x_sample.reshape(Bs * Ts, D), Bs, Ts, mods_s, ctx, p)
    y_p, y_s = _moe(x_p, x_s, norm2_g[1], mod[1, 0:1 + Bs], router_w[0], moe_w1[0], moe_w3[0], moe_w2[0],
                    final_g, Ts)

    k_norm, v_raw, a_sf, a_sb, c_sf, c_sb = kept
    return (y_p.reshape(Bp, Tp, D), y_s.reshape(Bs, Ts, D),
            k_norm.reshape(Bp, 1, Tp, B_HKV, B_HD), v_raw.reshape(Bp, 1, Tp, B_HKV, B_HD),
            a_sf[:, None], a_sb[:, None], c_sf[:, None], c_sb[:, None])
```

```python
import functools

import numpy as np
import jax
import jax.numpy as jnp
from jax import lax
from jax.experimental import pallas as pl
from jax.experimental.pallas import tpu as pltpu

F32 = jnp.float32
BF16 = jnp.bfloat16
EPS = 1e-6
LOG2E = 1.4426950408889634

VMEM_LIMIT_BYTES = 56 * 1024 * 1024

A_H, A_DK, A_DV = 4, 128, 256
B_H, B_HKV, B_HD = 8, 2, 64
C_H, C_DK, C_DV, C_RANK = 4, 128, 256, 16
C_TAU = 16.0
GRID_W = 64
ROPE_THETA = 10000.0
N_EXPERTS = 8
LANES = 128
SUB = 8
RET_CHUNK = 128
GLA_CHUNK = 128
Q_TILE = 128
SCAN_ROWS_MAX = 4
SCAN_FWD_BYTES = 32 * 1024 * 1024


def _rows_per_step(B, T, width):
    g = SCAN_ROWS_MAX
    while g > 1 and (B % g or g * T * width * 4 > SCAN_FWD_BYTES):
        g //= 2
    return g


def _params(*sem):
    return pltpu.CompilerParams(dimension_semantics=sem, vmem_limit_bytes=VMEM_LIMIT_BYTES)


def _dot(a, b):
    return jnp.dot(a, b, preferred_element_type=F32)


def _dot_nt(a, b):
    return lax.dot_general(a, b, (((1,), (1,)), ((), ())), preferred_element_type=F32)


def _split(x):
    hi = x.astype(BF16)
    return hi, (x - hi.astype(F32)).astype(BF16)


def _dot3(a, b):
    return _dot(a[0], b[0]) + (_dot(a[0], b[1]) + _dot(a[1], b[0]))


def _silu(x):
    return x * jax.nn.sigmoid(x)


def _norm_mod(x, g, sh, sc):
    r = lax.rsqrt(jnp.mean(x * x, axis=-1, keepdims=True) + EPS)
    return (x * r * g) * (1.0 + sc) + sh


def _mod_kernel(c_ref, w_ref, b_ref, o_ref):
    o_ref[...] = _dot3(_split(_silu(c_ref[...])), _split(w_ref[...])) + b_ref[...]


def _modulation(conds, w_mod, b_mod):
    L, D, D6 = w_mod.shape
    R = conds.shape[0]
    TN = 1024
    return pl.pallas_call(
        _mod_kernel,
        grid=(L, D6 // TN),
        in_specs=[pl.BlockSpec((R, D), lambda l, j: (0, 0)),
                  pl.BlockSpec((None, D, TN), lambda l, j: (l, 0, j)),
                  pl.BlockSpec((None, 1, TN), lambda l, j: (l, 0, j))],
        out_specs=pl.BlockSpec((None, R, TN), lambda l, j: (l, 0, j)),
        out_shape=jax.ShapeDtypeStruct((L, R, D6), F32),
        compiler_params=_params("arbitrary", "arbitrary"),
        name="modulation",
    )(conds, w_mod, b_mod.reshape(L, 1, D6))


def _mod_spec(part, D, TM, rows_per_group, axis):
    def idx(*g):
        return ((g[axis] * TM) // rows_per_group, part, 0, 0)
    return pl.BlockSpec((None, None, 1, D), idx)


MIX_MAIN = A_H * (2 * A_DK + 2 * A_DV)
MIX_TN = 768


def _norm_mm_kernel(x_ref, g_ref, sh_ref, sc_ref, w_ref, we_ref, o_ref, oe_ref, h_scr, *, nmain):
    j = pl.program_id(1)

    @pl.when(j == 0)
    def _():
        h_scr[...] = _norm_mod(x_ref[...], g_ref[...], sh_ref[...], sc_ref[...]).astype(BF16)

    @pl.when(j < nmain)
    def _():
        o_ref[...] = _dot(h_scr[...], w_ref[...].astype(BF16)).astype(o_ref.dtype)

    @pl.when(j == nmain)
    def _():
        oe_ref[...] = _dot(h_scr[...], we_ref[...].astype(BF16))


def _norm_mm(x, g, mod, parts, w, w_extra, extra_block, TM, rows_per_group):
    N, D = x.shape
    nmain = MIX_MAIN // MIX_TN
    WE = extra_block[0]
    return pl.pallas_call(
        functools.partial(_norm_mm_kernel, nmain=nmain),
        grid=(N // TM, nmain + 1),
        in_specs=[pl.BlockSpec((TM, D), lambda i, j: (i, 0)),
                  pl.BlockSpec((1, D), lambda i, j: (0, 0)),
                  _mod_spec(parts[0], D, TM, rows_per_group, 0),
                  _mod_spec(parts[1], D, TM, rows_per_group, 0),
                  pl.BlockSpec((D, MIX_TN), lambda i, j: (0, jnp.minimum(j, nmain - 1))),
                  pl.BlockSpec((D, WE), lambda i, j: (0, extra_block[1]))],
        out_specs=[pl.BlockSpec((TM, MIX_TN), lambda i, j: (i, jnp.minimum(j, nmain - 1))),
                   pl.BlockSpec((TM, WE), lambda i, j: (i, 0))],
        out_shape=[jax.ShapeDtypeStruct((N, MIX_MAIN), BF16), jax.ShapeDtypeStruct((N, WE), F32)],
        scratch_shapes=[pltpu.VMEM((TM, D), BF16)],
        compiler_params=_params("arbitrary", "arbitrary"),
        name="norm_mm",
    )(x, g.reshape(1, D), mod, mod, w, w_extra)


def _ret_kernel(lg_ref, q_ref, k_ref, v_ref, ag_ref, s0f_ref, s0b_ref, ng_ref,
                o_ref, sf_ref, sb_ref, s_scr, of_scr, *, n, C, G):
    d = pl.program_id(1)
    c = pl.program_id(2)

    @pl.when(jnp.logical_and(c == 0, d == 0))
    def _():
        s_scr[...] = s0f_ref[...]

    @pl.when(jnp.logical_and(c == 0, d == 1))
    def _():
        s_scr[...] = s0b_ref[...]

    df = d.astype(F32)
    sgn = 1.0 - 2.0 * df
    ii = lax.broadcasted_iota(jnp.int32, (C, C), 0).astype(F32)
    jj = lax.broadcasted_iota(jnp.int32, (C, C), 1).astype(F32)
    dd = (ii - jj) * sgn
    feeds = dd >= 0.0
    ddc = jnp.maximum(dd, 0.0)
    ri = lax.broadcasted_iota(jnp.int32, (C, 1), 0).astype(F32)
    pos_q = (ri + 1.0) + df * (C - 2.0 * ri - 1.0)
    pos_k = (C - 1.0 - ri) + df * (2.0 * ri - C + 1.0)
    chunk_len = jnp.full((1, A_DV), float(C), F32)

    outs = [[] for _ in range(G)]
    for h in range(A_H):
        lg = lg_ref[d, h]
        dmask = jnp.where(feeds, jnp.exp2(lg * ddc), 0.0)
        q_dec = jnp.exp2(lg * pos_q)
        k_dec = jnp.exp2(lg * pos_k)
        c_dec = jnp.exp2(lg * chunk_len)
        for bb in range(G):
            qh = q_ref[bb, :, h * A_DK:(h + 1) * A_DK].astype(F32) * (A_DK ** -0.5)
            kh = k_ref[bb, :, h * A_DK:(h + 1) * A_DK].astype(F32)
            vh = v_ref[bb, :, h * A_DV:(h + 1) * A_DV].astype(BF16)
            s = s_scr[bb, h]
            att = _dot_nt(qh.astype(BF16), kh.astype(BF16)) * dmask
            o = _dot(att.astype(BF16), vh) + _dot((qh * q_dec).astype(BF16), s.astype(BF16))
            kd = kh * k_dec
            s_scr[bb, h] = c_dec * s + _dot(kd.T.astype(BF16), vh)
            outs[bb].append(o)
    o_all = [jnp.concatenate(o, axis=-1) for o in outs]

    @pl.when(d == 0)
    def _():
        for bb in range(G):
            of_scr[bb, c] = o_all[bb]

    @pl.when(d == 1)
    def _():
        for bb in range(G):
            tot = o_all[bb] + of_scr[bb, n - 1 - c]
            res = []
            for h in range(A_H):
                sl = slice(h * A_DV, (h + 1) * A_DV)
                t = tot[:, sl]
                dev = t - jnp.mean(t, axis=-1, keepdims=True)
                y = dev * lax.rsqrt(jnp.mean(dev * dev, axis=-1, keepdims=True) + EPS) * ng_ref[:, sl]
                res.append(y * _silu(ag_ref[bb, :, sl].astype(F32)))
            o_ref[bb] = jnp.concatenate(res, axis=-1).astype(o_ref.dtype)

    @pl.when(jnp.logical_and(c == n - 1, d == 0))
    def _():
        sf_ref[...] = s_scr[...]

    @pl.when(jnp.logical_and(c == n - 1, d == 1))
    def _():
        sb_ref[...] = s_scr[...]


def _retention(z, B, T, log_gamma, s0f, s0b, norm_g):
    C = RET_CHUNK
    G = _rows_per_step(B, T, A_H * A_DV)
    assert B % G == 0 and T % C == 0
    n = T // C
    HK = A_H * A_DK
    HV = A_H * A_DV
    z3 = z.reshape(B, T, z.shape[1])

    def chunk(d, c):
        return c + d * (n - 1 - 2 * c)

    st_spec = pl.BlockSpec((G, A_H, A_DK, A_DV), lambda b, d, c: (b, 0, 0, 0))
    st_shape = jax.ShapeDtypeStruct((B, A_H, A_DK, A_DV), F32)
    o, sf, sb = pl.pallas_call(
        functools.partial(_ret_kernel, n=n, C=C, G=G),
        grid=(B // G, 2, n),
        in_specs=[pl.BlockSpec(memory_space=pltpu.SMEM),
                  pl.BlockSpec((G, C, HK), lambda b, d, c: (b, chunk(d, c), 0)),
                  pl.BlockSpec((G, C, HK), lambda b, d, c: (b, chunk(d, c), 1)),
                  pl.BlockSpec((G, C, HV), lambda b, d, c: (b, chunk(d, c), 1)),
                  pl.BlockSpec((G, C, HV), lambda b, d, c: (b, chunk(d, c), 2)),
                  st_spec, st_spec,
                  pl.BlockSpec((1, HV), lambda b, d, c: (0, 0))],
        out_specs=[pl.BlockSpec((G, C, HV), lambda b, d, c: (b, (n - 1) - d * c, 0)),
                   st_spec, st_spec],
        out_shape=[jax.ShapeDtypeStruct((B, T, HV), BF16), st_shape, st_shape],
        scratch_shapes=[pltpu.VMEM((G, A_H, A_DK, A_DV), F32), pltpu.VMEM((G, n, C, HV), F32)],
        compiler_params=_params("arbitrary", "arbitrary", "arbitrary"),
        name="retention",
    )(log_gamma * LOG2E, z3, z3, z3, z3, s0f, s0b, norm_g.reshape(1, HV))
    return o.reshape(B * T, HV), sf, sb


def _group_sum_matrix(width, group):
    i = np.arange(width)
    return jnp.asarray((i[:, None] // group == i[None, :] // group).astype(np.float32), dtype=BF16)


def _q_pad_matrix():
    m = np.zeros((B_H * B_HD, B_H * LANES), np.float32)
    g = B_H // B_HKV
    for h in range(B_H):
        for t in range(B_HD):
            m[h * B_HD + t, h * LANES + (h // g) * B_HD + t] = 1.0
    return jnp.asarray(m, dtype=BF16)


def _rope_tables(T):
    rows = T // GRID_W
    row = np.repeat(np.arange(rows, dtype=np.float64), GRID_W)
    col = np.tile(np.arange(GRID_W, dtype=np.float64), rows)
    nq = B_HD // 4
    inv = ROPE_THETA ** (-np.arange(nq, dtype=np.float64) / nq)
    ang = np.concatenate([row[:, None] * inv, col[:, None] * inv], axis=-1)
    cos = np.repeat(np.cos(ang), 2, axis=-1)
    sin = np.repeat(np.sin(ang), 2, axis=-1)
    sign = np.tile(np.array([-1.0, 1.0]), B_HD // 2)
    reps = LANES // B_HD
    return (jnp.asarray(np.tile(cos, (1, reps)), dtype=F32),
            jnp.asarray(np.tile(sin * sign, (1, reps)), dtype=F32))


def _group_rmsnorm(x, gsum, g):
    hi, lo = _split(x * x)
    ss = _dot(hi, gsum) + _dot(lo, gsum)
    return x * lax.rsqrt(ss * (1.0 / B_HD) + EPS) * g


def _rotate_pairs(x, cos, sin_signed):
    n = x.shape[1]
    lane = lax.broadcasted_iota(jnp.int32, x.shape, 1)
    partner = jnp.where(lane % 2 == 0, pltpu.roll(x, n - 1, 1), pltpu.roll(x, 1, 1))
    reps = n // LANES
    if reps > 1:
        cos = jnp.concatenate([cos] * reps, axis=1)
        sin_signed = jnp.concatenate([sin_signed] * reps, axis=1)
    return x * cos + partner * sin_signed


def _bprep_kernel(z_ref, qg_ref, kg_ref, cos_ref, sin_ref, gq_ref, gk_ref, pad_ref,
                  qpad_ref, kn_ref, kr_ref, vb_ref, *, rope):
    nq = B_H * B_HD
    nk = B_HKV * B_HD
    qn = _group_rmsnorm(z_ref[:, 0:nq], gq_ref[...], qg_ref[...])
    kn = _group_rmsnorm(z_ref[:, nq:nq + nk], gk_ref[...], kg_ref[...])
    kn_ref[...] = kn
    if rope:
        qn = _rotate_pairs(qn, cos_ref[...], sin_ref[...])
        kn = _rotate_pairs(kn, cos_ref[...], sin_ref[...])
    kr_ref[...] = kn.astype(BF16)
    vb_ref[...] = z_ref[:, nq + nk:nq + 2 * nk].astype(BF16)
    qs = (qn * (B_HD ** -0.5 * LOG2E)).astype(BF16)
    qpad_ref[...] = _dot(qs, pad_ref[...]).astype(BF16)


def _bprep(z, T, q_g, k_g, rope):
    N = z.shape[0]
    TM = min(512, T)
    nq = B_H * B_HD
    nk = B_HKV * B_HD
    width = nq + 2 * nk
    assert z.shape[1] == width
    cos, sin = _rope_tables(T if rope else TM)
    nt = T // TM if rope else 1
    const = lambda i: (0, 0)
    return pl.pallas_call(
        functools.partial(_bprep_kernel, rope=rope),
        grid=(N // TM,),
        in_specs=[pl.BlockSpec((TM, width), lambda i: (i, 0)),
                  pl.BlockSpec((1, nq), const),
                  pl.BlockSpec((1, nk), const),
                  pl.BlockSpec((TM, LANES), lambda i: (i % nt, 0)),
                  pl.BlockSpec((TM, LANES), lambda i: (i % nt, 0)),
                  pl.BlockSpec((nq, nq), const),
                  pl.BlockSpec((nk, nk), const),
                  pl.BlockSpec((nq, B_H * LANES), const)],
        out_specs=[pl.BlockSpec((TM, B_H * LANES), lambda i: (i, 0)),
                   pl.BlockSpec((TM, nk), lambda i: (i, 0)),
                   pl.BlockSpec((TM, nk), lambda i: (i, 0)),
                   pl.BlockSpec((TM, nk), lambda i: (i, 0))],
        out_shape=[jax.ShapeDtypeStruct((N, B_H * LANES), BF16),
                   jax.ShapeDtypeStruct((N, nk), F32),
                   jax.ShapeDtypeStruct((N, nk), BF16),
                   jax.ShapeDtypeStruct((N, nk), BF16)],
        compiler_params=_params("arbitrary"),
        name="attn_prep",
    )(z, jnp.tile(q_g, B_H).reshape(1, nq), jnp.tile(k_g, B_HKV).reshape(1, nk), cos, sin,
      _group_sum_matrix(nq, B_HD), _group_sum_matrix(nk, B_HD), _q_pad_matrix())


def _lane_fold(x, op):
    acc = x[:, 0:LANES]
    for j in range(1, x.shape[1] // LANES):
        acc = op(acc, x[:, j * LANES:(j + 1) * LANES])
    return acc


def _attn_kernel(*refs, has_cache, kc, nq):
    if has_cache:
        q_ref, k_ref, v_ref, ck_ref, cv_ref, o_ref, s_scr, m_scr, mprev_scr, l_scr, acc_scr = refs
        kcc = min(kc, ck_ref.shape[0])
        ncache = ck_ref.shape[0] // kcc
    else:
        q_ref, k_ref, v_ref, o_ref, s_scr, m_scr, mprev_scr, l_scr, acc_scr = refs
        kcc, ncache = kc, 0
    i = pl.program_id(1)
    tq = q_ref.shape[0]
    nlat = k_ref.shape[0] // kc

    def score(c, kblk):
        q = jnp.concatenate([q_ref[:, h * LANES:(h + 1) * LANES] for h in range(B_H)], axis=0)
        s = _dot_nt(q, kblk)
        s_scr[c, :, 0:kblk.shape[0]] = s
        m_scr[...] = jnp.maximum(m_scr[...], _lane_fold(s, jnp.maximum))

    def weight(c, vblk):
        s = s_scr[c, :, 0:vblk.shape[0]]
        mp = mprev_scr[...]
        ps = [jnp.exp2(s[:, j * LANES:(j + 1) * LANES] - mp) for j in range(vblk.shape[0] // LANES)]
        tot = ps[0]
        for pj in ps[1:]:
            tot = tot + pj
        l_scr[...] += tot
        acc_scr[...] += _dot(jnp.concatenate(ps, axis=1).astype(BF16), vblk)

    def run(do_weight, do_score):
        def unit(c, kblk, vblk):
            if do_weight:
                weight(c, vblk())
            if do_score:
                score(c, kblk())

        for c in range(ncache):
            unit(c, lambda: ck_ref[c * kcc:(c + 1) * kcc, :].astype(BF16),
                 lambda: cv_ref[c * kcc:(c + 1) * kcc, :].astype(BF16))

        def body(c, carry):
            rows = pl.ds(pl.multiple_of(c * kc, kc), kc)
            unit(ncache + c, lambda: k_ref[rows, :], lambda: v_ref[rows, :])
            return carry
        lax.fori_loop(0, nlat, body, 0)

    @pl.when(i < nq)
    def _():
        m_scr[...] = jnp.full(m_scr.shape, -jnp.inf, F32)

    @pl.when(i > 0)
    def _():
        l_scr[...] = jnp.zeros_like(l_scr)
        acc_scr[...] = jnp.zeros_like(acc_scr)

    @pl.when(i == 0)
    def _():
        run(False, True)

    @pl.when(jnp.logical_and(i > 0, i < nq))
    def _():
        run(True, True)

    @pl.when(i == nq)
    def _():
        run(True, False)

    @pl.when(i > 0)
    def _():
        r_all = acc_scr[...] / jnp.sum(l_scr[...], axis=-1, keepdims=True)
        g = B_H // B_HKV
        lane = lax.broadcasted_iota(jnp.int32, (tq, LANES), 1)
        outs = []
        for j in range(B_H // 2):
            pair = []
            for half in range(2):
                h = 2 * j + half
                r = r_all[h * tq:(h + 1) * tq, :]
                if h // g != half:
                    r = pltpu.roll(r, B_HD, 1)
                pair.append(r)
            outs.append(jnp.where(lane < B_HD, pair[0], pair[1]))
        o_ref[...] = jnp.concatenate(outs, axis=-1).astype(o_ref.dtype)

    @pl.when(i < nq)
    def _():
        mprev_scr[...] = jnp.broadcast_to(jnp.max(m_scr[...], axis=-1, keepdims=True), mprev_scr.shape)


def _attention(qpad, kr, vb, B, T, cache_k, cache_v):
    has_cache = cache_k is not None
    TQ = Q_TILE
    nq = T // TQ
    nk = B_HKV * B_HD
    in_specs = [pl.BlockSpec((TQ, B_H * LANES), lambda b, i: (b * nq + jnp.minimum(i, nq - 1), 0)),
                pl.BlockSpec((T, nk), lambda b, i: (b, 0)),
                pl.BlockSpec((T, nk), lambda b, i: (b, 0))]
    args = [qpad, kr, vb]
    kc = min(1024, T)
    nchunks = T // kc
    if has_cache:
        P = cache_k.shape[1]
        assert P % min(kc, P) == 0
        nchunks += P // min(kc, P)
        in_specs += [pl.BlockSpec((None, P, nk), lambda b, i: (b, 0, 0))] * 2
        args += [cache_k, cache_v]
    R = B_H * TQ
    return pl.pallas_call(
        functools.partial(_attn_kernel, has_cache=has_cache, kc=kc, nq=nq),
        grid=(B, nq + 1),
        in_specs=in_specs,
        out_specs=pl.BlockSpec((TQ, B_H * B_HD), lambda b, i: (b * nq + jnp.maximum(i - 1, 0), 0)),
        out_shape=jax.ShapeDtypeStruct((B * T, B_H * B_HD), BF16),
        scratch_shapes=[pltpu.VMEM((nchunks, R, kc), F32)] + [pltpu.VMEM((R, LANES), F32)] * 4,
        compiler_params=_params("arbitrary", "arbitrary"),
        name="attention",
    )(*args)


def _proj_res_kernel(*refs, n_in):
    x_ref, gate_ref = refs[0], refs[1]
    o_refs = refs[2:2 + n_in]
    w_refs = refs[2 + n_in:2 + 2 * n_in]
    out_ref = refs[2 + 2 * n_in]
    wbf_refs = refs[3 + 2 * n_in:]

    @pl.when(pl.program_id(0) == 0)
    def _():
        for w_ref, wbf_ref in zip(w_refs, wbf_refs):
            wbf_ref[...] = w_ref[...].astype(BF16)

    acc = _dot(o_refs[0][...], wbf_refs[0][...])
    for o_ref, wbf_ref in zip(o_refs[1:], wbf_refs[1:]):
        acc = acc + _dot(o_ref[...], wbf_ref[...])
    out_ref[...] = x_ref[...] + gate_ref[...] * acc


def _proj_res(x, mod, part, acts, w, rows_per_group):
    N, D = x.shape
    TM = min(1024, rows_per_group)
    n_in = len(acts)
    widths = [a.shape[1] for a in acts]
    offs = np.cumsum([0] + widths[:-1]).tolist()
    in_specs = [pl.BlockSpec((TM, D), lambda i: (i, 0)),
                _mod_spec(part, D, TM, rows_per_group, 0)]
    in_specs += [pl.BlockSpec((TM, wd), lambda i: (i, 0)) for wd in widths]
    in_specs += [pl.BlockSpec((wd, D), functools.partial(lambda i, blk: (blk, 0), blk=off // wd))
                 for wd, off in zip(widths, offs)]
    return pl.pallas_call(
        functools.partial(_proj_res_kernel, n_in=n_in),
        grid=(N // TM,),
        in_specs=in_specs,
        out_specs=pl.BlockSpec((TM, D), lambda i: (i, 0)),
        out_shape=jax.ShapeDtypeStruct((N, D), F32),
        scratch_shapes=[pltpu.VMEM((wd, D), BF16) for wd in widths],
        compiler_params=_params("arbitrary"),
        name="proj_residual",
    )(x, mod, *acts, *([w] * n_in))


def _ffn_kernel(x_ref, g_ref, sh_ref, sc_ref, gate_ref, w1_ref, w3_ref, w2_ref, out_ref, h_scr, acc_scr, *, nf):
    f = pl.program_id(1)

    @pl.when(f == 0)
    def _():
        h_scr[...] = _norm_mod(x_ref[...], g_ref[...], sh_ref[...], sc_ref[...]).astype(BF16)
        acc_scr[...] = jnp.zeros_like(acc_scr)

    h = h_scr[...]
    a = _dot(h, w1_ref[...].astype(BF16))
    b = _dot(h, w3_ref[...].astype(BF16))
    acc_scr[...] += _dot((_silu(a) * b).astype(BF16), w2_ref[...].astype(BF16))

    @pl.when(f == nf - 1)
    def _():
        out_ref[...] = x_ref[...] + gate_ref[...] * acc_scr[...]


def _ffn(x, g, mod, w1, w3, w2, rows_per_group):
    N, D = x.shape
    FF = w1.shape[1]
    TM, TF = min(1024, rows_per_group), 256
    nf = FF // TF
    return pl.pallas_call(
        functools.partial(_ffn_kernel, nf=nf),
        grid=(N // TM, nf),
        in_specs=[pl.BlockSpec((TM, D), lambda i, f: (i, 0)),
                  pl.BlockSpec((1, D), lambda i, f: (0, 0)),
                  _mod_spec(3, D, TM, rows_per_group, 0),
                  _mod_spec(4, D, TM, rows_per_group, 0),
                  _mod_spec(5, D, TM, rows_per_group, 0),
                  pl.BlockSpec((D, TF), lambda i, f: (0, f)),
                  pl.BlockSpec((D, TF), lambda i, f: (0, f)),
                  pl.BlockSpec((TF, D), lambda i, f: (f, 0))],
        out_specs=pl.BlockSpec((TM, D), lambda i, f: (i, 0)),
        out_shape=jax.ShapeDtypeStruct((N, D), F32),
        scratch_shapes=[pltpu.VMEM((TM, D), BF16), pltpu.VMEM((TM, D), F32)],
        compiler_params=_params("arbitrary", "arbitrary"),
        name="ffn",
    )(x, g.reshape(1, D), mod, mod, mod, w1, w3, w2)


MOE_SB = 1024
MOE_SBG = 512
MOE_GG = 4
MOE_TRG = 256
MOE_TR = 2048
MOE_CG = 4


def _two_stream_specs(shape, ntp, ax=0):
    def idx_p(*g):
        return (jnp.minimum(g[ax], ntp - 1), 0)

    def idx_s(*g):
        return (jnp.maximum(g[ax] - ntp, 0), 0)
    return pl.BlockSpec(shape, idx_p), pl.BlockSpec(shape, idx_s)


def _pool_mod_spec(part, D, TM, ntp, rows_per_group):
    def idx(i, *_):
        return (jnp.where(i < ntp, 0, 1 + ((i - ntp) * TM) // rows_per_group), part, 0, 0)
    return pl.BlockSpec((None, None, 1, D), idx)


def _route_kernel(xp_ref, xs_ref, g_ref, sh_ref, sc_ref, rw_ref, tri_ref, h_ref, info_ref, infot_ref, cum_ref,
                  carry_scr, *, ntp, cap):
    i = pl.program_id(0)

    @pl.when(i == 0)
    def _():
        carry_scr[...] = jnp.zeros_like(carry_scr)

    x = jnp.where(i < ntp, xp_ref[...], xs_ref[...])
    h = _norm_mod(x, g_ref[...], sh_ref[...], sc_ref[...])
    h_ref[...] = h.astype(BF16)
    lane = lax.broadcasted_iota(jnp.int32, (x.shape[0], LANES), 1).astype(F32)
    logits = _dot3(_split(h), _split(rw_ref[...]))
    logits = jnp.where(lane < N_EXPERTS, logits, -jnp.inf)
    m1 = jnp.max(logits, axis=-1, keepdims=True)
    i1 = jnp.min(jnp.where(logits == m1, lane, float(LANES)), axis=-1, keepdims=True)
    rest = jnp.where(lane == i1, -jnp.inf, logits)
    m2 = jnp.max(rest, axis=-1, keepdims=True)
    i2 = jnp.min(jnp.where(rest == m2, lane, float(LANES)), axis=-1, keepdims=True)
    e2 = jnp.exp(m2 - m1)
    w1 = 1.0 / (1.0 + e2)
    w2 = e2 / (1.0 + e2)
    ind = jnp.where(jnp.logical_or(lane == i1, lane == i2), 1.0, 0.0)
    before = _dot(tri_ref[...], ind.astype(BF16)) + carry_scr[...]
    r1 = jnp.sum(jnp.where(lane == i1, before, 0.0), axis=-1, keepdims=True) + i1 * float(cap)
    r2 = jnp.sum(jnp.where(lane == i2, before, 0.0), axis=-1, keepdims=True) + i2 * float(cap)
    total = carry_scr[...] + jnp.sum(ind, axis=0, keepdims=True)
    carry_scr[...] = total
    for part in range(1, MOE_SB // MOE_SBG):
        cum_ref[part - 1] = before[part * MOE_SBG:part * MOE_SBG + 1, :]
    cum_ref[MOE_SB // MOE_SBG - 1] = total
    info = jnp.where(lane == 0.0, i1, jnp.where(lane == 1.0, i2, jnp.where(lane == 2.0, w1, jnp.where(
        lane == 3.0, w2, jnp.where(lane == 4.0, r1, jnp.where(lane == 5.0, r2, 0.0))))))
    info_ref[...] = info[:, 0:SUB]
    info_t = jnp.concatenate([info[r:r + LANES, :].T for r in range(0, info.shape[0], LANES)], axis=1)
    infot_ref[...] = info_t[0:SUB, :]


def _moe_route(xp, xs, g, mod, router_w, rows_per_group):
    Np, D = xp.shape
    N = Np + xs.shape[0]
    TM = MOE_SB
    ntp = Np // TM
    nt = N // TM
    rw = jnp.pad(router_w, ((0, 0), (0, LANES - router_w.shape[1])))
    tri = jnp.asarray(np.tril(np.ones((TM, TM), np.float32), -1), dtype=BF16)
    xp_spec, xs_spec = _two_stream_specs((TM, D), ntp)
    return pl.pallas_call(
        functools.partial(_route_kernel, ntp=ntp, cap=_moe_cap(N)),
        grid=(nt,),
        in_specs=[xp_spec, xs_spec,
                  pl.BlockSpec((1, D), lambda i: (0, 0)),
                  _pool_mod_spec(3, D, TM, ntp, rows_per_group),
                  _pool_mod_spec(4, D, TM, ntp, rows_per_group),
                  pl.BlockSpec((D, LANES), lambda i: (0, 0)),
                  pl.BlockSpec((TM, TM), lambda i: (0, 0))],
        out_specs=[pl.BlockSpec((TM, D), lambda i: (i, 0)),
                   pl.BlockSpec((TM, SUB), lambda i: (i, 0)),
                   pl.BlockSpec((SUB, TM), lambda i: (0, i)),
                   pl.BlockSpec((MOE_SB // MOE_SBG, 1, LANES), lambda i: (i, 0, 0))],
        out_shape=[jax.ShapeDtypeStruct((N, D), BF16),
                   jax.ShapeDtypeStruct((N, SUB), F32),
                   jax.ShapeDtypeStruct((SUB, N), F32),
                   jax.ShapeDtypeStruct((nt * (MOE_SB // MOE_SBG), 1, LANES), F32)],
        scratch_shapes=[pltpu.VMEM((1, LANES), F32)],
        compiler_params=_params("arbitrary"),
        name="moe_route",
    )(xp, xs, g.reshape(1, D), mod, mod, rw, tri)


def _hold_unused(idx, used):
    steps, slots = idx.shape
    read = jnp.arange(slots, dtype=jnp.int32)[None, :] < used[:, None]
    step = jnp.arange(steps, dtype=jnp.int32)[:, None]
    last = lax.cummax(jnp.where(read, step, -1), axis=0)
    held = jnp.take_along_axis(idx, jnp.maximum(last, 0), axis=0)
    return jnp.where(last >= 0, held, 0)


def _moe_cap(N):
    return -(-N // MOE_TR) * MOE_TR


def _moe_plan(cum, N):
    E, SB, TRG, TR = N_EXPERTS, MOE_SB, MOE_TRG, MOE_TR
    NB = N // SB
    cap = _moe_cap(N)
    tpe_g, tpe = cap // TRG, cap // TR
    RG = E * tpe_g
    used_g, used_t = 2 * N // TRG + E, 2 * N // TR + E
    PMAX = used_g + E * NB
    i32 = jnp.int32
    parts = SB // MOE_SBG
    cum_g = cum[:, 0, :E].astype(i32).T
    cum_e = cum_g[:, parts - 1::parts]
    cnt = cum_e[:, -1]

    lo = jnp.asarray((np.arange(RG) % tpe_g) * TRG, dtype=i32)
    hi = jnp.minimum(lo + TRG, jnp.repeat(cnt, tpe_g))

    GG = MOE_GG
    cum_g_t = jnp.repeat(cum_g, tpe_g, axis=0)
    first_g = jnp.sum(cum_g_t <= lo[:, None], axis=1)
    last_g = jnp.sum(cum_g_t < hi[:, None], axis=1)
    nblk_g = jnp.where(hi > lo, last_g - first_g + 1, 0)
    nst = (nblk_g + GG - 1) // GG
    gst_end = jnp.cumsum(nst)
    gtotal = gst_end[-1]
    smax_g = (used_g + E * NB * parts + (GG - 1) * used_g) // GG + 1
    jg = jnp.arange(smax_g, dtype=i32)
    g_ok = jg < gtotal
    jgc = jnp.minimum(jg, gtotal - 1)
    tq = jnp.minimum(jnp.sum(gst_end[None, :] <= jgc[:, None], axis=1), RG - 1).astype(i32)
    gg = jgc - (gst_end - nst)[tq]
    g_slots = jnp.where(g_ok, jnp.clip(nblk_g[tq] - GG * gg, 0, GG), 0)
    g_parts = _hold_unused((first_g[tq] + GG * gg)[:, None] + jnp.arange(GG, dtype=i32)[None, :], g_slots)
    g_first = jnp.logical_and(g_ok, gg == 0)
    gather_plan = (tq, g_parts.reshape(-1).astype(i32), g_slots.astype(i32), g_first.astype(i32))

    CG = MOE_CG
    c_lo = jnp.concatenate([jnp.zeros((E, 1), i32), cum_e[:, :-1]], axis=1)
    t_lo = c_lo // TRG
    n_se = jnp.where(cum_e > c_lo, (cum_e - 1) // TRG - t_lo + 1, 0)
    cn = jnp.cumsum(n_se, axis=0)
    per_blk = cn[-1]
    nsteps = (per_blk + CG - 1) // CG
    st_end = jnp.cumsum(nsteps)
    total = st_end[-1]
    SMAX = (PMAX + CG - 1) // CG + NB
    j = jnp.arange(SMAX, dtype=i32)
    step_ok = j < total
    jc = jnp.minimum(j, total - 1)
    blk = jnp.minimum(jnp.sum(st_end[None, :] <= jc[:, None], axis=1), NB - 1).astype(i32)
    grp = jc - (st_end - nsteps)[blk]
    m = CG * grp[:, None] + jnp.arange(CG, dtype=i32)[None, :]
    slot_ok = jnp.logical_and(m < per_blk[blk][:, None], step_ok[:, None])
    c_slots = jnp.sum(slot_ok, axis=1).astype(i32)
    cn_b = cn[:, blk].T
    e_of = jnp.minimum(jnp.sum(cn_b[:, None, :] <= m[:, :, None], axis=2), E - 1)
    before = jnp.take_along_axis(cn_b - n_se[:, blk].T, e_of, axis=1)
    tile = e_of * tpe_g + jnp.take_along_axis(t_lo[:, blk].T, e_of, axis=1) + m - before
    slot_q = _hold_unused(tile, c_slots)
    c_first = jnp.logical_and(step_ok, grp == 0).astype(i32)
    c_last = jnp.logical_and(step_ok, grp == nsteps[blk] - 1).astype(i32)
    combine_plan = (blk, slot_q.reshape(-1).astype(i32), c_slots, c_first, c_last)

    tiles = (cnt + TR - 1) // TR
    t_end = jnp.cumsum(tiles)
    jt = jnp.arange(used_t, dtype=i32)
    t_ok = jt < t_end[-1]
    jtc = jnp.minimum(jt, t_end[-1] - 1)
    te = jnp.minimum(jnp.sum(t_end[None, :] <= jtc[:, None], axis=1), E - 1).astype(i32)
    tk = jtc - (t_end - tiles)[te]
    tvalid = jnp.where(t_ok, jnp.clip(cnt[te] - tk * TR, 0, TR), 0)
    ffn_plan = ((te * tpe + tk).astype(i32), te, tvalid.astype(i32))
    return gather_plan, combine_plan, ffn_plan, E * cap


def _moe_gather_kernel(q_ref, s_ref, slots_ref, first_ref, *refs):
    pos_refs, h_refs, out_ref = refs[:MOE_GG], refs[MOE_GG:2 * MOE_GG], refs[2 * MOE_GG]
    p = pl.program_id(0)
    rows = out_ref.shape[0]

    @pl.when(first_ref[p] == 1)
    def _():
        out_ref[...] = jnp.zeros_like(out_ref)

    for ns in range(1, MOE_GG + 1):
        @pl.when(slots_ref[p] == ns)
        def _(ns=ns):
            row = (lax.broadcasted_iota(jnp.int32, (rows, 1), 0) + q_ref[p] * rows).astype(F32)
            sels = []
            for k in range(ns):
                hit = jnp.logical_or(pos_refs[k][4:5, :] == row, pos_refs[k][5:6, :] == row)
                sels.append(jnp.where(hit, 1.0, 0.0).astype(BF16))
            sel = sels[0] if ns == 1 else jnp.concatenate(sels, axis=1)
            hs = h_refs[0][...] if ns == 1 else jnp.concatenate([h_refs[k][...] for k in range(ns)], axis=0)
            out_ref[...] = out_ref[...] + _dot(sel, hs).astype(BF16)


def _moe_gather(h, pos_rows, plan, rmax):
    N, D = h.shape
    nsteps = plan[0].shape[0]

    def pos_spec(k):
        return pl.BlockSpec((SUB, MOE_SBG), lambda p, q, s, *_: (0, s[MOE_GG * p + k]))

    def tok_spec(k):
        return pl.BlockSpec((MOE_SBG, D), lambda p, q, s, *_: (s[MOE_GG * p + k], 0))

    return pl.pallas_call(
        _moe_gather_kernel,
        grid_spec=pltpu.PrefetchScalarGridSpec(
            num_scalar_prefetch=4, grid=(nsteps,),
            in_specs=[pos_spec(k) for k in range(MOE_GG)] + [tok_spec(k) for k in range(MOE_GG)],
            out_specs=pl.BlockSpec((MOE_TRG, D), lambda p, q, *_: (q[p], 0))),
        out_shape=jax.ShapeDtypeStruct((rmax, D), BF16),
        compiler_params=_params("arbitrary"),
        name="moe_gather",
    )(*plan, *([pos_rows] * MOE_GG), *([h] * MOE_GG))


def _moe_ffn_kernel(t_ref, e_ref, nv_ref, x_ref, w1_ref, w3_ref, w2_ref, out_ref, acc_scr, *, nf):
    t = pl.program_id(0)
    f = pl.program_id(1)
    nv = nv_ref[t]

    def block(start, size):
        rows = pl.ds(start, size)

        @pl.when(f == 0)
        def _():
            acc_scr[rows, :] = jnp.zeros((size, acc_scr.shape[1]), F32)

        x = x_ref[rows, :]
        a = _dot(x, w1_ref[...].astype(BF16))
        b = _dot(x, w3_ref[...].astype(BF16))
        acc_scr[rows, :] += _dot((_silu(a) * b).astype(BF16), w2_ref[...].astype(BF16))

        @pl.when(f == nf - 1)
        def _():
            out_ref[rows, :] = acc_scr[rows, :].astype(out_ref.dtype)

    nsub = MOE_TR // MOE_TRG
    used = (nv + MOE_TRG - 1) // MOE_TRG

    @pl.when(used == nsub)
    def _():
        block(0, MOE_TR)

    @pl.when(jnp.logical_and(used > 0, used < nsub))
    def _():
        start = jnp.int32(0)
        size = MOE_TR // 2
        while size >= MOE_TRG:
            has = (used & (size // MOE_TRG)) != 0

            @pl.when(has)
            def _(start=start, size=size):
                block(pl.multiple_of(start, MOE_TRG), size)

            start = start + jnp.where(has, size, 0)
            size //= 2


def _moe_ffn(xs, plan, w1, w3, w2):
    rmax, D = xs.shape
    FF = w1.shape[2]
    TF = 256
    nf = FF // TF
    RT = plan[0].shape[0]

    def fidx(t, f, nv):
        return jnp.where(nv[t] > 0, f, nf - 1)

    return pl.pallas_call(
        functools.partial(_moe_ffn_kernel, nf=nf),
        grid_spec=pltpu.PrefetchScalarGridSpec(
            num_scalar_prefetch=3, grid=(RT, nf),
            in_specs=[pl.BlockSpec((MOE_TR, D), lambda t, f, ti, e, nv: (ti[t], 0)),
                      pl.BlockSpec((None, D, TF), lambda t, f, ti, e, nv: (e[t], 0, fidx(t, f, nv))),
                      pl.BlockSpec((None, D, TF), lambda t, f, ti, e, nv: (e[t], 0, fidx(t, f, nv))),
                      pl.BlockSpec((None, TF, D), lambda t, f, ti, e, nv: (e[t], fidx(t, f, nv), 0))],
            out_specs=pl.BlockSpec((MOE_TR, D), lambda t, f, ti, e, nv: (ti[t], 0)),
            scratch_shapes=[pltpu.VMEM((MOE_TR, D), F32)]),
        out_shape=jax.ShapeDtypeStruct((rmax, D), BF16),
        compiler_params=_params("arbitrary", "arbitrary"),
        name="moe_ffn",
    )(*plan, xs, w1, w3, w2)


def _moe_combine_kernel(s_ref, q_ref, slots_ref, first_ref, last_ref, pos_ref, *refs, ntp):
    ys_refs = refs[:MOE_CG]
    xp_ref, xs_ref, gate_ref, fg_ref, op_ref, os_ref, acc_scr = refs[MOE_CG:]
    p = pl.program_id(0)
    rows = ys_refs[0].shape[0]

    @pl.when(first_ref[p] == 1)
    def _():
        acc_scr[...] = jnp.zeros_like(acc_scr)

    for ns in range(1, MOE_CG + 1):
        @pl.when(slots_ref[p] == ns)
        def _(ns=ns):
            sels = []
            for k in range(ns):
                col = (lax.broadcasted_iota(jnp.int32, (1, rows), 1) + q_ref[MOE_CG * p + k] * rows).astype(F32)
                sels.append((jnp.where(pos_ref[:, 4:5] == col, pos_ref[:, 2:3], 0.0)
                             + jnp.where(pos_ref[:, 5:6] == col, pos_ref[:, 3:4], 0.0)).astype(BF16))
            sel = sels[0] if ns == 1 else jnp.concatenate(sels, axis=1)
            ys = ys_refs[0][...] if ns == 1 else jnp.concatenate([ys_refs[k][...] for k in range(ns)], axis=0)
            acc_scr[...] += _dot(sel, ys)

    @pl.when(last_ref[p] == 1)
    def _():
        s = s_ref[p]
        x = jnp.where(s < ntp, xp_ref[...], xs_ref[...])
        y = x + gate_ref[...] * acc_scr[...]
        out = y * lax.rsqrt(jnp.mean(y * y, axis=-1, keepdims=True) + EPS) * fg_ref[...]

        @pl.when(s < ntp)
        def _():
            op_ref[...] = out

        @pl.when(s >= ntp)
        def _():
            os_ref[...] = out


def _moe_combine(ys, pos_cols, plan, xp, xs, mod, final_g, rows_per_group):
    Np, D = xp.shape
    Ns = xs.shape[0]
    SB = MOE_SB
    ntp = Np // SB
    nsteps = plan[0].shape[0]

    def tile_spec(k):
        return pl.BlockSpec((MOE_TRG, D), lambda p, s, q, *_: (q[MOE_CG * p + k], 0))

    def tok_p(p, s, *_):
        return (jnp.minimum(s[p], ntp - 1), 0)

    def tok_s(p, s, *_):
        return (jnp.maximum(s[p] - ntp, 0), 0)

    def gate_idx(p, s, *_):
        return (jnp.where(s[p] < ntp, 0, 1 + ((s[p] - ntp) * SB) // rows_per_group), 5, 0, 0)

    return pl.pallas_call(
        functools.partial(_moe_combine_kernel, ntp=ntp),
        grid_spec=pltpu.PrefetchScalarGridSpec(
            num_scalar_prefetch=5, grid=(nsteps,),
            in_specs=[pl.BlockSpec((SB, SUB), lambda p, s, q, *_: (s[p], 0))]
            + [tile_spec(k) for k in range(MOE_CG)]
            + [pl.BlockSpec((SB, D), tok_p),
                      pl.BlockSpec((SB, D), tok_s),
                      pl.BlockSpec((None, None, 1, D), gate_idx),
                      pl.BlockSpec((1, D), lambda p, *_: (0, 0))],
            out_specs=[pl.BlockSpec((SB, D), tok_p), pl.BlockSpec((SB, D), tok_s)],
            scratch_shapes=[pltpu.VMEM((SB, D), F32)]),
        out_shape=[jax.ShapeDtypeStruct((Np, D), F32), jax.ShapeDtypeStruct((Ns, D), F32)],
        compiler_params=_params("arbitrary"),
        name="moe_combine",
    )(*plan, pos_cols, *([ys] * MOE_CG), xp, xs, mod, final_g.reshape(1, D))


def _moe(xp, xs, g, mod, router_w, w1, w3, w2, final_g, rows_per_group):
    N = xp.shape[0] + xs.shape[0]
    h, info, info_t, cum = _moe_route(xp, xs, g, mod, router_w, rows_per_group)
    gather_plan, combine_plan, ffn_plan, rmax = _moe_plan(cum, N)
    x_sorted = _moe_gather(h, info_t, gather_plan, rmax)
    y_sorted = _moe_ffn(x_sorted, ffn_plan, w1, w3, w2)
    return _moe_combine(y_sorted, info, combine_plan, xp, xs, mod, final_g, rows_per_group)


def _gla_levels(C):
    lv, c = [], C // 2
    while c >= SUB:
        lv.append(c)
        c //= 2
    return lv


def _gla_tables(C):
    levels = _gla_levels(C)
    nr = 1 + len(levels)
    mat = np.zeros((2, nr * C, C), np.float32)
    code = np.zeros((2, C, C), np.int32)
    for d in range(2):
        p = np.arange(C) if d == 0 else C - 1 - np.arange(C)
        pi, pj = p[:, None], p[None, :]
        mat[d, 0:C] = pj <= pi
        code[d] = np.where((pj <= pi) & (pi // SUB == pj // SUB), 1, 0)
        for lv, c in enumerate(levels):
            blk = pi // c
            later = blk % 2 == 1
            mat[d, (1 + lv) * C:(2 + lv) * C] = ((later & (pj > blk * c - 1) & (pj <= pi))
                                                 | (~later & (pj > pi) & (pj <= (blk + 1) * c - 1)))
            pair = (pi // (2 * c) == pj // (2 * c)) & (pi // c != pj // c) & (pj <= pi)
            code[d] = np.where(pair, 2 + lv, code[d])
    ones = np.zeros((SUB * LANES, C), np.float32)
    for jj in range(SUB):
        ones[jj * LANES:(jj + 1) * LANES, jj::SUB] = 1.0
    return jnp.asarray(mat, dtype=BF16), jnp.asarray(code), jnp.asarray(ones, dtype=BF16)


def _bcast_sublane(x, jj):
    r, w = x.shape
    x3 = x.reshape(r // SUB, SUB, w)
    return jnp.broadcast_to(x3[:, jj:jj + 1, :], x3.shape).reshape(r, w)


def _t128(x):
    r, w = x.shape
    if w > LANES:
        return jnp.concatenate([x[:, i:i + LANES].T for i in range(0, w, LANES)], axis=0)
    return jnp.concatenate([x[i:i + LANES, :].T for i in range(0, r, LANES)], axis=1)


def _gla_kernel(q_ref, k_ref, v_ref, g_ref, lr_ref, wg_ref, ba_ref, mat_ref, code_ref, ones_ref,
                s0f_ref, s0b_ref, ng_ref, o_ref, sf_ref, sb_ref, st_scr, of_scr, *, n, C, G):
    d = pl.program_id(1)
    c = pl.program_id(2)
    levels = _gla_levels(C)

    @pl.when(jnp.logical_and(c == 0, d == 0))
    def _():
        for bb in range(G):
            for h in range(C_H):
                st_scr[bb, h] = _t128(s0f_ref[bb, h])

    @pl.when(jnp.logical_and(c == 0, d == 1))
    def _():
        for bb in range(G):
            for h in range(C_H):
                st_scr[bb, h] = _t128(s0b_ref[bb, h])

    mat = mat_ref[...]
    code = code_ref[...]
    ones = ones_ref[...]
    wg = _split(wg_ref[...])
    cums = []
    for bb in range(G):
        xg = _dot3(_split(lr_ref[bb]), wg) + ba_ref[...]
        la = (jnp.minimum(xg, 0.0) - jnp.log1p(jnp.exp(-jnp.abs(xg)))) * (LOG2E / C_TAU)
        hi, lo = _split(la)
        cums.append(_dot(mat, hi) + _dot(mat, lo))

    def prepare(bb, h):
        cum = cums[bb]
        ks = slice(h * C_DK, (h + 1) * C_DK)
        qh = q_ref[bb, :, ks].astype(F32) * (C_DK ** -0.5)
        kh = k_ref[bb, :, ks].astype(F32)
        b = cum[0:C, ks]
        b_end = jnp.min(b, axis=0, keepdims=True)
        b_rest = b_end - b
        ps = []
        for jj in range(SUB):
            dec = jnp.exp2(jnp.minimum(b - _bcast_sublane(b, jj), 0.0))
            ps.append((qh * _bcast_sublane(kh, jj) * dec).astype(BF16))
        lv_ops = []
        for lv in range(len(levels)):
            fac = jnp.exp2(cum[(1 + lv) * C:(2 + lv) * C, ks])
            lv_ops.append(((qh * fac).astype(BF16), (kh * fac).astype(BF16)))
        qe = (qh * jnp.exp2(b)).astype(BF16)
        ke = (kh * jnp.exp2(b_rest)).astype(BF16)
        e_end = jnp.exp2(b_end)
        return jnp.concatenate(ps, axis=1), lv_ops, qe, ke, e_end

    def contract(bb, h, prep):
        pcat, lv_ops, qe, ke, e_end = prep
        vh = v_ref[bb, :, h * C_DV:(h + 1) * C_DV].astype(F32)
        att = jnp.where(code == 1, _dot(pcat, ones), 0.0)
        for lv, (qs, ks_) in enumerate(lv_ops):
            att = jnp.where(code == 2 + lv, _dot_nt(qs, ks_), att)
        st = st_scr[bb, h]
        o = _dot(att.astype(BF16), vh.astype(BF16)) + _dot_nt(qe, st.astype(BF16))
        st_scr[bb, h] = e_end * st + _dot(_t128(vh).astype(BF16), ke)
        return o

    units = [(bb, h) for h in range(C_H) for bb in range(G)]
    outs = {}
    prep = prepare(*units[0])
    for idx, (bb, h) in enumerate(units):
        nxt = prepare(*units[idx + 1]) if idx + 1 < len(units) else None
        outs[(bb, h)] = contract(bb, h, prep)
        prep = nxt
    o_all = [jnp.concatenate([outs[(bb, h)] for h in range(C_H)], axis=-1) for bb in range(G)]

    @pl.when(d == 0)
    def _():
        for bb in range(G):
            of_scr[bb, c] = o_all[bb]

    @pl.when(d == 1)
    def _():
        for bb in range(G):
            tot = o_all[bb] + of_scr[bb, n - 1 - c]
            res = []
            for h in range(C_H):
                sl = slice(h * C_DV, (h + 1) * C_DV)
                t = tot[:, sl]
                y = t * lax.rsqrt(jnp.mean(t * t, axis=-1, keepdims=True) + EPS) * ng_ref[:, sl]
                res.append(y * _silu(g_ref[bb, :, sl].astype(F32)))
            o_ref[bb] = jnp.concatenate(res, axis=-1).astype(o_ref.dtype)

    @pl.when(jnp.logical_and(c == n - 1, d == 0))
    def _():
        for bb in range(G):
            for h in range(C_H):
                sf_ref[bb, h] = _t128(st_scr[bb, h])

    @pl.when(jnp.logical_and(c == n - 1, d == 1))
    def _():
        for bb in range(G):
            for h in range(C_H):
                sb_ref[bb, h] = _t128(st_scr[bb, h])


def _gla(z, zg, B, T, w_a2, b_a, s0f, s0b, norm_g):
    C = GLA_CHUNK
    G = _rows_per_step(B, T, C_H * C_DV)
    assert B % G == 0 and T % C == 0
    n = T // C
    HK = C_H * C_DK
    HV = C_H * C_DV
    mat, code, ones = _gla_tables(C)
    nr = mat.shape[1] // C
    wg = jnp.zeros((2, LANES, HK), F32)
    for dr in range(2):
        wg = wg.at[dr, dr * C_RANK:(dr + 1) * C_RANK, :].set(w_a2[dr])
    z3 = z.reshape(B, T, z.shape[1])
    zg3 = zg.reshape(B, T, zg.shape[1])

    def chunk(d, c):
        return c + d * (n - 1 - 2 * c)

    st_spec = pl.BlockSpec((G, C_H, C_DK, C_DV), lambda b, d, c: (b, 0, 0, 0))
    st_shape = jax.ShapeDtypeStruct((B, C_H, C_DK, C_DV), F32)
    o, sf, sb = pl.pallas_call(
        functools.partial(_gla_kernel, n=n, C=C, G=G),
        grid=(B // G, 2, n),
        in_specs=[pl.BlockSpec((G, C, HK), lambda b, d, c: (b, chunk(d, c), 0)),
                  pl.BlockSpec((G, C, HK), lambda b, d, c: (b, chunk(d, c), 1)),
                  pl.BlockSpec((G, C, HV), lambda b, d, c: (b, chunk(d, c), 1)),
                  pl.BlockSpec((G, C, HV), lambda b, d, c: (b, chunk(d, c), 2)),
                  pl.BlockSpec((G, C, LANES), lambda b, d, c: (b, chunk(d, c), 0)),
                  pl.BlockSpec((None, LANES, HK), lambda b, d, c: (d, 0, 0)),
                  pl.BlockSpec((None, 1, HK), lambda b, d, c: (d, 0, 0)),
                  pl.BlockSpec((None, nr * C, C), lambda b, d, c: (d, 0, 0)),
                  pl.BlockSpec((None, C, C), lambda b, d, c: (d, 0, 0)),
                  pl.BlockSpec((SUB * LANES, C), lambda b, d, c: (0, 0)),
                  st_spec, st_spec,
                  pl.BlockSpec((1, HV), lambda b, d, c: (0, 0))],
        out_specs=[pl.BlockSpec((G, C, HV), lambda b, d, c: (b, (n - 1) - d * c, 0)),
                   st_spec, st_spec],
        out_shape=[jax.ShapeDtypeStruct((B, T, HV), BF16), st_shape, st_shape],
        scratch_shapes=[pltpu.VMEM((G, C_H, C_DV, C_DK), F32), pltpu.VMEM((G, n, C, HV), F32)],
        compiler_params=_params("arbitrary", "arbitrary", "arbitrary"),
        name="gla",
    )(z3, z3, z3, z3, zg3, wg, b_a.reshape(2, 1, HK), mat, code, ones, s0f, s0b, norm_g.reshape(1, HV))
    return o.reshape(B * T, HV), sf, sb


def _run_stream(x, B, T, mods, ctx, p):
    N, D = x.shape
    rpg = N // mods[0].shape[0]
    TM = min(2048, rpg)
    nb = (B_H + 2 * B_HKV) * B_HD

    w_in = p['even_w_in'][0]
    z, zb = _norm_mm(x, p['norm1_g'][0], mods[0], (0, 1), w_in, w_in, (nb, MIX_MAIN // nb), TM, rpg)
    if ctx is None:
        s0 = jnp.zeros((B, A_H, A_DK, A_DV), F32)
        a_f0, a_b0, cache_k, cache_v = s0, s0, None, None
    else:
        cache_k, cache_v, a_f0, a_b0 = ctx[0], ctx[1], ctx[2], ctx[3]
    o_a, a_sf, a_sb = _retention(z, B, T, p['a_log_gamma'][0], a_f0, a_b0, p['a_norm_g'][0])
    qpad, k_norm, k_rot, v_bf = _bprep(zb, T, p['b_q_g'][0], p['b_k_g'][0], rope=ctx is not None)
    o_b = _attention(qpad, k_rot, v_bf, B, T, cache_k, cache_v)
    x = _proj_res(x, mods[0], 2, [o_a, o_b], p['even_w_out'][0], rpg)
    x = _ffn(x, p['norm2_g'][0], mods[0], p['ff_w1'][0], p['ff_w3'][0], p['ff_w2'][0], rpg)

    w_in = p['odd_w_in'][0]
    w_gate = jnp.pad(w_in[:, MIX_MAIN:], ((0, 0), (0, LANES - 2 * C_RANK)))
    z1, z1g = _norm_mm(x, p['norm1_g'][1], mods[1], (0, 1), w_in, w_gate, (LANES, 0), TM, rpg)
    if ctx is None:
        s0 = jnp.zeros((B, C_H, C_DK, C_DV), F32)
        c_f0, c_b0 = s0, s0
    else:
        c_f0, c_b0 = ctx[4], ctx[5]
    o_c, c_sf, c_sb = _gla(z1, z1g, B, T, p['c_w_a2'][0], p['c_b_a'][0], c_f0, c_b0, p['c_norm_g'][0])
    x = _proj_res(x, mods[1], 2, [o_c], p['odd_w_out'][0], rpg)
    v_raw = zb[:, (B_H + B_HKV) * B_HD:]
    return x, (k_norm, v_raw, a_sf, a_sb, c_sf, c_sb)


def kernel(x_prompt, x_sample, c, cache_b_k, cache_b_v, state_a_fwd, state_a_bwd, state_c_fwd, state_c_bwd,
           c_ctx, w_mod, b_mod, norm1_g, norm2_g, final_g, even_w_in, even_w_out, a_log_gamma, a_norm_g,
           b_q_g, b_k_g, odd_w_in, c_w_a2, c_b_a, c_norm_g, odd_w_out, ff_w1, ff_w3, ff_w2,
           router_w, moe_w1, moe_w3, moe_w2):
    Bp, Tp, D = x_prompt.shape
    Bs, Ts, _ = x_sample.shape
    L = w_mod.shape[0]
    assert L == 2 and even_w_in.shape[0] == 1 and odd_w_in.shape[0] == 1
    p = dict(norm1_g=norm1_g, norm2_g=norm2_g, final_g=final_g, even_w_in=even_w_in, even_w_out=even_w_out,
             a_log_gamma=a_log_gamma, a_norm_g=a_norm_g, b_q_g=b_q_g, b_k_g=b_k_g, odd_w_in=odd_w_in,
             c_w_a2=c_w_a2, c_b_a=c_b_a, c_norm_g=c_norm_g, odd_w_out=odd_w_out, ff_w1=ff_w1, ff_w3=ff_w3,
             ff_w2=ff_w2, router_w=router_w, moe_w1=moe_w1, moe_w3=moe_w3, moe_w2=moe_w2)

    rows = 8
    conds = jnp.concatenate([c_ctx[None, :], c, jnp.zeros((rows - 1 - Bs, D), F32)], axis=0)
    mod = _modulation(conds, w_mod, b_mod).reshape(L, rows, 6, 1, D)
    mods_p = [mod[l, 0:1] for l in range(L)]
    mods_s = [mod[l, 1:1 + Bs] for l in range(L)]

    x_p, kept = _run_stream(x_prompt.reshape(Bp * Tp, D), Bp, Tp, mods_p, None, p)
    nk = B_HKV * B_HD
    ctx = (cache_b_k[:, 0].reshape(Bs, -1, nk), cache_b_v[:, 0].reshape(Bs, -1, nk),
           state_a_fwd[:, 0], state_a_bwd[:, 0], state_c_fwd[:, 0], state_c_bwd[:, 0])
    x_s, _ = _run_stream(x_sample.reshape(Bs * Ts, D), Bs, Ts, mods_s, ctx, p)
    y_p, y_s = _moe(x_p, x_s, norm2_g[1], mod[1, 0:1 + Bs], router_w[0], moe_w1[0], moe_w3[0], moe_w2[0],
                    final_g, Ts)

    k_norm, v_raw, a_sf, a_sb, c_sf, c_sb = kept
    return (y_p.reshape(Bp, Tp, D), y_s.reshape(Bs, Ts, D),
            k_norm.reshape(Bp, 1, Tp, B_HKV, B_HD), v_raw.reshape(Bp, 1, Tp, B_HKV, B_HD),
            a_sf[:, None], a_sb[:, None], c_sf[:, None], c_sb[:, None])
```

```python
import functools

import numpy as np
import jax
import jax.numpy as jnp
from jax import lax
from jax.experimental import pallas as pl
from jax.experimental.pallas import tpu as pltpu

F32 = jnp.float32
BF16 = jnp.bfloat16
EPS = 1e-6
LOG2E = 1.4426950408889634

VMEM_LIMIT_BYTES = 56 * 1024 * 1024

A_H, A_DK, A_DV = 4, 128, 256
B_H, B_HKV, B_HD = 8, 2, 64
C_H, C_DK, C_DV, C_RANK = 4, 128, 256, 16
C_TAU = 16.0
GRID_W = 64
ROPE_THETA = 10000.0
N_EXPERTS = 8
LANES = 128
SUB = 8
RET_CHUNK = 128
GLA_CHUNK = 128
Q_TILE = 128
SCAN_ROWS_MAX = 4
SCAN_FWD_BYTES = 32 * 1024 * 1024


def _rows_per_step(B, T, width):
    g = SCAN_ROWS_MAX
    while g > 1 and (B % g or g * T * width * 4 > SCAN_FWD_BYTES):
        g //= 2
    return g


def _params(*sem):
    return pltpu.CompilerParams(dimension_semantics=sem, vmem_limit_bytes=VMEM_LIMIT_BYTES)


def _dot(a, b):
    return jnp.dot(a, b, preferred_element_type=F32)


def _dot_nt(a, b):
    return lax.dot_general(a, b, (((1,), (1,)), ((), ())), preferred_element_type=F32)


def _split(x):
    hi = x.astype(BF16)
    return hi, (x - hi.astype(F32)).astype(BF16)


def _dot3(a, b):
    return _dot(a[0], b[0]) + (_dot(a[0], b[1]) + _dot(a[1], b[0]))


def _silu(x):
    return x * jax.nn.sigmoid(x)


def _norm_mod(x, g, sh, sc):
    r = lax.rsqrt(jnp.mean(x * x, axis=-1, keepdims=True) + EPS)
    return (x * r * g) * (1.0 + sc) + sh


def _mod_kernel(c_ref, w_ref, b_ref, o_ref):
    o_ref[...] = _dot3(_split(_silu(c_ref[...])), _split(w_ref[...])) + b_ref[...]


def _modulation(conds, w_mod, b_mod):
    L, D, D6 = w_mod.shape
    R = conds.shape[0]
    TN = 1024
    return pl.pallas_call(
        _mod_kernel,
        grid=(L, D6 // TN),
        in_specs=[pl.BlockSpec((R, D), lambda l, j: (0, 0)),
                  pl.BlockSpec((None, D, TN), lambda l, j: (l, 0, j)),
                  pl.BlockSpec((None, 1, TN), lambda l, j: (l, 0, j))],
        out_specs=pl.BlockSpec((None, R, TN), lambda l, j: (l, 0, j)),
        out_shape=jax.ShapeDtypeStruct((L, R, D6), F32),
        compiler_params=_params("arbitrary", "arbitrary"),
        name="modulation",
    )(conds, w_mod, b_mod.reshape(L, 1, D6))


def _mod_spec(part, D, TM, rows_per_group, axis):
    def idx(*g):
        return ((g[axis] * TM) // rows_per_group, part, 0, 0)
    return pl.BlockSpec((None, None, 1, D), idx)


MIX_MAIN = A_H * (2 * A_DK + 2 * A_DV)
MIX_TN = 768


def _norm_mm_kernel(x_ref, g_ref, sh_ref, sc_ref, w_ref, we_ref, o_ref, oe_ref, h_scr, *, nmain):
    j = pl.program_id(1)

    @pl.when(j == 0)
    def _():
        h_scr[...] = _norm_mod(x_ref[...], g_ref[...], sh_ref[...], sc_ref[...]).astype(BF16)

    @pl.when(j < nmain)
    def _():
        o_ref[...] = _dot(h_scr[...], w_ref[...].astype(BF16)).astype(o_ref.dtype)

    @pl.when(j == nmain)
    def _():
        oe_ref[...] = _dot(h_scr[...], we_ref[...].astype(BF16))


def _norm_mm(x, g, mod, parts, w, w_extra, extra_block, TM, rows_per_group):
    N, D = x.shape
    nmain = MIX_MAIN // MIX_TN
    WE = extra_block[0]
    return pl.pallas_call(
        functools.partial(_norm_mm_kernel, nmain=nmain),
        grid=(N // TM, nmain + 1),
        in_specs=[pl.BlockSpec((TM, D), lambda i, j: (i, 0)),
                  pl.BlockSpec((1, D), lambda i, j: (0, 0)),
                  _mod_spec(parts[0], D, TM, rows_per_group, 0),
                  _mod_spec(parts[1], D, TM, rows_per_group, 0),
                  pl.BlockSpec((D, MIX_TN), lambda i, j: (0, jnp.minimum(j, nmain - 1))),
                  pl.BlockSpec((D, WE), lambda i, j: (0, extra_block[1]))],
        out_specs=[pl.BlockSpec((TM, MIX_TN), lambda i, j: (i, jnp.minimum(j, nmain - 1))),
                   pl.BlockSpec((TM, WE), lambda i, j: (i, 0))],
        out_shape=[jax.ShapeDtypeStruct((N, MIX_MAIN), BF16), jax.ShapeDtypeStruct((N, WE), F32)],
        scratch_shapes=[pltpu.VMEM((TM, D), BF16)],
        compiler_params=_params("arbitrary", "arbitrary"),
        name="norm_mm",
    )(x, g.reshape(1, D), mod, mod, w, w_extra)


def _ret_kernel(lg_ref, q_ref, k_ref, v_ref, ag_ref, s0f_ref, s0b_ref, ng_ref,
                o_ref, sf_ref, sb_ref, s_scr, of_scr, *, n, C, G):
    d = pl.program_id(1)
    c = pl.program_id(2)

    @pl.when(jnp.logical_and(c == 0, d == 0))
    def _():
        s_scr[...] = s0f_ref[...]

    @pl.when(jnp.logical_and(c == 0, d == 1))
    def _():
        s_scr[...] = s0b_ref[...]

    df = d.astype(F32)
    sgn = 1.0 - 2.0 * df
    ii = lax.broadcasted_iota(jnp.int32, (C, C), 0).astype(F32)
    jj = lax.broadcasted_iota(jnp.int32, (C, C), 1).astype(F32)
    dd = (ii - jj) * sgn
    feeds = dd >= 0.0
    ddc = jnp.maximum(dd, 0.0)
    ri = lax.broadcasted_iota(jnp.int32, (C, 1), 0).astype(F32)
    pos_q = (ri + 1.0) + df * (C - 2.0 * ri - 1.0)
    pos_k = (C - 1.0 - ri) + df * (2.0 * ri - C + 1.0)
    chunk_len = jnp.full((1, A_DV), float(C), F32)

    outs = [[] for _ in range(G)]
    for h in range(A_H):
        lg = lg_ref[d, h]
        dmask = jnp.where(feeds, jnp.exp2(lg * ddc), 0.0)
        q_dec = jnp.exp2(lg * pos_q)
        k_dec = jnp.exp2(lg * pos_k)
        c_dec = jnp.exp2(lg * chunk_len)
        for bb in range(G):
            qh = q_ref[bb, :, h * A_DK:(h + 1) * A_DK].astype(F32) * (A_DK ** -0.5)
            kh = k_ref[bb, :, h * A_DK:(h + 1) * A_DK].astype(F32)
            vh = v_ref[bb, :, h * A_DV:(h + 1) * A_DV].astype(BF16)
            s = s_scr[bb, h]
            att = _dot_nt(qh.astype(BF16), kh.astype(BF16)) * dmask
            o = _dot(att.astype(BF16), vh) + _dot((qh * q_dec).astype(BF16), s.astype(BF16))
            kd = kh * k_dec
            s_scr[bb, h] = c_dec * s + _dot(kd.T.astype(BF16), vh)
            outs[bb].append(o)
    o_all = [jnp.concatenate(o, axis=-1) for o in outs]

    @pl.when(d == 0)
    def _():
        for bb in range(G):
            of_scr[bb, c] = o_all[bb]

    @pl.when(d == 1)
    def _():
        for bb in range(G):
            tot = o_all[bb] + of_scr[bb, n - 1 - c]
            res = []
            for h in range(A_H):
                sl = slice(h * A_DV, (h + 1) * A_DV)
                t = tot[:, sl]
                dev = t - jnp.mean(t, axis=-1, keepdims=True)
                y = dev * lax.rsqrt(jnp.mean(dev * dev, axis=-1, keepdims=True) + EPS) * ng_ref[:, sl]
                res.append(y * _silu(ag_ref[bb, :, sl].astype(F32)))
            o_ref[bb] = jnp.concatenate(res, axis=-1).astype(o_ref.dtype)

    @pl.when(jnp.logical_and(c == n - 1, d == 0))
    def _():
        sf_ref[...] = s_scr[...]

    @pl.when(jnp.logical_and(c == n - 1, d == 1))
    def _():
        sb_ref[...] = s_scr[...]


def _retention(z, B, T, log_gamma, s0f, s0b, norm_g):
    C = RET_CHUNK
    G = _rows_per_step(B, T, A_H * A_DV)
    assert B % G == 0 and T % C == 0
    n = T // C
    HK = A_H * A_DK
    HV = A_H * A_DV
    z3 = z.reshape(B, T, z.shape[1])

    def chunk(d, c):
        return c + d * (n - 1 - 2 * c)

    st_spec = pl.BlockSpec((G, A_H, A_DK, A_DV), lambda b, d, c: (b, 0, 0, 0))
    st_shape = jax.ShapeDtypeStruct((B, A_H, A_DK, A_DV), F32)
    o, sf, sb = pl.pallas_call(
        functools.partial(_ret_kernel, n=n, C=C, G=G),
        grid=(B // G, 2, n),
        in_specs=[pl.BlockSpec(memory_space=pltpu.SMEM),
                  pl.BlockSpec((G, C, HK), lambda b, d, c: (b, chunk(d, c), 0)),
                  pl.BlockSpec((G, C, HK), lambda b, d, c: (b, chunk(d, c), 1)),
                  pl.BlockSpec((G, C, HV), lambda b, d, c: (b, chunk(d, c), 1)),
                  pl.BlockSpec((G, C, HV), lambda b, d, c: (b, chunk(d, c), 2)),
                  st_spec, st_spec,
                  pl.BlockSpec((1, HV), lambda b, d, c: (0, 0))],
        out_specs=[pl.BlockSpec((G, C, HV), lambda b, d, c: (b, (n - 1) - d * c, 0)),
                   st_spec, st_spec],
        out_shape=[jax.ShapeDtypeStruct((B, T, HV), BF16), st_shape, st_shape],
        scratch_shapes=[pltpu.VMEM((G, A_H, A_DK, A_DV), F32), pltpu.VMEM((G, n, C, HV), F32)],
        compiler_params=_params("arbitrary", "arbitrary", "arbitrary"),
        name="retention",
    )(log_gamma * LOG2E, z3, z3, z3, z3, s0f, s0b, norm_g.reshape(1, HV))
    return o.reshape(B * T, HV), sf, sb


def _group_sum_matrix(width, group):
    i = np.arange(width)
    return jnp.asarray((i[:, None] // group == i[None, :] // group).astype(np.float32), dtype=BF16)


def _q_pad_matrix():
    m = np.zeros((B_H * B_HD, B_H * LANES), np.float32)
    g = B_H // B_HKV
    for h in range(B_H):
        for t in range(B_HD):
            m[h * B_HD + t, h * LANES + (h // g) * B_HD + t] = 1.0
    return jnp.asarray(m, dtype=BF16)


def _rope_tables(T):
    rows = T // GRID_W
    row = np.repeat(np.arange(rows, dtype=np.float64), GRID_W)
    col = np.tile(np.arange(GRID_W, dtype=np.float64), rows)
    nq = B_HD // 4
    inv = ROPE_THETA ** (-np.arange(nq, dtype=np.float64) / nq)
    ang = np.concatenate([row[:, None] * inv, col[:, None] * inv], axis=-1)
    cos = np.repeat(np.cos(ang), 2, axis=-1)
    sin = np.repeat(np.sin(ang), 2, axis=-1)
    sign = np.tile(np.array([-1.0, 1.0]), B_HD // 2)
    reps = LANES // B_HD
    return (jnp.asarray(np.tile(cos, (1, reps)), dtype=F32),
            jnp.asarray(np.tile(sin * sign, (1, reps)), dtype=F32))


def _group_rmsnorm(x, gsum, g):
    hi, lo = _split(x * x)
    ss = _dot(hi, gsum) + _dot(lo, gsum)
    return x * lax.rsqrt(ss * (1.0 / B_HD) + EPS) * g


def _rotate_pairs(x, cos, sin_signed):
    n = x.shape[1]
    lane = lax.broadcasted_iota(jnp.int32, x.shape, 1)
    partner = jnp.where(lane % 2 == 0, pltpu.roll(x, n - 1, 1), pltpu.roll(x, 1, 1))
    reps = n // LANES
    if reps > 1:
        cos = jnp.concatenate([cos] * reps, axis=1)
        sin_signed = jnp.concatenate([sin_signed] * reps, axis=1)
    return x * cos + partner * sin_signed


def _bprep_kernel(z_ref, qg_ref, kg_ref, cos_ref, sin_ref, gq_ref, gk_ref, pad_ref,
                  qpad_ref, kn_ref, kr_ref, vb_ref, *, rope):
    nq = B_H * B_HD
    nk = B_HKV * B_HD
    qn = _group_rmsnorm(z_ref[:, 0:nq], gq_ref[...], qg_ref[...])
    kn = _group_rmsnorm(z_ref[:, nq:nq + nk], gk_ref[...], kg_ref[...])
    kn_ref[...] = kn
    if rope:
        qn = _rotate_pairs(qn, cos_ref[...], sin_ref[...])
        kn = _rotate_pairs(kn, cos_ref[...], sin_ref[...])
    kr_ref[...] = kn.astype(BF16)
    vb_ref[...] = z_ref[:, nq + nk:nq + 2 * nk].astype(BF16)
    qs = (qn * (B_HD ** -0.5 * LOG2E)).astype(BF16)
    qpad_ref[...] = _dot(qs, pad_ref[...]).astype(BF16)


def _bprep(z, T, q_g, k_g, rope):
    N = z.shape[0]
    TM = min(512, T)
    nq = B_H * B_HD
    nk = B_HKV * B_HD
    width = nq + 2 * nk
    assert z.shape[1] == width
    cos, sin = _rope_tables(T if rope else TM)
    nt = T // TM if rope else 1
    const = lambda i: (0, 0)
    return pl.pallas_call(
        functools.partial(_bprep_kernel, rope=rope),
        grid=(N // TM,),
        in_specs=[pl.BlockSpec((TM, width), lambda i: (i, 0)),
                  pl.BlockSpec((1, nq), const),
                  pl.BlockSpec((1, nk), const),
                  pl.BlockSpec((TM, LANES), lambda i: (i % nt, 0)),
                  pl.BlockSpec((TM, LANES), lambda i: (i % nt, 0)),
                  pl.BlockSpec((nq, nq), const),
                  pl.BlockSpec((nk, nk), const),
                  pl.BlockSpec((nq, B_H * LANES), const)],
        out_specs=[pl.BlockSpec((TM, B_H * LANES), lambda i: (i, 0)),
                   pl.BlockSpec((TM, nk), lambda i: (i, 0)),
                   pl.BlockSpec((TM, nk), lambda i: (i, 0)),
                   pl.BlockSpec((TM, nk), lambda i: (i, 0))],
        out_shape=[jax.ShapeDtypeStruct((N, B_H * LANES), BF16),
                   jax.ShapeDtypeStruct((N, nk), F32),
                   jax.ShapeDtypeStruct((N, nk), BF16),
                   jax.ShapeDtypeStruct((N, nk), BF16)],
        compiler_params=_params("arbitrary"),
        name="attn_prep",
    )(z, jnp.tile(q_g, B_H).reshape(1, nq), jnp.tile(k_g, B_HKV).reshape(1, nk), cos, sin,
      _group_sum_matrix(nq, B_HD), _group_sum_matrix(nk, B_HD), _q_pad_matrix())


def _lane_fold(x, op):
    acc = x[:, 0:LANES]
    for j in range(1, x.shape[1] // LANES):
        acc = op(acc, x[:, j * LANES:(j + 1) * LANES])
    return acc


def _attn_kernel(*refs, has_cache, kc, nq):
    if has_cache:
        q_ref, k_ref, v_ref, ck_ref, cv_ref, o_ref, s_scr, m_scr, mprev_scr, l_scr, acc_scr = refs
        kcc = min(kc, ck_ref.shape[0])
        ncache = ck_ref.shape[0] // kcc
    else:
        q_ref, k_ref, v_ref, o_ref, s_scr, m_scr, mprev_scr, l_scr, acc_scr = refs
        kcc, ncache = kc, 0
    i = pl.program_id(1)
    tq = q_ref.shape[0]
    nlat = k_ref.shape[0] // kc

    def score(c, kblk):
        q = jnp.concatenate([q_ref[:, h * LANES:(h + 1) * LANES] for h in range(B_H)], axis=0)
        s = _dot_nt(q, kblk)
        s_scr[c, :, 0:kblk.shape[0]] = s
        m_scr[...] = jnp.maximum(m_scr[...], _lane_fold(s, jnp.maximum))

    def weight(c, vblk):
        s = s_scr[c, :, 0:vblk.shape[0]]
        mp = mprev_scr[...]
        ps = [jnp.exp2(s[:, j * LANES:(j + 1) * LANES] - mp) for j in range(vblk.shape[0] // LANES)]
        tot = ps[0]
        for pj in ps[1:]:
            tot = tot + pj
        l_scr[...] += tot
        acc_scr[...] += _dot(jnp.concatenate(ps, axis=1).astype(BF16), vblk)

    def run(do_weight, do_score):
        def unit(c, kblk, vblk):
            if do_weight:
                weight(c, vblk())
            if do_score:
                score(c, kblk())

        for c in range(ncache):
            unit(c, lambda: ck_ref[c * kcc:(c + 1) * kcc, :].astype(BF16),
                 lambda: cv_ref[c * kcc:(c + 1) * kcc, :].astype(BF16))

        def body(c, carry):
            rows = pl.ds(pl.multiple_of(c * kc, kc), kc)
            unit(ncache + c, lambda: k_ref[rows, :], lambda: v_ref[rows, :])
            return carry
        lax.fori_loop(0, nlat, body, 0)

    @pl.when(i < nq)
    def _():
        m_scr[...] = jnp.full(m_scr.shape, -jnp.inf, F32)

    @pl.when(i > 0)
    def _():
        l_scr[...] = jnp.zeros_like(l_scr)
        acc_scr[...] = jnp.zeros_like(acc_scr)

    @pl.when(i == 0)
    def _():
        run(False, True)

    @pl.when(jnp.logical_and(i > 0, i < nq))
    def _():
        run(True, True)

    @pl.when(i == nq)
    def _():
        run(True, False)

    @pl.when(i > 0)
    def _():
        r_all = acc_scr[...] / jnp.sum(l_scr[...], axis=-1, keepdims=True)
        g = B_H // B_HKV
        lane = lax.broadcasted_iota(jnp.int32, (tq, LANES), 1)
        outs = []
        for j in range(B_H // 2):
            pair = []
            for half in range(2):
                h = 2 * j + half
                r = r_all[h * tq:(h + 1) * tq, :]
                if h // g != half:
                    r = pltpu.roll(r, B_HD, 1)
                pair.append(r)
            outs.append(jnp.where(lane < B_HD, pair[0], pair[1]))
        o_ref[...] = jnp.concatenate(outs, axis=-1).astype(o_ref.dtype)

    @pl.when(i < nq)
    def _():
        mprev_scr[...] = jnp.broadcast_to(jnp.max(m_scr[...], axis=-1, keepdims=True), mprev_scr.shape)


def _attention(qpad, kr, vb, B, T, cache_k, cache_v):
    has_cache = cache_k is not None
    TQ = Q_TILE
    nq = T // TQ
    nk = B_HKV * B_HD
    in_specs = [pl.BlockSpec((TQ, B_H * LANES), lambda b, i: (b * nq + jnp.minimum(i, nq - 1), 0)),
                pl.BlockSpec((T, nk), lambda b, i: (b, 0)),
                pl.BlockSpec((T, nk), lambda b, i: (b, 0))]
    args = [qpad, kr, vb]
    kc = min(1024, T)
    nchunks = T // kc
    if has_cache:
        P = cache_k.shape[1]
        assert P % min(kc, P) == 0
        nchunks += P // min(kc, P)
        in_specs += [pl.BlockSpec((None, P, nk), lambda b, i: (b, 0, 0))] * 2
        args += [cache_k, cache_v]
    R = B_H * TQ
    return pl.pallas_call(
        functools.partial(_attn_kernel, has_cache=has_cache, kc=kc, nq=nq),
        grid=(B, nq + 1),
        in_specs=in_specs,
        out_specs=pl.BlockSpec((TQ, B_H * B_HD), lambda b, i: (b * nq + jnp.maximum(i - 1, 0), 0)),
        out_shape=jax.ShapeDtypeStruct((B * T, B_H * B_HD), BF16),
        scratch_shapes=[pltpu.VMEM((nchunks, R, kc), F32)] + [pltpu.VMEM((R, LANES), F32)] * 4,
        compiler_params=_params("arbitrary", "arbitrary"),
        name="attention",
    )(*args)


def _proj_res_kernel(*refs, n_in):
    x_ref, gate_ref = refs[0], refs[1]
    o_refs = refs[2:2 + n_in]
    w_refs = refs[2 + n_in:2 + 2 * n_in]
    out_ref = refs[2 + 2 * n_in]
    wbf_refs = refs[3 + 2 * n_in:]

    @pl.when(pl.program_id(0) == 0)
    def _():
        for w_ref, wbf_ref in zip(w_refs, wbf_refs):
            wbf_ref[...] = w_ref[...].astype(BF16)

    acc = _dot(o_refs[0][...], wbf_refs[0][...])
    for o_ref, wbf_ref in zip(o_refs[1:], wbf_refs[1:]):
        acc = acc + _dot(o_ref[...], wbf_ref[...])
    out_ref[...] = x_ref[...] + gate_ref[...] * acc


def _proj_res(x, mod, part, acts, w, rows_per_group):
    N, D = x.shape
    TM = min(1024, rows_per_group)
    n_in = len(acts)
    widths = [a.shape[1] for a in acts]
    offs = np.cumsum([0] + widths[:-1]).tolist()
    in_specs = [pl.BlockSpec((TM, D), lambda i: (i, 0)),
                _mod_spec(part, D, TM, rows_per_group, 0)]
    in_specs += [pl.BlockSpec((TM, wd), lambda i: (i, 0)) for wd in widths]
    in_specs += [pl.BlockSpec((wd, D), functools.partial(lambda i, blk: (blk, 0), blk=off // wd))
                 for wd, off in zip(widths, offs)]
    return pl.pallas_call(
        functools.partial(_proj_res_kernel, n_in=n_in),
        grid=(N // TM,),
        in_specs=in_specs,
        out_specs=pl.BlockSpec((TM, D), lambda i: (i, 0)),
        out_shape=jax.ShapeDtypeStruct((N, D), F32),
        scratch_shapes=[pltpu.VMEM((wd, D), BF16) for wd in widths],
        compiler_params=_params("arbitrary"),
        name="proj_residual",
    )(x, mod, *acts, *([w] * n_in))


def _ffn_kernel(x_ref, g_ref, sh_ref, sc_ref, gate_ref, w1_ref, w3_ref, w2_ref, out_ref, h_scr, acc_scr, *, nf):
    f = pl.program_id(1)

    @pl.when(f == 0)
    def _():
        h_scr[...] = _norm_mod(x_ref[...], g_ref[...], sh_ref[...], sc_ref[...]).astype(BF16)
        acc_scr[...] = jnp.zeros_like(acc_scr)

    h = h_scr[...]
    a = _dot(h, w1_ref[...].astype(BF16))
    b = _dot(h, w3_ref[...].astype(BF16))
    acc_scr[...] += _dot((_silu(a) * b).astype(BF16), w2_ref[...].astype(BF16))

    @pl.when(f == nf - 1)
    def _():
        out_ref[...] = x_ref[...] + gate_ref[...] * acc_scr[...]


def _ffn(x, g, mod, w1, w3, w2, rows_per_group):
    N, D = x.shape
    FF = w1.shape[1]
    TM, TF = min(1024, rows_per_group), 256
    nf = FF // TF
    return pl.pallas_call(
        functools.partial(_ffn_kernel, nf=nf),
        grid=(N // TM, nf),
        in_specs=[pl.BlockSpec((TM, D), lambda i, f: (i, 0)),
                  pl.BlockSpec((1, D), lambda i, f: (0, 0)),
                  _mod_spec(3, D, TM, rows_per_group, 0),
                  _mod_spec(4, D, TM, rows_per_group, 0),
                  _mod_spec(5, D, TM, rows_per_group, 0),
                  pl.BlockSpec((D, TF), lambda i, f: (0, f)),
                  pl.BlockSpec((D, TF), lambda i, f: (0, f)),
                  pl.BlockSpec((TF, D), lambda i, f: (f, 0))],
        out_specs=pl.BlockSpec((TM, D), lambda i, f: (i, 0)),
        out_shape=jax.ShapeDtypeStruct((N, D), F32),
        scratch_shapes=[pltpu.VMEM((TM, D), BF16), pltpu.VMEM((TM, D), F32)],
        compiler_params=_params("arbitrary", "arbitrary"),
        name="ffn",
    )(x, g.reshape(1, D), mod, mod, mod, w1, w3, w2)


MOE_SB = 1024
MOE_SBG = 512
MOE_GG = 4
MOE_TRG = 256
MOE_TR = 2048
MOE_CG = 4


def _two_stream_specs(shape, ntp, ax=0):
    def idx_p(*g):
        return (jnp.minimum(g[ax], ntp - 1), 0)

    def idx_s(*g):
        return (jnp.maximum(g[ax] - ntp, 0), 0)
    return pl.BlockSpec(shape, idx_p), pl.BlockSpec(shape, idx_s)


def _pool_mod_spec(part, D, TM, ntp, rows_per_group):
    def idx(i, *_):
        return (jnp.where(i < ntp, 0, 1 + ((i - ntp) * TM) // rows_per_group), part, 0, 0)
    return pl.BlockSpec((None, None, 1, D), idx)


def _route_kernel(xp_ref, xs_ref, g_ref, sh_ref, sc_ref, rw_ref, tri_ref, h_ref, info_ref, infot_ref, cum_ref,
                  carry_scr, *, ntp, cap):
    i = pl.program_id(0)

    @pl.when(i == 0)
    def _():
        carry_scr[...] = jnp.zeros_like(carry_scr)

    x = jnp.where(i < ntp, xp_ref[...], xs_ref[...])
    h = _norm_mod(x, g_ref[...], sh_ref[...], sc_ref[...])
    h_ref[...] = h.astype(BF16)
    lane = lax.broadcasted_iota(jnp.int32, (x.shape[0], LANES), 1).astype(F32)
    logits = _dot3(_split(h), _split(rw_ref[...]))
    logits = jnp.where(lane < N_EXPERTS, logits, -jnp.inf)
    m1 = jnp.max(logits, axis=-1, keepdims=True)
    i1 = jnp.min(jnp.where(logits == m1, lane, float(LANES)), axis=-1, keepdims=True)
    rest = jnp.where(lane == i1, -jnp.inf, logits)
    m2 = jnp.max(rest, axis=-1, keepdims=True)
    i2 = jnp.min(jnp.where(rest == m2, lane, float(LANES)), axis=-1, keepdims=True)
    e2 = jnp.exp(m2 - m1)
    w1 = 1.0 / (1.0 + e2)
    w2 = e2 / (1.0 + e2)
    ind = jnp.where(jnp.logical_or(lane == i1, lane == i2), 1.0, 0.0)
    before = _dot(tri_ref[...], ind.astype(BF16)) + carry_scr[...]
    r1 = jnp.sum(jnp.where(lane == i1, before, 0.0), axis=-1, keepdims=True) + i1 * float(cap)
    r2 = jnp.sum(jnp.where(lane == i2, before, 0.0), axis=-1, keepdims=True) + i2 * float(cap)
    total = carry_scr[...] + jnp.sum(ind, axis=0, keepdims=True)
    carry_scr[...] = total
    for part in range(1, MOE_SB // MOE_SBG):
        cum_ref[part - 1] = before[part * MOE_SBG:part * MOE_SBG + 1, :]
    cum_ref[MOE_SB // MOE_SBG - 1] = total
    info = jnp.where(lane == 0.0, i1, jnp.where(lane == 1.0, i2, jnp.where(lane == 2.0, w1, jnp.where(
        lane == 3.0, w2, jnp.where(lane == 4.0, r1, jnp.where(lane == 5.0, r2, 0.0))))))
    info_ref[...] = info[:, 0:SUB]
    info_t = jnp.concatenate([info[r:r + LANES, :].T for r in range(0, info.shape[0], LANES)], axis=1)
    infot_ref[...] = info_t[0:SUB, :]


def _moe_route(xp, xs, g, mod, router_w, rows_per_group):
    Np, D = xp.shape
    N = Np + xs.shape[0]
    TM = MOE_SB
    ntp = Np // TM
    nt = N // TM
    rw = jnp.pad(router_w, ((0, 0), (0, LANES - router_w.shape[1])))
    tri = jnp.asarray(np.tril(np.ones((TM, TM), np.float32), -1), dtype=BF16)
    xp_spec, xs_spec = _two_stream_specs((TM, D), ntp)
    return pl.pallas_call(
        functools.partial(_route_kernel, ntp=ntp, cap=_moe_cap(N)),
        grid=(nt,),
        in_specs=[xp_spec, xs_spec,
                  pl.BlockSpec((1, D), lambda i: (0, 0)),
                  _pool_mod_spec(3, D, TM, ntp, rows_per_group),
                  _pool_mod_spec(4, D, TM, ntp, rows_per_group),
                  pl.BlockSpec((D, LANES), lambda i: (0, 0)),
                  pl.BlockSpec((TM, TM), lambda i: (0, 0))],
        out_specs=[pl.BlockSpec((TM, D), lambda i: (i, 0)),
                   pl.BlockSpec((TM, SUB), lambda i: (i, 0)),
                   pl.BlockSpec((SUB, TM), lambda i: (0, i)),
                   pl.BlockSpec((MOE_SB // MOE_SBG, 1, LANES), lambda i: (i, 0, 0))],
        out_shape=[jax.ShapeDtypeStruct((N, D), BF16),
                   jax.ShapeDtypeStruct((N, SUB), F32),
                   jax.ShapeDtypeStruct((SUB, N), F32),
                   jax.ShapeDtypeStruct((nt * (MOE_SB // MOE_SBG), 1, LANES), F32)],
        scratch_shapes=[pltpu.VMEM((1, LANES), F32)],
        compiler_params=_params("arbitrary"),
        name="moe_route",
    )(xp, xs, g.reshape(1, D), mod, mod, rw, tri)


def _hold_unused(idx, used):
    steps, slots = idx.shape
    read = jnp.arange(slots, dtype=jnp.int32)[None, :] < used[:, None]
    step = jnp.arange(steps, dtype=jnp.int32)[:, None]
    last = lax.cummax(jnp.where(read, step, -1), axis=0)
    held = jnp.take_along_axis(idx, jnp.maximum(last, 0), axis=0)
    return jnp.where(last >= 0, held, 0)


def _moe_cap(N):
    return -(-N // MOE_TR) * MOE_TR


def _moe_plan(cum, N):
    E, SB, TRG, TR = N_EXPERTS, MOE_SB, MOE_TRG, MOE_TR
    NB = N // SB
    cap = _moe_cap(N)
    tpe_g, tpe = cap // TRG, cap // TR
    RG = E * tpe_g
    used_g, used_t = 2 * N // TRG + E, 2 * N // TR + E
    PMAX = used_g + E * NB
    i32 = jnp.int32
    parts = SB // MOE_SBG
    cum_g = cum[:, 0, :E].astype(i32).T
    cum_e = cum_g[:, parts - 1::parts]
    cnt = cum_e[:, -1]

    lo = jnp.asarray((np.arange(RG) % tpe_g) * TRG, dtype=i32)
    hi = jnp.minimum(lo + TRG, jnp.repeat(cnt, tpe_g))

    GG = MOE_GG
    cum_g_t = jnp.repeat(cum_g, tpe_g, axis=0)
    first_g = jnp.sum(cum_g_t <= lo[:, None], axis=1)
    last_g = jnp.sum(cum_g_t < hi[:, None], axis=1)
    nblk_g = jnp.where(hi > lo, last_g - first_g + 1, 0)
    nst = (nblk_g + GG - 1) // GG
    first_g, nblk_g, nst, gst_end = lax.optimization_barrier((first_g, nblk_g, nst, jnp.cumsum(nst)))
    gtotal = gst_end[-1]
    smax_g = (used_g + E * NB * parts + (GG - 1) * used_g) // GG + 1
    jg = jnp.arange(smax_g, dtype=i32)
    g_ok = jg < gtotal
    jgc = jnp.minimum(jg, gtotal - 1)
    tq = jnp.minimum(jnp.sum(gst_end[None, :] <= jgc[:, None], axis=1), RG - 1).astype(i32)
    gg = jgc - (gst_end - nst)[tq]
    g_slots = jnp.where(g_ok, jnp.clip(nblk_g[tq] - GG * gg, 0, GG), 0)
    g_parts = _hold_unused((first_g[tq] + GG * gg)[:, None] + jnp.arange(GG, dtype=i32)[None, :], g_slots)
    g_first = jnp.logical_and(g_ok, gg == 0)
    gather_plan = (tq, g_parts.reshape(-1).astype(i32), g_slots.astype(i32), g_first.astype(i32))

    CG = MOE_CG
    c_lo = jnp.concatenate([jnp.zeros((E, 1), i32), cum_e[:, :-1]], axis=1)
    t_lo = c_lo // TRG
    n_se = jnp.where(cum_e > c_lo, (cum_e - 1) // TRG - t_lo + 1, 0)
    cn = jnp.cumsum(n_se, axis=0)
    per_blk = cn[-1]
    nsteps = (per_blk + CG - 1) // CG
    t_lo, n_se, cn, per_blk, nsteps, st_end = lax.optimization_barrier(
        (t_lo, n_se, cn, per_blk, nsteps, jnp.cumsum(nsteps)))
    total = st_end[-1]
    SMAX = (PMAX + CG - 1) // CG + NB
    j = jnp.arange(SMAX, dtype=i32)
    step_ok = j < total
    jc = jnp.minimum(j, total - 1)
    blk = jnp.minimum(jnp.sum(st_end[None, :] <= jc[:, None], axis=1), NB - 1).astype(i32)
    grp = jc - (st_end - nsteps)[blk]
    m = CG * grp[:, None] + jnp.arange(CG, dtype=i32)[None, :]
    slot_ok = jnp.logical_and(m < per_blk[blk][:, None], step_ok[:, None])
    c_slots = jnp.sum(slot_ok, axis=1).astype(i32)
    cn_b = cn[:, blk].T
    e_of = jnp.minimum(jnp.sum(cn_b[:, None, :] <= m[:, :, None], axis=2), E - 1)
    before = jnp.take_along_axis(cn_b - n_se[:, blk].T, e_of, axis=1)
    tile = e_of * tpe_g + jnp.take_along_axis(t_lo[:, blk].T, e_of, axis=1) + m - before
    slot_q = _hold_unused(tile, c_slots)
    c_first = jnp.logical_and(step_ok, grp == 0).astype(i32)
    c_last = jnp.logical_and(step_ok, grp == nsteps[blk] - 1).astype(i32)
    combine_plan = (blk, slot_q.reshape(-1).astype(i32), c_slots, c_first, c_last)

    tiles = (cnt + TR - 1) // TR
    t_end = jnp.cumsum(tiles)
    jt = jnp.arange(used_t, dtype=i32)
    t_ok = jt < t_end[-1]
    jtc = jnp.minimum(jt, t_end[-1] - 1)
    te = jnp.minimum(jnp.sum(t_end[None, :] <= jtc[:, None], axis=1), E - 1).astype(i32)
    tk = jtc - (t_end - tiles)[te]
    tvalid = jnp.where(t_ok, jnp.clip(cnt[te] - tk * TR, 0, TR), 0)
    ffn_plan = ((te * tpe + tk).astype(i32), te, tvalid.astype(i32))
    return gather_plan, combine_plan, ffn_plan, E * cap


def _moe_gather_kernel(q_ref, s_ref, slots_ref, first_ref, *refs):
    pos_refs, h_refs, out_ref = refs[:MOE_GG], refs[MOE_GG:2 * MOE_GG], refs[2 * MOE_GG]
    p = pl.program_id(0)
    rows = out_ref.shape[0]

    @pl.when(first_ref[p] == 1)
    def _():
        out_ref[...] = jnp.zeros_like(out_ref)

    for ns in range(1, MOE_GG + 1):
        @pl.when(slots_ref[p] == ns)
        def _(ns=ns):
            row = (lax.broadcasted_iota(jnp.int32, (rows, 1), 0) + q_ref[p] * rows).astype(F32)
            sels = []
            for k in range(ns):
                hit = jnp.logical_or(pos_refs[k][4:5, :] == row, pos_refs[k][5:6, :] == row)
                sels.append(jnp.where(hit, 1.0, 0.0).astype(BF16))
            sel = sels[0] if ns == 1 else jnp.concatenate(sels, axis=1)
            hs = h_refs[0][...] if ns == 1 else jnp.concatenate([h_refs[k][...] for k in range(ns)], axis=0)
            out_ref[...] = out_ref[...] + _dot(sel, hs).astype(BF16)


def _moe_gather(h, pos_rows, plan, rmax):
    N, D = h.shape
    nsteps = plan[0].shape[0]

    def pos_spec(k):
        return pl.BlockSpec((SUB, MOE_SBG), lambda p, q, s, *_: (0, s[MOE_GG * p + k]))

    def tok_spec(k):
        return pl.BlockSpec((MOE_SBG, D), lambda p, q, s, *_: (s[MOE_GG * p + k], 0))

    return pl.pallas_call(
        _moe_gather_kernel,
        grid_spec=pltpu.PrefetchScalarGridSpec(
            num_scalar_prefetch=4, grid=(nsteps,),
            in_specs=[pos_spec(k) for k in range(MOE_GG)] + [tok_spec(k) for k in range(MOE_GG)],
            out_specs=pl.BlockSpec((MOE_TRG, D), lambda p, q, *_: (q[p], 0))),
        out_shape=jax.ShapeDtypeStruct((rmax, D), BF16),
        compiler_params=_params("arbitrary"),
        name="moe_gather",
    )(*plan, *([pos_rows] * MOE_GG), *([h] * MOE_GG))


def _moe_ffn_kernel(t_ref, e_ref, nv_ref, x_ref, w1_ref, w3_ref, w2_ref, out_ref, acc_scr, *, nf):
    t = pl.program_id(0)
    f = pl.program_id(1)
    nv = nv_ref[t]

    def block(start, size):
        rows = pl.ds(start, size)

        @pl.when(f == 0)
        def _():
            acc_scr[rows, :] = jnp.zeros((size, acc_scr.shape[1]), F32)

        x = x_ref[rows, :]
        a = _dot(x, w1_ref[...].astype(BF16))
        b = _dot(x, w3_ref[...].astype(BF16))
        acc_scr[rows, :] += _dot((_silu(a) * b).astype(BF16), w2_ref[...].astype(BF16))

        @pl.when(f == nf - 1)
        def _():
            out_ref[rows, :] = acc_scr[rows, :].astype(out_ref.dtype)

    nsub = MOE_TR // MOE_TRG
    used = (nv + MOE_TRG - 1) // MOE_TRG

    @pl.when(used == nsub)
    def _():
        block(0, MOE_TR)

    @pl.when(jnp.logical_and(used > 0, used < nsub))
    def _():
        start = jnp.int32(0)
        size = MOE_TR // 2
        while size >= MOE_TRG:
            has = (used & (size // MOE_TRG)) != 0

            @pl.when(has)
            def _(start=start, size=size):
                block(pl.multiple_of(start, MOE_TRG), size)

            start = start + jnp.where(has, size, 0)
            size //= 2


def _moe_ffn(xs, plan, w1, w3, w2):
    rmax, D = xs.shape
    FF = w1.shape[2]
    TF = 256
    nf = FF // TF
    RT = plan[0].shape[0]

    def fidx(t, f, nv):
        return jnp.where(nv[t] > 0, f, nf - 1)

    return pl.pallas_call(
        functools.partial(_moe_ffn_kernel, nf=nf),
        grid_spec=pltpu.PrefetchScalarGridSpec(
            num_scalar_prefetch=3, grid=(RT, nf),
            in_specs=[pl.BlockSpec((MOE_TR, D), lambda t, f, ti, e, nv: (ti[t], 0)),
                      pl.BlockSpec((None, D, TF), lambda t, f, ti, e, nv: (e[t], 0, fidx(t, f, nv))),
                      pl.BlockSpec((None, D, TF), lambda t, f, ti, e, nv: (e[t], 0, fidx(t, f, nv))),
                      pl.BlockSpec((None, TF, D), lambda t, f, ti, e, nv: (e[t], fidx(t, f, nv), 0))],
            out_specs=pl.BlockSpec((MOE_TR, D), lambda t, f, ti, e, nv: (ti[t], 0)),
            scratch_shapes=[pltpu.VMEM((MOE_TR, D), F32)]),
        out_shape=jax.ShapeDtypeStruct((rmax, D), BF16),
        compiler_params=_params("arbitrary", "arbitrary"),
        name="moe_ffn",
    )(*plan, xs, w1, w3, w2)


def _moe_combine_kernel(s_ref, q_ref, slots_ref, first_ref, last_ref, pos_ref, *refs, ntp):
    ys_refs = refs[:MOE_CG]
    xp_ref, xs_ref, gate_ref, fg_ref, op_ref, os_ref, acc_scr = refs[MOE_CG:]
    p = pl.program_id(0)
    rows = ys_refs[0].shape[0]

    @pl.when(first_ref[p] == 1)
    def _():
        acc_scr[...] = jnp.zeros_like(acc_scr)

    for ns in range(1, MOE_CG + 1):
        @pl.when(slots_ref[p] == ns)
        def _(ns=ns):
            sels = []
            for k in range(ns):
                col = (lax.broadcasted_iota(jnp.int32, (1, rows), 1) + q_ref[MOE_CG * p + k] * rows).astype(F32)
                sels.append((jnp.where(pos_ref[:, 4:5] == col, pos_ref[:, 2:3], 0.0)
                             + jnp.where(pos_ref[:, 5:6] == col, pos_ref[:, 3:4], 0.0)).astype(BF16))
            sel = sels[0] if ns == 1 else jnp.concatenate(sels, axis=1)
            ys = ys_refs[0][...] if ns == 1 else jnp.concatenate([ys_refs[k][...] for k in range(ns)], axis=0)
            acc_scr[...] += _dot(sel, ys)

    @pl.when(last_ref[p] == 1)
    def _():
        s = s_ref[p]
        x = jnp.where(s < ntp, xp_ref[...], xs_ref[...])
        y = x + gate_ref[...] * acc_scr[...]
        out = y * lax.rsqrt(jnp.mean(y * y, axis=-1, keepdims=True) + EPS) * fg_ref[...]

        @pl.when(s < ntp)
        def _():
            op_ref[...] = out

        @pl.when(s >= ntp)
        def _():
            os_ref[...] = out


def _moe_combine(ys, pos_cols, plan, xp, xs, mod, final_g, rows_per_group):
    Np, D = xp.shape
    Ns = xs.shape[0]
    SB = MOE_SB
    ntp = Np // SB
    nsteps = plan[0].shape[0]

    def tile_spec(k):
        return pl.BlockSpec((MOE_TRG, D), lambda p, s, q, *_: (q[MOE_CG * p + k], 0))

    def tok_p(p, s, *_):
        return (jnp.minimum(s[p], ntp - 1), 0)

    def tok_s(p, s, *_):
        return (jnp.maximum(s[p] - ntp, 0), 0)

    def gate_idx(p, s, *_):
        return (jnp.where(s[p] < ntp, 0, 1 + ((s[p] - ntp) * SB) // rows_per_group), 5, 0, 0)

    return pl.pallas_call(
        functools.partial(_moe_combine_kernel, ntp=ntp),
        grid_spec=pltpu.PrefetchScalarGridSpec(
            num_scalar_prefetch=5, grid=(nsteps,),
            in_specs=[pl.BlockSpec((SB, SUB), lambda p, s, q, *_: (s[p], 0))]
            + [tile_spec(k) for k in range(MOE_CG)]
            + [pl.BlockSpec((SB, D), tok_p),
                      pl.BlockSpec((SB, D), tok_s),
                      pl.BlockSpec((None, None, 1, D), gate_idx),
                      pl.BlockSpec((1, D), lambda p, *_: (0, 0))],
            out_specs=[pl.BlockSpec((SB, D), tok_p), pl.BlockSpec((SB, D), tok_s)],
            scratch_shapes=[pltpu.VMEM((SB, D), F32)]),
        out_shape=[jax.ShapeDtypeStruct((Np, D), F32), jax.ShapeDtypeStruct((Ns, D), F32)],
        compiler_params=_params("arbitrary"),
        name="moe_combine",
    )(*plan, pos_cols, *([ys] * MOE_CG), xp, xs, mod, final_g.reshape(1, D))


def _moe(xp, xs, g, mod, router_w, w1, w3, w2, final_g, rows_per_group):
    N = xp.shape[0] + xs.shape[0]
    h, info, info_t, cum = _moe_route(xp, xs, g, mod, router_w, rows_per_group)
    gather_plan, combine_plan, ffn_plan, rmax = _moe_plan(cum, N)
    x_sorted = _moe_gather(h, info_t, gather_plan, rmax)
    y_sorted = _moe_ffn(x_sorted, ffn_plan, w1, w3, w2)
    return _moe_combine(y_sorted, info, combine_plan, xp, xs, mod, final_g, rows_per_group)


def _gla_levels(C):
    lv, c = [], C // 2
    while c >= SUB:
        lv.append(c)
        c //= 2
    return lv


def _gla_tables(C):
    levels = _gla_levels(C)
    nr = 1 + len(levels)
    mat = np.zeros((2, nr * C, C), np.float32)
    code = np.zeros((2, C, C), np.int32)
    for d in range(2):
        p = np.arange(C) if d == 0 else C - 1 - np.arange(C)
        pi, pj = p[:, None], p[None, :]
        mat[d, 0:C] = pj <= pi
        code[d] = np.where((pj <= pi) & (pi // SUB == pj // SUB), 1, 0)
        for lv, c in enumerate(levels):
            blk = pi // c
            later = blk % 2 == 1
            mat[d, (1 + lv) * C:(2 + lv) * C] = ((later & (pj > blk * c - 1) & (pj <= pi))
                                                 | (~later & (pj > pi) & (pj <= (blk + 1) * c - 1)))
            pair = (pi // (2 * c) == pj // (2 * c)) & (pi // c != pj // c) & (pj <= pi)
            code[d] = np.where(pair, 2 + lv, code[d])
    ones = np.zeros((SUB * LANES, C), np.float32)
    for jj in range(SUB):
        ones[jj * LANES:(jj + 1) * LANES, jj::SUB] = 1.0
    return jnp.asarray(mat, dtype=BF16), jnp.asarray(code), jnp.asarray(ones, dtype=BF16)


def _bcast_sublane(x, jj):
    r, w = x.shape
    x3 = x.reshape(r // SUB, SUB, w)
    return jnp.broadcast_to(x3[:, jj:jj + 1, :], x3.shape).reshape(r, w)


def _t128(x):
    r, w = x.shape
    if w > LANES:
        return jnp.concatenate([x[:, i:i + LANES].T for i in range(0, w, LANES)], axis=0)
    return jnp.concatenate([x[i:i + LANES, :].T for i in range(0, r, LANES)], axis=1)


def _gla_kernel(q_ref, k_ref, v_ref, g_ref, lr_ref, wg_ref, ba_ref, mat_ref, code_ref, ones_ref,
                s0f_ref, s0b_ref, ng_ref, o_ref, sf_ref, sb_ref, st_scr, of_scr, *, n, C, G):
    d = pl.program_id(1)
    c = pl.program_id(2)
    levels = _gla_levels(C)

    @pl.when(jnp.logical_and(c == 0, d == 0))
    def _():
        for bb in range(G):
            for h in range(C_H):
                st_scr[bb, h] = _t128(s0f_ref[bb, h])

    @pl.when(jnp.logical_and(c == 0, d == 1))
    def _():
        for bb in range(G):
            for h in range(C_H):
                st_scr[bb, h] = _t128(s0b_ref[bb, h])

    mat = mat_ref[...]
    code = code_ref[...]
    ones = ones_ref[...]
    wg = _split(wg_ref[...])
    cums = []
    for bb in range(G):
        xg = _dot3(_split(lr_ref[bb]), wg) + ba_ref[...]
        la = (jnp.minimum(xg, 0.0) - jnp.log1p(jnp.exp(-jnp.abs(xg)))) * (LOG2E / C_TAU)
        hi, lo = _split(la)
        cums.append(_dot(mat, hi) + _dot(mat, lo))

    def prepare(bb, h):
        cum = cums[bb]
        ks = slice(h * C_DK, (h + 1) * C_DK)
        qh = q_ref[bb, :, ks].astype(F32) * (C_DK ** -0.5)
        kh = k_ref[bb, :, ks].astype(F32)
        b = cum[0:C, ks]
        b_end = jnp.min(b, axis=0, keepdims=True)
        b_rest = b_end - b
        ps = []
        for jj in range(SUB):
            dec = jnp.exp2(jnp.minimum(b - _bcast_sublane(b, jj), 0.0))
            ps.append((qh * _bcast_sublane(kh, jj) * dec).astype(BF16))
        lv_ops = []
        for lv in range(len(levels)):
            fac = jnp.exp2(cum[(1 + lv) * C:(2 + lv) * C, ks])
            lv_ops.append(((qh * fac).astype(BF16), (kh * fac).astype(BF16)))
        qe = (qh * jnp.exp2(b)).astype(BF16)
        ke = (kh * jnp.exp2(b_rest)).astype(BF16)
        e_end = jnp.exp2(b_end)
        return jnp.concatenate(ps, axis=1), lv_ops, qe, ke, e_end

    def contract(bb, h, prep):
        pcat, lv_ops, qe, ke, e_end = prep
        vh = v_ref[bb, :, h * C_DV:(h + 1) * C_DV].astype(F32)
        att = jnp.where(code == 1, _dot(pcat, ones), 0.0)
        for lv, (qs, ks_) in enumerate(lv_ops):
            att = jnp.where(code == 2 + lv, _dot_nt(qs, ks_), att)
        st = st_scr[bb, h]
        o = _dot(att.astype(BF16), vh.astype(BF16)) + _dot_nt(qe, st.astype(BF16))
        st_scr[bb, h] = e_end * st + _dot(_t128(vh).astype(BF16), ke)
        return o

    units = [(bb, h) for h in range(C_H) for bb in range(G)]
    outs = {}
    prep = prepare(*units[0])
    for idx, (bb, h) in enumerate(units):
        nxt = prepare(*units[idx + 1]) if idx + 1 < len(units) else None
        outs[(bb, h)] = contract(bb, h, prep)
        prep = nxt
    o_all = [jnp.concatenate([outs[(bb, h)] for h in range(C_H)], axis=-1) for bb in range(G)]

    @pl.when(d == 0)
    def _():
        for bb in range(G):
            of_scr[bb, c] = o_all[bb]

    @pl.when(d == 1)
    def _():
        for bb in range(G):
            tot = o_all[bb] + of_scr[bb, n - 1 - c]
            res = []
            for h in range(C_H):
                sl = slice(h * C_DV, (h + 1) * C_DV)
                t = tot[:, sl]
                y = t * lax.rsqrt(jnp.mean(t * t, axis=-1, keepdims=True) + EPS) * ng_ref[:, sl]
                res.append(y * _silu(g_ref[bb, :, sl].astype(F32)))
            o_ref[bb] = jnp.concatenate(res, axis=-1).astype(o_ref.dtype)

    @pl.when(jnp.logical_and(c == n - 1, d == 0))
    def _():
        for bb in range(G):
            for h in range(C_H):
                sf_ref[bb, h] = _t128(st_scr[bb, h])

    @pl.when(jnp.logical_and(c == n - 1, d == 1))
    def _():
        for bb in range(G):
            for h in range(C_H):
                sb_ref[bb, h] = _t128(st_scr[bb, h])


def _gla(z, zg, B, T, w_a2, b_a, s0f, s0b, norm_g):
    C = GLA_CHUNK
    G = _rows_per_step(B, T, C_H * C_DV)
    assert B % G == 0 and T % C == 0
    n = T // C
    HK = C_H * C_DK
    HV = C_H * C_DV
    mat, code, ones = _gla_tables(C)
    nr = mat.shape[1] // C
    wg = jnp.zeros((2, LANES, HK), F32)
    for dr in range(2):
        wg = wg.at[dr, dr * C_RANK:(dr + 1) * C_RANK, :].set(w_a2[dr])
    z3 = z.reshape(B, T, z.shape[1])
    zg3 = zg.reshape(B, T, zg.shape[1])

    def chunk(d, c):
        return c + d * (n - 1 - 2 * c)

    st_spec = pl.BlockSpec((G, C_H, C_DK, C_DV), lambda b, d, c: (b, 0, 0, 0))
    st_shape = jax.ShapeDtypeStruct((B, C_H, C_DK, C_DV), F32)
    o, sf, sb = pl.pallas_call(
        functools.partial(_gla_kernel, n=n, C=C, G=G),
        grid=(B // G, 2, n),
        in_specs=[pl.BlockSpec((G, C, HK), lambda b, d, c: (b, chunk(d, c), 0)),
                  pl.BlockSpec((G, C, HK), lambda b, d, c: (b, chunk(d, c), 1)),
                  pl.BlockSpec((G, C, HV), lambda b, d, c: (b, chunk(d, c), 1)),
                  pl.BlockSpec((G, C, HV), lambda b, d, c: (b, chunk(d, c), 2)),
                  pl.BlockSpec((G, C, LANES), lambda b, d, c: (b, chunk(d, c), 0)),
                  pl.BlockSpec((None, LANES, HK), lambda b, d, c: (d, 0, 0)),
                  pl.BlockSpec((None, 1, HK), lambda b, d, c: (d, 0, 0)),
                  pl.BlockSpec((None, nr * C, C), lambda b, d, c: (d, 0, 0)),
                  pl.BlockSpec((None, C, C), lambda b, d, c: (d, 0, 0)),
                  pl.BlockSpec((SUB * LANES, C), lambda b, d, c: (0, 0)),
                  st_spec, st_spec,
                  pl.BlockSpec((1, HV), lambda b, d, c: (0, 0))],
        out_specs=[pl.BlockSpec((G, C, HV), lambda b, d, c: (b, (n - 1) - d * c, 0)),
                   st_spec, st_spec],
        out_shape=[jax.ShapeDtypeStruct((B, T, HV), BF16), st_shape, st_shape],
        scratch_shapes=[pltpu.VMEM((G, C_H, C_DV, C_DK), F32), pltpu.VMEM((G, n, C, HV), F32)],
        compiler_params=_params("arbitrary", "arbitrary", "arbitrary"),
        name="gla",
    )(z3, z3, z3, z3, zg3, wg, b_a.reshape(2, 1, HK), mat, code, ones, s0f, s0b, norm_g.reshape(1, HV))
    return o.reshape(B * T, HV), sf, sb


def _run_stream(x, B, T, mods, ctx, p):
    N, D = x.shape
    rpg = N // mods[0].shape[0]
    TM = min(2048, rpg)
    nb = (B_H + 2 * B_HKV) * B_HD

    w_in = p['even_w_in'][0]
    z, zb = _norm_mm(x, p['norm1_g'][0], mods[0], (0, 1), w_in, w_in, (nb, MIX_MAIN // nb), TM, rpg)
    if ctx is None:
        s0 = jnp.zeros((B, A_H, A_DK, A_DV), F32)
        a_f0, a_b0, cache_k, cache_v = s0, s0, None, None
    else:
        cache_k, cache_v, a_f0, a_b0 = ctx[0], ctx[1], ctx[2], ctx[3]
    o_a, a_sf, a_sb = _retention(z, B, T, p['a_log_gamma'][0], a_f0, a_b0, p['a_norm_g'][0])
    qpad, k_norm, k_rot, v_bf = _bprep(zb, T, p['b_q_g'][0], p['b_k_g'][0], rope=ctx is not None)
    o_b = _attention(qpad, k_rot, v_bf, B, T, cache_k, cache_v)
    x = _proj_res(x, mods[0], 2, [o_a, o_b], p['even_w_out'][0], rpg)
    x = _ffn(x, p['norm2_g'][0], mods[0], p['ff_w1'][0], p['ff_w3'][0], p['ff_w2'][0], rpg)

    w_in = p['odd_w_in'][0]
    w_gate = jnp.pad(w_in[:, MIX_MAIN:], ((0, 0), (0, LANES - 2 * C_RANK)))
    z1, z1g = _norm_mm(x, p['norm1_g'][1], mods[1], (0, 1), w_in, w_gate, (LANES, 0), TM, rpg)
    if ctx is None:
        s0 = jnp.zeros((B, C_H, C_DK, C_DV), F32)
        c_f0, c_b0 = s0, s0
    else:
        c_f0, c_b0 = ctx[4], ctx[5]
    o_c, c_sf, c_sb = _gla(z1, z1g, B, T, p['c_w_a2'][0], p['c_b_a'][0], c_f0, c_b0, p['c_norm_g'][0])
    x = _proj_res(x, mods[1], 2, [o_c], p['odd_w_out'][0], rpg)
    v_raw = zb[:, (B_H + B_HKV) * B_HD:]
    return x, (k_norm, v_raw, a_sf, a_sb, c_sf, c_sb)


def kernel(x_prompt, x_sample, c, cache_b_k, cache_b_v, state_a_fwd, state_a_bwd, state_c_fwd, state_c_bwd,
           c_ctx, w_mod, b_mod, norm1_g, norm2_g, final_g, even_w_in, even_w_out, a_log_gamma, a_norm_g,
           b_q_g, b_k_g, odd_w_in, c_w_a2, c_b_a, c_norm_g, odd_w_out, ff_w1, ff_w3, ff_w2,
           router_w, moe_w1, moe_w3, moe_w2):
    Bp, Tp, D = x_prompt.shape
    Bs, Ts, _ = x_sample.shape
    L = w_mod.shape[0]
    assert L == 2 and even_w_in.shape[0] == 1 and odd_w_in.shape[0] == 1
    p = dict(norm1_g=norm1_g, norm2_g=norm2_g, final_g=final_g, even_w_in=even_w_in, even_w_out=even_w_out,
             a_log_gamma=a_log_gamma, a_norm_g=a_norm_g, b_q_g=b_q_g, b_k_g=b_k_g, odd_w_in=odd_w_in,
             c_w_a2=c_w_a2, c_b_a=c_b_a, c_norm_g=c_norm_g, odd_w_out=odd_w_out, ff_w1=ff_w1, ff_w3=ff_w3,
             ff_w2=ff_w2, router_w=router_w, moe_w1=moe_w1, moe_w3=moe_w3, moe_w2=moe_w2)

    rows = 8
    conds = jnp.concatenate([c_ctx[None, :], c, jnp.zeros((rows - 1 - Bs, D), F32)], axis=0)
    mod = _modulation(conds, w_mod, b_mod).reshape(L, rows, 6, 1, D)
    mods_p = [mod[l, 0:1] for l in range(L)]
    mods_s = [mod[l, 1:1 + Bs] for l in range(L)]

    x_p, kept = _run_stream(x_prompt.reshape(Bp * Tp, D), Bp, Tp, mods_p, None, p)
    nk = B_HKV * B_HD
    ctx = (cache_b_k[:, 0].reshape(Bs, -1, nk), cache_b_v[:, 0].reshape(Bs, -1, nk),
           state_a_fwd[:, 0], state_a_bwd[:, 0], state_c_fwd[:, 0], state_c_bwd[:, 0])
    x_s, _ = _run_stream(x_sample.reshape(Bs * Ts, D), Bs, Ts, mods_s, ctx, p)
    y_p, y_s = _moe(x_p, x_s, norm2_g[1], mod[1, 0:1 + Bs], router_w[0], moe_w1[0], moe_w3[0], moe_w2[0],
                    final_g, Ts)

    k_norm, v_raw, a_sf, a_sb, c_sf, c_sb = kept
    return (y_p.reshape(Bp, Tp, D), y_s.reshape(Bs, Ts, D),
            k_norm.reshape(Bp, 1, Tp, B_HKV, B_HD), v_raw.reshape(Bp, 1, Tp, B_HKV, B_HD),
            a_sf[:, None], a_sb[:, None], c_sf[:, None], c_sb[:, None])
```

```python
import functools

import numpy as np
import jax
import jax.numpy as jnp
from jax import lax
from jax.experimental import pallas as pl
from jax.experimental.pallas import tpu as pltpu

F32 = jnp.float32
BF16 = jnp.bfloat16
EPS = 1e-6
LOG2E = 1.4426950408889634

VMEM_LIMIT_BYTES = 56 * 1024 * 1024

A_H, A_DK, A_DV = 4, 128, 256
B_H, B_HKV, B_HD = 8, 2, 64
C_H, C_DK, C_DV, C_RANK = 4, 128, 256, 16
C_TAU = 16.0
GRID_W = 64
ROPE_THETA = 10000.0
N_EXPERTS = 8
LANES = 128
SUB = 8
RET_CHUNK = 128
GLA_CHUNK = 128
Q_TILE = 128
SCAN_ROWS_MAX = 4
SCAN_FWD_BYTES = 32 * 1024 * 1024


def _rows_per_step(B, T, width):
    g = SCAN_ROWS_MAX
    while g > 1 and (B % g or g * T * width * 4 > SCAN_FWD_BYTES):
        g //= 2
    return g


def _params(*sem):
    return pltpu.CompilerParams(dimension_semantics=sem, vmem_limit_bytes=VMEM_LIMIT_BYTES)


def _dot(a, b):
    return jnp.dot(a, b, preferred_element_type=F32)


def _dot_nt(a, b):
    return lax.dot_general(a, b, (((1,), (1,)), ((), ())), preferred_element_type=F32)


def _split(x):
    hi = x.astype(BF16)
    return hi, (x - hi.astype(F32)).astype(BF16)


def _dot3(a, b):
    return _dot(a[0], b[0]) + (_dot(a[0], b[1]) + _dot(a[1], b[0]))


def _silu(x):
    return x * jax.nn.sigmoid(x)


def _norm_mod(x, g, sh, sc):
    r = lax.rsqrt(jnp.mean(x * x, axis=-1, keepdims=True) + EPS)
    return (x * r * g) * (1.0 + sc) + sh


def _mod_kernel(c_ref, w_ref, b_ref, o_ref):
    o_ref[...] = _dot3(_split(_silu(c_ref[...])), _split(w_ref[...])) + b_ref[...]


def _modulation(conds, w_mod, b_mod):
    L, D, D6 = w_mod.shape
    R = conds.shape[0]
    TN = 1024
    return pl.pallas_call(
        _mod_kernel,
        grid=(L, D6 // TN),
        in_specs=[pl.BlockSpec((R, D), lambda l, j: (0, 0)),
                  pl.BlockSpec((None, D, TN), lambda l, j: (l, 0, j)),
                  pl.BlockSpec((None, 1, TN), lambda l, j: (l, 0, j))],
        out_specs=pl.BlockSpec((None, R, TN), lambda l, j: (l, 0, j)),
        out_shape=jax.ShapeDtypeStruct((L, R, D6), F32),
        compiler_params=_params("arbitrary", "arbitrary"),
        name="modulation",
    )(conds, w_mod, b_mod.reshape(L, 1, D6))


def _mod_spec(part, D, TM, rows_per_group, axis):
    def idx(*g):
        return ((g[axis] * TM) // rows_per_group, part, 0, 0)
    return pl.BlockSpec((None, None, 1, D), idx)


MIX_MAIN = A_H * (2 * A_DK + 2 * A_DV)
MIX_TN = 768


def _norm_mm_kernel(x_ref, g_ref, sh_ref, sc_ref, w_ref, we_ref, o_ref, oe_ref, h_scr, *, nmain):
    j = pl.program_id(1)

    @pl.when(j == 0)
    def _():
        h_scr[...] = _norm_mod(x_ref[...], g_ref[...], sh_ref[...], sc_ref[...]).astype(BF16)

    @pl.when(j < nmain)
    def _():
        o_ref[...] = _dot(h_scr[...], w_ref[...].astype(BF16)).astype(o_ref.dtype)

    @pl.when(j == nmain)
    def _():
        oe_ref[...] = _dot(h_scr[...], we_ref[...].astype(BF16))


def _norm_mm(x, g, mod, parts, w, w_extra, extra_block, TM, rows_per_group):
    N, D = x.shape
    nmain = MIX_MAIN // MIX_TN
    WE = extra_block[0]
    return pl.pallas_call(
        functools.partial(_norm_mm_kernel, nmain=nmain),
        grid=(N // TM, nmain + 1),
        in_specs=[pl.BlockSpec((TM, D), lambda i, j: (i, 0)),
                  pl.BlockSpec((1, D), lambda i, j: (0, 0)),
                  _mod_spec(parts[0], D, TM, rows_per_group, 0),
                  _mod_spec(parts[1], D, TM, rows_per_group, 0),
                  pl.BlockSpec((D, MIX_TN), lambda i, j: (0, jnp.minimum(j, nmain - 1))),
                  pl.BlockSpec((D, WE), lambda i, j: (0, extra_block[1]))],
        out_specs=[pl.BlockSpec((TM, MIX_TN), lambda i, j: (i, jnp.minimum(j, nmain - 1))),
                   pl.BlockSpec((TM, WE), lambda i, j: (i, 0))],
        out_shape=[jax.ShapeDtypeStruct((N, MIX_MAIN), BF16), jax.ShapeDtypeStruct((N, WE), F32)],
        scratch_shapes=[pltpu.VMEM((TM, D), BF16)],
        compiler_params=_params("arbitrary", "arbitrary"),
        name="norm_mm",
    )(x, g.reshape(1, D), mod, mod, w, w_extra)


def _ret_kernel(lg_ref, q_ref, k_ref, v_ref, ag_ref, s0f_ref, s0b_ref, ng_ref,
                o_ref, sf_ref, sb_ref, s_scr, of_scr, *, n, C, G):
    d = pl.program_id(1)
    c = pl.program_id(2)

    @pl.when(jnp.logical_and(c == 0, d == 0))
    def _():
        s_scr[...] = s0f_ref[...]

    @pl.when(jnp.logical_and(c == 0, d == 1))
    def _():
        s_scr[...] = s0b_ref[...]

    df = d.astype(F32)
    sgn = 1.0 - 2.0 * df
    ii = lax.broadcasted_iota(jnp.int32, (C, C), 0).astype(F32)
    jj = lax.broadcasted_iota(jnp.int32, (C, C), 1).astype(F32)
    dd = (ii - jj) * sgn
    feeds = dd >= 0.0
    ddc = jnp.maximum(dd, 0.0)
    ri = lax.broadcasted_iota(jnp.int32, (C, 1), 0).astype(F32)
    pos_q = (ri + 1.0) + df * (C - 2.0 * ri - 1.0)
    pos_k = (C - 1.0 - ri) + df * (2.0 * ri - C + 1.0)
    chunk_len = jnp.full((1, A_DV), float(C), F32)

    outs = [[] for _ in range(G)]
    for h in range(A_H):
        lg = lg_ref[d, h]
        dmask = jnp.where(feeds, jnp.exp2(lg * ddc), 0.0)
        q_dec = jnp.exp2(lg * pos_q)
        k_dec = jnp.exp2(lg * pos_k)
        c_dec = jnp.exp2(lg * chunk_len)
        for bb in range(G):
            qh = q_ref[bb, :, h * A_DK:(h + 1) * A_DK].astype(F32) * (A_DK ** -0.5)
            kh = k_ref[bb, :, h * A_DK:(h + 1) * A_DK].astype(F32)
            vh = v_ref[bb, :, h * A_DV:(h + 1) * A_DV].astype(BF16)
            s = s_scr[bb, h]
            att = _dot_nt(qh.astype(BF16), kh.astype(BF16)) * dmask
            o = _dot(att.astype(BF16), vh) + _dot((qh * q_dec).astype(BF16), s.astype(BF16))
            kd = kh * k_dec
            s_scr[bb, h] = c_dec * s + _dot(kd.T.astype(BF16), vh)
            outs[bb].append(o)
    o_all = [jnp.concatenate(o, axis=-1) for o in outs]

    @pl.when(d == 0)
    def _():
        for bb in range(G):
            of_scr[bb, c] = o_all[bb]

    @pl.when(d == 1)
    def _():
        for bb in range(G):
            tot = o_all[bb] + of_scr[bb, n - 1 - c]
            res = []
            for h in range(A_H):
                sl = slice(h * A_DV, (h + 1) * A_DV)
                t = tot[:, sl]
                dev = t - jnp.mean(t, axis=-1, keepdims=True)
                y = dev * lax.rsqrt(jnp.mean(dev * dev, axis=-1, keepdims=True) + EPS) * ng_ref[:, sl]
                res.append(y * _silu(ag_ref[bb, :, sl].astype(F32)))
            o_ref[bb] = jnp.concatenate(res, axis=-1).astype(o_ref.dtype)

    @pl.when(jnp.logical_and(c == n - 1, d == 0))
    def _():
        sf_ref[...] = s_scr[...]

    @pl.when(jnp.logical_and(c == n - 1, d == 1))
    def _():
        sb_ref[...] = s_scr[...]


def _retention(z, B, T, log_gamma, s0f, s0b, norm_g):
    C = RET_CHUNK
    G = _rows_per_step(B, T, A_H * A_DV)
    assert B % G == 0 and T % C == 0
    n = T // C
    HK = A_H * A_DK
    HV = A_H * A_DV
    z3 = z.reshape(B, T, z.shape[1])

    def chunk(d, c):
        return c + d * (n - 1 - 2 * c)

    st_spec = pl.BlockSpec((G, A_H, A_DK, A_DV), lambda b, d, c: (b, 0, 0, 0))
    st_shape = jax.ShapeDtypeStruct((B, A_H, A_DK, A_DV), F32)
    o, sf, sb = pl.pallas_call(
        functools.partial(_ret_kernel, n=n, C=C, G=G),
        grid=(B // G, 2, n),
        in_specs=[pl.BlockSpec(memory_space=pltpu.SMEM),
                  pl.BlockSpec((G, C, HK), lambda b, d, c: (b, chunk(d, c), 0)),
                  pl.BlockSpec((G, C, HK), lambda b, d, c: (b, chunk(d, c), 1)),
                  pl.BlockSpec((G, C, HV), lambda b, d, c: (b, chunk(d, c), 1)),
                  pl.BlockSpec((G, C, HV), lambda b, d, c: (b, chunk(d, c), 2)),
                  st_spec, st_spec,
                  pl.BlockSpec((1, HV), lambda b, d, c: (0, 0))],
        out_specs=[pl.BlockSpec((G, C, HV), lambda b, d, c: (b, (n - 1) - d * c, 0)),
                   st_spec, st_spec],
        out_shape=[jax.ShapeDtypeStruct((B, T, HV), BF16), st_shape, st_shape],
        scratch_shapes=[pltpu.VMEM((G, A_H, A_DK, A_DV), F32), pltpu.VMEM((G, n, C, HV), F32)],
        compiler_params=_params("arbitrary", "arbitrary", "arbitrary"),
        name="retention",
    )(log_gamma * LOG2E, z3, z3, z3, z3, s0f, s0b, norm_g.reshape(1, HV))
    return o.reshape(B * T, HV), sf, sb


def _group_sum_matrix(width, group):
    i = np.arange(width)
    return jnp.asarray((i[:, None] // group == i[None, :] // group).astype(np.float32), dtype=BF16)


def _q_pad_matrix():
    m = np.zeros((B_H * B_HD, B_H * LANES), np.float32)
    g = B_H // B_HKV
    for h in range(B_H):
        for t in range(B_HD):
            m[h * B_HD + t, h * LANES + (h // g) * B_HD + t] = 1.0
    return jnp.asarray(m, dtype=BF16)


def _rope_tables(T):
    rows = T // GRID_W
    row = np.repeat(np.arange(rows, dtype=np.float64), GRID_W)
    col = np.tile(np.arange(GRID_W, dtype=np.float64), rows)
    nq = B_HD // 4
    inv = ROPE_THETA ** (-np.arange(nq, dtype=np.float64) / nq)
    ang = np.concatenate([row[:, None] * inv, col[:, None] * inv], axis=-1)
    cos = np.repeat(np.cos(ang), 2, axis=-1)
    sin = np.repeat(np.sin(ang), 2, axis=-1)
    sign = np.tile(np.array([-1.0, 1.0]), B_HD // 2)
    reps = LANES // B_HD
    return (jnp.asarray(np.tile(cos, (1, reps)), dtype=F32),
            jnp.asarray(np.tile(sin * sign, (1, reps)), dtype=F32))


def _group_rmsnorm(x, gsum, g):
    hi, lo = _split(x * x)
    ss = _dot(hi, gsum) + _dot(lo, gsum)
    return x * lax.rsqrt(ss * (1.0 / B_HD) + EPS) * g


def _rotate_pairs(x, cos, sin_signed):
    n = x.shape[1]
    lane = lax.broadcasted_iota(jnp.int32, x.shape, 1)
    partner = jnp.where(lane % 2 == 0, pltpu.roll(x, n - 1, 1), pltpu.roll(x, 1, 1))
    reps = n // LANES
    if reps > 1:
        cos = jnp.concatenate([cos] * reps, axis=1)
        sin_signed = jnp.concatenate([sin_signed] * reps, axis=1)
    return x * cos + partner * sin_signed


def _bprep_kernel(z_ref, qg_ref, kg_ref, cos_ref, sin_ref, gq_ref, gk_ref, pad_ref,
                  qpad_ref, kn_ref, kr_ref, vb_ref, *, rope):
    nq = B_H * B_HD
    nk = B_HKV * B_HD
    qn = _group_rmsnorm(z_ref[:, 0:nq], gq_ref[...], qg_ref[...])
    kn = _group_rmsnorm(z_ref[:, nq:nq + nk], gk_ref[...], kg_ref[...])
    kn_ref[...] = kn
    if rope:
        qn = _rotate_pairs(qn, cos_ref[...], sin_ref[...])
        kn = _rotate_pairs(kn, cos_ref[...], sin_ref[...])
    kr_ref[...] = kn.astype(BF16)
    vb_ref[...] = z_ref[:, nq + nk:nq + 2 * nk].astype(BF16)
    qs = (qn * (B_HD ** -0.5 * LOG2E)).astype(BF16)
    qpad_ref[...] = _dot(qs, pad_ref[...]).astype(BF16)


def _bprep(z, T, q_g, k_g, rope):
    N = z.shape[0]
    TM = min(512, T)
    nq = B_H * B_HD
    nk = B_HKV * B_HD
    width = nq + 2 * nk
    assert z.shape[1] == width
    cos, sin = _rope_tables(T if rope else TM)
    nt = T // TM if rope else 1
    const = lambda i: (0, 0)
    return pl.pallas_call(
        functools.partial(_bprep_kernel, rope=rope),
        grid=(N // TM,),
        in_specs=[pl.BlockSpec((TM, width), lambda i: (i, 0)),
                  pl.BlockSpec((1, nq), const),
                  pl.BlockSpec((1, nk), const),
                  pl.BlockSpec((TM, LANES), lambda i: (i % nt, 0)),
                  pl.BlockSpec((TM, LANES), lambda i: (i % nt, 0)),
                  pl.BlockSpec((nq, nq), const),
                  pl.BlockSpec((nk, nk), const),
                  pl.BlockSpec((nq, B_H * LANES), const)],
        out_specs=[pl.BlockSpec((TM, B_H * LANES), lambda i: (i, 0)),
                   pl.BlockSpec((TM, nk), lambda i: (i, 0)),
                   pl.BlockSpec((TM, nk), lambda i: (i, 0)),
                   pl.BlockSpec((TM, nk), lambda i: (i, 0))],
        out_shape=[jax.ShapeDtypeStruct((N, B_H * LANES), BF16),
                   jax.ShapeDtypeStruct((N, nk), F32),
                   jax.ShapeDtypeStruct((N, nk), BF16),
                   jax.ShapeDtypeStruct((N, nk), BF16)],
        compiler_params=_params("arbitrary"),
        name="attn_prep",
    )(z, jnp.tile(q_g, B_H).reshape(1, nq), jnp.tile(k_g, B_HKV).reshape(1, nk), cos, sin,
      _group_sum_matrix(nq, B_HD), _group_sum_matrix(nk, B_HD), _q_pad_matrix())


def _lane_fold(x, op):
    acc = x[:, 0:LANES]
    for j in range(1, x.shape[1] // LANES):
        acc = op(acc, x[:, j * LANES:(j + 1) * LANES])
    return acc


def _attn_kernel(*refs, has_cache, kc, nq):
    if has_cache:
        q_ref, k_ref, v_ref, ck_ref, cv_ref, o_ref, s_scr, m_scr, mprev_scr, l_scr, acc_scr = refs
        kcc = min(kc, ck_ref.shape[0])
        ncache = ck_ref.shape[0] // kcc
    else:
        q_ref, k_ref, v_ref, o_ref, s_scr, m_scr, mprev_scr, l_scr, acc_scr = refs
        kcc, ncache = kc, 0
    i = pl.program_id(1)
    tq = q_ref.shape[0]
    nlat = k_ref.shape[0] // kc

    def score(c, kblk):
        q = jnp.concatenate([q_ref[:, h * LANES:(h + 1) * LANES] for h in range(B_H)], axis=0)
        s = _dot_nt(q, kblk)
        s_scr[c, :, 0:kblk.shape[0]] = s
        m_scr[...] = jnp.maximum(m_scr[...], _lane_fold(s, jnp.maximum))

    def weight(c, vblk):
        s = s_scr[c, :, 0:vblk.shape[0]]
        mp = mprev_scr[...]
        ps = [jnp.exp2(s[:, j * LANES:(j + 1) * LANES] - mp) for j in range(vblk.shape[0] // LANES)]
        tot = ps[0]
        for pj in ps[1:]:
            tot = tot + pj
        l_scr[...] += tot
        acc_scr[...] += _dot(jnp.concatenate(ps, axis=1).astype(BF16), vblk)

    def run(do_weight, do_score):
        def unit(c, kblk, vblk):
            if do_weight:
                weight(c, vblk())
            if do_score:
                score(c, kblk())

        for c in range(ncache):
            unit(c, lambda: ck_ref[c * kcc:(c + 1) * kcc, :].astype(BF16),
                 lambda: cv_ref[c * kcc:(c + 1) * kcc, :].astype(BF16))

        def body(c, carry):
            rows = pl.ds(pl.multiple_of(c * kc, kc), kc)
            unit(ncache + c, lambda: k_ref[rows, :], lambda: v_ref[rows, :])
            return carry
        lax.fori_loop(0, nlat, body, 0)

    @pl.when(i < nq)
    def _():
        m_scr[...] = jnp.full(m_scr.shape, -jnp.inf, F32)

    @pl.when(i > 0)
    def _():
        l_scr[...] = jnp.zeros_like(l_scr)
        acc_scr[...] = jnp.zeros_like(acc_scr)

    @pl.when(i == 0)
    def _():
        run(False, True)

    @pl.when(jnp.logical_and(i > 0, i < nq))
    def _():
        run(True, True)

    @pl.when(i == nq)
    def _():
        run(True, False)

    @pl.when(i > 0)
    def _():
        r_all = acc_scr[...] / jnp.sum(l_scr[...], axis=-1, keepdims=True)
        g = B_H // B_HKV
        lane = lax.broadcasted_iota(jnp.int32, (tq, LANES), 1)
        outs = []
        for j in range(B_H // 2):
            pair = []
            for half in range(2):
                h = 2 * j + half
                r = r_all[h * tq:(h + 1) * tq, :]
                if h // g != half:
                    r = pltpu.roll(r, B_HD, 1)
                pair.append(r)
            outs.append(jnp.where(lane < B_HD, pair[0], pair[1]))
        o_ref[...] = jnp.concatenate(outs, axis=-1).astype(o_ref.dtype)

    @pl.when(i < nq)
    def _():
        mprev_scr[...] = jnp.broadcast_to(jnp.max(m_scr[...], axis=-1, keepdims=True), mprev_scr.shape)


def _attention(qpad, kr, vb, B, T, cache_k, cache_v):
    has_cache = cache_k is not None
    TQ = Q_TILE
    nq = T // TQ
    nk = B_HKV * B_HD
    in_specs = [pl.BlockSpec((TQ, B_H * LANES), lambda b, i: (b * nq + jnp.minimum(i, nq - 1), 0)),
                pl.BlockSpec((T, nk), lambda b, i: (b, 0)),
                pl.BlockSpec((T, nk), lambda b, i: (b, 0))]
    args = [qpad, kr, vb]
    kc = min(1024, T)
    nchunks = T // kc
    if has_cache:
        P = cache_k.shape[1]
        assert P % min(kc, P) == 0
        nchunks += P // min(kc, P)
        in_specs += [pl.BlockSpec((None, P, nk), lambda b, i: (b, 0, 0))] * 2
        args += [cache_k, cache_v]
    R = B_H * TQ
    return pl.pallas_call(
        functools.partial(_attn_kernel, has_cache=has_cache, kc=kc, nq=nq),
        grid=(B, nq + 1),
        in_specs=in_specs,
        out_specs=pl.BlockSpec((TQ, B_H * B_HD), lambda b, i: (b * nq + jnp.maximum(i - 1, 0), 0)),
        out_shape=jax.ShapeDtypeStruct((B * T, B_H * B_HD), BF16),
        scratch_shapes=[pltpu.VMEM((nchunks, R, kc), F32)] + [pltpu.VMEM((R, LANES), F32)] * 4,
        compiler_params=_params("arbitrary", "arbitrary"),
        name="attention",
    )(*args)


def _proj_res_kernel(*refs, n_in):
    x_ref, gate_ref = refs[0], refs[1]
    o_refs = refs[2:2 + n_in]
    w_refs = refs[2 + n_in:2 + 2 * n_in]
    out_ref = refs[2 + 2 * n_in]
    wbf_refs = refs[3 + 2 * n_in:]

    @pl.when(pl.program_id(0) == 0)
    def _():
        for w_ref, wbf_ref in zip(w_refs, wbf_refs):
            wbf_ref[...] = w_ref[...].astype(BF16)

    acc = _dot(o_refs[0][...], wbf_refs[0][...])
    for o_ref, wbf_ref in zip(o_refs[1:], wbf_refs[1:]):
        acc = acc + _dot(o_ref[...], wbf_ref[...])
    out_ref[...] = x_ref[...] + gate_ref[...] * acc


def _proj_res(x, mod, part, acts, w, rows_per_group):
    N, D = x.shape
    TM = min(1024, rows_per_group)
    n_in = len(acts)
    widths = [a.shape[1] for a in acts]
    offs = np.cumsum([0] + widths[:-1]).tolist()
    in_specs = [pl.BlockSpec((TM, D), lambda i: (i, 0)),
                _mod_spec(part, D, TM, rows_per_group, 0)]
    in_specs += [pl.BlockSpec((TM, wd), lambda i: (i, 0)) for wd in widths]
    in_specs += [pl.BlockSpec((wd, D), functools.partial(lambda i, blk: (blk, 0), blk=off // wd))
                 for wd, off in zip(widths, offs)]
    return pl.pallas_call(
        functools.partial(_proj_res_kernel, n_in=n_in),
        grid=(N // TM,),
        in_specs=in_specs,
        out_specs=pl.BlockSpec((TM, D), lambda i: (i, 0)),
        out_shape=jax.ShapeDtypeStruct((N, D), F32),
        scratch_shapes=[pltpu.VMEM((wd, D), BF16) for wd in widths],
        compiler_params=_params("arbitrary"),
        name="proj_residual",
    )(x, mod, *acts, *([w] * n_in))


def _ffn_kernel(x_ref, g_ref, sh_ref, sc_ref, gate_ref, w1_ref, w3_ref, w2_ref, out_ref, h_scr, acc_scr, *, nf):
    f = pl.program_id(1)

    @pl.when(f == 0)
    def _():
        h_scr[...] = _norm_mod(x_ref[...], g_ref[...], sh_ref[...], sc_ref[...]).astype(BF16)
        acc_scr[...] = jnp.zeros_like(acc_scr)

    h = h_scr[...]
    a = _dot(h, w1_ref[...].astype(BF16))
    b = _dot(h, w3_ref[...].astype(BF16))
    acc_scr[...] += _dot((_silu(a) * b).astype(BF16), w2_ref[...].astype(BF16))

    @pl.when(f == nf - 1)
    def _():
        out_ref[...] = x_ref[...] + gate_ref[...] * acc_scr[...]


def _ffn(x, g, mod, w1, w3, w2, rows_per_group):
    N, D = x.shape
    FF = w1.shape[1]
    TM, TF = min(1024, rows_per_group), 256
    nf = FF // TF
    return pl.pallas_call(
        functools.partial(_ffn_kernel, nf=nf),
        grid=(N // TM, nf),
        in_specs=[pl.BlockSpec((TM, D), lambda i, f: (i, 0)),
                  pl.BlockSpec((1, D), lambda i, f: (0, 0)),
                  _mod_spec(3, D, TM, rows_per_group, 0),
                  _mod_spec(4, D, TM, rows_per_group, 0),
                  _mod_spec(5, D, TM, rows_per_group, 0),
                  pl.BlockSpec((D, TF), lambda i, f: (0, f)),
                  pl.BlockSpec((D, TF), lambda i, f: (0, f)),
                  pl.BlockSpec((TF, D), lambda i, f: (f, 0))],
        out_specs=pl.BlockSpec((TM, D), lambda i, f: (i, 0)),
        out_shape=jax.ShapeDtypeStruct((N, D), F32),
        scratch_shapes=[pltpu.VMEM((TM, D), BF16), pltpu.VMEM((TM, D), F32)],
        compiler_params=_params("arbitrary", "arbitrary"),
        name="ffn",
    )(x, g.reshape(1, D), mod, mod, mod, w1, w3, w2)


MOE_SB = 1024
MOE_SBG = 512
MOE_GG = 4
MOE_TRG = 256
MOE_TR = 2048
MOE_CG = 4


def _two_stream_specs(shape, ntp, ax=0):
    def idx_p(*g):
        return (jnp.minimum(g[ax], ntp - 1), 0)

    def idx_s(*g):
        return (jnp.maximum(g[ax] - ntp, 0), 0)
    return pl.BlockSpec(shape, idx_p), pl.BlockSpec(shape, idx_s)


def _pool_mod_spec(part, D, TM, ntp, rows_per_group):
    def idx(i, *_):
        return (jnp.where(i < ntp, 0, 1 + ((i - ntp) * TM) // rows_per_group), part, 0, 0)
    return pl.BlockSpec((None, None, 1, D), idx)


def _route_kernel(xp_ref, xs_ref, g_ref, sh_ref, sc_ref, rw_ref, tri_ref, h_ref, info_ref, infot_ref, cum_ref,
                  carry_scr, *, ntp, cap):
    i = pl.program_id(0)

    @pl.when(i == 0)
    def _():
        carry_scr[...] = jnp.zeros_like(carry_scr)

    x = jnp.where(i < ntp, xp_ref[...], xs_ref[...])
    h = _norm_mod(x, g_ref[...], sh_ref[...], sc_ref[...])
    h_ref[...] = h.astype(BF16)
    lane = lax.broadcasted_iota(jnp.int32, (x.shape[0], LANES), 1).astype(F32)
    logits = _dot3(_split(h), _split(rw_ref[...]))
    logits = jnp.where(lane < N_EXPERTS, logits, -jnp.inf)
    m1 = jnp.max(logits, axis=-1, keepdims=True)
    i1 = jnp.min(jnp.where(logits == m1, lane, float(LANES)), axis=-1, keepdims=True)
    rest = jnp.where(lane == i1, -jnp.inf, logits)
    m2 = jnp.max(rest, axis=-1, keepdims=True)
    i2 = jnp.min(jnp.where(rest == m2, lane, float(LANES)), axis=-1, keepdims=True)
    e2 = jnp.exp(m2 - m1)
    w1 = 1.0 / (1.0 + e2)
    w2 = e2 / (1.0 + e2)
    ind = jnp.where(jnp.logical_or(lane == i1, lane == i2), 1.0, 0.0)
    before = _dot(tri_ref[...], ind.astype(BF16)) + carry_scr[...]
    r1 = jnp.sum(jnp.where(lane == i1, before, 0.0), axis=-1, keepdims=True) + i1 * float(cap)
    r2 = jnp.sum(jnp.where(lane == i2, before, 0.0), axis=-1, keepdims=True) + i2 * float(cap)
    total = carry_scr[...] + jnp.sum(ind, axis=0, keepdims=True)
    carry_scr[...] = total
    for part in range(1, MOE_SB // MOE_SBG):
        cum_ref[part - 1] = before[part * MOE_SBG:part * MOE_SBG + 1, :]
    cum_ref[MOE_SB // MOE_SBG - 1] = total
    info = jnp.where(lane == 0.0, i1, jnp.where(lane == 1.0, i2, jnp.where(lane == 2.0, w1, jnp.where(
        lane == 3.0, w2, jnp.where(lane == 4.0, r1, jnp.where(lane == 5.0, r2, 0.0))))))
    info_ref[...] = info[:, 0:SUB]
    info_t = jnp.concatenate([info[r:r + LANES, :].T for r in range(0, info.shape[0], LANES)], axis=1)
    infot_ref[...] = info_t[0:SUB, :]


def _moe_route(xp, xs, g, mod, router_w, rows_per_group):
    Np, D = xp.shape
    N = Np + xs.shape[0]
    TM = MOE_SB
    ntp = Np // TM
    nt = N // TM
    rw = jnp.pad(router_w, ((0, 0), (0, LANES - router_w.shape[1])))
    tri = jnp.asarray(np.tril(np.ones((TM, TM), np.float32), -1), dtype=BF16)
    xp_spec, xs_spec = _two_stream_specs((TM, D), ntp)
    return pl.pallas_call(
        functools.partial(_route_kernel, ntp=ntp, cap=_moe_cap(N)),
        grid=(nt,),
        in_specs=[xp_spec, xs_spec,
                  pl.BlockSpec((1, D), lambda i: (0, 0)),
                  _pool_mod_spec(3, D, TM, ntp, rows_per_group),
                  _pool_mod_spec(4, D, TM, ntp, rows_per_group),
                  pl.BlockSpec((D, LANES), lambda i: (0, 0)),
                  pl.BlockSpec((TM, TM), lambda i: (0, 0))],
        out_specs=[pl.BlockSpec((TM, D), lambda i: (i, 0)),
                   pl.BlockSpec((TM, SUB), lambda i: (i, 0)),
                   pl.BlockSpec((SUB, TM), lambda i: (0, i)),
                   pl.BlockSpec((MOE_SB // MOE_SBG, 1, LANES), lambda i: (i, 0, 0))],
        out_shape=[jax.ShapeDtypeStruct((N, D), BF16),
                   jax.ShapeDtypeStruct((N, SUB), F32),
                   jax.ShapeDtypeStruct((SUB, N), F32),
                   jax.ShapeDtypeStruct((nt * (MOE_SB // MOE_SBG), 1, LANES), F32)],
        scratch_shapes=[pltpu.VMEM((1, LANES), F32)],
        compiler_params=_params("arbitrary"),
        name="moe_route",
    )(xp, xs, g.reshape(1, D), mod, mod, rw, tri)


def _hold_unused(idx, used):
    steps, slots = idx.shape
    read = jnp.arange(slots, dtype=jnp.int32)[None, :] < used[:, None]
    step = jnp.arange(steps, dtype=jnp.int32)[:, None]
    last = lax.cummax(jnp.where(read, step, -1), axis=0)
    held = jnp.take_along_axis(idx, jnp.maximum(last, 0), axis=0)
    return jnp.where(last >= 0, held, 0)


def _moe_cap(N):
    return -(-N // MOE_TR) * MOE_TR


def _moe_plan(cum, N):
    E, SB, TRG, TR = N_EXPERTS, MOE_SB, MOE_TRG, MOE_TR
    NB = N // SB
    cap = _moe_cap(N)
    tpe_g, tpe = cap // TRG, cap // TR
    RG = E * tpe_g
    used_g, used_t = 2 * N // TRG + E, 2 * N // TR + E
    PMAX = used_g + E * NB
    i32 = jnp.int32
    parts = SB // MOE_SBG
    cum_g = cum[:, 0, :E].astype(i32).T
    cum_e = cum_g[:, parts - 1::parts]
    cnt = cum_e[:, -1]

    lo = jnp.asarray((np.arange(RG) % tpe_g) * TRG, dtype=i32)
    hi = jnp.minimum(lo + TRG, jnp.repeat(cnt, tpe_g))

    GG = MOE_GG
    cum_g_t = jnp.repeat(cum_g, tpe_g, axis=0)
    first_g = jnp.sum(cum_g_t <= lo[:, None], axis=1)
    last_g = jnp.sum(cum_g_t < hi[:, None], axis=1)
    nblk_g = jnp.where(hi > lo, last_g - first_g + 1, 0)
    nst = (nblk_g + GG - 1) // GG
    gst_end = jnp.cumsum(nst)
    gtotal = gst_end[-1]
    smax_g = (used_g + E * NB * parts + (GG - 1) * used_g) // GG + 1
    jg = jnp.arange(smax_g, dtype=i32)
    g_ok = jg < gtotal
    jgc = jnp.minimum(jg, gtotal - 1)
    tq = jnp.minimum(jnp.sum(gst_end[None, :] <= jgc[:, None], axis=1), RG - 1).astype(i32)
    gg = jgc - (gst_end - nst)[tq]
    g_slots = jnp.where(g_ok, jnp.clip(nblk_g[tq] - GG * gg, 0, GG), 0)
    g_parts = _hold_unused((first_g[tq] + GG * gg)[:, None] + jnp.arange(GG, dtype=i32)[None, :], g_slots)
    g_first = jnp.logical_and(g_ok, gg == 0)
    gather_plan = (tq, g_parts.reshape(-1).astype(i32), g_slots.astype(i32), g_first.astype(i32))

    CG = MOE_CG
    c_lo = jnp.concatenate([jnp.zeros((E, 1), i32), cum_e[:, :-1]], axis=1)
    t_lo = c_lo // TRG
    n_se = jnp.where(cum_e > c_lo, (cum_e - 1) // TRG - t_lo + 1, 0)
    cn = jnp.cumsum(n_se, axis=0)
    per_blk = cn[-1]
    nsteps = (per_blk + CG - 1) // CG
    st_end = jnp.cumsum(nsteps)
    total = st_end[-1]
    SMAX = (PMAX + CG - 1) // CG + NB
    j = jnp.arange(SMAX, dtype=i32)
    step_ok = j < total
    jc = jnp.minimum(j, total - 1)
    blk = jnp.minimum(jnp.sum(st_end[None, :] <= jc[:, None], axis=1), NB - 1).astype(i32)
    grp = jc - (st_end - nsteps)[blk]
    m = CG * grp[:, None] + jnp.arange(CG, dtype=i32)[None, :]
    slot_ok = jnp.logical_and(m < per_blk[blk][:, None], step_ok[:, None])
    c_slots = jnp.sum(slot_ok, axis=1).astype(i32)
    cn_b = cn[:, blk].T
    e_of = jnp.minimum(jnp.sum(cn_b[:, None, :] <= m[:, :, None], axis=2), E - 1)
    before = jnp.take_along_axis(cn_b - n_se[:, blk].T, e_of, axis=1)
    tile = e_of * tpe_g + jnp.take_along_axis(t_lo[:, blk].T, e_of, axis=1) + m - before
    slot_q = _hold_unused(tile, c_slots)
    c_first = jnp.logical_and(step_ok, grp == 0).astype(i32)
    c_last = jnp.logical_and(step_ok, grp == nsteps[blk] - 1).astype(i32)
    combine_plan = (blk, slot_q.reshape(-1).astype(i32), c_slots, c_first, c_last)

    tiles = (cnt + TR - 1) // TR
    t_end = jnp.cumsum(tiles)
    jt = jnp.arange(used_t, dtype=i32)
    t_ok = jt < t_end[-1]
    jtc = jnp.minimum(jt, t_end[-1] - 1)
    te = jnp.minimum(jnp.sum(t_end[None, :] <= jtc[:, None], axis=1), E - 1).astype(i32)
    tk = jtc - (t_end - tiles)[te]
    tvalid = jnp.where(t_ok, jnp.clip(cnt[te] - tk * TR, 0, TR), 0)
    ffn_plan = ((te * tpe + tk).astype(i32), te, tvalid.astype(i32))
    return gather_plan, combine_plan, ffn_plan, E * cap


def _moe_gather_kernel(q_ref, s_ref, slots_ref, first_ref, *refs):
    pos_refs, h_refs, out_ref = refs[:MOE_GG], refs[MOE_GG:2 * MOE_GG], refs[2 * MOE_GG]
    p = pl.program_id(0)
    rows = out_ref.shape[0]

    @pl.when(first_ref[p] == 1)
    def _():
        out_ref[...] = jnp.zeros_like(out_ref)

    for ns in range(1, MOE_GG + 1):
        @pl.when(slots_ref[p] == ns)
        def _(ns=ns):
            row = (lax.broadcasted_iota(jnp.int32, (rows, 1), 0) + q_ref[p] * rows).astype(F32)
            sels = []
            for k in range(ns):
                hit = jnp.logical_or(pos_refs[k][4:5, :] == row, pos_refs[k][5:6, :] == row)
                sels.append(jnp.where(hit, 1.0, 0.0).astype(BF16))
            sel = sels[0] if ns == 1 else jnp.concatenate(sels, axis=1)
            hs = h_refs[0][...] if ns == 1 else jnp.concatenate([h_refs[k][...] for k in range(ns)], axis=0)
            out_ref[...] = out_ref[...] + _dot(sel, hs).astype(BF16)


def _moe_gather(h, pos_rows, plan, rmax):
    N, D = h.shape
    nsteps = plan[0].shape[0]

    def pos_spec(k):
        return pl.BlockSpec((SUB, MOE_SBG), lambda p, q, s, *_: (0, s[MOE_GG * p + k]))

    def tok_spec(k):
        return pl.BlockSpec((MOE_SBG, D), lambda p, q, s, *_: (s[MOE_GG * p + k], 0))

    return pl.pallas_call(
        _moe_gather_kernel,
        grid_spec=pltpu.PrefetchScalarGridSpec(
            num_scalar_prefetch=4, grid=(nsteps,),
            in_specs=[pos_spec(k) for k in range(MOE_GG)] + [tok_spec(k) for k in range(MOE_GG)],
            out_specs=pl.BlockSpec((MOE_TRG, D), lambda p, q, *_: (q[p], 0))),
        out_shape=jax.ShapeDtypeStruct((rmax, D), BF16),
        compiler_params=_params("arbitrary"),
        name="moe_gather",
    )(*plan, *([pos_rows] * MOE_GG), *([h] * MOE_GG))


def _moe_ffn_kernel(t_ref, e_ref, nv_ref, x_ref, w1_ref, w3_ref, w2_ref, out_ref, acc_scr, *, nf):
    t = pl.program_id(0)
    f = pl.program_id(1)
    nv = nv_ref[t]

    def block(start, size):
        rows = pl.ds(start, size)

        @pl.when(f == 0)
        def _():
            acc_scr[rows, :] = jnp.zeros((size, acc_scr.shape[1]), F32)

        x = x_ref[rows, :]
        a = _dot(x, w1_ref[...].astype(BF16))
        b = _dot(x, w3_ref[...].astype(BF16))
        acc_scr[rows, :] += _dot((_silu(a) * b).astype(BF16), w2_ref[...].astype(BF16))

        @pl.when(f == nf - 1)
        def _():
            out_ref[rows, :] = acc_scr[rows, :].astype(out_ref.dtype)

    nsub = MOE_TR // MOE_TRG
    used = (nv + MOE_TRG - 1) // MOE_TRG

    @pl.when(used == nsub)
    def _():
        block(0, MOE_TR)

    @pl.when(jnp.logical_and(used > 0, used < nsub))
    def _():
        start = jnp.int32(0)
        size = MOE_TR // 2
        while size >= MOE_TRG:
            has = (used & (size // MOE_TRG)) != 0

            @pl.when(has)
            def _(start=start, size=size):
                block(pl.multiple_of(start, MOE_TRG), size)

            start = start + jnp.where(has, size, 0)
            size //= 2


def _moe_ffn(xs, plan, w1, w3, w2):
    rmax, D = xs.shape
    FF = w1.shape[2]
    TF = 256
    nf = FF // TF
    RT = plan[0].shape[0]

    def fidx(t, f, nv):
        return jnp.where(nv[t] > 0, f, nf - 1)

    return pl.pallas_call(
        functools.partial(_moe_ffn_kernel, nf=nf),
        grid_spec=pltpu.PrefetchScalarGridSpec(
            num_scalar_prefetch=3, grid=(RT, nf),
            in_specs=[pl.BlockSpec((MOE_TR, D), lambda t, f, ti, e, nv: (ti[t], 0)),
                      pl.BlockSpec((None, D, TF), lambda t, f, ti, e, nv: (e[t], 0, fidx(t, f, nv))),
                      pl.BlockSpec((None, D, TF), lambda t, f, ti, e, nv: (e[t], 0, fidx(t, f, nv))),
                      pl.BlockSpec((None, TF, D), lambda t, f, ti, e, nv: (e[t], fidx(t, f, nv), 0))],
            out_specs=pl.BlockSpec((MOE_TR, D), lambda t, f, ti, e, nv: (ti[t], 0)),
            scratch_shapes=[pltpu.VMEM((MOE_TR, D), F32)]),
        out_shape=jax.ShapeDtypeStruct((rmax, D), BF16),
        compiler_params=_params("arbitrary", "arbitrary"),
        name="moe_ffn",
    )(*plan, xs, w1, w3, w2)


def _moe_combine_kernel(s_ref, q_ref, slots_ref, first_ref, last_ref, pos_ref, *refs, ntp):
    ys_refs = refs[:MOE_CG]
    xp_ref, xs_ref, gate_ref, fg_ref, op_ref, os_ref, acc_scr = refs[MOE_CG:]
    p = pl.program_id(0)
    rows = ys_refs[0].shape[0]

    @pl.when(first_ref[p] == 1)
    def _():
        acc_scr[...] = jnp.zeros_like(acc_scr)

    for ns in range(1, MOE_CG + 1):
        @pl.when(slots_ref[p] == ns)
        def _(ns=ns):
            sels = []
            for k in range(ns):
                col = (lax.broadcasted_iota(jnp.int32, (1, rows), 1) + q_ref[MOE_CG * p + k] * rows).astype(F32)
                sels.append((jnp.where(pos_ref[:, 4:5] == col, pos_ref[:, 2:3], 0.0)
                             + jnp.where(pos_ref[:, 5:6] == col, pos_ref[:, 3:4], 0.0)).astype(BF16))
            sel = sels[0] if ns == 1 else jnp.concatenate(sels, axis=1)
            ys = ys_refs[0][...] if ns == 1 else jnp.concatenate([ys_refs[k][...] for k in range(ns)], axis=0)
            acc_scr[...] += _dot(sel, ys)

    @pl.when(last_ref[p] == 1)
    def _():
        s = s_ref[p]
        x = jnp.where(s < ntp, xp_ref[...], xs_ref[...])
        y = x + gate_ref[...] * acc_scr[...]
        out = y * lax.rsqrt(jnp.mean(y * y, axis=-1, keepdims=True) + EPS) * fg_ref[...]

        @pl.when(s < ntp)
        def _():
            op_ref[...] = out

        @pl.when(s >= ntp)
        def _():
            os_ref[...] = out


def _moe_combine(ys, pos_cols, plan, xp, xs, mod, final_g, rows_per_group):
    Np, D = xp.shape
    Ns = xs.shape[0]
    SB = MOE_SB
    ntp = Np // SB
    nsteps = plan[0].shape[0]

    def tile_spec(k):
        return pl.BlockSpec((MOE_TRG, D), lambda p, s, q, *_: (q[MOE_CG * p + k], 0))

    def tok_p(p, s, *_):
        return (jnp.minimum(s[p], ntp - 1), 0)

    def tok_s(p, s, *_):
        return (jnp.maximum(s[p] - ntp, 0), 0)

    def gate_idx(p, s, *_):
        return (jnp.where(s[p] < ntp, 0, 1 + ((s[p] - ntp) * SB) // rows_per_group), 5, 0, 0)

    return pl.pallas_call(
        functools.partial(_moe_combine_kernel, ntp=ntp),
        grid_spec=pltpu.PrefetchScalarGridSpec(
            num_scalar_prefetch=5, grid=(nsteps,),
            in_specs=[pl.BlockSpec((SB, SUB), lambda p, s, q, *_: (s[p], 0))]
            + [tile_spec(k) for k in range(MOE_CG)]
            + [pl.BlockSpec((SB, D), tok_p),
                      pl.BlockSpec((SB, D), tok_s),
                      pl.BlockSpec((None, None, 1, D), gate_idx),
                      pl.BlockSpec((1, D), lambda p, *_: (0, 0))],
            out_specs=[pl.BlockSpec((SB, D), tok_p), pl.BlockSpec((SB, D), tok_s)],
            scratch_shapes=[pltpu.VMEM((SB, D), F32)]),
        out_shape=[jax.ShapeDtypeStruct((Np, D), F32), jax.ShapeDtypeStruct((Ns, D), F32)],
        compiler_params=_params("arbitrary"),
        name="moe_combine",
    )(*plan, pos_cols, *([ys] * MOE_CG), xp, xs, mod, final_g.reshape(1, D))


def _moe(xp, xs, g, mod, router_w, w1, w3, w2, final_g, rows_per_group):
    N = xp.shape[0] + xs.shape[0]
    h, info, info_t, cum = _moe_route(xp, xs, g, mod, router_w, rows_per_group)
    gather_plan, combine_plan, ffn_plan, rmax = _moe_plan(cum, N)
    x_sorted = _moe_gather(h, info_t, gather_plan, rmax)
    y_sorted = _moe_ffn(x_sorted, ffn_plan, w1, w3, w2)
    return _moe_combine(y_sorted, info, combine_plan, xp, xs, mod, final_g, rows_per_group)


def _gla_levels(C):
    lv, c = [], C // 2
    while c >= SUB:
        lv.append(c)
        c //= 2
    return lv


def _gla_tables(C):
    levels = _gla_levels(C)
    nr = 1 + len(levels)
    mat = np.zeros((2, nr * C, C), np.float32)
    code = np.zeros((2, C, C), np.int32)
    for d in range(2):
        p = np.arange(C) if d == 0 else C - 1 - np.arange(C)
        pi, pj = p[:, None], p[None, :]
        mat[d, 0:C] = pj <= pi
        code[d] = np.where((pj <= pi) & (pi // SUB == pj // SUB), 1, 0)
        for lv, c in enumerate(levels):
            blk = pi // c
            later = blk % 2 == 1
            mat[d, (1 + lv) * C:(2 + lv) * C] = ((later & (pj > blk * c - 1) & (pj <= pi))
                                                 | (~later & (pj > pi) & (pj <= (blk + 1) * c - 1)))
            pair = (pi // (2 * c) == pj // (2 * c)) & (pi // c != pj // c) & (pj <= pi)
            code[d] = np.where(pair, 2 + lv, code[d])
    ones = np.zeros((SUB * LANES, C), np.float32)
    for jj in range(SUB):
        ones[jj * LANES:(jj + 1) * LANES, jj::SUB] = 1.0
    return jnp.asarray(mat, dtype=BF16), jnp.asarray(code), jnp.asarray(ones, dtype=BF16)


def _bcast_sublane(x, jj):
    r, w = x.shape
    x3 = x.reshape(r // SUB, SUB, w)
    return jnp.broadcast_to(x3[:, jj:jj + 1, :], x3.shape).reshape(r, w)


def _t128(x):
    r, w = x.shape
    if w > LANES:
        return jnp.concatenate([x[:, i:i + LANES].T for i in range(0, w, LANES)], axis=0)
    return jnp.concatenate([x[i:i + LANES, :].T for i in range(0, r, LANES)], axis=1)


def _gla_kernel(q_ref, k_ref, v_ref, g_ref, lr_ref, wg_ref, ba_ref, mat_ref, code_ref, ones_ref,
                s0f_ref, s0b_ref, ng_ref, o_ref, sf_ref, sb_ref, st_scr, of_scr, *, n, C, G):
    d = pl.program_id(1)
    c = pl.program_id(2)
    levels = _gla_levels(C)

    @pl.when(jnp.logical_and(c == 0, d == 0))
    def _():
        for bb in range(G):
            for h in range(C_H):
                st_scr[bb, h] = _t128(s0f_ref[bb, h])

    @pl.when(jnp.logical_and(c == 0, d == 1))
    def _():
        for bb in range(G):
            for h in range(C_H):
                st_scr[bb, h] = _t128(s0b_ref[bb, h])

    mat = mat_ref[...]
    code = code_ref[...]
    ones = ones_ref[...]
    wg = _split(wg_ref[...])
    cums = []
    for bb in range(G):
        xg = _dot3(_split(lr_ref[bb]), wg) + ba_ref[...]
        la = (jnp.minimum(xg, 0.0) - jnp.log1p(jnp.exp(-jnp.abs(xg)))) * (LOG2E / C_TAU)
        hi, lo = _split(la)
        cums.append(_dot(mat, hi) + _dot(mat, lo))

    def prepare(bb, h):
        cum = cums[bb]
        ks = slice(h * C_DK, (h + 1) * C_DK)
        qh = q_ref[bb, :, ks].astype(F32) * (C_DK ** -0.5)
        kh = k_ref[bb, :, ks].astype(F32)
        b = cum[0:C, ks]
        b_end = jnp.min(b, axis=0, keepdims=True)
        b_rest = b_end - b
        ps = []
        for jj in range(SUB):
            dec = jnp.exp2(jnp.minimum(b - _bcast_sublane(b, jj), 0.0))
            ps.append((qh * _bcast_sublane(kh, jj) * dec).astype(BF16))
        lv_ops = []
        for lv in range(len(levels)):
            fac = jnp.exp2(cum[(1 + lv) * C:(2 + lv) * C, ks])
            lv_ops.append(((qh * fac).astype(BF16), (kh * fac).astype(BF16)))
        qe = (qh * jnp.exp2(b)).astype(BF16)
        ke = (kh * jnp.exp2(b_rest)).astype(BF16)
        e_end = jnp.exp2(b_end)
        return jnp.concatenate(ps, axis=1), lv_ops, qe, ke, e_end

    def contract(bb, h, prep):
        pcat, lv_ops, qe, ke, e_end = prep
        vh = v_ref[bb, :, h * C_DV:(h + 1) * C_DV].astype(F32)
        att = jnp.where(code == 1, _dot(pcat, ones), 0.0)
        for lv, (qs, ks_) in enumerate(lv_ops):
            att = jnp.where(code == 2 + lv, _dot_nt(qs, ks_), att)
        st = st_scr[bb, h]
        o = _dot(att.astype(BF16), vh.astype(BF16)) + _dot_nt(qe, st.astype(BF16))
        st_scr[bb, h] = e_end * st + _dot(_t128(vh).astype(BF16), ke)
        return o

    units = [(bb, h) for h in range(C_H) for bb in range(G)]
    outs = {}
    prep = prepare(*units[0])
    for idx, (bb, h) in enumerate(units):
        nxt = prepare(*units[idx + 1]) if idx + 1 < len(units) else None
        outs[(bb, h)] = contract(bb, h, prep)
        prep = nxt
    o_all = [jnp.concatenate([outs[(bb, h)] for h in range(C_H)], axis=-1) for bb in range(G)]

    @pl.when(d == 0)
    def _():
        for bb in range(G):
            of_scr[bb, c] = o_all[bb]

    @pl.when(d == 1)
    def _():
        for bb in range(G):
            tot = o_all[bb] + of_scr[bb, n - 1 - c]
            res = []
            for h in range(C_H):
                sl = slice(h * C_DV, (h + 1) * C_DV)
                t = tot[:, sl]
                y = t * lax.rsqrt(jnp.mean(t * t, axis=-1, keepdims=True) + EPS) * ng_ref[:, sl]
                res.append(y * _silu(g_ref[bb, :, sl].astype(F32)))
            o_ref[bb] = jnp.concatenate(res, axis=-1).astype(o_ref.dtype)

    @pl.when(jnp.logical_and(c == n - 1, d == 0))
    def _():
        for bb in range(G):
            for h in range(C_H):
                sf_ref[bb, h] = _t128(st_scr[bb, h])

    @pl.when(jnp.logical_and(c == n - 1, d == 1))
    def _():
        for bb in range(G):
            for h in range(C_H):
                sb_ref[bb, h] = _t128(st_scr[bb, h])


def _gla(z, zg, B, T, w_a2, b_a, s0f, s0b, norm_g):
    C = GLA_CHUNK
    G = _rows_per_step(B, T, C_H * C_DV)
    assert B % G == 0 and T % C == 0
    n = T // C
    HK = C_H * C_DK
    HV = C_H * C_DV
    mat, code, ones = _gla_tables(C)
    nr = mat.shape[1] // C
    wg = jnp.zeros((2, LANES, HK), F32)
    for dr in range(2):
        wg = wg.at[dr, dr * C_RANK:(dr + 1) * C_RANK, :].set(w_a2[dr])
    z3 = z.reshape(B, T, z.shape[1])
    zg3 = zg.reshape(B, T, zg.shape[1])

    def chunk(d, c):
        return c + d * (n - 1 - 2 * c)

    st_spec = pl.BlockSpec((G, C_H, C_DK, C_DV), lambda b, d, c: (b, 0, 0, 0))
    st_shape = jax.ShapeDtypeStruct((B, C_H, C_DK, C_DV), F32)
    o, sf, sb = pl.pallas_call(
        functools.partial(_gla_kernel, n=n, C=C, G=G),
        grid=(B // G, 2, n),
        in_specs=[pl.BlockSpec((G, C, HK), lambda b, d, c: (b, chunk(d, c), 0)),
                  pl.BlockSpec((G, C, HK), lambda b, d, c: (b, chunk(d, c), 1)),
                  pl.BlockSpec((G, C, HV), lambda b, d, c: (b, chunk(d, c), 1)),
                  pl.BlockSpec((G, C, HV), lambda b, d, c: (b, chunk(d, c), 2)),
                  pl.BlockSpec((G, C, LANES), lambda b, d, c: (b, chunk(d, c), 0)),
                  pl.BlockSpec((None, LANES, HK), lambda b, d, c: (d, 0, 0)),
                  pl.BlockSpec((None, 1, HK), lambda b, d, c: (d, 0, 0)),
                  pl.BlockSpec((None, nr * C, C), lambda b, d, c: (d, 0, 0)),
                  pl.BlockSpec((None, C, C), lambda b, d, c: (d, 0, 0)),
                  pl.BlockSpec((SUB * LANES, C), lambda b, d, c: (0, 0)),
                  st_spec, st_spec,
                  pl.BlockSpec((1, HV), lambda b, d, c: (0, 0))],
        out_specs=[pl.BlockSpec((G, C, HV), lambda b, d, c: (b, (n - 1) - d * c, 0)),
                   st_spec, st_spec],
        out_shape=[jax.ShapeDtypeStruct((B, T, HV), BF16), st_shape, st_shape],
        scratch_shapes=[pltpu.VMEM((G, C_H, C_DV, C_DK), F32), pltpu.VMEM((G, n, C, HV), F32)],
        compiler_params=_params("arbitrary", "arbitrary", "arbitrary"),
        name="gla",
    )(z3, z3, z3, z3, zg3, wg, b_a.reshape(2, 1, HK), mat, code, ones, s0f, s0b, norm_g.reshape(1, HV))
    return o.reshape(B * T, HV), sf, sb


def _run_stream(x, B, T, mods, ctx, p):
    N, D = x.shape
    rpg = N // mods[0].shape[0]
    TM = min(2048, rpg)
    nb = (B_H + 2 * B_HKV) * B_HD

    w_in = p['even_w_in'][0]
    z, zb = _norm_mm(x, p['norm1_g'][0], mods[0], (0, 1), w_in, w_in, (nb, MIX_MAIN // nb), TM, rpg)
    if ctx is None:
        s0 = jnp.zeros((B, A_H, A_DK, A_DV), F32)
        a_f0, a_b0, cache_k, cache_v = s0, s0, None, None
    else:
        cache_k, cache_v, a_f0, a_b0 = ctx[0], ctx[1], ctx[2], ctx[3]
    o_a, a_sf, a_sb = _retention(z, B, T, p['a_log_gamma'][0], a_f0, a_b0, p['a_norm_g'][0])
    qpad, k_norm, k_rot, v_bf = _bprep(zb, T, p['b_q_g'][0], p['b_k_g'][0], rope=ctx is not None)
    o_b = _attention(qpad, k_rot, v_bf, B, T, cache_k, cache_v)
    x = _proj_res(x, mods[0], 2, [o_a, o_b], p['even_w_out'][0], rpg)
    x = _ffn(x, p['norm2_g'][0], mods[0], p['ff_w1'][0], p['ff_w3'][0], p['ff_w2'][0], rpg)

    w_in = p['odd_w_in'][0]
    w_gate = jnp.pad(w_in[:, MIX_MAIN:], ((0, 0), (0, LANES - 2 * C_RANK)))
    z1, z1g = _norm_mm(x, p['norm1_g'][1], mods[1], (0, 1), w_in, w_gate, (LANES, 0), TM, rpg)
    if ctx is None:
        s0 = jnp.zeros((B, C_H, C_DK, C_DV), F32)
        c_f0, c_b0 = s0, s0
    else:
        c_f0, c_b0 = ctx[4], ctx[5]
    o_c, c_sf, c_sb = _gla(z1, z1g, B, T, p['c_w_a2'][0], p['c_b_a'][0], c_f0, c_b0, p['c_norm_g'][0])
    x = _proj_res(x, mods[1], 2, [o_c], p['odd_w_out'][0], rpg)
    v_raw = zb[:, (B_H + B_HKV) * B_HD:]
    return x, (k_norm, v_raw, a_sf, a_sb, c_sf, c_sb)


def kernel(x_prompt, x_sample, c, cache_b_k, cache_b_v, state_a_fwd, state_a_bwd, state_c_fwd, state_c_bwd,
           c_ctx, w_mod, b_mod, norm1_g, norm2_g, final_g, even_w_in, even_w_out, a_log_gamma, a_norm_g,
           b_q_g, b_k_g, odd_w_in, c_w_a2, c_b_a, c_norm_g, odd_w_out, ff_w1, ff_w3, ff_w2,
           router_w, moe_w1, moe_w3, moe_w2):
    Bp, Tp, D = x_prompt.shape
    Bs, Ts, _ = x_sample.shape
    L = w_mod.shape[0]
    assert L == 2 and even_w_in.shape[0] == 1 and odd_w_in.shape[0] == 1
    p = dict(norm1_g=norm1_g, norm2_g=norm2_g, final_g=final_g, even_w_in=even_w_in, even_w_out=even_w_out,
             a_log_gamma=a_log_gamma, a_norm_g=a_norm_g, b_q_g=b_q_g, b_k_g=b_k_g, odd_w_in=odd_w_in,
             c_w_a2=c_w_a2, c_b_a=c_b_a, c_norm_g=c_norm_g, odd_w_out=odd_w_out, ff_w1=ff_w1, ff_w3=ff_w3,
             ff_w2=ff_w2, router_w=router_w, moe_w1=moe_w1, moe_w3=moe_w3, moe_w2=moe_w2)

    rows = 8
    conds = jnp.concatenate([c_ctx[None, :], c, jnp.zeros((rows - 1 - Bs, D), F32)], axis=0)
    mod = _modulation(conds, w_mod, b_mod).reshape(L, rows, 6, 1, D)
    mods_p = [mod[l, 0:1] for l in range(L)]
    mods_s = [mod[l, 1:1 + Bs] for l in range(L)]

    x_p, kept = _run_stream(x_prompt.reshape(Bp * Tp, D), Bp, Tp, mods_p, None, p)
    nk = B_HKV * B_HD
    ctx = (cache_b_k[:, 0].reshape(Bs, -1, nk), cache_b_v[:, 0].reshape(Bs, -1, nk),
           state_a_fwd[:, 0], state_a_bwd[:, 0], state_c_fwd[:, 0], state_c_bwd[:, 0])
    x_s, _ = _run_stream(x_sample.reshape(Bs * Ts, D), Bs, Ts, mods_s, ctx, p)
    y_p, y_s = _moe(x_p, x_s, norm2_g[1], mod[1, 0:1 + Bs], router_w[0], moe_w1[0], moe_w3[0], moe_w2[0],
                    final_g, Ts)

    k_norm, v_raw, a_sf, a_sb, c_sf, c_sb = kept
    return (y_p.reshape(Bp, Tp, D), y_s.reshape(Bs, Ts, D),
            k_norm.reshape(Bp, 1, Tp, B_HKV, B_HD), v_raw.reshape(Bp, 1, Tp, B_HKV, B_HD),
            a_sf[:, None], a_sb[:, None], c_sf[:, None], c_sb[:, None])
```

```python
import functools

import numpy as np
import jax
import jax.numpy as jnp
from jax import lax
from jax.experimental import pallas as pl
from jax.experimental.pallas import tpu as pltpu

F32 = jnp.float32
BF16 = jnp.bfloat16
EPS = 1e-6
LOG2E = 1.4426950408889634

VMEM_LIMIT_BYTES = 56 * 1024 * 1024

A_H, A_DK, A_DV = 4, 128, 256
B_H, B_HKV, B_HD = 8, 2, 64
C_H, C_DK, C_DV, C_RANK = 4, 128, 256, 16
C_TAU = 16.0
GRID_W = 64
ROPE_THETA = 10000.0
N_EXPERTS = 8
LANES = 128
SUB = 8
RET_CHUNK = 128
GLA_CHUNK = 128
Q_TILE = 128
SCAN_ROWS_MAX = 4
SCAN_FWD_BYTES = 32 * 1024 * 1024


def _rows_per_step(B, T, width):
    g = SCAN_ROWS_MAX
    while g > 1 and (B % g or g * T * width * 4 > SCAN_FWD_BYTES):
        g //= 2
    return g


def _params(*sem):
    return pltpu.CompilerParams(dimension_semantics=sem, vmem_limit_bytes=VMEM_LIMIT_BYTES)


def _dot(a, b):
    return jnp.dot(a, b, preferred_element_type=F32)


def _dot_nt(a, b):
    return lax.dot_general(a, b, (((1,), (1,)), ((), ())), preferred_element_type=F32)


def _split(x):
    hi = x.astype(BF16)
    return hi, (x - hi.astype(F32)).astype(BF16)


def _dot3(a, b):
    return _dot(a[0], b[0]) + (_dot(a[0], b[1]) + _dot(a[1], b[0]))


def _silu(x):
    return x * jax.nn.sigmoid(x)


def _norm_mod(x, g, sh, sc):
    r = lax.rsqrt(jnp.mean(x * x, axis=-1, keepdims=True) + EPS)
    return (x * r * g) * (1.0 + sc) + sh


def _mod_kernel(c_ref, w_ref, b_ref, o_ref):
    o_ref[...] = _dot3(_split(_silu(c_ref[...])), _split(w_ref[...])) + b_ref[...]


def _modulation(conds, w_mod, b_mod):
    L, D, D6 = w_mod.shape
    R = conds.shape[0]
    TN = 1024
    return pl.pallas_call(
        _mod_kernel,
        grid=(L, D6 // TN),
        in_specs=[pl.BlockSpec((R, D), lambda l, j: (0, 0)),
                  pl.BlockSpec((None, D, TN), lambda l, j: (l, 0, j)),
                  pl.BlockSpec((None, 1, TN), lambda l, j: (l, 0, j))],
        out_specs=pl.BlockSpec((None, R, TN), lambda l, j: (l, 0, j)),
        out_shape=jax.ShapeDtypeStruct((L, R, D6), F32),
        compiler_params=_params("arbitrary", "arbitrary"),
        name="modulation",
    )(conds, w_mod, b_mod.reshape(L, 1, D6))


def _mod_spec(part, D, TM, rows_per_group, axis):
    def idx(*g):
        return ((g[axis] * TM) // rows_per_group, part, 0, 0)
    return pl.BlockSpec((None, None, 1, D), idx)


MIX_MAIN = A_H * (2 * A_DK + 2 * A_DV)
MIX_TN = 768


def _norm_mm_kernel(x_ref, g_ref, sh_ref, sc_ref, w_ref, we_ref, o_ref, oe_ref, h_scr, *, nmain):
    j = pl.program_id(1)

    @pl.when(j == 0)
    def _():
        h_scr[...] = _norm_mod(x_ref[...], g_ref[...], sh_ref[...], sc_ref[...]).astype(BF16)

    @pl.when(j < nmain)
    def _():
        o_ref[...] = _dot(h_scr[...], w_ref[...].astype(BF16)).astype(o_ref.dtype)

    @pl.when(j == nmain)
    def _():
        oe_ref[...] = _dot(h_scr[...], we_ref[...].astype(BF16))


def _norm_mm(x, g, mod, parts, w, w_extra, extra_block, TM, rows_per_group):
    N, D = x.shape
    nmain = MIX_MAIN // MIX_TN
    WE = extra_block[0]
    return pl.pallas_call(
        functools.partial(_norm_mm_kernel, nmain=nmain),
        grid=(N // TM, nmain + 1),
        in_specs=[pl.BlockSpec((TM, D), lambda i, j: (i, 0)),
                  pl.BlockSpec((1, D), lambda i, j: (0, 0)),
                  _mod_spec(parts[0], D, TM, rows_per_group, 0),
                  _mod_spec(parts[1], D, TM, rows_per_group, 0),
                  pl.BlockSpec((D, MIX_TN), lambda i, j: (0, jnp.minimum(j, nmain - 1))),
                  pl.BlockSpec((D, WE), lambda i, j: (0, extra_block[1]))],
        out_specs=[pl.BlockSpec((TM, MIX_TN), lambda i, j: (i, jnp.minimum(j, nmain - 1))),
                   pl.BlockSpec((TM, WE), lambda i, j: (i, 0))],
        out_shape=[jax.ShapeDtypeStruct((N, MIX_MAIN), BF16), jax.ShapeDtypeStruct((N, WE), F32)],
        scratch_shapes=[pltpu.VMEM((TM, D), BF16)],
        compiler_params=_params("arbitrary", "arbitrary"),
        name="norm_mm",
    )(x, g.reshape(1, D), mod, mod, w, w_extra)


def _ret_kernel(lg_ref, q_ref, k_ref, v_ref, ag_ref, s0f_ref, s0b_ref, ng_ref,
                o_ref, sf_ref, sb_ref, s_scr, of_scr, *, n, C, G):
    d = pl.program_id(1)
    c = pl.program_id(2)

    @pl.when(jnp.logical_and(c == 0, d == 0))
    def _():
        s_scr[...] = s0f_ref[...]

    @pl.when(jnp.logical_and(c == 0, d == 1))
    def _():
        s_scr[...] = s0b_ref[...]

    df = d.astype(F32)
    sgn = 1.0 - 2.0 * df
    ii = lax.broadcasted_iota(jnp.int32, (C, C), 0).astype(F32)
    jj = lax.broadcasted_iota(jnp.int32, (C, C), 1).astype(F32)
    dd = (ii - jj) * sgn
    feeds = dd >= 0.0
    ddc = jnp.maximum(dd, 0.0)
    ri = lax.broadcasted_iota(jnp.int32, (C, 1), 0).astype(F32)
    pos_q = (ri + 1.0) + df * (C - 2.0 * ri - 1.0)
    pos_k = (C - 1.0 - ri) + df * (2.0 * ri - C + 1.0)
    chunk_len = jnp.full((1, A_DV), float(C), F32)

    outs = [[] for _ in range(G)]
    for h in range(A_H):
        lg = lg_ref[d, h]
        dmask = jnp.where(feeds, jnp.exp2(lg * ddc), 0.0)
        q_dec = jnp.exp2(lg * pos_q)
        k_dec = jnp.exp2(lg * pos_k)
        c_dec = jnp.exp2(lg * chunk_len)
        for bb in range(G):
            qh = q_ref[bb, :, h * A_DK:(h + 1) * A_DK].astype(F32) * (A_DK ** -0.5)
            kh = k_ref[bb, :, h * A_DK:(h + 1) * A_DK].astype(F32)
            vh = v_ref[bb, :, h * A_DV:(h + 1) * A_DV].astype(BF16)
            s = s_scr[bb, h]
            att = _dot_nt(qh.astype(BF16), kh.astype(BF16)) * dmask
            o = _dot(att.astype(BF16), vh) + _dot((qh * q_dec).astype(BF16), s.astype(BF16))
            kd = kh * k_dec
            s_scr[bb, h] = c_dec * s + _dot(kd.T.astype(BF16), vh)
            outs[bb].append(o)
    o_all = [jnp.concatenate(o, axis=-1) for o in outs]

    @pl.when(d == 0)
    def _():
        for bb in range(G):
            of_scr[bb, c] = o_all[bb]

    @pl.when(d == 1)
    def _():
        for bb in range(G):
            tot = o_all[bb] + of_scr[bb, n - 1 - c]
            res = []
            for h in range(A_H):
                sl = slice(h * A_DV, (h + 1) * A_DV)
                t = tot[:, sl]
                dev = t - jnp.mean(t, axis=-1, keepdims=True)
                y = dev * lax.rsqrt(jnp.mean(dev * dev, axis=-1, keepdims=True) + EPS) * ng_ref[:, sl]
                res.append(y * _silu(ag_ref[bb, :, sl].astype(F32)))
            o_ref[bb] = jnp.concatenate(res, axis=-1).astype(o_ref.dtype)

    @pl.when(jnp.logical_and(c == n - 1, d == 0))
    def _():
        sf_ref[...] = s_scr[...]

    @pl.when(jnp.logical_and(c == n - 1, d == 1))
    def _():
        sb_ref[...] = s_scr[...]


def _retention(z, B, T, log_gamma, s0f, s0b, norm_g):
    C = RET_CHUNK
    G = _rows_per_step(B, T, A_H * A_DV)
    assert B % G == 0 and T % C == 0
    n = T // C
    HK = A_H * A_DK
    HV = A_H * A_DV
    z3 = z.reshape(B, T, z.shape[1])

    def chunk(d, c):
        return c + d * (n - 1 - 2 * c)

    st_spec = pl.BlockSpec((G, A_H, A_DK, A_DV), lambda b, d, c: (b, 0, 0, 0))
    st_shape = jax.ShapeDtypeStruct((B, A_H, A_DK, A_DV), F32)
    o, sf, sb = pl.pallas_call(
        functools.partial(_ret_kernel, n=n, C=C, G=G),
        grid=(B // G, 2, n),
        in_specs=[pl.BlockSpec(memory_space=pltpu.SMEM),
                  pl.BlockSpec((G, C, HK), lambda b, d, c: (b, chunk(d, c), 0)),
                  pl.BlockSpec((G, C, HK), lambda b, d, c: (b, chunk(d, c), 1)),
                  pl.BlockSpec((G, C, HV), lambda b, d, c: (b, chunk(d, c), 1)),
                  pl.BlockSpec((G, C, HV), lambda b, d, c: (b, chunk(d, c), 2)),
                  st_spec, st_spec,
                  pl.BlockSpec((1, HV), lambda b, d, c: (0, 0))],
        out_specs=[pl.BlockSpec((G, C, HV), lambda b, d, c: (b, (n - 1) - d * c, 0)),
                   st_spec, st_spec],
        out_shape=[jax.ShapeDtypeStruct((B, T, HV), BF16), st_shape, st_shape],
        scratch_shapes=[pltpu.VMEM((G, A_H, A_DK, A_DV), F32), pltpu.VMEM((G, n, C, HV), F32)],
        compiler_params=_params("arbitrary", "arbitrary", "arbitrary"),
        name="retention",
    )(log_gamma * LOG2E, z3, z3, z3, z3, s0f, s0b, norm_g.reshape(1, HV))
    return o.reshape(B * T, HV), sf, sb


def _group_sum_matrix(width, group):
    i = np.arange(width)
    return jnp.asarray((i[:, None] // group == i[None, :] // group).astype(np.float32), dtype=BF16)


def _q_pad_matrix():
    m = np.zeros((B_H * B_HD, B_H * LANES), np.float32)
    g = B_H // B_HKV
    for h in range(B_H):
        for t in range(B_HD):
            m[h * B_HD + t, h * LANES + (h // g) * B_HD + t] = 1.0
    return jnp.asarray(m, dtype=BF16)


def _rope_tables(T):
    rows = T // GRID_W
    row = np.repeat(np.arange(rows, dtype=np.float64), GRID_W)
    col = np.tile(np.arange(GRID_W, dtype=np.float64), rows)
    nq = B_HD // 4
    inv = ROPE_THETA ** (-np.arange(nq, dtype=np.float64) / nq)
    ang = np.concatenate([row[:, None] * inv, col[:, None] * inv], axis=-1)
    cos = np.repeat(np.cos(ang), 2, axis=-1)
    sin = np.repeat(np.sin(ang), 2, axis=-1)
    sign = np.tile(np.array([-1.0, 1.0]), B_HD // 2)
    reps = LANES // B_HD
    return (jnp.asarray(np.tile(cos, (1, reps)), dtype=F32),
            jnp.asarray(np.tile(sin * sign, (1, reps)), dtype=F32))


def _group_rmsnorm(x, gsum, g):
    hi, lo = _split(x * x)
    ss = _dot(hi, gsum) + _dot(lo, gsum)
    return x * lax.rsqrt(ss * (1.0 / B_HD) + EPS) * g


def _rotate_pairs(x, cos, sin_signed):
    n = x.shape[1]
    lane = lax.broadcasted_iota(jnp.int32, x.shape, 1)
    partner = jnp.where(lane % 2 == 0, pltpu.roll(x, n - 1, 1), pltpu.roll(x, 1, 1))
    reps = n // LANES
    if reps > 1:
        cos = jnp.concatenate([cos] * reps, axis=1)
        sin_signed = jnp.concatenate([sin_signed] * reps, axis=1)
    return x * cos + partner * sin_signed


def _bprep_kernel(z_ref, qg_ref, kg_ref, cos_ref, sin_ref, gq_ref, gk_ref, pad_ref,
                  qpad_ref, kn_ref, kr_ref, vb_ref, *, rope):
    nq = B_H * B_HD
    nk = B_HKV * B_HD
    qn = _group_rmsnorm(z_ref[:, 0:nq], gq_ref[...], qg_ref[...])
    kn = _group_rmsnorm(z_ref[:, nq:nq + nk], gk_ref[...], kg_ref[...])
    kn_ref[...] = kn
    if rope:
        qn = _rotate_pairs(qn, cos_ref[...], sin_ref[...])
        kn = _rotate_pairs(kn, cos_ref[...], sin_ref[...])
    kr_ref[...] = kn.astype(BF16)
    vb_ref[...] = z_ref[:, nq + nk:nq + 2 * nk].astype(BF16)
    qs = (qn * (B_HD ** -0.5 * LOG2E)).astype(BF16)
    qpad_ref[...] = _dot(qs, pad_ref[...]).astype(BF16)


def _bprep(z, T, q_g, k_g, rope):
    N = z.shape[0]
    TM = min(512, T)
    nq = B_H * B_HD
    nk = B_HKV * B_HD
    width = nq + 2 * nk
    assert z.shape[1] == width
    cos, sin = _rope_tables(T if rope else TM)
    nt = T // TM if rope else 1
    const = lambda i: (0, 0)
    return pl.pallas_call(
        functools.partial(_bprep_kernel, rope=rope),
        grid=(N // TM,),
        in_specs=[pl.BlockSpec((TM, width), lambda i: (i, 0)),
                  pl.BlockSpec((1, nq), const),
                  pl.BlockSpec((1, nk), const),
                  pl.BlockSpec((TM, LANES), lambda i: (i % nt, 0)),
                  pl.BlockSpec((TM, LANES), lambda i: (i % nt, 0)),
                  pl.BlockSpec((nq, nq), const),
                  pl.BlockSpec((nk, nk), const),
                  pl.BlockSpec((nq, B_H * LANES), const)],
        out_specs=[pl.BlockSpec((TM, B_H * LANES), lambda i: (i, 0)),
                   pl.BlockSpec((TM, nk), lambda i: (i, 0)),
                   pl.BlockSpec((TM, nk), lambda i: (i, 0)),
                   pl.BlockSpec((TM, nk), lambda i: (i, 0))],
        out_shape=[jax.ShapeDtypeStruct((N, B_H * LANES), BF16),
                   jax.ShapeDtypeStruct((N, nk), F32),
                   jax.ShapeDtypeStruct((N, nk), BF16),
                   jax.ShapeDtypeStruct((N, nk), BF16)],
        compiler_params=_params("arbitrary"),
        name="attn_prep",
    )(z, jnp.tile(q_g, B_H).reshape(1, nq), jnp.tile(k_g, B_HKV).reshape(1, nk), cos, sin,
      _group_sum_matrix(nq, B_HD), _group_sum_matrix(nk, B_HD), _q_pad_matrix())


def _lane_fold(x, op):
    acc = x[:, 0:LANES]
    for j in range(1, x.shape[1] // LANES):
        acc = op(acc, x[:, j * LANES:(j + 1) * LANES])
    return acc


def _attn_kernel(*refs, has_cache, kc, nq):
    if has_cache:
        q_ref, k_ref, v_ref, ck_ref, cv_ref, o_ref, s_scr, m_scr, mprev_scr, l_scr, acc_scr = refs
        kcc = min(kc, ck_ref.shape[0])
        ncache = ck_ref.shape[0] // kcc
    else:
        q_ref, k_ref, v_ref, o_ref, s_scr, m_scr, mprev_scr, l_scr, acc_scr = refs
        kcc, ncache = kc, 0
    i = pl.program_id(1)
    tq = q_ref.shape[0]
    nlat = k_ref.shape[0] // kc

    def score(c, kblk):
        q = jnp.concatenate([q_ref[:, h * LANES:(h + 1) * LANES] for h in range(B_H)], axis=0)
        s = _dot_nt(q, kblk)
        s_scr[c, :, 0:kblk.shape[0]] = s
        m_scr[...] = jnp.maximum(m_scr[...], _lane_fold(s, jnp.maximum))

    def weight(c, vblk):
        s = s_scr[c, :, 0:vblk.shape[0]]
        mp = mprev_scr[...]
        ps = [jnp.exp2(s[:, j * LANES:(j + 1) * LANES] - mp) for j in range(vblk.shape[0] // LANES)]
        tot = ps[0]
        for pj in ps[1:]:
            tot = tot + pj
        l_scr[...] += tot
        acc_scr[...] += _dot(jnp.concatenate(ps, axis=1).astype(BF16), vblk)

    def run(do_weight, do_score):
        def unit(c, kblk, vblk):
            if do_weight:
                weight(c, vblk())
            if do_score:
                score(c, kblk())

        for c in range(ncache):
            unit(c, lambda: ck_ref[c * kcc:(c + 1) * kcc, :].astype(BF16),
                 lambda: cv_ref[c * kcc:(c + 1) * kcc, :].astype(BF16))

        def body(c, carry):
            rows = pl.ds(pl.multiple_of(c * kc, kc), kc)
            unit(ncache + c, lambda: k_ref[rows, :], lambda: v_ref[rows, :])
            return carry
        lax.fori_loop(0, nlat, body, 0)

    @pl.when(i < nq)
    def _():
        m_scr[...] = jnp.full(m_scr.shape, -jnp.inf, F32)

    @pl.when(i > 0)
    def _():
        l_scr[...] = jnp.zeros_like(l_scr)
        acc_scr[...] = jnp.zeros_like(acc_scr)

    @pl.when(i == 0)
    def _():
        run(False, True)

    @pl.when(jnp.logical_and(i > 0, i < nq))
    def _():
        run(True, True)

    @pl.when(i == nq)
    def _():
        run(True, False)

    @pl.when(i > 0)
    def _():
        r_all = acc_scr[...] / jnp.sum(l_scr[...], axis=-1, keepdims=True)
        g = B_H // B_HKV
        lane = lax.broadcasted_iota(jnp.int32, (tq, LANES), 1)
        outs = []
        for j in range(B_H // 2):
            pair = []
            for half in range(2):
                h = 2 * j + half
                r = r_all[h * tq:(h + 1) * tq, :]
                if h // g != half:
                    r = pltpu.roll(r, B_HD, 1)
                pair.append(r)
            outs.append(jnp.where(lane < B_HD, pair[0], pair[1]))
        o_ref[...] = jnp.concatenate(outs, axis=-1).astype(o_ref.dtype)

    @pl.when(i < nq)
    def _():
        mprev_scr[...] = jnp.broadcast_to(jnp.max(m_scr[...], axis=-1, keepdims=True), mprev_scr.shape)


def _attention(qpad, kr, vb, B, T, cache_k, cache_v):
    has_cache = cache_k is not None
    TQ = Q_TILE
    nq = T // TQ
    nk = B_HKV * B_HD
    in_specs = [pl.BlockSpec((TQ, B_H * LANES), lambda b, i: (b * nq + jnp.minimum(i, nq - 1), 0)),
                pl.BlockSpec((T, nk), lambda b, i: (b, 0)),
                pl.BlockSpec((T, nk), lambda b, i: (b, 0))]
    args = [qpad, kr, vb]
    kc = min(2048, T)
    nchunks = T // kc
    if has_cache:
        P = cache_k.shape[1]
        assert P % min(kc, P) == 0
        nchunks += P // min(kc, P)
        in_specs += [pl.BlockSpec((None, P, nk), lambda b, i: (b, 0, 0))] * 2
        args += [cache_k, cache_v]
    R = B_H * TQ
    return pl.pallas_call(
        functools.partial(_attn_kernel, has_cache=has_cache, kc=kc, nq=nq),
        grid=(B, nq + 1),
        in_specs=in_specs,
        out_specs=pl.BlockSpec((TQ, B_H * B_HD), lambda b, i: (b * nq + jnp.maximum(i - 1, 0), 0)),
        out_shape=jax.ShapeDtypeStruct((B * T, B_H * B_HD), BF16),
        scratch_shapes=[pltpu.VMEM((nchunks, R, kc), F32)] + [pltpu.VMEM((R, LANES), F32)] * 4,
        compiler_params=_params("arbitrary", "arbitrary"),
        name="attention",
    )(*args)


def _proj_res_kernel(*refs, n_in):
    x_ref, gate_ref = refs[0], refs[1]
    o_refs = refs[2:2 + n_in]
    w_refs = refs[2 + n_in:2 + 2 * n_in]
    out_ref = refs[2 + 2 * n_in]
    wbf_refs = refs[3 + 2 * n_in:]

    @pl.when(pl.program_id(0) == 0)
    def _():
        for w_ref, wbf_ref in zip(w_refs, wbf_refs):
            wbf_ref[...] = w_ref[...].astype(BF16)

    acc = _dot(o_refs[0][...], wbf_refs[0][...])
    for o_ref, wbf_ref in zip(o_refs[1:], wbf_refs[1:]):
        acc = acc + _dot(o_ref[...], wbf_ref[...])
    out_ref[...] = x_ref[...] + gate_ref[...] * acc


def _proj_res(x, mod, part, acts, w, rows_per_group):
    N, D = x.shape
    TM = min(1024, rows_per_group)
    n_in = len(acts)
    widths = [a.shape[1] for a in acts]
    offs = np.cumsum([0] + widths[:-1]).tolist()
    in_specs = [pl.BlockSpec((TM, D), lambda i: (i, 0)),
                _mod_spec(part, D, TM, rows_per_group, 0)]
    in_specs += [pl.BlockSpec((TM, wd), lambda i: (i, 0)) for wd in widths]
    in_specs += [pl.BlockSpec((wd, D), functools.partial(lambda i, blk: (blk, 0), blk=off // wd))
                 for wd, off in zip(widths, offs)]
    return pl.pallas_call(
        functools.partial(_proj_res_kernel, n_in=n_in),
        grid=(N // TM,),
        in_specs=in_specs,
        out_specs=pl.BlockSpec((TM, D), lambda i: (i, 0)),
        out_shape=jax.ShapeDtypeStruct((N, D), F32),
        scratch_shapes=[pltpu.VMEM((wd, D), BF16) for wd in widths],
        compiler_params=_params("arbitrary"),
        name="proj_residual",
    )(x, mod, *acts, *([w] * n_in))


def _ffn_kernel(x_ref, g_ref, sh_ref, sc_ref, gate_ref, w1_ref, w3_ref, w2_ref, out_ref, h_scr, acc_scr, *, nf):
    f = pl.program_id(1)

    @pl.when(f == 0)
    def _():
        h_scr[...] = _norm_mod(x_ref[...], g_ref[...], sh_ref[...], sc_ref[...]).astype(BF16)
        acc_scr[...] = jnp.zeros_like(acc_scr)

    h = h_scr[...]
    a = _dot(h, w1_ref[...].astype(BF16))
    b = _dot(h, w3_ref[...].astype(BF16))
    acc_scr[...] += _dot((_silu(a) * b).astype(BF16), w2_ref[...].astype(BF16))

    @pl.when(f == nf - 1)
    def _():
        out_ref[...] = x_ref[...] + gate_ref[...] * acc_scr[...]


def _ffn(x, g, mod, w1, w3, w2, rows_per_group):
    N, D = x.shape
    FF = w1.shape[1]
    TM, TF = min(1024, rows_per_group), 256
    nf = FF // TF
    return pl.pallas_call(
        functools.partial(_ffn_kernel, nf=nf),
        grid=(N // TM, nf),
        in_specs=[pl.BlockSpec((TM, D), lambda i, f: (i, 0)),
                  pl.BlockSpec((1, D), lambda i, f: (0, 0)),
                  _mod_spec(3, D, TM, rows_per_group, 0),
                  _mod_spec(4, D, TM, rows_per_group, 0),
                  _mod_spec(5, D, TM, rows_per_group, 0),
                  pl.BlockSpec((D, TF), lambda i, f: (0, f)),
                  pl.BlockSpec((D, TF), lambda i, f: (0, f)),
                  pl.BlockSpec((TF, D), lambda i, f: (f, 0))],
        out_specs=pl.BlockSpec((TM, D), lambda i, f: (i, 0)),
        out_shape=jax.ShapeDtypeStruct((N, D), F32),
        scratch_shapes=[pltpu.VMEM((TM, D), BF16), pltpu.VMEM((TM, D), F32)],
        compiler_params=_params("arbitrary", "arbitrary"),
        name="ffn",
    )(x, g.reshape(1, D), mod, mod, mod, w1, w3, w2)


MOE_SB = 1024
MOE_SBG = 512
MOE_GG = 4
MOE_TRG = 256
MOE_TR = 2048
MOE_CG = 4


def _two_stream_specs(shape, ntp, ax=0):
    def idx_p(*g):
        return (jnp.minimum(g[ax], ntp - 1), 0)

    def idx_s(*g):
        return (jnp.maximum(g[ax] - ntp, 0), 0)
    return pl.BlockSpec(shape, idx_p), pl.BlockSpec(shape, idx_s)


def _pool_mod_spec(part, D, TM, ntp, rows_per_group):
    def idx(i, *_):
        return (jnp.where(i < ntp, 0, 1 + ((i - ntp) * TM) // rows_per_group), part, 0, 0)
    return pl.BlockSpec((None, None, 1, D), idx)


def _route_kernel(xp_ref, xs_ref, g_ref, sh_ref, sc_ref, rw_ref, tri_ref, h_ref, info_ref, infot_ref, cum_ref,
                  carry_scr, *, ntp, cap):
    i = pl.program_id(0)

    @pl.when(i == 0)
    def _():
        carry_scr[...] = jnp.zeros_like(carry_scr)

    x = jnp.where(i < ntp, xp_ref[...], xs_ref[...])
    h = _norm_mod(x, g_ref[...], sh_ref[...], sc_ref[...])
    h_ref[...] = h.astype(BF16)
    lane = lax.broadcasted_iota(jnp.int32, (x.shape[0], LANES), 1).astype(F32)
    logits = _dot3(_split(h), _split(rw_ref[...]))
    logits = jnp.where(lane < N_EXPERTS, logits, -jnp.inf)
    m1 = jnp.max(logits, axis=-1, keepdims=True)
    i1 = jnp.min(jnp.where(logits == m1, lane, float(LANES)), axis=-1, keepdims=True)
    rest = jnp.where(lane == i1, -jnp.inf, logits)
    m2 = jnp.max(rest, axis=-1, keepdims=True)
    i2 = jnp.min(jnp.where(rest == m2, lane, float(LANES)), axis=-1, keepdims=True)
    e2 = jnp.exp(m2 - m1)
    w1 = 1.0 / (1.0 + e2)
    w2 = e2 / (1.0 + e2)
    ind = jnp.where(jnp.logical_or(lane == i1, lane == i2), 1.0, 0.0)
    before = _dot(tri_ref[...], ind.astype(BF16)) + carry_scr[...]
    r1 = jnp.sum(jnp.where(lane == i1, before, 0.0), axis=-1, keepdims=True) + i1 * float(cap)
    r2 = jnp.sum(jnp.where(lane == i2, before, 0.0), axis=-1, keepdims=True) + i2 * float(cap)
    total = carry_scr[...] + jnp.sum(ind, axis=0, keepdims=True)
    carry_scr[...] = total
    for part in range(1, MOE_SB // MOE_SBG):
        cum_ref[part - 1] = before[part * MOE_SBG:part * MOE_SBG + 1, :]
    cum_ref[MOE_SB // MOE_SBG - 1] = total
    info = jnp.where(lane == 0.0, i1, jnp.where(lane == 1.0, i2, jnp.where(lane == 2.0, w1, jnp.where(
        lane == 3.0, w2, jnp.where(lane == 4.0, r1, jnp.where(lane == 5.0, r2, 0.0))))))
    info_ref[...] = info[:, 0:SUB]
    info_t = jnp.concatenate([info[r:r + LANES, :].T for r in range(0, info.shape[0], LANES)], axis=1)
    infot_ref[...] = info_t[0:SUB, :]


def _moe_route(xp, xs, g, mod, router_w, rows_per_group):
    Np, D = xp.shape
    N = Np + xs.shape[0]
    TM = MOE_SB
    ntp = Np // TM
    nt = N // TM
    rw = jnp.pad(router_w, ((0, 0), (0, LANES - router_w.shape[1])))
    tri = jnp.asarray(np.tril(np.ones((TM, TM), np.float32), -1), dtype=BF16)
    xp_spec, xs_spec = _two_stream_specs((TM, D), ntp)
    return pl.pallas_call(
        functools.partial(_route_kernel, ntp=ntp, cap=_moe_cap(N)),
        grid=(nt,),
        in_specs=[xp_spec, xs_spec,
                  pl.BlockSpec((1, D), lambda i: (0, 0)),
                  _pool_mod_spec(3, D, TM, ntp, rows_per_group),
                  _pool_mod_spec(4, D, TM, ntp, rows_per_group),
                  pl.BlockSpec((D, LANES), lambda i: (0, 0)),
                  pl.BlockSpec((TM, TM), lambda i: (0, 0))],
        out_specs=[pl.BlockSpec((TM, D), lambda i: (i, 0)),
                   pl.BlockSpec((TM, SUB), lambda i: (i, 0)),
                   pl.BlockSpec((SUB, TM), lambda i: (0, i)),
                   pl.BlockSpec((MOE_SB // MOE_SBG, 1, LANES), lambda i: (i, 0, 0))],
        out_shape=[jax.ShapeDtypeStruct((N, D), BF16),
                   jax.ShapeDtypeStruct((N, SUB), F32),
                   jax.ShapeDtypeStruct((SUB, N), F32),
                   jax.ShapeDtypeStruct((nt * (MOE_SB // MOE_SBG), 1, LANES), F32)],
        scratch_shapes=[pltpu.VMEM((1, LANES), F32)],
        compiler_params=_params("arbitrary"),
        name="moe_route",
    )(xp, xs, g.reshape(1, D), mod, mod, rw, tri)


def _hold_unused(idx, used):
    steps, slots = idx.shape
    read = jnp.arange(slots, dtype=jnp.int32)[None, :] < used[:, None]
    step = jnp.arange(steps, dtype=jnp.int32)[:, None]
    last = lax.cummax(jnp.where(read, step, -1), axis=0)
    held = jnp.take_along_axis(idx, jnp.maximum(last, 0), axis=0)
    return jnp.where(last >= 0, held, 0)


def _moe_cap(N):
    return -(-N // MOE_TR) * MOE_TR


def _moe_plan(cum, N):
    E, SB, TRG, TR = N_EXPERTS, MOE_SB, MOE_TRG, MOE_TR
    NB = N // SB
    cap = _moe_cap(N)
    tpe_g, tpe = cap // TRG, cap // TR
    RG = E * tpe_g
    used_g, used_t = 2 * N // TRG + E, 2 * N // TR + E
    PMAX = used_g + E * NB
    i32 = jnp.int32
    parts = SB // MOE_SBG
    cum_g = cum[:, 0, :E].astype(i32).T
    cum_e = cum_g[:, parts - 1::parts]
    cnt = cum_e[:, -1]

    lo = jnp.asarray((np.arange(RG) % tpe_g) * TRG, dtype=i32)
    hi = jnp.minimum(lo + TRG, jnp.repeat(cnt, tpe_g))

    GG = MOE_GG
    cum_g_t = jnp.repeat(cum_g, tpe_g, axis=0)
    first_g = jnp.sum(cum_g_t <= lo[:, None], axis=1)
    last_g = jnp.sum(cum_g_t < hi[:, None], axis=1)
    nblk_g = jnp.where(hi > lo, last_g - first_g + 1, 0)
    nst = (nblk_g + GG - 1) // GG
    gst_end = jnp.cumsum(nst)
    gtotal = gst_end[-1]
    smax_g = (used_g + E * NB * parts + (GG - 1) * used_g) // GG + 1
    jg = jnp.arange(smax_g, dtype=i32)
    g_ok = jg < gtotal
    jgc = jnp.minimum(jg, gtotal - 1)
    tq = jnp.minimum(jnp.sum(gst_end[None, :] <= jgc[:, None], axis=1), RG - 1).astype(i32)
    gg = jgc - (gst_end - nst)[tq]
    g_slots = jnp.where(g_ok, jnp.clip(nblk_g[tq] - GG * gg, 0, GG), 0)
    g_parts = _hold_unused((first_g[tq] + GG * gg)[:, None] + jnp.arange(GG, dtype=i32)[None, :], g_slots)
    g_first = jnp.logical_and(g_ok, gg == 0)
    gather_plan = (tq, g_parts.reshape(-1).astype(i32), g_slots.astype(i32), g_first.astype(i32))

    CG = MOE_CG
    c_lo = jnp.concatenate([jnp.zeros((E, 1), i32), cum_e[:, :-1]], axis=1)
    t_lo = c_lo // TRG
    n_se = jnp.where(cum_e > c_lo, (cum_e - 1) // TRG - t_lo + 1, 0)
    cn = jnp.cumsum(n_se, axis=0)
    per_blk = cn[-1]
    nsteps = (per_blk + CG - 1) // CG
    st_end = jnp.cumsum(nsteps)
    total = st_end[-1]
    SMAX = (PMAX + CG - 1) // CG + NB
    j = jnp.arange(SMAX, dtype=i32)
    step_ok = j < total
    jc = jnp.minimum(j, total - 1)
    blk = jnp.minimum(jnp.sum(st_end[None, :] <= jc[:, None], axis=1), NB - 1).astype(i32)
    grp = jc - (st_end - nsteps)[blk]
    m = CG * grp[:, None] + jnp.arange(CG, dtype=i32)[None, :]
    slot_ok = jnp.logical_and(m < per_blk[blk][:, None], step_ok[:, None])
    c_slots = jnp.sum(slot_ok, axis=1).astype(i32)
    cn_b = cn[:, blk].T
    e_of = jnp.minimum(jnp.sum(cn_b[:, None, :] <= m[:, :, None], axis=2), E - 1)
    before = jnp.take_along_axis(cn_b - n_se[:, blk].T, e_of, axis=1)
    tile = e_of * tpe_g + jnp.take_along_axis(t_lo[:, blk].T, e_of, axis=1) + m - before
    slot_q = _hold_unused(tile, c_slots)
    c_first = jnp.logical_and(step_ok, grp == 0).astype(i32)
    c_last = jnp.logical_and(step_ok, grp == nsteps[blk] - 1).astype(i32)
    combine_plan = (blk, slot_q.reshape(-1).astype(i32), c_slots, c_first, c_last)

    tiles = (cnt + TR - 1) // TR
    t_end = jnp.cumsum(tiles)
    jt = jnp.arange(used_t, dtype=i32)
    t_ok = jt < t_end[-1]
    jtc = jnp.minimum(jt, t_end[-1] - 1)
    te = jnp.minimum(jnp.sum(t_end[None, :] <= jtc[:, None], axis=1), E - 1).astype(i32)
    tk = jtc - (t_end - tiles)[te]
    tvalid = jnp.where(t_ok, jnp.clip(cnt[te] - tk * TR, 0, TR), 0)
    ffn_plan = ((te * tpe + tk).astype(i32), te, tvalid.astype(i32))
    return gather_plan, combine_plan, ffn_plan, E * cap


def _moe_gather_kernel(q_ref, s_ref, slots_ref, first_ref, *refs):
    pos_refs, h_refs, out_ref = refs[:MOE_GG], refs[MOE_GG:2 * MOE_GG], refs[2 * MOE_GG]
    p = pl.program_id(0)
    rows = out_ref.shape[0]

    @pl.when(first_ref[p] == 1)
    def _():
        out_ref[...] = jnp.zeros_like(out_ref)

    for ns in range(1, MOE_GG + 1):
        @pl.when(slots_ref[p] == ns)
        def _(ns=ns):
            row = (lax.broadcasted_iota(jnp.int32, (rows, 1), 0) + q_ref[p] * rows).astype(F32)
            sels = []
            for k in range(ns):
                hit = jnp.logical_or(pos_refs[k][4:5, :] == row, pos_refs[k][5:6, :] == row)
                sels.append(jnp.where(hit, 1.0, 0.0).astype(BF16))
            sel = sels[0] if ns == 1 else jnp.concatenate(sels, axis=1)
            hs = h_refs[0][...] if ns == 1 else jnp.concatenate([h_refs[k][...] for k in range(ns)], axis=0)
            out_ref[...] = out_ref[...] + _dot(sel, hs).astype(BF16)


def _moe_gather(h, pos_rows, plan, rmax):
    N, D = h.shape
    nsteps = plan[0].shape[0]

    def pos_spec(k):
        return pl.BlockSpec((SUB, MOE_SBG), lambda p, q, s, *_: (0, s[MOE_GG * p + k]))

    def tok_spec(k):
        return pl.BlockSpec((MOE_SBG, D), lambda p, q, s, *_: (s[MOE_GG * p + k], 0))

    return pl.pallas_call(
        _moe_gather_kernel,
        grid_spec=pltpu.PrefetchScalarGridSpec(
            num_scalar_prefetch=4, grid=(nsteps,),
            in_specs=[pos_spec(k) for k in range(MOE_GG)] + [tok_spec(k) for k in range(MOE_GG)],
            out_specs=pl.BlockSpec((MOE_TRG, D), lambda p, q, *_: (q[p], 0))),
        out_shape=jax.ShapeDtypeStruct((rmax, D), BF16),
        compiler_params=_params("arbitrary"),
        name="moe_gather",
    )(*plan, *([pos_rows] * MOE_GG), *([h] * MOE_GG))


def _moe_ffn_kernel(t_ref, e_ref, nv_ref, x_ref, w1_ref, w3_ref, w2_ref, out_ref, acc_scr, *, nf):
    t = pl.program_id(0)
    f = pl.program_id(1)
    nv = nv_ref[t]

    def block(start, size):
        rows = pl.ds(start, size)

        @pl.when(f == 0)
        def _():
            acc_scr[rows, :] = jnp.zeros((size, acc_scr.shape[1]), F32)

        x = x_ref[rows, :]
        a = _dot(x, w1_ref[...].astype(BF16))
        b = _dot(x, w3_ref[...].astype(BF16))
        acc_scr[rows, :] += _dot((_silu(a) * b).astype(BF16), w2_ref[...].astype(BF16))

        @pl.when(f == nf - 1)
        def _():
            out_ref[rows, :] = acc_scr[rows, :].astype(out_ref.dtype)

    nsub = MOE_TR // MOE_TRG
    used = (nv + MOE_TRG - 1) // MOE_TRG

    @pl.when(used == nsub)
    def _():
        block(0, MOE_TR)

    @pl.when(jnp.logical_and(used > 0, used < nsub))
    def _():
        start = jnp.int32(0)
        size = MOE_TR // 2
        while size >= MOE_TRG:
            has = (used & (size // MOE_TRG)) != 0

            @pl.when(has)
            def _(start=start, size=size):
                block(pl.multiple_of(start, MOE_TRG), size)

            start = start + jnp.where(has, size, 0)
            size //= 2


def _moe_ffn(xs, plan, w1, w3, w2):
    rmax, D = xs.shape
    FF = w1.shape[2]
    TF = 256
    nf = FF // TF
    RT = plan[0].shape[0]

    def fidx(t, f, nv):
        return jnp.where(nv[t] > 0, f, nf - 1)

    return pl.pallas_call(
        functools.partial(_moe_ffn_kernel, nf=nf),
        grid_spec=pltpu.PrefetchScalarGridSpec(
            num_scalar_prefetch=3, grid=(RT, nf),
            in_specs=[pl.BlockSpec((MOE_TR, D), lambda t, f, ti, e, nv: (ti[t], 0)),
                      pl.BlockSpec((None, D, TF), lambda t, f, ti, e, nv: (e[t], 0, fidx(t, f, nv))),
                      pl.BlockSpec((None, D, TF), lambda t, f, ti, e, nv: (e[t], 0, fidx(t, f, nv))),
                      pl.BlockSpec((None, TF, D), lambda t, f, ti, e, nv: (e[t], fidx(t, f, nv), 0))],
            out_specs=pl.BlockSpec((MOE_TR, D), lambda t, f, ti, e, nv: (ti[t], 0)),
            scratch_shapes=[pltpu.VMEM((MOE_TR, D), F32)]),
        out_shape=jax.ShapeDtypeStruct((rmax, D), BF16),
        compiler_params=_params("arbitrary", "arbitrary"),
        name="moe_ffn",
    )(*plan, xs, w1, w3, w2)


def _moe_combine_kernel(s_ref, q_ref, slots_ref, first_ref, last_ref, pos_ref, *refs, ntp):
    ys_refs = refs[:MOE_CG]
    xp_ref, xs_ref, gate_ref, fg_ref, op_ref, os_ref, acc_scr = refs[MOE_CG:]
    p = pl.program_id(0)
    rows = ys_refs[0].shape[0]

    @pl.when(first_ref[p] == 1)
    def _():
        acc_scr[...] = jnp.zeros_like(acc_scr)

    for ns in range(1, MOE_CG + 1):
        @pl.when(slots_ref[p] == ns)
        def _(ns=ns):
            sels = []
            for k in range(ns):
                col = (lax.broadcasted_iota(jnp.int32, (1, rows), 1) + q_ref[MOE_CG * p + k] * rows).astype(F32)
                sels.append((jnp.where(pos_ref[:, 4:5] == col, pos_ref[:, 2:3], 0.0)
                             + jnp.where(pos_ref[:, 5:6] == col, pos_ref[:, 3:4], 0.0)).astype(BF16))
            sel = sels[0] if ns == 1 else jnp.concatenate(sels, axis=1)
            ys = ys_refs[0][...] if ns == 1 else jnp.concatenate([ys_refs[k][...] for k in range(ns)], axis=0)
            acc_scr[...] += _dot(sel, ys)

    @pl.when(last_ref[p] == 1)
    def _():
        s = s_ref[p]
        x = jnp.where(s < ntp, xp_ref[...], xs_ref[...])
        y = x + gate_ref[...] * acc_scr[...]
        out = y * lax.rsqrt(jnp.mean(y * y, axis=-1, keepdims=True) + EPS) * fg_ref[...]

        @pl.when(s < ntp)
        def _():
            op_ref[...] = out

        @pl.when(s >= ntp)
        def _():
            os_ref[...] = out


def _moe_combine(ys, pos_cols, plan, xp, xs, mod, final_g, rows_per_group):
    Np, D = xp.shape
    Ns = xs.shape[0]
    SB = MOE_SB
    ntp = Np // SB
    nsteps = plan[0].shape[0]

    def tile_spec(k):
        return pl.BlockSpec((MOE_TRG, D), lambda p, s, q, *_: (q[MOE_CG * p + k], 0))

    def tok_p(p, s, *_):
        return (jnp.minimum(s[p], ntp - 1), 0)

    def tok_s(p, s, *_):
        return (jnp.maximum(s[p] - ntp, 0), 0)

    def gate_idx(p, s, *_):
        return (jnp.where(s[p] < ntp, 0, 1 + ((s[p] - ntp) * SB) // rows_per_group), 5, 0, 0)

    return pl.pallas_call(
        functools.partial(_moe_combine_kernel, ntp=ntp),
        grid_spec=pltpu.PrefetchScalarGridSpec(
            num_scalar_prefetch=5, grid=(nsteps,),
            in_specs=[pl.BlockSpec((SB, SUB), lambda p, s, q, *_: (s[p], 0))]
            + [tile_spec(k) for k in range(MOE_CG)]
            + [pl.BlockSpec((SB, D), tok_p),
                      pl.BlockSpec((SB, D), tok_s),
                      pl.BlockSpec((None, None, 1, D), gate_idx),
                      pl.BlockSpec((1, D), lambda p, *_: (0, 0))],
            out_specs=[pl.BlockSpec((SB, D), tok_p), pl.BlockSpec((SB, D), tok_s)],
            scratch_shapes=[pltpu.VMEM((SB, D), F32)]),
        out_shape=[jax.ShapeDtypeStruct((Np, D), F32), jax.ShapeDtypeStruct((Ns, D), F32)],
        compiler_params=_params("arbitrary"),
        name="moe_combine",
    )(*plan, pos_cols, *([ys] * MOE_CG), xp, xs, mod, final_g.reshape(1, D))


def _moe(xp, xs, g, mod, router_w, w1, w3, w2, final_g, rows_per_group):
    N = xp.shape[0] + xs.shape[0]
    h, info, info_t, cum = _moe_route(xp, xs, g, mod, router_w, rows_per_group)
    gather_plan, combine_plan, ffn_plan, rmax = _moe_plan(cum, N)
    x_sorted = _moe_gather(h, info_t, gather_plan, rmax)
    y_sorted = _moe_ffn(x_sorted, ffn_plan, w1, w3, w2)
    return _moe_combine(y_sorted, info, combine_plan, xp, xs, mod, final_g, rows_per_group)


def _gla_levels(C):
    lv, c = [], C // 2
    while c >= SUB:
        lv.append(c)
        c //= 2
    return lv


def _gla_tables(C):
    levels = _gla_levels(C)
    nr = 1 + len(levels)
    mat = np.zeros((2, nr * C, C), np.float32)
    code = np.zeros((2, C, C), np.int32)
    for d in range(2):
        p = np.arange(C) if d == 0 else C - 1 - np.arange(C)
        pi, pj = p[:, None], p[None, :]
        mat[d, 0:C] = pj <= pi
        code[d] = np.where((pj <= pi) & (pi // SUB == pj // SUB), 1, 0)
        for lv, c in enumerate(levels):
            blk = pi // c
            later = blk % 2 == 1
            mat[d, (1 + lv) * C:(2 + lv) * C] = ((later & (pj > blk * c - 1) & (pj <= pi))
                                                 | (~later & (pj > pi) & (pj <= (blk + 1) * c - 1)))
            pair = (pi // (2 * c) == pj // (2 * c)) & (pi // c != pj // c) & (pj <= pi)
            code[d] = np.where(pair, 2 + lv, code[d])
    ones = np.zeros((SUB * LANES, C), np.float32)
    for jj in range(SUB):
        ones[jj * LANES:(jj + 1) * LANES, jj::SUB] = 1.0
    return jnp.asarray(mat, dtype=BF16), jnp.asarray(code), jnp.asarray(ones, dtype=BF16)


def _bcast_sublane(x, jj):
    r, w = x.shape
    x3 = x.reshape(r // SUB, SUB, w)
    return jnp.broadcast_to(x3[:, jj:jj + 1, :], x3.shape).reshape(r, w)


def _t128(x):
    r, w = x.shape
    if w > LANES:
        return jnp.concatenate([x[:, i:i + LANES].T for i in range(0, w, LANES)], axis=0)
    return jnp.concatenate([x[i:i + LANES, :].T for i in range(0, r, LANES)], axis=1)


def _gla_kernel(q_ref, k_ref, v_ref, g_ref, lr_ref, wg_ref, ba_ref, mat_ref, code_ref, ones_ref,
                s0f_ref, s0b_ref, ng_ref, o_ref, sf_ref, sb_ref, st_scr, of_scr, *, n, C, G):
    d = pl.program_id(1)
    c = pl.program_id(2)
    levels = _gla_levels(C)

    @pl.when(jnp.logical_and(c == 0, d == 0))
    def _():
        for bb in range(G):
            for h in range(C_H):
                st_scr[bb, h] = _t128(s0f_ref[bb, h])

    @pl.when(jnp.logical_and(c == 0, d == 1))
    def _():
        for bb in range(G):
            for h in range(C_H):
                st_scr[bb, h] = _t128(s0b_ref[bb, h])

    mat = mat_ref[...]
    code = code_ref[...]
    ones = ones_ref[...]
    wg = _split(wg_ref[...])
    cums = []
    for bb in range(G):
        xg = _dot3(_split(lr_ref[bb]), wg) + ba_ref[...]
        la = (jnp.minimum(xg, 0.0) - jnp.log1p(jnp.exp(-jnp.abs(xg)))) * (LOG2E / C_TAU)
        hi, lo = _split(la)
        cums.append(_dot(mat, hi) + _dot(mat, lo))

    def prepare(bb, h):
        cum = cums[bb]
        ks = slice(h * C_DK, (h + 1) * C_DK)
        qh = q_ref[bb, :, ks].astype(F32) * (C_DK ** -0.5)
        kh = k_ref[bb, :, ks].astype(F32)
        b = cum[0:C, ks]
        b_end = jnp.min(b, axis=0, keepdims=True)
        b_rest = b_end - b
        ps = []
        for jj in range(SUB):
            dec = jnp.exp2(jnp.minimum(b - _bcast_sublane(b, jj), 0.0))
            ps.append((qh * _bcast_sublane(kh, jj) * dec).astype(BF16))
        lv_ops = []
        for lv in range(len(levels)):
            fac = jnp.exp2(cum[(1 + lv) * C:(2 + lv) * C, ks])
            lv_ops.append(((qh * fac).astype(BF16), (kh * fac).astype(BF16)))
        qe = (qh * jnp.exp2(b)).astype(BF16)
        ke = (kh * jnp.exp2(b_rest)).astype(BF16)
        e_end = jnp.exp2(b_end)
        return jnp.concatenate(ps, axis=1), lv_ops, qe, ke, e_end

    def contract(bb, h, prep):
        pcat, lv_ops, qe, ke, e_end = prep
        vh = v_ref[bb, :, h * C_DV:(h + 1) * C_DV].astype(F32)
        att = jnp.where(code == 1, _dot(pcat, ones), 0.0)
        for lv, (qs, ks_) in enumerate(lv_ops):
            att = jnp.where(code == 2 + lv, _dot_nt(qs, ks_), att)
        st = st_scr[bb, h]
        o = _dot(att.astype(BF16), vh.astype(BF16)) + _dot_nt(qe, st.astype(BF16))
        st_scr[bb, h] = e_end * st + _dot(_t128(vh).astype(BF16), ke)
        return o

    units = [(bb, h) for h in range(C_H) for bb in range(G)]
    outs = {}
    prep = prepare(*units[0])
    for idx, (bb, h) in enumerate(units):
        nxt = prepare(*units[idx + 1]) if idx + 1 < len(units) else None
        outs[(bb, h)] = contract(bb, h, prep)
        prep = nxt
    o_all = [jnp.concatenate([outs[(bb, h)] for h in range(C_H)], axis=-1) for bb in range(G)]

    @pl.when(d == 0)
    def _():
        for bb in range(G):
            of_scr[bb, c] = o_all[bb]

    @pl.when(d == 1)
    def _():
        for bb in range(G):
            tot = o_all[bb] + of_scr[bb, n - 1 - c]
            res = []
            for h in range(C_H):
                sl = slice(h * C_DV, (h + 1) * C_DV)
                t = tot[:, sl]
                y = t * lax.rsqrt(jnp.mean(t * t, axis=-1, keepdims=True) + EPS) * ng_ref[:, sl]
                res.append(y * _silu(g_ref[bb, :, sl].astype(F32)))
            o_ref[bb] = jnp.concatenate(res, axis=-1).astype(o_ref.dtype)

    @pl.when(jnp.logical_and(c == n - 1, d == 0))
    def _():
        for bb in range(G):
            for h in range(C_H):
                sf_ref[bb, h] = _t128(st_scr[bb, h])

    @pl.when(jnp.logical_and(c == n - 1, d == 1))
    def _():
        for bb in range(G):
            for h in range(C_H):
                sb_ref[bb, h] = _t128(st_scr[bb, h])


def _gla(z, zg, B, T, w_a2, b_a, s0f, s0b, norm_g):
    C = GLA_CHUNK
    G = _rows_per_step(B, T, C_H * C_DV)
    assert B % G == 0 and T % C == 0
    n = T // C
    HK = C_H * C_DK
    HV = C_H * C_DV
    mat, code, ones = _gla_tables(C)
    nr = mat.shape[1] // C
    wg = jnp.zeros((2, LANES, HK), F32)
    for dr in range(2):
        wg = wg.at[dr, dr * C_RANK:(dr + 1) * C_RANK, :].set(w_a2[dr])
    z3 = z.reshape(B, T, z.shape[1])
    zg3 = zg.reshape(B, T, zg.shape[1])

    def chunk(d, c):
        return c + d * (n - 1 - 2 * c)

    st_spec = pl.BlockSpec((G, C_H, C_DK, C_DV), lambda b, d, c: (b, 0, 0, 0))
    st_shape = jax.ShapeDtypeStruct((B, C_H, C_DK, C_DV), F32)
    o, sf, sb = pl.pallas_call(
        functools.partial(_gla_kernel, n=n, C=C, G=G),
        grid=(B // G, 2, n),
        in_specs=[pl.BlockSpec((G, C, HK), lambda b, d, c: (b, chunk(d, c), 0)),
                  pl.BlockSpec((G, C, HK), lambda b, d, c: (b, chunk(d, c), 1)),
                  pl.BlockSpec((G, C, HV), lambda b, d, c: (b, chunk(d, c), 1)),
                  pl.BlockSpec((G, C, HV), lambda b, d, c: (b, chunk(d, c), 2)),
                  pl.BlockSpec((G, C, LANES), lambda b, d, c: (b, chunk(d, c), 0)),
                  pl.BlockSpec((None, LANES, HK), lambda b, d, c: (d, 0, 0)),
                  pl.BlockSpec((None, 1, HK), lambda b, d, c: (d, 0, 0)),
                  pl.BlockSpec((None, nr * C, C), lambda b, d, c: (d, 0, 0)),
                  pl.BlockSpec((None, C, C), lambda b, d, c: (d, 0, 0)),
                  pl.BlockSpec((SUB * LANES, C), lambda b, d, c: (0, 0)),
                  st_spec, st_spec,
                  pl.BlockSpec((1, HV), lambda b, d, c: (0, 0))],
        out_specs=[pl.BlockSpec((G, C, HV), lambda b, d, c: (b, (n - 1) - d * c, 0)),
                   st_spec, st_spec],
        out_shape=[jax.ShapeDtypeStruct((B, T, HV), BF16), st_shape, st_shape],
        scratch_shapes=[pltpu.VMEM((G, C_H, C_DV, C_DK), F32), pltpu.VMEM((G, n, C, HV), F32)],
        compiler_params=_params("arbitrary", "arbitrary", "arbitrary"),
        name="gla",
    )(z3, z3, z3, z3, zg3, wg, b_a.reshape(2, 1, HK), mat, code, ones, s0f, s0b, norm_g.reshape(1, HV))
    return o.reshape(B * T, HV), sf, sb


def _run_stream(x, B, T, mods, ctx, p):
    N, D = x.shape
    rpg = N // mods[0].shape[0]
    TM = min(2048, rpg)
    nb = (B_H + 2 * B_HKV) * B_HD

    w_in = p['even_w_in'][0]
    z, zb = _norm_mm(x, p['norm1_g'][0], mods[0], (0, 1), w_in, w_in, (nb, MIX_MAIN // nb), TM, rpg)
    if ctx is None:
        s0 = jnp.zeros((B, A_H, A_DK, A_DV), F32)
        a_f0, a_b0, cache_k, cache_v = s0, s0, None, None
    else:
        cache_k, cache_v, a_f0, a_b0 = ctx[0], ctx[1], ctx[2], ctx[3]
    o_a, a_sf, a_sb = _retention(z, B, T, p['a_log_gamma'][0], a_f0, a_b0, p['a_norm_g'][0])
    qpad, k_norm, k_rot, v_bf = _bprep(zb, T, p['b_q_g'][0], p['b_k_g'][0], rope=ctx is not None)
    o_b = _attention(qpad, k_rot, v_bf, B, T, cache_k, cache_v)
    x = _proj_res(x, mods[0], 2, [o_a, o_b], p['even_w_out'][0], rpg)
    x = _ffn(x, p['norm2_g'][0], mods[0], p['ff_w1'][0], p['ff_w3'][0], p['ff_w2'][0], rpg)

    w_in = p['odd_w_in'][0]
    w_gate = jnp.pad(w_in[:, MIX_MAIN:], ((0, 0), (0, LANES - 2 * C_RANK)))
    z1, z1g = _norm_mm(x, p['norm1_g'][1], mods[1], (0, 1), w_in, w_gate, (LANES, 0), TM, rpg)
    if ctx is None:
        s0 = jnp.zeros((B, C_H, C_DK, C_DV), F32)
        c_f0, c_b0 = s0, s0
    else:
        c_f0, c_b0 = ctx[4], ctx[5]
    o_c, c_sf, c_sb = _gla(z1, z1g, B, T, p['c_w_a2'][0], p['c_b_a'][0], c_f0, c_b0, p['c_norm_g'][0])
    x = _proj_res(x, mods[1], 2, [o_c], p['odd_w_out'][0], rpg)
    v_raw = zb[:, (B_H + B_HKV) * B_HD:]
    return x, (k_norm, v_raw, a_sf, a_sb, c_sf, c_sb)


def kernel(x_prompt, x_sample, c, cache_b_k, cache_b_v, state_a_fwd, state_a_bwd, state_c_fwd, state_c_bwd,
           c_ctx, w_mod, b_mod, norm1_g, norm2_g, final_g, even_w_in, even_w_out, a_log_gamma, a_norm_g,
           b_q_g, b_k_g, odd_w_in, c_w_a2, c_b_a, c_norm_g, odd_w_out, ff_w1, ff_w3, ff_w2,
           router_w, moe_w1, moe_w3, moe_w2):
    Bp, Tp, D = x_prompt.shape
    Bs, Ts, _ = x_sample.shape
    L = w_mod.shape[0]
    assert L == 2 and even_w_in.shape[0] == 1 and odd_w_in.shape[0] == 1
    p = dict(norm1_g=norm1_g, norm2_g=norm2_g, final_g=final_g, even_w_in=even_w_in, even_w_out=even_w_out,
             a_log_gamma=a_log_gamma, a_norm_g=a_norm_g, b_q_g=b_q_g, b_k_g=b_k_g, odd_w_in=odd_w_in,
             c_w_a2=c_w_a2, c_b_a=c_b_a, c_norm_g=c_norm_g, odd_w_out=odd_w_out, ff_w1=ff_w1, ff_w3=ff_w3,
             ff_w2=ff_w2, router_w=router_w, moe_w1=moe_w1, moe_w3=moe_w3, moe_w2=moe_w2)

    rows = 8
    conds = jnp.concatenate([c_ctx[None, :], c, jnp.zeros((rows - 1 - Bs, D), F32)], axis=0)
    mod = _modulation(conds, w_mod, b_mod).reshape(L, rows, 6, 1, D)
    mods_p = [mod[l, 0:1] for l in range(L)]
    mods_s = [mod[l, 1:1 + Bs] for l in range(L)]

    x_p, kept = _run_stream(x_prompt.reshape(Bp * Tp, D), Bp, Tp, mods_p, None, p)
    nk = B_HKV * B_HD
    ctx = (cache_b_k[:, 0].reshape(Bs, -1, nk), cache_b_v[:, 0].reshape(Bs, -1, nk),
           state_a_fwd[:, 0], state_a_bwd[:, 0], state_c_fwd[:, 0], state_c_bwd[:, 0])
    x_s, _ = _run_stream(x_sample.reshape(Bs * Ts, D), Bs, Ts, mods_s, ctx, p)
    y_p, y_s = _moe(x_p, x_s, norm2_g[1], mod[1, 0:1 + Bs], router_w[0], moe_w1[0], moe_w3[0], moe_w2[0],
                    final_g, Ts)

    k_norm, v_raw, a_sf, a_sb, c_sf, c_sb = kept
    return (y_p.reshape(Bp, Tp, D), y_s.reshape(Bs, Ts, D),
            k_norm.reshape(Bp, 1, Tp, B_HKV, B_HD), v_raw.reshape(Bp, 1, Tp, B_HKV, B_HD),
            a_sf[:, None], a_sb[:, None], c_sf[:, None], c_sb[:, None])
```
